```python
import math
import jax
import jax.numpy as jnp
from jax import lax
import numpy as np

D_MODEL = 1024
BATCH = 8
SEQ = 8192
DEPTH = 1

MEM_LEN = 256
LRU_WIDTH = D_MODEL // 2
LRU_HEADS = 8
LRU_HEAD_DIM = LRU_WIDTH // LRU_HEADS
CONV_WIDTH = 4
LRU_C = 8.0
SB_WIDTH = D_MODEL - LRU_WIDTH
SB_HEADS = 8
SB_HEAD_DIM = SB_WIDTH // SB_HEADS
MIX_WIDTH = LRU_WIDTH + SB_WIDTH
IN_WIDTH = 2 * LRU_WIDTH + 3 * SB_WIDTH
Q_BLOCK = 128
D_FF = 128 * round(8 * D_MODEL / 3 / 128)
MEM_HEADS = 4
MEM_HEAD_DIM = D_MODEL // MEM_HEADS
ALPHA = (2 * DEPTH) ** 0.25
BETA = (8 * DEPTH) ** -0.25
LN_EPS = 1e-5
RMS_EPS = 1e-6

kernel_name = "hybrid_rglru_stickbreaking_deepnorm_layer"


def _layer_norm(x, g, b):
    xf = x.astype(jnp.float32)
    mu = jnp.mean(xf, axis=-1, keepdims=True)
    var = jnp.mean(jnp.square(xf - mu), axis=-1, keepdims=True)
    return ((xf - mu) * lax.rsqrt(var + LN_EPS) * g + b).astype(x.dtype)


def _rms_norm(x, g):
    xf = x.astype(jnp.float32)
    return xf * lax.rsqrt(jnp.mean(jnp.square(xf), axis=-1, keepdims=True) + RMS_EPS) * g


def _swiglu(h, w13, w2):
    gate, up = jnp.split(h @ w13, 2, axis=-1)
    return (jax.nn.silu(gate) * up) @ w2


def _causal_depthwise_conv(x, w, b):
    s = x.shape[1]
    xp = jnp.pad(x, ((0, 0), (CONV_WIDTH - 1, 0), (0, 0)))
    out = b
    for k in range(CONV_WIDTH):
        out = out + xp[:, k:k + s] * w[k]
    return out


def _lru_combine(left, right):
    a_l, b_l = left
    a_r, b_r = right
    return a_l * a_r, a_r * b_l + b_r


def _rglru(xb, conv_w, conv_b, w_rg, b_rg, w_ig, b_ig, lam):
    bsz, s, _ = xb.shape
    xc = _causal_depthwise_conv(xb.astype(jnp.float32), conv_w, conv_b)
    xh = xc.reshape(bsz, s, LRU_HEADS, LRU_HEAD_DIM)
    r = jax.nn.sigmoid(jnp.einsum('bshi,hij->bshj', xh, w_rg).reshape(bsz, s, LRU_WIDTH) + b_rg)
    i = jax.nn.sigmoid(jnp.einsum('bshi,hij->bshj', xh, w_ig).reshape(bsz, s, LRU_WIDTH) + b_ig)
    log_a = -LRU_C * r * jax.nn.softplus(-lam)
    a = jnp.exp(log_a)
    u = jnp.sqrt(-jnp.expm1(2.0 * log_a)) * (i * xc)
    _, h = lax.associative_scan(_lru_combine, (a, u), axis=1)
    return h


def _stick_breaking_block(qb, kp, vp, q0):
    z = jnp.einsum('bhqd,bhkd->bhqk', qb, kp) / math.sqrt(SB_HEAD_DIM)
    t_idx = q0 + jnp.arange(qb.shape[2])[:, None]
    s_idx = jnp.arange(kp.shape[2])[None, :]
    causal = s_idx < t_idx
    log_not = jnp.where(causal, jax.nn.log_sigmoid(-z), 0.0)
    suffix = lax.cumsum(log_not, axis=3, reverse=True) - log_not
    weight = jnp.where(causal, jnp.exp(jax.nn.log_sigmoid(z) + suffix), 0.0)
    return jnp.einsum('bhqk,bhkd->bhqd', weight, vp)


def _stick_breaking_attention(q, k, v):
    bsz, s, _ = q.shape
    def heads(t):
        return t.astype(jnp.float32).reshape(bsz, s, SB_HEADS, SB_HEAD_DIM).transpose(0, 2, 1, 3)
    q, k, v = heads(q), heads(k), heads(v)
    outs = []
    for blk in range(s // Q_BLOCK):
        q0 = blk * Q_BLOCK
        end = q0 + Q_BLOCK
        outs.append(_stick_breaking_block(q[:, :, q0:end], k[:, :, :end], v[:, :, :end], q0))
    o = jnp.concatenate(outs, axis=2)
    return o.transpose(0, 2, 1, 3).reshape(bsz, s, SB_WIDTH)


def _hybrid_mixer(h, w_in, conv_w, conv_b, w_rg, b_rg, w_ig, b_ig, lam, g_lru, g_sb, w_out):
    proj = h @ w_in
    splits = [LRU_WIDTH, 2 * LRU_WIDTH, 2 * LRU_WIDTH + SB_WIDTH, 2 * LRU_WIDTH + 2 * SB_WIDTH]
    xb, gate, q, k, v = jnp.split(proj, splits, axis=-1)
    y_lru = _rglru(xb, conv_w, conv_b, w_rg, b_rg, w_ig, b_ig, lam) * jax.nn.gelu(gate.astype(jnp.float32))
    y_sb = _stick_breaking_attention(q, k, v)
    y = jnp.concatenate([_rms_norm(y_lru, g_lru), _rms_norm(y_sb, g_sb)], axis=-1)
    return y @ w_out


def _memory_cross_attention(h, mem, wq, wkv, wo):
    bsz, s, _ = h.shape
    q = (h @ wq).reshape(bsz, s, MEM_HEADS, MEM_HEAD_DIM).astype(jnp.float32)
    km, vm = jnp.split(mem @ wkv, 2, axis=-1)
    km = km.reshape(bsz, -1, MEM_HEADS, MEM_HEAD_DIM).astype(jnp.float32)
    vm = vm.reshape(bsz, -1, MEM_HEADS, MEM_HEAD_DIM).astype(jnp.float32)
    scores = jnp.einsum('bshd,bmhd->bhsm', q, km) / math.sqrt(MEM_HEAD_DIM)
    p = jax.nn.softmax(scores, axis=-1)
    o = jnp.einsum('bhsm,bmhd->bshd', p, vm).reshape(bsz, s, D_MODEL)
    return o @ wo


def _fwd_setup_inputs(seed: int = 0) -> dict:
    key = jax.random.key(seed)
    ks = jax.random.split(key, 32)
    f32 = jnp.float32
    L = DEPTH

    def nrm(k, shape, scale):
        return jax.random.normal(k, shape, f32) * scale

    x = nrm(ks[0], (BATCH, SEQ, D_MODEL), 1.0)
    mem = nrm(ks[1], (BATCH, MEM_LEN, D_MODEL), 1.0)
    ffn1_w13 = nrm(ks[2], (L, D_MODEL, 2 * D_FF), BETA * D_MODEL ** -0.5)
    ffn1_w2 = nrm(ks[3], (L, D_FF, D_MODEL), BETA * D_FF ** -0.5)
    ln1_g = 1.0 + nrm(ks[4], (L, D_MODEL), 0.02)
    ln1_b = nrm(ks[5], (L, D_MODEL), 0.02)
    in_scale = jnp.concatenate([jnp.ones((IN_WIDTH - SB_WIDTH,), f32), jnp.full((SB_WIDTH,), BETA, f32)])
    w_in = nrm(ks[6], (L, D_MODEL, IN_WIDTH), D_MODEL ** -0.5) * in_scale
    conv_w = nrm(ks[7], (L, CONV_WIDTH, LRU_WIDTH), CONV_WIDTH ** -0.5)
    conv_b = nrm(ks[8], (L, LRU_WIDTH), 0.01)
    w_rgate = nrm(ks[9], (L, LRU_HEADS, LRU_HEAD_DIM, LRU_HEAD_DIM), LRU_HEAD_DIM ** -0.5)
    b_rgate = nrm(ks[10], (L, LRU_WIDTH), 0.01)
    w_igate = nrm(ks[11], (L, LRU_HEADS, LRU_HEAD_DIM, LRU_HEAD_DIM), LRU_HEAD_DIM ** -0.5)
    b_igate = nrm(ks[12], (L, LRU_WIDTH), 0.01)
    a_pow_c = jax.random.uniform(ks[13], (L, LRU_WIDTH), f32, 0.9, 0.999)
    log_a0 = jnp.log(a_pow_c) / LRU_C
    lru_lambda = log_a0 - jnp.log(-jnp.expm1(log_a0))
    g_lru = 1.0 + nrm(ks[14], (L, LRU_WIDTH), 0.02)
    g_sb = 1.0 + nrm(ks[15], (L, SB_WIDTH), 0.02)
    w_out = nrm(ks[16], (L, MIX_WIDTH, D_MODEL), BETA * MIX_WIDTH ** -0.5)
    ln2_g = 1.0 + nrm(ks[17], (L, D_MODEL), 0.02)
    ln2_b = nrm(ks[18], (L, D_MODEL), 0.02)
    mem_wq = nrm(ks[19], (L, D_MODEL, D_MODEL), D_MODEL ** -0.5)
    kv_scale = jnp.concatenate([jnp.ones((D_MODEL,), f32), jnp.full((D_MODEL,), BETA, f32)])
    mem_wkv = nrm(ks[20], (L, D_MODEL, 2 * D_MODEL), D_MODEL ** -0.5) * kv_scale
    mem_wo = nrm(ks[21], (L, D_MODEL, D_MODEL), BETA * D_MODEL ** -0.5)
    ln3_g = 1.0 + nrm(ks[22], (L, D_MODEL), 0.02)
    ln3_b = nrm(ks[23], (L, D_MODEL), 0.02)
    ffn2_w13 = nrm(ks[24], (L, D_MODEL, 2 * D_FF), BETA * D_MODEL ** -0.5)
    ffn2_w2 = nrm(ks[25], (L, D_FF, D_MODEL), BETA * D_FF ** -0.5)
    ln4_g = 1.0 + nrm(ks[26], (L, D_MODEL), 0.02)
    ln4_b = nrm(ks[27], (L, D_MODEL), 0.02)
    return {
        "x": x, "mem": mem,
        "ffn1_w13": ffn1_w13, "ffn1_w2": ffn1_w2, "ln1_g": ln1_g, "ln1_b": ln1_b,
        "w_in": w_in, "conv_w": conv_w, "conv_b": conv_b,
        "w_rgate": w_rgate, "b_rgate": b_rgate, "w_igate": w_igate, "b_igate": b_igate,
        "lru_lambda": lru_lambda, "g_lru": g_lru, "g_sb": g_sb, "w_out": w_out,
        "ln2_g": ln2_g, "ln2_b": ln2_b,
        "mem_wq": mem_wq, "mem_wkv": mem_wkv, "mem_wo": mem_wo, "ln3_g": ln3_g, "ln3_b": ln3_b,
        "ffn2_w13": ffn2_w13, "ffn2_w2": ffn2_w2, "ln4_g": ln4_g, "ln4_b": ln4_b,
    }


def _fwd_reference(x, mem, ffn1_w13, ffn1_w2, ln1_g, ln1_b, w_in, conv_w, conv_b,
              w_rgate, b_rgate, w_igate, b_igate, lru_lambda, g_lru, g_sb, w_out,
              ln2_g, ln2_b, mem_wq, mem_wkv, mem_wo, ln3_g, ln3_b,
              ffn2_w13, ffn2_w2, ln4_g, ln4_b):
    h = x
    for l in range(DEPTH):
        h = _layer_norm(ALPHA * h + 0.5 * _swiglu(h, ffn1_w13[l], ffn1_w2[l]), ln1_g[l], ln1_b[l])
        mix = _hybrid_mixer(h, w_in[l], conv_w[l], conv_b[l], w_rgate[l], b_rgate[l],
                            w_igate[l], b_igate[l], lru_lambda[l], g_lru[l], g_sb[l], w_out[l])
        h = _layer_norm(ALPHA * h + mix, ln2_g[l], ln2_b[l])
        cross = _memory_cross_attention(h, mem, mem_wq[l], mem_wkv[l], mem_wo[l])
        h = _layer_norm(ALPHA * h + cross, ln3_g[l], ln3_b[l])
        h = _layer_norm(ALPHA * h + 0.5 * _swiglu(h, ffn2_w13[l], ffn2_w2[l]), ln4_g[l], ln4_b[l])
    return h.astype(x.dtype)


import jax as _jax
import jax.numpy as _jnp

TWIN_FORMAT = 'train_step'
FWD_PARAMS = ['x', 'mem', 'ffn1_w13', 'ffn1_w2', 'ln1_g', 'ln1_b', 'w_in', 'conv_w', 'conv_b', 'w_rgate', 'b_rgate', 'w_igate', 'b_igate', 'lru_lambda', 'g_lru', 'g_sb', 'w_out', 'ln2_g', 'ln2_b', 'mem_wq', 'mem_wkv', 'mem_wo', 'ln3_g', 'ln3_b', 'ffn2_w13', 'ffn2_w2', 'ln4_g', 'ln4_b']
TWIN_WEIGHTS = ['ffn1_w13', 'ffn1_w2', 'ln1_g', 'ln1_b', 'w_in', 'conv_w', 'conv_b', 'w_rgate', 'b_rgate', 'w_igate', 'b_igate', 'lru_lambda', 'g_lru', 'g_sb', 'w_out', 'ln2_g', 'ln2_b', 'mem_wq', 'mem_wkv', 'mem_wo', 'ln3_g', 'ln3_b', 'ffn2_w13', 'ffn2_w2', 'ln4_g', 'ln4_b']
TWIN_DIFF_INPUT = 'x'
TWIN_INPUTS = ['x', 'mem', 'ffn1_w13', 'ffn1_w2', 'ln1_g', 'ln1_b', 'w_in', 'conv_w', 'conv_b', 'w_rgate', 'b_rgate', 'w_igate', 'b_igate', 'lru_lambda', 'g_lru', 'g_sb', 'w_out', 'ln2_g', 'ln2_b', 'mem_wq', 'mem_wkv', 'mem_wo', 'ln3_g', 'ln3_b', 'ffn2_w13', 'ffn2_w2', 'ln4_g', 'ln4_b', 'loss_target', 'm_ffn1_w13', 'm_ffn1_w2', 'm_ln1_g', 'm_ln1_b', 'm_w_in', 'm_conv_w', 'm_conv_b', 'm_w_rgate', 'm_b_rgate', 'm_w_igate', 'm_b_igate', 'm_lru_lambda', 'm_g_lru', 'm_g_sb', 'm_w_out', 'm_ln2_g', 'm_ln2_b', 'm_mem_wq', 'm_mem_wkv', 'm_mem_wo', 'm_ln3_g', 'm_ln3_b', 'm_ffn2_w13', 'm_ffn2_w2', 'm_ln4_g', 'm_ln4_b', 'v_ffn1_w13', 'v_ffn1_w2', 'v_ln1_g', 'v_ln1_b', 'v_w_in', 'v_conv_w', 'v_conv_b', 'v_w_rgate', 'v_b_rgate', 'v_w_igate', 'v_b_igate', 'v_lru_lambda', 'v_g_lru', 'v_g_sb', 'v_w_out', 'v_ln2_g', 'v_ln2_b', 'v_mem_wq', 'v_mem_wkv', 'v_mem_wo', 'v_ln3_g', 'v_ln3_b', 'v_ffn2_w13', 'v_ffn2_w2', 'v_ln4_g', 'v_ln4_b']
TWIN_OUTPUTS = ['loss', 'grad_x', 'grad_ffn1_w13', 'grad_ffn1_w2', 'grad_ln1_g', 'grad_ln1_b', 'grad_w_in', 'grad_conv_w', 'grad_conv_b', 'grad_w_rgate', 'grad_b_rgate', 'grad_w_igate', 'grad_b_igate', 'grad_lru_lambda', 'grad_g_lru', 'grad_g_sb', 'grad_w_out', 'grad_ln2_g', 'grad_ln2_b', 'grad_mem_wq', 'grad_mem_wkv', 'grad_mem_wo', 'grad_ln3_g', 'grad_ln3_b', 'grad_ffn2_w13', 'grad_ffn2_w2', 'grad_ln4_g', 'grad_ln4_b', 'delta_ffn1_w13', 'delta_ffn1_w2', 'delta_ln1_g', 'delta_ln1_b', 'delta_w_in', 'delta_conv_w', 'delta_conv_b', 'delta_w_rgate', 'delta_b_rgate', 'delta_w_igate', 'delta_b_igate', 'delta_lru_lambda', 'delta_g_lru', 'delta_g_sb', 'delta_w_out', 'delta_ln2_g', 'delta_ln2_b', 'delta_mem_wq', 'delta_mem_wkv', 'delta_mem_wo', 'delta_ln3_g', 'delta_ln3_b', 'delta_ffn2_w13', 'delta_ffn2_w2', 'delta_ln4_g', 'delta_ln4_b', 'new_m_ffn1_w13', 'new_m_ffn1_w2', 'new_m_ln1_g', 'new_m_ln1_b', 'new_m_w_in', 'new_m_conv_w', 'new_m_conv_b', 'new_m_w_rgate', 'new_m_b_rgate', 'new_m_w_igate', 'new_m_b_igate', 'new_m_lru_lambda', 'new_m_g_lru', 'new_m_g_sb', 'new_m_w_out', 'new_m_ln2_g', 'new_m_ln2_b', 'new_m_mem_wq', 'new_m_mem_wkv', 'new_m_mem_wo', 'new_m_ln3_g', 'new_m_ln3_b', 'new_m_ffn2_w13', 'new_m_ffn2_w2', 'new_m_ln4_g', 'new_m_ln4_b', 'new_v_ffn1_w13', 'new_v_ffn1_w2', 'new_v_ln1_g', 'new_v_ln1_b', 'new_v_w_in', 'new_v_conv_w', 'new_v_conv_b', 'new_v_w_rgate', 'new_v_b_rgate', 'new_v_w_igate', 'new_v_b_igate', 'new_v_lru_lambda', 'new_v_g_lru', 'new_v_g_sb', 'new_v_w_out', 'new_v_ln2_g', 'new_v_ln2_b', 'new_v_mem_wq', 'new_v_mem_wkv', 'new_v_mem_wo', 'new_v_ln3_g', 'new_v_ln3_b', 'new_v_ffn2_w13', 'new_v_ffn2_w2', 'new_v_ln4_g', 'new_v_ln4_b']
TWIN_LEAF_KINDS = {'loss': 'loss', 'grad_x': 'grad_x', 'grad_ffn1_w13': 'grad_w', 'grad_ffn1_w2': 'grad_w', 'grad_ln1_g': 'grad_w', 'grad_ln1_b': 'grad_w', 'grad_w_in': 'grad_w', 'grad_conv_w': 'grad_w', 'grad_conv_b': 'grad_w', 'grad_w_rgate': 'grad_w', 'grad_b_rgate': 'grad_w', 'grad_w_igate': 'grad_w', 'grad_b_igate': 'grad_w', 'grad_lru_lambda': 'grad_w', 'grad_g_lru': 'grad_w', 'grad_g_sb': 'grad_w', 'grad_w_out': 'grad_w', 'grad_ln2_g': 'grad_w', 'grad_ln2_b': 'grad_w', 'grad_mem_wq': 'grad_w', 'grad_mem_wkv': 'grad_w', 'grad_mem_wo': 'grad_w', 'grad_ln3_g': 'grad_w', 'grad_ln3_b': 'grad_w', 'grad_ffn2_w13': 'grad_w', 'grad_ffn2_w2': 'grad_w', 'grad_ln4_g': 'grad_w', 'grad_ln4_b': 'grad_w', 'delta_ffn1_w13': 'delta_w', 'delta_ffn1_w2': 'delta_w', 'delta_ln1_g': 'delta_w', 'delta_ln1_b': 'delta_w', 'delta_w_in': 'delta_w', 'delta_conv_w': 'delta_w', 'delta_conv_b': 'delta_w', 'delta_w_rgate': 'delta_w', 'delta_b_rgate': 'delta_w', 'delta_w_igate': 'delta_w', 'delta_b_igate': 'delta_w', 'delta_lru_lambda': 'delta_w', 'delta_g_lru': 'delta_w', 'delta_g_sb': 'delta_w', 'delta_w_out': 'delta_w', 'delta_ln2_g': 'delta_w', 'delta_ln2_b': 'delta_w', 'delta_mem_wq': 'delta_w', 'delta_mem_wkv': 'delta_w', 'delta_mem_wo': 'delta_w', 'delta_ln3_g': 'delta_w', 'delta_ln3_b': 'delta_w', 'delta_ffn2_w13': 'delta_w', 'delta_ffn2_w2': 'delta_w', 'delta_ln4_g': 'delta_w', 'delta_ln4_b': 'delta_w', 'new_m_ffn1_w13': 'new_m', 'new_m_ffn1_w2': 'new_m', 'new_m_ln1_g': 'new_m', 'new_m_ln1_b': 'new_m', 'new_m_w_in': 'new_m', 'new_m_conv_w': 'new_m', 'new_m_conv_b': 'new_m', 'new_m_w_rgate': 'new_m', 'new_m_b_rgate': 'new_m', 'new_m_w_igate': 'new_m', 'new_m_b_igate': 'new_m', 'new_m_lru_lambda': 'new_m', 'new_m_g_lru': 'new_m', 'new_m_g_sb': 'new_m', 'new_m_w_out': 'new_m', 'new_m_ln2_g': 'new_m', 'new_m_ln2_b': 'new_m', 'new_m_mem_wq': 'new_m', 'new_m_mem_wkv': 'new_m', 'new_m_mem_wo': 'new_m', 'new_m_ln3_g': 'new_m', 'new_m_ln3_b': 'new_m', 'new_m_ffn2_w13': 'new_m', 'new_m_ffn2_w2': 'new_m', 'new_m_ln4_g': 'new_m', 'new_m_ln4_b': 'new_m', 'new_v_ffn1_w13': 'new_v', 'new_v_ffn1_w2': 'new_v', 'new_v_ln1_g': 'new_v', 'new_v_ln1_b': 'new_v', 'new_v_w_in': 'new_v', 'new_v_conv_w': 'new_v', 'new_v_conv_b': 'new_v', 'new_v_w_rgate': 'new_v', 'new_v_b_rgate': 'new_v', 'new_v_w_igate': 'new_v', 'new_v_b_igate': 'new_v', 'new_v_lru_lambda': 'new_v', 'new_v_g_lru': 'new_v', 'new_v_g_sb': 'new_v', 'new_v_w_out': 'new_v', 'new_v_ln2_g': 'new_v', 'new_v_ln2_b': 'new_v', 'new_v_mem_wq': 'new_v', 'new_v_mem_wkv': 'new_v', 'new_v_mem_wo': 'new_v', 'new_v_ln3_g': 'new_v', 'new_v_ln3_b': 'new_v', 'new_v_ffn2_w13': 'new_v', 'new_v_ffn2_w2': 'new_v', 'new_v_ln4_g': 'new_v', 'new_v_ln4_b': 'new_v'}


def _forward(args):
    return _fwd_reference(*[args[k] for k in FWD_PARAMS])


def _output_shape():
    out = _jax.eval_shape(lambda: _forward(_fwd_setup_inputs(0)))
    return out.shape, out.dtype

N_MICROBATCH = 1
ADAM_LR = 0.001
ADAM_B1 = 0.9
ADAM_B2 = 0.999
ADAM_EPS = 1e-08
ADAM_WD = 0.01
ADAM_STEP = 10
PER_EXAMPLE_BATCH_AXIS = {'x': 0, 'mem': 0, 'loss_target': 0}
SHARED_INPUTS = []
_WEIGHT_DTYPES = {'ffn1_w13': _jnp.float32, 'ffn1_w2': _jnp.float32, 'ln1_g': _jnp.float32, 'ln1_b': _jnp.float32, 'w_in': _jnp.float32, 'conv_w': _jnp.float32, 'conv_b': _jnp.float32, 'w_rgate': _jnp.float32, 'b_rgate': _jnp.float32, 'w_igate': _jnp.float32, 'b_igate': _jnp.float32, 'lru_lambda': _jnp.float32, 'g_lru': _jnp.float32, 'g_sb': _jnp.float32, 'w_out': _jnp.float32, 'ln2_g': _jnp.float32, 'ln2_b': _jnp.float32, 'mem_wq': _jnp.float32, 'mem_wkv': _jnp.float32, 'mem_wo': _jnp.float32, 'ln3_g': _jnp.float32, 'ln3_b': _jnp.float32, 'ffn2_w13': _jnp.float32, 'ffn2_w2': _jnp.float32, 'ln4_g': _jnp.float32, 'ln4_b': _jnp.float32}
MOMENT_SCALE = {'ffn1_w13': 1.370540e-02, 'ffn1_w2': 2.193081e-02, 'ln1_g': 1.436554e+00, 'ln1_b': 1.146668e+00, 'w_in': 1.156840e-01, 'conv_w': 1.233730e-01, 'conv_b': 1.371289e+00, 'w_rgate': 4.265513e-02, 'b_rgate': 2.512896e-02, 'w_igate': 8.013296e-02, 'b_igate': 4.040551e-02, 'lru_lambda': 4.614638e-02, 'g_lru': 1.177111e-01, 'g_sb': 1.144517e-01, 'w_out': 2.044191e-01, 'ln2_g': 1.968454e+00, 'ln2_b': 7.209083e-01, 'mem_wq': 7.863637e-03, 'mem_wkv': 1.218349e-02, 'mem_wo': 1.540328e-02, 'ln3_g': 1.965398e+00, 'ln3_b': 7.216855e-01, 'ffn2_w13': 1.280404e-02, 'ffn2_w2': 2.051889e-02, 'ln4_g': 6.407618e+01, 'ln4_b': 6.023258e+00}


def _to_microbatches(a, axis):
    t = _jnp.moveaxis(a, axis, 0)
    t = t.reshape((N_MICROBATCH, t.shape[0] // N_MICROBATCH) + t.shape[1:])
    return _jnp.moveaxis(t, 1, axis + 1)


def setup_inputs(seed: int = 0) -> dict:
    inp = _fwd_setup_inputs(seed)
    key = _jax.random.fold_in(_jax.random.key(seed), 7919)
    shape, _ = _output_shape()
    out = dict(inp)
    out["loss_target"] = _jax.random.normal(_jax.random.fold_in(key, 0), shape, _jnp.float32)
    for i, name in enumerate(TWIN_WEIGHTS):
        w = inp[name].astype(_jnp.float32)
        if MOMENT_SCALE is None:
            s = _jnp.sqrt(_jnp.mean(_jnp.square(w)) + 1e-30)
        else:
            s = MOMENT_SCALE[name]
        km, kv = _jax.random.split(_jax.random.fold_in(key, i + 1))
        out[name] = w
        out["m_" + name] = s * _jax.random.normal(km, w.shape, _jnp.float32)
        out["v_" + name] = (s * s) * _jax.random.uniform(kv, w.shape, _jnp.float32, 0.5, 1.5)
    if N_MICROBATCH > 1:
        for name, axis in PER_EXAMPLE_BATCH_AXIS.items():
            out[name] = _to_microbatches(out[name], axis)
    return {'x': out['x'], 'mem': out['mem'], 'ffn1_w13': out['ffn1_w13'], 'ffn1_w2': out['ffn1_w2'], 'ln1_g': out['ln1_g'], 'ln1_b': out['ln1_b'], 'w_in': out['w_in'], 'conv_w': out['conv_w'], 'conv_b': out['conv_b'], 'w_rgate': out['w_rgate'], 'b_rgate': out['b_rgate'], 'w_igate': out['w_igate'], 'b_igate': out['b_igate'], 'lru_lambda': out['lru_lambda'], 'g_lru': out['g_lru'], 'g_sb': out['g_sb'], 'w_out': out['w_out'], 'ln2_g': out['ln2_g'], 'ln2_b': out['ln2_b'], 'mem_wq': out['mem_wq'], 'mem_wkv': out['mem_wkv'], 'mem_wo': out['mem_wo'], 'ln3_g': out['ln3_g'], 'ln3_b': out['ln3_b'], 'ffn2_w13': out['ffn2_w13'], 'ffn2_w2': out['ffn2_w2'], 'ln4_g': out['ln4_g'], 'ln4_b': out['ln4_b'], 'loss_target': out['loss_target'], 'm_ffn1_w13': out['m_ffn1_w13'], 'm_ffn1_w2': out['m_ffn1_w2'], 'm_ln1_g': out['m_ln1_g'], 'm_ln1_b': out['m_ln1_b'], 'm_w_in': out['m_w_in'], 'm_conv_w': out['m_conv_w'], 'm_conv_b': out['m_conv_b'], 'm_w_rgate': out['m_w_rgate'], 'm_b_rgate': out['m_b_rgate'], 'm_w_igate': out['m_w_igate'], 'm_b_igate': out['m_b_igate'], 'm_lru_lambda': out['m_lru_lambda'], 'm_g_lru': out['m_g_lru'], 'm_g_sb': out['m_g_sb'], 'm_w_out': out['m_w_out'], 'm_ln2_g': out['m_ln2_g'], 'm_ln2_b': out['m_ln2_b'], 'm_mem_wq': out['m_mem_wq'], 'm_mem_wkv': out['m_mem_wkv'], 'm_mem_wo': out['m_mem_wo'], 'm_ln3_g': out['m_ln3_g'], 'm_ln3_b': out['m_ln3_b'], 'm_ffn2_w13': out['m_ffn2_w13'], 'm_ffn2_w2': out['m_ffn2_w2'], 'm_ln4_g': out['m_ln4_g'], 'm_ln4_b': out['m_ln4_b'], 'v_ffn1_w13': out['v_ffn1_w13'], 'v_ffn1_w2': out['v_ffn1_w2'], 'v_ln1_g': out['v_ln1_g'], 'v_ln1_b': out['v_ln1_b'], 'v_w_in': out['v_w_in'], 'v_conv_w': out['v_conv_w'], 'v_conv_b': out['v_conv_b'], 'v_w_rgate': out['v_w_rgate'], 'v_b_rgate': out['v_b_rgate'], 'v_w_igate': out['v_w_igate'], 'v_b_igate': out['v_b_igate'], 'v_lru_lambda': out['v_lru_lambda'], 'v_g_lru': out['v_g_lru'], 'v_g_sb': out['v_g_sb'], 'v_w_out': out['v_w_out'], 'v_ln2_g': out['v_ln2_g'], 'v_ln2_b': out['v_ln2_b'], 'v_mem_wq': out['v_mem_wq'], 'v_mem_wkv': out['v_mem_wkv'], 'v_mem_wo': out['v_mem_wo'], 'v_ln3_g': out['v_ln3_g'], 'v_ln3_b': out['v_ln3_b'], 'v_ffn2_w13': out['v_ffn2_w13'], 'v_ffn2_w2': out['v_ffn2_w2'], 'v_ln4_g': out['v_ln4_g'], 'v_ln4_b': out['v_ln4_b']}


def _loss(weights, diff, rest, loss_target):
    with _jax.named_scope("forward"):
        args = {**rest, TWIN_DIFF_INPUT: diff, **{k: w.astype(_WEIGHT_DTYPES[k]) for k, w in weights.items()}}
        y = _forward(args)
    with _jax.named_scope("loss_head"):
        err = _jnp.square(y.astype(_jnp.float32) - loss_target)
        return 0.5 * _jnp.sum(_jnp.mean(err, axis=-1)) if err.ndim else 0.5 * err


def _adamw(w, g, m, v):
    m = ADAM_B1 * m + (1.0 - ADAM_B1) * g
    v = ADAM_B2 * v + (1.0 - ADAM_B2) * _jnp.square(g)
    m_hat = m / (1.0 - ADAM_B1 ** ADAM_STEP)
    v_hat = v / (1.0 - ADAM_B2 ** ADAM_STEP)
    delta = -ADAM_LR * (m_hat / (_jnp.sqrt(v_hat) + ADAM_EPS) + ADAM_WD * w)
    return delta, m, v


def reference(x, mem, ffn1_w13, ffn1_w2, ln1_g, ln1_b, w_in, conv_w, conv_b, w_rgate, b_rgate, w_igate, b_igate, lru_lambda, g_lru, g_sb, w_out, ln2_g, ln2_b, mem_wq, mem_wkv, mem_wo, ln3_g, ln3_b, ffn2_w13, ffn2_w2, ln4_g, ln4_b, loss_target, m_ffn1_w13, m_ffn1_w2, m_ln1_g, m_ln1_b, m_w_in, m_conv_w, m_conv_b, m_w_rgate, m_b_rgate, m_w_igate, m_b_igate, m_lru_lambda, m_g_lru, m_g_sb, m_w_out, m_ln2_g, m_ln2_b, m_mem_wq, m_mem_wkv, m_mem_wo, m_ln3_g, m_ln3_b, m_ffn2_w13, m_ffn2_w2, m_ln4_g, m_ln4_b, v_ffn1_w13, v_ffn1_w2, v_ln1_g, v_ln1_b, v_w_in, v_conv_w, v_conv_b, v_w_rgate, v_b_rgate, v_w_igate, v_b_igate, v_lru_lambda, v_g_lru, v_g_sb, v_w_out, v_ln2_g, v_ln2_b, v_mem_wq, v_mem_wkv, v_mem_wo, v_ln3_g, v_ln3_b, v_ffn2_w13, v_ffn2_w2, v_ln4_g, v_ln4_b):
    given = dict(x=x, mem=mem, ffn1_w13=ffn1_w13, ffn1_w2=ffn1_w2, ln1_g=ln1_g, ln1_b=ln1_b, w_in=w_in, conv_w=conv_w, conv_b=conv_b, w_rgate=w_rgate, b_rgate=b_rgate, w_igate=w_igate, b_igate=b_igate, lru_lambda=lru_lambda, g_lru=g_lru, g_sb=g_sb, w_out=w_out, ln2_g=ln2_g, ln2_b=ln2_b, mem_wq=mem_wq, mem_wkv=mem_wkv, mem_wo=mem_wo, ln3_g=ln3_g, ln3_b=ln3_b, ffn2_w13=ffn2_w13, ffn2_w2=ffn2_w2, ln4_g=ln4_g, ln4_b=ln4_b, loss_target=loss_target, m_ffn1_w13=m_ffn1_w13, m_ffn1_w2=m_ffn1_w2, m_ln1_g=m_ln1_g, m_ln1_b=m_ln1_b, m_w_in=m_w_in, m_conv_w=m_conv_w, m_conv_b=m_conv_b, m_w_rgate=m_w_rgate, m_b_rgate=m_b_rgate, m_w_igate=m_w_igate, m_b_igate=m_b_igate, m_lru_lambda=m_lru_lambda, m_g_lru=m_g_lru, m_g_sb=m_g_sb, m_w_out=m_w_out, m_ln2_g=m_ln2_g, m_ln2_b=m_ln2_b, m_mem_wq=m_mem_wq, m_mem_wkv=m_mem_wkv, m_mem_wo=m_mem_wo, m_ln3_g=m_ln3_g, m_ln3_b=m_ln3_b, m_ffn2_w13=m_ffn2_w13, m_ffn2_w2=m_ffn2_w2, m_ln4_g=m_ln4_g, m_ln4_b=m_ln4_b, v_ffn1_w13=v_ffn1_w13, v_ffn1_w2=v_ffn1_w2, v_ln1_g=v_ln1_g, v_ln1_b=v_ln1_b, v_w_in=v_w_in, v_conv_w=v_conv_w, v_conv_b=v_conv_b, v_w_rgate=v_w_rgate, v_b_rgate=v_b_rgate, v_w_igate=v_w_igate, v_b_igate=v_b_igate, v_lru_lambda=v_lru_lambda, v_g_lru=v_g_lru, v_g_sb=v_g_sb, v_w_out=v_w_out, v_ln2_g=v_ln2_g, v_ln2_b=v_ln2_b, v_mem_wq=v_mem_wq, v_mem_wkv=v_mem_wkv, v_mem_wo=v_mem_wo, v_ln3_g=v_ln3_g, v_ln3_b=v_ln3_b, v_ffn2_w13=v_ffn2_w13, v_ffn2_w2=v_ffn2_w2, v_ln4_g=v_ln4_g, v_ln4_b=v_ln4_b)
    weights = {n: given[n] for n in TWIN_WEIGHTS}
    shared = {n: given[n] for n in SHARED_INPUTS}
    per_example = {n: given[n] for n in ['x', 'mem']}
    grad_fn = _jax.value_and_grad(_loss, argnums=(0, 1))

    def one_microbatch(ex, loss_target):
        ex = dict(ex)
        diff = ex.pop(TWIN_DIFF_INPUT)
        return grad_fn(weights, diff, {**shared, **ex}, loss_target)

    if N_MICROBATCH == 1:
        loss, (grad_w, grad_x) = one_microbatch(per_example, given["loss_target"])
    else:
        def body(carry, xs):
            loss_sum, grad_sum = carry
            l_k, (gw_k, gx_k) = one_microbatch(xs[0], xs[1])
            with _jax.named_scope("update"):
                return (loss_sum + l_k, _jax.tree.map(_jnp.add, grad_sum, gw_k)), gx_k

        init = (_jnp.zeros((), _jnp.float32), _jax.tree.map(_jnp.zeros_like, weights))
        (loss, grad_w), grad_x = _jax.lax.scan(body, init, (per_example, given["loss_target"]))
    with _jax.named_scope("update"):
        delta_w, new_m, new_v = {}, {}, {}
        for n in TWIN_WEIGHTS:
            delta_w[n], new_m[n], new_v[n] = _adamw(weights[n], grad_w[n], given["m_" + n], given["v_" + n])
    return (loss, grad_x, *[grad_w[n] for n in TWIN_WEIGHTS], *[delta_w[n] for n in TWIN_WEIGHTS],
            *[new_m[n] for n in TWIN_WEIGHTS], *[new_v[n] for n in TWIN_WEIGHTS])
```

```python
import functools
import math

import jax
import jax.numpy as jnp
from jax import lax
from jax.experimental import pallas as pl
from jax.experimental.pallas import tpu as pltpu

F32, BF16 = jnp.float32, jnp.bfloat16

D_MODEL = 1024
LRU_W = 512
SB_W = 512
SB_PAIR = 128
SB_HEAD = 64
D_FF = 2688
FF_CHUNK = 672
N_DEV = 8
MEM_HEADS = 4
MEM_HD = 256
CONV_W = 4
ALPHA = 2.0 ** 0.25
LN_EPS = 1e-5
RMS_EPS = 1e-6
LRU_C = 8.0
ADAM_LR, ADAM_B1, ADAM_B2, ADAM_EPS, ADAM_WD, ADAM_STEP = 0.001, 0.9, 0.999, 1e-08, 0.01, 10

NN = (((1,), (0,)), ((), ()))
NT = (((1,), (1,)), ((), ()))
MESH = pl.DeviceIdType.MESH
VMEM_LIMIT_MB = 56


def _dot(a, b, dn=NN):
    return lax.dot_general(a, b, dn, preferred_element_type=F32)


def _cparams(n_axes, vmem_mb=None):
    kw = dict(dimension_semantics=("arbitrary",) * n_axes)
    if vmem_mb is not None:
        kw["vmem_limit_bytes"] = vmem_mb << 20
    return pltpu.CompilerParams(**kw)


def _row_tile(rows, want):
    if rows <= want:
        return rows
    t = want - want % 8
    while rows % t:
        t -= 8
    return t


def _matmul(a, b, *, name, out_dtype=F32, trans_a=False, trans_b=False, tm=512, tn=1024, tk=1024,
            scale=1.0, res=None, res_coeff=1.0):
    a_chunked, b_chunked = a.ndim == 3, b.ndim == 3
    nc = a.shape[0] if a_chunked else (b.shape[0] if b_chunked else 1)
    a2, b2 = a.shape[-2:], b.shape[-2:]
    (kdim, m) = a2 if trans_a else a2[::-1]
    n = b2[0] if trans_b else b2[1]
    tm, tn, tk = min(tm, m), min(tn, n), min(tk, kdim)
    assert m % tm == 0 and n % tn == 0 and kdim % tk == 0, (name, m, n, kdim)
    nk = kdim // tk

    def a_idx(c, i, j, k):
        idx = (k, i) if trans_a else (i, k)
        return (c,) + idx if a_chunked else idx

    def b_idx(c, i, j, k):
        idx = (j, k) if trans_b else (k, j)
        return (c,) + idx if b_chunked else idx

    a_blk = (tk, tm) if trans_a else (tm, tk)
    b_blk = (tn, tk) if trans_b else (tk, tn)
    in_specs = [pl.BlockSpec(((None,) + a_blk) if a_chunked else a_blk, a_idx),
                pl.BlockSpec(((None,) + b_blk) if b_chunked else b_blk, b_idx)]
    args = [a, b]
    if res is not None:
        in_specs.append(pl.BlockSpec((None, tm, tn), lambda c, i, j, k: (c, i, j)))
        args.append(res.reshape((nc, m, n)))

    def body(*refs):
        if res is not None:
            a_ref, b_ref, r_ref, o_ref, acc_ref = refs
        else:
            a_ref, b_ref, o_ref, acc_ref = refs
        k = pl.program_id(3)

        @pl.when(k == 0)
        def _():
            acc_ref[...] = jnp.zeros_like(acc_ref)

        av = a_ref[...]
        if trans_a:
            av = av.astype(F32).T
        acc_ref[...] += _dot(av.astype(BF16), b_ref[...].astype(BF16), NT if trans_b else NN)

        @pl.when(k == nk - 1)
        def _():
            out = acc_ref[...]
            if scale != 1.0:
                out = out * scale
            if res is not None:
                out = out + res_coeff * r_ref[...]
            o_ref[...] = out.astype(out_dtype)

    out = pl.pallas_call(
        body, grid=(nc, m // tm, n // tn, nk), in_specs=in_specs,
        out_specs=pl.BlockSpec((None, tm, tn), lambda c, i, j, k: (c, i, j)),
        out_shape=jax.ShapeDtypeStruct((nc, m, n), out_dtype),
        scratch_shapes=[pltpu.VMEM((tm, tn), F32)],
        compiler_params=_cparams(4, VMEM_LIMIT_MB), name=name)(*args)
    return out if (a_chunked or b_chunked) else out[0]


def _ln_math(z, g, b):
    mu = jnp.mean(z, axis=-1, keepdims=True)
    zc = z - mu
    var = jnp.mean(zc * zc, axis=-1, keepdims=True)
    return zc * lax.rsqrt(var + LN_EPS) * g + b


def _row_spec(tm, d):
    return pl.BlockSpec((tm, d), lambda i: (i, 0))


def _par_spec(d, rows=1):
    return pl.BlockSpec((rows, d), lambda i: (0, 0))


def _ln_fwd(x, f, fscale, g, b, *, name):
    s, d = x.shape
    tm = min(512, s)

    def body(x_ref, f_ref, g_ref, b_ref, o_ref):
        z = ALPHA * x_ref[...] + fscale * f_ref[...]
        o_ref[...] = _ln_math(z, g_ref[...], b_ref[...])

    return pl.pallas_call(
        body, grid=(s // tm,), in_specs=[_row_spec(tm, d), _row_spec(tm, d), _par_spec(d), _par_spec(d)],
        out_specs=_row_spec(tm, d), out_shape=jax.ShapeDtypeStruct((s, d), F32),
        compiler_params=_cparams(1), name=name)(x, f, g, b)


def _ln_bwd(x, f, fscale, g, b, dy, *, name, target=None):
    s, d = x.shape
    tm = min(512, s)
    with_loss = target is not None

    def body(*refs):
        if with_loss:
            x_ref, f_ref, g_ref, b_ref, t_ref, dz_ref, dg_ref, db_ref, loss_ref = refs
        else:
            x_ref, f_ref, g_ref, b_ref, dy_ref, dz_ref, dg_ref, db_ref = refs
        i = pl.program_id(0)

        @pl.when(i == 0)
        def _():
            dg_ref[...] = jnp.zeros_like(dg_ref)
            db_ref[...] = jnp.zeros_like(db_ref)
            if with_loss:
                loss_ref[...] = jnp.zeros_like(loss_ref)

        z = ALPHA * x_ref[...] + fscale * f_ref[...]
        y, vjp = jax.vjp(_ln_math, z, g_ref[...], b_ref[...])
        if with_loss:
            err = y - t_ref[...]
            sq = jnp.sum(err * err, axis=1, keepdims=True)
            loss_ref[...] += jnp.sum(sq, axis=0, keepdims=True) * (0.5 / d)
            cot = err * (1.0 / d)
        else:
            cot = dy_ref[...]
        dz, dg, db = vjp(cot)
        dz_ref[...] = dz
        dg_ref[...] += dg
        db_ref[...] += db

    in_specs = [_row_spec(tm, d), _row_spec(tm, d), _par_spec(d), _par_spec(d), _row_spec(tm, d)]
    out_specs = [_row_spec(tm, d), _par_spec(d), _par_spec(d)]
    out_shape = [jax.ShapeDtypeStruct((s, d), F32), jax.ShapeDtypeStruct((1, d), F32), jax.ShapeDtypeStruct((1, d), F32)]
    if with_loss:
        out_specs.append(pl.BlockSpec((1, 1), lambda i: (0, 0)))
        out_shape.append(jax.ShapeDtypeStruct((1, 1), F32))
    return pl.pallas_call(
        body, grid=(s // tm,), in_specs=in_specs, out_specs=out_specs, out_shape=out_shape,
        compiler_params=_cparams(1), name=name)(x, f, g, b, target if with_loss else dy)


def _load_weights_once(pairs, sem):
    @pl.when(pl.program_id(0) == 0)
    def _():
        copies = [pltpu.make_async_copy(src, dst, sem.at[n]) for n, (src, dst) in enumerate(pairs)]
        for c in copies:
            c.start()
        for c in copies:
            c.wait()


def _ffn_fwd(h, w13c, w2, *, name):
    s, d = h.shape
    tm = min(256, s)
    half = N_DEV // 2

    def body(h_ref, w13_hbm, w2_hbm, gu_ref, f_ref, w13_v, w2_v, sem):
        _load_weights_once([(w13_hbm, w13_v), (w2_hbm, w2_v)], sem)
        hb = h_ref[...].astype(BF16)
        acc = jnp.zeros((tm, d), F32)
        for k in range(half):
            g = _dot(hb, w13_v[k])
            u = _dot(hb, w13_v[k + half])
            gu_ref[k] = g.astype(BF16)
            gu_ref[k + half] = u.astype(BF16)
            a = g * jax.nn.sigmoid(g) * u
            acc = acc + _dot(a.astype(BF16), w2_v[pl.ds(k * FF_CHUNK, FF_CHUNK), :])
        f_ref[...] = acc

    any_spec = pl.BlockSpec(memory_space=pl.ANY)
    return pl.pallas_call(
        body, grid=(s // tm,), in_specs=[_row_spec(tm, d), any_spec, any_spec],
        out_specs=[pl.BlockSpec((N_DEV, tm, FF_CHUNK), lambda i: (0, i, 0)), _row_spec(tm, d)],
        out_shape=[jax.ShapeDtypeStruct((N_DEV, s, FF_CHUNK), BF16), jax.ShapeDtypeStruct((s, d), F32)],
        scratch_shapes=[pltpu.VMEM(w13c.shape, BF16), pltpu.VMEM(w2.shape, BF16), pltpu.SemaphoreType.DMA((2,))],
        compiler_params=_cparams(1, VMEM_LIMIT_MB), name=name)(h, w13c, w2)


def _ffn_bwd(dz, gu, w13c, w2, *, name):
    s, d = dz.shape
    tm = min(256, s)
    half = N_DEV // 2

    def body(dz_ref, gu_ref, w13_hbm, w2_hbm, dh_ref, dgu_ref, act_ref, w13_v, w2_v, sem):
        _load_weights_once([(w13_hbm, w13_v), (w2_hbm, w2_v)], sem)
        dzv = dz_ref[...]
        dfb = (0.5 * dzv).astype(BF16)
        acc = ALPHA * dzv
        for k in range(half):
            g = gu_ref[k].astype(F32)
            u = gu_ref[k + half].astype(F32)
            da = _dot(dfb, w2_v[pl.ds(k * FF_CHUNK, FF_CHUNK), :], NT)
            sg = jax.nn.sigmoid(g)
            silu = g * sg
            dg = (da * u * (sg * (1.0 + g * (1.0 - sg)))).astype(BF16)
            du = (da * silu).astype(BF16)
            act_ref[k] = (silu * u).astype(BF16)
            dgu_ref[k] = dg
            dgu_ref[k + half] = du
            acc = acc + _dot(dg, w13_v[k], NT) + _dot(du, w13_v[k + half], NT)
        dh_ref[...] = acc

    any_spec = pl.BlockSpec(memory_space=pl.ANY)
    return pl.pallas_call(
        body, grid=(s // tm,),
        in_specs=[_row_spec(tm, d), pl.BlockSpec((N_DEV, tm, FF_CHUNK), lambda i: (0, i, 0)), any_spec, any_spec],
        out_specs=[_row_spec(tm, d), pl.BlockSpec((N_DEV, tm, FF_CHUNK), lambda i: (0, i, 0)),
                   pl.BlockSpec((half, tm, FF_CHUNK), lambda i: (0, i, 0))],
        out_shape=[jax.ShapeDtypeStruct((s, d), F32), jax.ShapeDtypeStruct((N_DEV, s, FF_CHUNK), BF16),
                   jax.ShapeDtypeStruct((half, s, FF_CHUNK), BF16)],
        scratch_shapes=[pltpu.VMEM(w13c.shape, BF16), pltpu.VMEM(w2.shape, BF16), pltpu.SemaphoreType.DMA((2,))],
        compiler_params=_cparams(1, VMEM_LIMIT_MB), name=name)(dz, gu, w13c, w2)


def _shift_down(x, prev8, d):
    t, c = x.shape
    row = lax.broadcasted_iota(jnp.int32, (t, c), 0)
    xr = pltpu.roll(x, d, 0)
    pr = pltpu.roll(prev8, d, 0)
    if t > 8:
        pr = jnp.concatenate([pr, jnp.zeros((t - 8, c), x.dtype)], axis=0)
    return jnp.where(row < d, pr, xr)


def _shift_up(x, next8, d):
    t, c = x.shape
    row = lax.broadcasted_iota(jnp.int32, (t, c), 0)
    xr = pltpu.roll(x, t - d, 0)
    nr = pltpu.roll(next8, 8 - d, 0)
    if t > 8:
        nr = jnp.concatenate([jnp.zeros((t - 8, c), x.dtype), nr], axis=0)
    return jnp.where(row >= t - d, nr, xr)


def _scan_fwd(a, u):
    t = a.shape[0]
    row = lax.broadcasted_iota(jnp.int32, a.shape, 0)
    d = 1
    while d < t:
        a_s, u_s = pltpu.roll(a, d, 0), pltpu.roll(u, d, 0)
        m = row >= d
        u = jnp.where(m, u + a * u_s, u)
        a = jnp.where(m, a * a_s, a)
        d *= 2
    return a, u


def _scan_bwd(b, x):
    t = b.shape[0]
    row = lax.broadcasted_iota(jnp.int32, b.shape, 0)
    d = 1
    while d < t:
        b_s, x_s = pltpu.roll(b, t - d, 0), pltpu.roll(x, t - d, 0)
        m = row < t - d
        x = jnp.where(m, x + b * x_s, x)
        b = jnp.where(m, b * b_s, b)
        d *= 2
    return b, x


def _lru_elem(xc, pr, pi, b_r, b_i, lam):
    r = jax.nn.sigmoid(pr + b_r)
    ig = jax.nn.sigmoid(pi + b_i)
    softplus_neg_lam = jnp.maximum(-lam, 0.0) + jnp.log1p(jnp.exp(-jnp.abs(lam)))
    log_a = (-LRU_C) * r * softplus_neg_lam
    a = jnp.exp(log_a)
    one_minus_a2 = -jnp.tanh(log_a) * (jnp.exp(2.0 * log_a) + 1.0)
    u = jnp.sqrt(one_minus_a2) * (ig * xc)
    return a, u


def _gelu_tanh(x):
    return 0.5 * x * (1.0 + jnp.tanh(math.sqrt(2.0 / math.pi) * (x + 0.044715 * (x * x * x))))


def _conv_fwd(xb, prev8, w, b):
    out = b + w[3:4, :] * xb
    for d in range(1, CONV_W):
        out = out + w[3 - d:4 - d, :] * _shift_down(xb, prev8, d)
    return out


def _lru_fwd(xbg, conv_w, conv_b, w_rg, b_rg, w_ig, b_ig, lam, *, name):
    s = xbg.shape[0]
    c = LRU_W
    t = min(512, s)
    t8 = t // 8

    def body(xb_ref, gate_ref, prev_ref, cw_ref, cb_ref, wr_ref, br_ref, wi_ref, bi_ref, lam_ref,
             xc_ref, h_ref, y_ref, carry_ref):
        i = pl.program_id(0)

        @pl.when(i == 0)
        def _():
            carry_ref[...] = jnp.zeros_like(carry_ref)

        prev8 = jnp.where(i > 0, prev_ref[...], 0.0)
        xc = _conv_fwd(xb_ref[...], prev8, cw_ref[...], cb_ref[...])
        xcb = xc.astype(BF16)
        a, u = _lru_elem(xc, _dot(xcb, wr_ref[...]), _dot(xcb, wi_ref[...]), br_ref[...], bi_ref[...], lam_ref[...])
        a_cum, h0 = _scan_fwd(a, u)
        h = h0 + a_cum * carry_ref[0:1, :]
        carry_ref[...] = jnp.broadcast_to(h[t - 1:t, :], carry_ref.shape)
        xc_ref[...] = xc
        h_ref[...] = h
        y_ref[...] = h * _gelu_tanh(gate_ref[...])

    tile = lambda col: pl.BlockSpec((t, c), lambda i: (i, col))
    par = lambda rows: pl.BlockSpec((rows, c), lambda i: (0, 0))
    out_spec = pl.BlockSpec((t, c), lambda i: (i, 0))
    return pl.pallas_call(
        body, grid=(s // t,),
        in_specs=[tile(0), tile(1), pl.BlockSpec((8, c), lambda i: (jnp.maximum(i * t8 - 1, 0), 0)),
                  par(CONV_W), par(1), pl.BlockSpec((c, c), lambda i: (0, 0)), par(1),
                  pl.BlockSpec((c, c), lambda i: (0, 0)), par(1), par(1)],
        out_specs=[out_spec, out_spec, out_spec],
        out_shape=[jax.ShapeDtypeStruct((s, c), F32)] * 3,
        scratch_shapes=[pltpu.VMEM((8, c), F32)],
        compiler_params=_cparams(1, VMEM_LIMIT_MB), name=name)(
            xbg, xbg, xbg, conv_w, conv_b, w_rg, b_rg, w_ig, b_ig, lam)


def _lru_bwd(xc, xbg, hseq, dy, w_rg, b_rg, w_ig, b_ig, lam, *, name):
    s, c = xc.shape
    t = min(512, s)
    t8 = t // 8
    nt = s // t

    def body(xc_ref, gate_ref, h_ref, hprev_ref, dy_ref, wr_ref, br_ref, wi_ref, bi_ref, lam_ref,
             dxc_ref, dgate_ref, dwr_ref, dwi_ref, dbr_ref, dbi_ref, dlam_ref, a8_ref, l8_ref):
        i = pl.program_id(0)

        @pl.when(i == 0)
        def _():
            for r in (dwr_ref, dwi_ref, dbr_ref, dbi_ref, dlam_ref, a8_ref, l8_ref):
                r[...] = jnp.zeros_like(r)

        xc_v = xc_ref[...]
        xcb = xc_v.astype(BF16)
        (a, _), vjp = jax.vjp(_lru_elem, xc_v, _dot(xcb, wr_ref[...]), _dot(xcb, wi_ref[...]),
                              br_ref[...], bi_ref[...], lam_ref[...])
        gl, vjp_gelu = jax.vjp(_gelu_tanh, gate_ref[...])
        dyv, hs = dy_ref[...], h_ref[...]
        (dgate,) = vjp_gelu(dyv * hs)
        b_next = _shift_up(a, a8_ref[...], 1)
        b_cum, l0 = _scan_bwd(b_next, dyv * gl)
        lt = l0 + b_cum * l8_ref[0:1, :]
        a8_ref[...] = a[0:8, :]
        l8_ref[...] = lt[0:8, :]
        hprev8 = jnp.where(i < nt - 1, hprev_ref[...], 0.0)
        da = lt * _shift_down(hs, hprev8, 1)
        dxc_e, dpr, dpi, dbr, dbi, dlam = vjp((da, lt))
        dprb, dpib = dpr.astype(BF16), dpi.astype(BF16)
        dxc_ref[...] = dxc_e + _dot(dprb, wr_ref[...], NT) + _dot(dpib, wi_ref[...], NT)
        dgate_ref[...] = dgate
        xct = xc_v.T.astype(BF16)
        dwr_ref[...] += _dot(xct, dprb)
        dwi_ref[...] += _dot(xct, dpib)
        dbr_ref[...] += dbr
        dbi_ref[...] += dbi
        dlam_ref[...] += dlam

    rev = lambda col: pl.BlockSpec((t, c), lambda i: (nt - 1 - i, col))
    par = pl.BlockSpec((1, c), lambda i: (0, 0))
    sq = pl.BlockSpec((c, c), lambda i: (0, 0))
    return pl.pallas_call(
        body, grid=(nt,),
        in_specs=[rev(0), rev(1), rev(0), pl.BlockSpec((8, c), lambda i: (jnp.maximum((nt - 1 - i) * t8 - 1, 0), 0)),
                  rev(0), sq, par, sq, par, par],
        out_specs=[rev(0), rev(0), sq, sq, par, par, par],
        out_shape=[jax.ShapeDtypeStruct((s, c), F32)] * 2 + [jax.ShapeDtypeStruct((c, c), F32)] * 2
                  + [jax.ShapeDtypeStruct((1, c), F32)] * 3,
        scratch_shapes=[pltpu.VMEM((8, c), F32), pltpu.VMEM((8, c), F32)],
        compiler_params=_cparams(1, VMEM_LIMIT_MB), name=name)(
            xc, xbg, hseq, hseq, dy, w_rg, b_rg, w_ig, b_ig, lam)


def _conv_bwd(dxc, xbg, conv_w, *, name):
    s, c = dxc.shape
    t = min(512, s)
    t8 = t // 8
    nt = s // t

    def body(dxc_ref, next_ref, xb_ref, prev_ref, w_ref, dxb_ref, dw_ref, db_ref):
        i = pl.program_id(0)

        @pl.when(i == 0)
        def _():
            dw_ref[...] = jnp.zeros_like(dw_ref)
            db_ref[...] = jnp.zeros_like(db_ref)

        g = dxc_ref[...]
        xb = xb_ref[...]
        w = w_ref[...]
        next8 = jnp.where(i < nt - 1, next_ref[...], 0.0)
        prev8 = jnp.where(i > 0, prev_ref[...], 0.0)
        dxb = w[3:4, :] * g
        dw_ref[3:4, :] += jnp.sum(g * xb, axis=0, keepdims=True)
        for d in range(1, CONV_W):
            dxb = dxb + w[3 - d:4 - d, :] * _shift_up(g, next8, d)
            dw_ref[3 - d:4 - d, :] += jnp.sum(g * _shift_down(xb, prev8, d), axis=0, keepdims=True)
        dxb_ref[...] = dxb
        db_ref[...] += jnp.sum(g, axis=0, keepdims=True)

    tile = pl.BlockSpec((t, c), lambda i: (i, 0))
    return pl.pallas_call(
        body, grid=(nt,),
        in_specs=[tile, pl.BlockSpec((8, c), lambda i: (jnp.minimum((i + 1) * t8, s // 8 - 1), 0)),
                  tile, pl.BlockSpec((8, c), lambda i: (jnp.maximum(i * t8 - 1, 0), 0)),
                  pl.BlockSpec((CONV_W, c), lambda i: (0, 0))],
        out_specs=[tile, pl.BlockSpec((CONV_W, c), lambda i: (0, 0)), pl.BlockSpec((1, c), lambda i: (0, 0))],
        out_shape=[jax.ShapeDtypeStruct((s, c), F32), jax.ShapeDtypeStruct((CONV_W, c), F32),
                   jax.ShapeDtypeStruct((1, c), F32)],
        compiler_params=_cparams(1), name=name)(dxc, dxc, xbg, xbg, conv_w)


def _sb_scores(qh, kb, q0, k0, tq, tk):
    z = _dot(qh, kb, NT) * (1.0 / math.sqrt(SB_HEAD))
    row = lax.broadcasted_iota(jnp.int32, (tq, tk), 0)
    col = lax.broadcasted_iota(jnp.int32, (tq, tk), 1)
    valid = (k0 + col) < (q0 + row)
    ls = jnp.minimum(z, 0.0) - jnp.log1p(jnp.exp(-jnp.abs(z)))
    n = jnp.where(valid, ls - z, 0.0)
    return valid, ls, n


def _hi_lo_dot(x, tri):
    hi = x.astype(BF16)
    lo = (x - hi.astype(F32)).astype(BF16)
    return _dot(hi, tri) + _dot(lo, tri)


def _tri(tk, inclusive):
    r = lax.broadcasted_iota(jnp.int32, (tk, tk), 0)
    c = lax.broadcasted_iota(jnp.int32, (tk, tk), 1)
    return ((r >= c) if inclusive else (r > c)).astype(BF16)


def _sb_fwd(qkv, *, name):
    s = qkv.shape[0]
    tb = min(256, s)
    npair = SB_W // SB_PAIR

    def body(q_ref, k_ref, v_ref, o_ref):
        i = pl.program_id(1)
        q = q_ref[...]
        lane = lax.broadcasted_iota(jnp.int32, (tb, SB_PAIR), 1)
        tri = _tri(tb, False)

        def head(qh):
            def step(jj, carry):
                run, acc = carry
                j = i - jj
                k0 = pl.multiple_of(j * tb, tb)
                kb = k_ref[pl.ds(k0, tb), :]
                vb = v_ref[pl.ds(k0, tb), :]
                valid, ls, n = _sb_scores(qh, kb, i * tb, k0, tb, tb)
                suffix = _hi_lo_dot(n, tri) + run
                w = jnp.where(valid, jnp.exp(ls + suffix), 0.0)
                acc = acc + _dot(w.astype(BF16), vb)
                run = run + jnp.sum(n, axis=1, keepdims=True)
                return run, acc

            _, acc = lax.fori_loop(0, i + 1, step, (jnp.zeros((tb, 1), F32), jnp.zeros((tb, SB_PAIR), F32)))
            return acc

        first = lane < SB_HEAD
        o_a = head(jnp.where(first, q, jnp.zeros_like(q)))
        o_b = head(jnp.where(first, jnp.zeros_like(q), q))
        o_ref[...] = jnp.where(first, o_a, o_b)

    return pl.pallas_call(
        body, grid=(npair, s // tb),
        in_specs=[pl.BlockSpec((tb, SB_PAIR), lambda p, i: (i, p)),
                  pl.BlockSpec((s, SB_PAIR), lambda p, i: (0, npair + p)),
                  pl.BlockSpec((s, SB_PAIR), lambda p, i: (0, 2 * npair + p))],
        out_specs=pl.BlockSpec((tb, SB_PAIR), lambda p, i: (i, p)),
        out_shape=jax.ShapeDtypeStruct((s, SB_W), F32),
        compiler_params=_cparams(2, VMEM_LIMIT_MB), name=name)(qkv, qkv, qkv)


def _sb_bwd(qkv, o, do, *, name):
    s = qkv.shape[0]
    tb = min(256, s)
    npair = SB_W // SB_PAIR

    def body(q_ref, k_ref, v_ref, o_ref, do_ref, dq_ref, dk_ref, dv_ref):
        i = pl.program_id(1)

        @pl.when(i == 0)
        def _():
            dk_ref[...] = jnp.zeros_like(dk_ref)
            dv_ref[...] = jnp.zeros_like(dv_ref)

        q = q_ref[...]
        dob = do_ref[...].astype(BF16)
        prod = dob.astype(F32) * o_ref[...]
        lane = lax.broadcasted_iota(jnp.int32, (tb, SB_PAIR), 1)
        first = lane < SB_HEAD
        tri_x, tri_i = _tri(tb, False), _tri(tb, True)

        def head(sel):
            qh = jnp.where(sel, q, jnp.zeros_like(q))
            doh = jnp.where(sel, dob, jnp.zeros_like(dob))
            delta = jnp.sum(jnp.where(sel, prod, 0.0), axis=1, keepdims=True)

            def step(jj, carry):
                run_n, run_g, dq_acc = carry
                j = i - jj
                k0 = pl.multiple_of(j * tb, tb)
                kb = k_ref[pl.ds(k0, tb), :]
                vb = v_ref[pl.ds(k0, tb), :]
                valid, ls, n = _sb_scores(qh, kb, i * tb, k0, tb, tb)
                wb = jnp.where(valid, jnp.exp(ls + _hi_lo_dot(n, tri_x) + run_n), 0.0).astype(BF16)
                g = wb.astype(F32) * _dot(doh, vb, NT)
                pre = delta - (_hi_lo_dot(g, tri_i) + run_g)
                beta = jnp.exp(ls)
                dz = jnp.where(valid, g * (1.0 - beta) - pre * beta, 0.0) * (1.0 / math.sqrt(SB_HEAD))
                dq_acc = dq_acc + _dot(dz.astype(BF16), kb)
                dk_ref[pl.ds(k0, tb), :] += _dot(dz.T.astype(BF16), qh)
                dv_ref[pl.ds(k0, tb), :] += _dot(wb.astype(F32).T.astype(BF16), doh)
                return (run_n + jnp.sum(n, axis=1, keepdims=True), run_g + jnp.sum(g, axis=1, keepdims=True), dq_acc)

            zero = jnp.zeros((tb, 1), F32)
            return lax.fori_loop(0, i + 1, step, (zero, zero, jnp.zeros((tb, SB_PAIR), F32)))[2]

        dq_a = head(first)
        dq_b = head(jnp.logical_not(first))
        dq_ref[...] = jnp.where(first, dq_a, dq_b)

    qtile = pl.BlockSpec((tb, SB_PAIR), lambda p, i: (i, p))
    col = pl.BlockSpec((s, SB_PAIR), lambda p, i: (0, p))
    return pl.pallas_call(
        body, grid=(npair, s // tb),
        in_specs=[qtile, pl.BlockSpec((s, SB_PAIR), lambda p, i: (0, npair + p)),
                  pl.BlockSpec((s, SB_PAIR), lambda p, i: (0, 2 * npair + p)), qtile, qtile],
        out_specs=[qtile, col, col],
        out_shape=[jax.ShapeDtypeStruct((s, SB_W), F32)] * 3,
        compiler_params=_cparams(2, VMEM_LIMIT_MB), name=name)(qkv, qkv, qkv, o, do)


def _mixnorm_math(yl, ys, gl, gs):
    def rms(x, g):
        return x * lax.rsqrt(jnp.mean(x * x, axis=-1, keepdims=True) + RMS_EPS) * g
    return rms(yl, gl), rms(ys, gs)


def _mixnorm_fwd(yl, ys, gl, gs, *, name):
    s, c = yl.shape
    tm = min(512, s)

    def body(yl_ref, ys_ref, gl_ref, gs_ref, y_ref):
        a, b = _mixnorm_math(yl_ref[...], ys_ref[...], gl_ref[...], gs_ref[...])
        y_ref[:, 0:c] = a
        y_ref[:, c:2 * c] = b

    return pl.pallas_call(
        body, grid=(s // tm,), in_specs=[_row_spec(tm, c), _row_spec(tm, c), _par_spec(c), _par_spec(c)],
        out_specs=_row_spec(tm, 2 * c), out_shape=jax.ShapeDtypeStruct((s, 2 * c), F32),
        compiler_params=_cparams(1), name=name)(yl, ys, gl, gs)


def _mixnorm_bwd(yl, ys, gl, gs, dy, *, name):
    s, c = yl.shape
    tm = min(512, s)

    def body(yl_ref, ys_ref, gl_ref, gs_ref, dy_ref, dyl_ref, dys_ref, dgl_ref, dgs_ref):
        @pl.when(pl.program_id(0) == 0)
        def _():
            dgl_ref[...] = jnp.zeros_like(dgl_ref)
            dgs_ref[...] = jnp.zeros_like(dgs_ref)

        _, vjp = jax.vjp(_mixnorm_math, yl_ref[...], ys_ref[...], gl_ref[...], gs_ref[...])
        dyl, dys, dgl, dgs = vjp((dy_ref[:, 0:c], dy_ref[:, c:2 * c]))
        dyl_ref[...] = dyl
        dys_ref[...] = dys
        dgl_ref[...] += dgl
        dgs_ref[...] += dgs

    return pl.pallas_call(
        body, grid=(s // tm,),
        in_specs=[_row_spec(tm, c), _row_spec(tm, c), _par_spec(c), _par_spec(c), _row_spec(tm, 2 * c)],
        out_specs=[_row_spec(tm, c), _row_spec(tm, c), _par_spec(c), _par_spec(c)],
        out_shape=[jax.ShapeDtypeStruct((s, c), F32)] * 2 + [jax.ShapeDtypeStruct((1, c), F32)] * 2,
        compiler_params=_cparams(1), name=name)(yl, ys, gl, gs, dy)


def _cross_probs(qh, kh):
    sc = _dot(qh, kh, NT) * (1.0 / math.sqrt(MEM_HD))
    e = jnp.exp(sc - jnp.max(sc, axis=-1, keepdims=True))
    return e / jnp.sum(e, axis=-1, keepdims=True)


def _cross_fwd(q, kv, *, name):
    s, d = q.shape
    mlen = kv.shape[1]
    tm = min(512, s)

    def body(q_ref, kv_ref, o_ref):
        for h in range(MEM_HEADS):
            cols = slice(h * MEM_HD, (h + 1) * MEM_HD)
            p = _cross_probs(q_ref[:, cols].astype(BF16), kv_ref[h].astype(BF16))
            o_ref[:, cols] = _dot(p.astype(BF16), kv_ref[MEM_HEADS + h].astype(BF16))

    return pl.pallas_call(
        body, grid=(s // tm,),
        in_specs=[_row_spec(tm, d), pl.BlockSpec((2 * MEM_HEADS, mlen, MEM_HD), lambda i: (0, 0, 0))],
        out_specs=_row_spec(tm, d), out_shape=jax.ShapeDtypeStruct((s, d), F32),
        compiler_params=_cparams(1, VMEM_LIMIT_MB), name=name)(q, kv)


def _cross_bwd(q, kv, do, *, name):
    s, d = q.shape
    mlen = kv.shape[1]
    tm = min(512, s)

    def body(q_ref, kv_ref, do_ref, dq_ref, dkv_ref):
        @pl.when(pl.program_id(0) == 0)
        def _():
            dkv_ref[...] = jnp.zeros_like(dkv_ref)

        for h in range(MEM_HEADS):
            cols = slice(h * MEM_HD, (h + 1) * MEM_HD)
            qh = q_ref[:, cols].astype(BF16)
            kh = kv_ref[h].astype(BF16)
            doh = do_ref[:, cols].astype(BF16)
            p = _cross_probs(qh, kh)
            dp = _dot(doh, kv_ref[MEM_HEADS + h].astype(BF16), NT)
            ds = p * (dp - jnp.sum(dp * p, axis=-1, keepdims=True)) * (1.0 / math.sqrt(MEM_HD))
            dq_ref[:, cols] = _dot(ds.astype(BF16), kh)
            dkv_ref[h] += _dot(ds.T.astype(BF16), qh)
            dkv_ref[MEM_HEADS + h] += _dot(p.T.astype(BF16), doh)

    kv_spec = pl.BlockSpec((2 * MEM_HEADS, mlen, MEM_HD), lambda i: (0, 0, 0))
    return pl.pallas_call(
        body, grid=(s // tm,), in_specs=[_row_spec(tm, d), kv_spec, _row_spec(tm, d)],
        out_specs=[_row_spec(tm, d), kv_spec],
        out_shape=[jax.ShapeDtypeStruct((s, d), F32), jax.ShapeDtypeStruct(kv.shape, F32)],
        compiler_params=_cparams(1, VMEM_LIMIT_MB), name=name)(q, kv, do)


def _proj_bwd_dx(parts, w_in, dz, *, name):
    s = dz.shape[0]
    d = D_MODEL
    tm = min(256, s)
    widths = [p.shape[1] for p in parts]

    def body(*refs):
        part_refs, w_ref, dz_ref, o_ref = refs[:len(parts)], refs[len(parts)], refs[-2], refs[-1]
        acc = ALPHA * dz_ref[...]
        off = 0
        for p_ref, wd in zip(part_refs, widths):
            acc = acc + _dot(p_ref[...].astype(BF16), w_ref[:, off:off + wd], NT)
            off += wd
        o_ref[...] = acc

    return pl.pallas_call(
        body, grid=(s // tm,),
        in_specs=[_row_spec(tm, wd) for wd in widths] + [pl.BlockSpec(w_in.shape, lambda i: (0, 0)), _row_spec(tm, d)],
        out_specs=_row_spec(tm, d), out_shape=jax.ShapeDtypeStruct((s, d), F32),
        compiler_params=_cparams(1, VMEM_LIMIT_MB), name=name)(*parts, w_in, dz)


def _mesh_pos():
    return lax.axis_index("x"), lax.axis_index("y"), lax.axis_index("c")


def _all_gather(shards, *, name):
    n = len(shards)

    def body(*refs):
        ins, outs = refs[:n], refs[n:2 * n]
        send_sems, recv_sems, local_sems = refs[2 * n:]
        x, y, c = _mesh_pos()
        me, sibling = (x, y, c), (x, y, 1 - c)
        chips = [(1 - x, y), (x, 1 - y), (1 - x, 1 - y)]

        def copy(a, k, block, to, src=None):
            dst = outs[a].at[4 * block[0] + 2 * block[1] + block[2]]
            return pltpu.make_async_remote_copy(
                src_ref=dst if src is None else src, dst_ref=dst,
                send_sem=send_sems.at[a, k], recv_sem=recv_sems.at[a, k], device_id=to, device_id_type=MESH)

        mine = [pltpu.make_async_copy(ins[a], outs[a].at[4 * x + 2 * y + c], local_sems.at[a]) for a in range(n)]
        for cp in mine:
            cp.start()
        first = []
        for a in range(n):
            first.append(copy(a, 0, me, sibling, src=ins[a]))
            first += [copy(a, 1 + j, me, (*chip, c), src=ins[a]) for j, chip in enumerate(chips)]
        for cp in first:
            cp.start()
        passed = []
        for a in range(n):
            for j, chip in enumerate(chips):
                copy(a, 1 + j, (*chip, c), me).wait_recv()
                fwd = copy(a, 4 + j, (*chip, c), sibling)
                fwd.start()
                passed.append(fwd)
        for a in range(n):
            copy(a, 0, sibling, me).wait_recv()
            for j, chip in enumerate(chips):
                copy(a, 4 + j, (*chip, 1 - c), me).wait_recv()
        for cp in first + passed:
            cp.wait_send()
        for cp in mine:
            cp.wait()

    any_spec = pl.BlockSpec(memory_space=pl.ANY)
    return pl.pallas_call(
        body, in_specs=[any_spec] * n, out_specs=[any_spec] * n,
        out_shape=[jax.ShapeDtypeStruct((N_DEV,) + a.shape, a.dtype) for a in shards],
        scratch_shapes=[pltpu.SemaphoreType.DMA((n, 7)), pltpu.SemaphoreType.DMA((n, 7)), pltpu.SemaphoreType.DMA((n,))],
        name=name)(*shards)


def _sibling_exchange(grads, *, name):
    n = len(grads)
    half = N_DEV // 2

    def body(*refs):
        ins, outs = refs[:n], refs[n:2 * n]
        send_sems, recv_sems = refs[2 * n:]
        x, y, c = _mesh_pos()
        copies = [pltpu.make_async_remote_copy(
            src_ref=ins[a].at[2 * j + (1 - c)], dst_ref=outs[a].at[j], send_sem=send_sems.at[a, j],
            recv_sem=recv_sems.at[a, j], device_id=(x, y, 1 - c), device_id_type=MESH)
            for a in range(n) for j in range(half)]
        for cp in copies:
            cp.start()
        for cp in copies:
            cp.wait()

    any_spec = pl.BlockSpec(memory_space=pl.ANY)
    return pl.pallas_call(
        body, in_specs=[any_spec] * n, out_specs=[any_spec] * n,
        out_shape=[jax.ShapeDtypeStruct((half,) + g.shape[1:], g.dtype) for g in grads],
        scratch_shapes=[pltpu.SemaphoreType.DMA((n, half)), pltpu.SemaphoreType.DMA((n, half))],
        name=name)(*grads)


def _pair_add(grad, recv, c_idx, *, name):
    half, r, cdim = recv.shape
    tr = _row_tile(r, 512)

    def body(c_ref, g_ref, r_ref, t_ref, tb_ref):
        t = g_ref[...] + r_ref[...]
        t_ref[...] = t
        tb_ref[...] = t.astype(BF16)

    grid_spec = pltpu.PrefetchScalarGridSpec(
        num_scalar_prefetch=1, grid=(half, r // tr),
        in_specs=[pl.BlockSpec((None, tr, cdim), lambda j, i, c_ref: (2 * j + c_ref[0], i, 0)),
                  pl.BlockSpec((None, tr, cdim), lambda j, i, c_ref: (j, i, 0))],
        out_specs=[pl.BlockSpec((None, tr, cdim), lambda j, i, c_ref: (j, i, 0))] * 2)
    return pl.pallas_call(
        body, grid_spec=grid_spec,
        out_shape=[jax.ShapeDtypeStruct(recv.shape, F32), jax.ShapeDtypeStruct(recv.shape, BF16)],
        compiler_params=_cparams(2), name=name)(c_idx, grad, recv)


def _chip_exchange(partials, *, name):
    n = len(partials)
    half = N_DEV // 2

    def body(*refs):
        ins, outs = refs[:n], refs[n:2 * n]
        send_sems, recv_sems, local_sems = refs[2 * n:]
        x, y, c = _mesh_pos()
        jme = 2 * x + y
        local = [pltpu.make_async_copy(ins[a].at[jme], outs[a].at[jme], local_sems.at[a]) for a in range(n)]
        for cp in local:
            cp.start()
        for j in range(half):
            @pl.when(jme != j)
            def _(j=j):
                for a in range(n):
                    pltpu.make_async_remote_copy(
                        src_ref=ins[a].at[j], dst_ref=outs[a].at[jme], send_sem=send_sems.at[a, j],
                        recv_sem=recv_sems.at[a, jme], device_id=(j // 2, j % 2, c), device_id_type=MESH).start()
        for j in range(half):
            @pl.when(jme != j)
            def _(j=j):
                for a in range(n):
                    arrive = pltpu.make_async_remote_copy(
                        src_ref=ins[a].at[j], dst_ref=outs[a].at[j], send_sem=send_sems.at[a, j],
                        recv_sem=recv_sems.at[a, j], device_id=(j // 2, j % 2, c), device_id_type=MESH)
                    arrive.wait_recv()
                    arrive.wait_send()
        for cp in local:
            cp.wait()

    any_spec = pl.BlockSpec(memory_space=pl.ANY)
    return pl.pallas_call(
        body, in_specs=[any_spec] * n, out_specs=[any_spec] * n,
        out_shape=[jax.ShapeDtypeStruct(p.shape, p.dtype) for p in partials],
        scratch_shapes=[pltpu.SemaphoreType.DMA((n, half)), pltpu.SemaphoreType.DMA((n, half)),
                        pltpu.SemaphoreType.DMA((n,))],
        name=name)(*partials)


def _adam_math(w, g, m, v):
    m = ADAM_B1 * m + (1.0 - ADAM_B1) * g
    v = ADAM_B2 * v + (1.0 - ADAM_B2) * (g * g)
    m_hat = m / (1.0 - ADAM_B1 ** ADAM_STEP)
    v_hat = v / (1.0 - ADAM_B2 ** ADAM_STEP)
    delta = -ADAM_LR * (m_hat / (jnp.sqrt(v_hat) + ADAM_EPS) + ADAM_WD * w)
    return delta, m, v


def _reduce_adam(own, recv, j_idx, w, m, v, *, name):
    half, r, cdim = recv.shape
    tr = _row_tile(r, 256)

    def body(j_ref, own_ref, recv_ref, w_ref, m_ref, v_ref, g_ref, d_ref, nm_ref, nv_ref):
        jme = j_ref[0]
        g = own_ref[...]
        for sl in range(half):
            g = g + jnp.where(sl == jme, 0.0, recv_ref[sl].astype(F32))
        delta, nm, nv = _adam_math(w_ref[...], g, m_ref[...], v_ref[...])
        g_ref[...] = g
        d_ref[...] = delta
        nm_ref[...] = nm
        nv_ref[...] = nv

    tile = pl.BlockSpec((tr, cdim), lambda i, j_ref: (i, 0))
    grid_spec = pltpu.PrefetchScalarGridSpec(
        num_scalar_prefetch=1, grid=(r // tr,),
        in_specs=[pl.BlockSpec((None, tr, cdim), lambda i, j_ref: (j_ref[0], i, 0)),
                  pl.BlockSpec((half, tr, cdim), lambda i, j_ref: (0, i, 0)), tile, tile, tile],
        out_specs=[tile] * 4)
    return pl.pallas_call(
        body, grid_spec=grid_spec, out_shape=[jax.ShapeDtypeStruct((r, cdim), F32)] * 4,
        compiler_params=_cparams(1), name=name)(j_idx, own, recv, w, m, v)


def _small_all_reduce(pack, *, name):
    r, cdim = pack.shape

    def body(p_ref, o_ref, land_ref, send_sems, recv_sems):
        x, y, c = _mesh_pos()
        me = 4 * x + 2 * y + c
        land_ref[me] = p_ref[...]
        for t in range(N_DEV):
            @pl.when(me != t)
            def _(t=t):
                pltpu.make_async_remote_copy(
                    src_ref=p_ref, dst_ref=land_ref.at[me], send_sem=send_sems.at[t], recv_sem=recv_sems.at[me],
                    device_id=(t // 4, (t // 2) % 2, t % 2), device_id_type=MESH).start()
        for t in range(N_DEV):
            @pl.when(me != t)
            def _(t=t):
                cp = pltpu.make_async_remote_copy(
                    src_ref=p_ref, dst_ref=land_ref.at[t], send_sem=send_sems.at[t], recv_sem=recv_sems.at[t],
                    device_id=(t // 4, (t // 2) % 2, t % 2), device_id_type=MESH)
                cp.wait_recv()
                cp.wait_send()
        total = land_ref[0]
        for t in range(1, N_DEV):
            total = total + land_ref[t]
        o_ref[...] = total

    vmem = pl.BlockSpec(memory_space=pltpu.VMEM)
    return pl.pallas_call(
        body, in_specs=[vmem], out_specs=vmem, out_shape=jax.ShapeDtypeStruct((r, cdim), F32),
        scratch_shapes=[pltpu.VMEM((N_DEV, r, cdim), F32), pltpu.SemaphoreType.DMA((N_DEV,)),
                        pltpu.SemaphoreType.DMA((N_DEV,))],
        name=name)(pack)


def _adam_small(w, g, m, v, *, name):
    def body(w_ref, g_ref, m_ref, v_ref, d_ref, nm_ref, nv_ref):
        d_ref[...], nm_ref[...], nv_ref[...] = _adam_math(w_ref[...], g_ref[...], m_ref[...], v_ref[...])

    vmem = pl.BlockSpec(memory_space=pltpu.VMEM)
    return pl.pallas_call(
        body, in_specs=[vmem] * 4, out_specs=[vmem] * 3, out_shape=[jax.ShapeDtypeStruct(w.shape, F32)] * 3,
        name=name)(w, g, m, v)


def _block_diag(w):
    h, a, b = w.shape
    eye = jnp.eye(h, dtype=w.dtype)
    return (eye[:, None, :, None] * w[:, :, None, :]).reshape(h * a, h * b)


def _diag_blocks(w, h):
    a = w.shape[0] // h
    return jnp.stack([w[i * a:(i + 1) * a, i * a:(i + 1) * a] for i in range(h)])


def _local_step(x, mem, target, p):
    g = {}
    w_rg, w_ig = _block_diag(p["w_rgate"]).astype(BF16), _block_diag(p["w_igate"]).astype(BF16)
    w_in = p["w_in"]
    w_in_lru, w_in_qkv = w_in[:, :2 * LRU_W], w_in[:, 2 * LRU_W:]

    gu1, f1 = _ffn_fwd(x, p["ffn1_w13"], p["ffn1_w2"], name="ffn1_fwd")
    h1 = _ln_fwd(x, f1, 0.5, p["ln1_g"], p["ln1_b"], name="ln1_fwd")
    xbg = _matmul(h1, w_in_lru, name="proj_lru")
    qkv = _matmul(h1, w_in_qkv, out_dtype=BF16, tn=1536, name="proj_qkv")
    xc, hseq, y_lru = _lru_fwd(xbg, p["conv_w"], p["conv_b"], w_rg, p["b_rgate"], w_ig, p["b_igate"],
                               p["lru_lambda"], name="lru_fwd")
    y_sb = _sb_fwd(qkv, name="sb_fwd")
    ymix = _mixnorm_fwd(y_lru, y_sb, p["g_lru"], p["g_sb"], name="mixnorm_fwd")
    mix = _matmul(ymix, p["w_out"], name="mix_out")
    h2 = _ln_fwd(h1, mix, 1.0, p["ln2_g"], p["ln2_b"], name="ln2_fwd")
    qm = _matmul(h2, p["mem_wq"], name="cross_q")
    kv = _matmul(mem, p["mem_wkv"], name="cross_kv")
    o_cross = _cross_fwd(qm, kv, name="cross_fwd")
    cross = _matmul(o_cross, p["mem_wo"], name="cross_out")
    h3 = _ln_fwd(h2, cross, 1.0, p["ln3_g"], p["ln3_b"], name="ln3_fwd")
    gu2, f2 = _ffn_fwd(h3, p["ffn2_w13"], p["ffn2_w2"], name="ffn2_fwd")

    dz4, g["ln4_g"], g["ln4_b"], loss = _ln_bwd(h3, f2, 0.5, p["ln4_g"], p["ln4_b"], None, target=target, name="ln4_loss_bwd")
    dh3, dgu2, act2 = _ffn_bwd(dz4, gu2, p["ffn2_w13"], p["ffn2_w2"], name="ffn2_bwd")
    g["ffn2_w13"] = _matmul(h3, dgu2, trans_a=True, tm=1024, tn=FF_CHUNK, tk=512, name="ffn2_dw13")
    g["ffn2_w2"] = _matmul(act2, dz4, trans_a=True, tm=FF_CHUNK, tn=1024, tk=512, scale=0.5, name="ffn2_dw2")
    dz3, g["ln3_g"], g["ln3_b"] = _ln_bwd(h2, cross, 1.0, p["ln3_g"], p["ln3_b"], dh3, name="ln3_bwd")
    g["mem_wo"] = _matmul(o_cross, dz3, trans_a=True, tm=1024, tn=1024, tk=512, name="cross_dwo")
    do_cross = _matmul(dz3, p["mem_wo"], trans_b=True, name="cross_do")
    dqm, dkv = _cross_bwd(qm, kv, do_cross, name="cross_bwd")
    g["mem_wq"] = _matmul(h2, dqm, trans_a=True, tm=1024, tn=1024, tk=512, name="cross_dwq")
    g["mem_wkv"] = _matmul(mem, dkv, trans_a=True, tm=1024, tn=MEM_HD, tk=256, name="cross_dwkv")
    dh2 = _matmul(dqm, p["mem_wq"], trans_b=True, res=dz3, res_coeff=ALPHA, name="cross_dh")
    dz2, g["ln2_g"], g["ln2_b"] = _ln_bwd(h1, mix, 1.0, p["ln2_g"], p["ln2_b"], dh2, name="ln2_bwd")
    g["w_out"] = _matmul(ymix, dz2, trans_a=True, tm=1024, tn=1024, tk=512, name="mix_dwout")
    dymix = _matmul(dz2, p["w_out"], trans_b=True, name="mix_dy")
    dy_lru, dy_sb, g["g_lru"], g["g_sb"] = _mixnorm_bwd(y_lru, y_sb, p["g_lru"], p["g_sb"], dymix, name="mixnorm_bwd")
    dq, dk, dv = _sb_bwd(qkv, y_sb, dy_sb, name="sb_bwd")
    dxc, dgate, dwr, dwi, g["b_rgate"], g["b_igate"], g["lru_lambda"] = _lru_bwd(
        xc, xbg, hseq, dy_lru, w_rg, p["b_rgate"], w_ig, p["b_igate"], p["lru_lambda"], name="lru_bwd")
    g["w_rgate"], g["w_igate"] = _diag_blocks(dwr, 8), _diag_blocks(dwi, 8)
    dxb, g["conv_w"], g["conv_b"] = _conv_bwd(dxc, xbg, p["conv_w"], name="conv_bwd")
    parts = [dxb, dgate, dq, dk, dv]
    g["w_in"] = jnp.concatenate(
        [_matmul(h1, part, trans_a=True, tm=1024, tn=512, tk=512, name=f"proj_dw{n}") for n, part in enumerate(parts)], axis=1)
    dh1 = _proj_bwd_dx(parts, w_in, dz2, name="proj_dh")
    dz1, g["ln1_g"], g["ln1_b"] = _ln_bwd(x, f1, 0.5, p["ln1_g"], p["ln1_b"], dh1, name="ln1_bwd")
    grad_x, dgu1, act1 = _ffn_bwd(dz1, gu1, p["ffn1_w13"], p["ffn1_w2"], name="ffn1_bwd")
    g["ffn1_w13"] = _matmul(x, dgu1, trans_a=True, tm=1024, tn=FF_CHUNK, tk=512, name="ffn1_dw13")
    g["ffn1_w2"] = _matmul(act1, dz1, trans_a=True, tm=FF_CHUNK, tn=1024, tk=512, scale=0.5, name="ffn1_dw2")
    return loss, grad_x, g


_WEIGHTS = ['ffn1_w13', 'ffn1_w2', 'ln1_g', 'ln1_b', 'w_in', 'conv_w', 'conv_b', 'w_rgate', 'b_rgate', 'w_igate',
            'b_igate', 'lru_lambda', 'g_lru', 'g_sb', 'w_out', 'ln2_g', 'ln2_b', 'mem_wq', 'mem_wkv', 'mem_wo',
            'ln3_g', 'ln3_b', 'ffn2_w13', 'ffn2_w2', 'ln4_g', 'ln4_b']
_SHARDED = ['ffn1_w13', 'ffn1_w2', 'w_in', 'w_out', 'mem_wq', 'mem_wkv', 'mem_wo', 'ffn2_w13', 'ffn2_w2']
_COL_SHARDED = ('ffn1_w13', 'ffn2_w13', 'w_in', 'mem_wkv', 'conv_w')
_SMALL = ['ln1_g', 'ln1_b', 'ln2_g', 'ln2_b', 'ln3_g', 'ln3_b', 'ln4_g', 'ln4_b', 'conv_b', 'b_rgate', 'b_igate',
          'lru_lambda', 'g_lru', 'g_sb', 'w_rgate', 'w_igate']


def _pack_small(d, conv_w_full=None):
    rows = [d[n].reshape(-1) for n in _SMALL]
    if conv_w_full is not None:
        rows.append(conv_w_full.reshape(-1))
    flat = jnp.concatenate(rows)
    pad = (-flat.shape[0]) % (8 * D_MODEL)
    return jnp.pad(flat, (0, pad)).reshape(-1, D_MODEL)


def _unpack_small(pack, like, with_conv):
    flat = pack.reshape(-1)
    out, off = {}, 0
    for n in _SMALL:
        size = math.prod(like[n].shape)
        out[n] = flat[off:off + size].reshape(like[n].shape)
        off += size
    conv = flat[off:off + CONV_W * LRU_W].reshape(CONV_W, LRU_W) if with_conv else None
    return out, conv


def _gathered_weights(w):
    names = _SHARDED
    shards = [w[n][0].astype(BF16) for n in names] + [w['conv_w'][0]]
    full = dict(zip(names + ['conv_w'], _all_gather(shards, name="gather_weights")))
    p = {}
    for n in ('ffn1_w13', 'ffn2_w13', 'mem_wkv'):
        p[n] = full[n]
    for n in ('ffn1_w2', 'ffn2_w2', 'w_out', 'mem_wq', 'mem_wo'):
        p[n] = full[n].reshape(-1, full[n].shape[-1])
    p['w_in'] = full['w_in'].transpose(1, 0, 2).reshape(D_MODEL, -1)
    p['conv_w'] = full['conv_w'].transpose(1, 0, 2).reshape(CONV_W, LRU_W)
    for n in _WEIGHTS:
        if n not in p:
            p[n] = w[n][0] if w[n].ndim == 4 else w[n]
    return p


def kernel(x, mem, ffn1_w13, ffn1_w2, ln1_g, ln1_b, w_in, conv_w, conv_b, w_rgate, b_rgate, w_igate, b_igate, lru_lambda, g_lru, g_sb, w_out, ln2_g, ln2_b, mem_wq, mem_wkv, mem_wo, ln3_g, ln3_b, ffn2_w13, ffn2_w2, ln4_g, ln4_b, loss_target, m_ffn1_w13, m_ffn1_w2, m_ln1_g, m_ln1_b, m_w_in, m_conv_w, m_conv_b, m_w_rgate, m_b_rgate, m_w_igate, m_b_igate, m_lru_lambda, m_g_lru, m_g_sb, m_w_out, m_ln2_g, m_ln2_b, m_mem_wq, m_mem_wkv, m_mem_wo, m_ln3_g, m_ln3_b, m_ffn2_w13, m_ffn2_w2, m_ln4_g, m_ln4_b, v_ffn1_w13, v_ffn1_w2, v_ln1_g, v_ln1_b, v_w_in, v_conv_w, v_conv_b, v_w_rgate, v_b_rgate, v_w_igate, v_b_igate, v_lru_lambda, v_g_lru, v_g_sb, v_w_out, v_ln2_g, v_ln2_b, v_mem_wq, v_mem_wkv, v_mem_wo, v_ln3_g, v_ln3_b, v_ffn2_w13, v_ffn2_w2, v_ln4_g, v_ln4_b):
    args = locals()
    w = {n: args[n] for n in _WEIGHTS}
    mom = {n: args["m_" + n] for n in _WEIGHTS}
    var = {n: args["v_" + n] for n in _WEIGHTS}
    ix, iy, ic = lax.axis_index("x"), lax.axis_index("y"), lax.axis_index("c")

    p = _gathered_weights(w)
    loss_local, grad_x, g = _local_step(x[0], mem[0], loss_target[0], p)
    loss = lax.psum(loss_local[0, 0], ("x", "y", "c"))

    g["w_in"] = g["w_in"].reshape(D_MODEL, N_DEV, -1).transpose(1, 0, 2)
    blocks = [g[n].reshape((N_DEV, -1, g[n].shape[-1])) for n in _SHARDED]
    from_sibling = _sibling_exchange(blocks, name="reduce_sibling")
    c_idx = jnp.reshape(ic, (1,)).astype(jnp.int32)
    j_idx = jnp.reshape(2 * ix + iy, (1,)).astype(jnp.int32)
    sums = [_pair_add(b, r, c_idx, name=f"pair_add_{n}") for n, b, r in zip(_SHARDED, blocks, from_sibling)]
    from_chips = _chip_exchange([tb for _, tb in sums], name="reduce_chips")
    grads, delta, new_m, new_v = {}, {}, {}, {}
    for n, (t, _), r in zip(_SHARDED, sums, from_chips):
        w2d = w[n][0]
        grads[n], delta[n], new_m[n], new_v[n] = (
            o.reshape(w[n].shape) for o in _reduce_adam(t, r, j_idx, w2d, mom[n][0], var[n][0], name=f"adam_{n}"))

    total = _small_all_reduce(_pack_small(g, conv_w_full=g["conv_w"]), name="reduce_small")
    gsmall, gconv = _unpack_small(total, w, with_conv=True)
    me = 4 * ix + 2 * iy + ic
    gsmall["conv_w"] = lax.dynamic_slice_in_dim(gconv, me * (LRU_W // N_DEV), LRU_W // N_DEV, axis=1)[None]
    small_names = _SMALL + ["conv_w"]
    pk = lambda d: _pack_small({n: d[n] for n in _SMALL}, conv_w_full=jnp.pad(d["conv_w"].reshape(-1), (0, CONV_W * LRU_W - CONV_W * LRU_W // N_DEV)))
    d_s, m_s, v_s = _adam_small(pk(w), pk(gsmall), pk(mom), pk(var), name="adam_small")
    for src, dst in ((d_s, delta), (m_s, new_m), (v_s, new_v)):
        vals, conv = _unpack_small(src, w, with_conv=True)
        dst.update(vals)
        dst["conv_w"] = conv.reshape(-1)[:CONV_W * LRU_W // N_DEV].reshape(w["conv_w"].shape)
    grads.update({n: gsmall[n].reshape(w[n].shape) for n in small_names})

    return (loss, grad_x[None], *[grads[n] for n in _WEIGHTS], *[delta[n] for n in _WEIGHTS],
            *[new_m[n] for n in _WEIGHTS], *[new_v[n] for n in _WEIGHTS])
```

```python
import functools
import math

import jax
import jax.numpy as jnp
from jax import lax
from jax.experimental import pallas as pl
from jax.experimental.pallas import tpu as pltpu

F32, BF16 = jnp.float32, jnp.bfloat16

D_MODEL = 1024
LRU_W = 512
SB_W = 512
SB_PAIR = 128
SB_HEAD = 64
SB_ROWS = 128
SB_SKIP = -110.0
D_FF = 2688
FF_CHUNK = 672
N_DEV = 8
MEM_HEADS = 4
MEM_HD = 256
CONV_W = 4
ALPHA = 2.0 ** 0.25
LN_EPS = 1e-5
RMS_EPS = 1e-6
LRU_C = 8.0
ADAM_LR, ADAM_B1, ADAM_B2, ADAM_EPS, ADAM_WD, ADAM_STEP = 0.001, 0.9, 0.999, 1e-08, 0.01, 10

NN = (((1,), (0,)), ((), ()))
NT = (((1,), (1,)), ((), ()))
MESH = pl.DeviceIdType.MESH
VMEM_LIMIT_MB = 56


def _dot(a, b, dn=NN):
    return lax.dot_general(a, b, dn, preferred_element_type=F32)


def _cparams(n_axes, vmem_mb=None):
    kw = dict(dimension_semantics=("arbitrary",) * n_axes)
    if vmem_mb is not None:
        kw["vmem_limit_bytes"] = vmem_mb << 20
    return pltpu.CompilerParams(**kw)


def _row_tile(rows, want):
    if rows <= want:
        return rows
    t = want - want % 8
    while rows % t:
        t -= 8
    return t


def _matmul(a, b, *, name, out_dtype=F32, trans_a=False, trans_b=False, tm=512, tn=1024, tk=1024,
            scale=1.0, res=None, res_coeff=1.0):
    a_chunked, b_chunked = a.ndim == 3, b.ndim == 3
    nc = a.shape[0] if a_chunked else (b.shape[0] if b_chunked else 1)
    a2, b2 = a.shape[-2:], b.shape[-2:]
    (kdim, m) = a2 if trans_a else a2[::-1]
    n = b2[0] if trans_b else b2[1]
    tm, tn, tk = min(tm, m), min(tn, n), min(tk, kdim)
    assert m % tm == 0 and n % tn == 0 and kdim % tk == 0, (name, m, n, kdim)
    nk = kdim // tk

    def a_idx(c, i, j, k):
        idx = (k, i) if trans_a else (i, k)
        return (c,) + idx if a_chunked else idx

    def b_idx(c, i, j, k):
        idx = (j, k) if trans_b else (k, j)
        return (c,) + idx if b_chunked else idx

    a_blk = (tk, tm) if trans_a else (tm, tk)
    b_blk = (tn, tk) if trans_b else (tk, tn)
    in_specs = [pl.BlockSpec(((None,) + a_blk) if a_chunked else a_blk, a_idx),
                pl.BlockSpec(((None,) + b_blk) if b_chunked else b_blk, b_idx)]
    args = [a, b]
    if res is not None:
        in_specs.append(pl.BlockSpec((None, tm, tn), lambda c, i, j, k: (c, i, j)))
        args.append(res.reshape((nc, m, n)))

    def body(*refs):
        if res is not None:
            a_ref, b_ref, r_ref, o_ref, acc_ref = refs
        else:
            a_ref, b_ref, o_ref, acc_ref = refs
        k = pl.program_id(3)

        @pl.when(k == 0)
        def _():
            acc_ref[...] = jnp.zeros_like(acc_ref)

        av = a_ref[...]
        if trans_a:
            av = av.astype(F32).T
        acc_ref[...] += _dot(av.astype(BF16), b_ref[...].astype(BF16), NT if trans_b else NN)

        @pl.when(k == nk - 1)
        def _():
            out = acc_ref[...]
            if scale != 1.0:
                out = out * scale
            if res is not None:
                out = out + res_coeff * r_ref[...]
            o_ref[...] = out.astype(out_dtype)

    out = pl.pallas_call(
        body, grid=(nc, m // tm, n // tn, nk), in_specs=in_specs,
        out_specs=pl.BlockSpec((None, tm, tn), lambda c, i, j, k: (c, i, j)),
        out_shape=jax.ShapeDtypeStruct((nc, m, n), out_dtype),
        scratch_shapes=[pltpu.VMEM((tm, tn), F32)],
        compiler_params=_cparams(4, VMEM_LIMIT_MB), name=name)(*args)
    return out if (a_chunked or b_chunked) else out[0]


def _ln_math(z, g, b):
    mu = jnp.mean(z, axis=-1, keepdims=True)
    zc = z - mu
    var = jnp.mean(zc * zc, axis=-1, keepdims=True)
    return zc * lax.rsqrt(var + LN_EPS) * g + b


def _row_spec(tm, d):
    return pl.BlockSpec((tm, d), lambda i: (i, 0))


def _par_spec(d, rows=1):
    return pl.BlockSpec((rows, d), lambda i: (0, 0))


def _ln_fwd(x, f, fscale, g, b, *, name):
    s, d = x.shape
    tm = min(512, s)

    def body(x_ref, f_ref, g_ref, b_ref, o_ref):
        z = ALPHA * x_ref[...] + fscale * f_ref[...]
        o_ref[...] = _ln_math(z, g_ref[...], b_ref[...])

    return pl.pallas_call(
        body, grid=(s // tm,), in_specs=[_row_spec(tm, d), _row_spec(tm, d), _par_spec(d), _par_spec(d)],
        out_specs=_row_spec(tm, d), out_shape=jax.ShapeDtypeStruct((s, d), F32),
        compiler_params=_cparams(1), name=name)(x, f, g, b)


def _ln_bwd(x, f, fscale, g, b, dy, *, name, target=None):
    s, d = x.shape
    tm = min(512, s)
    with_loss = target is not None

    def body(*refs):
        if with_loss:
            x_ref, f_ref, g_ref, b_ref, t_ref, dz_ref, dg_ref, db_ref, loss_ref = refs
        else:
            x_ref, f_ref, g_ref, b_ref, dy_ref, dz_ref, dg_ref, db_ref = refs
        i = pl.program_id(0)

        @pl.when(i == 0)
        def _():
            dg_ref[...] = jnp.zeros_like(dg_ref)
            db_ref[...] = jnp.zeros_like(db_ref)
            if with_loss:
                loss_ref[...] = jnp.zeros_like(loss_ref)

        z = ALPHA * x_ref[...] + fscale * f_ref[...]
        y, vjp = jax.vjp(_ln_math, z, g_ref[...], b_ref[...])
        if with_loss:
            err = y - t_ref[...]
            sq = jnp.sum(err * err, axis=1, keepdims=True)
            loss_ref[...] += jnp.sum(sq, axis=0, keepdims=True) * (0.5 / d)
            cot = err * (1.0 / d)
        else:
            cot = dy_ref[...]
        dz, dg, db = vjp(cot)
        dz_ref[...] = dz
        dg_ref[...] += dg
        db_ref[...] += db

    in_specs = [_row_spec(tm, d), _row_spec(tm, d), _par_spec(d), _par_spec(d), _row_spec(tm, d)]
    out_specs = [_row_spec(tm, d), _par_spec(d), _par_spec(d)]
    out_shape = [jax.ShapeDtypeStruct((s, d), F32), jax.ShapeDtypeStruct((1, d), F32), jax.ShapeDtypeStruct((1, d), F32)]
    if with_loss:
        out_specs.append(pl.BlockSpec((1, 1), lambda i: (0, 0)))
        out_shape.append(jax.ShapeDtypeStruct((1, 1), F32))
    return pl.pallas_call(
        body, grid=(s // tm,), in_specs=in_specs, out_specs=out_specs, out_shape=out_shape,
        compiler_params=_cparams(1), name=name)(x, f, g, b, target if with_loss else dy)


def _load_weights_once(pairs, sem):
    @pl.when(pl.program_id(0) == 0)
    def _():
        copies = [pltpu.make_async_copy(src, dst, sem.at[n]) for n, (src, dst) in enumerate(pairs)]
        for c in copies:
            c.start()
        for c in copies:
            c.wait()


def _ffn_fwd(h, w13c, w2, *, name):
    s, d = h.shape
    tm = min(256, s)
    half = N_DEV // 2

    def body(h_ref, w13_hbm, w2_hbm, gu_ref, f_ref, w13_v, w2_v, sem):
        _load_weights_once([(w13_hbm, w13_v), (w2_hbm, w2_v)], sem)
        hb = h_ref[...].astype(BF16)
        acc = jnp.zeros((tm, d), F32)
        for k in range(half):
            g = _dot(hb, w13_v[k])
            u = _dot(hb, w13_v[k + half])
            gu_ref[k] = g.astype(BF16)
            gu_ref[k + half] = u.astype(BF16)
            a = g * jax.nn.sigmoid(g) * u
            acc = acc + _dot(a.astype(BF16), w2_v[pl.ds(k * FF_CHUNK, FF_CHUNK), :])
        f_ref[...] = acc

    any_spec = pl.BlockSpec(memory_space=pl.ANY)
    return pl.pallas_call(
        body, grid=(s // tm,), in_specs=[_row_spec(tm, d), any_spec, any_spec],
        out_specs=[pl.BlockSpec((N_DEV, tm, FF_CHUNK), lambda i: (0, i, 0)), _row_spec(tm, d)],
        out_shape=[jax.ShapeDtypeStruct((N_DEV, s, FF_CHUNK), BF16), jax.ShapeDtypeStruct((s, d), F32)],
        scratch_shapes=[pltpu.VMEM(w13c.shape, BF16), pltpu.VMEM(w2.shape, BF16), pltpu.SemaphoreType.DMA((2,))],
        compiler_params=_cparams(1, VMEM_LIMIT_MB), name=name)(h, w13c, w2)


def _ffn_bwd(dz, gu, w13c, w2, *, name):
    s, d = dz.shape
    tm = min(256, s)
    half = N_DEV // 2

    def body(dz_ref, gu_ref, w13_hbm, w2_hbm, dh_ref, dgu_ref, act_ref, w13_v, w2_v, sem):
        _load_weights_once([(w13_hbm, w13_v), (w2_hbm, w2_v)], sem)
        dzv = dz_ref[...]
        dfb = (0.5 * dzv).astype(BF16)
        acc = ALPHA * dzv
        for k in range(half):
            g = gu_ref[k].astype(F32)
            u = gu_ref[k + half].astype(F32)
            da = _dot(dfb, w2_v[pl.ds(k * FF_CHUNK, FF_CHUNK), :], NT)
            sg = jax.nn.sigmoid(g)
            silu = g * sg
            dg = (da * u * (sg * (1.0 + g * (1.0 - sg)))).astype(BF16)
            du = (da * silu).astype(BF16)
            act_ref[k] = (silu * u).astype(BF16)
            dgu_ref[k] = dg
            dgu_ref[k + half] = du
            acc = acc + _dot(dg, w13_v[k], NT) + _dot(du, w13_v[k + half], NT)
        dh_ref[...] = acc

    any_spec = pl.BlockSpec(memory_space=pl.ANY)
    return pl.pallas_call(
        body, grid=(s // tm,),
        in_specs=[_row_spec(tm, d), pl.BlockSpec((N_DEV, tm, FF_CHUNK), lambda i: (0, i, 0)), any_spec, any_spec],
        out_specs=[_row_spec(tm, d), pl.BlockSpec((N_DEV, tm, FF_CHUNK), lambda i: (0, i, 0)),
                   pl.BlockSpec((half, tm, FF_CHUNK), lambda i: (0, i, 0))],
        out_shape=[jax.ShapeDtypeStruct((s, d), F32), jax.ShapeDtypeStruct((N_DEV, s, FF_CHUNK), BF16),
                   jax.ShapeDtypeStruct((half, s, FF_CHUNK), BF16)],
        scratch_shapes=[pltpu.VMEM(w13c.shape, BF16), pltpu.VMEM(w2.shape, BF16), pltpu.SemaphoreType.DMA((2,))],
        compiler_params=_cparams(1, VMEM_LIMIT_MB), name=name)(dz, gu, w13c, w2)


def _shift_down(x, prev8, d):
    t, c = x.shape
    row = lax.broadcasted_iota(jnp.int32, (t, c), 0)
    xr = pltpu.roll(x, d, 0)
    pr = pltpu.roll(prev8, d, 0)
    if t > 8:
        pr = jnp.concatenate([pr, jnp.zeros((t - 8, c), x.dtype)], axis=0)
    return jnp.where(row < d, pr, xr)


def _shift_up(x, next8, d):
    t, c = x.shape
    row = lax.broadcasted_iota(jnp.int32, (t, c), 0)
    xr = pltpu.roll(x, t - d, 0)
    nr = pltpu.roll(next8, 8 - d, 0)
    if t > 8:
        nr = jnp.concatenate([jnp.zeros((t - 8, c), x.dtype), nr], axis=0)
    return jnp.where(row >= t - d, nr, xr)


def _scan_fwd(a, u):
    t = a.shape[0]
    row = lax.broadcasted_iota(jnp.int32, a.shape, 0)
    d = 1
    while d < t:
        a_s, u_s = pltpu.roll(a, d, 0), pltpu.roll(u, d, 0)
        m = row >= d
        u = jnp.where(m, u + a * u_s, u)
        a = jnp.where(m, a * a_s, a)
        d *= 2
    return a, u


def _scan_bwd(b, x):
    t = b.shape[0]
    row = lax.broadcasted_iota(jnp.int32, b.shape, 0)
    d = 1
    while d < t:
        b_s, x_s = pltpu.roll(b, t - d, 0), pltpu.roll(x, t - d, 0)
        m = row < t - d
        x = jnp.where(m, x + b * x_s, x)
        b = jnp.where(m, b * b_s, b)
        d *= 2
    return b, x


def _lru_elem(xc, pr, pi, b_r, b_i, lam):
    r = jax.nn.sigmoid(pr + b_r)
    ig = jax.nn.sigmoid(pi + b_i)
    softplus_neg_lam = jnp.maximum(-lam, 0.0) + jnp.log1p(jnp.exp(-jnp.abs(lam)))
    log_a = (-LRU_C) * r * softplus_neg_lam
    a = jnp.exp(log_a)
    one_minus_a2 = -jnp.tanh(log_a) * (jnp.exp(2.0 * log_a) + 1.0)
    u = jnp.sqrt(one_minus_a2) * (ig * xc)
    return a, u


def _gelu_tanh(x):
    return 0.5 * x * (1.0 + jnp.tanh(math.sqrt(2.0 / math.pi) * (x + 0.044715 * (x * x * x))))


def _conv_fwd(xb, prev8, w, b):
    out = b + w[3:4, :] * xb
    for d in range(1, CONV_W):
        out = out + w[3 - d:4 - d, :] * _shift_down(xb, prev8, d)
    return out


def _lru_fwd(xbg, conv_w, conv_b, w_rg, b_rg, w_ig, b_ig, lam, *, name):
    s = xbg.shape[0]
    c = LRU_W
    t = min(512, s)
    t8 = t // 8

    def body(xb_ref, gate_ref, prev_ref, cw_ref, cb_ref, wr_ref, br_ref, wi_ref, bi_ref, lam_ref,
             xc_ref, h_ref, y_ref, carry_ref):
        i = pl.program_id(0)

        @pl.when(i == 0)
        def _():
            carry_ref[...] = jnp.zeros_like(carry_ref)

        prev8 = jnp.where(i > 0, prev_ref[...], 0.0)
        xc = _conv_fwd(xb_ref[...], prev8, cw_ref[...], cb_ref[...])
        xcb = xc.astype(BF16)
        a, u = _lru_elem(xc, _dot(xcb, wr_ref[...]), _dot(xcb, wi_ref[...]), br_ref[...], bi_ref[...], lam_ref[...])
        a_cum, h0 = _scan_fwd(a, u)
        h = h0 + a_cum * carry_ref[0:1, :]
        carry_ref[...] = jnp.broadcast_to(h[t - 1:t, :], carry_ref.shape)
        xc_ref[...] = xc
        h_ref[...] = h
        y_ref[...] = h * _gelu_tanh(gate_ref[...])

    tile = lambda col: pl.BlockSpec((t, c), lambda i: (i, col))
    par = lambda rows: pl.BlockSpec((rows, c), lambda i: (0, 0))
    out_spec = pl.BlockSpec((t, c), lambda i: (i, 0))
    return pl.pallas_call(
        body, grid=(s // t,),
        in_specs=[tile(0), tile(1), pl.BlockSpec((8, c), lambda i: (jnp.maximum(i * t8 - 1, 0), 0)),
                  par(CONV_W), par(1), pl.BlockSpec((c, c), lambda i: (0, 0)), par(1),
                  pl.BlockSpec((c, c), lambda i: (0, 0)), par(1), par(1)],
        out_specs=[out_spec, out_spec, out_spec],
        out_shape=[jax.ShapeDtypeStruct((s, c), F32)] * 3,
        scratch_shapes=[pltpu.VMEM((8, c), F32)],
        compiler_params=_cparams(1, VMEM_LIMIT_MB), name=name)(
            xbg, xbg, xbg, conv_w, conv_b, w_rg, b_rg, w_ig, b_ig, lam)


def _lru_bwd(xc, xbg, hseq, dy, w_rg, b_rg, w_ig, b_ig, lam, *, name):
    s, c = xc.shape
    t = min(512, s)
    t8 = t // 8
    nt = s // t

    def body(xc_ref, gate_ref, h_ref, hprev_ref, dy_ref, wr_ref, br_ref, wi_ref, bi_ref, lam_ref,
             dxc_ref, dgate_ref, dwr_ref, dwi_ref, dbr_ref, dbi_ref, dlam_ref, a8_ref, l8_ref):
        i = pl.program_id(0)

        @pl.when(i == 0)
        def _():
            for r in (dwr_ref, dwi_ref, dbr_ref, dbi_ref, dlam_ref, a8_ref, l8_ref):
                r[...] = jnp.zeros_like(r)

        xc_v = xc_ref[...]
        xcb = xc_v.astype(BF16)
        (a, _), vjp = jax.vjp(_lru_elem, xc_v, _dot(xcb, wr_ref[...]), _dot(xcb, wi_ref[...]),
                              br_ref[...], bi_ref[...], lam_ref[...])
        gl, vjp_gelu = jax.vjp(_gelu_tanh, gate_ref[...])
        dyv, hs = dy_ref[...], h_ref[...]
        (dgate,) = vjp_gelu(dyv * hs)
        b_next = _shift_up(a, a8_ref[...], 1)
        b_cum, l0 = _scan_bwd(b_next, dyv * gl)
        lt = l0 + b_cum * l8_ref[0:1, :]
        a8_ref[...] = a[0:8, :]
        l8_ref[...] = lt[0:8, :]
        hprev8 = jnp.where(i < nt - 1, hprev_ref[...], 0.0)
        da = lt * _shift_down(hs, hprev8, 1)
        dxc_e, dpr, dpi, dbr, dbi, dlam = vjp((da, lt))
        dprb, dpib = dpr.astype(BF16), dpi.astype(BF16)
        dxc_ref[...] = dxc_e + _dot(dprb, wr_ref[...], NT) + _dot(dpib, wi_ref[...], NT)
        dgate_ref[...] = dgate
        xct = xc_v.T.astype(BF16)
        dwr_ref[...] += _dot(xct, dprb)
        dwi_ref[...] += _dot(xct, dpib)
        dbr_ref[...] += dbr
        dbi_ref[...] += dbi
        dlam_ref[...] += dlam

    rev = lambda col: pl.BlockSpec((t, c), lambda i: (nt - 1 - i, col))
    par = pl.BlockSpec((1, c), lambda i: (0, 0))
    sq = pl.BlockSpec((c, c), lambda i: (0, 0))
    return pl.pallas_call(
        body, grid=(nt,),
        in_specs=[rev(0), rev(1), rev(0), pl.BlockSpec((8, c), lambda i: (jnp.maximum((nt - 1 - i) * t8 - 1, 0), 0)),
                  rev(0), sq, par, sq, par, par],
        out_specs=[rev(0), rev(0), sq, sq, par, par, par],
        out_shape=[jax.ShapeDtypeStruct((s, c), F32)] * 2 + [jax.ShapeDtypeStruct((c, c), F32)] * 2
                  + [jax.ShapeDtypeStruct((1, c), F32)] * 3,
        scratch_shapes=[pltpu.VMEM((8, c), F32), pltpu.VMEM((8, c), F32)],
        compiler_params=_cparams(1, VMEM_LIMIT_MB), name=name)(
            xc, xbg, hseq, hseq, dy, w_rg, b_rg, w_ig, b_ig, lam)


def _conv_bwd(dxc, xbg, conv_w, *, name):
    s, c = dxc.shape
    t = min(512, s)
    t8 = t // 8
    nt = s // t

    def body(dxc_ref, next_ref, xb_ref, prev_ref, w_ref, dxb_ref, dw_ref, db_ref):
        i = pl.program_id(0)

        @pl.when(i == 0)
        def _():
            dw_ref[...] = jnp.zeros_like(dw_ref)
            db_ref[...] = jnp.zeros_like(db_ref)

        g = dxc_ref[...]
        xb = xb_ref[...]
        w = w_ref[...]
        next8 = jnp.where(i < nt - 1, next_ref[...], 0.0)
        prev8 = jnp.where(i > 0, prev_ref[...], 0.0)
        dxb = w[3:4, :] * g
        dw_ref[3:4, :] += jnp.sum(g * xb, axis=0, keepdims=True)
        for d in range(1, CONV_W):
            dxb = dxb + w[3 - d:4 - d, :] * _shift_up(g, next8, d)
            dw_ref[3 - d:4 - d, :] += jnp.sum(g * _shift_down(xb, prev8, d), axis=0, keepdims=True)
        dxb_ref[...] = dxb
        db_ref[...] += jnp.sum(g, axis=0, keepdims=True)

    tile = pl.BlockSpec((t, c), lambda i: (i, 0))
    return pl.pallas_call(
        body, grid=(nt,),
        in_specs=[tile, pl.BlockSpec((8, c), lambda i: (jnp.minimum((i + 1) * t8, s // 8 - 1), 0)),
                  tile, pl.BlockSpec((8, c), lambda i: (jnp.maximum(i * t8 - 1, 0), 0)),
                  pl.BlockSpec((CONV_W, c), lambda i: (0, 0))],
        out_specs=[tile, pl.BlockSpec((CONV_W, c), lambda i: (0, 0)), pl.BlockSpec((1, c), lambda i: (0, 0))],
        out_shape=[jax.ShapeDtypeStruct((s, c), F32), jax.ShapeDtypeStruct((CONV_W, c), F32),
                   jax.ShapeDtypeStruct((1, c), F32)],
        compiler_params=_cparams(1), name=name)(dxc, dxc, xbg, xbg, conv_w)


def _hi_lo_dot(x, tri):
    hi = x.astype(BF16)
    lo = (x - hi.astype(F32)).astype(BF16)
    return _dot(hi, tri) + _dot(lo, tri)


def _tri(tk, inclusive):
    r = lax.broadcasted_iota(jnp.int32, (tk, tk), 0)
    c = lax.broadcasted_iota(jnp.int32, (tk, tk), 1)
    return ((r >= c) if inclusive else (r > c)).astype(BF16)


def _split_heads(x):
    first = lax.broadcasted_iota(jnp.int32, x.shape, 1) < SB_HEAD
    zero = jnp.zeros_like(x)
    return first, (jnp.where(first, x, zero), jnp.where(first, zero, x))


def _sb_softplus_terms(z):
    ls = jnp.minimum(z, 0.0) - jnp.log(1.0 + jnp.exp(-jnp.abs(z)))
    return ls, ls - z


def _sb_alive(runs):
    m = runs[0]
    for r in runs[1:]:
        m = jnp.maximum(m, r)
    return jnp.max(m) > SB_SKIP


def _sb_fwd(q, kv, *, name):
    s = q.shape[0]
    tb = min(256, s)
    rc = min(SB_ROWS, tb)
    npair = SB_W // SB_PAIR
    chains = [(h, c) for h in range(2) for c in range(tb // rc)]

    def body(q_ref, k_ref, v_ref, o_ref):
        i = pl.program_id(1)
        first, qs = _split_heads(q_ref[...])
        tri = _tri(tb, False)
        causal = lax.broadcasted_iota(jnp.int32, (tb, tb), 1) < lax.broadcasted_iota(jnp.int32, (tb, tb), 0)
        causal_c = [causal[c * rc:(c + 1) * rc, :] for c in range(tb // rc)]

        def group(j_lo, width, carry, diag_last):
            k0 = pl.multiple_of(j_lo * tb, tb)
            kb = k_ref[pl.ds(k0, width * tb), :]
            vb = v_ref[pl.ds(k0, width * tb), :]
            zw = [_dot(qs[h][c * rc:(c + 1) * rc, :], kb, NT) for h, c in chains]
            cols = [slice(t * tb, (t + 1) * tb) for t in range(width)]
            terms = [[_sb_softplus_terms(z[:, cl]) for cl in cols] for z in zw]
            lss = [[ls for ls, _ in tc] for tc in terms]
            ns = [[n for _, n in tc] for tc in terms]
            if diag_last:
                ns = [nc[:-1] + [jnp.where(causal_c[c], nc[-1], 0.0)] for nc, (h, c) in zip(ns, chains)]
            sufs = [[_hi_lo_dot(n, tri) for n in nc] for nc in ns]
            out = []
            for (h, c), lc, nc, sc, (run, acc) in zip(chains, lss, ns, sufs, carry):
                ws = [None] * width
                for t in reversed(range(width)):
                    w = jnp.exp(lc[t] + (sc[t] + run))
                    if diag_last and t == width - 1:
                        w = jnp.where(causal_c[c], w, 0.0)
                    ws[t] = w.astype(BF16)
                    run = run + jnp.sum(nc[t], axis=1, keepdims=True)
                out.append((run, acc + _dot(ws[0] if width == 1 else jnp.concatenate(ws, axis=1), vb)))
            return tuple(out)

        zero = (jnp.zeros((rc, 1), F32), jnp.zeros((rc, SB_PAIR), F32))
        carry = lax.cond(i == 0, lambda c: group(0, 1, c, True), lambda c: group(i - 1, 2, c, True),
                         (zero,) * len(chains))

        def step(st):
            j, _, c = st
            c = group(j, 1, c, False)
            return j - 1, _sb_alive([run for run, _ in c]), c

        _, _, carry = lax.while_loop(lambda st: jnp.logical_and(st[0] >= 0, st[1]), step,
                                     (i - 2, _sb_alive([run for run, _ in carry]), carry))
        accs = [jnp.concatenate([carry[n][1] for n, (h, c) in enumerate(chains) if h == hh], axis=0) for hh in range(2)]
        o_ref[...] = jnp.where(first, accs[0], accs[1])

    return pl.pallas_call(
        body, grid=(npair, s // tb),
        in_specs=[pl.BlockSpec((tb, SB_PAIR), lambda p, i: (i, p)),
                  pl.BlockSpec((s, SB_PAIR), lambda p, i: (0, p)),
                  pl.BlockSpec((s, SB_PAIR), lambda p, i: (0, npair + p))],
        out_specs=pl.BlockSpec((tb, SB_PAIR), lambda p, i: (i, p)),
        out_shape=jax.ShapeDtypeStruct((s, SB_W), F32),
        compiler_params=_cparams(2, VMEM_LIMIT_MB), name=name)(q, kv, kv)


def _sb_bwd(q, kv, o, do, *, name):
    s = q.shape[0]
    tb = min(256, s)
    rc = min(SB_ROWS, tb)
    npair = SB_W // SB_PAIR
    chains = [(h, c) for h in range(2) for c in range(tb // rc)]

    def body(q_ref, k_ref, v_ref, o_ref, do_ref, dq_ref, dk_ref, dv_ref):
        i = pl.program_id(1)

        @pl.when(i == 0)
        def _():
            dk_ref[...] = jnp.zeros_like(dk_ref)
            dv_ref[...] = jnp.zeros_like(dv_ref)

        first, qs = _split_heads(q_ref[...])
        dob = do_ref[...].astype(BF16)
        _, dos = _split_heads(dob)
        prod = dob.astype(F32) * o_ref[...]
        deltas = (jnp.sum(jnp.where(first, prod, 0.0), axis=1, keepdims=True),
                  jnp.sum(jnp.where(first, 0.0, prod), axis=1, keepdims=True))
        tri_x, tri_i = _tri(tb, False), _tri(tb, True)
        causal = lax.broadcasted_iota(jnp.int32, (tb, tb), 1) < lax.broadcasted_iota(jnp.int32, (tb, tb), 0)
        causal_c = [causal[c * rc:(c + 1) * rc, :] for c in range(tb // rc)]
        rows = [slice(c * rc, (c + 1) * rc) for c in range(tb // rc)]

        def group(j_lo, width, carry, diag_last):
            k0 = pl.multiple_of(j_lo * tb, tb)
            kb = k_ref[pl.ds(k0, width * tb), :]
            vb = v_ref[pl.ds(k0, width * tb), :]
            cols = [slice(t * tb, (t + 1) * tb) for t in range(width)]
            zw = [_dot(qs[h][rows[c], :], kb, NT) for h, c in chains]
            dww = [_dot(dos[h][rows[c], :], vb, NT) for h, c in chains]
            terms = [[_sb_softplus_terms(z[:, cl]) for cl in cols] for z in zw]
            lss = [[ls for ls, _ in tc] for tc in terms]
            ns = [[n for _, n in tc] for tc in terms]
            if diag_last:
                ns = [nc[:-1] + [jnp.where(causal_c[c], nc[-1], 0.0)] for nc, (h, c) in zip(ns, chains)]
            sufs = [[_hi_lo_dot(n, tri_x) for n in nc] for nc in ns]
            wbs, gs, runs_n = [], [], []
            for (h, c), lc, nc, sc, dw, (run_n, _, _) in zip(chains, lss, ns, sufs, dww, carry):
                wb, g = [None] * width, [None] * width
                for t in reversed(range(width)):
                    w = jnp.exp(lc[t] + (sc[t] + run_n))
                    if diag_last and t == width - 1:
                        w = jnp.where(causal_c[c], w, 0.0)
                    wb[t] = w.astype(BF16)
                    g[t] = wb[t].astype(F32) * dw[:, cols[t]]
                    run_n = run_n + jnp.sum(nc[t], axis=1, keepdims=True)
                wbs.append(wb)
                gs.append(g)
                runs_n.append(run_n)
            gsufs = [[_hi_lo_dot(g, tri_i) for g in gc] for gc in gs]
            out = []
            dk_t = [jnp.zeros((tb, SB_PAIR), F32) for _ in range(width)]
            dv_t = [jnp.zeros((tb, SB_PAIR), F32) for _ in range(width)]
            for (h, c), lc, gc, gsc, wb, run_n, (_, run_g, dq_acc) in zip(chains, lss, gs, gsufs, wbs, runs_n, carry):
                qh, doh, delta = qs[h][rows[c], :], dos[h][rows[c], :], deltas[h][rows[c], :]
                dzb = [None] * width
                for t in reversed(range(width)):
                    pre = delta - (gsc[t] + run_g)
                    dz = gc[t] - jnp.exp(lc[t]) * (gc[t] + pre)
                    if diag_last and t == width - 1:
                        dz = jnp.where(causal_c[c], dz, 0.0)
                    run_g = run_g + jnp.sum(gc[t], axis=1, keepdims=True)
                    dzb[t] = dz.astype(BF16)
                    dk_t[t] = dk_t[t] + _dot(dz.T.astype(BF16), qh)
                    dv_t[t] = dv_t[t] + _dot(wb[t].astype(F32).T.astype(BF16), doh)
                out.append((run_n, run_g, dq_acc + _dot(dzb[0] if width == 1 else jnp.concatenate(dzb, axis=1), kb)))
            for t in range(width):
                dk_ref[pl.ds(pl.multiple_of((j_lo + t) * tb, tb), tb), :] += dk_t[t]
                dv_ref[pl.ds(pl.multiple_of((j_lo + t) * tb, tb), tb), :] += dv_t[t]
            return tuple(out)

        zero = (jnp.zeros((rc, 1), F32), jnp.zeros((rc, 1), F32), jnp.zeros((rc, SB_PAIR), F32))
        carry = lax.cond(i == 0, lambda c: group(0, 1, c, True), lambda c: group(i - 1, 2, c, True),
                         (zero,) * len(chains))

        def step(st):
            j, _, c = st
            c = group(j, 1, c, False)
            return j - 1, _sb_alive([r[0] for r in c]), c

        _, _, carry = lax.while_loop(lambda st: jnp.logical_and(st[0] >= 0, st[1]), step,
                                     (i - 2, _sb_alive([r[0] for r in carry]), carry))
        dqs = [jnp.concatenate([carry[n][2] for n, (h, c) in enumerate(chains) if h == hh], axis=0) for hh in range(2)]
        dq_ref[...] = jnp.where(first, dqs[0], dqs[1]) * (1.0 / math.sqrt(SB_HEAD))

    qtile = pl.BlockSpec((tb, SB_PAIR), lambda p, i: (i, p))
    col = pl.BlockSpec((s, SB_PAIR), lambda p, i: (0, p))
    return pl.pallas_call(
        body, grid=(npair, s // tb),
        in_specs=[qtile, col, pl.BlockSpec((s, SB_PAIR), lambda p, i: (0, npair + p)), qtile, qtile],
        out_specs=[qtile, col, col],
        out_shape=[jax.ShapeDtypeStruct((s, SB_W), F32)] * 3,
        compiler_params=_cparams(2, VMEM_LIMIT_MB), name=name)(q, kv, kv, o, do)


def _mixnorm_math(yl, ys, gl, gs):
    def rms(x, g):
        return x * lax.rsqrt(jnp.mean(x * x, axis=-1, keepdims=True) + RMS_EPS) * g
    return rms(yl, gl), rms(ys, gs)


def _mixnorm_fwd(yl, ys, gl, gs, *, name):
    s, c = yl.shape
    tm = min(512, s)

    def body(yl_ref, ys_ref, gl_ref, gs_ref, y_ref):
        a, b = _mixnorm_math(yl_ref[...], ys_ref[...], gl_ref[...], gs_ref[...])
        y_ref[:, 0:c] = a
        y_ref[:, c:2 * c] = b

    return pl.pallas_call(
        body, grid=(s // tm,), in_specs=[_row_spec(tm, c), _row_spec(tm, c), _par_spec(c), _par_spec(c)],
        out_specs=_row_spec(tm, 2 * c), out_shape=jax.ShapeDtypeStruct((s, 2 * c), F32),
        compiler_params=_cparams(1), name=name)(yl, ys, gl, gs)


def _mixnorm_bwd(yl, ys, gl, gs, dy, *, name):
    s, c = yl.shape
    tm = min(512, s)

    def body(yl_ref, ys_ref, gl_ref, gs_ref, dy_ref, dyl_ref, dys_ref, dgl_ref, dgs_ref):
        @pl.when(pl.program_id(0) == 0)
        def _():
            dgl_ref[...] = jnp.zeros_like(dgl_ref)
            dgs_ref[...] = jnp.zeros_like(dgs_ref)

        _, vjp = jax.vjp(_mixnorm_math, yl_ref[...], ys_ref[...], gl_ref[...], gs_ref[...])
        dyl, dys, dgl, dgs = vjp((dy_ref[:, 0:c], dy_ref[:, c:2 * c]))
        dyl_ref[...] = dyl
        dys_ref[...] = dys
        dgl_ref[...] += dgl
        dgs_ref[...] += dgs

    return pl.pallas_call(
        body, grid=(s // tm,),
        in_specs=[_row_spec(tm, c), _row_spec(tm, c), _par_spec(c), _par_spec(c), _row_spec(tm, 2 * c)],
        out_specs=[_row_spec(tm, c), _row_spec(tm, c), _par_spec(c), _par_spec(c)],
        out_shape=[jax.ShapeDtypeStruct((s, c), F32)] * 2 + [jax.ShapeDtypeStruct((1, c), F32)] * 2,
        compiler_params=_cparams(1), name=name)(yl, ys, gl, gs, dy)


def _cross_probs(qh, kh):
    sc = _dot(qh, kh, NT) * (1.0 / math.sqrt(MEM_HD))
    e = jnp.exp(sc - jnp.max(sc, axis=-1, keepdims=True))
    return e / jnp.sum(e, axis=-1, keepdims=True)


def _cross_fwd(q, kv, *, name):
    s, d = q.shape
    mlen = kv.shape[1]
    tm = min(512, s)

    def body(q_ref, kv_ref, o_ref):
        for h in range(MEM_HEADS):
            cols = slice(h * MEM_HD, (h + 1) * MEM_HD)
            p = _cross_probs(q_ref[:, cols].astype(BF16), kv_ref[h].astype(BF16))
            o_ref[:, cols] = _dot(p.astype(BF16), kv_ref[MEM_HEADS + h].astype(BF16))

    return pl.pallas_call(
        body, grid=(s // tm,),
        in_specs=[_row_spec(tm, d), pl.BlockSpec((2 * MEM_HEADS, mlen, MEM_HD), lambda i: (0, 0, 0))],
        out_specs=_row_spec(tm, d), out_shape=jax.ShapeDtypeStruct((s, d), F32),
        compiler_params=_cparams(1, VMEM_LIMIT_MB), name=name)(q, kv)


def _cross_bwd(q, kv, do, *, name):
    s, d = q.shape
    mlen = kv.shape[1]
    tm = min(512, s)

    def body(q_ref, kv_ref, do_ref, dq_ref, dkv_ref):
        @pl.when(pl.program_id(0) == 0)
        def _():
            dkv_ref[...] = jnp.zeros_like(dkv_ref)

        for h in range(MEM_HEADS):
            cols = slice(h * MEM_HD, (h + 1) * MEM_HD)
            qh = q_ref[:, cols].astype(BF16)
            kh = kv_ref[h].astype(BF16)
            doh = do_ref[:, cols].astype(BF16)
            p = _cross_probs(qh, kh)
            dp = _dot(doh, kv_ref[MEM_HEADS + h].astype(BF16), NT)
            ds = p * (dp - jnp.sum(dp * p, axis=-1, keepdims=True)) * (1.0 / math.sqrt(MEM_HD))
            dq_ref[:, cols] = _dot(ds.astype(BF16), kh)
            dkv_ref[h] += _dot(ds.T.astype(BF16), qh)
            dkv_ref[MEM_HEADS + h] += _dot(p.T.astype(BF16), doh)

    kv_spec = pl.BlockSpec((2 * MEM_HEADS, mlen, MEM_HD), lambda i: (0, 0, 0))
    return pl.pallas_call(
        body, grid=(s // tm,), in_specs=[_row_spec(tm, d), kv_spec, _row_spec(tm, d)],
        out_specs=[_row_spec(tm, d), kv_spec],
        out_shape=[jax.ShapeDtypeStruct((s, d), F32), jax.ShapeDtypeStruct(kv.shape, F32)],
        compiler_params=_cparams(1, VMEM_LIMIT_MB), name=name)(q, kv, do)


def _proj_bwd_dx(parts, w_in, dz, *, name):
    s = dz.shape[0]
    d = D_MODEL
    tm = min(256, s)
    widths = [p.shape[1] for p in parts]

    def body(*refs):
        part_refs, w_ref, dz_ref, o_ref = refs[:len(parts)], refs[len(parts)], refs[-2], refs[-1]
        acc = ALPHA * dz_ref[...]
        off = 0
        for p_ref, wd in zip(part_refs, widths):
            acc = acc + _dot(p_ref[...].astype(BF16), w_ref[:, off:off + wd], NT)
            off += wd
        o_ref[...] = acc

    return pl.pallas_call(
        body, grid=(s // tm,),
        in_specs=[_row_spec(tm, wd) for wd in widths] + [pl.BlockSpec(w_in.shape, lambda i: (0, 0)), _row_spec(tm, d)],
        out_specs=_row_spec(tm, d), out_shape=jax.ShapeDtypeStruct((s, d), F32),
        compiler_params=_cparams(1, VMEM_LIMIT_MB), name=name)(*parts, w_in, dz)


def _mesh_pos():
    return lax.axis_index("x"), lax.axis_index("y"), lax.axis_index("c")


def _all_gather(shards, *, name):
    n = len(shards)

    def body(*refs):
        ins, outs = refs[:n], refs[n:2 * n]
        send_sems, recv_sems, local_sems = refs[2 * n:]
        x, y, c = _mesh_pos()
        me, sibling = (x, y, c), (x, y, 1 - c)
        chips = [(1 - x, y), (x, 1 - y), (1 - x, 1 - y)]

        def copy(a, k, block, to, src=None):
            dst = outs[a].at[4 * block[0] + 2 * block[1] + block[2]]
            return pltpu.make_async_remote_copy(
                src_ref=dst if src is None else src, dst_ref=dst,
                send_sem=send_sems.at[a, k], recv_sem=recv_sems.at[a, k], device_id=to, device_id_type=MESH)

        mine = [pltpu.make_async_copy(ins[a], outs[a].at[4 * x + 2 * y + c], local_sems.at[a]) for a in range(n)]
        for cp in mine:
            cp.start()
        first = []
        for a in range(n):
            first.append(copy(a, 0, me, sibling, src=ins[a]))
            first += [copy(a, 1 + j, me, (*chip, c), src=ins[a]) for j, chip in enumerate(chips)]
        for cp in first:
            cp.start()
        passed = []
        for a in range(n):
            for j, chip in enumerate(chips):
                copy(a, 1 + j, (*chip, c), me).wait_recv()
                fwd = copy(a, 4 + j, (*chip, c), sibling)
                fwd.start()
                passed.append(fwd)
        for a in range(n):
            copy(a, 0, sibling, me).wait_recv()
            for j, chip in enumerate(chips):
                copy(a, 4 + j, (*chip, 1 - c), me).wait_recv()
        for cp in first + passed:
            cp.wait_send()
        for cp in mine:
            cp.wait()

    any_spec = pl.BlockSpec(memory_space=pl.ANY)
    return pl.pallas_call(
        body, in_specs=[any_spec] * n, out_specs=[any_spec] * n,
        out_shape=[jax.ShapeDtypeStruct((N_DEV,) + a.shape, a.dtype) for a in shards],
        scratch_shapes=[pltpu.SemaphoreType.DMA((n, 7)), pltpu.SemaphoreType.DMA((n, 7)), pltpu.SemaphoreType.DMA((n,))],
        name=name)(*shards)


def _sibling_exchange(grads, *, name):
    n = len(grads)
    half = N_DEV // 2

    def body(*refs):
        ins, outs = refs[:n], refs[n:2 * n]
        send_sems, recv_sems = refs[2 * n:]
        x, y, c = _mesh_pos()
        copies = [pltpu.make_async_remote_copy(
            src_ref=ins[a].at[2 * j + (1 - c)], dst_ref=outs[a].at[j], send_sem=send_sems.at[a, j],
            recv_sem=recv_sems.at[a, j], device_id=(x, y, 1 - c), device_id_type=MESH)
            for a in range(n) for j in range(half)]
        for cp in copies:
            cp.start()
        for cp in copies:
            cp.wait()

    any_spec = pl.BlockSpec(memory_space=pl.ANY)
    return pl.pallas_call(
        body, in_specs=[any_spec] * n, out_specs=[any_spec] * n,
        out_shape=[jax.ShapeDtypeStruct((half,) + g.shape[1:], g.dtype) for g in grads],
        scratch_shapes=[pltpu.SemaphoreType.DMA((n, half)), pltpu.SemaphoreType.DMA((n, half))],
        name=name)(*grads)


def _pair_add(grad, recv, c_idx, *, name):
    half, r, cdim = recv.shape
    tr = _row_tile(r, 512)

    def body(c_ref, g_ref, r_ref, t_ref, tb_ref):
        t = g_ref[...] + r_ref[...]
        t_ref[...] = t
        tb_ref[...] = t.astype(BF16)

    grid_spec = pltpu.PrefetchScalarGridSpec(
        num_scalar_prefetch=1, grid=(half, r // tr),
        in_specs=[pl.BlockSpec((None, tr, cdim), lambda j, i, c_ref: (2 * j + c_ref[0], i, 0)),
                  pl.BlockSpec((None, tr, cdim), lambda j, i, c_ref: (j, i, 0))],
        out_specs=[pl.BlockSpec((None, tr, cdim), lambda j, i, c_ref: (j, i, 0))] * 2)
    return pl.pallas_call(
        body, grid_spec=grid_spec,
        out_shape=[jax.ShapeDtypeStruct(recv.shape, F32), jax.ShapeDtypeStruct(recv.shape, BF16)],
        compiler_params=_cparams(2), name=name)(c_idx, grad, recv)


def _chip_exchange(partials, *, name):
    n = len(partials)
    half = N_DEV // 2

    def body(*refs):
        ins, outs = refs[:n], refs[n:2 * n]
        send_sems, recv_sems, local_sems = refs[2 * n:]
        x, y, c = _mesh_pos()
        jme = 2 * x + y
        local = [pltpu.make_async_copy(ins[a].at[jme], outs[a].at[jme], local_sems.at[a]) for a in range(n)]
        for cp in local:
            cp.start()
        for j in range(half):
            @pl.when(jme != j)
            def _(j=j):
                for a in range(n):
                    pltpu.make_async_remote_copy(
                        src_ref=ins[a].at[j], dst_ref=outs[a].at[jme], send_sem=send_sems.at[a, j],
                        recv_sem=recv_sems.at[a, jme], device_id=(j // 2, j % 2, c), device_id_type=MESH).start()
        for j in range(half):
            @pl.when(jme != j)
            def _(j=j):
                for a in range(n):
                    arrive = pltpu.make_async_remote_copy(
                        src_ref=ins[a].at[j], dst_ref=outs[a].at[j], send_sem=send_sems.at[a, j],
                        recv_sem=recv_sems.at[a, j], device_id=(j // 2, j % 2, c), device_id_type=MESH)
                    arrive.wait_recv()
                    arrive.wait_send()
        for cp in local:
            cp.wait()

    any_spec = pl.BlockSpec(memory_space=pl.ANY)
    return pl.pallas_call(
        body, in_specs=[any_spec] * n, out_specs=[any_spec] * n,
        out_shape=[jax.ShapeDtypeStruct(p.shape, p.dtype) for p in partials],
        scratch_shapes=[pltpu.SemaphoreType.DMA((n, half)), pltpu.SemaphoreType.DMA((n, half)),
                        pltpu.SemaphoreType.DMA((n,))],
        name=name)(*partials)


def _adam_math(w, g, m, v):
    m = ADAM_B1 * m + (1.0 - ADAM_B1) * g
    v = ADAM_B2 * v + (1.0 - ADAM_B2) * (g * g)
    m_hat = m / (1.0 - ADAM_B1 ** ADAM_STEP)
    v_hat = v / (1.0 - ADAM_B2 ** ADAM_STEP)
    delta = -ADAM_LR * (m_hat / (jnp.sqrt(v_hat) + ADAM_EPS) + ADAM_WD * w)
    return delta, m, v


def _reduce_adam(own, recv, j_idx, w, m, v, *, name):
    half, r, cdim = recv.shape
    tr = _row_tile(r, 256)

    def body(j_ref, own_ref, recv_ref, w_ref, m_ref, v_ref, g_ref, d_ref, nm_ref, nv_ref):
        jme = j_ref[0]
        g = own_ref[...]
        for sl in range(half):
            g = g + jnp.where(sl == jme, 0.0, recv_ref[sl].astype(F32))
        delta, nm, nv = _adam_math(w_ref[...], g, m_ref[...], v_ref[...])
        g_ref[...] = g
        d_ref[...] = delta
        nm_ref[...] = nm
        nv_ref[...] = nv

    tile = pl.BlockSpec((tr, cdim), lambda i, j_ref: (i, 0))
    grid_spec = pltpu.PrefetchScalarGridSpec(
        num_scalar_prefetch=1, grid=(r // tr,),
        in_specs=[pl.BlockSpec((None, tr, cdim), lambda i, j_ref: (j_ref[0], i, 0)),
                  pl.BlockSpec((half, tr, cdim), lambda i, j_ref: (0, i, 0)), tile, tile, tile],
        out_specs=[tile] * 4)
    return pl.pallas_call(
        body, grid_spec=grid_spec, out_shape=[jax.ShapeDtypeStruct((r, cdim), F32)] * 4,
        compiler_params=_cparams(1), name=name)(j_idx, own, recv, w, m, v)


def _small_all_reduce(pack, *, name):
    r, cdim = pack.shape

    def body(p_ref, o_ref, land_ref, send_sems, recv_sems):
        x, y, c = _mesh_pos()
        me = 4 * x + 2 * y + c
        land_ref[me] = p_ref[...]
        for t in range(N_DEV):
            @pl.when(me != t)
            def _(t=t):
                pltpu.make_async_remote_copy(
                    src_ref=p_ref, dst_ref=land_ref.at[me], send_sem=send_sems.at[t], recv_sem=recv_sems.at[me],
                    device_id=(t // 4, (t // 2) % 2, t % 2), device_id_type=MESH).start()
        for t in range(N_DEV):
            @pl.when(me != t)
            def _(t=t):
                cp = pltpu.make_async_remote_copy(
                    src_ref=p_ref, dst_ref=land_ref.at[t], send_sem=send_sems.at[t], recv_sem=recv_sems.at[t],
                    device_id=(t // 4, (t // 2) % 2, t % 2), device_id_type=MESH)
                cp.wait_recv()
                cp.wait_send()
        total = land_ref[0]
        for t in range(1, N_DEV):
            total = total + land_ref[t]
        o_ref[...] = total

    vmem = pl.BlockSpec(memory_space=pltpu.VMEM)
    return pl.pallas_call(
        body, in_specs=[vmem], out_specs=vmem, out_shape=jax.ShapeDtypeStruct((r, cdim), F32),
        scratch_shapes=[pltpu.VMEM((N_DEV, r, cdim), F32), pltpu.SemaphoreType.DMA((N_DEV,)),
                        pltpu.SemaphoreType.DMA((N_DEV,))],
        name=name)(pack)


def _adam_small(w, g, m, v, *, name):
    def body(w_ref, g_ref, m_ref, v_ref, d_ref, nm_ref, nv_ref):
        d_ref[...], nm_ref[...], nv_ref[...] = _adam_math(w_ref[...], g_ref[...], m_ref[...], v_ref[...])

    vmem = pl.BlockSpec(memory_space=pltpu.VMEM)
    return pl.pallas_call(
        body, in_specs=[vmem] * 4, out_specs=[vmem] * 3, out_shape=[jax.ShapeDtypeStruct(w.shape, F32)] * 3,
        name=name)(w, g, m, v)


def _block_diag(w):
    h, a, b = w.shape
    eye = jnp.eye(h, dtype=w.dtype)
    return (eye[:, None, :, None] * w[:, :, None, :]).reshape(h * a, h * b)


def _diag_blocks(w, h):
    a = w.shape[0] // h
    return jnp.stack([w[i * a:(i + 1) * a, i * a:(i + 1) * a] for i in range(h)])


def _local_step(x, mem, target, p):
    g = {}
    w_rg, w_ig = _block_diag(p["w_rgate"]).astype(BF16), _block_diag(p["w_igate"]).astype(BF16)
    w_in = p["w_in"]
    w_in_lru, w_in_q, w_in_kv = w_in[:, :2 * LRU_W], w_in[:, 2 * LRU_W:2 * LRU_W + SB_W], w_in[:, 2 * LRU_W + SB_W:]

    gu1, f1 = _ffn_fwd(x, p["ffn1_w13"], p["ffn1_w2"], name="ffn1_fwd")
    h1 = _ln_fwd(x, f1, 0.5, p["ln1_g"], p["ln1_b"], name="ln1_fwd")
    xbg = _matmul(h1, w_in_lru, name="proj_lru")
    q_sb = _matmul(h1, w_in_q, out_dtype=BF16, scale=1.0 / math.sqrt(SB_HEAD), name="proj_q")
    kv_sb = _matmul(h1, w_in_kv, out_dtype=BF16, name="proj_kv")
    xc, hseq, y_lru = _lru_fwd(xbg, p["conv_w"], p["conv_b"], w_rg, p["b_rgate"], w_ig, p["b_igate"],
                               p["lru_lambda"], name="lru_fwd")
    y_sb = _sb_fwd(q_sb, kv_sb, name="sb_fwd")
    ymix = _mixnorm_fwd(y_lru, y_sb, p["g_lru"], p["g_sb"], name="mixnorm_fwd")
    mix = _matmul(ymix, p["w_out"], name="mix_out")
    h2 = _ln_fwd(h1, mix, 1.0, p["ln2_g"], p["ln2_b"], name="ln2_fwd")
    qm = _matmul(h2, p["mem_wq"], name="cross_q")
    kv = _matmul(mem, p["mem_wkv"], name="cross_kv")
    o_cross = _cross_fwd(qm, kv, name="cross_fwd")
    cross = _matmul(o_cross, p["mem_wo"], name="cross_out")
    h3 = _ln_fwd(h2, cross, 1.0, p["ln3_g"], p["ln3_b"], name="ln3_fwd")
    gu2, f2 = _ffn_fwd(h3, p["ffn2_w13"], p["ffn2_w2"], name="ffn2_fwd")

    dz4, g["ln4_g"], g["ln4_b"], loss = _ln_bwd(h3, f2, 0.5, p["ln4_g"], p["ln4_b"], None, target=target, name="ln4_loss_bwd")
    dh3, dgu2, act2 = _ffn_bwd(dz4, gu2, p["ffn2_w13"], p["ffn2_w2"], name="ffn2_bwd")
    g["ffn2_w13"] = _matmul(h3, dgu2, trans_a=True, tm=1024, tn=FF_CHUNK, tk=512, name="ffn2_dw13")
    g["ffn2_w2"] = _matmul(act2, dz4, trans_a=True, tm=FF_CHUNK, tn=1024, tk=512, scale=0.5, name="ffn2_dw2")
    dz3, g["ln3_g"], g["ln3_b"] = _ln_bwd(h2, cross, 1.0, p["ln3_g"], p["ln3_b"], dh3, name="ln3_bwd")
    g["mem_wo"] = _matmul(o_cross, dz3, trans_a=True, tm=1024, tn=1024, tk=512, name="cross_dwo")
    do_cross = _matmul(dz3, p["mem_wo"], trans_b=True, name="cross_do")
    dqm, dkv = _cross_bwd(qm, kv, do_cross, name="cross_bwd")
    g["mem_wq"] = _matmul(h2, dqm, trans_a=True, tm=1024, tn=1024, tk=512, name="cross_dwq")
    g["mem_wkv"] = _matmul(mem, dkv, trans_a=True, tm=1024, tn=MEM_HD, tk=256, name="cross_dwkv")
    dh2 = _matmul(dqm, p["mem_wq"], trans_b=True, res=dz3, res_coeff=ALPHA, name="cross_dh")
    dz2, g["ln2_g"], g["ln2_b"] = _ln_bwd(h1, mix, 1.0, p["ln2_g"], p["ln2_b"], dh2, name="ln2_bwd")
    g["w_out"] = _matmul(ymix, dz2, trans_a=True, tm=1024, tn=1024, tk=512, name="mix_dwout")
    dymix = _matmul(dz2, p["w_out"], trans_b=True, name="mix_dy")
    dy_lru, dy_sb, g["g_lru"], g["g_sb"] = _mixnorm_bwd(y_lru, y_sb, p["g_lru"], p["g_sb"], dymix, name="mixnorm_bwd")
    dq, dk, dv = _sb_bwd(q_sb, kv_sb, y_sb, dy_sb, name="sb_bwd")
    dxc, dgate, dwr, dwi, g["b_rgate"], g["b_igate"], g["lru_lambda"] = _lru_bwd(
        xc, xbg, hseq, dy_lru, w_rg, p["b_rgate"], w_ig, p["b_igate"], p["lru_lambda"], name="lru_bwd")
    g["w_rgate"], g["w_igate"] = _diag_blocks(dwr, 8), _diag_blocks(dwi, 8)
    dxb, g["conv_w"], g["conv_b"] = _conv_bwd(dxc, xbg, p["conv_w"], name="conv_bwd")
    parts = [dxb, dgate, dq, dk, dv]
    g["w_in"] = jnp.concatenate(
        [_matmul(h1, part, trans_a=True, tm=1024, tn=512, tk=512, name=f"proj_dw{n}") for n, part in enumerate(parts)], axis=1)
    dh1 = _proj_bwd_dx(parts, w_in, dz2, name="proj_dh")
    dz1, g["ln1_g"], g["ln1_b"] = _ln_bwd(x, f1, 0.5, p["ln1_g"], p["ln1_b"], dh1, name="ln1_bwd")
    grad_x, dgu1, act1 = _ffn_bwd(dz1, gu1, p["ffn1_w13"], p["ffn1_w2"], name="ffn1_bwd")
    g["ffn1_w13"] = _matmul(x, dgu1, trans_a=True, tm=1024, tn=FF_CHUNK, tk=512, name="ffn1_dw13")
    g["ffn1_w2"] = _matmul(act1, dz1, trans_a=True, tm=FF_CHUNK, tn=1024, tk=512, scale=0.5, name="ffn1_dw2")
    return loss, grad_x, g


_WEIGHTS = ['ffn1_w13', 'ffn1_w2', 'ln1_g', 'ln1_b', 'w_in', 'conv_w', 'conv_b', 'w_rgate', 'b_rgate', 'w_igate',
            'b_igate', 'lru_lambda', 'g_lru', 'g_sb', 'w_out', 'ln2_g', 'ln2_b', 'mem_wq', 'mem_wkv', 'mem_wo',
            'ln3_g', 'ln3_b', 'ffn2_w13', 'ffn2_w2', 'ln4_g', 'ln4_b']
_SHARDED = ['ffn1_w13', 'ffn1_w2', 'w_in', 'w_out', 'mem_wq', 'mem_wkv', 'mem_wo', 'ffn2_w13', 'ffn2_w2']
_COL_SHARDED = ('ffn1_w13', 'ffn2_w13', 'w_in', 'mem_wkv', 'conv_w')
_SMALL = ['ln1_g', 'ln1_b', 'ln2_g', 'ln2_b', 'ln3_g', 'ln3_b', 'ln4_g', 'ln4_b', 'conv_b', 'b_rgate', 'b_igate',
          'lru_lambda', 'g_lru', 'g_sb', 'w_rgate', 'w_igate']


def _pack_small(d, conv_w_full=None):
    rows = [d[n].reshape(-1) for n in _SMALL]
    if conv_w_full is not None:
        rows.append(conv_w_full.reshape(-1))
    flat = jnp.concatenate(rows)
    pad = (-flat.shape[0]) % (8 * D_MODEL)
    return jnp.pad(flat, (0, pad)).reshape(-1, D_MODEL)


def _unpack_small(pack, like, with_conv):
    flat = pack.reshape(-1)
    out, off = {}, 0
    for n in _SMALL:
        size = math.prod(like[n].shape)
        out[n] = flat[off:off + size].reshape(like[n].shape)
        off += size
    conv = flat[off:off + CONV_W * LRU_W].reshape(CONV_W, LRU_W) if with_conv else None
    return out, conv


def _gathered_weights(w):
    names = _SHARDED
    shards = [w[n][0].astype(BF16) for n in names] + [w['conv_w'][0]]
    full = dict(zip(names + ['conv_w'], _all_gather(shards, name="gather_weights")))
    p = {}
    for n in ('ffn1_w13', 'ffn2_w13', 'mem_wkv'):
        p[n] = full[n]
    for n in ('ffn1_w2', 'ffn2_w2', 'w_out', 'mem_wq', 'mem_wo'):
        p[n] = full[n].reshape(-1, full[n].shape[-1])
    p['w_in'] = full['w_in'].transpose(1, 0, 2).reshape(D_MODEL, -1)
    p['conv_w'] = full['conv_w'].transpose(1, 0, 2).reshape(CONV_W, LRU_W)
    for n in _WEIGHTS:
        if n not in p:
            p[n] = w[n][0] if w[n].ndim == 4 else w[n]
    return p


def kernel(x, mem, ffn1_w13, ffn1_w2, ln1_g, ln1_b, w_in, conv_w, conv_b, w_rgate, b_rgate, w_igate, b_igate, lru_lambda, g_lru, g_sb, w_out, ln2_g, ln2_b, mem_wq, mem_wkv, mem_wo, ln3_g, ln3_b, ffn2_w13, ffn2_w2, ln4_g, ln4_b, loss_target, m_ffn1_w13, m_ffn1_w2, m_ln1_g, m_ln1_b, m_w_in, m_conv_w, m_conv_b, m_w_rgate, m_b_rgate, m_w_igate, m_b_igate, m_lru_lambda, m_g_lru, m_g_sb, m_w_out, m_ln2_g, m_ln2_b, m_mem_wq, m_mem_wkv, m_mem_wo, m_ln3_g, m_ln3_b, m_ffn2_w13, m_ffn2_w2, m_ln4_g, m_ln4_b, v_ffn1_w13, v_ffn1_w2, v_ln1_g, v_ln1_b, v_w_in, v_conv_w, v_conv_b, v_w_rgate, v_b_rgate, v_w_igate, v_b_igate, v_lru_lambda, v_g_lru, v_g_sb, v_w_out, v_ln2_g, v_ln2_b, v_mem_wq, v_mem_wkv, v_mem_wo, v_ln3_g, v_ln3_b, v_ffn2_w13, v_ffn2_w2, v_ln4_g, v_ln4_b):
    args = locals()
    w = {n: args[n] for n in _WEIGHTS}
    mom = {n: args["m_" + n] for n in _WEIGHTS}
    var = {n: args["v_" + n] for n in _WEIGHTS}
    ix, iy, ic = lax.axis_index("x"), lax.axis_index("y"), lax.axis_index("c")

    p = _gathered_weights(w)
    loss_local, grad_x, g = _local_step(x[0], mem[0], loss_target[0], p)
    loss = lax.psum(loss_local[0, 0], ("x", "y", "c"))

    g["w_in"] = g["w_in"].reshape(D_MODEL, N_DEV, -1).transpose(1, 0, 2)
    blocks = [g[n].reshape((N_DEV, -1, g[n].shape[-1])) for n in _SHARDED]
    from_sibling = _sibling_exchange(blocks, name="reduce_sibling")
    c_idx = jnp.reshape(ic, (1,)).astype(jnp.int32)
    j_idx = jnp.reshape(2 * ix + iy, (1,)).astype(jnp.int32)
    sums = [_pair_add(b, r, c_idx, name=f"pair_add_{n}") for n, b, r in zip(_SHARDED, blocks, from_sibling)]
    from_chips = _chip_exchange([tb for _, tb in sums], name="reduce_chips")
    grads, delta, new_m, new_v = {}, {}, {}, {}
    for n, (t, _), r in zip(_SHARDED, sums, from_chips):
        w2d = w[n][0]
        grads[n], delta[n], new_m[n], new_v[n] = (
            o.reshape(w[n].shape) for o in _reduce_adam(t, r, j_idx, w2d, mom[n][0], var[n][0], name=f"adam_{n}"))

    total = _small_all_reduce(_pack_small(g, conv_w_full=g["conv_w"]), name="reduce_small")
    gsmall, gconv = _unpack_small(total, w, with_conv=True)
    me = 4 * ix + 2 * iy + ic
    gsmall["conv_w"] = lax.dynamic_slice_in_dim(gconv, me * (LRU_W // N_DEV), LRU_W // N_DEV, axis=1)[None]
    small_names = _SMALL + ["conv_w"]
    pk = lambda d: _pack_small({n: d[n] for n in _SMALL}, conv_w_full=jnp.pad(d["conv_w"].reshape(-1), (0, CONV_W * LRU_W - CONV_W * LRU_W // N_DEV)))
    d_s, m_s, v_s = _adam_small(pk(w), pk(gsmall), pk(mom), pk(var), name="adam_small")
    for src, dst in ((d_s, delta), (m_s, new_m), (v_s, new_v)):
        vals, conv = _unpack_small(src, w, with_conv=True)
        dst.update(vals)
        dst["conv_w"] = conv.reshape(-1)[:CONV_W * LRU_W // N_DEV].reshape(w["conv_w"].shape)
    grads.update({n: gsmall[n].reshape(w[n].shape) for n in small_names})

    return (loss, grad_x[None], *[grads[n] for n in _WEIGHTS], *[delta[n] for n in _WEIGHTS],
            *[new_m[n] for n in _WEIGHTS], *[new_v[n] for n in _WEIGHTS])
```

```python
import functools
import math

import jax
import jax.numpy as jnp
from jax import lax
from jax.experimental import pallas as pl
from jax.experimental.pallas import tpu as pltpu

F32, BF16 = jnp.float32, jnp.bfloat16

D_MODEL = 1024
LRU_W = 512
SB_W = 512
SB_PAIR = 128
SB_HEAD = 64
SB_ROWS = 128
SB_SKIP = -110.0
D_FF = 2688
FF_CHUNK = 672
N_DEV = 8
MEM_HEADS = 4
MEM_HD = 256
CONV_W = 4
ALPHA = 2.0 ** 0.25
LN_EPS = 1e-5
RMS_EPS = 1e-6
LRU_C = 8.0
ADAM_LR, ADAM_B1, ADAM_B2, ADAM_EPS, ADAM_WD, ADAM_STEP = 0.001, 0.9, 0.999, 1e-08, 0.01, 10

NN = (((1,), (0,)), ((), ()))
NT = (((1,), (1,)), ((), ()))
MESH = pl.DeviceIdType.MESH
VMEM_LIMIT_MB = 56


def _dot(a, b, dn=NN):
    return lax.dot_general(a, b, dn, preferred_element_type=F32)


def _cparams(n_axes, vmem_mb=None):
    kw = dict(dimension_semantics=("arbitrary",) * n_axes)
    if vmem_mb is not None:
        kw["vmem_limit_bytes"] = vmem_mb << 20
    return pltpu.CompilerParams(**kw)


def _row_tile(rows, want):
    if rows <= want:
        return rows
    t = want - want % 8
    while rows % t:
        t -= 8
    return t


def _matmul(a, b, *, name, out_dtype=F32, trans_a=False, trans_b=False, tm=512, tn=1024, tk=1024,
            scale=1.0, res=None, res_coeff=1.0):
    a_chunked, b_chunked = a.ndim == 3, b.ndim == 3
    nc = a.shape[0] if a_chunked else (b.shape[0] if b_chunked else 1)
    a2, b2 = a.shape[-2:], b.shape[-2:]
    (kdim, m) = a2 if trans_a else a2[::-1]
    n = b2[0] if trans_b else b2[1]
    tm, tn, tk = min(tm, m), min(tn, n), min(tk, kdim)
    assert m % tm == 0 and n % tn == 0 and kdim % tk == 0, (name, m, n, kdim)
    nk = kdim // tk

    def a_idx(c, i, j, k):
        idx = (k, i) if trans_a else (i, k)
        return (c,) + idx if a_chunked else idx

    def b_idx(c, i, j, k):
        idx = (j, k) if trans_b else (k, j)
        return (c,) + idx if b_chunked else idx

    a_blk = (tk, tm) if trans_a else (tm, tk)
    b_blk = (tn, tk) if trans_b else (tk, tn)
    in_specs = [pl.BlockSpec(((None,) + a_blk) if a_chunked else a_blk, a_idx),
                pl.BlockSpec(((None,) + b_blk) if b_chunked else b_blk, b_idx)]
    args = [a, b]
    if res is not None:
        in_specs.append(pl.BlockSpec((None, tm, tn), lambda c, i, j, k: (c, i, j)))
        args.append(res.reshape((nc, m, n)))

    def body(*refs):
        if res is not None:
            a_ref, b_ref, r_ref, o_ref, acc_ref = refs
        else:
            a_ref, b_ref, o_ref, acc_ref = refs
        k = pl.program_id(3)

        @pl.when(k == 0)
        def _():
            acc_ref[...] = jnp.zeros_like(acc_ref)

        av = a_ref[...]
        if trans_a:
            av = av.astype(F32).T
        acc_ref[...] += _dot(av.astype(BF16), b_ref[...].astype(BF16), NT if trans_b else NN)

        @pl.when(k == nk - 1)
        def _():
            out = acc_ref[...]
            if scale != 1.0:
                out = out * scale
            if res is not None:
                out = out + res_coeff * r_ref[...]
            o_ref[...] = out.astype(out_dtype)

    out = pl.pallas_call(
        body, grid=(nc, m // tm, n // tn, nk), in_specs=in_specs,
        out_specs=pl.BlockSpec((None, tm, tn), lambda c, i, j, k: (c, i, j)),
        out_shape=jax.ShapeDtypeStruct((nc, m, n), out_dtype),
        scratch_shapes=[pltpu.VMEM((tm, tn), F32)],
        compiler_params=_cparams(4, VMEM_LIMIT_MB), name=name)(*args)
    return out if (a_chunked or b_chunked) else out[0]


def _ln_math(z, g, b):
    mu = jnp.mean(z, axis=-1, keepdims=True)
    zc = z - mu
    var = jnp.mean(zc * zc, axis=-1, keepdims=True)
    return zc * lax.rsqrt(var + LN_EPS) * g + b


def _row_spec(tm, d):
    return pl.BlockSpec((tm, d), lambda i: (i, 0))


def _par_spec(d, rows=1):
    return pl.BlockSpec((rows, d), lambda i: (0, 0))


def _ln_fwd(x, f, fscale, g, b, *, name):
    s, d = x.shape
    tm = min(512, s)

    def body(x_ref, f_ref, g_ref, b_ref, o_ref):
        z = ALPHA * x_ref[...] + fscale * f_ref[...]
        o_ref[...] = _ln_math(z, g_ref[...], b_ref[...])

    return pl.pallas_call(
        body, grid=(s // tm,), in_specs=[_row_spec(tm, d), _row_spec(tm, d), _par_spec(d), _par_spec(d)],
        out_specs=_row_spec(tm, d), out_shape=jax.ShapeDtypeStruct((s, d), F32),
        compiler_params=_cparams(1), name=name)(x, f, g, b)


def _ln_bwd(x, f, fscale, g, b, dy, *, name, target=None):
    s, d = x.shape
    tm = min(512, s)
    with_loss = target is not None

    def body(*refs):
        if with_loss:
            x_ref, f_ref, g_ref, b_ref, t_ref, dz_ref, dg_ref, db_ref, loss_ref = refs
        else:
            x_ref, f_ref, g_ref, b_ref, dy_ref, dz_ref, dg_ref, db_ref = refs
        i = pl.program_id(0)

        @pl.when(i == 0)
        def _():
            dg_ref[...] = jnp.zeros_like(dg_ref)
            db_ref[...] = jnp.zeros_like(db_ref)
            if with_loss:
                loss_ref[...] = jnp.zeros_like(loss_ref)

        z = ALPHA * x_ref[...] + fscale * f_ref[...]
        y, vjp = jax.vjp(_ln_math, z, g_ref[...], b_ref[...])
        if with_loss:
            err = y - t_ref[...]
            sq = jnp.sum(err * err, axis=1, keepdims=True)
            loss_ref[...] += jnp.sum(sq, axis=0, keepdims=True) * (0.5 / d)
            cot = err * (1.0 / d)
        else:
            cot = dy_ref[...]
        dz, dg, db = vjp(cot)
        dz_ref[...] = dz
        dg_ref[...] += dg
        db_ref[...] += db

    in_specs = [_row_spec(tm, d), _row_spec(tm, d), _par_spec(d), _par_spec(d), _row_spec(tm, d)]
    out_specs = [_row_spec(tm, d), _par_spec(d), _par_spec(d)]
    out_shape = [jax.ShapeDtypeStruct((s, d), F32), jax.ShapeDtypeStruct((1, d), F32), jax.ShapeDtypeStruct((1, d), F32)]
    if with_loss:
        out_specs.append(pl.BlockSpec((1, 1), lambda i: (0, 0)))
        out_shape.append(jax.ShapeDtypeStruct((1, 1), F32))
    return pl.pallas_call(
        body, grid=(s // tm,), in_specs=in_specs, out_specs=out_specs, out_shape=out_shape,
        compiler_params=_cparams(1), name=name)(x, f, g, b, target if with_loss else dy)


def _load_weights_once(pairs, sem):
    @pl.when(pl.program_id(0) == 0)
    def _():
        copies = [pltpu.make_async_copy(src, dst, sem.at[n]) for n, (src, dst) in enumerate(pairs)]
        for c in copies:
            c.start()
        for c in copies:
            c.wait()


def _ffn_fwd(h, w13c, w2, *, name, job=None):
    s, d = h.shape
    tm = min(256, s)
    half = N_DEV // 2

    def body(h_ref, w13_hbm, w2_hbm, gu_ref, f_ref, w13_v, w2_v, sem):
        _load_weights_once([(w13_hbm, w13_v), (w2_hbm, w2_v)], sem)
        hb = h_ref[...].astype(BF16)
        acc = jnp.zeros((tm, d), F32)
        for k in range(half):
            g = _dot(hb, w13_v[k])
            u = _dot(hb, w13_v[k + half])
            gu_ref[k] = g.astype(BF16)
            gu_ref[k + half] = u.astype(BF16)
            a = g * jax.nn.sigmoid(g) * u
            acc = acc + _dot(a.astype(BF16), w2_v[pl.ds(k * FF_CHUNK, FF_CHUNK), :])
        f_ref[...] = acc

    any_spec = pl.BlockSpec(memory_space=pl.ANY)
    return _pallas_with_job(
        body, job, lambda: pl.program_id(0), s // tm,
        grid=(s // tm,), in_specs=[_row_spec(tm, d), any_spec, any_spec],
        out_specs=[pl.BlockSpec((N_DEV, tm, FF_CHUNK), lambda i: (0, i, 0)), _row_spec(tm, d)],
        out_shape=[jax.ShapeDtypeStruct((N_DEV, s, FF_CHUNK), BF16), jax.ShapeDtypeStruct((s, d), F32)],
        scratch_shapes=[pltpu.VMEM(w13c.shape, BF16), pltpu.VMEM(w2.shape, BF16), pltpu.SemaphoreType.DMA((2,))],
        compiler_params=_cparams(1, VMEM_LIMIT_MB), name=name, args=(h, w13c, w2))


def _ffn_bwd(dz, gu, w13c, w2, *, name):
    s, d = dz.shape
    tm = min(256, s)
    half = N_DEV // 2

    def body(dz_ref, gu_ref, w13_hbm, w2_hbm, dh_ref, dgu_ref, act_ref, w13_v, w2_v, sem):
        _load_weights_once([(w13_hbm, w13_v), (w2_hbm, w2_v)], sem)
        dzv = dz_ref[...]
        dfb = (0.5 * dzv).astype(BF16)
        acc = ALPHA * dzv
        for k in range(half):
            g = gu_ref[k].astype(F32)
            u = gu_ref[k + half].astype(F32)
            da = _dot(dfb, w2_v[pl.ds(k * FF_CHUNK, FF_CHUNK), :], NT)
            sg = jax.nn.sigmoid(g)
            silu = g * sg
            dg = (da * u * (sg * (1.0 + g * (1.0 - sg)))).astype(BF16)
            du = (da * silu).astype(BF16)
            act_ref[k] = (silu * u).astype(BF16)
            dgu_ref[k] = dg
            dgu_ref[k + half] = du
            acc = acc + _dot(dg, w13_v[k], NT) + _dot(du, w13_v[k + half], NT)
        dh_ref[...] = acc

    any_spec = pl.BlockSpec(memory_space=pl.ANY)
    return pl.pallas_call(
        body, grid=(s // tm,),
        in_specs=[_row_spec(tm, d), pl.BlockSpec((N_DEV, tm, FF_CHUNK), lambda i: (0, i, 0)), any_spec, any_spec],
        out_specs=[_row_spec(tm, d), pl.BlockSpec((N_DEV, tm, FF_CHUNK), lambda i: (0, i, 0)),
                   pl.BlockSpec((half, tm, FF_CHUNK), lambda i: (0, i, 0))],
        out_shape=[jax.ShapeDtypeStruct((s, d), F32), jax.ShapeDtypeStruct((N_DEV, s, FF_CHUNK), BF16),
                   jax.ShapeDtypeStruct((half, s, FF_CHUNK), BF16)],
        scratch_shapes=[pltpu.VMEM(w13c.shape, BF16), pltpu.VMEM(w2.shape, BF16), pltpu.SemaphoreType.DMA((2,))],
        compiler_params=_cparams(1, VMEM_LIMIT_MB), name=name)(dz, gu, w13c, w2)


def _shift_down(x, prev8, d):
    t, c = x.shape
    row = lax.broadcasted_iota(jnp.int32, (t, c), 0)
    xr = pltpu.roll(x, d, 0)
    pr = pltpu.roll(prev8, d, 0)
    if t > 8:
        pr = jnp.concatenate([pr, jnp.zeros((t - 8, c), x.dtype)], axis=0)
    return jnp.where(row < d, pr, xr)


def _shift_up(x, next8, d):
    t, c = x.shape
    row = lax.broadcasted_iota(jnp.int32, (t, c), 0)
    xr = pltpu.roll(x, t - d, 0)
    nr = pltpu.roll(next8, 8 - d, 0)
    if t > 8:
        nr = jnp.concatenate([jnp.zeros((t - 8, c), x.dtype), nr], axis=0)
    return jnp.where(row >= t - d, nr, xr)


def _scan_fwd(a, u):
    t = a.shape[0]
    row = lax.broadcasted_iota(jnp.int32, a.shape, 0)
    d = 1
    while d < t:
        a_s, u_s = pltpu.roll(a, d, 0), pltpu.roll(u, d, 0)
        m = row >= d
        u = jnp.where(m, u + a * u_s, u)
        a = jnp.where(m, a * a_s, a)
        d *= 2
    return a, u


def _scan_bwd(b, x):
    t = b.shape[0]
    row = lax.broadcasted_iota(jnp.int32, b.shape, 0)
    d = 1
    while d < t:
        b_s, x_s = pltpu.roll(b, t - d, 0), pltpu.roll(x, t - d, 0)
        m = row < t - d
        x = jnp.where(m, x + b * x_s, x)
        b = jnp.where(m, b * b_s, b)
        d *= 2
    return b, x


def _lru_elem(xc, pr, pi, b_r, b_i, lam):
    r = jax.nn.sigmoid(pr + b_r)
    ig = jax.nn.sigmoid(pi + b_i)
    softplus_neg_lam = jnp.maximum(-lam, 0.0) + jnp.log1p(jnp.exp(-jnp.abs(lam)))
    log_a = (-LRU_C) * r * softplus_neg_lam
    a = jnp.exp(log_a)
    one_minus_a2 = -jnp.tanh(log_a) * (jnp.exp(2.0 * log_a) + 1.0)
    u = jnp.sqrt(one_minus_a2) * (ig * xc)
    return a, u


def _gelu_tanh(x):
    return 0.5 * x * (1.0 + jnp.tanh(math.sqrt(2.0 / math.pi) * (x + 0.044715 * (x * x * x))))


def _conv_fwd(xb, prev8, w, b):
    out = b + w[3:4, :] * xb
    for d in range(1, CONV_W):
        out = out + w[3 - d:4 - d, :] * _shift_down(xb, prev8, d)
    return out


def _lru_fwd(xbg, conv_w, conv_b, w_rg, b_rg, w_ig, b_ig, lam, *, name):
    s = xbg.shape[0]
    c = LRU_W
    t = min(512, s)
    t8 = t // 8

    def body(xb_ref, gate_ref, prev_ref, cw_ref, cb_ref, wr_ref, br_ref, wi_ref, bi_ref, lam_ref,
             xc_ref, h_ref, y_ref, carry_ref):
        i = pl.program_id(0)

        @pl.when(i == 0)
        def _():
            carry_ref[...] = jnp.zeros_like(carry_ref)

        prev8 = jnp.where(i > 0, prev_ref[...], 0.0)
        xc = _conv_fwd(xb_ref[...], prev8, cw_ref[...], cb_ref[...])
        xcb = xc.astype(BF16)
        a, u = _lru_elem(xc, _dot(xcb, wr_ref[...]), _dot(xcb, wi_ref[...]), br_ref[...], bi_ref[...], lam_ref[...])
        a_cum, h0 = _scan_fwd(a, u)
        h = h0 + a_cum * carry_ref[0:1, :]
        carry_ref[...] = jnp.broadcast_to(h[t - 1:t, :], carry_ref.shape)
        xc_ref[...] = xc
        h_ref[...] = h
        y_ref[...] = h * _gelu_tanh(gate_ref[...])

    tile = lambda col: pl.BlockSpec((t, c), lambda i: (i, col))
    par = lambda rows: pl.BlockSpec((rows, c), lambda i: (0, 0))
    out_spec = pl.BlockSpec((t, c), lambda i: (i, 0))
    return pl.pallas_call(
        body, grid=(s // t,),
        in_specs=[tile(0), tile(1), pl.BlockSpec((8, c), lambda i: (jnp.maximum(i * t8 - 1, 0), 0)),
                  par(CONV_W), par(1), pl.BlockSpec((c, c), lambda i: (0, 0)), par(1),
                  pl.BlockSpec((c, c), lambda i: (0, 0)), par(1), par(1)],
        out_specs=[out_spec, out_spec, out_spec],
        out_shape=[jax.ShapeDtypeStruct((s, c), F32)] * 3,
        scratch_shapes=[pltpu.VMEM((8, c), F32)],
        compiler_params=_cparams(1, VMEM_LIMIT_MB), name=name)(
            xbg, xbg, xbg, conv_w, conv_b, w_rg, b_rg, w_ig, b_ig, lam)


def _lru_bwd(xc, xbg, hseq, dy, w_rg, b_rg, w_ig, b_ig, lam, *, name):
    s, c = xc.shape
    t = min(512, s)
    t8 = t // 8
    nt = s // t

    def body(xc_ref, gate_ref, h_ref, hprev_ref, dy_ref, wr_ref, br_ref, wi_ref, bi_ref, lam_ref,
             dxc_ref, dgate_ref, dwr_ref, dwi_ref, dbr_ref, dbi_ref, dlam_ref, a8_ref, l8_ref):
        i = pl.program_id(0)

        @pl.when(i == 0)
        def _():
            for r in (dwr_ref, dwi_ref, dbr_ref, dbi_ref, dlam_ref, a8_ref, l8_ref):
                r[...] = jnp.zeros_like(r)

        xc_v = xc_ref[...]
        xcb = xc_v.astype(BF16)
        (a, _), vjp = jax.vjp(_lru_elem, xc_v, _dot(xcb, wr_ref[...]), _dot(xcb, wi_ref[...]),
                              br_ref[...], bi_ref[...], lam_ref[...])
        gl, vjp_gelu = jax.vjp(_gelu_tanh, gate_ref[...])
        dyv, hs = dy_ref[...], h_ref[...]
        (dgate,) = vjp_gelu(dyv * hs)
        b_next = _shift_up(a, a8_ref[...], 1)
        b_cum, l0 = _scan_bwd(b_next, dyv * gl)
        lt = l0 + b_cum * l8_ref[0:1, :]
        a8_ref[...] = a[0:8, :]
        l8_ref[...] = lt[0:8, :]
        hprev8 = jnp.where(i < nt - 1, hprev_ref[...], 0.0)
        da = lt * _shift_down(hs, hprev8, 1)
        dxc_e, dpr, dpi, dbr, dbi, dlam = vjp((da, lt))
        dprb, dpib = dpr.astype(BF16), dpi.astype(BF16)
        dxc_ref[...] = dxc_e + _dot(dprb, wr_ref[...], NT) + _dot(dpib, wi_ref[...], NT)
        dgate_ref[...] = dgate
        xct = xc_v.T.astype(BF16)
        dwr_ref[...] += _dot(xct, dprb)
        dwi_ref[...] += _dot(xct, dpib)
        dbr_ref[...] += dbr
        dbi_ref[...] += dbi
        dlam_ref[...] += dlam

    rev = lambda col: pl.BlockSpec((t, c), lambda i: (nt - 1 - i, col))
    par = pl.BlockSpec((1, c), lambda i: (0, 0))
    sq = pl.BlockSpec((c, c), lambda i: (0, 0))
    return pl.pallas_call(
        body, grid=(nt,),
        in_specs=[rev(0), rev(1), rev(0), pl.BlockSpec((8, c), lambda i: (jnp.maximum((nt - 1 - i) * t8 - 1, 0), 0)),
                  rev(0), sq, par, sq, par, par],
        out_specs=[rev(0), rev(0), sq, sq, par, par, par],
        out_shape=[jax.ShapeDtypeStruct((s, c), F32)] * 2 + [jax.ShapeDtypeStruct((c, c), F32)] * 2
                  + [jax.ShapeDtypeStruct((1, c), F32)] * 3,
        scratch_shapes=[pltpu.VMEM((8, c), F32), pltpu.VMEM((8, c), F32)],
        compiler_params=_cparams(1, VMEM_LIMIT_MB), name=name)(
            xc, xbg, hseq, hseq, dy, w_rg, b_rg, w_ig, b_ig, lam)


def _conv_bwd(dxc, xbg, conv_w, *, name):
    s, c = dxc.shape
    t = min(512, s)
    t8 = t // 8
    nt = s // t

    def body(dxc_ref, next_ref, xb_ref, prev_ref, w_ref, dxb_ref, dw_ref, db_ref):
        i = pl.program_id(0)

        @pl.when(i == 0)
        def _():
            dw_ref[...] = jnp.zeros_like(dw_ref)
            db_ref[...] = jnp.zeros_like(db_ref)

        g = dxc_ref[...]
        xb = xb_ref[...]
        w = w_ref[...]
        next8 = jnp.where(i < nt - 1, next_ref[...], 0.0)
        prev8 = jnp.where(i > 0, prev_ref[...], 0.0)
        dxb = w[3:4, :] * g
        dw_ref[3:4, :] += jnp.sum(g * xb, axis=0, keepdims=True)
        for d in range(1, CONV_W):
            dxb = dxb + w[3 - d:4 - d, :] * _shift_up(g, next8, d)
            dw_ref[3 - d:4 - d, :] += jnp.sum(g * _shift_down(xb, prev8, d), axis=0, keepdims=True)
        dxb_ref[...] = dxb
        db_ref[...] += jnp.sum(g, axis=0, keepdims=True)

    tile = pl.BlockSpec((t, c), lambda i: (i, 0))
    return pl.pallas_call(
        body, grid=(nt,),
        in_specs=[tile, pl.BlockSpec((8, c), lambda i: (jnp.minimum((i + 1) * t8, s // 8 - 1), 0)),
                  tile, pl.BlockSpec((8, c), lambda i: (jnp.maximum(i * t8 - 1, 0), 0)),
                  pl.BlockSpec((CONV_W, c), lambda i: (0, 0))],
        out_specs=[tile, pl.BlockSpec((CONV_W, c), lambda i: (0, 0)), pl.BlockSpec((1, c), lambda i: (0, 0))],
        out_shape=[jax.ShapeDtypeStruct((s, c), F32), jax.ShapeDtypeStruct((CONV_W, c), F32),
                   jax.ShapeDtypeStruct((1, c), F32)],
        compiler_params=_cparams(1), name=name)(dxc, dxc, xbg, xbg, conv_w)


def _hi_lo_dot(x, tri):
    hi = x.astype(BF16)
    lo = (x - hi.astype(F32)).astype(BF16)
    return _dot(hi, tri) + _dot(lo, tri)


def _tri(tk, inclusive):
    r = lax.broadcasted_iota(jnp.int32, (tk, tk), 0)
    c = lax.broadcasted_iota(jnp.int32, (tk, tk), 1)
    return ((r >= c) if inclusive else (r > c)).astype(BF16)


def _split_heads(x):
    first = lax.broadcasted_iota(jnp.int32, x.shape, 1) < SB_HEAD
    zero = jnp.zeros_like(x)
    return first, (jnp.where(first, x, zero), jnp.where(first, zero, x))


def _sb_softplus_terms(z):
    ls = jnp.minimum(z, 0.0) - jnp.log(1.0 + jnp.exp(-jnp.abs(z)))
    return ls, ls - z


def _sb_alive(runs):
    m = runs[0]
    for r in runs[1:]:
        m = jnp.maximum(m, r)
    return jnp.max(m) > SB_SKIP


def _sb_fwd(q, kv, *, name, job=None):
    s = q.shape[0]
    tb = min(256, s)
    rc = min(SB_ROWS, tb)
    npair = SB_W // SB_PAIR
    chains = [(h, c) for h in range(2) for c in range(tb // rc)]

    def body(q_ref, k_ref, v_ref, o_ref):
        i = pl.program_id(1)
        first, qs = _split_heads(q_ref[...])
        tri = _tri(tb, False)
        causal = lax.broadcasted_iota(jnp.int32, (tb, tb), 1) < lax.broadcasted_iota(jnp.int32, (tb, tb), 0)
        causal_c = [causal[c * rc:(c + 1) * rc, :] for c in range(tb // rc)]

        def group(j_lo, width, carry, diag_last):
            k0 = pl.multiple_of(j_lo * tb, tb)
            kb = k_ref[pl.ds(k0, width * tb), :]
            vb = v_ref[pl.ds(k0, width * tb), :]
            zw = [_dot(qs[h][c * rc:(c + 1) * rc, :], kb, NT) for h, c in chains]
            cols = [slice(t * tb, (t + 1) * tb) for t in range(width)]
            terms = [[_sb_softplus_terms(z[:, cl]) for cl in cols] for z in zw]
            lss = [[ls for ls, _ in tc] for tc in terms]
            ns = [[n for _, n in tc] for tc in terms]
            if diag_last:
                ns = [nc[:-1] + [jnp.where(causal_c[c], nc[-1], 0.0)] for nc, (h, c) in zip(ns, chains)]
            sufs = [[_hi_lo_dot(n, tri) for n in nc] for nc in ns]
            out = []
            for (h, c), lc, nc, sc, (run, acc) in zip(chains, lss, ns, sufs, carry):
                ws = [None] * width
                for t in reversed(range(width)):
                    w = jnp.exp(lc[t] + (sc[t] + run))
                    if diag_last and t == width - 1:
                        w = jnp.where(causal_c[c], w, 0.0)
                    ws[t] = w.astype(BF16)
                    run = run + jnp.sum(nc[t], axis=1, keepdims=True)
                out.append((run, acc + _dot(ws[0] if width == 1 else jnp.concatenate(ws, axis=1), vb)))
            return tuple(out)

        zero = (jnp.zeros((rc, 1), F32), jnp.zeros((rc, SB_PAIR), F32))
        carry = lax.cond(i == 0, lambda c: group(0, 1, c, True), lambda c: group(i - 1, 2, c, True),
                         (zero,) * len(chains))

        def step(st):
            j, _, c = st
            c = group(j, 1, c, False)
            return j - 1, _sb_alive([run for run, _ in c]), c

        _, _, carry = lax.while_loop(lambda st: jnp.logical_and(st[0] >= 0, st[1]), step,
                                     (i - 2, _sb_alive([run for run, _ in carry]), carry))
        accs = [jnp.concatenate([carry[n][1] for n, (h, c) in enumerate(chains) if h == hh], axis=0) for hh in range(2)]
        o_ref[...] = jnp.where(first, accs[0], accs[1])

    (o,), job_out = _pallas_with_job(
        body, job, lambda: pl.program_id(0) * (s // tb) + pl.program_id(1), npair * (s // tb),
        grid=(npair, s // tb),
        in_specs=[pl.BlockSpec((tb, SB_PAIR), lambda p, i: (i, p)),
                  pl.BlockSpec((s, SB_PAIR), lambda p, i: (0, p)),
                  pl.BlockSpec((s, SB_PAIR), lambda p, i: (0, npair + p))],
        out_specs=[pl.BlockSpec((tb, SB_PAIR), lambda p, i: (i, p))],
        out_shape=[jax.ShapeDtypeStruct((s, SB_W), F32)],
        compiler_params=_cparams(2, VMEM_LIMIT_MB), name=name, args=(q, kv, kv))
    return o, job_out


def _sb_bwd(q, kv, o, do, *, name, job=None):
    s = q.shape[0]
    tb = min(256, s)
    rc = min(SB_ROWS, tb)
    npair = SB_W // SB_PAIR
    chains = [(h, c) for h in range(2) for c in range(tb // rc)]

    def body(q_ref, k_ref, v_ref, o_ref, do_ref, dq_ref, dk_ref, dv_ref):
        i = pl.program_id(1)

        @pl.when(i == 0)
        def _():
            dk_ref[...] = jnp.zeros_like(dk_ref)
            dv_ref[...] = jnp.zeros_like(dv_ref)

        first, qs = _split_heads(q_ref[...])
        dob = do_ref[...].astype(BF16)
        _, dos = _split_heads(dob)
        prod = dob.astype(F32) * o_ref[...]
        deltas = (jnp.sum(jnp.where(first, prod, 0.0), axis=1, keepdims=True),
                  jnp.sum(jnp.where(first, 0.0, prod), axis=1, keepdims=True))
        tri_x, tri_i = _tri(tb, False), _tri(tb, True)
        causal = lax.broadcasted_iota(jnp.int32, (tb, tb), 1) < lax.broadcasted_iota(jnp.int32, (tb, tb), 0)
        causal_c = [causal[c * rc:(c + 1) * rc, :] for c in range(tb // rc)]
        rows = [slice(c * rc, (c + 1) * rc) for c in range(tb // rc)]

        def group(j_lo, width, carry, diag_last):
            k0 = pl.multiple_of(j_lo * tb, tb)
            kb = k_ref[pl.ds(k0, width * tb), :]
            vb = v_ref[pl.ds(k0, width * tb), :]
            cols = [slice(t * tb, (t + 1) * tb) for t in range(width)]
            zw = [_dot(qs[h][rows[c], :], kb, NT) for h, c in chains]
            dww = [_dot(dos[h][rows[c], :], vb, NT) for h, c in chains]
            terms = [[_sb_softplus_terms(z[:, cl]) for cl in cols] for z in zw]
            lss = [[ls for ls, _ in tc] for tc in terms]
            ns = [[n for _, n in tc] for tc in terms]
            if diag_last:
                ns = [nc[:-1] + [jnp.where(causal_c[c], nc[-1], 0.0)] for nc, (h, c) in zip(ns, chains)]
            sufs = [[_hi_lo_dot(n, tri_x) for n in nc] for nc in ns]
            wbs, gs, runs_n = [], [], []
            for (h, c), lc, nc, sc, dw, (run_n, _, _) in zip(chains, lss, ns, sufs, dww, carry):
                wb, g = [None] * width, [None] * width
                for t in reversed(range(width)):
                    w = jnp.exp(lc[t] + (sc[t] + run_n))
                    if diag_last and t == width - 1:
                        w = jnp.where(causal_c[c], w, 0.0)
                    wb[t] = w.astype(BF16)
                    g[t] = wb[t].astype(F32) * dw[:, cols[t]]
                    run_n = run_n + jnp.sum(nc[t], axis=1, keepdims=True)
                wbs.append(wb)
                gs.append(g)
                runs_n.append(run_n)
            gsufs = [[_hi_lo_dot(g, tri_i) for g in gc] for gc in gs]
            out = []
            dk_t = [jnp.zeros((tb, SB_PAIR), F32) for _ in range(width)]
            dv_t = [jnp.zeros((tb, SB_PAIR), F32) for _ in range(width)]
            for (h, c), lc, gc, gsc, wb, run_n, (_, run_g, dq_acc) in zip(chains, lss, gs, gsufs, wbs, runs_n, carry):
                qh, doh, delta = qs[h][rows[c], :], dos[h][rows[c], :], deltas[h][rows[c], :]
                dzb = [None] * width
                for t in reversed(range(width)):
                    pre = delta - (gsc[t] + run_g)
                    dz = gc[t] - jnp.exp(lc[t]) * (gc[t] + pre)
                    if diag_last and t == width - 1:
                        dz = jnp.where(causal_c[c], dz, 0.0)
                    run_g = run_g + jnp.sum(gc[t], axis=1, keepdims=True)
                    dzb[t] = dz.astype(BF16)
                    dk_t[t] = dk_t[t] + _dot(dz.T.astype(BF16), qh)
                    dv_t[t] = dv_t[t] + _dot(wb[t].astype(F32).T.astype(BF16), doh)
                out.append((run_n, run_g, dq_acc + _dot(dzb[0] if width == 1 else jnp.concatenate(dzb, axis=1), kb)))
            for t in range(width):
                dk_ref[pl.ds(pl.multiple_of((j_lo + t) * tb, tb), tb), :] += dk_t[t]
                dv_ref[pl.ds(pl.multiple_of((j_lo + t) * tb, tb), tb), :] += dv_t[t]
            return tuple(out)

        zero = (jnp.zeros((rc, 1), F32), jnp.zeros((rc, 1), F32), jnp.zeros((rc, SB_PAIR), F32))
        carry = lax.cond(i == 0, lambda c: group(0, 1, c, True), lambda c: group(i - 1, 2, c, True),
                         (zero,) * len(chains))

        def step(st):
            j, _, c = st
            c = group(j, 1, c, False)
            return j - 1, _sb_alive([r[0] for r in c]), c

        _, _, carry = lax.while_loop(lambda st: jnp.logical_and(st[0] >= 0, st[1]), step,
                                     (i - 2, _sb_alive([r[0] for r in carry]), carry))
        dqs = [jnp.concatenate([carry[n][2] for n, (h, c) in enumerate(chains) if h == hh], axis=0) for hh in range(2)]
        dq_ref[...] = jnp.where(first, dqs[0], dqs[1]) * (1.0 / math.sqrt(SB_HEAD))

    qtile = pl.BlockSpec((tb, SB_PAIR), lambda p, i: (i, p))
    col = pl.BlockSpec((s, SB_PAIR), lambda p, i: (0, p))
    return _pallas_with_job(
        body, job, lambda: pl.program_id(0) * (s // tb) + pl.program_id(1), npair * (s // tb),
        grid=(npair, s // tb),
        in_specs=[qtile, col, pl.BlockSpec((s, SB_PAIR), lambda p, i: (0, npair + p)), qtile, qtile],
        out_specs=[qtile, col, col],
        out_shape=[jax.ShapeDtypeStruct((s, SB_W), F32)] * 3,
        compiler_params=_cparams(2, VMEM_LIMIT_MB), name=name, args=(q, kv, kv, o, do))


def _mixnorm_math(yl, ys, gl, gs):
    def rms(x, g):
        return x * lax.rsqrt(jnp.mean(x * x, axis=-1, keepdims=True) + RMS_EPS) * g
    return rms(yl, gl), rms(ys, gs)


def _mixnorm_fwd(yl, ys, gl, gs, *, name):
    s, c = yl.shape
    tm = min(512, s)

    def body(yl_ref, ys_ref, gl_ref, gs_ref, y_ref):
        a, b = _mixnorm_math(yl_ref[...], ys_ref[...], gl_ref[...], gs_ref[...])
        y_ref[:, 0:c] = a
        y_ref[:, c:2 * c] = b

    return pl.pallas_call(
        body, grid=(s // tm,), in_specs=[_row_spec(tm, c), _row_spec(tm, c), _par_spec(c), _par_spec(c)],
        out_specs=_row_spec(tm, 2 * c), out_shape=jax.ShapeDtypeStruct((s, 2 * c), F32),
        compiler_params=_cparams(1), name=name)(yl, ys, gl, gs)


def _mixnorm_bwd(yl, ys, gl, gs, dy, *, name, job=None):
    s, c = yl.shape
    tm = min(512, s)

    def body(yl_ref, ys_ref, gl_ref, gs_ref, dy_ref, dyl_ref, dys_ref, dgl_ref, dgs_ref):
        @pl.when(pl.program_id(0) == 0)
        def _():
            dgl_ref[...] = jnp.zeros_like(dgl_ref)
            dgs_ref[...] = jnp.zeros_like(dgs_ref)

        _, vjp = jax.vjp(_mixnorm_math, yl_ref[...], ys_ref[...], gl_ref[...], gs_ref[...])
        dyl, dys, dgl, dgs = vjp((dy_ref[:, 0:c], dy_ref[:, c:2 * c]))
        dyl_ref[...] = dyl
        dys_ref[...] = dys
        dgl_ref[...] += dgl
        dgs_ref[...] += dgs

    return _pallas_with_job(
        body, job, lambda: pl.program_id(0), s // tm, grid=(s // tm,),
        in_specs=[_row_spec(tm, c), _row_spec(tm, c), _par_spec(c), _par_spec(c), _row_spec(tm, 2 * c)],
        out_specs=[_row_spec(tm, c), _row_spec(tm, c), _par_spec(c), _par_spec(c)],
        out_shape=[jax.ShapeDtypeStruct((s, c), F32)] * 2 + [jax.ShapeDtypeStruct((1, c), F32)] * 2,
        compiler_params=_cparams(1), name=name, args=(yl, ys, gl, gs, dy))


def _cross_probs(qh, kh):
    sc = _dot(qh, kh, NT) * (1.0 / math.sqrt(MEM_HD))
    e = jnp.exp(sc - jnp.max(sc, axis=-1, keepdims=True))
    return e / jnp.sum(e, axis=-1, keepdims=True)


def _cross_fwd(q, kv, *, name):
    s, d = q.shape
    mlen = kv.shape[1]
    tm = min(512, s)

    def body(q_ref, kv_ref, o_ref):
        for h in range(MEM_HEADS):
            cols = slice(h * MEM_HD, (h + 1) * MEM_HD)
            p = _cross_probs(q_ref[:, cols].astype(BF16), kv_ref[h].astype(BF16))
            o_ref[:, cols] = _dot(p.astype(BF16), kv_ref[MEM_HEADS + h].astype(BF16))

    return pl.pallas_call(
        body, grid=(s // tm,),
        in_specs=[_row_spec(tm, d), pl.BlockSpec((2 * MEM_HEADS, mlen, MEM_HD), lambda i: (0, 0, 0))],
        out_specs=_row_spec(tm, d), out_shape=jax.ShapeDtypeStruct((s, d), F32),
        compiler_params=_cparams(1, VMEM_LIMIT_MB), name=name)(q, kv)


def _cross_bwd(q, kv, do, *, name):
    s, d = q.shape
    mlen = kv.shape[1]
    tm = min(512, s)

    def body(q_ref, kv_ref, do_ref, dq_ref, dkv_ref):
        @pl.when(pl.program_id(0) == 0)
        def _():
            dkv_ref[...] = jnp.zeros_like(dkv_ref)

        for h in range(MEM_HEADS):
            cols = slice(h * MEM_HD, (h + 1) * MEM_HD)
            qh = q_ref[:, cols].astype(BF16)
            kh = kv_ref[h].astype(BF16)
            doh = do_ref[:, cols].astype(BF16)
            p = _cross_probs(qh, kh)
            dp = _dot(doh, kv_ref[MEM_HEADS + h].astype(BF16), NT)
            ds = p * (dp - jnp.sum(dp * p, axis=-1, keepdims=True)) * (1.0 / math.sqrt(MEM_HD))
            dq_ref[:, cols] = _dot(ds.astype(BF16), kh)
            dkv_ref[h] += _dot(ds.T.astype(BF16), qh)
            dkv_ref[MEM_HEADS + h] += _dot(p.T.astype(BF16), doh)

    kv_spec = pl.BlockSpec((2 * MEM_HEADS, mlen, MEM_HD), lambda i: (0, 0, 0))
    return pl.pallas_call(
        body, grid=(s // tm,), in_specs=[_row_spec(tm, d), kv_spec, _row_spec(tm, d)],
        out_specs=[_row_spec(tm, d), kv_spec],
        out_shape=[jax.ShapeDtypeStruct((s, d), F32), jax.ShapeDtypeStruct(kv.shape, F32)],
        compiler_params=_cparams(1, VMEM_LIMIT_MB), name=name)(q, kv, do)


def _proj_bwd_dx(parts, w_in, dz, *, name):
    s = dz.shape[0]
    d = D_MODEL
    tm = min(256, s)
    widths = [p.shape[1] for p in parts]

    def body(*refs):
        part_refs, w_ref, dz_ref, o_ref = refs[:len(parts)], refs[len(parts)], refs[-2], refs[-1]
        acc = ALPHA * dz_ref[...]
        off = 0
        for p_ref, wd in zip(part_refs, widths):
            acc = acc + _dot(p_ref[...].astype(BF16), w_ref[:, off:off + wd], NT)
            off += wd
        o_ref[...] = acc

    return pl.pallas_call(
        body, grid=(s // tm,),
        in_specs=[_row_spec(tm, wd) for wd in widths] + [pl.BlockSpec(w_in.shape, lambda i: (0, 0)), _row_spec(tm, d)],
        out_specs=_row_spec(tm, d), out_shape=jax.ShapeDtypeStruct((s, d), F32),
        compiler_params=_cparams(1, VMEM_LIMIT_MB), name=name)(*parts, w_in, dz)


def _mesh_pos():
    return lax.axis_index("x"), lax.axis_index("y"), lax.axis_index("c")


class _GatherJob:
    n_phases = 3

    def __init__(self, shards):
        n = len(shards)
        self.inputs = list(shards)
        self.out_shape = [jax.ShapeDtypeStruct((N_DEV,) + a.shape, a.dtype) for a in shards]
        self.scratch = [pltpu.SemaphoreType.DMA((n, 7)), pltpu.SemaphoreType.DMA((n, 7)), pltpu.SemaphoreType.DMA((n,))]

    def phase(self, k, ins, outs, sems):
        send_sems, recv_sems, local_sems = sems
        n = len(ins)
        x, y, c = _mesh_pos()
        me, sibling = (x, y, c), (x, y, 1 - c)
        chips = [(1 - x, y), (x, 1 - y), (1 - x, 1 - y)]

        def copy(a, slot, block, to, src=None):
            dst = outs[a].at[4 * block[0] + 2 * block[1] + block[2]]
            return pltpu.make_async_remote_copy(
                src_ref=dst if src is None else src, dst_ref=dst,
                send_sem=send_sems.at[a, slot], recv_sem=recv_sems.at[a, slot], device_id=to, device_id_type=MESH)

        def mine():
            return [pltpu.make_async_copy(ins[a], outs[a].at[4 * x + 2 * y + c], local_sems.at[a]) for a in range(n)]

        def first():
            return [cp for a in range(n) for cp in
                    [copy(a, 0, me, sibling, src=ins[a])] + [copy(a, 1 + j, me, (*chip, c), src=ins[a]) for j, chip in enumerate(chips)]]

        def passed(a, j):
            return copy(a, 4 + j, (*chips[j], c), sibling)

        if k == 0:
            for cp in mine() + first():
                cp.start()
        elif k == 1:
            for a in range(n):
                for j, chip in enumerate(chips):
                    copy(a, 1 + j, (*chip, c), me).wait_recv()
                    passed(a, j).start()
        else:
            for a in range(n):
                copy(a, 0, sibling, me).wait_recv()
                for j, chip in enumerate(chips):
                    copy(a, 4 + j, (*chip, 1 - c), me).wait_recv()
            for cp in first() + [passed(a, j) for a in range(n) for j in range(3)]:
                cp.wait_send()
            for cp in mine():
                cp.wait()


class _SiblingJob:
    n_phases = 2

    def __init__(self, grads):
        n, half = len(grads), N_DEV // 2
        self.inputs = list(grads)
        self.out_shape = [jax.ShapeDtypeStruct((half,) + g.shape[1:], g.dtype) for g in grads]
        self.scratch = [pltpu.SemaphoreType.DMA((n, half)), pltpu.SemaphoreType.DMA((n, half))]

    def phase(self, k, ins, outs, sems):
        send_sems, recv_sems = sems
        x, y, c = _mesh_pos()
        copies = [pltpu.make_async_remote_copy(
            src_ref=ins[a].at[2 * j + (1 - c)], dst_ref=outs[a].at[j], send_sem=send_sems.at[a, j],
            recv_sem=recv_sems.at[a, j], device_id=(x, y, 1 - c), device_id_type=MESH)
            for a in range(len(ins)) for j in range(N_DEV // 2)]
        for cp in copies:
            if k == 0:
                cp.start()
            else:
                cp.wait()


class _ChipJob:
    n_phases = 2

    def __init__(self, partials):
        n, half = len(partials), N_DEV // 2
        self.inputs = list(partials)
        self.out_shape = [jax.ShapeDtypeStruct(p.shape, p.dtype) for p in partials]
        self.scratch = [pltpu.SemaphoreType.DMA((n, half)), pltpu.SemaphoreType.DMA((n, half)), pltpu.SemaphoreType.DMA((n,))]

    def phase(self, k, ins, outs, sems):
        send_sems, recv_sems, local_sems = sems
        n, half = len(ins), N_DEV // 2
        x, y, c = _mesh_pos()
        jme = 2 * x + y
        local = [pltpu.make_async_copy(ins[a].at[jme], outs[a].at[jme], local_sems.at[a]) for a in range(n)]
        for cp in local:
            if k == 0:
                cp.start()
            else:
                cp.wait()
        for j in range(half):
            @pl.when(jme != j)
            def _(j=j):
                for a in range(n):
                    if k == 0:
                        pltpu.make_async_remote_copy(
                            src_ref=ins[a].at[j], dst_ref=outs[a].at[jme], send_sem=send_sems.at[a, j],
                            recv_sem=recv_sems.at[a, jme], device_id=(j // 2, j % 2, c), device_id_type=MESH).start()
                    else:
                        arrive = pltpu.make_async_remote_copy(
                            src_ref=ins[a].at[j], dst_ref=outs[a].at[j], send_sem=send_sems.at[a, j],
                            recv_sem=recv_sems.at[a, j], device_id=(j // 2, j % 2, c), device_id_type=MESH)
                        arrive.wait_recv()
                        arrive.wait_send()


def _phase_steps(n_phases, n_steps):
    return [0, n_steps - 1] if n_phases == 2 else [0, n_steps // 2, n_steps - 1]


def _pallas_with_job(body, job, step_fn, n_steps, *, grid, in_specs, out_specs, out_shape, scratch_shapes=(),
                     compiler_params, name, args):
    in_specs, out_specs, out_shape, scratch_shapes = list(in_specs), list(out_specs), list(out_shape), list(scratch_shapes)
    if job is None:
        outs = pl.pallas_call(body, grid=grid, in_specs=in_specs, out_specs=out_specs, out_shape=out_shape,
                              scratch_shapes=scratch_shapes, compiler_params=compiler_params, name=name)(*args)
        return list(outs), None
    ni, no, ns = len(in_specs), len(out_specs), len(scratch_shapes)
    ji, jo = len(job.inputs), len(job.out_shape)
    steps = _phase_steps(job.n_phases, n_steps)

    def wrapped(*refs):
        own_in, job_in = refs[:ni], refs[ni:ni + ji]
        own_out, job_out = refs[ni + ji:ni + ji + no], refs[ni + ji + no:ni + ji + no + jo]
        own_scr, job_scr = refs[ni + ji + no + jo:ni + ji + no + jo + ns], refs[ni + ji + no + jo + ns:]
        step = step_fn()
        for k, at in enumerate(steps):
            @pl.when(step == at)
            def _(k=k):
                job.phase(k, job_in, job_out, job_scr)
        body(*own_in, *own_out, *own_scr)

    any_spec = pl.BlockSpec(memory_space=pl.ANY)
    outs = pl.pallas_call(
        wrapped, grid=grid, in_specs=in_specs + [any_spec] * ji, out_specs=out_specs + [any_spec] * jo,
        out_shape=out_shape + list(job.out_shape), scratch_shapes=scratch_shapes + list(job.scratch),
        compiler_params=compiler_params, name=name)(*args, *job.inputs)
    return list(outs[:no]), list(outs[no:])


def _run_job(job, *, name):
    ji, jo = len(job.inputs), len(job.out_shape)

    def body(*refs):
        for k in range(job.n_phases):
            job.phase(k, refs[:ji], refs[ji:ji + jo], refs[ji + jo:])

    any_spec = pl.BlockSpec(memory_space=pl.ANY)
    return pl.pallas_call(body, in_specs=[any_spec] * ji, out_specs=[any_spec] * jo, out_shape=list(job.out_shape),
                          scratch_shapes=list(job.scratch), name=name)(*job.inputs)


def _pair_add(grad, recv, c_idx, *, name):
    half, r, cdim = recv.shape
    tr = _row_tile(r, 512)

    def body(c_ref, g_ref, r_ref, t_ref, tb_ref):
        t = g_ref[...] + r_ref[...]
        t_ref[...] = t
        tb_ref[...] = t.astype(BF16)

    grid_spec = pltpu.PrefetchScalarGridSpec(
        num_scalar_prefetch=1, grid=(half, r // tr),
        in_specs=[pl.BlockSpec((None, tr, cdim), lambda j, i, c_ref: (2 * j + c_ref[0], i, 0)),
                  pl.BlockSpec((None, tr, cdim), lambda j, i, c_ref: (j, i, 0))],
        out_specs=[pl.BlockSpec((None, tr, cdim), lambda j, i, c_ref: (j, i, 0))] * 2)
    return pl.pallas_call(
        body, grid_spec=grid_spec,
        out_shape=[jax.ShapeDtypeStruct(recv.shape, F32), jax.ShapeDtypeStruct(recv.shape, BF16)],
        compiler_params=_cparams(2), name=name)(c_idx, grad, recv)


def _adam_math(w, g, m, v):
    m = ADAM_B1 * m + (1.0 - ADAM_B1) * g
    v = ADAM_B2 * v + (1.0 - ADAM_B2) * (g * g)
    m_hat = m / (1.0 - ADAM_B1 ** ADAM_STEP)
    v_hat = v / (1.0 - ADAM_B2 ** ADAM_STEP)
    delta = -ADAM_LR * (m_hat / (jnp.sqrt(v_hat) + ADAM_EPS) + ADAM_WD * w)
    return delta, m, v


def _reduce_adam(own, recv, j_idx, w, m, v, *, name):
    half, r, cdim = recv.shape
    tr = _row_tile(r, 256)

    def body(j_ref, own_ref, recv_ref, w_ref, m_ref, v_ref, g_ref, d_ref, nm_ref, nv_ref):
        jme = j_ref[0]
        g = own_ref[...]
        for sl in range(half):
            g = g + jnp.where(sl == jme, 0.0, recv_ref[sl].astype(F32))
        delta, nm, nv = _adam_math(w_ref[...], g, m_ref[...], v_ref[...])
        g_ref[...] = g
        d_ref[...] = delta
        nm_ref[...] = nm
        nv_ref[...] = nv

    tile = pl.BlockSpec((tr, cdim), lambda i, j_ref: (i, 0))
    grid_spec = pltpu.PrefetchScalarGridSpec(
        num_scalar_prefetch=1, grid=(r // tr,),
        in_specs=[pl.BlockSpec((None, tr, cdim), lambda i, j_ref: (j_ref[0], i, 0)),
                  pl.BlockSpec((half, tr, cdim), lambda i, j_ref: (0, i, 0)), tile, tile, tile],
        out_specs=[tile] * 4)
    return pl.pallas_call(
        body, grid_spec=grid_spec, out_shape=[jax.ShapeDtypeStruct((r, cdim), F32)] * 4,
        compiler_params=_cparams(1), name=name)(j_idx, own, recv, w, m, v)


def _small_all_reduce(pack, *, name):
    r, cdim = pack.shape

    def body(p_ref, o_ref, land_ref, send_sems, recv_sems):
        x, y, c = _mesh_pos()
        me = 4 * x + 2 * y + c
        land_ref[me] = p_ref[...]
        for t in range(N_DEV):
            @pl.when(me != t)
            def _(t=t):
                pltpu.make_async_remote_copy(
                    src_ref=p_ref, dst_ref=land_ref.at[me], send_sem=send_sems.at[t], recv_sem=recv_sems.at[me],
                    device_id=(t // 4, (t // 2) % 2, t % 2), device_id_type=MESH).start()
        for t in range(N_DEV):
            @pl.when(me != t)
            def _(t=t):
                cp = pltpu.make_async_remote_copy(
                    src_ref=p_ref, dst_ref=land_ref.at[t], send_sem=send_sems.at[t], recv_sem=recv_sems.at[t],
                    device_id=(t // 4, (t // 2) % 2, t % 2), device_id_type=MESH)
                cp.wait_recv()
                cp.wait_send()
        total = land_ref[0]
        for t in range(1, N_DEV):
            total = total + land_ref[t]
        o_ref[...] = total

    vmem = pl.BlockSpec(memory_space=pltpu.VMEM)
    return pl.pallas_call(
        body, in_specs=[vmem], out_specs=vmem, out_shape=jax.ShapeDtypeStruct((r, cdim), F32),
        scratch_shapes=[pltpu.VMEM((N_DEV, r, cdim), F32), pltpu.SemaphoreType.DMA((N_DEV,)),
                        pltpu.SemaphoreType.DMA((N_DEV,))],
        name=name)(pack)


def _adam_small(w, g, m, v, *, name):
    def body(w_ref, g_ref, m_ref, v_ref, d_ref, nm_ref, nv_ref):
        d_ref[...], nm_ref[...], nv_ref[...] = _adam_math(w_ref[...], g_ref[...], m_ref[...], v_ref[...])

    vmem = pl.BlockSpec(memory_space=pltpu.VMEM)
    return pl.pallas_call(
        body, in_specs=[vmem] * 4, out_specs=[vmem] * 3, out_shape=[jax.ShapeDtypeStruct(w.shape, F32)] * 3,
        name=name)(w, g, m, v)


def _block_diag(w):
    h, a, b = w.shape
    eye = jnp.eye(h, dtype=w.dtype)
    return (eye[:, None, :, None] * w[:, :, None, :]).reshape(h * a, h * b)


def _diag_blocks(w, h):
    a = w.shape[0] // h
    return jnp.stack([w[i * a:(i + 1) * a, i * a:(i + 1) * a] for i in range(h)])


def _local_step(x, mem, target, p, hooks=None):
    g = {}
    p = dict(p)
    w_rg, w_ig = _block_diag(p["w_rgate"]).astype(BF16), _block_diag(p["w_igate"]).astype(BF16)

    (gu1, f1), arrived = _ffn_fwd(x, p["ffn1_w13"], p["ffn1_w2"], name="ffn1_fwd",
                                  job=hooks and hooks.gather_job(_GATHER_MIXER))
    if hooks:
        p.update(hooks.gathered(_GATHER_MIXER, arrived))
    w_in = p["w_in"]
    w_in_lru, w_in_q, w_in_kv = w_in[:, :2 * LRU_W], w_in[:, 2 * LRU_W:2 * LRU_W + SB_W], w_in[:, 2 * LRU_W + SB_W:]
    h1 = _ln_fwd(x, f1, 0.5, p["ln1_g"], p["ln1_b"], name="ln1_fwd")
    xbg = _matmul(h1, w_in_lru, name="proj_lru")
    q_sb = _matmul(h1, w_in_q, out_dtype=BF16, scale=1.0 / math.sqrt(SB_HEAD), name="proj_q")
    kv_sb = _matmul(h1, w_in_kv, out_dtype=BF16, name="proj_kv")
    xc, hseq, y_lru = _lru_fwd(xbg, p["conv_w"], p["conv_b"], w_rg, p["b_rgate"], w_ig, p["b_igate"],
                               p["lru_lambda"], name="lru_fwd")
    y_sb, arrived = _sb_fwd(q_sb, kv_sb, name="sb_fwd", job=hooks and hooks.gather_job(_GATHER_FFN2))
    if hooks:
        p.update(hooks.gathered(_GATHER_FFN2, arrived))
    ymix = _mixnorm_fwd(y_lru, y_sb, p["g_lru"], p["g_sb"], name="mixnorm_fwd")
    mix = _matmul(ymix, p["w_out"], name="mix_out")
    h2 = _ln_fwd(h1, mix, 1.0, p["ln2_g"], p["ln2_b"], name="ln2_fwd")
    qm = _matmul(h2, p["mem_wq"], name="cross_q")
    kv = _matmul(mem, p["mem_wkv"], name="cross_kv")
    o_cross = _cross_fwd(qm, kv, name="cross_fwd")
    cross = _matmul(o_cross, p["mem_wo"], name="cross_out")
    h3 = _ln_fwd(h2, cross, 1.0, p["ln3_g"], p["ln3_b"], name="ln3_fwd")
    (gu2, f2), _ = _ffn_fwd(h3, p["ffn2_w13"], p["ffn2_w2"], name="ffn2_fwd")

    dz4, g["ln4_g"], g["ln4_b"], loss = _ln_bwd(h3, f2, 0.5, p["ln4_g"], p["ln4_b"], None, target=target, name="ln4_loss_bwd")
    dh3, dgu2, act2 = _ffn_bwd(dz4, gu2, p["ffn2_w13"], p["ffn2_w2"], name="ffn2_bwd")
    g["ffn2_w13"] = _matmul(h3, dgu2, trans_a=True, tm=1024, tn=FF_CHUNK, tk=2048, name="ffn2_dw13")
    g["ffn2_w2"] = _matmul(act2, dz4, trans_a=True, tm=FF_CHUNK, tn=1024, tk=2048, scale=0.5, name="ffn2_dw2")
    dz3, g["ln3_g"], g["ln3_b"] = _ln_bwd(h2, cross, 1.0, p["ln3_g"], p["ln3_b"], dh3, name="ln3_bwd")
    g["mem_wo"] = _matmul(o_cross, dz3, trans_a=True, tm=1024, tn=1024, tk=1024, name="cross_dwo")
    do_cross = _matmul(dz3, p["mem_wo"], trans_b=True, name="cross_do")
    dqm, dkv = _cross_bwd(qm, kv, do_cross, name="cross_bwd")
    g["mem_wq"] = _matmul(h2, dqm, trans_a=True, tm=1024, tn=1024, tk=1024, name="cross_dwq")
    g["mem_wkv"] = _matmul(mem, dkv, trans_a=True, tm=1024, tn=MEM_HD, tk=256, name="cross_dwkv")
    dh2 = _matmul(dqm, p["mem_wq"], trans_b=True, res=dz3, res_coeff=ALPHA, name="cross_dh")
    dz2, g["ln2_g"], g["ln2_b"] = _ln_bwd(h1, mix, 1.0, p["ln2_g"], p["ln2_b"], dh2, name="ln2_bwd")
    g["w_out"] = _matmul(ymix, dz2, trans_a=True, tm=1024, tn=1024, tk=1024, name="mix_dwout")
    dymix = _matmul(dz2, p["w_out"], trans_b=True, name="mix_dy")
    (dy_lru, dy_sb, g["g_lru"], g["g_sb"]), arrived = _mixnorm_bwd(
        y_lru, y_sb, p["g_lru"], p["g_sb"], dymix, name="mixnorm_bwd", job=hooks and hooks.sibling_job(_REDUCE_EARLY, g))
    (dq, dk, dv), arrived = _sb_bwd(q_sb, kv_sb, y_sb, dy_sb, name="sb_bwd",
                                    job=hooks and hooks.chip_job(_REDUCE_EARLY, arrived))
    if hooks:
        hooks.reduced(_REDUCE_EARLY, arrived)
    dxc, dgate, dwr, dwi, g["b_rgate"], g["b_igate"], g["lru_lambda"] = _lru_bwd(
        xc, xbg, hseq, dy_lru, w_rg, p["b_rgate"], w_ig, p["b_igate"], p["lru_lambda"], name="lru_bwd")
    g["w_rgate"], g["w_igate"] = _diag_blocks(dwr, 8), _diag_blocks(dwi, 8)
    dxb, g["conv_w"], g["conv_b"] = _conv_bwd(dxc, xbg, p["conv_w"], name="conv_bwd")
    parts = [dxb, dgate, dq, dk, dv]
    g["w_in"] = jnp.concatenate(
        [_matmul(h1, part, trans_a=True, tm=1024, tn=512, tk=1024, name=f"proj_dw{n}") for n, part in enumerate(parts)], axis=1)
    dh1 = _proj_bwd_dx(parts, w_in, dz2, name="proj_dh")
    dz1, g["ln1_g"], g["ln1_b"] = _ln_bwd(x, f1, 0.5, p["ln1_g"], p["ln1_b"], dh1, name="ln1_bwd")
    grad_x, dgu1, act1 = _ffn_bwd(dz1, gu1, p["ffn1_w13"], p["ffn1_w2"], name="ffn1_bwd")
    g["ffn1_w13"] = _matmul(x, dgu1, trans_a=True, tm=1024, tn=FF_CHUNK, tk=2048, name="ffn1_dw13")
    g["ffn1_w2"] = _matmul(act1, dz1, trans_a=True, tm=FF_CHUNK, tn=1024, tk=2048, scale=0.5, name="ffn1_dw2")
    return loss, grad_x, g


_WEIGHTS = ['ffn1_w13', 'ffn1_w2', 'ln1_g', 'ln1_b', 'w_in', 'conv_w', 'conv_b', 'w_rgate', 'b_rgate', 'w_igate',
            'b_igate', 'lru_lambda', 'g_lru', 'g_sb', 'w_out', 'ln2_g', 'ln2_b', 'mem_wq', 'mem_wkv', 'mem_wo',
            'ln3_g', 'ln3_b', 'ffn2_w13', 'ffn2_w2', 'ln4_g', 'ln4_b']
_SHARDED = ['ffn1_w13', 'ffn1_w2', 'w_in', 'w_out', 'mem_wq', 'mem_wkv', 'mem_wo', 'ffn2_w13', 'ffn2_w2']
_SMALL = ['ln1_g', 'ln1_b', 'ln2_g', 'ln2_b', 'ln3_g', 'ln3_b', 'ln4_g', 'ln4_b', 'conv_b', 'b_rgate', 'b_igate',
          'lru_lambda', 'g_lru', 'g_sb', 'w_rgate', 'w_igate']


def _pack_small(d, conv_w_full=None):
    rows = [d[n].reshape(-1) for n in _SMALL]
    if conv_w_full is not None:
        rows.append(conv_w_full.reshape(-1))
    flat = jnp.concatenate(rows)
    pad = (-flat.shape[0]) % (8 * D_MODEL)
    return jnp.pad(flat, (0, pad)).reshape(-1, D_MODEL)


def _unpack_small(pack, like, with_conv):
    flat = pack.reshape(-1)
    out, off = {}, 0
    for n in _SMALL:
        size = math.prod(like[n].shape)
        out[n] = flat[off:off + size].reshape(like[n].shape)
        off += size
    conv = flat[off:off + CONV_W * LRU_W].reshape(CONV_W, LRU_W) if with_conv else None
    return out, conv


_GATHER_FIRST = ['ffn1_w13', 'ffn1_w2']
_GATHER_MIXER = ['w_in', 'conv_w', 'w_out', 'mem_wq', 'mem_wkv', 'mem_wo']
_GATHER_FFN2 = ['ffn2_w13', 'ffn2_w2']
_REDUCE_EARLY = ['ffn2_w13', 'ffn2_w2', 'mem_wo', 'mem_wq', 'mem_wkv', 'w_out']
_REDUCE_LATE = ['w_in', 'ffn1_w13', 'ffn1_w2']


def _weight_layout(n, full):
    if n in ('ffn1_w13', 'ffn2_w13', 'mem_wkv'):
        return full
    if n == 'w_in':
        return full.transpose(1, 0, 2).reshape(D_MODEL, -1)
    if n == 'conv_w':
        return full.transpose(1, 0, 2).reshape(CONV_W, LRU_W)
    return full.reshape(-1, full.shape[-1])


def _grad_blocks(n, g):
    if n == 'w_in':
        return g.reshape(D_MODEL, N_DEV, -1).transpose(1, 0, 2)
    return g.reshape((N_DEV, -1, g.shape[-1]))


class _Hooks:
    def __init__(self, w, c_idx):
        self.w, self.c_idx = w, c_idx
        self.sums, self.recv = {}, {}

    def shard(self, n):
        return self.w[n][0] if n == 'conv_w' else self.w[n][0].astype(BF16)

    def gather_job(self, names):
        return _GatherJob([self.shard(n) for n in names])

    def gathered(self, names, outs):
        return {n: _weight_layout(n, o) for n, o in zip(names, outs)}

    def sibling_job(self, names, g):
        self.blocks = [_grad_blocks(n, g[n]) for n in names]
        return _SiblingJob(self.blocks)

    def chip_job(self, names, from_sibling):
        for n, b, r in zip(names, self.blocks, from_sibling):
            self.sums[n] = _pair_add(b, r, self.c_idx, name=f"pair_add_{n}")
        return _ChipJob([self.sums[n][1] for n in names])

    def reduced(self, names, from_chips):
        self.recv.update(zip(names, from_chips))


def kernel(x, mem, ffn1_w13, ffn1_w2, ln1_g, ln1_b, w_in, conv_w, conv_b, w_rgate, b_rgate, w_igate, b_igate, lru_lambda, g_lru, g_sb, w_out, ln2_g, ln2_b, mem_wq, mem_wkv, mem_wo, ln3_g, ln3_b, ffn2_w13, ffn2_w2, ln4_g, ln4_b, loss_target, m_ffn1_w13, m_ffn1_w2, m_ln1_g, m_ln1_b, m_w_in, m_conv_w, m_conv_b, m_w_rgate, m_b_rgate, m_w_igate, m_b_igate, m_lru_lambda, m_g_lru, m_g_sb, m_w_out, m_ln2_g, m_ln2_b, m_mem_wq, m_mem_wkv, m_mem_wo, m_ln3_g, m_ln3_b, m_ffn2_w13, m_ffn2_w2, m_ln4_g, m_ln4_b, v_ffn1_w13, v_ffn1_w2, v_ln1_g, v_ln1_b, v_w_in, v_conv_w, v_conv_b, v_w_rgate, v_b_rgate, v_w_igate, v_b_igate, v_lru_lambda, v_g_lru, v_g_sb, v_w_out, v_ln2_g, v_ln2_b, v_mem_wq, v_mem_wkv, v_mem_wo, v_ln3_g, v_ln3_b, v_ffn2_w13, v_ffn2_w2, v_ln4_g, v_ln4_b):
    args = locals()
    w = {n: args[n] for n in _WEIGHTS}
    mom = {n: args["m_" + n] for n in _WEIGHTS}
    var = {n: args["v_" + n] for n in _WEIGHTS}
    ix, iy, ic = lax.axis_index("x"), lax.axis_index("y"), lax.axis_index("c")

    c_idx = jnp.reshape(ic, (1,)).astype(jnp.int32)
    j_idx = jnp.reshape(2 * ix + iy, (1,)).astype(jnp.int32)
    hooks = _Hooks(w, c_idx)
    p = {n: (w[n][0] if w[n].ndim == 4 else w[n]) for n in _WEIGHTS if n not in _SHARDED and n != 'conv_w'}
    p.update(hooks.gathered(_GATHER_FIRST, _run_job(hooks.gather_job(_GATHER_FIRST), name="gather_ffn1")))
    loss_local, grad_x, g = _local_step(x[0], mem[0], loss_target[0], p, hooks)
    loss = lax.psum(loss_local[0, 0], ("x", "y", "c"))

    from_sibling = _run_job(hooks.sibling_job(_REDUCE_LATE, g), name="reduce_sibling")
    from_chips = _run_job(hooks.chip_job(_REDUCE_LATE, from_sibling), name="reduce_chips")
    hooks.reduced(_REDUCE_LATE, from_chips)
    grads, delta, new_m, new_v = {}, {}, {}, {}
    for n in _SHARDED:
        grads[n], delta[n], new_m[n], new_v[n] = (
            o.reshape(w[n].shape) for o in _reduce_adam(hooks.sums[n][0], hooks.recv[n], j_idx, w[n][0], mom[n][0],
                                                        var[n][0], name=f"adam_{n}"))

    total = _small_all_reduce(_pack_small(g, conv_w_full=g["conv_w"]), name="reduce_small")
    gsmall, gconv = _unpack_small(total, w, with_conv=True)
    me = 4 * ix + 2 * iy + ic
    gsmall["conv_w"] = lax.dynamic_slice_in_dim(gconv, me * (LRU_W // N_DEV), LRU_W // N_DEV, axis=1)[None]
    small_names = _SMALL + ["conv_w"]
    pk = lambda d: _pack_small({n: d[n] for n in _SMALL}, conv_w_full=jnp.pad(d["conv_w"].reshape(-1), (0, CONV_W * LRU_W - CONV_W * LRU_W // N_DEV)))
    d_s, m_s, v_s = _adam_small(pk(w), pk(gsmall), pk(mom), pk(var), name="adam_small")
    for src, dst in ((d_s, delta), (m_s, new_m), (v_s, new_v)):
        vals, conv = _unpack_small(src, w, with_conv=True)
        dst.update(vals)
        dst["conv_w"] = conv.reshape(-1)[:CONV_W * LRU_W // N_DEV].reshape(w["conv_w"].shape)
    grads.update({n: gsmall[n].reshape(w[n].shape) for n in small_names})

    return (loss, grad_x[None], *[grads[n] for n in _WEIGHTS], *[delta[n] for n in _WEIGHTS],
            *[new_m[n] for n in _WEIGHTS], *[new_v[n] for n in _WEIGHTS])
```

```python
import functools
import math

import jax
import jax.numpy as jnp
from jax import lax
from jax.experimental import pallas as pl
from jax.experimental.pallas import tpu as pltpu

F32, BF16 = jnp.float32, jnp.bfloat16

D_MODEL = 1024
LRU_W = 512
SB_W = 512
SB_PAIR = 128
SB_HEAD = 64
SB_ROWS = 128
SB_SKIP = -110.0
D_FF = 2688
FF_CHUNK = 672
N_DEV = 8
MEM_HEADS = 4
MEM_HD = 256
CONV_W = 4
ALPHA = 2.0 ** 0.25
LN_EPS = 1e-5
RMS_EPS = 1e-6
LRU_C = 8.0
ADAM_LR, ADAM_B1, ADAM_B2, ADAM_EPS, ADAM_WD, ADAM_STEP = 0.001, 0.9, 0.999, 1e-08, 0.01, 10

NN = (((1,), (0,)), ((), ()))
NT = (((1,), (1,)), ((), ()))
MESH = pl.DeviceIdType.MESH
VMEM_LIMIT_MB = 56


def _dot(a, b, dn=NN):
    return lax.dot_general(a, b, dn, preferred_element_type=F32)


def _cparams(n_axes, vmem_mb=None):
    kw = dict(dimension_semantics=("arbitrary",) * n_axes)
    if vmem_mb is not None:
        kw["vmem_limit_bytes"] = vmem_mb << 20
    return pltpu.CompilerParams(**kw)


def _row_tile(rows, want):
    if rows <= want:
        return rows
    t = want - want % 8
    while rows % t:
        t -= 8
    return t


def _matmul(a, b, *, name, out_dtype=F32, trans_a=False, trans_b=False, tm=512, tn=1024, tk=1024,
            scale=1.0, res=None, res_coeff=1.0):
    a_chunked, b_chunked = a.ndim == 3, b.ndim == 3
    nc = a.shape[0] if a_chunked else (b.shape[0] if b_chunked else 1)
    a2, b2 = a.shape[-2:], b.shape[-2:]
    (kdim, m) = a2 if trans_a else a2[::-1]
    n = b2[0] if trans_b else b2[1]
    tm, tn, tk = min(tm, m), min(tn, n), min(tk, kdim)
    assert m % tm == 0 and n % tn == 0 and kdim % tk == 0, (name, m, n, kdim)
    nk = kdim // tk

    def a_idx(c, i, j, k):
        idx = (k, i) if trans_a else (i, k)
        return (c,) + idx if a_chunked else idx

    def b_idx(c, i, j, k):
        idx = (j, k) if trans_b else (k, j)
        return (c,) + idx if b_chunked else idx

    a_blk = (tk, tm) if trans_a else (tm, tk)
    b_blk = (tn, tk) if trans_b else (tk, tn)
    in_specs = [pl.BlockSpec(((None,) + a_blk) if a_chunked else a_blk, a_idx),
                pl.BlockSpec(((None,) + b_blk) if b_chunked else b_blk, b_idx)]
    args = [a, b]
    if res is not None:
        in_specs.append(pl.BlockSpec((None, tm, tn), lambda c, i, j, k: (c, i, j)))
        args.append(res.reshape((nc, m, n)))

    def body(*refs):
        if res is not None:
            a_ref, b_ref, r_ref, o_ref, acc_ref = refs
        else:
            a_ref, b_ref, o_ref, acc_ref = refs
        k = pl.program_id(3)

        @pl.when(k == 0)
        def _():
            acc_ref[...] = jnp.zeros_like(acc_ref)

        av = a_ref[...]
        if trans_a:
            av = av.astype(F32).T
        acc_ref[...] += _dot(av.astype(BF16), b_ref[...].astype(BF16), NT if trans_b else NN)

        @pl.when(k == nk - 1)
        def _():
            out = acc_ref[...]
            if scale != 1.0:
                out = out * scale
            if res is not None:
                out = out + res_coeff * r_ref[...]
            o_ref[...] = out.astype(out_dtype)

    out = pl.pallas_call(
        body, grid=(nc, m // tm, n // tn, nk), in_specs=in_specs,
        out_specs=pl.BlockSpec((None, tm, tn), lambda c, i, j, k: (c, i, j)),
        out_shape=jax.ShapeDtypeStruct((nc, m, n), out_dtype),
        scratch_shapes=[pltpu.VMEM((tm, tn), F32)],
        compiler_params=_cparams(4, VMEM_LIMIT_MB), name=name)(*args)
    return out if (a_chunked or b_chunked) else out[0]


def _ln_math(z, g, b):
    mu = jnp.mean(z, axis=-1, keepdims=True)
    zc = z - mu
    var = jnp.mean(zc * zc, axis=-1, keepdims=True)
    return zc * lax.rsqrt(var + LN_EPS) * g + b


def _row_spec(tm, d):
    return pl.BlockSpec((tm, d), lambda i: (i, 0))


def _par_spec(d, rows=1):
    return pl.BlockSpec((rows, d), lambda i: (0, 0))


def _ln_fwd(x, f, fscale, g, b, *, name):
    s, d = x.shape
    tm = min(512, s)

    def body(x_ref, f_ref, g_ref, b_ref, o_ref, ob_ref):
        z = ALPHA * x_ref[...] + fscale * f_ref[...]
        h = _ln_math(z, g_ref[...], b_ref[...])
        o_ref[...] = h
        ob_ref[...] = h.astype(BF16)

    return pl.pallas_call(
        body, grid=(s // tm,), in_specs=[_row_spec(tm, d), _row_spec(tm, d), _par_spec(d), _par_spec(d)],
        out_specs=[_row_spec(tm, d), _row_spec(tm, d)],
        out_shape=[jax.ShapeDtypeStruct((s, d), F32), jax.ShapeDtypeStruct((s, d), BF16)],
        compiler_params=_cparams(1), name=name)(x, f, g, b)


def _ln_bwd(x, f, fscale, g, b, dy, *, name, target=None):
    s, d = x.shape
    tm = min(512, s)
    with_loss = target is not None

    def body(*refs):
        if with_loss:
            x_ref, f_ref, g_ref, b_ref, t_ref, dz_ref, dzb_ref, dg_ref, db_ref, loss_ref = refs
        else:
            x_ref, f_ref, g_ref, b_ref, dy_ref, dz_ref, dzb_ref, dg_ref, db_ref = refs
        i = pl.program_id(0)

        @pl.when(i == 0)
        def _():
            dg_ref[...] = jnp.zeros_like(dg_ref)
            db_ref[...] = jnp.zeros_like(db_ref)
            if with_loss:
                loss_ref[...] = jnp.zeros_like(loss_ref)

        z = ALPHA * x_ref[...] + fscale * f_ref[...]
        y, vjp = jax.vjp(_ln_math, z, g_ref[...], b_ref[...])
        if with_loss:
            err = y - t_ref[...]
            sq = jnp.sum(err * err, axis=1, keepdims=True)
            loss_ref[...] += jnp.sum(sq, axis=0, keepdims=True) * (0.5 / d)
            cot = err * (1.0 / d)
        else:
            cot = dy_ref[...]
        dz, dg, db = vjp(cot)
        dz_ref[...] = dz
        dzb_ref[...] = dz.astype(BF16)
        dg_ref[...] += dg
        db_ref[...] += db

    in_specs = [_row_spec(tm, d), _row_spec(tm, d), _par_spec(d), _par_spec(d), _row_spec(tm, d)]
    out_specs = [_row_spec(tm, d), _row_spec(tm, d), _par_spec(d), _par_spec(d)]
    out_shape = [jax.ShapeDtypeStruct((s, d), F32), jax.ShapeDtypeStruct((s, d), BF16),
                 jax.ShapeDtypeStruct((1, d), F32), jax.ShapeDtypeStruct((1, d), F32)]
    if with_loss:
        out_specs.append(pl.BlockSpec((1, 1), lambda i: (0, 0)))
        out_shape.append(jax.ShapeDtypeStruct((1, 1), F32))
    return pl.pallas_call(
        body, grid=(s // tm,), in_specs=in_specs, out_specs=out_specs, out_shape=out_shape,
        compiler_params=_cparams(1), name=name)(x, f, g, b, target if with_loss else dy)


def _load_weights_once(pairs, sem):
    @pl.when(pl.program_id(0) == 0)
    def _():
        copies = [pltpu.make_async_copy(src, dst, sem.at[n]) for n, (src, dst) in enumerate(pairs)]
        for c in copies:
            c.start()
        for c in copies:
            c.wait()


def _ffn_fwd(h, w13c, w2, *, name, job=None):
    s, d = h.shape
    tm = min(256, s)
    half = N_DEV // 2

    def body(h_ref, w13_hbm, w2_hbm, gu_ref, f_ref, hb_ref, w13_v, w2_v, sem):
        _load_weights_once([(w13_hbm, w13_v), (w2_hbm, w2_v)], sem)
        hb = h_ref[...].astype(BF16)
        hb_ref[...] = hb
        acc = jnp.zeros((tm, d), F32)
        for k in range(half):
            g = _dot(hb, w13_v[k])
            u = _dot(hb, w13_v[k + half])
            gu_ref[k] = g.astype(BF16)
            gu_ref[k + half] = u.astype(BF16)
            a = g * jax.nn.sigmoid(g) * u
            acc = acc + _dot(a.astype(BF16), w2_v[pl.ds(k * FF_CHUNK, FF_CHUNK), :])
        f_ref[...] = acc

    any_spec = pl.BlockSpec(memory_space=pl.ANY)
    return _pallas_with_job(
        body, job, lambda: pl.program_id(0), s // tm,
        grid=(s // tm,), in_specs=[_row_spec(tm, d), any_spec, any_spec],
        out_specs=[pl.BlockSpec((N_DEV, tm, FF_CHUNK), lambda i: (0, i, 0)), _row_spec(tm, d), _row_spec(tm, d)],
        out_shape=[jax.ShapeDtypeStruct((N_DEV, s, FF_CHUNK), BF16), jax.ShapeDtypeStruct((s, d), F32),
                   jax.ShapeDtypeStruct((s, d), BF16)],
        scratch_shapes=[pltpu.VMEM(w13c.shape, BF16), pltpu.VMEM(w2.shape, BF16), pltpu.SemaphoreType.DMA((2,))],
        compiler_params=_cparams(1, VMEM_LIMIT_MB), name=name, args=(h, w13c, w2))


def _ffn_bwd(dz, gu, w13c, w2, *, name):
    s, d = dz.shape
    tm = min(256, s)
    half = N_DEV // 2

    def body(dz_ref, gu_ref, w13_hbm, w2_hbm, dh_ref, dgu_ref, act_ref, w13_v, w2_v, sem):
        _load_weights_once([(w13_hbm, w13_v), (w2_hbm, w2_v)], sem)
        dzv = dz_ref[...]
        dfb = (0.5 * dzv).astype(BF16)
        acc = ALPHA * dzv
        for k in range(half):
            g = gu_ref[k].astype(F32)
            u = gu_ref[k + half].astype(F32)
            da = _dot(dfb, w2_v[pl.ds(k * FF_CHUNK, FF_CHUNK), :], NT)
            sg = jax.nn.sigmoid(g)
            silu = g * sg
            dg = (da * u * (sg * (1.0 + g * (1.0 - sg)))).astype(BF16)
            du = (da * silu).astype(BF16)
            act_ref[k] = (silu * u).astype(BF16)
            dgu_ref[k] = dg
            dgu_ref[k + half] = du
            acc = acc + _dot(dg, w13_v[k], NT) + _dot(du, w13_v[k + half], NT)
        dh_ref[...] = acc

    any_spec = pl.BlockSpec(memory_space=pl.ANY)
    return pl.pallas_call(
        body, grid=(s // tm,),
        in_specs=[_row_spec(tm, d), pl.BlockSpec((N_DEV, tm, FF_CHUNK), lambda i: (0, i, 0)), any_spec, any_spec],
        out_specs=[_row_spec(tm, d), pl.BlockSpec((N_DEV, tm, FF_CHUNK), lambda i: (0, i, 0)),
                   pl.BlockSpec((half, tm, FF_CHUNK), lambda i: (0, i, 0))],
        out_shape=[jax.ShapeDtypeStruct((s, d), F32), jax.ShapeDtypeStruct((N_DEV, s, FF_CHUNK), BF16),
                   jax.ShapeDtypeStruct((half, s, FF_CHUNK), BF16)],
        scratch_shapes=[pltpu.VMEM(w13c.shape, BF16), pltpu.VMEM(w2.shape, BF16), pltpu.SemaphoreType.DMA((2,))],
        compiler_params=_cparams(1, VMEM_LIMIT_MB), name=name)(dz, gu, w13c, w2)


def _shift_down(x, prev8, d):
    t, c = x.shape
    row = lax.broadcasted_iota(jnp.int32, (t, c), 0)
    xr = pltpu.roll(x, d, 0)
    pr = pltpu.roll(prev8, d, 0)
    if t > 8:
        pr = jnp.concatenate([pr, jnp.zeros((t - 8, c), x.dtype)], axis=0)
    return jnp.where(row < d, pr, xr)


def _shift_up(x, next8, d):
    t, c = x.shape
    row = lax.broadcasted_iota(jnp.int32, (t, c), 0)
    xr = pltpu.roll(x, t - d, 0)
    nr = pltpu.roll(next8, 8 - d, 0)
    if t > 8:
        nr = jnp.concatenate([jnp.zeros((t - 8, c), x.dtype), nr], axis=0)
    return jnp.where(row >= t - d, nr, xr)


def _scan_fwd(a, u):
    t = a.shape[0]
    row = lax.broadcasted_iota(jnp.int32, a.shape, 0)
    d = 1
    while d < t:
        a_s, u_s = pltpu.roll(a, d, 0), pltpu.roll(u, d, 0)
        m = row >= d
        u = jnp.where(m, u + a * u_s, u)
        a = jnp.where(m, a * a_s, a)
        d *= 2
    return a, u


def _scan_bwd(b, x):
    t = b.shape[0]
    row = lax.broadcasted_iota(jnp.int32, b.shape, 0)
    d = 1
    while d < t:
        b_s, x_s = pltpu.roll(b, t - d, 0), pltpu.roll(x, t - d, 0)
        m = row < t - d
        x = jnp.where(m, x + b * x_s, x)
        b = jnp.where(m, b * b_s, b)
        d *= 2
    return b, x


def _lru_elem(xc, pr, pi, b_r, b_i, lam):
    r = jax.nn.sigmoid(pr + b_r)
    ig = jax.nn.sigmoid(pi + b_i)
    softplus_neg_lam = jnp.maximum(-lam, 0.0) + jnp.log1p(jnp.exp(-jnp.abs(lam)))
    log_a = (-LRU_C) * r * softplus_neg_lam
    a = jnp.exp(log_a)
    one_minus_a2 = -jnp.tanh(log_a) * (jnp.exp(2.0 * log_a) + 1.0)
    u = jnp.sqrt(one_minus_a2) * (ig * xc)
    return a, u


def _gelu_tanh(x):
    return 0.5 * x * (1.0 + jnp.tanh(math.sqrt(2.0 / math.pi) * (x + 0.044715 * (x * x * x))))


def _conv_fwd(xb, prev8, w, b):
    out = b + w[3:4, :] * xb
    for d in range(1, CONV_W):
        out = out + w[3 - d:4 - d, :] * _shift_down(xb, prev8, d)
    return out


def _lru_fwd(xbg, conv_w, conv_b, w_rg, b_rg, w_ig, b_ig, lam, *, name):
    s = xbg.shape[0]
    c = LRU_W
    t = min(512, s)
    t8 = t // 8

    def body(xb_ref, gate_ref, prev_ref, cw_ref, cb_ref, wr_ref, br_ref, wi_ref, bi_ref, lam_ref,
             xc_ref, h_ref, y_ref, carry_ref):
        i = pl.program_id(0)

        @pl.when(i == 0)
        def _():
            carry_ref[...] = jnp.zeros_like(carry_ref)

        prev8 = jnp.where(i > 0, prev_ref[...], 0.0)
        xc = _conv_fwd(xb_ref[...], prev8, cw_ref[...], cb_ref[...])
        xcb = xc.astype(BF16)
        a, u = _lru_elem(xc, _dot(xcb, wr_ref[...]), _dot(xcb, wi_ref[...]), br_ref[...], bi_ref[...], lam_ref[...])
        a_cum, h0 = _scan_fwd(a, u)
        h = h0 + a_cum * carry_ref[0:1, :]
        carry_ref[...] = jnp.broadcast_to(h[t - 1:t, :], carry_ref.shape)
        xc_ref[...] = xc
        h_ref[...] = h
        y_ref[...] = h * _gelu_tanh(gate_ref[...])

    tile = lambda col: pl.BlockSpec((t, c), lambda i: (i, col))
    par = lambda rows: pl.BlockSpec((rows, c), lambda i: (0, 0))
    out_spec = pl.BlockSpec((t, c), lambda i: (i, 0))
    return pl.pallas_call(
        body, grid=(s // t,),
        in_specs=[tile(0), tile(1), pl.BlockSpec((8, c), lambda i: (jnp.maximum(i * t8 - 1, 0), 0)),
                  par(CONV_W), par(1), pl.BlockSpec((c, c), lambda i: (0, 0)), par(1),
                  pl.BlockSpec((c, c), lambda i: (0, 0)), par(1), par(1)],
        out_specs=[out_spec, out_spec, out_spec],
        out_shape=[jax.ShapeDtypeStruct((s, c), F32)] * 3,
        scratch_shapes=[pltpu.VMEM((8, c), F32)],
        compiler_params=_cparams(1, VMEM_LIMIT_MB), name=name)(
            xbg, xbg, xbg, conv_w, conv_b, w_rg, b_rg, w_ig, b_ig, lam)


def _lru_bwd(xc, xbg, hseq, dy, w_rg, b_rg, w_ig, b_ig, lam, *, name):
    s, c = xc.shape
    t = min(512, s)
    t8 = t // 8
    nt = s // t

    def body(xc_ref, gate_ref, h_ref, hprev_ref, dy_ref, wr_ref, br_ref, wi_ref, bi_ref, lam_ref,
             dxc_ref, dgate_ref, dwr_ref, dwi_ref, dbr_ref, dbi_ref, dlam_ref, a8_ref, l8_ref):
        i = pl.program_id(0)

        @pl.when(i == 0)
        def _():
            for r in (dwr_ref, dwi_ref, dbr_ref, dbi_ref, dlam_ref, a8_ref, l8_ref):
                r[...] = jnp.zeros_like(r)

        xc_v = xc_ref[...]
        xcb = xc_v.astype(BF16)
        (a, _), vjp = jax.vjp(_lru_elem, xc_v, _dot(xcb, wr_ref[...]), _dot(xcb, wi_ref[...]),
                              br_ref[...], bi_ref[...], lam_ref[...])
        gl, vjp_gelu = jax.vjp(_gelu_tanh, gate_ref[...])
        dyv, hs = dy_ref[...], h_ref[...]
        (dgate,) = vjp_gelu(dyv * hs)
        b_next = _shift_up(a, a8_ref[...], 1)
        b_cum, l0 = _scan_bwd(b_next, dyv * gl)
        lt = l0 + b_cum * l8_ref[0:1, :]
        a8_ref[...] = a[0:8, :]
        l8_ref[...] = lt[0:8, :]
        hprev8 = jnp.where(i < nt - 1, hprev_ref[...], 0.0)
        da = lt * _shift_down(hs, hprev8, 1)
        dxc_e, dpr, dpi, dbr, dbi, dlam = vjp((da, lt))
        dprb, dpib = dpr.astype(BF16), dpi.astype(BF16)
        dxc_ref[...] = dxc_e + _dot(dprb, wr_ref[...], NT) + _dot(dpib, wi_ref[...], NT)
        dgate_ref[...] = dgate.astype(BF16)
        xct = xc_v.T.astype(BF16)
        dwr_ref[...] += _dot(xct, dprb)
        dwi_ref[...] += _dot(xct, dpib)
        dbr_ref[...] += dbr
        dbi_ref[...] += dbi
        dlam_ref[...] += dlam

    rev = lambda col: pl.BlockSpec((t, c), lambda i: (nt - 1 - i, col))
    par = pl.BlockSpec((1, c), lambda i: (0, 0))
    sq = pl.BlockSpec((c, c), lambda i: (0, 0))
    return pl.pallas_call(
        body, grid=(nt,),
        in_specs=[rev(0), rev(1), rev(0), pl.BlockSpec((8, c), lambda i: (jnp.maximum((nt - 1 - i) * t8 - 1, 0), 0)),
                  rev(0), sq, par, sq, par, par],
        out_specs=[rev(0), rev(0), sq, sq, par, par, par],
        out_shape=[jax.ShapeDtypeStruct((s, c), F32), jax.ShapeDtypeStruct((s, c), BF16)]
                  + [jax.ShapeDtypeStruct((c, c), F32)] * 2 + [jax.ShapeDtypeStruct((1, c), F32)] * 3,
        scratch_shapes=[pltpu.VMEM((8, c), F32), pltpu.VMEM((8, c), F32)],
        compiler_params=_cparams(1, VMEM_LIMIT_MB), name=name)(
            xc, xbg, hseq, hseq, dy, w_rg, b_rg, w_ig, b_ig, lam)


def _conv_bwd(dxc, xbg, conv_w, *, name):
    s, c = dxc.shape
    t = min(512, s)
    t8 = t // 8
    nt = s // t

    def body(dxc_ref, next_ref, xb_ref, prev_ref, w_ref, dxb_ref, dw_ref, db_ref):
        i = pl.program_id(0)

        @pl.when(i == 0)
        def _():
            dw_ref[...] = jnp.zeros_like(dw_ref)
            db_ref[...] = jnp.zeros_like(db_ref)

        g = dxc_ref[...]
        xb = xb_ref[...]
        w = w_ref[...]
        next8 = jnp.where(i < nt - 1, next_ref[...], 0.0)
        prev8 = jnp.where(i > 0, prev_ref[...], 0.0)
        dxb = w[3:4, :] * g
        dw_ref[3:4, :] += jnp.sum(g * xb, axis=0, keepdims=True)
        for d in range(1, CONV_W):
            dxb = dxb + w[3 - d:4 - d, :] * _shift_up(g, next8, d)
            dw_ref[3 - d:4 - d, :] += jnp.sum(g * _shift_down(xb, prev8, d), axis=0, keepdims=True)
        dxb_ref[...] = dxb.astype(BF16)
        db_ref[...] += jnp.sum(g, axis=0, keepdims=True)

    tile = pl.BlockSpec((t, c), lambda i: (i, 0))
    return pl.pallas_call(
        body, grid=(nt,),
        in_specs=[tile, pl.BlockSpec((8, c), lambda i: (jnp.minimum((i + 1) * t8, s // 8 - 1), 0)),
                  tile, pl.BlockSpec((8, c), lambda i: (jnp.maximum(i * t8 - 1, 0), 0)),
                  pl.BlockSpec((CONV_W, c), lambda i: (0, 0))],
        out_specs=[tile, pl.BlockSpec((CONV_W, c), lambda i: (0, 0)), pl.BlockSpec((1, c), lambda i: (0, 0))],
        out_shape=[jax.ShapeDtypeStruct((s, c), BF16), jax.ShapeDtypeStruct((CONV_W, c), F32),
                   jax.ShapeDtypeStruct((1, c), F32)],
        compiler_params=_cparams(1), name=name)(dxc, dxc, xbg, xbg, conv_w)


def _hi_lo_dot(x, tri):
    hi = x.astype(BF16)
    lo = (x - hi.astype(F32)).astype(BF16)
    return _dot(hi, tri) + _dot(lo, tri)


def _tri(tk, inclusive):
    r = lax.broadcasted_iota(jnp.int32, (tk, tk), 0)
    c = lax.broadcasted_iota(jnp.int32, (tk, tk), 1)
    return ((r >= c) if inclusive else (r > c)).astype(BF16)


def _split_heads(x):
    first = lax.broadcasted_iota(jnp.int32, x.shape, 1) < SB_HEAD
    zero = jnp.zeros_like(x)
    return first, (jnp.where(first, x, zero), jnp.where(first, zero, x))


def _sb_softplus_terms(z):
    ls = jnp.minimum(z, 0.0) - jnp.log(1.0 + jnp.exp(-jnp.abs(z)))
    return ls, ls - z


def _sb_alive(runs):
    m = runs[0]
    for r in runs[1:]:
        m = jnp.maximum(m, r)
    return jnp.max(m) > SB_SKIP


def _sb_fwd(q, kv, *, name, job=None):
    s = q.shape[0]
    tb = min(256, s)
    rc = min(SB_ROWS, tb)
    npair = SB_W // SB_PAIR
    chains = [(h, c) for h in range(2) for c in range(tb // rc)]

    def body(q_ref, k_ref, v_ref, o_ref):
        i = pl.program_id(1)
        first, qs = _split_heads(q_ref[...])
        tri = _tri(tb, False)
        causal = lax.broadcasted_iota(jnp.int32, (tb, tb), 1) < lax.broadcasted_iota(jnp.int32, (tb, tb), 0)
        causal_c = [causal[c * rc:(c + 1) * rc, :] for c in range(tb // rc)]

        def group(j_lo, width, carry, diag_last):
            k0 = pl.multiple_of(j_lo * tb, tb)
            kb = k_ref[pl.ds(k0, width * tb), :]
            vb = v_ref[pl.ds(k0, width * tb), :]
            zw = [_dot(qs[h][c * rc:(c + 1) * rc, :], kb, NT) for h, c in chains]
            cols = [slice(t * tb, (t + 1) * tb) for t in range(width)]
            terms = [[_sb_softplus_terms(z[:, cl]) for cl in cols] for z in zw]
            lss = [[ls for ls, _ in tc] for tc in terms]
            ns = [[n for _, n in tc] for tc in terms]
            if diag_last:
                ns = [nc[:-1] + [jnp.where(causal_c[c], nc[-1], 0.0)] for nc, (h, c) in zip(ns, chains)]
            sufs = [[_hi_lo_dot(n, tri) for n in nc] for nc in ns]
            out = []
            for (h, c), lc, nc, sc, (run, acc) in zip(chains, lss, ns, sufs, carry):
                ws = [None] * width
                for t in reversed(range(width)):
                    w = jnp.exp(lc[t] + (sc[t] + run))
                    if diag_last and t == width - 1:
                        w = jnp.where(causal_c[c], w, 0.0)
                    ws[t] = w.astype(BF16)
                    run = run + jnp.sum(nc[t], axis=1, keepdims=True)
                out.append((run, acc + _dot(ws[0] if width == 1 else jnp.concatenate(ws, axis=1), vb)))
            return tuple(out)

        zero = (jnp.zeros((rc, 1), F32), jnp.zeros((rc, SB_PAIR), F32))
        carry = lax.cond(i == 0, lambda c: group(0, 1, c, True), lambda c: group(i - 1, 2, c, True),
                         (zero,) * len(chains))

        def step(st):
            j, _, c = st
            c = group(j, 1, c, False)
            return j - 1, _sb_alive([run for run, _ in c]), c

        _, _, carry = lax.while_loop(lambda st: jnp.logical_and(st[0] >= 0, st[1]), step,
                                     (i - 2, _sb_alive([run for run, _ in carry]), carry))
        accs = [jnp.concatenate([carry[n][1] for n, (h, c) in enumerate(chains) if h == hh], axis=0) for hh in range(2)]
        o_ref[...] = jnp.where(first, accs[0], accs[1])

    (o,), job_out = _pallas_with_job(
        body, job, lambda: pl.program_id(0) * (s // tb) + pl.program_id(1), npair * (s // tb),
        grid=(npair, s // tb),
        in_specs=[pl.BlockSpec((tb, SB_PAIR), lambda p, i: (i, p)),
                  pl.BlockSpec((s, SB_PAIR), lambda p, i: (0, p)),
                  pl.BlockSpec((s, SB_PAIR), lambda p, i: (0, npair + p))],
        out_specs=[pl.BlockSpec((tb, SB_PAIR), lambda p, i: (i, p))],
        out_shape=[jax.ShapeDtypeStruct((s, SB_W), F32)],
        compiler_params=_cparams(2, VMEM_LIMIT_MB), name=name, args=(q, kv, kv))
    return o, job_out


def _sb_bwd(q, kv, o, do, *, name, job=None):
    s = q.shape[0]
    tb = min(256, s)
    rc = min(SB_ROWS, tb)
    npair = SB_W // SB_PAIR
    chains = [(h, c) for h in range(2) for c in range(tb // rc)]

    def body(q_ref, k_ref, v_ref, o_ref, do_ref, dq_ref, dkb_ref, dvb_ref, dk_ref, dv_ref):
        i = pl.program_id(1)

        @pl.when(i == 0)
        def _():
            dk_ref[...] = jnp.zeros_like(dk_ref)
            dv_ref[...] = jnp.zeros_like(dv_ref)

        first, qs = _split_heads(q_ref[...])
        dob = do_ref[...].astype(BF16)
        _, dos = _split_heads(dob)
        prod = dob.astype(F32) * o_ref[...]
        deltas = (jnp.sum(jnp.where(first, prod, 0.0), axis=1, keepdims=True),
                  jnp.sum(jnp.where(first, 0.0, prod), axis=1, keepdims=True))
        tri_x, tri_i = _tri(tb, False), _tri(tb, True)
        causal = lax.broadcasted_iota(jnp.int32, (tb, tb), 1) < lax.broadcasted_iota(jnp.int32, (tb, tb), 0)
        causal_c = [causal[c * rc:(c + 1) * rc, :] for c in range(tb // rc)]
        rows = [slice(c * rc, (c + 1) * rc) for c in range(tb // rc)]

        def group(j_lo, width, carry, diag_last):
            k0 = pl.multiple_of(j_lo * tb, tb)
            kb = k_ref[pl.ds(k0, width * tb), :]
            vb = v_ref[pl.ds(k0, width * tb), :]
            cols = [slice(t * tb, (t + 1) * tb) for t in range(width)]
            zw = [_dot(qs[h][rows[c], :], kb, NT) for h, c in chains]
            dww = [_dot(dos[h][rows[c], :], vb, NT) for h, c in chains]
            terms = [[_sb_softplus_terms(z[:, cl]) for cl in cols] for z in zw]
            lss = [[ls for ls, _ in tc] for tc in terms]
            ns = [[n for _, n in tc] for tc in terms]
            if diag_last:
                ns = [nc[:-1] + [jnp.where(causal_c[c], nc[-1], 0.0)] for nc, (h, c) in zip(ns, chains)]
            sufs = [[_hi_lo_dot(n, tri_x) for n in nc] for nc in ns]
            wbs, gs, runs_n = [], [], []
            for (h, c), lc, nc, sc, dw, (run_n, _, _) in zip(chains, lss, ns, sufs, dww, carry):
                wb, g = [None] * width, [None] * width
                for t in reversed(range(width)):
                    w = jnp.exp(lc[t] + (sc[t] + run_n))
                    if diag_last and t == width - 1:
                        w = jnp.where(causal_c[c], w, 0.0)
                    wb[t] = w.astype(BF16)
                    g[t] = wb[t].astype(F32) * dw[:, cols[t]]
                    run_n = run_n + jnp.sum(nc[t], axis=1, keepdims=True)
                wbs.append(wb)
                gs.append(g)
                runs_n.append(run_n)
            gsufs = [[_hi_lo_dot(g, tri_i) for g in gc] for gc in gs]
            out = []
            dk_t = [jnp.zeros((tb, SB_PAIR), F32) for _ in range(width)]
            dv_t = [jnp.zeros((tb, SB_PAIR), F32) for _ in range(width)]
            for (h, c), lc, gc, gsc, wb, run_n, (_, run_g, dq_acc) in zip(chains, lss, gs, gsufs, wbs, runs_n, carry):
                qh, doh, delta = qs[h][rows[c], :], dos[h][rows[c], :], deltas[h][rows[c], :]
                dzb = [None] * width
                for t in reversed(range(width)):
                    pre = delta - (gsc[t] + run_g)
                    dz = gc[t] - jnp.exp(lc[t]) * (gc[t] + pre)
                    if diag_last and t == width - 1:
                        dz = jnp.where(causal_c[c], dz, 0.0)
                    run_g = run_g + jnp.sum(gc[t], axis=1, keepdims=True)
                    dzb[t] = dz.astype(BF16)
                    dk_t[t] = dk_t[t] + _dot(dz.T.astype(BF16), qh)
                    dv_t[t] = dv_t[t] + _dot(wb[t].astype(F32).T.astype(BF16), doh)
                out.append((run_n, run_g, dq_acc + _dot(dzb[0] if width == 1 else jnp.concatenate(dzb, axis=1), kb)))
            for t in range(width):
                dk_ref[pl.ds(pl.multiple_of((j_lo + t) * tb, tb), tb), :] += dk_t[t]
                dv_ref[pl.ds(pl.multiple_of((j_lo + t) * tb, tb), tb), :] += dv_t[t]
            return tuple(out)

        zero = (jnp.zeros((rc, 1), F32), jnp.zeros((rc, 1), F32), jnp.zeros((rc, SB_PAIR), F32))
        carry = lax.cond(i == 0, lambda c: group(0, 1, c, True), lambda c: group(i - 1, 2, c, True),
                         (zero,) * len(chains))

        def step(st):
            j, _, c = st
            c = group(j, 1, c, False)
            return j - 1, _sb_alive([r[0] for r in c]), c

        _, _, carry = lax.while_loop(lambda st: jnp.logical_and(st[0] >= 0, st[1]), step,
                                     (i - 2, _sb_alive([r[0] for r in carry]), carry))
        dqs = [jnp.concatenate([carry[n][2] for n, (h, c) in enumerate(chains) if h == hh], axis=0) for hh in range(2)]
        dq_ref[...] = (jnp.where(first, dqs[0], dqs[1]) * (1.0 / math.sqrt(SB_HEAD))).astype(BF16)

        @pl.when(i == s // tb - 1)
        def _():
            dkb_ref[...] = dk_ref[...].astype(BF16)
            dvb_ref[...] = dv_ref[...].astype(BF16)

    qtile = pl.BlockSpec((tb, SB_PAIR), lambda p, i: (i, p))
    col = pl.BlockSpec((s, SB_PAIR), lambda p, i: (0, p))
    return _pallas_with_job(
        body, job, lambda: pl.program_id(0) * (s // tb) + pl.program_id(1), npair * (s // tb),
        grid=(npair, s // tb),
        in_specs=[qtile, col, pl.BlockSpec((s, SB_PAIR), lambda p, i: (0, npair + p)), qtile, qtile],
        out_specs=[qtile, col, col],
        out_shape=[jax.ShapeDtypeStruct((s, SB_W), BF16)] * 3,
        scratch_shapes=[pltpu.VMEM((s, SB_PAIR), F32)] * 2,
        compiler_params=_cparams(2, VMEM_LIMIT_MB), name=name, args=(q, kv, kv, o, do))


def _mixnorm_math(yl, ys, gl, gs):
    def rms(x, g):
        return x * lax.rsqrt(jnp.mean(x * x, axis=-1, keepdims=True) + RMS_EPS) * g
    return rms(yl, gl), rms(ys, gs)


def _mixnorm_fwd(yl, ys, gl, gs, *, name):
    s, c = yl.shape
    tm = min(512, s)

    def body(yl_ref, ys_ref, gl_ref, gs_ref, y_ref):
        a, b = _mixnorm_math(yl_ref[...], ys_ref[...], gl_ref[...], gs_ref[...])
        y_ref[:, 0:c] = a.astype(BF16)
        y_ref[:, c:2 * c] = b.astype(BF16)

    return pl.pallas_call(
        body, grid=(s // tm,), in_specs=[_row_spec(tm, c), _row_spec(tm, c), _par_spec(c), _par_spec(c)],
        out_specs=_row_spec(tm, 2 * c), out_shape=jax.ShapeDtypeStruct((s, 2 * c), BF16),
        compiler_params=_cparams(1), name=name)(yl, ys, gl, gs)


def _mixnorm_bwd(yl, ys, gl, gs, dy, *, name, job=None):
    s, c = yl.shape
    tm = min(512, s)

    def body(yl_ref, ys_ref, gl_ref, gs_ref, dy_ref, dyl_ref, dys_ref, dgl_ref, dgs_ref):
        @pl.when(pl.program_id(0) == 0)
        def _():
            dgl_ref[...] = jnp.zeros_like(dgl_ref)
            dgs_ref[...] = jnp.zeros_like(dgs_ref)

        _, vjp = jax.vjp(_mixnorm_math, yl_ref[...], ys_ref[...], gl_ref[...], gs_ref[...])
        dyl, dys, dgl, dgs = vjp((dy_ref[:, 0:c], dy_ref[:, c:2 * c]))
        dyl_ref[...] = dyl
        dys_ref[...] = dys
        dgl_ref[...] += dgl
        dgs_ref[...] += dgs

    return _pallas_with_job(
        body, job, lambda: pl.program_id(0), s // tm, grid=(s // tm,),
        in_specs=[_row_spec(tm, c), _row_spec(tm, c), _par_spec(c), _par_spec(c), _row_spec(tm, 2 * c)],
        out_specs=[_row_spec(tm, c), _row_spec(tm, c), _par_spec(c), _par_spec(c)],
        out_shape=[jax.ShapeDtypeStruct((s, c), F32)] * 2 + [jax.ShapeDtypeStruct((1, c), F32)] * 2,
        compiler_params=_cparams(1), name=name, args=(yl, ys, gl, gs, dy))


def _cross_probs(qh, kh):
    sc = _dot(qh, kh, NT) * (1.0 / math.sqrt(MEM_HD))
    e = jnp.exp(sc - jnp.max(sc, axis=-1, keepdims=True))
    return e / jnp.sum(e, axis=-1, keepdims=True)


def _cross_fwd(q, kv, *, name):
    s, d = q.shape
    mlen = kv.shape[1]
    tm = min(512, s)

    def body(q_ref, kv_ref, o_ref):
        for h in range(MEM_HEADS):
            cols = slice(h * MEM_HD, (h + 1) * MEM_HD)
            p = _cross_probs(q_ref[:, cols].astype(BF16), kv_ref[h].astype(BF16))
            o_ref[:, cols] = _dot(p.astype(BF16), kv_ref[MEM_HEADS + h].astype(BF16)).astype(BF16)

    return pl.pallas_call(
        body, grid=(s // tm,),
        in_specs=[_row_spec(tm, d), pl.BlockSpec((2 * MEM_HEADS, mlen, MEM_HD), lambda i: (0, 0, 0))],
        out_specs=_row_spec(tm, d), out_shape=jax.ShapeDtypeStruct((s, d), BF16),
        compiler_params=_cparams(1, VMEM_LIMIT_MB), name=name)(q, kv)


def _cross_bwd(q, kv, do, *, name):
    s, d = q.shape
    mlen = kv.shape[1]
    tm = min(512, s)

    def body(q_ref, kv_ref, do_ref, dq_ref, dkv_ref):
        @pl.when(pl.program_id(0) == 0)
        def _():
            dkv_ref[...] = jnp.zeros_like(dkv_ref)

        for h in range(MEM_HEADS):
            cols = slice(h * MEM_HD, (h + 1) * MEM_HD)
            qh = q_ref[:, cols].astype(BF16)
            kh = kv_ref[h].astype(BF16)
            doh = do_ref[:, cols].astype(BF16)
            p = _cross_probs(qh, kh)
            dp = _dot(doh, kv_ref[MEM_HEADS + h].astype(BF16), NT)
            ds = p * (dp - jnp.sum(dp * p, axis=-1, keepdims=True)) * (1.0 / math.sqrt(MEM_HD))
            dq_ref[:, cols] = _dot(ds.astype(BF16), kh).astype(BF16)
            dkv_ref[h] += _dot(ds.T.astype(BF16), qh)
            dkv_ref[MEM_HEADS + h] += _dot(p.T.astype(BF16), doh)

    kv_spec = pl.BlockSpec((2 * MEM_HEADS, mlen, MEM_HD), lambda i: (0, 0, 0))
    return pl.pallas_call(
        body, grid=(s // tm,), in_specs=[_row_spec(tm, d), kv_spec, _row_spec(tm, d)],
        out_specs=[_row_spec(tm, d), kv_spec],
        out_shape=[jax.ShapeDtypeStruct((s, d), BF16), jax.ShapeDtypeStruct(kv.shape, F32)],
        compiler_params=_cparams(1, VMEM_LIMIT_MB), name=name)(q, kv, do)


def _proj_bwd_dx(parts, w_in, dz, *, name):
    s = dz.shape[0]
    d = D_MODEL
    tm = min(256, s)
    widths = [p.shape[1] for p in parts]

    def body(*refs):
        part_refs, w_ref, dz_ref, o_ref = refs[:len(parts)], refs[len(parts)], refs[-2], refs[-1]
        acc = ALPHA * dz_ref[...]
        off = 0
        for p_ref, wd in zip(part_refs, widths):
            acc = acc + _dot(p_ref[...].astype(BF16), w_ref[:, off:off + wd], NT)
            off += wd
        o_ref[...] = acc

    return pl.pallas_call(
        body, grid=(s // tm,),
        in_specs=[_row_spec(tm, wd) for wd in widths] + [pl.BlockSpec(w_in.shape, lambda i: (0, 0)), _row_spec(tm, d)],
        out_specs=_row_spec(tm, d), out_shape=jax.ShapeDtypeStruct((s, d), F32),
        compiler_params=_cparams(1, VMEM_LIMIT_MB), name=name)(*parts, w_in, dz)


def _mesh_pos():
    return lax.axis_index("x"), lax.axis_index("y"), lax.axis_index("c")


class _GatherJob:
    n_phases = 3

    def __init__(self, shards):
        n = len(shards)
        self.inputs = list(shards)
        self.out_shape = [jax.ShapeDtypeStruct((N_DEV,) + a.shape, a.dtype) for a in shards]
        self.scratch = [pltpu.SemaphoreType.DMA((n, 7)), pltpu.SemaphoreType.DMA((n, 7)), pltpu.SemaphoreType.DMA((n,))]

    def phase(self, k, ins, outs, sems):
        send_sems, recv_sems, local_sems = sems
        n = len(ins)
        x, y, c = _mesh_pos()
        me, sibling = (x, y, c), (x, y, 1 - c)
        chips = [(1 - x, y), (x, 1 - y), (1 - x, 1 - y)]

        def copy(a, slot, block, to, src=None):
            dst = outs[a].at[4 * block[0] + 2 * block[1] + block[2]]
            return pltpu.make_async_remote_copy(
                src_ref=dst if src is None else src, dst_ref=dst,
                send_sem=send_sems.at[a, slot], recv_sem=recv_sems.at[a, slot], device_id=to, device_id_type=MESH)

        def mine():
            return [pltpu.make_async_copy(ins[a], outs[a].at[4 * x + 2 * y + c], local_sems.at[a]) for a in range(n)]

        def first():
            return [cp for a in range(n) for cp in
                    [copy(a, 0, me, sibling, src=ins[a])] + [copy(a, 1 + j, me, (*chip, c), src=ins[a]) for j, chip in enumerate(chips)]]

        def passed(a, j):
            return copy(a, 4 + j, (*chips[j], c), sibling)

        if k == 0:
            for cp in mine() + first():
                cp.start()
        elif k == 1:
            for a in range(n):
                for j, chip in enumerate(chips):
                    copy(a, 1 + j, (*chip, c), me).wait_recv()
                    passed(a, j).start()
        else:
            for a in range(n):
                copy(a, 0, sibling, me).wait_recv()
                for j, chip in enumerate(chips):
                    copy(a, 4 + j, (*chip, 1 - c), me).wait_recv()
            for cp in first() + [passed(a, j) for a in range(n) for j in range(3)]:
                cp.wait_send()
            for cp in mine():
                cp.wait()


class _SiblingJob:
    n_phases = 2

    def __init__(self, grads):
        n, half = len(grads), N_DEV // 2
        self.inputs = list(grads)
        self.out_shape = [jax.ShapeDtypeStruct((half,) + g.shape[1:], g.dtype) for g in grads]
        self.scratch = [pltpu.SemaphoreType.DMA((n, half)), pltpu.SemaphoreType.DMA((n, half))]

    def phase(self, k, ins, outs, sems):
        send_sems, recv_sems = sems
        x, y, c = _mesh_pos()
        copies = [pltpu.make_async_remote_copy(
            src_ref=ins[a].at[2 * j + (1 - c)], dst_ref=outs[a].at[j], send_sem=send_sems.at[a, j],
            recv_sem=recv_sems.at[a, j], device_id=(x, y, 1 - c), device_id_type=MESH)
            for a in range(len(ins)) for j in range(N_DEV // 2)]
        for cp in copies:
            if k == 0:
                cp.start()
            else:
                cp.wait()


class _ChipJob:
    n_phases = 2

    def __init__(self, partials):
        n, half = len(partials), N_DEV // 2
        self.inputs = list(partials)
        self.out_shape = [jax.ShapeDtypeStruct(p.shape, p.dtype) for p in partials]
        self.scratch = [pltpu.SemaphoreType.DMA((n, half)), pltpu.SemaphoreType.DMA((n, half)), pltpu.SemaphoreType.DMA((n,))]

    def phase(self, k, ins, outs, sems):
        send_sems, recv_sems, local_sems = sems
        n, half = len(ins), N_DEV // 2
        x, y, c = _mesh_pos()
        jme = 2 * x + y
        local = [pltpu.make_async_copy(ins[a].at[jme], outs[a].at[jme], local_sems.at[a]) for a in range(n)]
        for cp in local:
            if k == 0:
                cp.start()
            else:
                cp.wait()
        for j in range(half):
            @pl.when(jme != j)
            def _(j=j):
                for a in range(n):
                    if k == 0:
                        pltpu.make_async_remote_copy(
                            src_ref=ins[a].at[j], dst_ref=outs[a].at[jme], send_sem=send_sems.at[a, j],
                            recv_sem=recv_sems.at[a, jme], device_id=(j // 2, j % 2, c), device_id_type=MESH).start()
                    else:
                        arrive = pltpu.make_async_remote_copy(
                            src_ref=ins[a].at[j], dst_ref=outs[a].at[j], send_sem=send_sems.at[a, j],
                            recv_sem=recv_sems.at[a, j], device_id=(j // 2, j % 2, c), device_id_type=MESH)
                        arrive.wait_recv()
                        arrive.wait_send()


def _phase_steps(n_phases, n_steps):
    return [0, n_steps - 1] if n_phases == 2 else [0, n_steps // 2, n_steps - 1]


def _pallas_with_job(body, job, step_fn, n_steps, *, grid, in_specs, out_specs, out_shape, scratch_shapes=(),
                     compiler_params, name, args):
    in_specs, out_specs, out_shape, scratch_shapes = list(in_specs), list(out_specs), list(out_shape), list(scratch_shapes)
    if job is None:
        outs = pl.pallas_call(body, grid=grid, in_specs=in_specs, out_specs=out_specs, out_shape=out_shape,
                              scratch_shapes=scratch_shapes, compiler_params=compiler_params, name=name)(*args)
        return list(outs), None
    ni, no, ns = len(in_specs), len(out_specs), len(scratch_shapes)
    ji, jo = len(job.inputs), len(job.out_shape)
    steps = _phase_steps(job.n_phases, n_steps)

    def wrapped(*refs):
        own_in, job_in = refs[:ni], refs[ni:ni + ji]
        own_out, job_out = refs[ni + ji:ni + ji + no], refs[ni + ji + no:ni + ji + no + jo]
        own_scr, job_scr = refs[ni + ji + no + jo:ni + ji + no + jo + ns], refs[ni + ji + no + jo + ns:]
        step = step_fn()
        for k, at in enumerate(steps):
            @pl.when(step == at)
            def _(k=k):
                job.phase(k, job_in, job_out, job_scr)
        body(*own_in, *own_out, *own_scr)

    any_spec = pl.BlockSpec(memory_space=pl.ANY)
    outs = pl.pallas_call(
        wrapped, grid=grid, in_specs=in_specs + [any_spec] * ji, out_specs=out_specs + [any_spec] * jo,
        out_shape=out_shape + list(job.out_shape), scratch_shapes=scratch_shapes + list(job.scratch),
        compiler_params=compiler_params, name=name)(*args, *job.inputs)
    return list(outs[:no]), list(outs[no:])


def _run_job(job, *, name):
    ji, jo = len(job.inputs), len(job.out_shape)

    def body(*refs):
        for k in range(job.n_phases):
            job.phase(k, refs[:ji], refs[ji:ji + jo], refs[ji + jo:])

    any_spec = pl.BlockSpec(memory_space=pl.ANY)
    return pl.pallas_call(body, in_specs=[any_spec] * ji, out_specs=[any_spec] * jo, out_shape=list(job.out_shape),
                          scratch_shapes=list(job.scratch), name=name)(*job.inputs)


def _pair_add(grad, recv, c_idx, *, name):
    half, r, cdim = recv.shape
    tr = _row_tile(r, 512)

    def body(c_ref, g_ref, r_ref, t_ref, tb_ref):
        t = g_ref[...] + r_ref[...]
        t_ref[...] = t
        tb_ref[...] = t.astype(BF16)

    grid_spec = pltpu.PrefetchScalarGridSpec(
        num_scalar_prefetch=1, grid=(half, r // tr),
        in_specs=[pl.BlockSpec((None, tr, cdim), lambda j, i, c_ref: (2 * j + c_ref[0], i, 0)),
                  pl.BlockSpec((None, tr, cdim), lambda j, i, c_ref: (j, i, 0))],
        out_specs=[pl.BlockSpec((None, tr, cdim), lambda j, i, c_ref: (j, i, 0))] * 2)
    return pl.pallas_call(
        body, grid_spec=grid_spec,
        out_shape=[jax.ShapeDtypeStruct(recv.shape, F32), jax.ShapeDtypeStruct(recv.shape, BF16)],
        compiler_params=_cparams(2), name=name)(c_idx, grad, recv)


def _adam_math(w, g, m, v):
    m = ADAM_B1 * m + (1.0 - ADAM_B1) * g
    v = ADAM_B2 * v + (1.0 - ADAM_B2) * (g * g)
    m_hat = m / (1.0 - ADAM_B1 ** ADAM_STEP)
    v_hat = v / (1.0 - ADAM_B2 ** ADAM_STEP)
    delta = -ADAM_LR * (m_hat / (jnp.sqrt(v_hat) + ADAM_EPS) + ADAM_WD * w)
    return delta, m, v


def _reduce_adam(own, recv, j_idx, w, m, v, *, name):
    half, r, cdim = recv.shape
    tr = _row_tile(r, 256)

    def body(j_ref, own_ref, recv_ref, w_ref, m_ref, v_ref, g_ref, d_ref, nm_ref, nv_ref):
        jme = j_ref[0]
        g = own_ref[...]
        for sl in range(half):
            g = g + jnp.where(sl == jme, 0.0, recv_ref[sl].astype(F32))
        delta, nm, nv = _adam_math(w_ref[...], g, m_ref[...], v_ref[...])
        g_ref[...] = g
        d_ref[...] = delta
        nm_ref[...] = nm
        nv_ref[...] = nv

    tile = pl.BlockSpec((tr, cdim), lambda i, j_ref: (i, 0))
    grid_spec = pltpu.PrefetchScalarGridSpec(
        num_scalar_prefetch=1, grid=(r // tr,),
        in_specs=[pl.BlockSpec((None, tr, cdim), lambda i, j_ref: (j_ref[0], i, 0)),
                  pl.BlockSpec((half, tr, cdim), lambda i, j_ref: (0, i, 0)), tile, tile, tile],
        out_specs=[tile] * 4)
    return pl.pallas_call(
        body, grid_spec=grid_spec, out_shape=[jax.ShapeDtypeStruct((r, cdim), F32)] * 4,
        compiler_params=_cparams(1), name=name)(j_idx, own, recv, w, m, v)


def _small_all_reduce(pack, *, name):
    r, cdim = pack.shape

    def body(p_ref, o_ref, land_ref, send_sems, recv_sems):
        x, y, c = _mesh_pos()
        me = 4 * x + 2 * y + c
        land_ref[me] = p_ref[...]
        for t in range(N_DEV):
            @pl.when(me != t)
            def _(t=t):
                pltpu.make_async_remote_copy(
                    src_ref=p_ref, dst_ref=land_ref.at[me], send_sem=send_sems.at[t], recv_sem=recv_sems.at[me],
                    device_id=(t // 4, (t // 2) % 2, t % 2), device_id_type=MESH).start()
        for t in range(N_DEV):
            @pl.when(me != t)
            def _(t=t):
                cp = pltpu.make_async_remote_copy(
                    src_ref=p_ref, dst_ref=land_ref.at[t], send_sem=send_sems.at[t], recv_sem=recv_sems.at[t],
                    device_id=(t // 4, (t // 2) % 2, t % 2), device_id_type=MESH)
                cp.wait_recv()
                cp.wait_send()
        total = land_ref[0]
        for t in range(1, N_DEV):
            total = total + land_ref[t]
        o_ref[...] = total

    vmem = pl.BlockSpec(memory_space=pltpu.VMEM)
    return pl.pallas_call(
        body, in_specs=[vmem], out_specs=vmem, out_shape=jax.ShapeDtypeStruct((r, cdim), F32),
        scratch_shapes=[pltpu.VMEM((N_DEV, r, cdim), F32), pltpu.SemaphoreType.DMA((N_DEV,)),
                        pltpu.SemaphoreType.DMA((N_DEV,))],
        name=name)(pack)


def _adam_small(w, g, m, v, *, name):
    def body(w_ref, g_ref, m_ref, v_ref, d_ref, nm_ref, nv_ref):
        d_ref[...], nm_ref[...], nv_ref[...] = _adam_math(w_ref[...], g_ref[...], m_ref[...], v_ref[...])

    vmem = pl.BlockSpec(memory_space=pltpu.VMEM)
    return pl.pallas_call(
        body, in_specs=[vmem] * 4, out_specs=[vmem] * 3, out_shape=[jax.ShapeDtypeStruct(w.shape, F32)] * 3,
        name=name)(w, g, m, v)


def _block_diag(w):
    h, a, b = w.shape
    eye = jnp.eye(h, dtype=w.dtype)
    return (eye[:, None, :, None] * w[:, :, None, :]).reshape(h * a, h * b)


def _diag_blocks(w, h):
    a = w.shape[0] // h
    return jnp.stack([w[i * a:(i + 1) * a, i * a:(i + 1) * a] for i in range(h)])


def _local_step(x, mem, target, p, hooks=None):
    g = {}
    p = dict(p)
    w_rg, w_ig = _block_diag(p["w_rgate"]).astype(BF16), _block_diag(p["w_igate"]).astype(BF16)

    (gu1, f1, xb), arrived = _ffn_fwd(x, p["ffn1_w13"], p["ffn1_w2"], name="ffn1_fwd",
                                  job=hooks and hooks.gather_job(_GATHER_MIXER))
    if hooks:
        p.update(hooks.gathered(_GATHER_MIXER, arrived))
    w_in = p["w_in"]
    w_in_lru, w_in_q, w_in_kv = w_in[:, :2 * LRU_W], w_in[:, 2 * LRU_W:2 * LRU_W + SB_W], w_in[:, 2 * LRU_W + SB_W:]
    h1, h1b = _ln_fwd(x, f1, 0.5, p["ln1_g"], p["ln1_b"], name="ln1_fwd")
    xbg = _matmul(h1b, w_in_lru, name="proj_lru")
    q_sb = _matmul(h1b, w_in_q, out_dtype=BF16, scale=1.0 / math.sqrt(SB_HEAD), name="proj_q")
    kv_sb = _matmul(h1b, w_in_kv, out_dtype=BF16, name="proj_kv")
    xc, hseq, y_lru = _lru_fwd(xbg, p["conv_w"], p["conv_b"], w_rg, p["b_rgate"], w_ig, p["b_igate"],
                               p["lru_lambda"], name="lru_fwd")
    y_sb, arrived = _sb_fwd(q_sb, kv_sb, name="sb_fwd", job=hooks and hooks.gather_job(_GATHER_FFN2))
    if hooks:
        p.update(hooks.gathered(_GATHER_FFN2, arrived))
    ymix = _mixnorm_fwd(y_lru, y_sb, p["g_lru"], p["g_sb"], name="mixnorm_fwd")
    mix = _matmul(ymix, p["w_out"], name="mix_out")
    h2, h2b = _ln_fwd(h1, mix, 1.0, p["ln2_g"], p["ln2_b"], name="ln2_fwd")
    qm = _matmul(h2b, p["mem_wq"], out_dtype=BF16, name="cross_q")
    kv = _matmul(mem, p["mem_wkv"], name="cross_kv")
    o_cross = _cross_fwd(qm, kv, name="cross_fwd")
    cross = _matmul(o_cross, p["mem_wo"], name="cross_out")
    h3, h3b = _ln_fwd(h2, cross, 1.0, p["ln3_g"], p["ln3_b"], name="ln3_fwd")
    (gu2, f2, _), _ = _ffn_fwd(h3b, p["ffn2_w13"], p["ffn2_w2"], name="ffn2_fwd")

    dz4, dz4b, g["ln4_g"], g["ln4_b"], loss = _ln_bwd(h3, f2, 0.5, p["ln4_g"], p["ln4_b"], None, target=target, name="ln4_loss_bwd")
    dh3, dgu2, act2 = _ffn_bwd(dz4, gu2, p["ffn2_w13"], p["ffn2_w2"], name="ffn2_bwd")
    g["ffn2_w13"] = _matmul(h3b, dgu2, trans_a=True, tm=1024, tn=FF_CHUNK, tk=2048, name="ffn2_dw13")
    g["ffn2_w2"] = _matmul(act2, dz4b, trans_a=True, tm=FF_CHUNK, tn=1024, tk=2048, scale=0.5, name="ffn2_dw2")
    dz3, dz3b, g["ln3_g"], g["ln3_b"] = _ln_bwd(h2, cross, 1.0, p["ln3_g"], p["ln3_b"], dh3, name="ln3_bwd")
    g["mem_wo"] = _matmul(o_cross, dz3b, trans_a=True, tm=1024, tn=1024, tk=1024, name="cross_dwo")
    do_cross = _matmul(dz3b, p["mem_wo"], trans_b=True, out_dtype=BF16, name="cross_do")
    dqm, dkv = _cross_bwd(qm, kv, do_cross, name="cross_bwd")
    g["mem_wq"] = _matmul(h2b, dqm, trans_a=True, tm=1024, tn=1024, tk=1024, name="cross_dwq")
    g["mem_wkv"] = _matmul(mem, dkv, trans_a=True, tm=1024, tn=MEM_HD, tk=256, name="cross_dwkv")
    dh2 = _matmul(dqm, p["mem_wq"], trans_b=True, res=dz3, res_coeff=ALPHA, name="cross_dh")
    dz2, dz2b, g["ln2_g"], g["ln2_b"] = _ln_bwd(h1, mix, 1.0, p["ln2_g"], p["ln2_b"], dh2, name="ln2_bwd")
    g["w_out"] = _matmul(ymix, dz2b, trans_a=True, tm=1024, tn=1024, tk=1024, name="mix_dwout")
    dymix = _matmul(dz2b, p["w_out"], trans_b=True, name="mix_dy")
    (dy_lru, dy_sb, g["g_lru"], g["g_sb"]), arrived = _mixnorm_bwd(
        y_lru, y_sb, p["g_lru"], p["g_sb"], dymix, name="mixnorm_bwd", job=hooks and hooks.sibling_job(_REDUCE_EARLY, g))
    (dq, dk, dv), arrived = _sb_bwd(q_sb, kv_sb, y_sb, dy_sb, name="sb_bwd",
                                    job=hooks and hooks.chip_job(_REDUCE_EARLY, arrived))
    if hooks:
        hooks.reduced(_REDUCE_EARLY, arrived)
    dxc, dgate, dwr, dwi, g["b_rgate"], g["b_igate"], g["lru_lambda"] = _lru_bwd(
        xc, xbg, hseq, dy_lru, w_rg, p["b_rgate"], w_ig, p["b_igate"], p["lru_lambda"], name="lru_bwd")
    g["w_rgate"], g["w_igate"] = _diag_blocks(dwr, 8), _diag_blocks(dwi, 8)
    dxb, g["conv_w"], g["conv_b"] = _conv_bwd(dxc, xbg, p["conv_w"], name="conv_bwd")
    parts = [dxb, dgate, dq, dk, dv]
    g["w_in"] = jnp.concatenate(
        [_matmul(h1b, part, trans_a=True, tm=1024, tn=512, tk=1024, name=f"proj_dw{n}") for n, part in enumerate(parts)], axis=1)
    dh1 = _proj_bwd_dx(parts, w_in, dz2, name="proj_dh")
    dz1, dz1b, g["ln1_g"], g["ln1_b"] = _ln_bwd(x, f1, 0.5, p["ln1_g"], p["ln1_b"], dh1, name="ln1_bwd")
    grad_x, dgu1, act1 = _ffn_bwd(dz1, gu1, p["ffn1_w13"], p["ffn1_w2"], name="ffn1_bwd")
    g["ffn1_w13"] = _matmul(xb, dgu1, trans_a=True, tm=1024, tn=FF_CHUNK, tk=2048, name="ffn1_dw13")
    g["ffn1_w2"] = _matmul(act1, dz1b, trans_a=True, tm=FF_CHUNK, tn=1024, tk=2048, scale=0.5, name="ffn1_dw2")
    return loss, grad_x, g


_WEIGHTS = ['ffn1_w13', 'ffn1_w2', 'ln1_g', 'ln1_b', 'w_in', 'conv_w', 'conv_b', 'w_rgate', 'b_rgate', 'w_igate',
            'b_igate', 'lru_lambda', 'g_lru', 'g_sb', 'w_out', 'ln2_g', 'ln2_b', 'mem_wq', 'mem_wkv', 'mem_wo',
            'ln3_g', 'ln3_b', 'ffn2_w13', 'ffn2_w2', 'ln4_g', 'ln4_b']
_SHARDED = ['ffn1_w13', 'ffn1_w2', 'w_in', 'w_out', 'mem_wq', 'mem_wkv', 'mem_wo', 'ffn2_w13', 'ffn2_w2']
_SMALL = ['ln1_g', 'ln1_b', 'ln2_g', 'ln2_b', 'ln3_g', 'ln3_b', 'ln4_g', 'ln4_b', 'conv_b', 'b_rgate', 'b_igate',
          'lru_lambda', 'g_lru', 'g_sb', 'w_rgate', 'w_igate']


def _pack_small(d, conv_w_full=None):
    rows = [d[n].reshape(-1) for n in _SMALL]
    if conv_w_full is not None:
        rows.append(conv_w_full.reshape(-1))
    flat = jnp.concatenate(rows)
    pad = (-flat.shape[0]) % (8 * D_MODEL)
    return jnp.pad(flat, (0, pad)).reshape(-1, D_MODEL)


def _unpack_small(pack, like, with_conv):
    flat = pack.reshape(-1)
    out, off = {}, 0
    for n in _SMALL:
        size = math.prod(like[n].shape)
        out[n] = flat[off:off + size].reshape(like[n].shape)
        off += size
    conv = flat[off:off + CONV_W * LRU_W].reshape(CONV_W, LRU_W) if with_conv else None
    return out, conv


_GATHER_FIRST = ['ffn1_w13', 'ffn1_w2']
_GATHER_MIXER = ['w_in', 'conv_w', 'w_out', 'mem_wq', 'mem_wkv', 'mem_wo']
_GATHER_FFN2 = ['ffn2_w13', 'ffn2_w2']
_REDUCE_EARLY = ['ffn2_w13', 'ffn2_w2', 'mem_wo', 'mem_wq', 'mem_wkv', 'w_out']
_REDUCE_LATE = ['w_in', 'ffn1_w13', 'ffn1_w2']


def _weight_layout(n, full):
    if n in ('ffn1_w13', 'ffn2_w13', 'mem_wkv'):
        return full
    if n == 'w_in':
        return full.transpose(1, 0, 2).reshape(D_MODEL, -1)
    if n == 'conv_w':
        return full.transpose(1, 0, 2).reshape(CONV_W, LRU_W)
    return full.reshape(-1, full.shape[-1])


def _grad_blocks(n, g):
    if n == 'w_in':
        return g.reshape(D_MODEL, N_DEV, -1).transpose(1, 0, 2)
    return g.reshape((N_DEV, -1, g.shape[-1]))


class _Hooks:
    def __init__(self, w, c_idx):
        self.w, self.c_idx = w, c_idx
        self.sums, self.recv = {}, {}

    def shard(self, n):
        return self.w[n][0] if n == 'conv_w' else self.w[n][0].astype(BF16)

    def gather_job(self, names):
        return _GatherJob([self.shard(n) for n in names])

    def gathered(self, names, outs):
        return {n: _weight_layout(n, o) for n, o in zip(names, outs)}

    def sibling_job(self, names, g):
        self.blocks = [_grad_blocks(n, g[n]) for n in names]
        return _SiblingJob(self.blocks)

    def chip_job(self, names, from_sibling):
        for n, b, r in zip(names, self.blocks, from_sibling):
            self.sums[n] = _pair_add(b, r, self.c_idx, name=f"pair_add_{n}")
        return _ChipJob([self.sums[n][1] for n in names])

    def reduced(self, names, from_chips):
        self.recv.update(zip(names, from_chips))


def kernel(x, mem, ffn1_w13, ffn1_w2, ln1_g, ln1_b, w_in, conv_w, conv_b, w_rgate, b_rgate, w_igate, b_igate, lru_lambda, g_lru, g_sb, w_out, ln2_g, ln2_b, mem_wq, mem_wkv, mem_wo, ln3_g, ln3_b, ffn2_w13, ffn2_w2, ln4_g, ln4_b, loss_target, m_ffn1_w13, m_ffn1_w2, m_ln1_g, m_ln1_b, m_w_in, m_conv_w, m_conv_b, m_w_rgate, m_b_rgate, m_w_igate, m_b_igate, m_lru_lambda, m_g_lru, m_g_sb, m_w_out, m_ln2_g, m_ln2_b, m_mem_wq, m_mem_wkv, m_mem_wo, m_ln3_g, m_ln3_b, m_ffn2_w13, m_ffn2_w2, m_ln4_g, m_ln4_b, v_ffn1_w13, v_ffn1_w2, v_ln1_g, v_ln1_b, v_w_in, v_conv_w, v_conv_b, v_w_rgate, v_b_rgate, v_w_igate, v_b_igate, v_lru_lambda, v_g_lru, v_g_sb, v_w_out, v_ln2_g, v_ln2_b, v_mem_wq, v_mem_wkv, v_mem_wo, v_ln3_g, v_ln3_b, v_ffn2_w13, v_ffn2_w2, v_ln4_g, v_ln4_b):
    args = locals()
    w = {n: args[n] for n in _WEIGHTS}
    mom = {n: args["m_" + n] for n in _WEIGHTS}
    var = {n: args["v_" + n] for n in _WEIGHTS}
    ix, iy, ic = lax.axis_index("x"), lax.axis_index("y"), lax.axis_index("c")

    c_idx = jnp.reshape(ic, (1,)).astype(jnp.int32)
    j_idx = jnp.reshape(2 * ix + iy, (1,)).astype(jnp.int32)
    hooks = _Hooks(w, c_idx)
    p = {n: (w[n][0] if w[n].ndim == 4 else w[n]) for n in _WEIGHTS if n not in _SHARDED and n != 'conv_w'}
    p.update(hooks.gathered(_GATHER_FIRST, _run_job(hooks.gather_job(_GATHER_FIRST), name="gather_ffn1")))
    loss_local, grad_x, g = _local_step(x[0], mem[0], loss_target[0], p, hooks)
    loss = lax.psum(loss_local[0, 0], ("x", "y", "c"))

    from_sibling = _run_job(hooks.sibling_job(_REDUCE_LATE, g), name="reduce_sibling")
    from_chips = _run_job(hooks.chip_job(_REDUCE_LATE, from_sibling), name="reduce_chips")
    hooks.reduced(_REDUCE_LATE, from_chips)
    grads, delta, new_m, new_v = {}, {}, {}, {}
    for n in _SHARDED:
        grads[n], delta[n], new_m[n], new_v[n] = (
            o.reshape(w[n].shape) for o in _reduce_adam(hooks.sums[n][0], hooks.recv[n], j_idx, w[n][0], mom[n][0],
                                                        var[n][0], name=f"adam_{n}"))

    total = _small_all_reduce(_pack_small(g, conv_w_full=g["conv_w"]), name="reduce_small")
    gsmall, gconv = _unpack_small(total, w, with_conv=True)
    me = 4 * ix + 2 * iy + ic
    gsmall["conv_w"] = lax.dynamic_slice_in_dim(gconv, me * (LRU_W // N_DEV), LRU_W // N_DEV, axis=1)[None]
    small_names = _SMALL + ["conv_w"]
    pk = lambda d: _pack_small({n: d[n] for n in _SMALL}, conv_w_full=jnp.pad(d["conv_w"].reshape(-1), (0, CONV_W * LRU_W - CONV_W * LRU_W // N_DEV)))
    d_s, m_s, v_s = _adam_small(pk(w), pk(gsmall), pk(mom), pk(var), name="adam_small")
    for src, dst in ((d_s, delta), (m_s, new_m), (v_s, new_v)):
        vals, conv = _unpack_small(src, w, with_conv=True)
        dst.update(vals)
        dst["conv_w"] = conv.reshape(-1)[:CONV_W * LRU_W // N_DEV].reshape(w["conv_w"].shape)
    grads.update({n: gsmall[n].reshape(w[n].shape) for n in small_names})

    return (loss, grad_x[None], *[grads[n] for n in _WEIGHTS], *[delta[n] for n in _WEIGHTS],
            *[new_m[n] for n in _WEIGHTS], *[new_v[n] for n in _WEIGHTS])
```

```python
import functools
import math

import jax
import jax.numpy as jnp
from jax import lax
from jax.experimental import pallas as pl
from jax.experimental.pallas import tpu as pltpu

F32, BF16 = jnp.float32, jnp.bfloat16

D_MODEL = 1024
LRU_W = 512
SB_W = 512
SB_PAIR = 128
SB_HEAD = 64
SB_ROWS = 128
SB_SKIP = -110.0
D_FF = 2688
FF_CHUNK = 672
N_DEV = 8
MEM_HEADS = 4
MEM_HD = 256
CONV_W = 4
ALPHA = 2.0 ** 0.25
LN_EPS = 1e-5
RMS_EPS = 1e-6
LRU_C = 8.0
ADAM_LR, ADAM_B1, ADAM_B2, ADAM_EPS, ADAM_WD, ADAM_STEP = 0.001, 0.9, 0.999, 1e-08, 0.01, 10

NN = (((1,), (0,)), ((), ()))
NT = (((1,), (1,)), ((), ()))
MESH = pl.DeviceIdType.MESH
VMEM_LIMIT_MB = 56


def _dot(a, b, dn=NN):
    return lax.dot_general(a, b, dn, preferred_element_type=F32)


def _cparams(n_axes, vmem_mb=None):
    kw = dict(dimension_semantics=("arbitrary",) * n_axes)
    if vmem_mb is not None:
        kw["vmem_limit_bytes"] = vmem_mb << 20
    return pltpu.CompilerParams(**kw)


def _row_tile(rows, want):
    if rows <= want:
        return rows
    t = want - want % 8
    while rows % t:
        t -= 8
    return t


def _matmul(a, b, *, name, out_dtype=F32, trans_a=False, trans_b=False, tm=512, tn=1024, tk=1024,
            scale=1.0, res=None, res_coeff=1.0, job=None):
    a_chunked, b_chunked = a.ndim == 3, b.ndim == 3
    nc = a.shape[0] if a_chunked else (b.shape[0] if b_chunked else 1)
    a2, b2 = a.shape[-2:], b.shape[-2:]
    (kdim, m) = a2 if trans_a else a2[::-1]
    n = b2[0] if trans_b else b2[1]
    tm, tn, tk = min(tm, m), min(tn, n), min(tk, kdim)
    assert m % tm == 0 and n % tn == 0 and kdim % tk == 0, (name, m, n, kdim)
    nk = kdim // tk

    def a_idx(c, i, j, k):
        idx = (k, i) if trans_a else (i, k)
        return (c,) + idx if a_chunked else idx

    def b_idx(c, i, j, k):
        idx = (j, k) if trans_b else (k, j)
        return (c,) + idx if b_chunked else idx

    a_blk = (tk, tm) if trans_a else (tm, tk)
    b_blk = (tn, tk) if trans_b else (tk, tn)
    in_specs = [pl.BlockSpec(((None,) + a_blk) if a_chunked else a_blk, a_idx),
                pl.BlockSpec(((None,) + b_blk) if b_chunked else b_blk, b_idx)]
    args = [a, b]
    if res is not None:
        in_specs.append(pl.BlockSpec((None, tm, tn), lambda c, i, j, k: (c, i, j)))
        args.append(res.reshape((nc, m, n)))

    def body(*refs):
        if res is not None:
            a_ref, b_ref, r_ref, o_ref, acc_ref = refs
        else:
            a_ref, b_ref, o_ref, acc_ref = refs
        k = pl.program_id(3)

        @pl.when(k == 0)
        def _():
            acc_ref[...] = jnp.zeros_like(acc_ref)

        av = a_ref[...]
        if trans_a:
            av = av.astype(F32).T
        acc_ref[...] += _dot(av.astype(BF16), b_ref[...].astype(BF16), NT if trans_b else NN)

        @pl.when(k == nk - 1)
        def _():
            out = acc_ref[...]
            if scale != 1.0:
                out = out * scale
            if res is not None:
                out = out + res_coeff * r_ref[...]
            o_ref[...] = out.astype(out_dtype)

    grid = (nc, m // tm, n // tn, nk)

    def step():
        idx = pl.program_id(0)
        for ax in range(1, 4):
            idx = idx * grid[ax] + pl.program_id(ax)
        return idx

    (out,), job_out = _pallas_with_job(
        body, job, step, math.prod(grid), grid=grid, in_specs=in_specs,
        out_specs=[pl.BlockSpec((None, tm, tn), lambda c, i, j, k: (c, i, j))],
        out_shape=[jax.ShapeDtypeStruct((nc, m, n), out_dtype)],
        scratch_shapes=[pltpu.VMEM((tm, tn), F32)],
        compiler_params=_cparams(4, VMEM_LIMIT_MB), name=name, args=args)
    out = out if (a_chunked or b_chunked) else out[0]
    return out if job is None else (out, job_out)


def _ln_math(z, g, b):
    mu = jnp.mean(z, axis=-1, keepdims=True)
    zc = z - mu
    var = jnp.mean(zc * zc, axis=-1, keepdims=True)
    return zc * lax.rsqrt(var + LN_EPS) * g + b


def _row_spec(tm, d):
    return pl.BlockSpec((tm, d), lambda i: (i, 0))


def _par_spec(d, rows=1):
    return pl.BlockSpec((rows, d), lambda i: (0, 0))


def _ln_fwd(x, f, fscale, g, b, *, name):
    s, d = x.shape
    tm = min(512, s)

    def body(x_ref, f_ref, g_ref, b_ref, o_ref, ob_ref):
        z = ALPHA * x_ref[...] + fscale * f_ref[...]
        h = _ln_math(z, g_ref[...], b_ref[...])
        o_ref[...] = h
        ob_ref[...] = h.astype(BF16)

    return pl.pallas_call(
        body, grid=(s // tm,), in_specs=[_row_spec(tm, d), _row_spec(tm, d), _par_spec(d), _par_spec(d)],
        out_specs=[_row_spec(tm, d), _row_spec(tm, d)],
        out_shape=[jax.ShapeDtypeStruct((s, d), F32), jax.ShapeDtypeStruct((s, d), BF16)],
        compiler_params=_cparams(1), name=name)(x, f, g, b)


def _ln_bwd(x, f, fscale, g, b, dy, *, name, target=None):
    s, d = x.shape
    tm = min(512, s)
    with_loss = target is not None

    def body(*refs):
        if with_loss:
            x_ref, f_ref, g_ref, b_ref, t_ref, dz_ref, dzb_ref, dg_ref, db_ref, loss_ref = refs
        else:
            x_ref, f_ref, g_ref, b_ref, dy_ref, dz_ref, dzb_ref, dg_ref, db_ref = refs
        i = pl.program_id(0)

        @pl.when(i == 0)
        def _():
            dg_ref[...] = jnp.zeros_like(dg_ref)
            db_ref[...] = jnp.zeros_like(db_ref)
            if with_loss:
                loss_ref[...] = jnp.zeros_like(loss_ref)

        z = ALPHA * x_ref[...] + fscale * f_ref[...]
        y, vjp = jax.vjp(_ln_math, z, g_ref[...], b_ref[...])
        if with_loss:
            err = y - t_ref[...]
            sq = jnp.sum(err * err, axis=1, keepdims=True)
            loss_ref[...] += jnp.sum(sq, axis=0, keepdims=True) * (0.5 / d)
            cot = err * (1.0 / d)
        else:
            cot = dy_ref[...]
        dz, dg, db = vjp(cot)
        dz_ref[...] = dz
        dzb_ref[...] = dz.astype(BF16)
        dg_ref[...] += dg
        db_ref[...] += db

    in_specs = [_row_spec(tm, d), _row_spec(tm, d), _par_spec(d), _par_spec(d), _row_spec(tm, d)]
    out_specs = [_row_spec(tm, d), _row_spec(tm, d), _par_spec(d), _par_spec(d)]
    out_shape = [jax.ShapeDtypeStruct((s, d), F32), jax.ShapeDtypeStruct((s, d), BF16),
                 jax.ShapeDtypeStruct((1, d), F32), jax.ShapeDtypeStruct((1, d), F32)]
    if with_loss:
        out_specs.append(pl.BlockSpec((1, 1), lambda i: (0, 0)))
        out_shape.append(jax.ShapeDtypeStruct((1, 1), F32))
    return pl.pallas_call(
        body, grid=(s // tm,), in_specs=in_specs, out_specs=out_specs, out_shape=out_shape,
        compiler_params=_cparams(1), name=name)(x, f, g, b, target if with_loss else dy)


def _load_weights_once(pairs, sem):
    @pl.when(pl.program_id(0) == 0)
    def _():
        copies = [pltpu.make_async_copy(src, dst, sem.at[n]) for n, (src, dst) in enumerate(pairs)]
        for c in copies:
            c.start()
        for c in copies:
            c.wait()


def _ffn_fwd(h, w13c, w2, *, name, job=None):
    s, d = h.shape
    tm = min(256, s)
    half = N_DEV // 2

    def body(h_ref, w13_hbm, w2_hbm, gu_ref, f_ref, hb_ref, w13_v, w2_v, sem):
        _load_weights_once([(w13_hbm, w13_v), (w2_hbm, w2_v)], sem)
        hb = h_ref[...].astype(BF16)
        hb_ref[...] = hb
        acc = jnp.zeros((tm, d), F32)
        for k in range(half):
            g = _dot(hb, w13_v[k])
            u = _dot(hb, w13_v[k + half])
            gu_ref[k] = g.astype(BF16)
            gu_ref[k + half] = u.astype(BF16)
            a = g * jax.nn.sigmoid(g) * u
            acc = acc + _dot(a.astype(BF16), w2_v[pl.ds(k * FF_CHUNK, FF_CHUNK), :])
        f_ref[...] = acc

    any_spec = pl.BlockSpec(memory_space=pl.ANY)
    return _pallas_with_job(
        body, job, lambda: pl.program_id(0), s // tm,
        grid=(s // tm,), in_specs=[_row_spec(tm, d), any_spec, any_spec],
        out_specs=[pl.BlockSpec((N_DEV, tm, FF_CHUNK), lambda i: (0, i, 0)), _row_spec(tm, d), _row_spec(tm, d)],
        out_shape=[jax.ShapeDtypeStruct((N_DEV, s, FF_CHUNK), BF16), jax.ShapeDtypeStruct((s, d), F32),
                   jax.ShapeDtypeStruct((s, d), BF16)],
        scratch_shapes=[pltpu.VMEM(w13c.shape, BF16), pltpu.VMEM(w2.shape, BF16), pltpu.SemaphoreType.DMA((2,))],
        compiler_params=_cparams(1, VMEM_LIMIT_MB), name=name, args=(h, w13c, w2))


def _ffn_bwd(dz, gu, w13c, w2, *, name, job=None):
    s, d = dz.shape
    tm = min(256, s)
    half = N_DEV // 2

    def body(dz_ref, gu_ref, w13_hbm, w2_hbm, dh_ref, dgu_ref, act_ref, w13_v, w2_v, sem):
        _load_weights_once([(w13_hbm, w13_v), (w2_hbm, w2_v)], sem)
        dzv = dz_ref[...]
        dfb = (0.5 * dzv).astype(BF16)
        acc = ALPHA * dzv
        for k in range(half):
            g = gu_ref[k].astype(F32)
            u = gu_ref[k + half].astype(F32)
            da = _dot(dfb, w2_v[pl.ds(k * FF_CHUNK, FF_CHUNK), :], NT)
            sg = jax.nn.sigmoid(g)
            silu = g * sg
            dg = (da * u * (sg * (1.0 + g * (1.0 - sg)))).astype(BF16)
            du = (da * silu).astype(BF16)
            act_ref[k] = (silu * u).astype(BF16)
            dgu_ref[k] = dg
            dgu_ref[k + half] = du
            acc = acc + _dot(dg, w13_v[k], NT) + _dot(du, w13_v[k + half], NT)
        dh_ref[...] = acc

    any_spec = pl.BlockSpec(memory_space=pl.ANY)
    return _pallas_with_job(
        body, job, lambda: pl.program_id(0), s // tm, grid=(s // tm,),
        in_specs=[_row_spec(tm, d), pl.BlockSpec((N_DEV, tm, FF_CHUNK), lambda i: (0, i, 0)), any_spec, any_spec],
        out_specs=[_row_spec(tm, d), pl.BlockSpec((N_DEV, tm, FF_CHUNK), lambda i: (0, i, 0)),
                   pl.BlockSpec((half, tm, FF_CHUNK), lambda i: (0, i, 0))],
        out_shape=[jax.ShapeDtypeStruct((s, d), F32), jax.ShapeDtypeStruct((N_DEV, s, FF_CHUNK), BF16),
                   jax.ShapeDtypeStruct((half, s, FF_CHUNK), BF16)],
        scratch_shapes=[pltpu.VMEM(w13c.shape, BF16), pltpu.VMEM(w2.shape, BF16), pltpu.SemaphoreType.DMA((2,))],
        compiler_params=_cparams(1, VMEM_LIMIT_MB), name=name, args=(dz, gu, w13c, w2))


def _shift_down(x, prev8, d):
    t, c = x.shape
    row = lax.broadcasted_iota(jnp.int32, (t, c), 0)
    xr = pltpu.roll(x, d, 0)
    pr = pltpu.roll(prev8, d, 0)
    if t > 8:
        pr = jnp.concatenate([pr, jnp.zeros((t - 8, c), x.dtype)], axis=0)
    return jnp.where(row < d, pr, xr)


def _shift_up(x, next8, d):
    t, c = x.shape
    row = lax.broadcasted_iota(jnp.int32, (t, c), 0)
    xr = pltpu.roll(x, t - d, 0)
    nr = pltpu.roll(next8, 8 - d, 0)
    if t > 8:
        nr = jnp.concatenate([jnp.zeros((t - 8, c), x.dtype), nr], axis=0)
    return jnp.where(row >= t - d, nr, xr)


def _scan_fwd(a, u):
    t = a.shape[0]
    row = lax.broadcasted_iota(jnp.int32, a.shape, 0)
    d = 1
    while d < t:
        a_s, u_s = pltpu.roll(a, d, 0), pltpu.roll(u, d, 0)
        m = row >= d
        u = jnp.where(m, u + a * u_s, u)
        a = jnp.where(m, a * a_s, a)
        d *= 2
    return a, u


def _scan_bwd(b, x):
    t = b.shape[0]
    row = lax.broadcasted_iota(jnp.int32, b.shape, 0)
    d = 1
    while d < t:
        b_s, x_s = pltpu.roll(b, t - d, 0), pltpu.roll(x, t - d, 0)
        m = row < t - d
        x = jnp.where(m, x + b * x_s, x)
        b = jnp.where(m, b * b_s, b)
        d *= 2
    return b, x


def _lru_elem(xc, pr, pi, b_r, b_i, lam):
    r = jax.nn.sigmoid(pr + b_r)
    ig = jax.nn.sigmoid(pi + b_i)
    softplus_neg_lam = jnp.maximum(-lam, 0.0) + jnp.log1p(jnp.exp(-jnp.abs(lam)))
    log_a = (-LRU_C) * r * softplus_neg_lam
    a = jnp.exp(log_a)
    one_minus_a2 = -jnp.tanh(log_a) * (jnp.exp(2.0 * log_a) + 1.0)
    u = jnp.sqrt(one_minus_a2) * (ig * xc)
    return a, u


def _gelu_tanh(x):
    return 0.5 * x * (1.0 + jnp.tanh(math.sqrt(2.0 / math.pi) * (x + 0.044715 * (x * x * x))))


def _conv_fwd(xb, prev8, w, b):
    out = b + w[3:4, :] * xb
    for d in range(1, CONV_W):
        out = out + w[3 - d:4 - d, :] * _shift_down(xb, prev8, d)
    return out


def _lru_fwd(xbg, conv_w, conv_b, w_rg, b_rg, w_ig, b_ig, lam, *, name):
    s = xbg.shape[0]
    c = LRU_W
    t = min(512, s)
    t8 = t // 8

    def body(xb_ref, gate_ref, prev_ref, cw_ref, cb_ref, wr_ref, br_ref, wi_ref, bi_ref, lam_ref,
             xc_ref, h_ref, y_ref, carry_ref):
        i = pl.program_id(0)

        @pl.when(i == 0)
        def _():
            carry_ref[...] = jnp.zeros_like(carry_ref)

        prev8 = jnp.where(i > 0, prev_ref[...], 0.0)
        xc = _conv_fwd(xb_ref[...], prev8, cw_ref[...], cb_ref[...])
        xcb = xc.astype(BF16)
        a, u = _lru_elem(xc, _dot(xcb, wr_ref[...]), _dot(xcb, wi_ref[...]), br_ref[...], bi_ref[...], lam_ref[...])
        a_cum, h0 = _scan_fwd(a, u)
        h = h0 + a_cum * carry_ref[0:1, :]
        carry_ref[...] = jnp.broadcast_to(h[t - 1:t, :], carry_ref.shape)
        xc_ref[...] = xc
        h_ref[...] = h
        y_ref[...] = h * _gelu_tanh(gate_ref[...])

    tile = lambda col: pl.BlockSpec((t, c), lambda i: (i, col))
    par = lambda rows: pl.BlockSpec((rows, c), lambda i: (0, 0))
    out_spec = pl.BlockSpec((t, c), lambda i: (i, 0))
    return pl.pallas_call(
        body, grid=(s // t,),
        in_specs=[tile(0), tile(1), pl.BlockSpec((8, c), lambda i: (jnp.maximum(i * t8 - 1, 0), 0)),
                  par(CONV_W), par(1), pl.BlockSpec((c, c), lambda i: (0, 0)), par(1),
                  pl.BlockSpec((c, c), lambda i: (0, 0)), par(1), par(1)],
        out_specs=[out_spec, out_spec, out_spec],
        out_shape=[jax.ShapeDtypeStruct((s, c), F32)] * 3,
        scratch_shapes=[pltpu.VMEM((8, c), F32)],
        compiler_params=_cparams(1, VMEM_LIMIT_MB), name=name)(
            xbg, xbg, xbg, conv_w, conv_b, w_rg, b_rg, w_ig, b_ig, lam)


def _lru_bwd(xc, xbg, hseq, dy, w_rg, b_rg, w_ig, b_ig, lam, *, name):
    s, c = xc.shape
    t = min(512, s)
    t8 = t // 8
    nt = s // t

    def body(xc_ref, gate_ref, h_ref, hprev_ref, dy_ref, wr_ref, br_ref, wi_ref, bi_ref, lam_ref,
             dxc_ref, dgate_ref, dwr_ref, dwi_ref, dbr_ref, dbi_ref, dlam_ref, a8_ref, l8_ref):
        i = pl.program_id(0)

        @pl.when(i == 0)
        def _():
            for r in (dwr_ref, dwi_ref, dbr_ref, dbi_ref, dlam_ref, a8_ref, l8_ref):
                r[...] = jnp.zeros_like(r)

        xc_v = xc_ref[...]
        xcb = xc_v.astype(BF16)
        (a, _), vjp = jax.vjp(_lru_elem, xc_v, _dot(xcb, wr_ref[...]), _dot(xcb, wi_ref[...]),
                              br_ref[...], bi_ref[...], lam_ref[...])
        gl, vjp_gelu = jax.vjp(_gelu_tanh, gate_ref[...])
        dyv, hs = dy_ref[...], h_ref[...]
        (dgate,) = vjp_gelu(dyv * hs)
        b_next = _shift_up(a, a8_ref[...], 1)
        b_cum, l0 = _scan_bwd(b_next, dyv * gl)
        lt = l0 + b_cum * l8_ref[0:1, :]
        a8_ref[...] = a[0:8, :]
        l8_ref[...] = lt[0:8, :]
        hprev8 = jnp.where(i < nt - 1, hprev_ref[...], 0.0)
        da = lt * _shift_down(hs, hprev8, 1)
        dxc_e, dpr, dpi, dbr, dbi, dlam = vjp((da, lt))
        dprb, dpib = dpr.astype(BF16), dpi.astype(BF16)
        dxc_ref[...] = dxc_e + _dot(dprb, wr_ref[...], NT) + _dot(dpib, wi_ref[...], NT)
        dgate_ref[...] = dgate.astype(BF16)
        xct = xc_v.T.astype(BF16)
        dwr_ref[...] += _dot(xct, dprb)
        dwi_ref[...] += _dot(xct, dpib)
        dbr_ref[...] += dbr
        dbi_ref[...] += dbi
        dlam_ref[...] += dlam

    rev = lambda col: pl.BlockSpec((t, c), lambda i: (nt - 1 - i, col))
    par = pl.BlockSpec((1, c), lambda i: (0, 0))
    sq = pl.BlockSpec((c, c), lambda i: (0, 0))
    return pl.pallas_call(
        body, grid=(nt,),
        in_specs=[rev(0), rev(1), rev(0), pl.BlockSpec((8, c), lambda i: (jnp.maximum((nt - 1 - i) * t8 - 1, 0), 0)),
                  rev(0), sq, par, sq, par, par],
        out_specs=[rev(0), rev(0), sq, sq, par, par, par],
        out_shape=[jax.ShapeDtypeStruct((s, c), F32), jax.ShapeDtypeStruct((s, c), BF16)]
                  + [jax.ShapeDtypeStruct((c, c), F32)] * 2 + [jax.ShapeDtypeStruct((1, c), F32)] * 3,
        scratch_shapes=[pltpu.VMEM((8, c), F32), pltpu.VMEM((8, c), F32)],
        compiler_params=_cparams(1, VMEM_LIMIT_MB), name=name)(
            xc, xbg, hseq, hseq, dy, w_rg, b_rg, w_ig, b_ig, lam)


def _conv_bwd(dxc, xbg, conv_w, *, name):
    s, c = dxc.shape
    t = min(512, s)
    t8 = t // 8
    nt = s // t

    def body(dxc_ref, next_ref, xb_ref, prev_ref, w_ref, dxb_ref, dw_ref, db_ref):
        i = pl.program_id(0)

        @pl.when(i == 0)
        def _():
            dw_ref[...] = jnp.zeros_like(dw_ref)
            db_ref[...] = jnp.zeros_like(db_ref)

        g = dxc_ref[...]
        xb = xb_ref[...]
        w = w_ref[...]
        next8 = jnp.where(i < nt - 1, next_ref[...], 0.0)
        prev8 = jnp.where(i > 0, prev_ref[...], 0.0)
        dxb = w[3:4, :] * g
        dw_ref[3:4, :] += jnp.sum(g * xb, axis=0, keepdims=True)
        for d in range(1, CONV_W):
            dxb = dxb + w[3 - d:4 - d, :] * _shift_up(g, next8, d)
            dw_ref[3 - d:4 - d, :] += jnp.sum(g * _shift_down(xb, prev8, d), axis=0, keepdims=True)
        dxb_ref[...] = dxb.astype(BF16)
        db_ref[...] += jnp.sum(g, axis=0, keepdims=True)

    tile = pl.BlockSpec((t, c), lambda i: (i, 0))
    return pl.pallas_call(
        body, grid=(nt,),
        in_specs=[tile, pl.BlockSpec((8, c), lambda i: (jnp.minimum((i + 1) * t8, s // 8 - 1), 0)),
                  tile, pl.BlockSpec((8, c), lambda i: (jnp.maximum(i * t8 - 1, 0), 0)),
                  pl.BlockSpec((CONV_W, c), lambda i: (0, 0))],
        out_specs=[tile, pl.BlockSpec((CONV_W, c), lambda i: (0, 0)), pl.BlockSpec((1, c), lambda i: (0, 0))],
        out_shape=[jax.ShapeDtypeStruct((s, c), BF16), jax.ShapeDtypeStruct((CONV_W, c), F32),
                   jax.ShapeDtypeStruct((1, c), F32)],
        compiler_params=_cparams(1), name=name)(dxc, dxc, xbg, xbg, conv_w)


def _hi_lo_dot(x, tri):
    hi = x.astype(BF16)
    lo = (x - hi.astype(F32)).astype(BF16)
    return _dot(hi, tri) + _dot(lo, tri)


def _tri(tk, inclusive):
    r = lax.broadcasted_iota(jnp.int32, (tk, tk), 0)
    c = lax.broadcasted_iota(jnp.int32, (tk, tk), 1)
    return ((r >= c) if inclusive else (r > c)).astype(BF16)


def _split_heads(x):
    first = lax.broadcasted_iota(jnp.int32, x.shape, 1) < SB_HEAD
    zero = jnp.zeros_like(x)
    return first, (jnp.where(first, x, zero), jnp.where(first, zero, x))


def _sb_softplus_terms(z):
    ls = jnp.minimum(z, 0.0) - jnp.log(1.0 + jnp.exp(-jnp.abs(z)))
    return ls, ls - z


def _sb_alive(runs):
    m = runs[0]
    for r in runs[1:]:
        m = jnp.maximum(m, r)
    return jnp.max(m) > SB_SKIP


def _sb_fwd(q, kv, *, name, job=None):
    s = q.shape[0]
    tb = min(256, s)
    rc = min(SB_ROWS, tb)
    npair = SB_W // SB_PAIR
    chains = [(h, c) for h in range(2) for c in range(tb // rc)]

    def body(q_ref, k_ref, v_ref, o_ref):
        i = pl.program_id(1)
        first, qs = _split_heads(q_ref[...])
        tri = _tri(tb, False)
        causal = lax.broadcasted_iota(jnp.int32, (tb, tb), 1) < lax.broadcasted_iota(jnp.int32, (tb, tb), 0)
        causal_c = [causal[c * rc:(c + 1) * rc, :] for c in range(tb // rc)]

        def group(j_lo, width, carry, diag_last):
            k0 = pl.multiple_of(j_lo * tb, tb)
            kb = k_ref[pl.ds(k0, width * tb), :]
            vb = v_ref[pl.ds(k0, width * tb), :]
            zw = [_dot(qs[h][c * rc:(c + 1) * rc, :], kb, NT) for h, c in chains]
            cols = [slice(t * tb, (t + 1) * tb) for t in range(width)]
            terms = [[_sb_softplus_terms(z[:, cl]) for cl in cols] for z in zw]
            lss = [[ls for ls, _ in tc] for tc in terms]
            ns = [[n for _, n in tc] for tc in terms]
            if diag_last:
                ns = [nc[:-1] + [jnp.where(causal_c[c], nc[-1], 0.0)] for nc, (h, c) in zip(ns, chains)]
            sufs = [[_hi_lo_dot(n, tri) for n in nc] for nc in ns]
            out = []
            for (h, c), lc, nc, sc, (run, acc) in zip(chains, lss, ns, sufs, carry):
                ws = [None] * width
                for t in reversed(range(width)):
                    w = jnp.exp(lc[t] + (sc[t] + run))
                    if diag_last and t == width - 1:
                        w = jnp.where(causal_c[c], w, 0.0)
                    ws[t] = w.astype(BF16)
                    run = run + jnp.sum(nc[t], axis=1, keepdims=True)
                out.append((run, acc + _dot(ws[0] if width == 1 else jnp.concatenate(ws, axis=1), vb)))
            return tuple(out)

        zero = (jnp.zeros((rc, 1), F32), jnp.zeros((rc, SB_PAIR), F32))
        carry = lax.cond(i == 0, lambda c: group(0, 1, c, True), lambda c: group(i - 1, 2, c, True),
                         (zero,) * len(chains))

        def step(st):
            j, _, c = st
            c = group(j, 1, c, False)
            return j - 1, _sb_alive([run for run, _ in c]), c

        _, _, carry = lax.while_loop(lambda st: jnp.logical_and(st[0] >= 0, st[1]), step,
                                     (i - 2, _sb_alive([run for run, _ in carry]), carry))
        accs = [jnp.concatenate([carry[n][1] for n, (h, c) in enumerate(chains) if h == hh], axis=0) for hh in range(2)]
        o_ref[...] = jnp.where(first, accs[0], accs[1])

    (o,), job_out = _pallas_with_job(
        body, job, lambda: pl.program_id(0) * (s // tb) + pl.program_id(1), npair * (s // tb),
        grid=(npair, s // tb),
        in_specs=[pl.BlockSpec((tb, SB_PAIR), lambda p, i: (i, p)),
                  pl.BlockSpec((s, SB_PAIR), lambda p, i: (0, p)),
                  pl.BlockSpec((s, SB_PAIR), lambda p, i: (0, npair + p))],
        out_specs=[pl.BlockSpec((tb, SB_PAIR), lambda p, i: (i, p))],
        out_shape=[jax.ShapeDtypeStruct((s, SB_W), F32)],
        compiler_params=_cparams(2, VMEM_LIMIT_MB), name=name, args=(q, kv, kv))
    return o, job_out


def _sb_bwd(q, kv, o, do, *, name, job=None):
    s = q.shape[0]
    tb = min(256, s)
    rc = min(SB_ROWS, tb)
    npair = SB_W // SB_PAIR
    chains = [(h, c) for h in range(2) for c in range(tb // rc)]

    def body(q_ref, k_ref, v_ref, o_ref, do_ref, dq_ref, dkb_ref, dvb_ref, dk_ref, dv_ref):
        i = pl.program_id(1)

        @pl.when(i == 0)
        def _():
            dk_ref[...] = jnp.zeros_like(dk_ref)
            dv_ref[...] = jnp.zeros_like(dv_ref)

        first, qs = _split_heads(q_ref[...])
        dob = do_ref[...].astype(BF16)
        _, dos = _split_heads(dob)
        prod = dob.astype(F32) * o_ref[...]
        deltas = (jnp.sum(jnp.where(first, prod, 0.0), axis=1, keepdims=True),
                  jnp.sum(jnp.where(first, 0.0, prod), axis=1, keepdims=True))
        tri_x, tri_i = _tri(tb, False), _tri(tb, True)
        causal = lax.broadcasted_iota(jnp.int32, (tb, tb), 1) < lax.broadcasted_iota(jnp.int32, (tb, tb), 0)
        causal_c = [causal[c * rc:(c + 1) * rc, :] for c in range(tb // rc)]
        rows = [slice(c * rc, (c + 1) * rc) for c in range(tb // rc)]

        def group(j_lo, width, carry, diag_last):
            k0 = pl.multiple_of(j_lo * tb, tb)
            kb = k_ref[pl.ds(k0, width * tb), :]
            vb = v_ref[pl.ds(k0, width * tb), :]
            cols = [slice(t * tb, (t + 1) * tb) for t in range(width)]
            zw = [_dot(qs[h][rows[c], :], kb, NT) for h, c in chains]
            dww = [_dot(dos[h][rows[c], :], vb, NT) for h, c in chains]
            terms = [[_sb_softplus_terms(z[:, cl]) for cl in cols] for z in zw]
            lss = [[ls for ls, _ in tc] for tc in terms]
            ns = [[n for _, n in tc] for tc in terms]
            if diag_last:
                ns = [nc[:-1] + [jnp.where(causal_c[c], nc[-1], 0.0)] for nc, (h, c) in zip(ns, chains)]
            sufs = [[_hi_lo_dot(n, tri_x) for n in nc] for nc in ns]
            wbs, gs, runs_n = [], [], []
            for (h, c), lc, nc, sc, dw, (run_n, _, _) in zip(chains, lss, ns, sufs, dww, carry):
                wb, g = [None] * width, [None] * width
                for t in reversed(range(width)):
                    w = jnp.exp(lc[t] + (sc[t] + run_n))
                    if diag_last and t == width - 1:
                        w = jnp.where(causal_c[c], w, 0.0)
                    wb[t] = w.astype(BF16)
                    g[t] = wb[t].astype(F32) * dw[:, cols[t]]
                    run_n = run_n + jnp.sum(nc[t], axis=1, keepdims=True)
                wbs.append(wb)
                gs.append(g)
                runs_n.append(run_n)
            gsufs = [[_hi_lo_dot(g, tri_i) for g in gc] for gc in gs]
            out = []
            dk_t = [jnp.zeros((tb, SB_PAIR), F32) for _ in range(width)]
            dv_t = [jnp.zeros((tb, SB_PAIR), F32) for _ in range(width)]
            for (h, c), lc, gc, gsc, wb, run_n, (_, run_g, dq_acc) in zip(chains, lss, gs, gsufs, wbs, runs_n, carry):
                qh, doh, delta = qs[h][rows[c], :], dos[h][rows[c], :], deltas[h][rows[c], :]
                dzb = [None] * width
                for t in reversed(range(width)):
                    pre = delta - (gsc[t] + run_g)
                    dz = gc[t] - jnp.exp(lc[t]) * (gc[t] + pre)
                    if diag_last and t == width - 1:
                        dz = jnp.where(causal_c[c], dz, 0.0)
                    run_g = run_g + jnp.sum(gc[t], axis=1, keepdims=True)
                    dzb[t] = dz.astype(BF16)
                    dk_t[t] = dk_t[t] + _dot(dz.T.astype(BF16), qh)
                    dv_t[t] = dv_t[t] + _dot(wb[t].astype(F32).T.astype(BF16), doh)
                out.append((run_n, run_g, dq_acc + _dot(dzb[0] if width == 1 else jnp.concatenate(dzb, axis=1), kb)))
            for t in range(width):
                dk_ref[pl.ds(pl.multiple_of((j_lo + t) * tb, tb), tb), :] += dk_t[t]
                dv_ref[pl.ds(pl.multiple_of((j_lo + t) * tb, tb), tb), :] += dv_t[t]
            return tuple(out)

        zero = (jnp.zeros((rc, 1), F32), jnp.zeros((rc, 1), F32), jnp.zeros((rc, SB_PAIR), F32))
        carry = lax.cond(i == 0, lambda c: group(0, 1, c, True), lambda c: group(i - 1, 2, c, True),
                         (zero,) * len(chains))

        def step(st):
            j, _, c = st
            c = group(j, 1, c, False)
            return j - 1, _sb_alive([r[0] for r in c]), c

        _, _, carry = lax.while_loop(lambda st: jnp.logical_and(st[0] >= 0, st[1]), step,
                                     (i - 2, _sb_alive([r[0] for r in carry]), carry))
        dqs = [jnp.concatenate([carry[n][2] for n, (h, c) in enumerate(chains) if h == hh], axis=0) for hh in range(2)]
        dq_ref[...] = (jnp.where(first, dqs[0], dqs[1]) * (1.0 / math.sqrt(SB_HEAD))).astype(BF16)

        @pl.when(i == s // tb - 1)
        def _():
            dkb_ref[...] = dk_ref[...].astype(BF16)
            dvb_ref[...] = dv_ref[...].astype(BF16)

    qtile = pl.BlockSpec((tb, SB_PAIR), lambda p, i: (i, p))
    col = pl.BlockSpec((s, SB_PAIR), lambda p, i: (0, p))
    return _pallas_with_job(
        body, job, lambda: pl.program_id(0) * (s // tb) + pl.program_id(1), npair * (s // tb),
        grid=(npair, s // tb),
        in_specs=[qtile, col, pl.BlockSpec((s, SB_PAIR), lambda p, i: (0, npair + p)), qtile, qtile],
        out_specs=[qtile, col, col],
        out_shape=[jax.ShapeDtypeStruct((s, SB_W), BF16)] * 3,
        scratch_shapes=[pltpu.VMEM((s, SB_PAIR), F32)] * 2,
        compiler_params=_cparams(2, VMEM_LIMIT_MB), name=name, args=(q, kv, kv, o, do))


def _mixnorm_math(yl, ys, gl, gs):
    def rms(x, g):
        return x * lax.rsqrt(jnp.mean(x * x, axis=-1, keepdims=True) + RMS_EPS) * g
    return rms(yl, gl), rms(ys, gs)


def _mixnorm_fwd(yl, ys, gl, gs, *, name):
    s, c = yl.shape
    tm = min(512, s)

    def body(yl_ref, ys_ref, gl_ref, gs_ref, y_ref):
        a, b = _mixnorm_math(yl_ref[...], ys_ref[...], gl_ref[...], gs_ref[...])
        y_ref[:, 0:c] = a.astype(BF16)
        y_ref[:, c:2 * c] = b.astype(BF16)

    return pl.pallas_call(
        body, grid=(s // tm,), in_specs=[_row_spec(tm, c), _row_spec(tm, c), _par_spec(c), _par_spec(c)],
        out_specs=_row_spec(tm, 2 * c), out_shape=jax.ShapeDtypeStruct((s, 2 * c), BF16),
        compiler_params=_cparams(1), name=name)(yl, ys, gl, gs)


def _mixnorm_bwd(yl, ys, gl, gs, dy, *, name, job=None):
    s, c = yl.shape
    tm = min(512, s)

    def body(yl_ref, ys_ref, gl_ref, gs_ref, dy_ref, dyl_ref, dys_ref, dgl_ref, dgs_ref):
        @pl.when(pl.program_id(0) == 0)
        def _():
            dgl_ref[...] = jnp.zeros_like(dgl_ref)
            dgs_ref[...] = jnp.zeros_like(dgs_ref)

        _, vjp = jax.vjp(_mixnorm_math, yl_ref[...], ys_ref[...], gl_ref[...], gs_ref[...])
        dyl, dys, dgl, dgs = vjp((dy_ref[:, 0:c], dy_ref[:, c:2 * c]))
        dyl_ref[...] = dyl
        dys_ref[...] = dys
        dgl_ref[...] += dgl
        dgs_ref[...] += dgs

    return _pallas_with_job(
        body, job, lambda: pl.program_id(0), s // tm, grid=(s // tm,),
        in_specs=[_row_spec(tm, c), _row_spec(tm, c), _par_spec(c), _par_spec(c), _row_spec(tm, 2 * c)],
        out_specs=[_row_spec(tm, c), _row_spec(tm, c), _par_spec(c), _par_spec(c)],
        out_shape=[jax.ShapeDtypeStruct((s, c), F32)] * 2 + [jax.ShapeDtypeStruct((1, c), F32)] * 2,
        compiler_params=_cparams(1), name=name, args=(yl, ys, gl, gs, dy))


def _cross_probs(qh, kh):
    sc = _dot(qh, kh, NT) * (1.0 / math.sqrt(MEM_HD))
    e = jnp.exp(sc - jnp.max(sc, axis=-1, keepdims=True))
    return e / jnp.sum(e, axis=-1, keepdims=True)


def _cross_fwd(q, kv, *, name):
    s, d = q.shape
    mlen = kv.shape[1]
    tm = min(512, s)

    def body(q_ref, kv_ref, o_ref):
        for h in range(MEM_HEADS):
            cols = slice(h * MEM_HD, (h + 1) * MEM_HD)
            p = _cross_probs(q_ref[:, cols].astype(BF16), kv_ref[h].astype(BF16))
            o_ref[:, cols] = _dot(p.astype(BF16), kv_ref[MEM_HEADS + h].astype(BF16)).astype(BF16)

    return pl.pallas_call(
        body, grid=(s // tm,),
        in_specs=[_row_spec(tm, d), pl.BlockSpec((2 * MEM_HEADS, mlen, MEM_HD), lambda i: (0, 0, 0))],
        out_specs=_row_spec(tm, d), out_shape=jax.ShapeDtypeStruct((s, d), BF16),
        compiler_params=_cparams(1, VMEM_LIMIT_MB), name=name)(q, kv)


def _cross_bwd(q, kv, do, *, name):
    s, d = q.shape
    mlen = kv.shape[1]
    tm = min(512, s)

    def body(q_ref, kv_ref, do_ref, dq_ref, dkv_ref):
        @pl.when(pl.program_id(0) == 0)
        def _():
            dkv_ref[...] = jnp.zeros_like(dkv_ref)

        for h in range(MEM_HEADS):
            cols = slice(h * MEM_HD, (h + 1) * MEM_HD)
            qh = q_ref[:, cols].astype(BF16)
            kh = kv_ref[h].astype(BF16)
            doh = do_ref[:, cols].astype(BF16)
            p = _cross_probs(qh, kh)
            dp = _dot(doh, kv_ref[MEM_HEADS + h].astype(BF16), NT)
            ds = p * (dp - jnp.sum(dp * p, axis=-1, keepdims=True)) * (1.0 / math.sqrt(MEM_HD))
            dq_ref[:, cols] = _dot(ds.astype(BF16), kh).astype(BF16)
            dkv_ref[h] += _dot(ds.T.astype(BF16), qh)
            dkv_ref[MEM_HEADS + h] += _dot(p.T.astype(BF16), doh)

    kv_spec = pl.BlockSpec((2 * MEM_HEADS, mlen, MEM_HD), lambda i: (0, 0, 0))
    return pl.pallas_call(
        body, grid=(s // tm,), in_specs=[_row_spec(tm, d), kv_spec, _row_spec(tm, d)],
        out_specs=[_row_spec(tm, d), kv_spec],
        out_shape=[jax.ShapeDtypeStruct((s, d), BF16), jax.ShapeDtypeStruct(kv.shape, F32)],
        compiler_params=_cparams(1, VMEM_LIMIT_MB), name=name)(q, kv, do)


def _proj_bwd_dx(parts, w_in, dz, *, name, job=None):
    s = dz.shape[0]
    d = D_MODEL
    tm = min(256, s)
    widths = [p.shape[1] for p in parts]

    def body(*refs):
        part_refs, w_ref, dz_ref, o_ref = refs[:len(parts)], refs[len(parts)], refs[-2], refs[-1]
        acc = ALPHA * dz_ref[...]
        off = 0
        for p_ref, wd in zip(part_refs, widths):
            acc = acc + _dot(p_ref[...].astype(BF16), w_ref[:, off:off + wd], NT)
            off += wd
        o_ref[...] = acc

    (out,), job_out = _pallas_with_job(
        body, job, lambda: pl.program_id(0), s // tm, grid=(s // tm,),
        in_specs=[_row_spec(tm, wd) for wd in widths] + [pl.BlockSpec(w_in.shape, lambda i: (0, 0)), _row_spec(tm, d)],
        out_specs=[_row_spec(tm, d)], out_shape=[jax.ShapeDtypeStruct((s, d), F32)],
        compiler_params=_cparams(1, VMEM_LIMIT_MB), name=name, args=(*parts, w_in, dz))
    return out, job_out


def _proj_dw(h, parts, *, name):
    s, d = h.shape
    tk = min(1024, s)
    widths = [p.shape[1] for p in parts]

    def body(*refs):
        h_ref, part_refs, o_ref = refs[0], refs[1:-1], refs[-1]

        @pl.when(pl.program_id(0) == 0)
        def _():
            o_ref[...] = jnp.zeros_like(o_ref)

        ht = h_ref[...].astype(F32).T.astype(BF16)
        off = 0
        for p_ref, wd in zip(part_refs, widths):
            o_ref[:, off:off + wd] += _dot(ht, p_ref[...].astype(BF16))
            off += wd

    return pl.pallas_call(
        body, grid=(s // tk,), in_specs=[_row_spec(tk, d)] + [_row_spec(tk, wd) for wd in widths],
        out_specs=pl.BlockSpec((d, sum(widths)), lambda k: (0, 0)),
        out_shape=jax.ShapeDtypeStruct((d, sum(widths)), F32),
        compiler_params=_cparams(1, VMEM_LIMIT_MB), name=name)(h, *parts)


def _mesh_pos():
    return lax.axis_index("x"), lax.axis_index("y"), lax.axis_index("c")


class _GatherJob:
    n_phases = 3

    def __init__(self, shards):
        n = len(shards)
        self.inputs = list(shards)
        self.out_shape = [jax.ShapeDtypeStruct((N_DEV,) + a.shape, a.dtype) for a in shards]
        self.scratch = [pltpu.SemaphoreType.DMA((n, 7)), pltpu.SemaphoreType.DMA((n, 7)), pltpu.SemaphoreType.DMA((n,))]

    def phase(self, k, ins, outs, sems):
        send_sems, recv_sems, local_sems = sems
        n = len(ins)
        x, y, c = _mesh_pos()
        me, sibling = (x, y, c), (x, y, 1 - c)
        chips = [(1 - x, y), (x, 1 - y), (1 - x, 1 - y)]

        def copy(a, slot, block, to, src=None):
            dst = outs[a].at[4 * block[0] + 2 * block[1] + block[2]]
            return pltpu.make_async_remote_copy(
                src_ref=dst if src is None else src, dst_ref=dst,
                send_sem=send_sems.at[a, slot], recv_sem=recv_sems.at[a, slot], device_id=to, device_id_type=MESH)

        def mine():
            return [pltpu.make_async_copy(ins[a], outs[a].at[4 * x + 2 * y + c], local_sems.at[a]) for a in range(n)]

        def first():
            return [cp for a in range(n) for cp in
                    [copy(a, 0, me, sibling, src=ins[a])] + [copy(a, 1 + j, me, (*chip, c), src=ins[a]) for j, chip in enumerate(chips)]]

        def passed(a, j):
            return copy(a, 4 + j, (*chips[j], c), sibling)

        if k == 0:
            for cp in mine() + first():
                cp.start()
        elif k == 1:
            for a in range(n):
                for j, chip in enumerate(chips):
                    copy(a, 1 + j, (*chip, c), me).wait_recv()
                    passed(a, j).start()
        else:
            for a in range(n):
                copy(a, 0, sibling, me).wait_recv()
                for j, chip in enumerate(chips):
                    copy(a, 4 + j, (*chip, 1 - c), me).wait_recv()
            for cp in first() + [passed(a, j) for a in range(n) for j in range(3)]:
                cp.wait_send()
            for cp in mine():
                cp.wait()


class _SiblingJob:
    n_phases = 2

    def __init__(self, grads):
        n, half = len(grads), N_DEV // 2
        self.inputs = list(grads)
        self.out_shape = [jax.ShapeDtypeStruct((half,) + g.shape[1:], g.dtype) for g in grads]
        self.scratch = [pltpu.SemaphoreType.DMA((n, half)), pltpu.SemaphoreType.DMA((n, half))]

    def phase(self, k, ins, outs, sems):
        send_sems, recv_sems = sems
        x, y, c = _mesh_pos()
        copies = [pltpu.make_async_remote_copy(
            src_ref=ins[a].at[2 * j + (1 - c)], dst_ref=outs[a].at[j], send_sem=send_sems.at[a, j],
            recv_sem=recv_sems.at[a, j], device_id=(x, y, 1 - c), device_id_type=MESH)
            for a in range(len(ins)) for j in range(N_DEV // 2)]
        for cp in copies:
            if k == 0:
                cp.start()
            else:
                cp.wait()


class _ChipJob:
    n_phases = 2

    def __init__(self, partials):
        n, half = len(partials), N_DEV // 2
        self.inputs = list(partials)
        self.out_shape = [jax.ShapeDtypeStruct(p.shape, p.dtype) for p in partials]
        self.scratch = [pltpu.SemaphoreType.DMA((n, half)), pltpu.SemaphoreType.DMA((n, half)), pltpu.SemaphoreType.DMA((n,))]

    def phase(self, k, ins, outs, sems):
        send_sems, recv_sems, local_sems = sems
        n, half = len(ins), N_DEV // 2
        x, y, c = _mesh_pos()
        jme = 2 * x + y
        local = [pltpu.make_async_copy(ins[a].at[jme], outs[a].at[jme], local_sems.at[a]) for a in range(n)]
        for cp in local:
            if k == 0:
                cp.start()
            else:
                cp.wait()
        for j in range(half):
            @pl.when(jme != j)
            def _(j=j):
                for a in range(n):
                    if k == 0:
                        pltpu.make_async_remote_copy(
                            src_ref=ins[a].at[j], dst_ref=outs[a].at[jme], send_sem=send_sems.at[a, j],
                            recv_sem=recv_sems.at[a, jme], device_id=(j // 2, j % 2, c), device_id_type=MESH).start()
                    else:
                        arrive = pltpu.make_async_remote_copy(
                            src_ref=ins[a].at[j], dst_ref=outs[a].at[j], send_sem=send_sems.at[a, j],
                            recv_sem=recv_sems.at[a, j], device_id=(j // 2, j % 2, c), device_id_type=MESH)
                        arrive.wait_recv()
                        arrive.wait_send()


def _phase_steps(n_phases, n_steps):
    return [0, n_steps - 1] if n_phases == 2 else [0, n_steps // 2, n_steps - 1]


def _pallas_with_job(body, job, step_fn, n_steps, *, grid, in_specs, out_specs, out_shape, scratch_shapes=(),
                     compiler_params, name, args):
    in_specs, out_specs, out_shape, scratch_shapes = list(in_specs), list(out_specs), list(out_shape), list(scratch_shapes)
    if job is None:
        outs = pl.pallas_call(body, grid=grid, in_specs=in_specs, out_specs=out_specs, out_shape=out_shape,
                              scratch_shapes=scratch_shapes, compiler_params=compiler_params, name=name)(*args)
        return list(outs), None
    ni, no, ns = len(in_specs), len(out_specs), len(scratch_shapes)
    ji, jo = len(job.inputs), len(job.out_shape)
    steps = _phase_steps(job.n_phases, n_steps)

    def wrapped(*refs):
        own_in, job_in = refs[:ni], refs[ni:ni + ji]
        own_out, job_out = refs[ni + ji:ni + ji + no], refs[ni + ji + no:ni + ji + no + jo]
        own_scr, job_scr = refs[ni + ji + no + jo:ni + ji + no + jo + ns], refs[ni + ji + no + jo + ns:]
        step = step_fn()
        for k, at in enumerate(steps):
            @pl.when(step == at)
            def _(k=k):
                job.phase(k, job_in, job_out, job_scr)
        body(*own_in, *own_out, *own_scr)

    any_spec = pl.BlockSpec(memory_space=pl.ANY)
    outs = pl.pallas_call(
        wrapped, grid=grid, in_specs=in_specs + [any_spec] * ji, out_specs=out_specs + [any_spec] * jo,
        out_shape=out_shape + list(job.out_shape), scratch_shapes=scratch_shapes + list(job.scratch),
        compiler_params=compiler_params, name=name)(*args, *job.inputs)
    return list(outs[:no]), list(outs[no:])


def _run_job(job, *, name):
    ji, jo = len(job.inputs), len(job.out_shape)

    def body(*refs):
        for k in range(job.n_phases):
            job.phase(k, refs[:ji], refs[ji:ji + jo], refs[ji + jo:])

    any_spec = pl.BlockSpec(memory_space=pl.ANY)
    return pl.pallas_call(body, in_specs=[any_spec] * ji, out_specs=[any_spec] * jo, out_shape=list(job.out_shape),
                          scratch_shapes=list(job.scratch), name=name)(*job.inputs)


def _pair_add(grad, recv, c_idx, *, name):
    half, r, cdim = recv.shape
    tr = _row_tile(r, 512)

    def body(c_ref, g_ref, r_ref, t_ref, tb_ref):
        t = g_ref[...] + r_ref[...]
        t_ref[...] = t
        tb_ref[...] = t.astype(BF16)

    grid_spec = pltpu.PrefetchScalarGridSpec(
        num_scalar_prefetch=1, grid=(half, r // tr),
        in_specs=[pl.BlockSpec((None, tr, cdim), lambda j, i, c_ref: (2 * j + c_ref[0], i, 0)),
                  pl.BlockSpec((None, tr, cdim), lambda j, i, c_ref: (j, i, 0))],
        out_specs=[pl.BlockSpec((None, tr, cdim), lambda j, i, c_ref: (j, i, 0))] * 2)
    return pl.pallas_call(
        body, grid_spec=grid_spec,
        out_shape=[jax.ShapeDtypeStruct(recv.shape, F32), jax.ShapeDtypeStruct(recv.shape, BF16)],
        compiler_params=_cparams(2), name=name)(c_idx, grad, recv)


def _adam_math(w, g, m, v):
    m = ADAM_B1 * m + (1.0 - ADAM_B1) * g
    v = ADAM_B2 * v + (1.0 - ADAM_B2) * (g * g)
    m_hat = m / (1.0 - ADAM_B1 ** ADAM_STEP)
    v_hat = v / (1.0 - ADAM_B2 ** ADAM_STEP)
    delta = -ADAM_LR * (m_hat / (jnp.sqrt(v_hat) + ADAM_EPS) + ADAM_WD * w)
    return delta, m, v


def _reduce_adam(own, recv, j_idx, w, m, v, *, name):
    half, r, cdim = recv.shape
    tr = _row_tile(r, 256)

    def body(j_ref, own_ref, recv_ref, w_ref, m_ref, v_ref, g_ref, d_ref, nm_ref, nv_ref):
        jme = j_ref[0]
        g = own_ref[...]
        for sl in range(half):
            g = g + jnp.where(sl == jme, 0.0, recv_ref[sl].astype(F32))
        delta, nm, nv = _adam_math(w_ref[...], g, m_ref[...], v_ref[...])
        g_ref[...] = g
        d_ref[...] = delta
        nm_ref[...] = nm
        nv_ref[...] = nv

    tile = pl.BlockSpec((tr, cdim), lambda i, j_ref: (i, 0))
    grid_spec = pltpu.PrefetchScalarGridSpec(
        num_scalar_prefetch=1, grid=(r // tr,),
        in_specs=[pl.BlockSpec((None, tr, cdim), lambda i, j_ref: (j_ref[0], i, 0)),
                  pl.BlockSpec((half, tr, cdim), lambda i, j_ref: (0, i, 0)), tile, tile, tile],
        out_specs=[tile] * 4)
    return pl.pallas_call(
        body, grid_spec=grid_spec, out_shape=[jax.ShapeDtypeStruct((r, cdim), F32)] * 4,
        compiler_params=_cparams(1), name=name)(j_idx, own, recv, w, m, v)


class _SmallGatherJob:
    n_phases = 2

    def __init__(self, pack):
        self.inputs = [pack]
        self.out_shape = [jax.ShapeDtypeStruct((N_DEV,) + pack.shape, pack.dtype)]
        self.scratch = [pltpu.SemaphoreType.DMA((N_DEV,)), pltpu.SemaphoreType.DMA((N_DEV,)), pltpu.SemaphoreType.DMA(())]

    def phase(self, k, ins, outs, sems):
        (pack,), (land,), (send_sems, recv_sems, local_sem) = ins, outs, sems
        x, y, c = _mesh_pos()
        me = 4 * x + 2 * y + c
        mine = pltpu.make_async_copy(pack, land.at[me], local_sem)
        if k == 0:
            mine.start()
        else:
            mine.wait()
        for t in range(N_DEV):
            @pl.when(me != t)
            def _(t=t):
                peer = (t // 4, (t // 2) % 2, t % 2)
                if k == 0:
                    pltpu.make_async_remote_copy(
                        src_ref=pack, dst_ref=land.at[me], send_sem=send_sems.at[t], recv_sem=recv_sems.at[me],
                        device_id=peer, device_id_type=MESH).start()
                else:
                    cp = pltpu.make_async_remote_copy(
                        src_ref=pack, dst_ref=land.at[t], send_sem=send_sems.at[t], recv_sem=recv_sems.at[t],
                        device_id=peer, device_id_type=MESH)
                    cp.wait_recv()
                    cp.wait_send()


class _Jobs:
    def __init__(self, jobs):
        self.jobs = jobs
        self.n_phases = jobs[0].n_phases
        assert all(j.n_phases == self.n_phases for j in jobs)
        self.inputs = [a for j in jobs for a in j.inputs]
        self.out_shape = [o for j in jobs for o in j.out_shape]
        self.scratch = [sc for j in jobs for sc in j.scratch]

    def phase(self, k, ins, outs, sems):
        i = o = sc = 0
        for j in self.jobs:
            ni, no, nsc = len(j.inputs), len(j.out_shape), len(j.scratch)
            j.phase(k, ins[i:i + ni], outs[o:o + no], sems[sc:sc + nsc])
            i, o, sc = i + ni, o + no, sc + nsc


def _adam_small(w, land, m, v, *, name):
    def body(w_ref, land_ref, m_ref, v_ref, g_ref, d_ref, nm_ref, nv_ref):
        total = land_ref[0]
        for t in range(1, N_DEV):
            total = total + land_ref[t]
        g_ref[...] = total
        d_ref[...], nm_ref[...], nv_ref[...] = _adam_math(w_ref[...], total, m_ref[...], v_ref[...])

    vmem = pl.BlockSpec(memory_space=pltpu.VMEM)
    return pl.pallas_call(
        body, in_specs=[vmem] * 4, out_specs=[vmem] * 4, out_shape=[jax.ShapeDtypeStruct(w.shape, F32)] * 4,
        name=name)(w, land, m, v)


def _block_diag(w):
    h, a, b = w.shape
    eye = jnp.eye(h, dtype=w.dtype)
    return (eye[:, None, :, None] * w[:, :, None, :]).reshape(h * a, h * b)


def _diag_blocks(w, h):
    a = w.shape[0] // h
    return jnp.stack([w[i * a:(i + 1) * a, i * a:(i + 1) * a] for i in range(h)])


def _local_step(x, mem, target, p, hooks=None):
    g = {}
    p = dict(p)
    w_rg, w_ig = _block_diag(p["w_rgate"]).astype(BF16), _block_diag(p["w_igate"]).astype(BF16)

    (gu1, f1, xb), arrived = _ffn_fwd(x, p["ffn1_w13"], p["ffn1_w2"], name="ffn1_fwd",
                                  job=hooks and hooks.gather_job(_GATHER_MIXER))
    if hooks:
        p.update(hooks.gathered(_GATHER_MIXER, arrived))
    w_in = p["w_in"]
    w_in_lru, w_in_q, w_in_kv = w_in[:, :2 * LRU_W], w_in[:, 2 * LRU_W:2 * LRU_W + SB_W], w_in[:, 2 * LRU_W + SB_W:]
    h1, h1b = _ln_fwd(x, f1, 0.5, p["ln1_g"], p["ln1_b"], name="ln1_fwd")
    xbg = _matmul(h1b, w_in_lru, name="proj_lru")
    q_sb = _matmul(h1b, w_in_q, out_dtype=BF16, scale=1.0 / math.sqrt(SB_HEAD), name="proj_q")
    kv_sb = _matmul(h1b, w_in_kv, out_dtype=BF16, name="proj_kv")
    xc, hseq, y_lru = _lru_fwd(xbg, p["conv_w"], p["conv_b"], w_rg, p["b_rgate"], w_ig, p["b_igate"],
                               p["lru_lambda"], name="lru_fwd")
    y_sb, arrived = _sb_fwd(q_sb, kv_sb, name="sb_fwd", job=hooks and hooks.gather_job(_GATHER_FFN2))
    if hooks:
        p.update(hooks.gathered(_GATHER_FFN2, arrived))
    ymix = _mixnorm_fwd(y_lru, y_sb, p["g_lru"], p["g_sb"], name="mixnorm_fwd")
    mix = _matmul(ymix, p["w_out"], name="mix_out")
    h2, h2b = _ln_fwd(h1, mix, 1.0, p["ln2_g"], p["ln2_b"], name="ln2_fwd")
    qm = _matmul(h2b, p["mem_wq"], out_dtype=BF16, name="cross_q")
    kv = _matmul(mem, p["mem_wkv"], name="cross_kv")
    o_cross = _cross_fwd(qm, kv, name="cross_fwd")
    cross = _matmul(o_cross, p["mem_wo"], name="cross_out")
    h3, h3b = _ln_fwd(h2, cross, 1.0, p["ln3_g"], p["ln3_b"], name="ln3_fwd")
    (gu2, f2, _), _ = _ffn_fwd(h3b, p["ffn2_w13"], p["ffn2_w2"], name="ffn2_fwd")

    dz4, dz4b, g["ln4_g"], g["ln4_b"], loss = _ln_bwd(h3, f2, 0.5, p["ln4_g"], p["ln4_b"], None, target=target, name="ln4_loss_bwd")
    (dh3, dgu2, act2), _ = _ffn_bwd(dz4, gu2, p["ffn2_w13"], p["ffn2_w2"], name="ffn2_bwd")
    g["ffn2_w13"] = _matmul(h3b, dgu2, trans_a=True, tm=1024, tn=FF_CHUNK, tk=2048, name="ffn2_dw13")
    g["ffn2_w2"] = _matmul(act2, dz4b, trans_a=True, tm=FF_CHUNK, tn=1024, tk=2048, scale=0.5, name="ffn2_dw2")
    dz3, dz3b, g["ln3_g"], g["ln3_b"] = _ln_bwd(h2, cross, 1.0, p["ln3_g"], p["ln3_b"], dh3, name="ln3_bwd")
    g["mem_wo"] = _matmul(o_cross, dz3b, trans_a=True, tm=1024, tn=1024, tk=1024, name="cross_dwo")
    do_cross = _matmul(dz3b, p["mem_wo"], trans_b=True, out_dtype=BF16, name="cross_do")
    dqm, dkv = _cross_bwd(qm, kv, do_cross, name="cross_bwd")
    g["mem_wq"] = _matmul(h2b, dqm, trans_a=True, tm=1024, tn=1024, tk=1024, name="cross_dwq")
    g["mem_wkv"] = _matmul(mem, dkv, trans_a=True, tm=1024, tn=MEM_HD, tk=256, name="cross_dwkv")
    dh2 = _matmul(dqm, p["mem_wq"], trans_b=True, res=dz3, res_coeff=ALPHA, name="cross_dh")
    dz2, dz2b, g["ln2_g"], g["ln2_b"] = _ln_bwd(h1, mix, 1.0, p["ln2_g"], p["ln2_b"], dh2, name="ln2_bwd")
    g["w_out"] = _matmul(ymix, dz2b, trans_a=True, tm=1024, tn=1024, tk=1024, name="mix_dwout")
    dymix = _matmul(dz2b, p["w_out"], trans_b=True, name="mix_dy")
    (dy_lru, dy_sb, g["g_lru"], g["g_sb"]), arrived = _mixnorm_bwd(
        y_lru, y_sb, p["g_lru"], p["g_sb"], dymix, name="mixnorm_bwd", job=hooks and hooks.sibling_job(_REDUCE_EARLY, g))
    (dq, dk, dv), arrived = _sb_bwd(q_sb, kv_sb, y_sb, dy_sb, name="sb_bwd",
                                    job=hooks and hooks.chip_job(_REDUCE_EARLY, arrived))
    if hooks:
        hooks.reduced(_REDUCE_EARLY, arrived)
    dxc, dgate, dwr, dwi, g["b_rgate"], g["b_igate"], g["lru_lambda"] = _lru_bwd(
        xc, xbg, hseq, dy_lru, w_rg, p["b_rgate"], w_ig, p["b_igate"], p["lru_lambda"], name="lru_bwd")
    g["w_rgate"], g["w_igate"] = _diag_blocks(dwr, 8), _diag_blocks(dwi, 8)
    dxb, g["conv_w"], g["conv_b"] = _conv_bwd(dxc, xbg, p["conv_w"], name="conv_bwd")
    parts = [dxb, dgate, dq, dk, dv]
    g["w_in"] = _proj_dw(h1b, parts, name="proj_dw")
    dh1, arrived = _proj_bwd_dx(parts, w_in, dz2, name="proj_dh", job=hooks and hooks.sibling_job(['w_in'], g))
    dz1, dz1b, g["ln1_g"], g["ln1_b"] = _ln_bwd(x, f1, 0.5, p["ln1_g"], p["ln1_b"], dh1, name="ln1_bwd")
    (grad_x, dgu1, act1), arrived = _ffn_bwd(
        dz1, gu1, p["ffn1_w13"], p["ffn1_w2"], name="ffn1_bwd",
        job=hooks and _Jobs([hooks.chip_job(['w_in'], arrived), hooks.small_job(g)]))
    if hooks:
        hooks.reduced(['w_in'], arrived[:1])
        hooks.small_land = arrived[1]
    g["ffn1_w2"] = _matmul(act1, dz1b, trans_a=True, tm=FF_CHUNK, tn=1024, tk=2048, scale=0.5, name="ffn1_dw2")
    if not hooks:
        g["ffn1_w13"] = _matmul(xb, dgu1, trans_a=True, tm=1024, tn=FF_CHUNK, tk=2048, name="ffn1_dw13")
        return loss, grad_x, g
    arrived = _run_job(hooks.sibling_job(['ffn1_w2'], g), name="reduce_sibling_w2")
    g["ffn1_w13"], arrived = _matmul(xb, dgu1, trans_a=True, tm=1024, tn=FF_CHUNK, tk=2048, name="ffn1_dw13",
                                     job=hooks.chip_job(['ffn1_w2'], arrived))
    hooks.reduced(['ffn1_w2'], arrived)
    arrived = _run_job(hooks.sibling_job(['ffn1_w13'], g), name="reduce_sibling_w13")
    hooks.reduced(['ffn1_w13'], _run_job(hooks.chip_job(['ffn1_w13'], arrived), name="reduce_chips_w13"))
    return loss, grad_x, g


_WEIGHTS = ['ffn1_w13', 'ffn1_w2', 'ln1_g', 'ln1_b', 'w_in', 'conv_w', 'conv_b', 'w_rgate', 'b_rgate', 'w_igate',
            'b_igate', 'lru_lambda', 'g_lru', 'g_sb', 'w_out', 'ln2_g', 'ln2_b', 'mem_wq', 'mem_wkv', 'mem_wo',
            'ln3_g', 'ln3_b', 'ffn2_w13', 'ffn2_w2', 'ln4_g', 'ln4_b']
_SHARDED = ['ffn1_w13', 'ffn1_w2', 'w_in', 'w_out', 'mem_wq', 'mem_wkv', 'mem_wo', 'ffn2_w13', 'ffn2_w2']
_SMALL = ['ln1_g', 'ln1_b', 'ln2_g', 'ln2_b', 'ln3_g', 'ln3_b', 'ln4_g', 'ln4_b', 'conv_b', 'b_rgate', 'b_igate',
          'lru_lambda', 'g_lru', 'g_sb', 'w_rgate', 'w_igate']


def _pack_small(d, conv_w_full=None):
    rows = [d[n].reshape(-1) for n in _SMALL]
    if conv_w_full is not None:
        rows.append(conv_w_full.reshape(-1))
    flat = jnp.concatenate(rows)
    pad = (-flat.shape[0]) % (8 * D_MODEL)
    return jnp.pad(flat, (0, pad)).reshape(-1, D_MODEL)


def _unpack_small(pack, like, with_conv):
    flat = pack.reshape(-1)
    out, off = {}, 0
    for n in _SMALL:
        size = math.prod(like[n].shape)
        out[n] = flat[off:off + size].reshape(like[n].shape)
        off += size
    conv = flat[off:off + CONV_W * LRU_W].reshape(CONV_W, LRU_W) if with_conv else None
    return out, conv


_GATHER_FIRST = ['ffn1_w13', 'ffn1_w2']
_GATHER_MIXER = ['w_in', 'conv_w', 'w_out', 'mem_wq', 'mem_wkv', 'mem_wo']
_GATHER_FFN2 = ['ffn2_w13', 'ffn2_w2']
_REDUCE_EARLY = ['ffn2_w13', 'ffn2_w2', 'mem_wo', 'mem_wq', 'mem_wkv', 'w_out']


def _weight_layout(n, full):
    if n in ('ffn1_w13', 'ffn2_w13', 'mem_wkv'):
        return full
    if n == 'w_in':
        return full.transpose(1, 0, 2).reshape(D_MODEL, -1)
    if n == 'conv_w':
        return full.transpose(1, 0, 2).reshape(CONV_W, LRU_W)
    return full.reshape(-1, full.shape[-1])


def _grad_blocks(n, g):
    if n == 'w_in':
        return g.reshape(D_MODEL, N_DEV, -1).transpose(1, 0, 2)
    return g.reshape((N_DEV, -1, g.shape[-1]))


class _Hooks:
    def __init__(self, w, c_idx):
        self.w, self.c_idx = w, c_idx
        self.sums, self.recv = {}, {}

    def shard(self, n):
        return self.w[n][0] if n == 'conv_w' else self.w[n][0].astype(BF16)

    def gather_job(self, names):
        return _GatherJob([self.shard(n) for n in names])

    def gathered(self, names, outs):
        return {n: _weight_layout(n, o) for n, o in zip(names, outs)}

    def sibling_job(self, names, g):
        self.blocks = [_grad_blocks(n, g[n]) for n in names]
        return _SiblingJob(self.blocks)

    def chip_job(self, names, from_sibling):
        for n, b, r in zip(names, self.blocks, from_sibling):
            self.sums[n] = _pair_add(b, r, self.c_idx, name=f"pair_add_{n}")
        return _ChipJob([self.sums[n][1] for n in names])

    def reduced(self, names, from_chips):
        self.recv.update(zip(names, from_chips))

    def small_job(self, g):
        return _SmallGatherJob(_pack_small(g, conv_w_full=g["conv_w"]))


def kernel(x, mem, ffn1_w13, ffn1_w2, ln1_g, ln1_b, w_in, conv_w, conv_b, w_rgate, b_rgate, w_igate, b_igate, lru_lambda, g_lru, g_sb, w_out, ln2_g, ln2_b, mem_wq, mem_wkv, mem_wo, ln3_g, ln3_b, ffn2_w13, ffn2_w2, ln4_g, ln4_b, loss_target, m_ffn1_w13, m_ffn1_w2, m_ln1_g, m_ln1_b, m_w_in, m_conv_w, m_conv_b, m_w_rgate, m_b_rgate, m_w_igate, m_b_igate, m_lru_lambda, m_g_lru, m_g_sb, m_w_out, m_ln2_g, m_ln2_b, m_mem_wq, m_mem_wkv, m_mem_wo, m_ln3_g, m_ln3_b, m_ffn2_w13, m_ffn2_w2, m_ln4_g, m_ln4_b, v_ffn1_w13, v_ffn1_w2, v_ln1_g, v_ln1_b, v_w_in, v_conv_w, v_conv_b, v_w_rgate, v_b_rgate, v_w_igate, v_b_igate, v_lru_lambda, v_g_lru, v_g_sb, v_w_out, v_ln2_g, v_ln2_b, v_mem_wq, v_mem_wkv, v_mem_wo, v_ln3_g, v_ln3_b, v_ffn2_w13, v_ffn2_w2, v_ln4_g, v_ln4_b):
    args = locals()
    w = {n: args[n] for n in _WEIGHTS}
    mom = {n: args["m_" + n] for n in _WEIGHTS}
    var = {n: args["v_" + n] for n in _WEIGHTS}
    ix, iy, ic = lax.axis_index("x"), lax.axis_index("y"), lax.axis_index("c")

    c_idx = jnp.reshape(ic, (1,)).astype(jnp.int32)
    j_idx = jnp.reshape(2 * ix + iy, (1,)).astype(jnp.int32)
    hooks = _Hooks(w, c_idx)
    p = {n: (w[n][0] if w[n].ndim == 4 else w[n]) for n in _WEIGHTS if n not in _SHARDED and n != 'conv_w'}
    p.update(hooks.gathered(_GATHER_FIRST, _run_job(hooks.gather_job(_GATHER_FIRST), name="gather_ffn1")))
    loss_local, grad_x, g = _local_step(x[0], mem[0], loss_target[0], p, hooks)
    loss = lax.psum(loss_local[0, 0], ("x", "y", "c"))

    grads, delta, new_m, new_v = {}, {}, {}, {}
    for n in _SHARDED:
        grads[n], delta[n], new_m[n], new_v[n] = (
            o.reshape(w[n].shape) for o in _reduce_adam(hooks.sums[n][0], hooks.recv[n], j_idx, w[n][0], mom[n][0],
                                                        var[n][0], name=f"adam_{n}"))

    me = 4 * ix + 2 * iy + ic
    own_cols = lambda full: lax.dynamic_slice_in_dim(full, me * (LRU_W // N_DEV), LRU_W // N_DEV, axis=1)
    land = hooks.small_land
    conv_off = sum(math.prod(w[n].shape) for n in _SMALL)
    land_conv = land.reshape(N_DEV, -1)[:, conv_off:conv_off + CONV_W * LRU_W].reshape(N_DEV, CONV_W, LRU_W)
    land_own = jnp.pad(own_cols(land_conv.reshape(N_DEV * CONV_W, LRU_W)).reshape(N_DEV, -1),
                       ((0, 0), (0, CONV_W * LRU_W - CONV_W * LRU_W // N_DEV)))
    land = lax.dynamic_update_slice(land.reshape(N_DEV, -1), land_own, (0, conv_off)).reshape(land.shape)
    pk = lambda d: _pack_small({n: d[n] for n in _SMALL}, conv_w_full=jnp.pad(d["conv_w"].reshape(-1), (0, CONV_W * LRU_W - CONV_W * LRU_W // N_DEV)))
    g_s, d_s, m_s, v_s = _adam_small(pk(w), land, pk(mom), pk(var), name="adam_small")
    for src, dst in ((g_s, grads), (d_s, delta), (m_s, new_m), (v_s, new_v)):
        vals, conv = _unpack_small(src, w, with_conv=True)
        dst.update(vals)
        dst["conv_w"] = conv.reshape(-1)[:CONV_W * LRU_W // N_DEV].reshape(w["conv_w"].shape)

    return (loss, grad_x[None], *[grads[n] for n in _WEIGHTS], *[delta[n] for n in _WEIGHTS],
            *[new_m[n] for n in _WEIGHTS], *[new_v[n] for n in _WEIGHTS])
```

```python
import functools
import math

import jax
import jax.numpy as jnp
from jax import lax
from jax.experimental import pallas as pl
from jax.experimental.pallas import tpu as pltpu

F32, BF16 = jnp.float32, jnp.bfloat16

D_MODEL = 1024
LRU_W = 512
SB_W = 512
SB_PAIR = 128
SB_HEAD = 64
SB_ROWS = 128
SB_SKIP = -110.0
D_FF = 2688
FF_CHUNK = 672
N_DEV = 8
MEM_HEADS = 4
MEM_HD = 256
CONV_W = 4
ALPHA = 2.0 ** 0.25
LN_EPS = 1e-5
RMS_EPS = 1e-6
LRU_C = 8.0
ADAM_LR, ADAM_B1, ADAM_B2, ADAM_EPS, ADAM_WD, ADAM_STEP = 0.001, 0.9, 0.999, 1e-08, 0.01, 10

NN = (((1,), (0,)), ((), ()))
NT = (((1,), (1,)), ((), ()))
MESH = pl.DeviceIdType.MESH
VMEM_LIMIT_MB = 56


def _dot(a, b, dn=NN):
    return lax.dot_general(a, b, dn, preferred_element_type=F32)


def _cparams(n_axes, vmem_mb=None):
    kw = dict(dimension_semantics=("arbitrary",) * n_axes)
    if vmem_mb is not None:
        kw["vmem_limit_bytes"] = vmem_mb << 20
    return pltpu.CompilerParams(**kw)


def _row_tile(rows, want):
    if rows <= want:
        return rows
    t = want - want % 8
    while rows % t:
        t -= 8
    return t


def _matmul(a, b, *, name, out_dtype=F32, trans_a=False, trans_b=False, tm=512, tn=1024, tk=1024,
            scale=1.0, res=None, res_coeff=1.0, job=None, ln=None):
    a_chunked, b_chunked = a.ndim == 3, b.ndim == 3
    nc = a.shape[0] if a_chunked else (b.shape[0] if b_chunked else 1)
    a2, b2 = a.shape[-2:], b.shape[-2:]
    (kdim, m) = a2 if trans_a else a2[::-1]
    n = b2[0] if trans_b else b2[1]
    tm, tn, tk = min(tm, m), min(tn, n), min(tk, kdim)
    assert m % tm == 0 and n % tn == 0 and kdim % tk == 0, (name, m, n, kdim)
    nk = kdim // tk

    def a_idx(c, i, j, k):
        idx = (k, i) if trans_a else (i, k)
        return (c,) + idx if a_chunked else idx

    def b_idx(c, i, j, k):
        idx = (j, k) if trans_b else (k, j)
        return (c,) + idx if b_chunked else idx

    a_blk = (tk, tm) if trans_a else (tm, tk)
    b_blk = (tn, tk) if trans_b else (tk, tn)
    in_specs = [pl.BlockSpec(((None,) + a_blk) if a_chunked else a_blk, a_idx),
                pl.BlockSpec(((None,) + b_blk) if b_chunked else b_blk, b_idx)]
    args = [a, b]
    if res is not None:
        in_specs.append(pl.BlockSpec((None, tm, tn), lambda c, i, j, k: (c, i, j)))
        args.append(res.reshape((nc, m, n)))
    if ln is not None:
        assert nc == 1 and tn == n and res is None
        in_specs += [pl.BlockSpec((tm, tn), lambda c, i, j, k: (i, 0)), pl.BlockSpec((1, tn), lambda c, i, j, k: (0, 0)),
                     pl.BlockSpec((1, tn), lambda c, i, j, k: (0, 0))]
        args += [ln[0], ln[2], ln[3]]

    def body(*refs):
        if res is not None:
            a_ref, b_ref, r_ref, o_ref, acc_ref = refs
        elif ln is not None:
            a_ref, b_ref, x_ref, g_ref, bb_ref, o_ref, h_ref, hb_ref, acc_ref = refs
        else:
            a_ref, b_ref, o_ref, acc_ref = refs
        k = pl.program_id(3)

        @pl.when(k == 0)
        def _():
            acc_ref[...] = jnp.zeros_like(acc_ref)

        av = a_ref[...]
        if trans_a:
            av = av.astype(F32).T
        acc_ref[...] += _dot(av.astype(BF16), b_ref[...].astype(BF16), NT if trans_b else NN)

        @pl.when(k == nk - 1)
        def _():
            out = acc_ref[...]
            if scale != 1.0:
                out = out * scale
            if res is not None:
                out = out + res_coeff * r_ref[...]
            o_ref[...] = out.astype(out_dtype)
            if ln is not None:
                h = _ln_math(ALPHA * x_ref[...] + ln[1] * out, g_ref[...], bb_ref[...])
                h_ref[...] = h
                hb_ref[...] = h.astype(BF16)

    grid = (nc, m // tm, n // tn, nk)

    def step():
        idx = pl.program_id(0)
        for ax in range(1, 4):
            idx = idx * grid[ax] + pl.program_id(ax)
        return idx

    out_specs = [pl.BlockSpec((None, tm, tn), lambda c, i, j, k: (c, i, j))]
    out_shape = [jax.ShapeDtypeStruct((nc, m, n), out_dtype)]
    if ln is not None:
        out_specs += [pl.BlockSpec((tm, tn), lambda c, i, j, k: (i, 0))] * 2
        out_shape += [jax.ShapeDtypeStruct((m, n), F32), jax.ShapeDtypeStruct((m, n), BF16)]
    outs, job_out = _pallas_with_job(
        body, job, step, math.prod(grid), grid=grid, in_specs=in_specs, out_specs=out_specs, out_shape=out_shape,
        scratch_shapes=[pltpu.VMEM((tm, tn), F32)],
        compiler_params=_cparams(4, VMEM_LIMIT_MB), name=name, args=args)
    out = outs[0] if (a_chunked or b_chunked) else outs[0][0]
    if ln is not None:
        return out, outs[1], outs[2]
    return out if job is None else (out, job_out)


def _ln_math(z, g, b):
    mu = jnp.mean(z, axis=-1, keepdims=True)
    zc = z - mu
    var = jnp.mean(zc * zc, axis=-1, keepdims=True)
    return zc * lax.rsqrt(var + LN_EPS) * g + b


def _row_spec(tm, d):
    return pl.BlockSpec((tm, d), lambda i: (i, 0))


def _par_spec(d, rows=1):
    return pl.BlockSpec((rows, d), lambda i: (0, 0))


def _ln_bwd(x, f, fscale, g, b, dy, *, name, target=None):
    s, d = x.shape
    tm = min(512, s)
    with_loss = target is not None

    def body(*refs):
        if with_loss:
            x_ref, f_ref, g_ref, b_ref, t_ref, dz_ref, dzb_ref, dg_ref, db_ref, loss_ref = refs
        else:
            x_ref, f_ref, g_ref, b_ref, dy_ref, dz_ref, dzb_ref, dg_ref, db_ref = refs
        i = pl.program_id(0)

        @pl.when(i == 0)
        def _():
            dg_ref[...] = jnp.zeros_like(dg_ref)
            db_ref[...] = jnp.zeros_like(db_ref)
            if with_loss:
                loss_ref[...] = jnp.zeros_like(loss_ref)

        z = ALPHA * x_ref[...] + fscale * f_ref[...]
        y, vjp = jax.vjp(_ln_math, z, g_ref[...], b_ref[...])
        if with_loss:
            err = y - t_ref[...]
            sq = jnp.sum(err * err, axis=1, keepdims=True)
            loss_ref[...] += jnp.sum(sq, axis=0, keepdims=True) * (0.5 / d)
            cot = err * (1.0 / d)
        else:
            cot = dy_ref[...]
        dz, dg, db = vjp(cot)
        dz_ref[...] = dz
        dzb_ref[...] = dz.astype(BF16)
        dg_ref[...] += dg
        db_ref[...] += db

    in_specs = [_row_spec(tm, d), _row_spec(tm, d), _par_spec(d), _par_spec(d), _row_spec(tm, d)]
    out_specs = [_row_spec(tm, d), _row_spec(tm, d), _par_spec(d), _par_spec(d)]
    out_shape = [jax.ShapeDtypeStruct((s, d), F32), jax.ShapeDtypeStruct((s, d), BF16),
                 jax.ShapeDtypeStruct((1, d), F32), jax.ShapeDtypeStruct((1, d), F32)]
    if with_loss:
        out_specs.append(pl.BlockSpec((1, 1), lambda i: (0, 0)))
        out_shape.append(jax.ShapeDtypeStruct((1, 1), F32))
    return pl.pallas_call(
        body, grid=(s // tm,), in_specs=in_specs, out_specs=out_specs, out_shape=out_shape,
        compiler_params=_cparams(1), name=name)(x, f, g, b, target if with_loss else dy)


def _load_weights_once(pairs, sem):
    @pl.when(pl.program_id(0) == 0)
    def _():
        copies = [pltpu.make_async_copy(src, dst, sem.at[n]) for n, (src, dst) in enumerate(pairs)]
        for c in copies:
            c.start()
        for c in copies:
            c.wait()


def _ffn_fwd(h, w13c, w2, *, name, job=None, ln=None):
    s, d = h.shape
    tm = min(256, s)
    half = N_DEV // 2

    def body(*refs):
        if ln is not None:
            h_ref, w13_hbm, w2_hbm, g_ref, b_ref, gu_ref, f_ref, hb_ref, n_ref, nb_ref, w13_v, w2_v, sem = refs
        else:
            h_ref, w13_hbm, w2_hbm, gu_ref, f_ref, hb_ref, w13_v, w2_v, sem = refs
        _load_weights_once([(w13_hbm, w13_v), (w2_hbm, w2_v)], sem)
        hb = h_ref[...].astype(BF16)
        hb_ref[...] = hb
        acc = jnp.zeros((tm, d), F32)
        for k in range(half):
            g = _dot(hb, w13_v[k])
            u = _dot(hb, w13_v[k + half])
            gu_ref[k] = g.astype(BF16)
            gu_ref[k + half] = u.astype(BF16)
            a = g * jax.nn.sigmoid(g) * u
            acc = acc + _dot(a.astype(BF16), w2_v[pl.ds(k * FF_CHUNK, FF_CHUNK), :])
        f_ref[...] = acc
        if ln is not None:
            hn = _ln_math(ALPHA * h_ref[...] + 0.5 * acc, g_ref[...], b_ref[...])
            n_ref[...] = hn
            nb_ref[...] = hn.astype(BF16)

    any_spec = pl.BlockSpec(memory_space=pl.ANY)
    n_ln = 0 if ln is None else 2
    return _pallas_with_job(
        body, job, lambda: pl.program_id(0), s // tm,
        grid=(s // tm,), in_specs=[_row_spec(tm, d), any_spec, any_spec] + [_par_spec(d)] * n_ln,
        out_specs=[pl.BlockSpec((N_DEV, tm, FF_CHUNK), lambda i: (0, i, 0)), _row_spec(tm, d), _row_spec(tm, d)]
                  + [_row_spec(tm, d)] * n_ln,
        out_shape=[jax.ShapeDtypeStruct((N_DEV, s, FF_CHUNK), BF16), jax.ShapeDtypeStruct((s, d), F32),
                   jax.ShapeDtypeStruct((s, d), BF16)]
                  + [jax.ShapeDtypeStruct((s, d), F32), jax.ShapeDtypeStruct((s, d), BF16)][:n_ln],
        scratch_shapes=[pltpu.VMEM(w13c.shape, BF16), pltpu.VMEM(w2.shape, BF16), pltpu.SemaphoreType.DMA((2,))],
        compiler_params=_cparams(1, VMEM_LIMIT_MB), name=name, args=(h, w13c, w2) + (() if ln is None else tuple(ln)))


def _ffn_bwd(dz, gu, w13c, w2, *, name, job=None):
    s, d = dz.shape
    tm = min(256, s)
    half = N_DEV // 2

    def body(dz_ref, gu_ref, w13_hbm, w2_hbm, dh_ref, dgu_ref, act_ref, w13_v, w2_v, sem):
        _load_weights_once([(w13_hbm, w13_v), (w2_hbm, w2_v)], sem)
        dzv = dz_ref[...]
        dfb = (0.5 * dzv).astype(BF16)
        acc = ALPHA * dzv
        for k in range(half):
            g = gu_ref[k].astype(F32)
            u = gu_ref[k + half].astype(F32)
            da = _dot(dfb, w2_v[pl.ds(k * FF_CHUNK, FF_CHUNK), :], NT)
            sg = jax.nn.sigmoid(g)
            silu = g * sg
            dg = (da * u * (sg * (1.0 + g * (1.0 - sg)))).astype(BF16)
            du = (da * silu).astype(BF16)
            act_ref[k] = (silu * u).astype(BF16)
            dgu_ref[k] = dg
            dgu_ref[k + half] = du
            acc = acc + _dot(dg, w13_v[k], NT) + _dot(du, w13_v[k + half], NT)
        dh_ref[...] = acc

    any_spec = pl.BlockSpec(memory_space=pl.ANY)
    return _pallas_with_job(
        body, job, lambda: pl.program_id(0), s // tm, grid=(s // tm,),
        in_specs=[_row_spec(tm, d), pl.BlockSpec((N_DEV, tm, FF_CHUNK), lambda i: (0, i, 0)), any_spec, any_spec],
        out_specs=[_row_spec(tm, d), pl.BlockSpec((N_DEV, tm, FF_CHUNK), lambda i: (0, i, 0)),
                   pl.BlockSpec((half, tm, FF_CHUNK), lambda i: (0, i, 0))],
        out_shape=[jax.ShapeDtypeStruct((s, d), F32), jax.ShapeDtypeStruct((N_DEV, s, FF_CHUNK), BF16),
                   jax.ShapeDtypeStruct((half, s, FF_CHUNK), BF16)],
        scratch_shapes=[pltpu.VMEM(w13c.shape, BF16), pltpu.VMEM(w2.shape, BF16), pltpu.SemaphoreType.DMA((2,))],
        compiler_params=_cparams(1, VMEM_LIMIT_MB), name=name, args=(dz, gu, w13c, w2))


def _shift_down(x, prev8, d):
    t, c = x.shape
    row = lax.broadcasted_iota(jnp.int32, (t, c), 0)
    xr = pltpu.roll(x, d, 0)
    pr = pltpu.roll(prev8, d, 0)
    if t > 8:
        pr = jnp.concatenate([pr, jnp.zeros((t - 8, c), x.dtype)], axis=0)
    return jnp.where(row < d, pr, xr)


def _shift_up(x, next8, d):
    t, c = x.shape
    row = lax.broadcasted_iota(jnp.int32, (t, c), 0)
    xr = pltpu.roll(x, t - d, 0)
    nr = pltpu.roll(next8, 8 - d, 0)
    if t > 8:
        nr = jnp.concatenate([jnp.zeros((t - 8, c), x.dtype), nr], axis=0)
    return jnp.where(row >= t - d, nr, xr)


def _scan_fwd(a, u, carry):
    t = a.shape[0]
    row = lax.broadcasted_iota(jnp.int32, a.shape, 0) % 8
    for d in (1, 2, 4):
        a_s, u_s = pltpu.roll(a, d, 0), pltpu.roll(u, d, 0)
        m = row >= d
        u = jnp.where(m, u + a * u_s, u)
        a = jnp.where(m, a * a_s, a)
    out = []
    for g in range(t // 8):
        hg = u[8 * g:8 * g + 8, :] + a[8 * g:8 * g + 8, :] * carry
        out.append(hg)
        carry = hg[7:8, :]
    return jnp.concatenate(out, axis=0)


def _scan_bwd(b, x, carry):
    t = b.shape[0]
    row = lax.broadcasted_iota(jnp.int32, b.shape, 0) % 8
    for d in (1, 2, 4):
        b_s, x_s = pltpu.roll(b, t - d, 0), pltpu.roll(x, t - d, 0)
        m = row < 8 - d
        x = jnp.where(m, x + b * x_s, x)
        b = jnp.where(m, b * b_s, b)
    out = [None] * (t // 8)
    for g in reversed(range(t // 8)):
        lg = x[8 * g:8 * g + 8, :] + b[8 * g:8 * g + 8, :] * carry
        out[g] = lg
        carry = lg[0:1, :]
    return jnp.concatenate(out, axis=0)


def _lru_elem(xc, pr, pi, b_r, b_i, lam):
    r = jax.nn.sigmoid(pr + b_r)
    ig = jax.nn.sigmoid(pi + b_i)
    softplus_neg_lam = jnp.maximum(-lam, 0.0) + jnp.log1p(jnp.exp(-jnp.abs(lam)))
    log_a = (-LRU_C) * r * softplus_neg_lam
    a = jnp.exp(log_a)
    one_minus_a2 = -jnp.tanh(log_a) * (jnp.exp(2.0 * log_a) + 1.0)
    u = jnp.sqrt(one_minus_a2) * (ig * xc)
    return a, u


def _gelu_tanh(x):
    return 0.5 * x * (1.0 + jnp.tanh(math.sqrt(2.0 / math.pi) * (x + 0.044715 * (x * x * x))))


def _conv_fwd(xb, prev8, w, b):
    out = b + w[3:4, :] * xb
    for d in range(1, CONV_W):
        out = out + w[3 - d:4 - d, :] * _shift_down(xb, prev8, d)
    return out


def _lru_fwd(xbg, conv_w, conv_b, w_rg, b_rg, w_ig, b_ig, lam, *, name):
    s = xbg.shape[0]
    c = LRU_W
    t = min(512, s)
    t8 = t // 8

    def body(xb_ref, gate_ref, prev_ref, cw_ref, cb_ref, wr_ref, br_ref, wi_ref, bi_ref, lam_ref,
             xc_ref, h_ref, y_ref, carry_ref):
        i = pl.program_id(0)

        @pl.when(i == 0)
        def _():
            carry_ref[...] = jnp.zeros_like(carry_ref)

        prev8 = jnp.where(i > 0, prev_ref[...], 0.0)
        xc = _conv_fwd(xb_ref[...], prev8, cw_ref[...], cb_ref[...])
        xcb = xc.astype(BF16)
        a, u = _lru_elem(xc, _dot(xcb, wr_ref[...]), _dot(xcb, wi_ref[...]), br_ref[...], bi_ref[...], lam_ref[...])
        h = _scan_fwd(a, u, carry_ref[0:1, :])
        carry_ref[...] = jnp.broadcast_to(h[t - 1:t, :], carry_ref.shape)
        xc_ref[...] = xc
        h_ref[...] = h
        y_ref[...] = h * _gelu_tanh(gate_ref[...])

    tile = lambda col: pl.BlockSpec((t, c), lambda i: (i, col))
    par = lambda rows: pl.BlockSpec((rows, c), lambda i: (0, 0))
    out_spec = pl.BlockSpec((t, c), lambda i: (i, 0))
    return pl.pallas_call(
        body, grid=(s // t,),
        in_specs=[tile(0), tile(1), pl.BlockSpec((8, c), lambda i: (jnp.maximum(i * t8 - 1, 0), 0)),
                  par(CONV_W), par(1), pl.BlockSpec((c, c), lambda i: (0, 0)), par(1),
                  pl.BlockSpec((c, c), lambda i: (0, 0)), par(1), par(1)],
        out_specs=[out_spec, out_spec, out_spec],
        out_shape=[jax.ShapeDtypeStruct((s, c), F32)] * 3,
        scratch_shapes=[pltpu.VMEM((8, c), F32)],
        compiler_params=_cparams(1, VMEM_LIMIT_MB), name=name)(
            xbg, xbg, xbg, conv_w, conv_b, w_rg, b_rg, w_ig, b_ig, lam)


def _lru_bwd(xc, xbg, hseq, dy, w_rg, b_rg, w_ig, b_ig, lam, *, name):
    s, c = xc.shape
    t = min(512, s)
    t8 = t // 8
    nt = s // t

    def body(xc_ref, gate_ref, h_ref, hprev_ref, dy_ref, wr_ref, br_ref, wi_ref, bi_ref, lam_ref,
             dxc_ref, dgate_ref, dwr_ref, dwi_ref, dbr_ref, dbi_ref, dlam_ref, a8_ref, l8_ref):
        i = pl.program_id(0)

        @pl.when(i == 0)
        def _():
            for r in (dwr_ref, dwi_ref, dbr_ref, dbi_ref, dlam_ref, a8_ref, l8_ref):
                r[...] = jnp.zeros_like(r)

        xc_v = xc_ref[...]
        xcb = xc_v.astype(BF16)
        (a, _), vjp = jax.vjp(_lru_elem, xc_v, _dot(xcb, wr_ref[...]), _dot(xcb, wi_ref[...]),
                              br_ref[...], bi_ref[...], lam_ref[...])
        gl, vjp_gelu = jax.vjp(_gelu_tanh, gate_ref[...])
        dyv, hs = dy_ref[...], h_ref[...]
        (dgate,) = vjp_gelu(dyv * hs)
        b_next = _shift_up(a, a8_ref[...], 1)
        lt = _scan_bwd(b_next, dyv * gl, l8_ref[0:1, :])
        a8_ref[...] = a[0:8, :]
        l8_ref[...] = lt[0:8, :]
        hprev8 = jnp.where(i < nt - 1, hprev_ref[...], 0.0)
        da = lt * _shift_down(hs, hprev8, 1)
        dxc_e, dpr, dpi, dbr, dbi, dlam = vjp((da, lt))
        dprb, dpib = dpr.astype(BF16), dpi.astype(BF16)
        dxc_ref[...] = dxc_e + _dot(dprb, wr_ref[...], NT) + _dot(dpib, wi_ref[...], NT)
        dgate_ref[...] = dgate.astype(BF16)
        xct = xc_v.T.astype(BF16)
        dwr_ref[...] += _dot(xct, dprb)
        dwi_ref[...] += _dot(xct, dpib)
        dbr_ref[...] += dbr
        dbi_ref[...] += dbi
        dlam_ref[...] += dlam

    rev = lambda col: pl.BlockSpec((t, c), lambda i: (nt - 1 - i, col))
    par = pl.BlockSpec((1, c), lambda i: (0, 0))
    sq = pl.BlockSpec((c, c), lambda i: (0, 0))
    return pl.pallas_call(
        body, grid=(nt,),
        in_specs=[rev(0), rev(1), rev(0), pl.BlockSpec((8, c), lambda i: (jnp.maximum((nt - 1 - i) * t8 - 1, 0), 0)),
                  rev(0), sq, par, sq, par, par],
        out_specs=[rev(0), rev(0), sq, sq, par, par, par],
        out_shape=[jax.ShapeDtypeStruct((s, c), F32), jax.ShapeDtypeStruct((s, c), BF16)]
                  + [jax.ShapeDtypeStruct((c, c), F32)] * 2 + [jax.ShapeDtypeStruct((1, c), F32)] * 3,
        scratch_shapes=[pltpu.VMEM((8, c), F32), pltpu.VMEM((8, c), F32)],
        compiler_params=_cparams(1, VMEM_LIMIT_MB), name=name)(
            xc, xbg, hseq, hseq, dy, w_rg, b_rg, w_ig, b_ig, lam)


def _conv_bwd(dxc, xbg, conv_w, *, name):
    s, c = dxc.shape
    t = min(512, s)
    t8 = t // 8
    nt = s // t

    def body(dxc_ref, next_ref, xb_ref, prev_ref, w_ref, dxb_ref, dw_ref, db_ref):
        i = pl.program_id(0)

        @pl.when(i == 0)
        def _():
            dw_ref[...] = jnp.zeros_like(dw_ref)
            db_ref[...] = jnp.zeros_like(db_ref)

        g = dxc_ref[...]
        xb = xb_ref[...]
        w = w_ref[...]
        next8 = jnp.where(i < nt - 1, next_ref[...], 0.0)
        prev8 = jnp.where(i > 0, prev_ref[...], 0.0)
        dxb = w[3:4, :] * g
        dw_ref[3:4, :] += jnp.sum(g * xb, axis=0, keepdims=True)
        for d in range(1, CONV_W):
            dxb = dxb + w[3 - d:4 - d, :] * _shift_up(g, next8, d)
            dw_ref[3 - d:4 - d, :] += jnp.sum(g * _shift_down(xb, prev8, d), axis=0, keepdims=True)
        dxb_ref[...] = dxb.astype(BF16)
        db_ref[...] += jnp.sum(g, axis=0, keepdims=True)

    tile = pl.BlockSpec((t, c), lambda i: (i, 0))
    return pl.pallas_call(
        body, grid=(nt,),
        in_specs=[tile, pl.BlockSpec((8, c), lambda i: (jnp.minimum((i + 1) * t8, s // 8 - 1), 0)),
                  tile, pl.BlockSpec((8, c), lambda i: (jnp.maximum(i * t8 - 1, 0), 0)),
                  pl.BlockSpec((CONV_W, c), lambda i: (0, 0))],
        out_specs=[tile, pl.BlockSpec((CONV_W, c), lambda i: (0, 0)), pl.BlockSpec((1, c), lambda i: (0, 0))],
        out_shape=[jax.ShapeDtypeStruct((s, c), BF16), jax.ShapeDtypeStruct((CONV_W, c), F32),
                   jax.ShapeDtypeStruct((1, c), F32)],
        compiler_params=_cparams(1), name=name)(dxc, dxc, xbg, xbg, conv_w)


def _hi_lo_dot(x, tri):
    hi = x.astype(BF16)
    lo = (x - hi.astype(F32)).astype(BF16)
    return _dot(hi, tri) + _dot(lo, tri)


def _tri(tk, inclusive):
    r = lax.broadcasted_iota(jnp.int32, (tk, tk), 0)
    c = lax.broadcasted_iota(jnp.int32, (tk, tk), 1)
    return ((r >= c) if inclusive else (r > c)).astype(BF16)


def _split_heads(x):
    first = lax.broadcasted_iota(jnp.int32, x.shape, 1) < SB_HEAD
    zero = jnp.zeros_like(x)
    return first, (jnp.where(first, x, zero), jnp.where(first, zero, x))


def _sb_softplus_terms(z):
    ls = jnp.minimum(z, 0.0) - jnp.log(1.0 + jnp.exp(-jnp.abs(z)))
    return ls, ls - z


def _sb_alive(runs):
    m = runs[0]
    for r in runs[1:]:
        m = jnp.maximum(m, r)
    return jnp.max(m) > SB_SKIP


def _sb_fwd(q, kv, *, name, job=None):
    s = q.shape[0]
    tb = min(256, s)
    rc = min(SB_ROWS, tb)
    npair = SB_W // SB_PAIR
    chains = [(h, c) for h in range(2) for c in range(tb // rc)]

    def body(q_ref, k_ref, v_ref, o_ref):
        i = pl.program_id(1)
        first, qs = _split_heads(q_ref[...])
        tri = _tri(tb, False)
        causal = lax.broadcasted_iota(jnp.int32, (tb, tb), 1) < lax.broadcasted_iota(jnp.int32, (tb, tb), 0)
        causal_c = [causal[c * rc:(c + 1) * rc, :] for c in range(tb // rc)]

        def group(j_lo, width, carry, diag_last):
            k0 = pl.multiple_of(j_lo * tb, tb)
            kb = k_ref[pl.ds(k0, width * tb), :]
            vb = v_ref[pl.ds(k0, width * tb), :]
            zw = [_dot(qs[h][c * rc:(c + 1) * rc, :], kb, NT) for h, c in chains]
            cols = [slice(t * tb, (t + 1) * tb) for t in range(width)]
            terms = [[_sb_softplus_terms(z[:, cl]) for cl in cols] for z in zw]
            lss = [[ls for ls, _ in tc] for tc in terms]
            ns = [[n for _, n in tc] for tc in terms]
            if diag_last:
                ns = [nc[:-1] + [jnp.where(causal_c[c], nc[-1], 0.0)] for nc, (h, c) in zip(ns, chains)]
            sufs = [[_hi_lo_dot(n, tri) for n in nc] for nc in ns]
            out = []
            for (h, c), lc, nc, sc, (run, acc) in zip(chains, lss, ns, sufs, carry):
                ws = [None] * width
                for t in reversed(range(width)):
                    w = jnp.exp(lc[t] + (sc[t] + run))
                    if diag_last and t == width - 1:
                        w = jnp.where(causal_c[c], w, 0.0)
                    ws[t] = w.astype(BF16)
                    run = run + jnp.sum(nc[t], axis=1, keepdims=True)
                out.append((run, acc + _dot(ws[0] if width == 1 else jnp.concatenate(ws, axis=1), vb)))
            return tuple(out)

        zero = (jnp.zeros((rc, 1), F32), jnp.zeros((rc, SB_PAIR), F32))
        carry = lax.cond(i == 0, lambda c: group(0, 1, c, True), lambda c: group(i - 1, 2, c, True),
                         (zero,) * len(chains))

        def step(st):
            j, _, c = st
            c = group(j, 1, c, False)
            return j - 1, _sb_alive([run for run, _ in c]), c

        _, _, carry = lax.while_loop(lambda st: jnp.logical_and(st[0] >= 0, st[1]), step,
                                     (i - 2, _sb_alive([run for run, _ in carry]), carry))
        accs = [jnp.concatenate([carry[n][1] for n, (h, c) in enumerate(chains) if h == hh], axis=0) for hh in range(2)]
        o_ref[...] = jnp.where(first, accs[0], accs[1])

    (o,), job_out = _pallas_with_job(
        body, job, lambda: pl.program_id(0) * (s // tb) + pl.program_id(1), npair * (s // tb),
        grid=(npair, s // tb),
        in_specs=[pl.BlockSpec((tb, SB_PAIR), lambda p, i: (i, p)),
                  pl.BlockSpec((s, SB_PAIR), lambda p, i: (0, p)),
                  pl.BlockSpec((s, SB_PAIR), lambda p, i: (0, npair + p))],
        out_specs=[pl.BlockSpec((tb, SB_PAIR), lambda p, i: (i, p))],
        out_shape=[jax.ShapeDtypeStruct((s, SB_W), F32)],
        compiler_params=_cparams(2, VMEM_LIMIT_MB), name=name, args=(q, kv, kv))
    return o, job_out


def _sb_bwd(q, kv, o, do, *, name, job=None):
    s = q.shape[0]
    tb = min(256, s)
    rc = min(SB_ROWS, tb)
    npair = SB_W // SB_PAIR
    chains = [(h, c) for h in range(2) for c in range(tb // rc)]

    def body(q_ref, k_ref, v_ref, o_ref, do_ref, dq_ref, dkb_ref, dvb_ref, dk_ref, dv_ref):
        i = pl.program_id(1)

        @pl.when(i == 0)
        def _():
            dk_ref[...] = jnp.zeros_like(dk_ref)
            dv_ref[...] = jnp.zeros_like(dv_ref)

        first, qs = _split_heads(q_ref[...])
        dob = do_ref[...].astype(BF16)
        _, dos = _split_heads(dob)
        prod = dob.astype(F32) * o_ref[...]
        deltas = (jnp.sum(jnp.where(first, prod, 0.0), axis=1, keepdims=True),
                  jnp.sum(jnp.where(first, 0.0, prod), axis=1, keepdims=True))
        tri_x, tri_i = _tri(tb, False), _tri(tb, True)
        causal = lax.broadcasted_iota(jnp.int32, (tb, tb), 1) < lax.broadcasted_iota(jnp.int32, (tb, tb), 0)
        causal_c = [causal[c * rc:(c + 1) * rc, :] for c in range(tb // rc)]
        rows = [slice(c * rc, (c + 1) * rc) for c in range(tb // rc)]

        def group(j_lo, width, carry, diag_last):
            k0 = pl.multiple_of(j_lo * tb, tb)
            kb = k_ref[pl.ds(k0, width * tb), :]
            vb = v_ref[pl.ds(k0, width * tb), :]
            cols = [slice(t * tb, (t + 1) * tb) for t in range(width)]
            zw = [_dot(qs[h][rows[c], :], kb, NT) for h, c in chains]
            dww = [_dot(dos[h][rows[c], :], vb, NT) for h, c in chains]
            terms = [[_sb_softplus_terms(z[:, cl]) for cl in cols] for z in zw]
            lss = [[ls for ls, _ in tc] for tc in terms]
            ns = [[n for _, n in tc] for tc in terms]
            if diag_last:
                ns = [nc[:-1] + [jnp.where(causal_c[c], nc[-1], 0.0)] for nc, (h, c) in zip(ns, chains)]
            sufs = [[_hi_lo_dot(n, tri_x) for n in nc] for nc in ns]
            wbs, gs, runs_n = [], [], []
            for (h, c), lc, nc, sc, dw, (run_n, _, _) in zip(chains, lss, ns, sufs, dww, carry):
                wb, g = [None] * width, [None] * width
                for t in reversed(range(width)):
                    w = jnp.exp(lc[t] + (sc[t] + run_n))
                    if diag_last and t == width - 1:
                        w = jnp.where(causal_c[c], w, 0.0)
                    wb[t] = w.astype(BF16)
                    g[t] = wb[t].astype(F32) * dw[:, cols[t]]
                    run_n = run_n + jnp.sum(nc[t], axis=1, keepdims=True)
                wbs.append(wb)
                gs.append(g)
                runs_n.append(run_n)
            gsufs = [[_hi_lo_dot(g, tri_i) for g in gc] for gc in gs]
            out = []
            dk_t = [jnp.zeros((tb, SB_PAIR), F32) for _ in range(width)]
            dv_t = [jnp.zeros((tb, SB_PAIR), F32) for _ in range(width)]
            for (h, c), lc, gc, gsc, wb, run_n, (_, run_g, dq_acc) in zip(chains, lss, gs, gsufs, wbs, runs_n, carry):
                qh, doh, delta = qs[h][rows[c], :], dos[h][rows[c], :], deltas[h][rows[c], :]
                dzb = [None] * width
                for t in reversed(range(width)):
                    pre = delta - (gsc[t] + run_g)
                    dz = gc[t] - jnp.exp(lc[t]) * (gc[t] + pre)
                    if diag_last and t == width - 1:
                        dz = jnp.where(causal_c[c], dz, 0.0)
                    run_g = run_g + jnp.sum(gc[t], axis=1, keepdims=True)
                    dzb[t] = dz.astype(BF16)
                    dk_t[t] = dk_t[t] + _dot(dz.T.astype(BF16), qh)
                    dv_t[t] = dv_t[t] + _dot(wb[t].astype(F32).T.astype(BF16), doh)
                out.append((run_n, run_g, dq_acc + _dot(dzb[0] if width == 1 else jnp.concatenate(dzb, axis=1), kb)))
            for t in range(width):
                dk_ref[pl.ds(pl.multiple_of((j_lo + t) * tb, tb), tb), :] += dk_t[t]
                dv_ref[pl.ds(pl.multiple_of((j_lo + t) * tb, tb), tb), :] += dv_t[t]
            return tuple(out)

        zero = (jnp.zeros((rc, 1), F32), jnp.zeros((rc, 1), F32), jnp.zeros((rc, SB_PAIR), F32))
        carry = lax.cond(i == 0, lambda c: group(0, 1, c, True), lambda c: group(i - 1, 2, c, True),
                         (zero,) * len(chains))

        def step(st):
            j, _, c = st
            c = group(j, 1, c, False)
            return j - 1, _sb_alive([r[0] for r in c]), c

        _, _, carry = lax.while_loop(lambda st: jnp.logical_and(st[0] >= 0, st[1]), step,
                                     (i - 2, _sb_alive([r[0] for r in carry]), carry))
        dqs = [jnp.concatenate([carry[n][2] for n, (h, c) in enumerate(chains) if h == hh], axis=0) for hh in range(2)]
        dq_ref[...] = (jnp.where(first, dqs[0], dqs[1]) * (1.0 / math.sqrt(SB_HEAD))).astype(BF16)

        @pl.when(i == s // tb - 1)
        def _():
            dkb_ref[...] = dk_ref[...].astype(BF16)
            dvb_ref[...] = dv_ref[...].astype(BF16)

    qtile = pl.BlockSpec((tb, SB_PAIR), lambda p, i: (i, p))
    col = pl.BlockSpec((s, SB_PAIR), lambda p, i: (0, p))
    return _pallas_with_job(
        body, job, lambda: pl.program_id(0) * (s // tb) + pl.program_id(1), npair * (s // tb),
        grid=(npair, s // tb),
        in_specs=[qtile, col, pl.BlockSpec((s, SB_PAIR), lambda p, i: (0, npair + p)), qtile, qtile],
        out_specs=[qtile, col, col],
        out_shape=[jax.ShapeDtypeStruct((s, SB_W), BF16)] * 3,
        scratch_shapes=[pltpu.VMEM((s, SB_PAIR), F32)] * 2,
        compiler_params=_cparams(2, VMEM_LIMIT_MB), name=name, args=(q, kv, kv, o, do))


def _mixnorm_math(yl, ys, gl, gs):
    def rms(x, g):
        return x * lax.rsqrt(jnp.mean(x * x, axis=-1, keepdims=True) + RMS_EPS) * g
    return rms(yl, gl), rms(ys, gs)


def _mixnorm_fwd(yl, ys, gl, gs, *, name):
    s, c = yl.shape
    tm = min(512, s)

    def body(yl_ref, ys_ref, gl_ref, gs_ref, y_ref):
        a, b = _mixnorm_math(yl_ref[...], ys_ref[...], gl_ref[...], gs_ref[...])
        y_ref[:, 0:c] = a.astype(BF16)
        y_ref[:, c:2 * c] = b.astype(BF16)

    return pl.pallas_call(
        body, grid=(s // tm,), in_specs=[_row_spec(tm, c), _row_spec(tm, c), _par_spec(c), _par_spec(c)],
        out_specs=_row_spec(tm, 2 * c), out_shape=jax.ShapeDtypeStruct((s, 2 * c), BF16),
        compiler_params=_cparams(1), name=name)(yl, ys, gl, gs)


def _mixnorm_bwd(yl, ys, gl, gs, dy, *, name, job=None):
    s, c = yl.shape
    tm = min(512, s)

    def body(yl_ref, ys_ref, gl_ref, gs_ref, dy_ref, dyl_ref, dys_ref, dgl_ref, dgs_ref):
        @pl.when(pl.program_id(0) == 0)
        def _():
            dgl_ref[...] = jnp.zeros_like(dgl_ref)
            dgs_ref[...] = jnp.zeros_like(dgs_ref)

        _, vjp = jax.vjp(_mixnorm_math, yl_ref[...], ys_ref[...], gl_ref[...], gs_ref[...])
        dyl, dys, dgl, dgs = vjp((dy_ref[:, 0:c], dy_ref[:, c:2 * c]))
        dyl_ref[...] = dyl
        dys_ref[...] = dys
        dgl_ref[...] += dgl
        dgs_ref[...] += dgs

    return _pallas_with_job(
        body, job, lambda: pl.program_id(0), s // tm, grid=(s // tm,),
        in_specs=[_row_spec(tm, c), _row_spec(tm, c), _par_spec(c), _par_spec(c), _row_spec(tm, 2 * c)],
        out_specs=[_row_spec(tm, c), _row_spec(tm, c), _par_spec(c), _par_spec(c)],
        out_shape=[jax.ShapeDtypeStruct((s, c), F32)] * 2 + [jax.ShapeDtypeStruct((1, c), F32)] * 2,
        compiler_params=_cparams(1), name=name, args=(yl, ys, gl, gs, dy))


def _cross_probs(qh, kh):
    sc = _dot(qh, kh, NT) * (1.0 / math.sqrt(MEM_HD))
    e = jnp.exp(sc - jnp.max(sc, axis=-1, keepdims=True))
    return e / jnp.sum(e, axis=-1, keepdims=True)


def _cross_fwd(q, kv, *, name):
    s, d = q.shape
    mlen = kv.shape[1]
    tm = min(512, s)

    def body(q_ref, kv_ref, o_ref):
        for h in range(MEM_HEADS):
            cols = slice(h * MEM_HD, (h + 1) * MEM_HD)
            p = _cross_probs(q_ref[:, cols].astype(BF16), kv_ref[h].astype(BF16))
            o_ref[:, cols] = _dot(p.astype(BF16), kv_ref[MEM_HEADS + h].astype(BF16)).astype(BF16)

    return pl.pallas_call(
        body, grid=(s // tm,),
        in_specs=[_row_spec(tm, d), pl.BlockSpec((2 * MEM_HEADS, mlen, MEM_HD), lambda i: (0, 0, 0))],
        out_specs=_row_spec(tm, d), out_shape=jax.ShapeDtypeStruct((s, d), BF16),
        compiler_params=_cparams(1, VMEM_LIMIT_MB), name=name)(q, kv)


def _cross_bwd(q, kv, do, *, name):
    s, d = q.shape
    mlen = kv.shape[1]
    tm = min(512, s)

    def body(q_ref, kv_ref, do_ref, dq_ref, dkv_ref):
        @pl.when(pl.program_id(0) == 0)
        def _():
            dkv_ref[...] = jnp.zeros_like(dkv_ref)

        for h in range(MEM_HEADS):
            cols = slice(h * MEM_HD, (h + 1) * MEM_HD)
            qh = q_ref[:, cols].astype(BF16)
            kh = kv_ref[h].astype(BF16)
            doh = do_ref[:, cols].astype(BF16)
            p = _cross_probs(qh, kh)
            dp = _dot(doh, kv_ref[MEM_HEADS + h].astype(BF16), NT)
            ds = p * (dp - jnp.sum(dp * p, axis=-1, keepdims=True)) * (1.0 / math.sqrt(MEM_HD))
            dq_ref[:, cols] = _dot(ds.astype(BF16), kh).astype(BF16)
            dkv_ref[h] += _dot(ds.T.astype(BF16), qh)
            dkv_ref[MEM_HEADS + h] += _dot(p.T.astype(BF16), doh)

    kv_spec = pl.BlockSpec((2 * MEM_HEADS, mlen, MEM_HD), lambda i: (0, 0, 0))
    return pl.pallas_call(
        body, grid=(s // tm,), in_specs=[_row_spec(tm, d), kv_spec, _row_spec(tm, d)],
        out_specs=[_row_spec(tm, d), kv_spec],
        out_shape=[jax.ShapeDtypeStruct((s, d), BF16), jax.ShapeDtypeStruct(kv.shape, F32)],
        compiler_params=_cparams(1, VMEM_LIMIT_MB), name=name)(q, kv, do)


def _proj_bwd_dx(parts, w_in, dz, *, name, job=None):
    s = dz.shape[0]
    d = D_MODEL
    tm = min(256, s)
    widths = [p.shape[1] for p in parts]

    def body(*refs):
        part_refs, w_ref, dz_ref, o_ref = refs[:len(parts)], refs[len(parts)], refs[-2], refs[-1]
        acc = ALPHA * dz_ref[...]
        off = 0
        for p_ref, wd in zip(part_refs, widths):
            acc = acc + _dot(p_ref[...].astype(BF16), w_ref[:, off:off + wd], NT)
            off += wd
        o_ref[...] = acc

    (out,), job_out = _pallas_with_job(
        body, job, lambda: pl.program_id(0), s // tm, grid=(s // tm,),
        in_specs=[_row_spec(tm, wd) for wd in widths] + [pl.BlockSpec(w_in.shape, lambda i: (0, 0)), _row_spec(tm, d)],
        out_specs=[_row_spec(tm, d)], out_shape=[jax.ShapeDtypeStruct((s, d), F32)],
        compiler_params=_cparams(1, VMEM_LIMIT_MB), name=name, args=(*parts, w_in, dz))
    return out, job_out


def _proj_dw(h, parts, *, name):
    s, d = h.shape
    tk = min(1024, s)
    widths = [p.shape[1] for p in parts]

    def body(*refs):
        h_ref, part_refs, o_ref = refs[0], refs[1:-1], refs[-1]

        @pl.when(pl.program_id(0) == 0)
        def _():
            o_ref[...] = jnp.zeros_like(o_ref)

        ht = h_ref[...].astype(F32).T.astype(BF16)
        off = 0
        for p_ref, wd in zip(part_refs, widths):
            o_ref[:, off:off + wd] += _dot(ht, p_ref[...].astype(BF16))
            off += wd

    return pl.pallas_call(
        body, grid=(s // tk,), in_specs=[_row_spec(tk, d)] + [_row_spec(tk, wd) for wd in widths],
        out_specs=pl.BlockSpec((d, sum(widths)), lambda k: (0, 0)),
        out_shape=jax.ShapeDtypeStruct((d, sum(widths)), F32),
        compiler_params=_cparams(1, VMEM_LIMIT_MB), name=name)(h, *parts)


def _mesh_pos():
    return lax.axis_index("x"), lax.axis_index("y"), lax.axis_index("c")


class _GatherJob:
    n_phases = 3

    def __init__(self, shards):
        n = len(shards)
        self.inputs = list(shards)
        self.out_shape = [jax.ShapeDtypeStruct((N_DEV,) + a.shape, a.dtype) for a in shards]
        self.scratch = [pltpu.SemaphoreType.DMA((n, 7)), pltpu.SemaphoreType.DMA((n, 7)), pltpu.SemaphoreType.DMA((n,))]

    def phase(self, k, ins, outs, sems):
        send_sems, recv_sems, local_sems = sems
        n = len(ins)
        x, y, c = _mesh_pos()
        me, sibling = (x, y, c), (x, y, 1 - c)
        chips = [(1 - x, y), (x, 1 - y), (1 - x, 1 - y)]

        def copy(a, slot, block, to, src=None):
            dst = outs[a].at[4 * block[0] + 2 * block[1] + block[2]]
            return pltpu.make_async_remote_copy(
                src_ref=dst if src is None else src, dst_ref=dst,
                send_sem=send_sems.at[a, slot], recv_sem=recv_sems.at[a, slot], device_id=to, device_id_type=MESH)

        def mine():
            return [pltpu.make_async_copy(ins[a], outs[a].at[4 * x + 2 * y + c], local_sems.at[a]) for a in range(n)]

        def first():
            return [cp for a in range(n) for cp in
                    [copy(a, 0, me, sibling, src=ins[a])] + [copy(a, 1 + j, me, (*chip, c), src=ins[a]) for j, chip in enumerate(chips)]]

        def passed(a, j):
            return copy(a, 4 + j, (*chips[j], c), sibling)

        if k == 0:
            for cp in mine() + first():
                cp.start()
        elif k == 1:
            for a in range(n):
                for j, chip in enumerate(chips):
                    copy(a, 1 + j, (*chip, c), me).wait_recv()
                    passed(a, j).start()
        else:
            for a in range(n):
                copy(a, 0, sibling, me).wait_recv()
                for j, chip in enumerate(chips):
                    copy(a, 4 + j, (*chip, 1 - c), me).wait_recv()
            for cp in first() + [passed(a, j) for a in range(n) for j in range(3)]:
                cp.wait_send()
            for cp in mine():
                cp.wait()


class _SiblingJob:
    n_phases = 2

    def __init__(self, grads):
        n, half = len(grads), N_DEV // 2
        self.inputs = list(grads)
        self.out_shape = [jax.ShapeDtypeStruct((half,) + g.shape[1:], g.dtype) for g in grads]
        self.scratch = [pltpu.SemaphoreType.DMA((n, half)), pltpu.SemaphoreType.DMA((n, half))]

    def phase(self, k, ins, outs, sems):
        send_sems, recv_sems = sems
        x, y, c = _mesh_pos()
        copies = [pltpu.make_async_remote_copy(
            src_ref=ins[a].at[2 * j + (1 - c)], dst_ref=outs[a].at[j], send_sem=send_sems.at[a, j],
            recv_sem=recv_sems.at[a, j], device_id=(x, y, 1 - c), device_id_type=MESH)
            for a in range(len(ins)) for j in range(N_DEV // 2)]
        for cp in copies:
            if k == 0:
                cp.start()
            else:
                cp.wait()


class _ChipJob:
    n_phases = 2

    def __init__(self, partials):
        n, half = len(partials), N_DEV // 2
        self.inputs = list(partials)
        self.out_shape = [jax.ShapeDtypeStruct(p.shape, p.dtype) for p in partials]
        self.scratch = [pltpu.SemaphoreType.DMA((n, half)), pltpu.SemaphoreType.DMA((n, half)), pltpu.SemaphoreType.DMA((n,))]

    def phase(self, k, ins, outs, sems):
        send_sems, recv_sems, local_sems = sems
        n, half = len(ins), N_DEV // 2
        x, y, c = _mesh_pos()
        jme = 2 * x + y
        local = [pltpu.make_async_copy(ins[a].at[jme], outs[a].at[jme], local_sems.at[a]) for a in range(n)]
        for cp in local:
            if k == 0:
                cp.start()
            else:
                cp.wait()
        for j in range(half):
            @pl.when(jme != j)
            def _(j=j):
                for a in range(n):
                    if k == 0:
                        pltpu.make_async_remote_copy(
                            src_ref=ins[a].at[j], dst_ref=outs[a].at[jme], send_sem=send_sems.at[a, j],
                            recv_sem=recv_sems.at[a, jme], device_id=(j // 2, j % 2, c), device_id_type=MESH).start()
                    else:
                        arrive = pltpu.make_async_remote_copy(
                            src_ref=ins[a].at[j], dst_ref=outs[a].at[j], send_sem=send_sems.at[a, j],
                            recv_sem=recv_sems.at[a, j], device_id=(j // 2, j % 2, c), device_id_type=MESH)
                        arrive.wait_recv()
                        arrive.wait_send()


def _phase_steps(n_phases, n_steps):
    return [0, n_steps - 1] if n_phases == 2 else [0, n_steps // 2, n_steps - 1]


def _pallas_with_job(body, job, step_fn, n_steps, *, grid, in_specs, out_specs, out_shape, scratch_shapes=(),
                     compiler_params, name, args):
    in_specs, out_specs, out_shape, scratch_shapes = list(in_specs), list(out_specs), list(out_shape), list(scratch_shapes)
    if job is None:
        outs = pl.pallas_call(body, grid=grid, in_specs=in_specs, out_specs=out_specs, out_shape=out_shape,
                              scratch_shapes=scratch_shapes, compiler_params=compiler_params, name=name)(*args)
        return list(outs), None
    ni, no, ns = len(in_specs), len(out_specs), len(scratch_shapes)
    ji, jo = len(job.inputs), len(job.out_shape)
    steps = _phase_steps(job.n_phases, n_steps)

    def wrapped(*refs):
        own_in, job_in = refs[:ni], refs[ni:ni + ji]
        own_out, job_out = refs[ni + ji:ni + ji + no], refs[ni + ji + no:ni + ji + no + jo]
        own_scr, job_scr = refs[ni + ji + no + jo:ni + ji + no + jo + ns], refs[ni + ji + no + jo + ns:]
        step = step_fn()
        for k, at in enumerate(steps):
            @pl.when(step == at)
            def _(k=k):
                job.phase(k, job_in, job_out, job_scr)
        body(*own_in, *own_out, *own_scr)

    any_spec = pl.BlockSpec(memory_space=pl.ANY)
    outs = pl.pallas_call(
        wrapped, grid=grid, in_specs=in_specs + [any_spec] * ji, out_specs=out_specs + [any_spec] * jo,
        out_shape=out_shape + list(job.out_shape), scratch_shapes=scratch_shapes + list(job.scratch),
        compiler_params=compiler_params, name=name)(*args, *job.inputs)
    return list(outs[:no]), list(outs[no:])


def _run_job(job, *, name):
    ji, jo = len(job.inputs), len(job.out_shape)

    def body(*refs):
        for k in range(job.n_phases):
            job.phase(k, refs[:ji], refs[ji:ji + jo], refs[ji + jo:])

    any_spec = pl.BlockSpec(memory_space=pl.ANY)
    return pl.pallas_call(body, in_specs=[any_spec] * ji, out_specs=[any_spec] * jo, out_shape=list(job.out_shape),
                          scratch_shapes=list(job.scratch), name=name)(*job.inputs)


def _pair_add(grad, recv, c_idx, *, name):
    half, r, cdim = recv.shape
    tr = _row_tile(r, 512)

    def body(c_ref, g_ref, r_ref, t_ref, tb_ref):
        t = g_ref[...] + r_ref[...]
        t_ref[...] = t
        tb_ref[...] = t.astype(BF16)

    grid_spec = pltpu.PrefetchScalarGridSpec(
        num_scalar_prefetch=1, grid=(half, r // tr),
        in_specs=[pl.BlockSpec((None, tr, cdim), lambda j, i, c_ref: (2 * j + c_ref[0], i, 0)),
                  pl.BlockSpec((None, tr, cdim), lambda j, i, c_ref: (j, i, 0))],
        out_specs=[pl.BlockSpec((None, tr, cdim), lambda j, i, c_ref: (j, i, 0))] * 2)
    return pl.pallas_call(
        body, grid_spec=grid_spec,
        out_shape=[jax.ShapeDtypeStruct(recv.shape, F32), jax.ShapeDtypeStruct(recv.shape, BF16)],
        compiler_params=_cparams(2), name=name)(c_idx, grad, recv)


def _adam_math(w, g, m, v):
    m = ADAM_B1 * m + (1.0 - ADAM_B1) * g
    v = ADAM_B2 * v + (1.0 - ADAM_B2) * (g * g)
    m_hat = m / (1.0 - ADAM_B1 ** ADAM_STEP)
    v_hat = v / (1.0 - ADAM_B2 ** ADAM_STEP)
    delta = -ADAM_LR * (m_hat / (jnp.sqrt(v_hat) + ADAM_EPS) + ADAM_WD * w)
    return delta, m, v


def _reduce_adam(own, recv, j_idx, w, m, v, *, name):
    half, r, cdim = recv.shape
    tr = _row_tile(r, 256)

    def body(j_ref, own_ref, recv_ref, w_ref, m_ref, v_ref, g_ref, d_ref, nm_ref, nv_ref):
        jme = j_ref[0]
        g = own_ref[...]
        for sl in range(half):
            g = g + jnp.where(sl == jme, 0.0, recv_ref[sl].astype(F32))
        delta, nm, nv = _adam_math(w_ref[...], g, m_ref[...], v_ref[...])
        g_ref[...] = g
        d_ref[...] = delta
        nm_ref[...] = nm
        nv_ref[...] = nv

    tile = pl.BlockSpec((tr, cdim), lambda i, j_ref: (i, 0))
    grid_spec = pltpu.PrefetchScalarGridSpec(
        num_scalar_prefetch=1, grid=(r // tr,),
        in_specs=[pl.BlockSpec((None, tr, cdim), lambda i, j_ref: (j_ref[0], i, 0)),
                  pl.BlockSpec((half, tr, cdim), lambda i, j_ref: (0, i, 0)), tile, tile, tile],
        out_specs=[tile] * 4)
    return pl.pallas_call(
        body, grid_spec=grid_spec, out_shape=[jax.ShapeDtypeStruct((r, cdim), F32)] * 4,
        compiler_params=_cparams(1), name=name)(j_idx, own, recv, w, m, v)


class _SmallGatherJob:
    n_phases = 2

    def __init__(self, pack):
        self.inputs = [pack]
        self.out_shape = [jax.ShapeDtypeStruct((N_DEV,) + pack.shape, pack.dtype)]
        self.scratch = [pltpu.SemaphoreType.DMA((N_DEV,)), pltpu.SemaphoreType.DMA((N_DEV,)), pltpu.SemaphoreType.DMA(())]

    def phase(self, k, ins, outs, sems):
        (pack,), (land,), (send_sems, recv_sems, local_sem) = ins, outs, sems
        x, y, c = _mesh_pos()
        me = 4 * x + 2 * y + c
        mine = pltpu.make_async_copy(pack, land.at[me], local_sem)
        if k == 0:
            mine.start()
        else:
            mine.wait()
        for t in range(N_DEV):
            @pl.when(me != t)
            def _(t=t):
                peer = (t // 4, (t // 2) % 2, t % 2)
                if k == 0:
                    pltpu.make_async_remote_copy(
                        src_ref=pack, dst_ref=land.at[me], send_sem=send_sems.at[t], recv_sem=recv_sems.at[me],
                        device_id=peer, device_id_type=MESH).start()
                else:
                    cp = pltpu.make_async_remote_copy(
                        src_ref=pack, dst_ref=land.at[t], send_sem=send_sems.at[t], recv_sem=recv_sems.at[t],
                        device_id=peer, device_id_type=MESH)
                    cp.wait_recv()
                    cp.wait_send()


class _Jobs:
    def __init__(self, jobs):
        self.jobs = jobs
        self.n_phases = jobs[0].n_phases
        assert all(j.n_phases == self.n_phases for j in jobs)
        self.inputs = [a for j in jobs for a in j.inputs]
        self.out_shape = [o for j in jobs for o in j.out_shape]
        self.scratch = [sc for j in jobs for sc in j.scratch]

    def phase(self, k, ins, outs, sems):
        i = o = sc = 0
        for j in self.jobs:
            ni, no, nsc = len(j.inputs), len(j.out_shape), len(j.scratch)
            j.phase(k, ins[i:i + ni], outs[o:o + no], sems[sc:sc + nsc])
            i, o, sc = i + ni, o + no, sc + nsc


def _adam_small(w, land, m, v, *, name):
    def body(w_ref, land_ref, m_ref, v_ref, g_ref, d_ref, nm_ref, nv_ref):
        total = land_ref[0]
        for t in range(1, N_DEV):
            total = total + land_ref[t]
        g_ref[...] = total
        d_ref[...], nm_ref[...], nv_ref[...] = _adam_math(w_ref[...], total, m_ref[...], v_ref[...])

    vmem = pl.BlockSpec(memory_space=pltpu.VMEM)
    return pl.pallas_call(
        body, in_specs=[vmem] * 4, out_specs=[vmem] * 4, out_shape=[jax.ShapeDtypeStruct(w.shape, F32)] * 4,
        name=name)(w, land, m, v)


def _block_diag(w):
    h, a, b = w.shape
    eye = jnp.eye(h, dtype=w.dtype)
    return (eye[:, None, :, None] * w[:, :, None, :]).reshape(h * a, h * b)


def _diag_blocks(w, h):
    a = w.shape[0] // h
    return jnp.stack([w[i * a:(i + 1) * a, i * a:(i + 1) * a] for i in range(h)])


def _local_step(x, mem, target, p, hooks=None):
    g = {}
    p = dict(p)
    w_rg, w_ig = _block_diag(p["w_rgate"]).astype(BF16), _block_diag(p["w_igate"]).astype(BF16)

    (gu1, f1, xb, h1, h1b), arrived = _ffn_fwd(x, p["ffn1_w13"], p["ffn1_w2"], name="ffn1_fwd", ln=(p["ln1_g"], p["ln1_b"]),
                                               job=hooks and hooks.gather_job(_GATHER_MIXER))
    if hooks:
        p.update(hooks.gathered(_GATHER_MIXER, arrived))
    w_in = p["w_in"]
    w_in_lru, w_in_q, w_in_kv = w_in[:, :2 * LRU_W], w_in[:, 2 * LRU_W:2 * LRU_W + SB_W], w_in[:, 2 * LRU_W + SB_W:]
    xbg = _matmul(h1b, w_in_lru, name="proj_lru")
    q_sb = _matmul(h1b, w_in_q, out_dtype=BF16, scale=1.0 / math.sqrt(SB_HEAD), name="proj_q")
    kv_sb = _matmul(h1b, w_in_kv, out_dtype=BF16, name="proj_kv")
    xc, hseq, y_lru = _lru_fwd(xbg, p["conv_w"], p["conv_b"], w_rg, p["b_rgate"], w_ig, p["b_igate"],
                               p["lru_lambda"], name="lru_fwd")
    y_sb, arrived = _sb_fwd(q_sb, kv_sb, name="sb_fwd", job=hooks and hooks.gather_job(_GATHER_FFN2))
    if hooks:
        p.update(hooks.gathered(_GATHER_FFN2, arrived))
    ymix = _mixnorm_fwd(y_lru, y_sb, p["g_lru"], p["g_sb"], name="mixnorm_fwd")
    mix, h2, h2b = _matmul(ymix, p["w_out"], ln=(h1, 1.0, p["ln2_g"], p["ln2_b"]), name="mix_out_ln2")
    qm = _matmul(h2b, p["mem_wq"], out_dtype=BF16, name="cross_q")
    kv = _matmul(mem, p["mem_wkv"], name="cross_kv")
    o_cross = _cross_fwd(qm, kv, name="cross_fwd")
    cross, h3, h3b = _matmul(o_cross, p["mem_wo"], ln=(h2, 1.0, p["ln3_g"], p["ln3_b"]), name="cross_out_ln3")
    (gu2, f2, _), _ = _ffn_fwd(h3b, p["ffn2_w13"], p["ffn2_w2"], name="ffn2_fwd")

    dz4, dz4b, g["ln4_g"], g["ln4_b"], loss = _ln_bwd(h3, f2, 0.5, p["ln4_g"], p["ln4_b"], None, target=target, name="ln4_loss_bwd")
    (dh3, dgu2, act2), _ = _ffn_bwd(dz4, gu2, p["ffn2_w13"], p["ffn2_w2"], name="ffn2_bwd")
    g["ffn2_w13"] = _matmul(h3b, dgu2, trans_a=True, tm=1024, tn=FF_CHUNK, tk=2048, name="ffn2_dw13")
    g["ffn2_w2"] = _matmul(act2, dz4b, trans_a=True, tm=FF_CHUNK, tn=1024, tk=2048, scale=0.5, name="ffn2_dw2")
    dz3, dz3b, g["ln3_g"], g["ln3_b"] = _ln_bwd(h2, cross, 1.0, p["ln3_g"], p["ln3_b"], dh3, name="ln3_bwd")
    g["mem_wo"] = _matmul(o_cross, dz3b, trans_a=True, tm=1024, tn=1024, tk=1024, name="cross_dwo")
    do_cross = _matmul(dz3b, p["mem_wo"], trans_b=True, out_dtype=BF16, name="cross_do")
    dqm, dkv = _cross_bwd(qm, kv, do_cross, name="cross_bwd")
    g["mem_wq"] = _matmul(h2b, dqm, trans_a=True, tm=1024, tn=1024, tk=1024, name="cross_dwq")
    g["mem_wkv"] = _matmul(mem, dkv, trans_a=True, tm=1024, tn=MEM_HD, tk=256, name="cross_dwkv")
    dh2 = _matmul(dqm, p["mem_wq"], trans_b=True, res=dz3, res_coeff=ALPHA, name="cross_dh")
    dz2, dz2b, g["ln2_g"], g["ln2_b"] = _ln_bwd(h1, mix, 1.0, p["ln2_g"], p["ln2_b"], dh2, name="ln2_bwd")
    g["w_out"] = _matmul(ymix, dz2b, trans_a=True, tm=1024, tn=1024, tk=1024, name="mix_dwout")
    dymix = _matmul(dz2b, p["w_out"], trans_b=True, name="mix_dy")
    (dy_lru, dy_sb, g["g_lru"], g["g_sb"]), arrived = _mixnorm_bwd(
        y_lru, y_sb, p["g_lru"], p["g_sb"], dymix, name="mixnorm_bwd", job=hooks and hooks.sibling_job(_REDUCE_EARLY, g))
    (dq, dk, dv), arrived = _sb_bwd(q_sb, kv_sb, y_sb, dy_sb, name="sb_bwd",
                                    job=hooks and hooks.chip_job(_REDUCE_EARLY, arrived))
    if hooks:
        hooks.reduced(_REDUCE_EARLY, arrived)
    dxc, dgate, dwr, dwi, g["b_rgate"], g["b_igate"], g["lru_lambda"] = _lru_bwd(
        xc, xbg, hseq, dy_lru, w_rg, p["b_rgate"], w_ig, p["b_igate"], p["lru_lambda"], name="lru_bwd")
    g["w_rgate"], g["w_igate"] = _diag_blocks(dwr, 8), _diag_blocks(dwi, 8)
    dxb, g["conv_w"], g["conv_b"] = _conv_bwd(dxc, xbg, p["conv_w"], name="conv_bwd")
    parts = [dxb, dgate, dq, dk, dv]
    g["w_in"] = _proj_dw(h1b, parts, name="proj_dw")
    dh1, arrived = _proj_bwd_dx(parts, w_in, dz2, name="proj_dh", job=hooks and hooks.sibling_job(['w_in'], g))
    dz1, dz1b, g["ln1_g"], g["ln1_b"] = _ln_bwd(x, f1, 0.5, p["ln1_g"], p["ln1_b"], dh1, name="ln1_bwd")
    (grad_x, dgu1, act1), arrived = _ffn_bwd(
        dz1, gu1, p["ffn1_w13"], p["ffn1_w2"], name="ffn1_bwd",
        job=hooks and _Jobs([hooks.chip_job(['w_in'], arrived), hooks.small_job(g)]))
    if hooks:
        hooks.reduced(['w_in'], arrived[:1])
        hooks.small_land = arrived[1]
    g["ffn1_w13"] = _matmul(xb, dgu1, trans_a=True, tm=1024, tn=FF_CHUNK, tk=2048, name="ffn1_dw13")
    dw2 = dict(trans_a=True, tm=FF_CHUNK, tn=1024, tk=2048, scale=0.5, name="ffn1_dw2")
    if not hooks:
        g["ffn1_w2"] = _matmul(act1, dz1b, **dw2)
        return loss, grad_x, g
    arrived = _run_job(hooks.sibling_job(['ffn1_w13'], g), name="reduce_sibling_w13")
    g["ffn1_w2"], arrived = _matmul(act1, dz1b, job=hooks.chip_job(['ffn1_w13'], arrived), **dw2)
    hooks.reduced(['ffn1_w13'], arrived)
    arrived = _run_job(hooks.sibling_job(['ffn1_w2'], g), name="reduce_sibling_w2")
    hooks.reduced(['ffn1_w2'], _run_job(hooks.chip_job(['ffn1_w2'], arrived), name="reduce_chips_w2"))
    return loss, grad_x, g


_WEIGHTS = ['ffn1_w13', 'ffn1_w2', 'ln1_g', 'ln1_b', 'w_in', 'conv_w', 'conv_b', 'w_rgate', 'b_rgate', 'w_igate',
            'b_igate', 'lru_lambda', 'g_lru', 'g_sb', 'w_out', 'ln2_g', 'ln2_b', 'mem_wq', 'mem_wkv', 'mem_wo',
            'ln3_g', 'ln3_b', 'ffn2_w13', 'ffn2_w2', 'ln4_g', 'ln4_b']
_SHARDED = ['ffn1_w13', 'ffn1_w2', 'w_in', 'w_out', 'mem_wq', 'mem_wkv', 'mem_wo', 'ffn2_w13', 'ffn2_w2']
_SMALL = ['ln1_g', 'ln1_b', 'ln2_g', 'ln2_b', 'ln3_g', 'ln3_b', 'ln4_g', 'ln4_b', 'conv_b', 'b_rgate', 'b_igate',
          'lru_lambda', 'g_lru', 'g_sb', 'w_rgate', 'w_igate']


def _pack_small(d, conv_w_full=None):
    rows = [d[n].reshape(-1) for n in _SMALL]
    if conv_w_full is not None:
        rows.append(conv_w_full.reshape(-1))
    flat = jnp.concatenate(rows)
    pad = (-flat.shape[0]) % (8 * D_MODEL)
    return jnp.pad(flat, (0, pad)).reshape(-1, D_MODEL)


def _unpack_small(pack, like, with_conv):
    flat = pack.reshape(-1)
    out, off = {}, 0
    for n in _SMALL:
        size = math.prod(like[n].shape)
        out[n] = flat[off:off + size].reshape(like[n].shape)
        off += size
    conv = flat[off:off + CONV_W * LRU_W].reshape(CONV_W, LRU_W) if with_conv else None
    return out, conv


_GATHER_FIRST = ['ffn1_w13', 'ffn1_w2']
_GATHER_MIXER = ['w_in', 'conv_w', 'w_out', 'mem_wq', 'mem_wkv', 'mem_wo']
_GATHER_FFN2 = ['ffn2_w13', 'ffn2_w2']
_REDUCE_EARLY = ['ffn2_w13', 'ffn2_w2', 'mem_wo', 'mem_wq', 'mem_wkv', 'w_out']


def _weight_layout(n, full):
    if n in ('ffn1_w13', 'ffn2_w13', 'mem_wkv'):
        return full
    if n == 'w_in':
        return full.transpose(1, 0, 2).reshape(D_MODEL, -1)
    if n == 'conv_w':
        return full.transpose(1, 0, 2).reshape(CONV_W, LRU_W)
    return full.reshape(-1, full.shape[-1])


def _grad_blocks(n, g):
    if n == 'w_in':
        return g.reshape(D_MODEL, N_DEV, -1).transpose(1, 0, 2)
    return g.reshape((N_DEV, -1, g.shape[-1]))


class _Hooks:
    def __init__(self, w, c_idx):
        self.w, self.c_idx = w, c_idx
        self.sums, self.recv = {}, {}

    def shard(self, n):
        return self.w[n][0] if n == 'conv_w' else self.w[n][0].astype(BF16)

    def gather_job(self, names):
        return _GatherJob([self.shard(n) for n in names])

    def gathered(self, names, outs):
        return {n: _weight_layout(n, o) for n, o in zip(names, outs)}

    def sibling_job(self, names, g):
        self.blocks = [_grad_blocks(n, g[n]) for n in names]
        return _SiblingJob(self.blocks)

    def chip_job(self, names, from_sibling):
        for n, b, r in zip(names, self.blocks, from_sibling):
            self.sums[n] = _pair_add(b, r, self.c_idx, name=f"pair_add_{n}")
        return _ChipJob([self.sums[n][1] for n in names])

    def reduced(self, names, from_chips):
        self.recv.update(zip(names, from_chips))

    def small_job(self, g):
        return _SmallGatherJob(_pack_small(g, conv_w_full=g["conv_w"]))


def kernel(x, mem, ffn1_w13, ffn1_w2, ln1_g, ln1_b, w_in, conv_w, conv_b, w_rgate, b_rgate, w_igate, b_igate, lru_lambda, g_lru, g_sb, w_out, ln2_g, ln2_b, mem_wq, mem_wkv, mem_wo, ln3_g, ln3_b, ffn2_w13, ffn2_w2, ln4_g, ln4_b, loss_target, m_ffn1_w13, m_ffn1_w2, m_ln1_g, m_ln1_b, m_w_in, m_conv_w, m_conv_b, m_w_rgate, m_b_rgate, m_w_igate, m_b_igate, m_lru_lambda, m_g_lru, m_g_sb, m_w_out, m_ln2_g, m_ln2_b, m_mem_wq, m_mem_wkv, m_mem_wo, m_ln3_g, m_ln3_b, m_ffn2_w13, m_ffn2_w2, m_ln4_g, m_ln4_b, v_ffn1_w13, v_ffn1_w2, v_ln1_g, v_ln1_b, v_w_in, v_conv_w, v_conv_b, v_w_rgate, v_b_rgate, v_w_igate, v_b_igate, v_lru_lambda, v_g_lru, v_g_sb, v_w_out, v_ln2_g, v_ln2_b, v_mem_wq, v_mem_wkv, v_mem_wo, v_ln3_g, v_ln3_b, v_ffn2_w13, v_ffn2_w2, v_ln4_g, v_ln4_b):
    args = locals()
    w = {n: args[n] for n in _WEIGHTS}
    mom = {n: args["m_" + n] for n in _WEIGHTS}
    var = {n: args["v_" + n] for n in _WEIGHTS}
    ix, iy, ic = lax.axis_index("x"), lax.axis_index("y"), lax.axis_index("c")

    c_idx = jnp.reshape(ic, (1,)).astype(jnp.int32)
    j_idx = jnp.reshape(2 * ix + iy, (1,)).astype(jnp.int32)
    hooks = _Hooks(w, c_idx)
    p = {n: (w[n][0] if w[n].ndim == 4 else w[n]) for n in _WEIGHTS if n not in _SHARDED and n != 'conv_w'}
    p.update(hooks.gathered(_GATHER_FIRST, _run_job(hooks.gather_job(_GATHER_FIRST), name="gather_ffn1")))
    loss_local, grad_x, g = _local_step(x[0], mem[0], loss_target[0], p, hooks)
    loss = lax.psum(loss_local[0, 0], ("x", "y", "c"))

    grads, delta, new_m, new_v = {}, {}, {}, {}
    for n in _SHARDED:
        grads[n], delta[n], new_m[n], new_v[n] = (
            o.reshape(w[n].shape) for o in _reduce_adam(hooks.sums[n][0], hooks.recv[n], j_idx, w[n][0], mom[n][0],
                                                        var[n][0], name=f"adam_{n}"))

    me = 4 * ix + 2 * iy + ic
    own_cols = lambda full: lax.dynamic_slice_in_dim(full, me * (LRU_W // N_DEV), LRU_W // N_DEV, axis=1)
    land = hooks.small_land
    conv_off = sum(math.prod(w[n].shape) for n in _SMALL)
    land_conv = land.reshape(N_DEV, -1)[:, conv_off:conv_off + CONV_W * LRU_W].reshape(N_DEV, CONV_W, LRU_W)
    land_own = jnp.pad(own_cols(land_conv.reshape(N_DEV * CONV_W, LRU_W)).reshape(N_DEV, -1),
                       ((0, 0), (0, CONV_W * LRU_W - CONV_W * LRU_W // N_DEV)))
    land = lax.dynamic_update_slice(land.reshape(N_DEV, -1), land_own, (0, conv_off)).reshape(land.shape)
    pk = lambda d: _pack_small({n: d[n] for n in _SMALL}, conv_w_full=jnp.pad(d["conv_w"].reshape(-1), (0, CONV_W * LRU_W - CONV_W * LRU_W // N_DEV)))
    g_s, d_s, m_s, v_s = _adam_small(pk(w), land, pk(mom), pk(var), name="adam_small")
    for src, dst in ((g_s, grads), (d_s, delta), (m_s, new_m), (v_s, new_v)):
        vals, conv = _unpack_small(src, w, with_conv=True)
        dst.update(vals)
        dst["conv_w"] = conv.reshape(-1)[:CONV_W * LRU_W // N_DEV].reshape(w["conv_w"].shape)

    return (loss, grad_x[None], *[grads[n] for n in _WEIGHTS], *[delta[n] for n in _WEIGHTS],
            *[new_m[n] for n in _WEIGHTS], *[new_v[n] for n in _WEIGHTS])
```

```python
import functools
import math

import jax
import jax.numpy as jnp
from jax import lax
from jax.experimental import pallas as pl
from jax.experimental.pallas import tpu as pltpu

F32, BF16 = jnp.float32, jnp.bfloat16

D_MODEL = 1024
LRU_W = 512
SB_W = 512
SB_PAIR = 128
SB_HEAD = 64
SB_ROWS = 128
SB_SKIP = -110.0
D_FF = 2688
FF_CHUNK = 672
N_DEV = 8
MEM_HEADS = 4
MEM_HD = 256
CONV_W = 4
ALPHA = 2.0 ** 0.25
LN_EPS = 1e-5
RMS_EPS = 1e-6
LRU_C = 8.0
ADAM_LR, ADAM_B1, ADAM_B2, ADAM_EPS, ADAM_WD, ADAM_STEP = 0.001, 0.9, 0.999, 1e-08, 0.01, 10

NN = (((1,), (0,)), ((), ()))
NT = (((1,), (1,)), ((), ()))
MESH = pl.DeviceIdType.MESH
VMEM_LIMIT_MB = 56


def _dot(a, b, dn=NN):
    return lax.dot_general(a, b, dn, preferred_element_type=F32)


def _cparams(n_axes, vmem_mb=None):
    kw = dict(dimension_semantics=("arbitrary",) * n_axes)
    if vmem_mb is not None:
        kw["vmem_limit_bytes"] = vmem_mb << 20
    return pltpu.CompilerParams(**kw)


def _row_tile(rows, want):
    if rows <= want:
        return rows
    t = want - want % 8
    while rows % t:
        t -= 8
    return t


def _matmul(a, b, *, name, out_dtype=F32, trans_a=False, trans_b=False, tm=512, tn=1024, tk=1024,
            scale=1.0, res=None, res_coeff=1.0, job=None, ln=None):
    a_chunked, b_chunked = a.ndim == 3, b.ndim == 3
    nc = a.shape[0] if a_chunked else (b.shape[0] if b_chunked else 1)
    a2, b2 = a.shape[-2:], b.shape[-2:]
    (kdim, m) = a2 if trans_a else a2[::-1]
    n = b2[0] if trans_b else b2[1]
    tm, tn, tk = min(tm, m), min(tn, n), min(tk, kdim)
    assert m % tm == 0 and n % tn == 0 and kdim % tk == 0, (name, m, n, kdim)
    nk = kdim // tk

    def a_idx(c, i, j, k):
        idx = (k, i) if trans_a else (i, k)
        return (c,) + idx if a_chunked else idx

    def b_idx(c, i, j, k):
        idx = (j, k) if trans_b else (k, j)
        return (c,) + idx if b_chunked else idx

    a_blk = (tk, tm) if trans_a else (tm, tk)
    b_blk = (tn, tk) if trans_b else (tk, tn)
    in_specs = [pl.BlockSpec(((None,) + a_blk) if a_chunked else a_blk, a_idx),
                pl.BlockSpec(((None,) + b_blk) if b_chunked else b_blk, b_idx)]
    args = [a, b]
    if res is not None:
        in_specs.append(pl.BlockSpec((None, tm, tn), lambda c, i, j, k: (c, i, j)))
        args.append(res.reshape((nc, m, n)))
    if ln is not None:
        assert nc == 1 and tn == n and res is None
        in_specs += [pl.BlockSpec((tm, tn), lambda c, i, j, k: (i, 0)), pl.BlockSpec((1, tn), lambda c, i, j, k: (0, 0)),
                     pl.BlockSpec((1, tn), lambda c, i, j, k: (0, 0))]
        args += [ln[0], ln[2], ln[3]]

    def body(*refs):
        if res is not None:
            a_ref, b_ref, r_ref, o_ref, acc_ref = refs
        elif ln is not None:
            a_ref, b_ref, x_ref, g_ref, bb_ref, o_ref, h_ref, hb_ref, acc_ref = refs
        else:
            a_ref, b_ref, o_ref, acc_ref = refs
        k = pl.program_id(3)

        @pl.when(k == 0)
        def _():
            acc_ref[...] = jnp.zeros_like(acc_ref)

        av = a_ref[...]
        if trans_a:
            av = av.astype(F32).T
        acc_ref[...] += _dot(av.astype(BF16), b_ref[...].astype(BF16), NT if trans_b else NN)

        @pl.when(k == nk - 1)
        def _():
            out = acc_ref[...]
            if scale != 1.0:
                out = out * scale
            if res is not None:
                out = out + res_coeff * r_ref[...]
            o_ref[...] = out.astype(out_dtype)
            if ln is not None:
                h = _ln_math(ALPHA * x_ref[...] + ln[1] * out, g_ref[...], bb_ref[...])
                h_ref[...] = h
                hb_ref[...] = h.astype(BF16)

    grid = (nc, m // tm, n // tn, nk)

    def step():
        idx = pl.program_id(0)
        for ax in range(1, 4):
            idx = idx * grid[ax] + pl.program_id(ax)
        return idx

    out_specs = [pl.BlockSpec((None, tm, tn), lambda c, i, j, k: (c, i, j))]
    out_shape = [jax.ShapeDtypeStruct((nc, m, n), out_dtype)]
    if ln is not None:
        out_specs += [pl.BlockSpec((tm, tn), lambda c, i, j, k: (i, 0))] * 2
        out_shape += [jax.ShapeDtypeStruct((m, n), F32), jax.ShapeDtypeStruct((m, n), BF16)]
    outs, job_out = _pallas_with_job(
        body, job, step, math.prod(grid), grid=grid, in_specs=in_specs, out_specs=out_specs, out_shape=out_shape,
        scratch_shapes=[pltpu.VMEM((tm, tn), F32)],
        compiler_params=_cparams(4, VMEM_LIMIT_MB), name=name, args=args)
    out = outs[0] if (a_chunked or b_chunked) else outs[0][0]
    if ln is not None:
        return out, outs[1], outs[2]
    return out if job is None else (out, job_out)


def _ln_math(z, g, b):
    mu = jnp.mean(z, axis=-1, keepdims=True)
    zc = z - mu
    var = jnp.mean(zc * zc, axis=-1, keepdims=True)
    return zc * lax.rsqrt(var + LN_EPS) * g + b


def _row_spec(tm, d):
    return pl.BlockSpec((tm, d), lambda i: (i, 0))


def _par_spec(d, rows=1):
    return pl.BlockSpec((rows, d), lambda i: (0, 0))


def _ln_bwd(x, f, fscale, g, b, dy, *, name, target=None):
    s, d = x.shape
    tm = min(512, s)
    with_loss = target is not None

    def body(*refs):
        if with_loss:
            x_ref, f_ref, g_ref, b_ref, t_ref, dz_ref, dzb_ref, dg_ref, db_ref, loss_ref = refs
        else:
            x_ref, f_ref, g_ref, b_ref, dy_ref, dz_ref, dzb_ref, dg_ref, db_ref = refs
        i = pl.program_id(0)

        @pl.when(i == 0)
        def _():
            dg_ref[...] = jnp.zeros_like(dg_ref)
            db_ref[...] = jnp.zeros_like(db_ref)
            if with_loss:
                loss_ref[...] = jnp.zeros_like(loss_ref)

        z = ALPHA * x_ref[...] + fscale * f_ref[...]
        y, vjp = jax.vjp(_ln_math, z, g_ref[...], b_ref[...])
        if with_loss:
            err = y - t_ref[...]
            sq = jnp.sum(err * err, axis=1, keepdims=True)
            loss_ref[...] += jnp.sum(sq, axis=0, keepdims=True) * (0.5 / d)
            cot = err * (1.0 / d)
        else:
            cot = dy_ref[...]
        dz, dg, db = vjp(cot)
        dz_ref[...] = dz
        dzb_ref[...] = dz.astype(BF16)
        dg_ref[...] += dg
        db_ref[...] += db

    in_specs = [_row_spec(tm, d), _row_spec(tm, d), _par_spec(d), _par_spec(d), _row_spec(tm, d)]
    out_specs = [_row_spec(tm, d), _row_spec(tm, d), _par_spec(d), _par_spec(d)]
    out_shape = [jax.ShapeDtypeStruct((s, d), F32), jax.ShapeDtypeStruct((s, d), BF16),
                 jax.ShapeDtypeStruct((1, d), F32), jax.ShapeDtypeStruct((1, d), F32)]
    if with_loss:
        out_specs.append(pl.BlockSpec((1, 1), lambda i: (0, 0)))
        out_shape.append(jax.ShapeDtypeStruct((1, 1), F32))
    return pl.pallas_call(
        body, grid=(s // tm,), in_specs=in_specs, out_specs=out_specs, out_shape=out_shape,
        compiler_params=_cparams(1), name=name)(x, f, g, b, target if with_loss else dy)


def _load_weights_once(pairs, sem):
    @pl.when(pl.program_id(0) == 0)
    def _():
        copies = [pltpu.make_async_copy(src, dst, sem.at[n]) for n, (src, dst) in enumerate(pairs)]
        for c in copies:
            c.start()
        for c in copies:
            c.wait()


def _ffn_fwd(h, w13c, w2, *, name, job=None, ln=None):
    s, d = h.shape
    tm = min(256, s)
    half = N_DEV // 2

    def body(*refs):
        if ln is not None:
            h_ref, w13_hbm, w2_hbm, g_ref, b_ref, gu_ref, f_ref, hb_ref, n_ref, nb_ref, w13_v, w2_v, sem = refs
        else:
            h_ref, w13_hbm, w2_hbm, gu_ref, f_ref, hb_ref, w13_v, w2_v, sem = refs
        _load_weights_once([(w13_hbm, w13_v), (w2_hbm, w2_v)], sem)
        hb = h_ref[...].astype(BF16)
        hb_ref[...] = hb
        acc = jnp.zeros((tm, d), F32)
        for k in range(half):
            g = _dot(hb, w13_v[k])
            u = _dot(hb, w13_v[k + half])
            gu_ref[k] = g.astype(BF16)
            gu_ref[k + half] = u.astype(BF16)
            a = g * jax.nn.sigmoid(g) * u
            acc = acc + _dot(a.astype(BF16), w2_v[pl.ds(k * FF_CHUNK, FF_CHUNK), :])
        f_ref[...] = acc
        if ln is not None:
            hn = _ln_math(ALPHA * h_ref[...] + 0.5 * acc, g_ref[...], b_ref[...])
            n_ref[...] = hn
            nb_ref[...] = hn.astype(BF16)

    any_spec = pl.BlockSpec(memory_space=pl.ANY)
    n_ln = 0 if ln is None else 2
    return _pallas_with_job(
        body, job, lambda: pl.program_id(0), s // tm,
        grid=(s // tm,), in_specs=[_row_spec(tm, d), any_spec, any_spec] + [_par_spec(d)] * n_ln,
        out_specs=[pl.BlockSpec((N_DEV, tm, FF_CHUNK), lambda i: (0, i, 0)), _row_spec(tm, d), _row_spec(tm, d)]
                  + [_row_spec(tm, d)] * n_ln,
        out_shape=[jax.ShapeDtypeStruct((N_DEV, s, FF_CHUNK), BF16), jax.ShapeDtypeStruct((s, d), F32),
                   jax.ShapeDtypeStruct((s, d), BF16)]
                  + [jax.ShapeDtypeStruct((s, d), F32), jax.ShapeDtypeStruct((s, d), BF16)][:n_ln],
        scratch_shapes=[pltpu.VMEM(w13c.shape, BF16), pltpu.VMEM(w2.shape, BF16), pltpu.SemaphoreType.DMA((2,))],
        compiler_params=_cparams(1, VMEM_LIMIT_MB), name=name, args=(h, w13c, w2) + (() if ln is None else tuple(ln)))


def _ffn_bwd(dz, gu, w13c, w2, *, name, job=None):
    s, d = dz.shape
    tm = min(256, s)
    half = N_DEV // 2

    def body(dz_ref, gu_ref, w13_hbm, w2_hbm, dh_ref, dgu_ref, act_ref, w13_v, w2_v, sem):
        _load_weights_once([(w13_hbm, w13_v), (w2_hbm, w2_v)], sem)
        dzv = dz_ref[...]
        dfb = (0.5 * dzv).astype(BF16)
        acc = ALPHA * dzv
        for k in range(half):
            g = gu_ref[k].astype(F32)
            u = gu_ref[k + half].astype(F32)
            da = _dot(dfb, w2_v[pl.ds(k * FF_CHUNK, FF_CHUNK), :], NT)
            sg = jax.nn.sigmoid(g)
            silu = g * sg
            dg = (da * u * (sg * (1.0 + g * (1.0 - sg)))).astype(BF16)
            du = (da * silu).astype(BF16)
            act_ref[k] = (silu * u).astype(BF16)
            dgu_ref[k] = dg
            dgu_ref[k + half] = du
            acc = acc + _dot(dg, w13_v[k], NT) + _dot(du, w13_v[k + half], NT)
        dh_ref[...] = acc

    any_spec = pl.BlockSpec(memory_space=pl.ANY)
    return _pallas_with_job(
        body, job, lambda: pl.program_id(0), s // tm, grid=(s // tm,),
        in_specs=[_row_spec(tm, d), pl.BlockSpec((N_DEV, tm, FF_CHUNK), lambda i: (0, i, 0)), any_spec, any_spec],
        out_specs=[_row_spec(tm, d), pl.BlockSpec((N_DEV, tm, FF_CHUNK), lambda i: (0, i, 0)),
                   pl.BlockSpec((half, tm, FF_CHUNK), lambda i: (0, i, 0))],
        out_shape=[jax.ShapeDtypeStruct((s, d), F32), jax.ShapeDtypeStruct((N_DEV, s, FF_CHUNK), BF16),
                   jax.ShapeDtypeStruct((half, s, FF_CHUNK), BF16)],
        scratch_shapes=[pltpu.VMEM(w13c.shape, BF16), pltpu.VMEM(w2.shape, BF16), pltpu.SemaphoreType.DMA((2,))],
        compiler_params=_cparams(1, VMEM_LIMIT_MB), name=name, args=(dz, gu, w13c, w2))


def _shift_down(x, prev8, d):
    t, c = x.shape
    row = lax.broadcasted_iota(jnp.int32, (t, c), 0)
    xr = pltpu.roll(x, d, 0)
    pr = pltpu.roll(prev8, d, 0)
    if t > 8:
        pr = jnp.concatenate([pr, jnp.zeros((t - 8, c), x.dtype)], axis=0)
    return jnp.where(row < d, pr, xr)


def _shift_up(x, next8, d):
    t, c = x.shape
    row = lax.broadcasted_iota(jnp.int32, (t, c), 0)
    xr = pltpu.roll(x, t - d, 0)
    nr = pltpu.roll(next8, 8 - d, 0)
    if t > 8:
        nr = jnp.concatenate([jnp.zeros((t - 8, c), x.dtype), nr], axis=0)
    return jnp.where(row >= t - d, nr, xr)


def _scan_fwd(a, u, carry):
    t = a.shape[0]
    row = lax.broadcasted_iota(jnp.int32, a.shape, 0) % 8
    for d in (1, 2, 4):
        a_s, u_s = pltpu.roll(a, d, 0), pltpu.roll(u, d, 0)
        m = row >= d
        u = jnp.where(m, u + a * u_s, u)
        a = jnp.where(m, a * a_s, a)
    out = []
    for g in range(t // 8):
        hg = u[8 * g:8 * g + 8, :] + a[8 * g:8 * g + 8, :] * carry
        out.append(hg)
        carry = hg[7:8, :]
    return jnp.concatenate(out, axis=0)


def _scan_bwd(b, x, carry):
    t = b.shape[0]
    row = lax.broadcasted_iota(jnp.int32, b.shape, 0) % 8
    for d in (1, 2, 4):
        b_s, x_s = pltpu.roll(b, t - d, 0), pltpu.roll(x, t - d, 0)
        m = row < 8 - d
        x = jnp.where(m, x + b * x_s, x)
        b = jnp.where(m, b * b_s, b)
    out = [None] * (t // 8)
    for g in reversed(range(t // 8)):
        lg = x[8 * g:8 * g + 8, :] + b[8 * g:8 * g + 8, :] * carry
        out[g] = lg
        carry = lg[0:1, :]
    return jnp.concatenate(out, axis=0)


def _lru_elem(xc, pr, pi, b_r, b_i, lam):
    r = jax.nn.sigmoid(pr + b_r)
    ig = jax.nn.sigmoid(pi + b_i)
    softplus_neg_lam = jnp.maximum(-lam, 0.0) + jnp.log1p(jnp.exp(-jnp.abs(lam)))
    log_a = (-LRU_C) * r * softplus_neg_lam
    a = jnp.exp(log_a)
    one_minus_a2 = -jnp.tanh(log_a) * (jnp.exp(2.0 * log_a) + 1.0)
    u = jnp.sqrt(one_minus_a2) * (ig * xc)
    return a, u


def _gelu_tanh(x):
    return 0.5 * x * (1.0 + jnp.tanh(math.sqrt(2.0 / math.pi) * (x + 0.044715 * (x * x * x))))


def _conv_fwd(xb, prev8, w, b):
    out = b + w[3:4, :] * xb
    for d in range(1, CONV_W):
        out = out + w[3 - d:4 - d, :] * _shift_down(xb, prev8, d)
    return out


def _lru_fwd(xbg, conv_w, conv_b, w_rg, b_rg, w_ig, b_ig, lam, *, name):
    s = xbg.shape[0]
    c = LRU_W
    t = min(512, s)
    t8 = t // 8

    def body(xb_ref, gate_ref, prev_ref, cw_ref, cb_ref, wr_ref, br_ref, wi_ref, bi_ref, lam_ref,
             xc_ref, h_ref, y_ref, carry_ref):
        i = pl.program_id(0)

        @pl.when(i == 0)
        def _():
            carry_ref[...] = jnp.zeros_like(carry_ref)

        prev8 = jnp.where(i > 0, prev_ref[...], 0.0)
        xc = _conv_fwd(xb_ref[...], prev8, cw_ref[...], cb_ref[...])
        xcb = xc.astype(BF16)
        a, u = _lru_elem(xc, _dot(xcb, wr_ref[...]), _dot(xcb, wi_ref[...]), br_ref[...], bi_ref[...], lam_ref[...])
        h = _scan_fwd(a, u, carry_ref[0:1, :])
        carry_ref[...] = jnp.broadcast_to(h[t - 1:t, :], carry_ref.shape)
        xc_ref[...] = xc
        h_ref[...] = h
        y_ref[...] = h * _gelu_tanh(gate_ref[...])

    tile = lambda col: pl.BlockSpec((t, c), lambda i: (i, col))
    par = lambda rows: pl.BlockSpec((rows, c), lambda i: (0, 0))
    out_spec = pl.BlockSpec((t, c), lambda i: (i, 0))
    return pl.pallas_call(
        body, grid=(s // t,),
        in_specs=[tile(0), tile(1), pl.BlockSpec((8, c), lambda i: (jnp.maximum(i * t8 - 1, 0), 0)),
                  par(CONV_W), par(1), pl.BlockSpec((c, c), lambda i: (0, 0)), par(1),
                  pl.BlockSpec((c, c), lambda i: (0, 0)), par(1), par(1)],
        out_specs=[out_spec, out_spec, out_spec],
        out_shape=[jax.ShapeDtypeStruct((s, c), F32)] * 3,
        scratch_shapes=[pltpu.VMEM((8, c), F32)],
        compiler_params=_cparams(1, VMEM_LIMIT_MB), name=name)(
            xbg, xbg, xbg, conv_w, conv_b, w_rg, b_rg, w_ig, b_ig, lam)


def _lru_bwd(xc, xbg, hseq, dy, w_rg, b_rg, w_ig, b_ig, lam, *, name):
    s, c = xc.shape
    t = min(512, s)
    t8 = t // 8
    nt = s // t

    def body(xc_ref, gate_ref, h_ref, hprev_ref, dy_ref, wr_ref, br_ref, wi_ref, bi_ref, lam_ref,
             dxc_ref, dgate_ref, dwr_ref, dwi_ref, dbr_ref, dbi_ref, dlam_ref, a8_ref, l8_ref):
        i = pl.program_id(0)

        @pl.when(i == 0)
        def _():
            for r in (dwr_ref, dwi_ref, dbr_ref, dbi_ref, dlam_ref, a8_ref, l8_ref):
                r[...] = jnp.zeros_like(r)

        xc_v = xc_ref[...]
        xcb = xc_v.astype(BF16)
        (a, _), vjp = jax.vjp(_lru_elem, xc_v, _dot(xcb, wr_ref[...]), _dot(xcb, wi_ref[...]),
                              br_ref[...], bi_ref[...], lam_ref[...])
        gl, vjp_gelu = jax.vjp(_gelu_tanh, gate_ref[...])
        dyv, hs = dy_ref[...], h_ref[...]
        (dgate,) = vjp_gelu(dyv * hs)
        b_next = _shift_up(a, a8_ref[...], 1)
        lt = _scan_bwd(b_next, dyv * gl, l8_ref[0:1, :])
        a8_ref[...] = a[0:8, :]
        l8_ref[...] = lt[0:8, :]
        hprev8 = jnp.where(i < nt - 1, hprev_ref[...], 0.0)
        da = lt * _shift_down(hs, hprev8, 1)
        dxc_e, dpr, dpi, dbr, dbi, dlam = vjp((da, lt))
        dprb, dpib = dpr.astype(BF16), dpi.astype(BF16)
        dxc_ref[...] = dxc_e + _dot(dprb, wr_ref[...], NT) + _dot(dpib, wi_ref[...], NT)
        dgate_ref[...] = dgate.astype(BF16)
        xct = xc_v.T.astype(BF16)
        dwr_ref[...] += _dot(xct, dprb)
        dwi_ref[...] += _dot(xct, dpib)
        dbr_ref[...] += dbr
        dbi_ref[...] += dbi
        dlam_ref[...] += dlam

    rev = lambda col: pl.BlockSpec((t, c), lambda i: (nt - 1 - i, col))
    par = pl.BlockSpec((1, c), lambda i: (0, 0))
    sq = pl.BlockSpec((c, c), lambda i: (0, 0))
    return pl.pallas_call(
        body, grid=(nt,),
        in_specs=[rev(0), rev(1), rev(0), pl.BlockSpec((8, c), lambda i: (jnp.maximum((nt - 1 - i) * t8 - 1, 0), 0)),
                  rev(0), sq, par, sq, par, par],
        out_specs=[rev(0), rev(0), sq, sq, par, par, par],
        out_shape=[jax.ShapeDtypeStruct((s, c), F32), jax.ShapeDtypeStruct((s, c), BF16)]
                  + [jax.ShapeDtypeStruct((c, c), F32)] * 2 + [jax.ShapeDtypeStruct((1, c), F32)] * 3,
        scratch_shapes=[pltpu.VMEM((8, c), F32), pltpu.VMEM((8, c), F32)],
        compiler_params=_cparams(1, VMEM_LIMIT_MB), name=name)(
            xc, xbg, hseq, hseq, dy, w_rg, b_rg, w_ig, b_ig, lam)


def _conv_bwd(dxc, xbg, conv_w, *, name):
    s, c = dxc.shape
    t = min(512, s)
    t8 = t // 8
    nt = s // t

    def body(dxc_ref, next_ref, xb_ref, prev_ref, w_ref, dxb_ref, dw_ref, db_ref):
        i = pl.program_id(0)

        @pl.when(i == 0)
        def _():
            dw_ref[...] = jnp.zeros_like(dw_ref)
            db_ref[...] = jnp.zeros_like(db_ref)

        g = dxc_ref[...]
        xb = xb_ref[...]
        w = w_ref[...]
        next8 = jnp.where(i < nt - 1, next_ref[...], 0.0)
        prev8 = jnp.where(i > 0, prev_ref[...], 0.0)
        dxb = w[3:4, :] * g
        dw_ref[3:4, :] += jnp.sum(g * xb, axis=0, keepdims=True)
        for d in range(1, CONV_W):
            dxb = dxb + w[3 - d:4 - d, :] * _shift_up(g, next8, d)
            dw_ref[3 - d:4 - d, :] += jnp.sum(g * _shift_down(xb, prev8, d), axis=0, keepdims=True)
        dxb_ref[...] = dxb.astype(BF16)
        db_ref[...] += jnp.sum(g, axis=0, keepdims=True)

    tile = pl.BlockSpec((t, c), lambda i: (i, 0))
    return pl.pallas_call(
        body, grid=(nt,),
        in_specs=[tile, pl.BlockSpec((8, c), lambda i: (jnp.minimum((i + 1) * t8, s // 8 - 1), 0)),
                  tile, pl.BlockSpec((8, c), lambda i: (jnp.maximum(i * t8 - 1, 0), 0)),
                  pl.BlockSpec((CONV_W, c), lambda i: (0, 0))],
        out_specs=[tile, pl.BlockSpec((CONV_W, c), lambda i: (0, 0)), pl.BlockSpec((1, c), lambda i: (0, 0))],
        out_shape=[jax.ShapeDtypeStruct((s, c), BF16), jax.ShapeDtypeStruct((CONV_W, c), F32),
                   jax.ShapeDtypeStruct((1, c), F32)],
        compiler_params=_cparams(1), name=name)(dxc, dxc, xbg, xbg, conv_w)


def _hi_lo_dot(x, tri):
    hi = x.astype(BF16)
    lo = (x - hi.astype(F32)).astype(BF16)
    return _dot(hi, tri) + _dot(lo, tri)


def _tri(tk, inclusive):
    r = lax.broadcasted_iota(jnp.int32, (tk, tk), 0)
    c = lax.broadcasted_iota(jnp.int32, (tk, tk), 1)
    return ((r >= c) if inclusive else (r > c)).astype(BF16)


def _split_heads(x):
    first = lax.broadcasted_iota(jnp.int32, x.shape, 1) < SB_HEAD
    zero = jnp.zeros_like(x)
    return first, (jnp.where(first, x, zero), jnp.where(first, zero, x))


def _sb_softplus_terms(z):
    ls = jnp.minimum(z, 0.0) - jnp.log(1.0 + jnp.exp(-jnp.abs(z)))
    return ls, ls - z


def _sb_alive(runs):
    m = runs[0]
    for r in runs[1:]:
        m = jnp.maximum(m, r)
    return jnp.max(m) > SB_SKIP


def _sb_fwd(q, kv, *, name, job=None):
    s = q.shape[0]
    tb = min(256, s)
    rc = min(SB_ROWS, tb)
    npair = SB_W // SB_PAIR
    chains = [(h, c) for h in range(2) for c in range(tb // rc)]

    def body(q_ref, k_ref, v_ref, o_ref):
        i = pl.program_id(1)
        first, qs = _split_heads(q_ref[...])
        tri = _tri(tb, False)
        causal = lax.broadcasted_iota(jnp.int32, (tb, tb), 1) < lax.broadcasted_iota(jnp.int32, (tb, tb), 0)
        causal_c = [causal[c * rc:(c + 1) * rc, :] for c in range(tb // rc)]

        def group(j_lo, width, carry, diag_last):
            k0 = pl.multiple_of(j_lo * tb, tb)
            kb = k_ref[pl.ds(k0, width * tb), :]
            vb = v_ref[pl.ds(k0, width * tb), :]
            zw = [_dot(qs[h][c * rc:(c + 1) * rc, :], kb, NT) for h, c in chains]
            cols = [slice(t * tb, (t + 1) * tb) for t in range(width)]
            terms = [[_sb_softplus_terms(z[:, cl]) for cl in cols] for z in zw]
            lss = [[ls for ls, _ in tc] for tc in terms]
            ns = [[n for _, n in tc] for tc in terms]
            if diag_last:
                ns = [nc[:-1] + [jnp.where(causal_c[c], nc[-1], 0.0)] for nc, (h, c) in zip(ns, chains)]
            sufs = [[_hi_lo_dot(n, tri) for n in nc] for nc in ns]
            out = []
            for (h, c), lc, nc, sc, (run, acc) in zip(chains, lss, ns, sufs, carry):
                ws = [None] * width
                for t in reversed(range(width)):
                    w = jnp.exp(lc[t] + (sc[t] + run))
                    if diag_last and t == width - 1:
                        w = jnp.where(causal_c[c], w, 0.0)
                    ws[t] = w.astype(BF16)
                    run = run + jnp.sum(nc[t], axis=1, keepdims=True)
                out.append((run, acc + _dot(ws[0] if width == 1 else jnp.concatenate(ws, axis=1), vb)))
            return tuple(out)

        zero = (jnp.zeros((rc, 1), F32), jnp.zeros((rc, SB_PAIR), F32))
        carry = lax.cond(i == 0, lambda c: group(0, 1, c, True), lambda c: group(i - 1, 2, c, True),
                         (zero,) * len(chains))

        def step(st):
            j, _, c = st
            c = group(j, 1, c, False)
            return j - 1, _sb_alive([run for run, _ in c]), c

        _, _, carry = lax.while_loop(lambda st: jnp.logical_and(st[0] >= 0, st[1]), step,
                                     (i - 2, _sb_alive([run for run, _ in carry]), carry))
        accs = [jnp.concatenate([carry[n][1] for n, (h, c) in enumerate(chains) if h == hh], axis=0) for hh in range(2)]
        o_ref[...] = jnp.where(first, accs[0], accs[1])

    (o,), job_out = _pallas_with_job(
        body, job, lambda: pl.program_id(0) * (s // tb) + pl.program_id(1), npair * (s // tb),
        grid=(npair, s // tb),
        in_specs=[pl.BlockSpec((tb, SB_PAIR), lambda p, i: (i, p)),
                  pl.BlockSpec((s, SB_PAIR), lambda p, i: (0, p)),
                  pl.BlockSpec((s, SB_PAIR), lambda p, i: (0, npair + p))],
        out_specs=[pl.BlockSpec((tb, SB_PAIR), lambda p, i: (i, p))],
        out_shape=[jax.ShapeDtypeStruct((s, SB_W), F32)],
        compiler_params=_cparams(2, VMEM_LIMIT_MB), name=name, args=(q, kv, kv))
    return o, job_out


def _sb_bwd(q, kv, o, do, *, name, job=None):
    s = q.shape[0]
    tb = min(256, s)
    rc = min(SB_ROWS, tb)
    npair = SB_W // SB_PAIR
    chains = [(h, c) for h in range(2) for c in range(tb // rc)]

    def body(q_ref, k_ref, v_ref, o_ref, do_ref, dq_ref, dkb_ref, dvb_ref, dk_ref, dv_ref):
        i = pl.program_id(1)

        @pl.when(i == 0)
        def _():
            dk_ref[...] = jnp.zeros_like(dk_ref)
            dv_ref[...] = jnp.zeros_like(dv_ref)

        first, qs = _split_heads(q_ref[...])
        dob = do_ref[...].astype(BF16)
        _, dos = _split_heads(dob)
        prod = dob.astype(F32) * o_ref[...]
        deltas = (jnp.sum(jnp.where(first, prod, 0.0), axis=1, keepdims=True),
                  jnp.sum(jnp.where(first, 0.0, prod), axis=1, keepdims=True))
        tri_x, tri_i = _tri(tb, False), _tri(tb, True)
        causal = lax.broadcasted_iota(jnp.int32, (tb, tb), 1) < lax.broadcasted_iota(jnp.int32, (tb, tb), 0)
        causal_c = [causal[c * rc:(c + 1) * rc, :] for c in range(tb // rc)]
        rows = [slice(c * rc, (c + 1) * rc) for c in range(tb // rc)]

        def group(j_lo, width, carry, diag_last):
            k0 = pl.multiple_of(j_lo * tb, tb)
            kb = k_ref[pl.ds(k0, width * tb), :]
            vb = v_ref[pl.ds(k0, width * tb), :]
            cols = [slice(t * tb, (t + 1) * tb) for t in range(width)]
            zw = [_dot(qs[h][rows[c], :], kb, NT) for h, c in chains]
            dww = [_dot(dos[h][rows[c], :], vb, NT) for h, c in chains]
            terms = [[_sb_softplus_terms(z[:, cl]) for cl in cols] for z in zw]
            lss = [[ls for ls, _ in tc] for tc in terms]
            ns = [[n for _, n in tc] for tc in terms]
            if diag_last:
                ns = [nc[:-1] + [jnp.where(causal_c[c], nc[-1], 0.0)] for nc, (h, c) in zip(ns, chains)]
            sufs = [[_hi_lo_dot(n, tri_x) for n in nc] for nc in ns]
            wbs, gs, runs_n = [], [], []
            for (h, c), lc, nc, sc, dw, (run_n, _, _) in zip(chains, lss, ns, sufs, dww, carry):
                wb, g = [None] * width, [None] * width
                for t in reversed(range(width)):
                    w = jnp.exp(lc[t] + (sc[t] + run_n))
                    if diag_last and t == width - 1:
                        w = jnp.where(causal_c[c], w, 0.0)
                    wb[t] = w.astype(BF16)
                    g[t] = wb[t].astype(F32) * dw[:, cols[t]]
                    run_n = run_n + jnp.sum(nc[t], axis=1, keepdims=True)
                wbs.append(wb)
                gs.append(g)
                runs_n.append(run_n)
            gsufs = [[_hi_lo_dot(g, tri_i) for g in gc] for gc in gs]
            out = []
            dk_t = [jnp.zeros((tb, SB_PAIR), F32) for _ in range(width)]
            dv_t = [jnp.zeros((tb, SB_PAIR), F32) for _ in range(width)]
            for (h, c), lc, gc, gsc, wb, run_n, (_, run_g, dq_acc) in zip(chains, lss, gs, gsufs, wbs, runs_n, carry):
                qh, doh, delta = qs[h][rows[c], :], dos[h][rows[c], :], deltas[h][rows[c], :]
                dzb = [None] * width
                for t in reversed(range(width)):
                    pre = delta - (gsc[t] + run_g)
                    dz = gc[t] - jnp.exp(lc[t]) * (gc[t] + pre)
                    if diag_last and t == width - 1:
                        dz = jnp.where(causal_c[c], dz, 0.0)
                    run_g = run_g + jnp.sum(gc[t], axis=1, keepdims=True)
                    dzb[t] = dz.astype(BF16)
                    dk_t[t] = dk_t[t] + _dot(dz.T.astype(BF16), qh)
                    dv_t[t] = dv_t[t] + _dot(wb[t].astype(F32).T.astype(BF16), doh)
                out.append((run_n, run_g, dq_acc + _dot(dzb[0] if width == 1 else jnp.concatenate(dzb, axis=1), kb)))
            for t in range(width):
                dk_ref[pl.ds(pl.multiple_of((j_lo + t) * tb, tb), tb), :] += dk_t[t]
                dv_ref[pl.ds(pl.multiple_of((j_lo + t) * tb, tb), tb), :] += dv_t[t]
            return tuple(out)

        zero = (jnp.zeros((rc, 1), F32), jnp.zeros((rc, 1), F32), jnp.zeros((rc, SB_PAIR), F32))
        carry = lax.cond(i == 0, lambda c: group(0, 1, c, True), lambda c: group(i - 1, 2, c, True),
                         (zero,) * len(chains))

        def step(st):
            j, _, c = st
            c = group(j, 1, c, False)
            return j - 1, _sb_alive([r[0] for r in c]), c

        _, _, carry = lax.while_loop(lambda st: jnp.logical_and(st[0] >= 0, st[1]), step,
                                     (i - 2, _sb_alive([r[0] for r in carry]), carry))
        dqs = [jnp.concatenate([carry[n][2] for n, (h, c) in enumerate(chains) if h == hh], axis=0) for hh in range(2)]
        dq_ref[...] = (jnp.where(first, dqs[0], dqs[1]) * (1.0 / math.sqrt(SB_HEAD))).astype(BF16)

        @pl.when(i == s // tb - 1)
        def _():
            dkb_ref[...] = dk_ref[...].astype(BF16)
            dvb_ref[...] = dv_ref[...].astype(BF16)

    qtile = pl.BlockSpec((tb, SB_PAIR), lambda p, i: (i, p))
    col = pl.BlockSpec((s, SB_PAIR), lambda p, i: (0, p))
    return _pallas_with_job(
        body, job, lambda: pl.program_id(0) * (s // tb) + pl.program_id(1), npair * (s // tb),
        grid=(npair, s // tb),
        in_specs=[qtile, col, pl.BlockSpec((s, SB_PAIR), lambda p, i: (0, npair + p)), qtile, qtile],
        out_specs=[qtile, col, col],
        out_shape=[jax.ShapeDtypeStruct((s, SB_W), BF16)] * 3,
        scratch_shapes=[pltpu.VMEM((s, SB_PAIR), F32)] * 2,
        compiler_params=_cparams(2, VMEM_LIMIT_MB), name=name, args=(q, kv, kv, o, do))


def _mixnorm_math(yl, ys, gl, gs):
    def rms(x, g):
        return x * lax.rsqrt(jnp.mean(x * x, axis=-1, keepdims=True) + RMS_EPS) * g
    return rms(yl, gl), rms(ys, gs)


def _mixnorm_fwd(yl, ys, gl, gs, *, name):
    s, c = yl.shape
    tm = min(512, s)

    def body(yl_ref, ys_ref, gl_ref, gs_ref, y_ref):
        a, b = _mixnorm_math(yl_ref[...], ys_ref[...], gl_ref[...], gs_ref[...])
        y_ref[:, 0:c] = a.astype(BF16)
        y_ref[:, c:2 * c] = b.astype(BF16)

    return pl.pallas_call(
        body, grid=(s // tm,), in_specs=[_row_spec(tm, c), _row_spec(tm, c), _par_spec(c), _par_spec(c)],
        out_specs=_row_spec(tm, 2 * c), out_shape=jax.ShapeDtypeStruct((s, 2 * c), BF16),
        compiler_params=_cparams(1), name=name)(yl, ys, gl, gs)


def _mixnorm_bwd(yl, ys, gl, gs, dy, *, name, job=None):
    s, c = yl.shape
    tm = min(512, s)

    def body(yl_ref, ys_ref, gl_ref, gs_ref, dy_ref, dyl_ref, dys_ref, dgl_ref, dgs_ref):
        @pl.when(pl.program_id(0) == 0)
        def _():
            dgl_ref[...] = jnp.zeros_like(dgl_ref)
            dgs_ref[...] = jnp.zeros_like(dgs_ref)

        _, vjp = jax.vjp(_mixnorm_math, yl_ref[...], ys_ref[...], gl_ref[...], gs_ref[...])
        dyl, dys, dgl, dgs = vjp((dy_ref[:, 0:c], dy_ref[:, c:2 * c]))
        dyl_ref[...] = dyl
        dys_ref[...] = dys
        dgl_ref[...] += dgl
        dgs_ref[...] += dgs

    return _pallas_with_job(
        body, job, lambda: pl.program_id(0), s // tm, grid=(s // tm,),
        in_specs=[_row_spec(tm, c), _row_spec(tm, c), _par_spec(c), _par_spec(c), _row_spec(tm, 2 * c)],
        out_specs=[_row_spec(tm, c), _row_spec(tm, c), _par_spec(c), _par_spec(c)],
        out_shape=[jax.ShapeDtypeStruct((s, c), F32)] * 2 + [jax.ShapeDtypeStruct((1, c), F32)] * 2,
        compiler_params=_cparams(1), name=name, args=(yl, ys, gl, gs, dy))


def _cross_probs(qh, kh):
    sc = _dot(qh, kh, NT) * (1.0 / math.sqrt(MEM_HD))
    e = jnp.exp(sc - jnp.max(sc, axis=-1, keepdims=True))
    return e / jnp.sum(e, axis=-1, keepdims=True)


def _cross_fwd(q, kv, *, name):
    s, d = q.shape
    mlen = kv.shape[1]
    tm = min(512, s)

    def body(q_ref, kv_ref, o_ref):
        for h in range(MEM_HEADS):
            cols = slice(h * MEM_HD, (h + 1) * MEM_HD)
            p = _cross_probs(q_ref[:, cols].astype(BF16), kv_ref[h].astype(BF16))
            o_ref[:, cols] = _dot(p.astype(BF16), kv_ref[MEM_HEADS + h].astype(BF16)).astype(BF16)

    return pl.pallas_call(
        body, grid=(s // tm,),
        in_specs=[_row_spec(tm, d), pl.BlockSpec((2 * MEM_HEADS, mlen, MEM_HD), lambda i: (0, 0, 0))],
        out_specs=_row_spec(tm, d), out_shape=jax.ShapeDtypeStruct((s, d), BF16),
        compiler_params=_cparams(1, VMEM_LIMIT_MB), name=name)(q, kv)


def _cross_bwd(q, kv, do, *, name):
    s, d = q.shape
    mlen = kv.shape[1]
    tm = min(512, s)

    def body(q_ref, kv_ref, do_ref, dq_ref, dkv_ref):
        @pl.when(pl.program_id(0) == 0)
        def _():
            dkv_ref[...] = jnp.zeros_like(dkv_ref)

        for h in range(MEM_HEADS):
            cols = slice(h * MEM_HD, (h + 1) * MEM_HD)
            qh = q_ref[:, cols].astype(BF16)
            kh = kv_ref[h].astype(BF16)
            doh = do_ref[:, cols].astype(BF16)
            p = _cross_probs(qh, kh)
            dp = _dot(doh, kv_ref[MEM_HEADS + h].astype(BF16), NT)
            ds = p * (dp - jnp.sum(dp * p, axis=-1, keepdims=True)) * (1.0 / math.sqrt(MEM_HD))
            dq_ref[:, cols] = _dot(ds.astype(BF16), kh).astype(BF16)
            dkv_ref[h] += _dot(ds.T.astype(BF16), qh)
            dkv_ref[MEM_HEADS + h] += _dot(p.T.astype(BF16), doh)

    kv_spec = pl.BlockSpec((2 * MEM_HEADS, mlen, MEM_HD), lambda i: (0, 0, 0))
    return pl.pallas_call(
        body, grid=(s // tm,), in_specs=[_row_spec(tm, d), kv_spec, _row_spec(tm, d)],
        out_specs=[_row_spec(tm, d), kv_spec],
        out_shape=[jax.ShapeDtypeStruct((s, d), BF16), jax.ShapeDtypeStruct(kv.shape, F32)],
        compiler_params=_cparams(1, VMEM_LIMIT_MB), name=name)(q, kv, do)


def _proj_bwd_dx(parts, w_in, dz, *, name, job=None):
    s = dz.shape[0]
    d = D_MODEL
    tm = min(256, s)
    widths = [p.shape[1] for p in parts]

    def body(*refs):
        part_refs, w_ref, dz_ref, o_ref = refs[:len(parts)], refs[len(parts)], refs[-2], refs[-1]
        acc = ALPHA * dz_ref[...]
        off = 0
        for p_ref, wd in zip(part_refs, widths):
            acc = acc + _dot(p_ref[...].astype(BF16), w_ref[:, off:off + wd], NT)
            off += wd
        o_ref[...] = acc

    (out,), job_out = _pallas_with_job(
        body, job, lambda: pl.program_id(0), s // tm, grid=(s // tm,),
        in_specs=[_row_spec(tm, wd) for wd in widths] + [pl.BlockSpec(w_in.shape, lambda i: (0, 0)), _row_spec(tm, d)],
        out_specs=[_row_spec(tm, d)], out_shape=[jax.ShapeDtypeStruct((s, d), F32)],
        compiler_params=_cparams(1, VMEM_LIMIT_MB), name=name, args=(*parts, w_in, dz))
    return out, job_out


def _proj_dw(h, parts, *, name):
    s, d = h.shape
    tk = min(1024, s)
    widths = [p.shape[1] for p in parts]

    def body(*refs):
        h_ref, part_refs, o_ref, acc_ref = refs[0], refs[1:-2], refs[-2], refs[-1]
        k = pl.program_id(0)

        @pl.when(k == 0)
        def _():
            acc_ref[...] = jnp.zeros_like(acc_ref)

        ht = h_ref[...].astype(F32).T.astype(BF16)
        off = 0
        for p_ref, wd in zip(part_refs, widths):
            acc_ref[:, off:off + wd] += _dot(ht, p_ref[...].astype(BF16))
            off += wd

        @pl.when(k == s // tk - 1)
        def _():
            o_ref[...] = acc_ref[...].astype(BF16)

    return pl.pallas_call(
        body, grid=(s // tk,), in_specs=[_row_spec(tk, d)] + [_row_spec(tk, wd) for wd in widths],
        out_specs=pl.BlockSpec((d, sum(widths)), lambda k: (0, 0)),
        out_shape=jax.ShapeDtypeStruct((d, sum(widths)), BF16),
        scratch_shapes=[pltpu.VMEM((d, sum(widths)), F32)],
        compiler_params=_cparams(1, VMEM_LIMIT_MB), name=name)(h, *parts)


def _mesh_pos():
    return lax.axis_index("x"), lax.axis_index("y"), lax.axis_index("c")


class _GatherJob:
    n_phases = 3

    def __init__(self, shards):
        n = len(shards)
        self.inputs = list(shards)
        self.out_shape = [jax.ShapeDtypeStruct((N_DEV,) + a.shape, a.dtype) for a in shards]
        self.scratch = [pltpu.SemaphoreType.DMA((n, 7)), pltpu.SemaphoreType.DMA((n, 7)), pltpu.SemaphoreType.DMA((n,))]

    def phase(self, k, ins, outs, sems):
        send_sems, recv_sems, local_sems = sems
        n = len(ins)
        x, y, c = _mesh_pos()
        me, sibling = (x, y, c), (x, y, 1 - c)
        chips = [(1 - x, y), (x, 1 - y), (1 - x, 1 - y)]

        def copy(a, slot, block, to, src=None):
            dst = outs[a].at[4 * block[0] + 2 * block[1] + block[2]]
            return pltpu.make_async_remote_copy(
                src_ref=dst if src is None else src, dst_ref=dst,
                send_sem=send_sems.at[a, slot], recv_sem=recv_sems.at[a, slot], device_id=to, device_id_type=MESH)

        def mine():
            return [pltpu.make_async_copy(ins[a], outs[a].at[4 * x + 2 * y + c], local_sems.at[a]) for a in range(n)]

        def first():
            return [cp for a in range(n) for cp in
                    [copy(a, 0, me, sibling, src=ins[a])] + [copy(a, 1 + j, me, (*chip, c), src=ins[a]) for j, chip in enumerate(chips)]]

        def passed(a, j):
            return copy(a, 4 + j, (*chips[j], c), sibling)

        if k == 0:
            for cp in mine() + first():
                cp.start()
        elif k == 1:
            for a in range(n):
                for j, chip in enumerate(chips):
                    copy(a, 1 + j, (*chip, c), me).wait_recv()
                    passed(a, j).start()
        else:
            for a in range(n):
                copy(a, 0, sibling, me).wait_recv()
                for j, chip in enumerate(chips):
                    copy(a, 4 + j, (*chip, 1 - c), me).wait_recv()
            for cp in first() + [passed(a, j) for a in range(n) for j in range(3)]:
                cp.wait_send()
            for cp in mine():
                cp.wait()


class _SiblingJob:
    n_phases = 2

    def __init__(self, grads):
        n, half = len(grads), N_DEV // 2
        self.inputs = list(grads)
        self.out_shape = [jax.ShapeDtypeStruct((half,) + g.shape[1:], g.dtype) for g in grads]
        self.scratch = [pltpu.SemaphoreType.DMA((n, half)), pltpu.SemaphoreType.DMA((n, half))]

    def phase(self, k, ins, outs, sems):
        send_sems, recv_sems = sems
        x, y, c = _mesh_pos()
        copies = [pltpu.make_async_remote_copy(
            src_ref=ins[a].at[2 * j + (1 - c)], dst_ref=outs[a].at[j], send_sem=send_sems.at[a, j],
            recv_sem=recv_sems.at[a, j], device_id=(x, y, 1 - c), device_id_type=MESH)
            for a in range(len(ins)) for j in range(N_DEV // 2)]
        for cp in copies:
            if k == 0:
                cp.start()
            else:
                cp.wait()


class _ChipJob:
    n_phases = 2

    def __init__(self, partials):
        n, half = len(partials), N_DEV // 2
        self.inputs = list(partials)
        self.out_shape = [jax.ShapeDtypeStruct(p.shape, p.dtype) for p in partials]
        self.scratch = [pltpu.SemaphoreType.DMA((n, half)), pltpu.SemaphoreType.DMA((n, half)), pltpu.SemaphoreType.DMA((n,))]

    def phase(self, k, ins, outs, sems):
        send_sems, recv_sems, local_sems = sems
        n, half = len(ins), N_DEV // 2
        x, y, c = _mesh_pos()
        jme = 2 * x + y
        local = [pltpu.make_async_copy(ins[a].at[jme], outs[a].at[jme], local_sems.at[a]) for a in range(n)]
        for cp in local:
            if k == 0:
                cp.start()
            else:
                cp.wait()
        for j in range(half):
            @pl.when(jme != j)
            def _(j=j):
                for a in range(n):
                    if k == 0:
                        pltpu.make_async_remote_copy(
                            src_ref=ins[a].at[j], dst_ref=outs[a].at[jme], send_sem=send_sems.at[a, j],
                            recv_sem=recv_sems.at[a, jme], device_id=(j // 2, j % 2, c), device_id_type=MESH).start()
                    else:
                        arrive = pltpu.make_async_remote_copy(
                            src_ref=ins[a].at[j], dst_ref=outs[a].at[j], send_sem=send_sems.at[a, j],
                            recv_sem=recv_sems.at[a, j], device_id=(j // 2, j % 2, c), device_id_type=MESH)
                        arrive.wait_recv()
                        arrive.wait_send()


def _phase_steps(n_phases, n_steps):
    return [0, n_steps - 1] if n_phases == 2 else [0, n_steps // 2, n_steps - 1]


def _pallas_with_job(body, job, step_fn, n_steps, *, grid, in_specs, out_specs, out_shape, scratch_shapes=(),
                     compiler_params, name, args):
    in_specs, out_specs, out_shape, scratch_shapes = list(in_specs), list(out_specs), list(out_shape), list(scratch_shapes)
    if job is None:
        outs = pl.pallas_call(body, grid=grid, in_specs=in_specs, out_specs=out_specs, out_shape=out_shape,
                              scratch_shapes=scratch_shapes, compiler_params=compiler_params, name=name)(*args)
        return list(outs), None
    ni, no, ns = len(in_specs), len(out_specs), len(scratch_shapes)
    ji, jo = len(job.inputs), len(job.out_shape)
    steps = _phase_steps(job.n_phases, n_steps)

    def wrapped(*refs):
        own_in, job_in = refs[:ni], refs[ni:ni + ji]
        own_out, job_out = refs[ni + ji:ni + ji + no], refs[ni + ji + no:ni + ji + no + jo]
        own_scr, job_scr = refs[ni + ji + no + jo:ni + ji + no + jo + ns], refs[ni + ji + no + jo + ns:]
        step = step_fn()
        for k, at in enumerate(steps):
            @pl.when(step == at)
            def _(k=k):
                job.phase(k, job_in, job_out, job_scr)
        body(*own_in, *own_out, *own_scr)

    any_spec = pl.BlockSpec(memory_space=pl.ANY)
    outs = pl.pallas_call(
        wrapped, grid=grid, in_specs=in_specs + [any_spec] * ji, out_specs=out_specs + [any_spec] * jo,
        out_shape=out_shape + list(job.out_shape), scratch_shapes=scratch_shapes + list(job.scratch),
        compiler_params=compiler_params, name=name)(*args, *job.inputs)
    return list(outs[:no]), list(outs[no:])


def _run_job(job, *, name):
    ji, jo = len(job.inputs), len(job.out_shape)

    def body(*refs):
        for k in range(job.n_phases):
            job.phase(k, refs[:ji], refs[ji:ji + jo], refs[ji + jo:])

    any_spec = pl.BlockSpec(memory_space=pl.ANY)
    return pl.pallas_call(body, in_specs=[any_spec] * ji, out_specs=[any_spec] * jo, out_shape=list(job.out_shape),
                          scratch_shapes=list(job.scratch), name=name)(*job.inputs)


def _pair_add(grad, recv, cj_idx, *, name):
    half, r, cdim = recv.shape
    tr = _row_tile(r, 512)

    def body(cj_ref, g_ref, r_ref, own_ref, tb_ref):
        t = g_ref[...].astype(F32) + r_ref[...].astype(F32)
        tb_ref[...] = t.astype(BF16)

        @pl.when(pl.program_id(1) == cj_ref[1])
        def _():
            own_ref[...] = t

    grid_spec = pltpu.PrefetchScalarGridSpec(
        num_scalar_prefetch=1, grid=(r // tr, half),
        in_specs=[pl.BlockSpec((None, tr, cdim), lambda i, j, cj: (2 * j + cj[0], i, 0)),
                  pl.BlockSpec((None, tr, cdim), lambda i, j, cj: (j, i, 0))],
        out_specs=[pl.BlockSpec((tr, cdim), lambda i, j, cj: (i, 0)),
                   pl.BlockSpec((None, tr, cdim), lambda i, j, cj: (j, i, 0))])
    return pl.pallas_call(
        body, grid_spec=grid_spec,
        out_shape=[jax.ShapeDtypeStruct((r, cdim), F32), jax.ShapeDtypeStruct(recv.shape, BF16)],
        compiler_params=_cparams(2), name=name)(cj_idx, grad, recv)


def _adam_math(w, g, m, v):
    m = ADAM_B1 * m + (1.0 - ADAM_B1) * g
    v = ADAM_B2 * v + (1.0 - ADAM_B2) * (g * g)
    m_hat = m / (1.0 - ADAM_B1 ** ADAM_STEP)
    v_hat = v / (1.0 - ADAM_B2 ** ADAM_STEP)
    delta = -ADAM_LR * (m_hat / (jnp.sqrt(v_hat) + ADAM_EPS) + ADAM_WD * w)
    return delta, m, v


def _reduce_adam(own, recv, j_idx, w, m, v, *, name):
    half, r, cdim = recv.shape
    tr = _row_tile(r, 256)

    def body(j_ref, own_ref, recv_ref, w_ref, m_ref, v_ref, g_ref, d_ref, nm_ref, nv_ref):
        jme = j_ref[0]
        g = own_ref[...]
        for sl in range(half):
            g = g + jnp.where(sl == jme, 0.0, recv_ref[sl].astype(F32))
        delta, nm, nv = _adam_math(w_ref[...], g, m_ref[...], v_ref[...])
        g_ref[...] = g
        d_ref[...] = delta
        nm_ref[...] = nm
        nv_ref[...] = nv

    tile = pl.BlockSpec((tr, cdim), lambda i, j_ref: (i, 0))
    grid_spec = pltpu.PrefetchScalarGridSpec(
        num_scalar_prefetch=1, grid=(r // tr,),
        in_specs=[tile, pl.BlockSpec((half, tr, cdim), lambda i, j_ref: (0, i, 0)), tile, tile, tile],
        out_specs=[tile] * 4)
    return pl.pallas_call(
        body, grid_spec=grid_spec, out_shape=[jax.ShapeDtypeStruct((r, cdim), F32)] * 4,
        compiler_params=_cparams(1), name=name)(j_idx, own, recv, w, m, v)


class _SmallGatherJob:
    n_phases = 2

    def __init__(self, pack):
        self.inputs = [pack]
        self.out_shape = [jax.ShapeDtypeStruct((N_DEV,) + pack.shape, pack.dtype)]
        self.scratch = [pltpu.SemaphoreType.DMA((N_DEV,)), pltpu.SemaphoreType.DMA((N_DEV,)), pltpu.SemaphoreType.DMA(())]

    def phase(self, k, ins, outs, sems):
        (pack,), (land,), (send_sems, recv_sems, local_sem) = ins, outs, sems
        x, y, c = _mesh_pos()
        me = 4 * x + 2 * y + c
        mine = pltpu.make_async_copy(pack, land.at[me], local_sem)
        if k == 0:
            mine.start()
        else:
            mine.wait()
        for t in range(N_DEV):
            @pl.when(me != t)
            def _(t=t):
                peer = (t // 4, (t // 2) % 2, t % 2)
                if k == 0:
                    pltpu.make_async_remote_copy(
                        src_ref=pack, dst_ref=land.at[me], send_sem=send_sems.at[t], recv_sem=recv_sems.at[me],
                        device_id=peer, device_id_type=MESH).start()
                else:
                    cp = pltpu.make_async_remote_copy(
                        src_ref=pack, dst_ref=land.at[t], send_sem=send_sems.at[t], recv_sem=recv_sems.at[t],
                        device_id=peer, device_id_type=MESH)
                    cp.wait_recv()
                    cp.wait_send()


class _Jobs:
    def __init__(self, jobs):
        self.jobs = jobs
        self.n_phases = jobs[0].n_phases
        assert all(j.n_phases == self.n_phases for j in jobs)
        self.inputs = [a for j in jobs for a in j.inputs]
        self.out_shape = [o for j in jobs for o in j.out_shape]
        self.scratch = [sc for j in jobs for sc in j.scratch]

    def phase(self, k, ins, outs, sems):
        i = o = sc = 0
        for j in self.jobs:
            ni, no, nsc = len(j.inputs), len(j.out_shape), len(j.scratch)
            j.phase(k, ins[i:i + ni], outs[o:o + no], sems[sc:sc + nsc])
            i, o, sc = i + ni, o + no, sc + nsc


def _adam_small(w, land, m, v, *, name):
    def body(w_ref, land_ref, m_ref, v_ref, g_ref, d_ref, nm_ref, nv_ref):
        total = land_ref[0]
        for t in range(1, N_DEV):
            total = total + land_ref[t]
        g_ref[...] = total
        d_ref[...], nm_ref[...], nv_ref[...] = _adam_math(w_ref[...], total, m_ref[...], v_ref[...])

    vmem = pl.BlockSpec(memory_space=pltpu.VMEM)
    return pl.pallas_call(
        body, in_specs=[vmem] * 4, out_specs=[vmem] * 4, out_shape=[jax.ShapeDtypeStruct(w.shape, F32)] * 4,
        name=name)(w, land, m, v)


def _block_diag(w):
    h, a, b = w.shape
    eye = jnp.eye(h, dtype=w.dtype)
    return (eye[:, None, :, None] * w[:, :, None, :]).reshape(h * a, h * b)


def _diag_blocks(w, h):
    a = w.shape[0] // h
    return jnp.stack([w[i * a:(i + 1) * a, i * a:(i + 1) * a] for i in range(h)])


def _local_step(x, mem, target, p, hooks=None):
    g = {}
    p = dict(p)
    w_rg, w_ig = _block_diag(p["w_rgate"]).astype(BF16), _block_diag(p["w_igate"]).astype(BF16)

    (gu1, f1, xb, h1, h1b), arrived = _ffn_fwd(x, p["ffn1_w13"], p["ffn1_w2"], name="ffn1_fwd", ln=(p["ln1_g"], p["ln1_b"]),
                                               job=hooks and hooks.gather_job(_GATHER_MIXER))
    if hooks:
        p.update(hooks.gathered(_GATHER_MIXER, arrived))
    w_in = p["w_in"]
    w_in_lru, w_in_q, w_in_kv = w_in[:, :2 * LRU_W], w_in[:, 2 * LRU_W:2 * LRU_W + SB_W], w_in[:, 2 * LRU_W + SB_W:]
    xbg = _matmul(h1b, w_in_lru, name="proj_lru")
    q_sb = _matmul(h1b, w_in_q, out_dtype=BF16, scale=1.0 / math.sqrt(SB_HEAD), name="proj_q")
    kv_sb = _matmul(h1b, w_in_kv, out_dtype=BF16, name="proj_kv")
    xc, hseq, y_lru = _lru_fwd(xbg, p["conv_w"], p["conv_b"], w_rg, p["b_rgate"], w_ig, p["b_igate"],
                               p["lru_lambda"], name="lru_fwd")
    y_sb, arrived = _sb_fwd(q_sb, kv_sb, name="sb_fwd", job=hooks and hooks.gather_job(_GATHER_FFN2))
    if hooks:
        p.update(hooks.gathered(_GATHER_FFN2, arrived))
    ymix = _mixnorm_fwd(y_lru, y_sb, p["g_lru"], p["g_sb"], name="mixnorm_fwd")
    mix, h2, h2b = _matmul(ymix, p["w_out"], ln=(h1, 1.0, p["ln2_g"], p["ln2_b"]), name="mix_out_ln2")
    qm = _matmul(h2b, p["mem_wq"], out_dtype=BF16, name="cross_q")
    kv = _matmul(mem, p["mem_wkv"], name="cross_kv")
    o_cross = _cross_fwd(qm, kv, name="cross_fwd")
    cross, h3, h3b = _matmul(o_cross, p["mem_wo"], ln=(h2, 1.0, p["ln3_g"], p["ln3_b"]), name="cross_out_ln3")
    (gu2, f2, _), _ = _ffn_fwd(h3b, p["ffn2_w13"], p["ffn2_w2"], name="ffn2_fwd")

    dz4, dz4b, g["ln4_g"], g["ln4_b"], loss = _ln_bwd(h3, f2, 0.5, p["ln4_g"], p["ln4_b"], None, target=target, name="ln4_loss_bwd")
    (dh3, dgu2, act2), _ = _ffn_bwd(dz4, gu2, p["ffn2_w13"], p["ffn2_w2"], name="ffn2_bwd")
    g["ffn2_w13"] = _matmul(h3b, dgu2, trans_a=True, tm=1024, tn=FF_CHUNK, tk=2048, out_dtype=BF16, name="ffn2_dw13")
    g["ffn2_w2"] = _matmul(act2, dz4b, trans_a=True, tm=FF_CHUNK, tn=1024, tk=2048, scale=0.5, out_dtype=BF16, name="ffn2_dw2")
    dz3, dz3b, g["ln3_g"], g["ln3_b"] = _ln_bwd(h2, cross, 1.0, p["ln3_g"], p["ln3_b"], dh3, name="ln3_bwd")
    g["mem_wo"] = _matmul(o_cross, dz3b, trans_a=True, tm=1024, tn=1024, tk=1024, out_dtype=BF16, name="cross_dwo")
    do_cross = _matmul(dz3b, p["mem_wo"], trans_b=True, out_dtype=BF16, name="cross_do")
    dqm, dkv = _cross_bwd(qm, kv, do_cross, name="cross_bwd")
    g["mem_wq"] = _matmul(h2b, dqm, trans_a=True, tm=1024, tn=1024, tk=1024, out_dtype=BF16, name="cross_dwq")
    g["mem_wkv"] = _matmul(mem, dkv, trans_a=True, tm=1024, tn=MEM_HD, tk=256, out_dtype=BF16, name="cross_dwkv")
    dh2 = _matmul(dqm, p["mem_wq"], trans_b=True, res=dz3, res_coeff=ALPHA, name="cross_dh")
    dz2, dz2b, g["ln2_g"], g["ln2_b"] = _ln_bwd(h1, mix, 1.0, p["ln2_g"], p["ln2_b"], dh2, name="ln2_bwd")
    g["w_out"] = _matmul(ymix, dz2b, trans_a=True, tm=1024, tn=1024, tk=1024, out_dtype=BF16, name="mix_dwout")
    dymix = _matmul(dz2b, p["w_out"], trans_b=True, name="mix_dy")
    (dy_lru, dy_sb, g["g_lru"], g["g_sb"]), arrived = _mixnorm_bwd(
        y_lru, y_sb, p["g_lru"], p["g_sb"], dymix, name="mixnorm_bwd", job=hooks and hooks.sibling_job(_REDUCE_EARLY, g))
    (dq, dk, dv), arrived = _sb_bwd(q_sb, kv_sb, y_sb, dy_sb, name="sb_bwd",
                                    job=hooks and hooks.chip_job(_REDUCE_EARLY, arrived))
    if hooks:
        hooks.reduced(_REDUCE_EARLY, arrived)
    dxc, dgate, dwr, dwi, g["b_rgate"], g["b_igate"], g["lru_lambda"] = _lru_bwd(
        xc, xbg, hseq, dy_lru, w_rg, p["b_rgate"], w_ig, p["b_igate"], p["lru_lambda"], name="lru_bwd")
    g["w_rgate"], g["w_igate"] = _diag_blocks(dwr, 8), _diag_blocks(dwi, 8)
    dxb, g["conv_w"], g["conv_b"] = _conv_bwd(dxc, xbg, p["conv_w"], name="conv_bwd")
    parts = [dxb, dgate, dq, dk, dv]
    g["w_in"] = _proj_dw(h1b, parts, name="proj_dw")
    dh1, arrived = _proj_bwd_dx(parts, w_in, dz2, name="proj_dh", job=hooks and hooks.sibling_job(['w_in'], g))
    dz1, dz1b, g["ln1_g"], g["ln1_b"] = _ln_bwd(x, f1, 0.5, p["ln1_g"], p["ln1_b"], dh1, name="ln1_bwd")
    (grad_x, dgu1, act1), _ = _ffn_bwd(dz1, gu1, p["ffn1_w13"], p["ffn1_w2"], name="ffn1_bwd")
    dw13 = dict(trans_a=True, tm=1024, tn=FF_CHUNK, tk=2048, out_dtype=BF16, name="ffn1_dw13")
    dw2 = dict(trans_a=True, tm=FF_CHUNK, tn=1024, tk=2048, scale=0.5, out_dtype=BF16, name="ffn1_dw2")
    if not hooks:
        g["ffn1_w13"] = _matmul(xb, dgu1, **dw13)
        g["ffn1_w2"] = _matmul(act1, dz1b, **dw2)
        return loss, grad_x, g
    g["ffn1_w13"], arrived = _matmul(xb, dgu1, job=_Jobs([hooks.chip_job(['w_in'], arrived), hooks.small_job(g)]), **dw13)
    hooks.reduced(['w_in'], arrived[:1])
    hooks.small_land = arrived[1]
    arrived = _run_job(hooks.sibling_job(['ffn1_w13'], g), name="reduce_sibling_w13")
    g["ffn1_w2"], arrived = _matmul(act1, dz1b, job=hooks.chip_job(['ffn1_w13'], arrived), **dw2)
    hooks.reduced(['ffn1_w13'], arrived)
    arrived = _run_job(hooks.sibling_job(['ffn1_w2'], g), name="reduce_sibling_w2")
    hooks.reduced(['ffn1_w2'], _run_job(hooks.chip_job(['ffn1_w2'], arrived), name="reduce_chips_w2"))
    return loss, grad_x, g


_WEIGHTS = ['ffn1_w13', 'ffn1_w2', 'ln1_g', 'ln1_b', 'w_in', 'conv_w', 'conv_b', 'w_rgate', 'b_rgate', 'w_igate',
            'b_igate', 'lru_lambda', 'g_lru', 'g_sb', 'w_out', 'ln2_g', 'ln2_b', 'mem_wq', 'mem_wkv', 'mem_wo',
            'ln3_g', 'ln3_b', 'ffn2_w13', 'ffn2_w2', 'ln4_g', 'ln4_b']
_SHARDED = ['ffn1_w13', 'ffn1_w2', 'w_in', 'w_out', 'mem_wq', 'mem_wkv', 'mem_wo', 'ffn2_w13', 'ffn2_w2']
_SMALL = ['ln1_g', 'ln1_b', 'ln2_g', 'ln2_b', 'ln3_g', 'ln3_b', 'ln4_g', 'ln4_b', 'conv_b', 'b_rgate', 'b_igate',
          'lru_lambda', 'g_lru', 'g_sb', 'w_rgate', 'w_igate']


def _pack_small(d, conv_w_full=None):
    rows = [d[n].reshape(-1) for n in _SMALL]
    if conv_w_full is not None:
        rows.append(conv_w_full.reshape(-1))
    flat = jnp.concatenate(rows)
    pad = (-flat.shape[0]) % (8 * D_MODEL)
    return jnp.pad(flat, (0, pad)).reshape(-1, D_MODEL)


def _unpack_small(pack, like, with_conv):
    flat = pack.reshape(-1)
    out, off = {}, 0
    for n in _SMALL:
        size = math.prod(like[n].shape)
        out[n] = flat[off:off + size].reshape(like[n].shape)
        off += size
    conv = flat[off:off + CONV_W * LRU_W].reshape(CONV_W, LRU_W) if with_conv else None
    return out, conv


_GATHER_FIRST = ['ffn1_w13', 'ffn1_w2']
_GATHER_MIXER = ['w_in', 'conv_w', 'w_out', 'mem_wq', 'mem_wkv', 'mem_wo']
_GATHER_FFN2 = ['ffn2_w13', 'ffn2_w2']
_REDUCE_EARLY = ['ffn2_w13', 'ffn2_w2', 'mem_wo', 'mem_wq', 'mem_wkv', 'w_out']


def _weight_layout(n, full):
    if n in ('ffn1_w13', 'ffn2_w13', 'mem_wkv'):
        return full
    if n == 'w_in':
        return full.transpose(1, 0, 2).reshape(D_MODEL, -1)
    if n == 'conv_w':
        return full.transpose(1, 0, 2).reshape(CONV_W, LRU_W)
    return full.reshape(-1, full.shape[-1])


def _grad_blocks(n, g):
    if n == 'w_in':
        return g.reshape(D_MODEL, N_DEV, -1).transpose(1, 0, 2)
    return g.reshape((N_DEV, -1, g.shape[-1]))


class _Hooks:
    def __init__(self, w, cj_idx):
        self.w, self.cj_idx = w, cj_idx
        self.sums, self.recv = {}, {}

    def shard(self, n):
        return self.w[n][0] if n == 'conv_w' else self.w[n][0].astype(BF16)

    def gather_job(self, names):
        return _GatherJob([self.shard(n) for n in names])

    def gathered(self, names, outs):
        return {n: _weight_layout(n, o) for n, o in zip(names, outs)}

    def sibling_job(self, names, g):
        self.blocks = [_grad_blocks(n, g[n]) for n in names]
        return _SiblingJob(self.blocks)

    def chip_job(self, names, from_sibling):
        for n, b, r in zip(names, self.blocks, from_sibling):
            self.sums[n] = _pair_add(b, r, self.cj_idx, name=f"pair_add_{n}")
        return _ChipJob([self.sums[n][1] for n in names])

    def reduced(self, names, from_chips):
        self.recv.update(zip(names, from_chips))

    def small_job(self, g):
        return _SmallGatherJob(_pack_small(g, conv_w_full=g["conv_w"]))


def kernel(x, mem, ffn1_w13, ffn1_w2, ln1_g, ln1_b, w_in, conv_w, conv_b, w_rgate, b_rgate, w_igate, b_igate, lru_lambda, g_lru, g_sb, w_out, ln2_g, ln2_b, mem_wq, mem_wkv, mem_wo, ln3_g, ln3_b, ffn2_w13, ffn2_w2, ln4_g, ln4_b, loss_target, m_ffn1_w13, m_ffn1_w2, m_ln1_g, m_ln1_b, m_w_in, m_conv_w, m_conv_b, m_w_rgate, m_b_rgate, m_w_igate, m_b_igate, m_lru_lambda, m_g_lru, m_g_sb, m_w_out, m_ln2_g, m_ln2_b, m_mem_wq, m_mem_wkv, m_mem_wo, m_ln3_g, m_ln3_b, m_ffn2_w13, m_ffn2_w2, m_ln4_g, m_ln4_b, v_ffn1_w13, v_ffn1_w2, v_ln1_g, v_ln1_b, v_w_in, v_conv_w, v_conv_b, v_w_rgate, v_b_rgate, v_w_igate, v_b_igate, v_lru_lambda, v_g_lru, v_g_sb, v_w_out, v_ln2_g, v_ln2_b, v_mem_wq, v_mem_wkv, v_mem_wo, v_ln3_g, v_ln3_b, v_ffn2_w13, v_ffn2_w2, v_ln4_g, v_ln4_b):
    args = locals()
    w = {n: args[n] for n in _WEIGHTS}
    mom = {n: args["m_" + n] for n in _WEIGHTS}
    var = {n: args["v_" + n] for n in _WEIGHTS}
    ix, iy, ic = lax.axis_index("x"), lax.axis_index("y"), lax.axis_index("c")

    j_idx = jnp.reshape(2 * ix + iy, (1,)).astype(jnp.int32)
    hooks = _Hooks(w, jnp.stack([ic, 2 * ix + iy]).astype(jnp.int32))
    p = {n: (w[n][0] if w[n].ndim == 4 else w[n]) for n in _WEIGHTS if n not in _SHARDED and n != 'conv_w'}
    p.update(hooks.gathered(_GATHER_FIRST, _run_job(hooks.gather_job(_GATHER_FIRST), name="gather_ffn1")))
    loss_local, grad_x, g = _local_step(x[0], mem[0], loss_target[0], p, hooks)
    loss = lax.psum(loss_local[0, 0], ("x", "y", "c"))

    grads, delta, new_m, new_v = {}, {}, {}, {}
    for n in _SHARDED:
        grads[n], delta[n], new_m[n], new_v[n] = (
            o.reshape(w[n].shape) for o in _reduce_adam(hooks.sums[n][0], hooks.recv[n], j_idx, w[n][0], mom[n][0],
                                                        var[n][0], name=f"adam_{n}"))

    me = 4 * ix + 2 * iy + ic
    own_cols = lambda full: lax.dynamic_slice_in_dim(full, me * (LRU_W // N_DEV), LRU_W // N_DEV, axis=1)
    land = hooks.small_land
    conv_off = sum(math.prod(w[n].shape) for n in _SMALL)
    land_conv = land.reshape(N_DEV, -1)[:, conv_off:conv_off + CONV_W * LRU_W].reshape(N_DEV, CONV_W, LRU_W)
    land_own = jnp.pad(own_cols(land_conv.reshape(N_DEV * CONV_W, LRU_W)).reshape(N_DEV, -1),
                       ((0, 0), (0, CONV_W * LRU_W - CONV_W * LRU_W // N_DEV)))
    land = lax.dynamic_update_slice(land.reshape(N_DEV, -1), land_own, (0, conv_off)).reshape(land.shape)
    pk = lambda d: _pack_small({n: d[n] for n in _SMALL}, conv_w_full=jnp.pad(d["conv_w"].reshape(-1), (0, CONV_W * LRU_W - CONV_W * LRU_W // N_DEV)))
    g_s, d_s, m_s, v_s = _adam_small(pk(w), land, pk(mom), pk(var), name="adam_small")
    for src, dst in ((g_s, grads), (d_s, delta), (m_s, new_m), (v_s, new_v)):
        vals, conv = _unpack_small(src, w, with_conv=True)
        dst.update(vals)
        dst["conv_w"] = conv.reshape(-1)[:CONV_W * LRU_W // N_DEV].reshape(w["conv_w"].shape)

    return (loss, grad_x[None], *[grads[n] for n in _WEIGHTS], *[delta[n] for n in _WEIGHTS],
            *[new_m[n] for n in _WEIGHTS], *[new_v[n] for n in _WEIGHTS])
```

```python
import functools
import math

import jax
import jax.numpy as jnp
from jax import lax
from jax.experimental import pallas as pl
from jax.experimental.pallas import tpu as pltpu

F32, BF16 = jnp.float32, jnp.bfloat16

D_MODEL = 1024
LRU_W = 512
SB_W = 512
SB_PAIR = 128
SB_HEAD = 64
SB_ROWS = 128
SB_SKIP = -110.0
D_FF = 2688
FF_CHUNK = 672
N_DEV = 8
MEM_HEADS = 4
MEM_HD = 256
CONV_W = 4
ALPHA = 2.0 ** 0.25
LN_EPS = 1e-5
RMS_EPS = 1e-6
LRU_C = 8.0
ADAM_LR, ADAM_B1, ADAM_B2, ADAM_EPS, ADAM_WD, ADAM_STEP = 0.001, 0.9, 0.999, 1e-08, 0.01, 10

NN = (((1,), (0,)), ((), ()))
NT = (((1,), (1,)), ((), ()))
MESH = pl.DeviceIdType.MESH
VMEM_LIMIT_MB = 56


def _dot(a, b, dn=NN):
    return lax.dot_general(a, b, dn, preferred_element_type=F32)


def _cparams(n_axes, vmem_mb=None):
    kw = dict(dimension_semantics=("arbitrary",) * n_axes)
    if vmem_mb is not None:
        kw["vmem_limit_bytes"] = vmem_mb << 20
    return pltpu.CompilerParams(**kw)


def _row_tile(rows, want):
    if rows <= want:
        return rows
    t = want - want % 8
    while rows % t:
        t -= 8
    return t


def _matmul(a, b, *, name, out_dtype=F32, trans_a=False, trans_b=False, tm=512, tn=1024, tk=1024,
            scale=1.0, res=None, res_coeff=1.0, job=None, ln=None, ln_bwd=None):
    a_chunked, b_chunked = a.ndim == 3, b.ndim == 3
    nc = a.shape[0] if a_chunked else (b.shape[0] if b_chunked else 1)
    a2, b2 = a.shape[-2:], b.shape[-2:]
    (kdim, m) = a2 if trans_a else a2[::-1]
    n = b2[0] if trans_b else b2[1]
    tm, tn, tk = min(tm, m), min(tn, n), min(tk, kdim)
    assert m % tm == 0 and n % tn == 0 and kdim % tk == 0, (name, m, n, kdim)
    nk = kdim // tk

    def a_idx(c, i, j, k):
        idx = (k, i) if trans_a else (i, k)
        return (c,) + idx if a_chunked else idx

    def b_idx(c, i, j, k):
        idx = (j, k) if trans_b else (k, j)
        return (c,) + idx if b_chunked else idx

    a_blk = (tk, tm) if trans_a else (tm, tk)
    b_blk = (tn, tk) if trans_b else (tk, tn)
    in_specs = [pl.BlockSpec(((None,) + a_blk) if a_chunked else a_blk, a_idx),
                pl.BlockSpec(((None,) + b_blk) if b_chunked else b_blk, b_idx)]
    args = [a, b]
    if res is not None:
        in_specs.append(pl.BlockSpec((None, tm, tn), lambda c, i, j, k: (c, i, j)))
        args.append(res.reshape((nc, m, n)))
    if ln is not None:
        assert nc == 1 and tn == n and res is None
        in_specs += [pl.BlockSpec((tm, tn), lambda c, i, j, k: (i, 0)), pl.BlockSpec((1, tn), lambda c, i, j, k: (0, 0)),
                     pl.BlockSpec((1, tn), lambda c, i, j, k: (0, 0))]
        args += [ln[0], ln[2], ln[3]]
    if ln_bwd is not None:
        assert nc == 1 and tn == n and res is not None and ln is None
        in_specs += [pl.BlockSpec((tm, tn), lambda c, i, j, k: (i, 0))] * 2 + [pl.BlockSpec((1, tn), lambda c, i, j, k: (0, 0))] * 2
        args += [ln_bwd[0], ln_bwd[1], ln_bwd[3], ln_bwd[4]]

    def body(*refs):
        if ln_bwd is not None:
            a_ref, b_ref, r_ref, x_ref, f_ref, g_ref, bb_ref, o_ref, dzb_ref, dg_ref, db_ref, acc_ref = refs
        elif res is not None:
            a_ref, b_ref, r_ref, o_ref, acc_ref = refs
        elif ln is not None:
            a_ref, b_ref, x_ref, g_ref, bb_ref, o_ref, h_ref, hb_ref, acc_ref = refs
        else:
            a_ref, b_ref, o_ref, acc_ref = refs
        k = pl.program_id(3)

        @pl.when(k == 0)
        def _():
            acc_ref[...] = jnp.zeros_like(acc_ref)

        av = a_ref[...]
        if trans_a:
            av = av.astype(F32).T
        acc_ref[...] += _dot(av.astype(BF16), b_ref[...].astype(BF16), NT if trans_b else NN)

        @pl.when(k == nk - 1)
        def _():
            out = acc_ref[...]
            if scale != 1.0:
                out = out * scale
            if res is not None:
                out = out + res_coeff * r_ref[...]
            if ln_bwd is not None:
                dz, dg, db = _ln_bwd_tile(x_ref[...], f_ref[...], ln_bwd[2], g_ref[...], bb_ref[...], out)
                _ln_bwd_store(pl.program_id(1) == 0, dz, dg, db, o_ref, dzb_ref, dg_ref, db_ref)
                return
            o_ref[...] = out.astype(out_dtype)
            if ln is not None:
                h = _ln_math(ALPHA * x_ref[...] + ln[1] * out, g_ref[...], bb_ref[...])
                h_ref[...] = h
                hb_ref[...] = h.astype(BF16)

    grid = (nc, m // tm, n // tn, nk)

    def step():
        idx = pl.program_id(0)
        for ax in range(1, 4):
            idx = idx * grid[ax] + pl.program_id(ax)
        return idx

    out_specs = [pl.BlockSpec((None, tm, tn), lambda c, i, j, k: (c, i, j))]
    out_shape = [jax.ShapeDtypeStruct((nc, m, n), out_dtype)]
    if ln is not None:
        out_specs += [pl.BlockSpec((tm, tn), lambda c, i, j, k: (i, 0))] * 2
        out_shape += [jax.ShapeDtypeStruct((m, n), F32), jax.ShapeDtypeStruct((m, n), BF16)]
    if ln_bwd is not None:
        out_specs += [pl.BlockSpec((tm, tn), lambda c, i, j, k: (i, 0))] + [pl.BlockSpec((1, tn), lambda c, i, j, k: (0, 0))] * 2
        out_shape += _ln_bwd_shapes(m, n)[1:]
    outs, job_out = _pallas_with_job(
        body, job, step, math.prod(grid), grid=grid, in_specs=in_specs, out_specs=out_specs, out_shape=out_shape,
        scratch_shapes=[pltpu.VMEM((tm, tn), F32)],
        compiler_params=_cparams(4, VMEM_LIMIT_MB), name=name, args=args)
    out = outs[0] if (a_chunked or b_chunked) else outs[0][0]
    if ln is not None:
        return out, outs[1], outs[2]
    if ln_bwd is not None:
        return out, outs[1], outs[2], outs[3]
    return out if job is None else (out, job_out)


def _ln_math(z, g, b):
    mu = jnp.mean(z, axis=-1, keepdims=True)
    zc = z - mu
    var = jnp.mean(zc * zc, axis=-1, keepdims=True)
    return zc * lax.rsqrt(var + LN_EPS) * g + b


def _row_spec(tm, d):
    return pl.BlockSpec((tm, d), lambda i: (i, 0))


def _par_spec(d, rows=1):
    return pl.BlockSpec((rows, d), lambda i: (0, 0))


def _ln_bwd_tile(x, f, fscale, g, b, dy):
    _, vjp = jax.vjp(_ln_math, ALPHA * x + fscale * f, g, b)
    return vjp(dy)


def _ln_bwd_specs(tm, d):
    return ([_row_spec(tm, d), _row_spec(tm, d), _par_spec(d), _par_spec(d)],
            [_row_spec(tm, d), _row_spec(tm, d), _par_spec(d), _par_spec(d)])


def _ln_bwd_shapes(s, d):
    return [jax.ShapeDtypeStruct((s, d), F32), jax.ShapeDtypeStruct((s, d), BF16),
            jax.ShapeDtypeStruct((1, d), F32), jax.ShapeDtypeStruct((1, d), F32)]


def _ln_bwd_store(first, dz, dg, db, dz_ref, dzb_ref, dg_ref, db_ref):
    @pl.when(first)
    def _():
        dg_ref[...] = jnp.zeros_like(dg_ref)
        db_ref[...] = jnp.zeros_like(db_ref)

    dz_ref[...] = dz
    dzb_ref[...] = dz.astype(BF16)
    dg_ref[...] += dg
    db_ref[...] += db


def _load_weights_once(pairs, sem):
    @pl.when(pl.program_id(0) == 0)
    def _():
        copies = [pltpu.make_async_copy(src, dst, sem.at[n]) for n, (src, dst) in enumerate(pairs)]
        for c in copies:
            c.start()
        for c in copies:
            c.wait()


def _ffn_fwd(h, w13c, w2, *, name, job=None, ln=None, loss_head=None):
    s, d = h.shape
    tm = min(256, s)
    half = N_DEV // 2

    def body(*refs):
        if loss_head is not None:
            (h_ref, w13_hbm, w2_hbm, x_ref, t_ref, g_ref, b_ref, gu_ref, dz_ref, dzb_ref, dg_ref, db_ref, loss_ref,
             w13_v, w2_v, sem) = refs
        elif ln is not None:
            h_ref, w13_hbm, w2_hbm, g_ref, b_ref, gu_ref, f_ref, hb_ref, n_ref, nb_ref, w13_v, w2_v, sem = refs
        else:
            h_ref, w13_hbm, w2_hbm, gu_ref, f_ref, hb_ref, w13_v, w2_v, sem = refs
        _load_weights_once([(w13_hbm, w13_v), (w2_hbm, w2_v)], sem)
        hb = h_ref[...].astype(BF16)
        if loss_head is None:
            hb_ref[...] = hb
        acc = jnp.zeros((tm, d), F32)
        for k in range(half):
            g = _dot(hb, w13_v[k])
            u = _dot(hb, w13_v[k + half])
            gu_ref[k] = g.astype(BF16)
            gu_ref[k + half] = u.astype(BF16)
            a = g * jax.nn.sigmoid(g) * u
            acc = acc + _dot(a.astype(BF16), w2_v[pl.ds(k * FF_CHUNK, FF_CHUNK), :])
        if loss_head is not None:
            first = pl.program_id(0) == 0

            @pl.when(first)
            def _():
                loss_ref[...] = jnp.zeros_like(loss_ref)

            y, vjp = jax.vjp(_ln_math, ALPHA * x_ref[...] + 0.5 * acc, g_ref[...], b_ref[...])
            err = y - t_ref[...]
            sq = jnp.sum(err * err, axis=1, keepdims=True)
            loss_ref[...] += jnp.sum(sq, axis=0, keepdims=True) * (0.5 / d)
            dz, dg, db = vjp(err * (1.0 / d))
            _ln_bwd_store(first, dz, dg, db, dz_ref, dzb_ref, dg_ref, db_ref)
            return
        f_ref[...] = acc
        if ln is not None:
            hn = _ln_math(ALPHA * h_ref[...] + 0.5 * acc, g_ref[...], b_ref[...])
            n_ref[...] = hn
            nb_ref[...] = hn.astype(BF16)

    any_spec = pl.BlockSpec(memory_space=pl.ANY)
    gu_spec = pl.BlockSpec((N_DEV, tm, FF_CHUNK), lambda i: (0, i, 0))
    gu_shape = jax.ShapeDtypeStruct((N_DEV, s, FF_CHUNK), BF16)
    scratch = [pltpu.VMEM(w13c.shape, BF16), pltpu.VMEM(w2.shape, BF16), pltpu.SemaphoreType.DMA((2,))]
    if loss_head is not None:
        return _pallas_with_job(
            body, job, lambda: pl.program_id(0), s // tm, grid=(s // tm,),
            in_specs=[_row_spec(tm, d), any_spec, any_spec, _row_spec(tm, d), _row_spec(tm, d), _par_spec(d), _par_spec(d)],
            out_specs=[gu_spec] + _ln_bwd_specs(tm, d)[1] + [pl.BlockSpec((1, 1), lambda i: (0, 0))],
            out_shape=[gu_shape] + _ln_bwd_shapes(s, d) + [jax.ShapeDtypeStruct((1, 1), F32)],
            scratch_shapes=scratch, compiler_params=_cparams(1, VMEM_LIMIT_MB), name=name,
            args=(h, w13c, w2) + tuple(loss_head))
    n_ln = 0 if ln is None else 2
    return _pallas_with_job(
        body, job, lambda: pl.program_id(0), s // tm,
        grid=(s // tm,), in_specs=[_row_spec(tm, d), any_spec, any_spec] + [_par_spec(d)] * n_ln,
        out_specs=[pl.BlockSpec((N_DEV, tm, FF_CHUNK), lambda i: (0, i, 0)), _row_spec(tm, d), _row_spec(tm, d)]
                  + [_row_spec(tm, d)] * n_ln,
        out_shape=[jax.ShapeDtypeStruct((N_DEV, s, FF_CHUNK), BF16), jax.ShapeDtypeStruct((s, d), F32),
                   jax.ShapeDtypeStruct((s, d), BF16)]
                  + [jax.ShapeDtypeStruct((s, d), F32), jax.ShapeDtypeStruct((s, d), BF16)][:n_ln],
        scratch_shapes=[pltpu.VMEM(w13c.shape, BF16), pltpu.VMEM(w2.shape, BF16), pltpu.SemaphoreType.DMA((2,))],
        compiler_params=_cparams(1, VMEM_LIMIT_MB), name=name, args=(h, w13c, w2) + (() if ln is None else tuple(ln)))


def _ffn_bwd(dz, gu, w13c, w2, *, name, job=None, ln_bwd=None):
    s, d = dz.shape
    tm = min(256, s)
    half = N_DEV // 2

    def body(*refs):
        if ln_bwd is not None:
            (dz_ref, gu_ref, w13_hbm, w2_hbm, x_ref, f_ref, g_ref, b_ref, dzp_ref, dzpb_ref, dgu_ref, act_ref,
             dg_ref, db_ref, w13_v, w2_v, sem) = refs
        else:
            dz_ref, gu_ref, w13_hbm, w2_hbm, dh_ref, dgu_ref, act_ref, w13_v, w2_v, sem = refs
        _load_weights_once([(w13_hbm, w13_v), (w2_hbm, w2_v)], sem)
        dzv = dz_ref[...]
        dfb = (0.5 * dzv).astype(BF16)
        acc = ALPHA * dzv
        for k in range(half):
            g = gu_ref[k].astype(F32)
            u = gu_ref[k + half].astype(F32)
            da = _dot(dfb, w2_v[pl.ds(k * FF_CHUNK, FF_CHUNK), :], NT)
            sg = jax.nn.sigmoid(g)
            silu = g * sg
            dg = (da * u * (sg * (1.0 + g * (1.0 - sg)))).astype(BF16)
            du = (da * silu).astype(BF16)
            act_ref[k] = (silu * u).astype(BF16)
            dgu_ref[k] = dg
            dgu_ref[k + half] = du
            acc = acc + _dot(dg, w13_v[k], NT) + _dot(du, w13_v[k + half], NT)
        if ln_bwd is not None:
            dzp, dg, db = _ln_bwd_tile(x_ref[...], f_ref[...], ln_bwd[2], g_ref[...], b_ref[...], acc)
            _ln_bwd_store(pl.program_id(0) == 0, dzp, dg, db, dzp_ref, dzpb_ref, dg_ref, db_ref)
        else:
            dh_ref[...] = acc

    any_spec = pl.BlockSpec(memory_space=pl.ANY)
    in_specs = [_row_spec(tm, d), pl.BlockSpec((N_DEV, tm, FF_CHUNK), lambda i: (0, i, 0)), any_spec, any_spec]
    mid_specs = [pl.BlockSpec((N_DEV, tm, FF_CHUNK), lambda i: (0, i, 0)), pl.BlockSpec((half, tm, FF_CHUNK), lambda i: (0, i, 0))]
    mid_shapes = [jax.ShapeDtypeStruct((N_DEV, s, FF_CHUNK), BF16), jax.ShapeDtypeStruct((half, s, FF_CHUNK), BF16)]
    args = (dz, gu, w13c, w2)
    if ln_bwd is not None:
        ln_in, ln_out = _ln_bwd_specs(tm, d)
        in_specs, out_specs = in_specs + ln_in, ln_out[:2] + mid_specs + ln_out[2:]
        shapes = _ln_bwd_shapes(s, d)
        out_shape = shapes[:2] + mid_shapes + shapes[2:]
        args += (ln_bwd[0], ln_bwd[1], ln_bwd[3], ln_bwd[4])
    else:
        out_specs, out_shape = [_row_spec(tm, d)] + mid_specs, [jax.ShapeDtypeStruct((s, d), F32)] + mid_shapes
    return _pallas_with_job(
        body, job, lambda: pl.program_id(0), s // tm, grid=(s // tm,), in_specs=in_specs, out_specs=out_specs,
        out_shape=out_shape,
        scratch_shapes=[pltpu.VMEM(w13c.shape, BF16), pltpu.VMEM(w2.shape, BF16), pltpu.SemaphoreType.DMA((2,))],
        compiler_params=_cparams(1, VMEM_LIMIT_MB), name=name, args=args)


def _shift_down(x, prev8, d):
    t, c = x.shape
    row = lax.broadcasted_iota(jnp.int32, (t, c), 0)
    xr = pltpu.roll(x, d, 0)
    pr = pltpu.roll(prev8, d, 0)
    if t > 8:
        pr = jnp.concatenate([pr, jnp.zeros((t - 8, c), x.dtype)], axis=0)
    return jnp.where(row < d, pr, xr)


def _shift_up(x, next8, d):
    t, c = x.shape
    row = lax.broadcasted_iota(jnp.int32, (t, c), 0)
    xr = pltpu.roll(x, t - d, 0)
    nr = pltpu.roll(next8, 8 - d, 0)
    if t > 8:
        nr = jnp.concatenate([jnp.zeros((t - 8, c), x.dtype), nr], axis=0)
    return jnp.where(row >= t - d, nr, xr)


def _scan_fwd(a, u, carry):
    t = a.shape[0]
    row = lax.broadcasted_iota(jnp.int32, a.shape, 0) % 8
    for d in (1, 2, 4):
        a_s, u_s = pltpu.roll(a, d, 0), pltpu.roll(u, d, 0)
        m = row >= d
        u = jnp.where(m, u + a * u_s, u)
        a = jnp.where(m, a * a_s, a)
    out = []
    for g in range(t // 8):
        hg = u[8 * g:8 * g + 8, :] + a[8 * g:8 * g + 8, :] * carry
        out.append(hg)
        carry = hg[7:8, :]
    return jnp.concatenate(out, axis=0)


def _scan_bwd(b, x, carry):
    t = b.shape[0]
    row = lax.broadcasted_iota(jnp.int32, b.shape, 0) % 8
    for d in (1, 2, 4):
        b_s, x_s = pltpu.roll(b, t - d, 0), pltpu.roll(x, t - d, 0)
        m = row < 8 - d
        x = jnp.where(m, x + b * x_s, x)
        b = jnp.where(m, b * b_s, b)
    out = [None] * (t // 8)
    for g in reversed(range(t // 8)):
        lg = x[8 * g:8 * g + 8, :] + b[8 * g:8 * g + 8, :] * carry
        out[g] = lg
        carry = lg[0:1, :]
    return jnp.concatenate(out, axis=0)


def _lru_elem(xc, pr, pi, b_r, b_i, lam):
    r = jax.nn.sigmoid(pr + b_r)
    ig = jax.nn.sigmoid(pi + b_i)
    softplus_neg_lam = jnp.maximum(-lam, 0.0) + jnp.log1p(jnp.exp(-jnp.abs(lam)))
    log_a = (-LRU_C) * r * softplus_neg_lam
    a = jnp.exp(log_a)
    one_minus_a2 = -jnp.tanh(log_a) * (jnp.exp(2.0 * log_a) + 1.0)
    u = jnp.sqrt(one_minus_a2) * (ig * xc)
    return a, u


def _gelu_tanh(x):
    return 0.5 * x * (1.0 + jnp.tanh(math.sqrt(2.0 / math.pi) * (x + 0.044715 * (x * x * x))))


def _conv_fwd(xb, prev8, w, b):
    out = b + w[3:4, :] * xb
    for d in range(1, CONV_W):
        out = out + w[3 - d:4 - d, :] * _shift_down(xb, prev8, d)
    return out


def _lru_fwd(xbg, conv_w, conv_b, w_rg, b_rg, w_ig, b_ig, lam, *, name):
    s = xbg.shape[0]
    c = LRU_W
    t = min(512, s)
    t8 = t // 8

    def body(xb_ref, gate_ref, prev_ref, cw_ref, cb_ref, wr_ref, br_ref, wi_ref, bi_ref, lam_ref,
             xc_ref, h_ref, y_ref, carry_ref):
        i = pl.program_id(0)

        @pl.when(i == 0)
        def _():
            carry_ref[...] = jnp.zeros_like(carry_ref)

        prev8 = jnp.where(i > 0, prev_ref[...], 0.0)
        xc = _conv_fwd(xb_ref[...], prev8, cw_ref[...], cb_ref[...])
        xcb = xc.astype(BF16)
        a, u = _lru_elem(xc, _dot(xcb, wr_ref[...]), _dot(xcb, wi_ref[...]), br_ref[...], bi_ref[...], lam_ref[...])
        h = _scan_fwd(a, u, carry_ref[0:1, :])
        carry_ref[...] = jnp.broadcast_to(h[t - 1:t, :], carry_ref.shape)
        xc_ref[...] = xc
        h_ref[...] = h
        y_ref[...] = h * _gelu_tanh(gate_ref[...])

    tile = lambda col: pl.BlockSpec((t, c), lambda i: (i, col))
    par = lambda rows: pl.BlockSpec((rows, c), lambda i: (0, 0))
    out_spec = pl.BlockSpec((t, c), lambda i: (i, 0))
    return pl.pallas_call(
        body, grid=(s // t,),
        in_specs=[tile(0), tile(1), pl.BlockSpec((8, c), lambda i: (jnp.maximum(i * t8 - 1, 0), 0)),
                  par(CONV_W), par(1), pl.BlockSpec((c, c), lambda i: (0, 0)), par(1),
                  pl.BlockSpec((c, c), lambda i: (0, 0)), par(1), par(1)],
        out_specs=[out_spec, out_spec, out_spec],
        out_shape=[jax.ShapeDtypeStruct((s, c), F32)] * 3,
        scratch_shapes=[pltpu.VMEM((8, c), F32)],
        compiler_params=_cparams(1, VMEM_LIMIT_MB), name=name)(
            xbg, xbg, xbg, conv_w, conv_b, w_rg, b_rg, w_ig, b_ig, lam)


def _lru_bwd(xc, xbg, hseq, dy, w_rg, b_rg, w_ig, b_ig, lam, *, name):
    s, c = xc.shape
    t = min(512, s)
    t8 = t // 8
    nt = s // t

    def body(xc_ref, gate_ref, h_ref, hprev_ref, dy_ref, wr_ref, br_ref, wi_ref, bi_ref, lam_ref,
             dxc_ref, dgate_ref, dwr_ref, dwi_ref, dbr_ref, dbi_ref, dlam_ref, a8_ref, l8_ref):
        i = pl.program_id(0)

        @pl.when(i == 0)
        def _():
            for r in (dwr_ref, dwi_ref, dbr_ref, dbi_ref, dlam_ref, a8_ref, l8_ref):
                r[...] = jnp.zeros_like(r)

        xc_v = xc_ref[...]
        xcb = xc_v.astype(BF16)
        (a, _), vjp = jax.vjp(_lru_elem, xc_v, _dot(xcb, wr_ref[...]), _dot(xcb, wi_ref[...]),
                              br_ref[...], bi_ref[...], lam_ref[...])
        gl, vjp_gelu = jax.vjp(_gelu_tanh, gate_ref[...])
        dyv, hs = dy_ref[...], h_ref[...]
        (dgate,) = vjp_gelu(dyv * hs)
        b_next = _shift_up(a, a8_ref[...], 1)
        lt = _scan_bwd(b_next, dyv * gl, l8_ref[0:1, :])
        a8_ref[...] = a[0:8, :]
        l8_ref[...] = lt[0:8, :]
        hprev8 = jnp.where(i < nt - 1, hprev_ref[...], 0.0)
        da = lt * _shift_down(hs, hprev8, 1)
        dxc_e, dpr, dpi, dbr, dbi, dlam = vjp((da, lt))
        dprb, dpib = dpr.astype(BF16), dpi.astype(BF16)
        dxc_ref[...] = dxc_e + _dot(dprb, wr_ref[...], NT) + _dot(dpib, wi_ref[...], NT)
        dgate_ref[...] = dgate.astype(BF16)
        xct = xc_v.T.astype(BF16)
        dwr_ref[...] += _dot(xct, dprb)
        dwi_ref[...] += _dot(xct, dpib)
        dbr_ref[...] += dbr
        dbi_ref[...] += dbi
        dlam_ref[...] += dlam

    rev = lambda col: pl.BlockSpec((t, c), lambda i: (nt - 1 - i, col))
    par = pl.BlockSpec((1, c), lambda i: (0, 0))
    sq = pl.BlockSpec((c, c), lambda i: (0, 0))
    return pl.pallas_call(
        body, grid=(nt,),
        in_specs=[rev(0), rev(1), rev(0), pl.BlockSpec((8, c), lambda i: (jnp.maximum((nt - 1 - i) * t8 - 1, 0), 0)),
                  rev(0), sq, par, sq, par, par],
        out_specs=[rev(0), rev(0), sq, sq, par, par, par],
        out_shape=[jax.ShapeDtypeStruct((s, c), F32), jax.ShapeDtypeStruct((s, c), BF16)]
                  + [jax.ShapeDtypeStruct((c, c), F32)] * 2 + [jax.ShapeDtypeStruct((1, c), F32)] * 3,
        scratch_shapes=[pltpu.VMEM((8, c), F32), pltpu.VMEM((8, c), F32)],
        compiler_params=_cparams(1, VMEM_LIMIT_MB), name=name)(
            xc, xbg, hseq, hseq, dy, w_rg, b_rg, w_ig, b_ig, lam)


def _conv_bwd(dxc, xbg, conv_w, *, name):
    s, c = dxc.shape
    t = min(512, s)
    t8 = t // 8
    nt = s // t

    def body(dxc_ref, next_ref, xb_ref, prev_ref, w_ref, dxb_ref, dw_ref, db_ref):
        i = pl.program_id(0)

        @pl.when(i == 0)
        def _():
            dw_ref[...] = jnp.zeros_like(dw_ref)
            db_ref[...] = jnp.zeros_like(db_ref)

        g = dxc_ref[...]
        xb = xb_ref[...]
        w = w_ref[...]
        next8 = jnp.where(i < nt - 1, next_ref[...], 0.0)
        prev8 = jnp.where(i > 0, prev_ref[...], 0.0)
        dxb = w[3:4, :] * g
        dw_ref[3:4, :] += jnp.sum(g * xb, axis=0, keepdims=True)
        for d in range(1, CONV_W):
            dxb = dxb + w[3 - d:4 - d, :] * _shift_up(g, next8, d)
            dw_ref[3 - d:4 - d, :] += jnp.sum(g * _shift_down(xb, prev8, d), axis=0, keepdims=True)
        dxb_ref[...] = dxb.astype(BF16)
        db_ref[...] += jnp.sum(g, axis=0, keepdims=True)

    tile = pl.BlockSpec((t, c), lambda i: (i, 0))
    return pl.pallas_call(
        body, grid=(nt,),
        in_specs=[tile, pl.BlockSpec((8, c), lambda i: (jnp.minimum((i + 1) * t8, s // 8 - 1), 0)),
                  tile, pl.BlockSpec((8, c), lambda i: (jnp.maximum(i * t8 - 1, 0), 0)),
                  pl.BlockSpec((CONV_W, c), lambda i: (0, 0))],
        out_specs=[tile, pl.BlockSpec((CONV_W, c), lambda i: (0, 0)), pl.BlockSpec((1, c), lambda i: (0, 0))],
        out_shape=[jax.ShapeDtypeStruct((s, c), BF16), jax.ShapeDtypeStruct((CONV_W, c), F32),
                   jax.ShapeDtypeStruct((1, c), F32)],
        compiler_params=_cparams(1), name=name)(dxc, dxc, xbg, xbg, conv_w)


def _hi_lo_dot(x, tri):
    hi = x.astype(BF16)
    lo = (x - hi.astype(F32)).astype(BF16)
    return _dot(hi, tri) + _dot(lo, tri)


def _tri(tk, inclusive):
    r = lax.broadcasted_iota(jnp.int32, (tk, tk), 0)
    c = lax.broadcasted_iota(jnp.int32, (tk, tk), 1)
    return ((r >= c) if inclusive else (r > c)).astype(BF16)


def _split_heads(x):
    first = lax.broadcasted_iota(jnp.int32, x.shape, 1) < SB_HEAD
    zero = jnp.zeros_like(x)
    return first, (jnp.where(first, x, zero), jnp.where(first, zero, x))


def _sb_softplus_terms(z):
    ls = jnp.minimum(z, 0.0) - jnp.log(1.0 + jnp.exp(-jnp.abs(z)))
    return ls, ls - z


def _sb_alive(runs):
    m = runs[0]
    for r in runs[1:]:
        m = jnp.maximum(m, r)
    return jnp.max(m) > SB_SKIP


def _sb_fwd(q, kv, *, name, job=None):
    s = q.shape[0]
    tb = min(256, s)
    rc = min(SB_ROWS, tb)
    npair = SB_W // SB_PAIR
    chains = [(h, c) for h in range(2) for c in range(tb // rc)]

    def body(q_ref, k_ref, v_ref, o_ref):
        i = pl.program_id(1)
        first, qs = _split_heads(q_ref[...])
        tri = _tri(tb, False)
        causal = lax.broadcasted_iota(jnp.int32, (tb, tb), 1) < lax.broadcasted_iota(jnp.int32, (tb, tb), 0)
        causal_c = [causal[c * rc:(c + 1) * rc, :] for c in range(tb // rc)]

        def group(j_lo, width, carry, diag_last):
            k0 = pl.multiple_of(j_lo * tb, tb)
            kb = k_ref[pl.ds(k0, width * tb), :]
            vb = v_ref[pl.ds(k0, width * tb), :]
            zw = [_dot(qs[h][c * rc:(c + 1) * rc, :], kb, NT) for h, c in chains]
            cols = [slice(t * tb, (t + 1) * tb) for t in range(width)]
            terms = [[_sb_softplus_terms(z[:, cl]) for cl in cols] for z in zw]
            lss = [[ls for ls, _ in tc] for tc in terms]
            ns = [[n for _, n in tc] for tc in terms]
            if diag_last:
                ns = [nc[:-1] + [jnp.where(causal_c[c], nc[-1], 0.0)] for nc, (h, c) in zip(ns, chains)]
            sufs = [[_hi_lo_dot(n, tri) for n in nc] for nc in ns]
            out = []
            for (h, c), lc, nc, sc, (run, acc) in zip(chains, lss, ns, sufs, carry):
                ws = [None] * width
                for t in reversed(range(width)):
                    w = jnp.exp(lc[t] + (sc[t] + run))
                    if diag_last and t == width - 1:
                        w = jnp.where(causal_c[c], w, 0.0)
                    ws[t] = w.astype(BF16)
                    run = run + jnp.sum(nc[t], axis=1, keepdims=True)
                out.append((run, acc + _dot(ws[0] if width == 1 else jnp.concatenate(ws, axis=1), vb)))
            return tuple(out)

        zero = (jnp.zeros((rc, 1), F32), jnp.zeros((rc, SB_PAIR), F32))
        carry = lax.cond(i == 0, lambda c: group(0, 1, c, True), lambda c: group(i - 1, 2, c, True),
                         (zero,) * len(chains))

        def step(st):
            j, _, c = st
            c = group(j, 1, c, False)
            return j - 1, _sb_alive([run for run, _ in c]), c

        _, _, carry = lax.while_loop(lambda st: jnp.logical_and(st[0] >= 0, st[1]), step,
                                     (i - 2, _sb_alive([run for run, _ in carry]), carry))
        accs = [jnp.concatenate([carry[n][1] for n, (h, c) in enumerate(chains) if h == hh], axis=0) for hh in range(2)]
        o_ref[...] = jnp.where(first, accs[0], accs[1])

    (o,), job_out = _pallas_with_job(
        body, job, lambda: pl.program_id(0) * (s // tb) + pl.program_id(1), npair * (s // tb),
        grid=(npair, s // tb),
        in_specs=[pl.BlockSpec((tb, SB_PAIR), lambda p, i: (i, p)),
                  pl.BlockSpec((s, SB_PAIR), lambda p, i: (0, p)),
                  pl.BlockSpec((s, SB_PAIR), lambda p, i: (0, npair + p))],
        out_specs=[pl.BlockSpec((tb, SB_PAIR), lambda p, i: (i, p))],
        out_shape=[jax.ShapeDtypeStruct((s, SB_W), F32)],
        compiler_params=_cparams(2, VMEM_LIMIT_MB), name=name, args=(q, kv, kv))
    return o, job_out


def _sb_bwd(q, kv, o, do, *, name, job=None):
    s = q.shape[0]
    tb = min(256, s)
    rc = min(SB_ROWS, tb)
    npair = SB_W // SB_PAIR
    chains = [(h, c) for h in range(2) for c in range(tb // rc)]

    def body(q_ref, k_ref, v_ref, o_ref, do_ref, dq_ref, dkb_ref, dvb_ref, dk_ref, dv_ref):
        i = pl.program_id(1)

        @pl.when(i == 0)
        def _():
            dk_ref[...] = jnp.zeros_like(dk_ref)
            dv_ref[...] = jnp.zeros_like(dv_ref)

        first, qs = _split_heads(q_ref[...])
        dob = do_ref[...].astype(BF16)
        _, dos = _split_heads(dob)
        prod = dob.astype(F32) * o_ref[...]
        deltas = (jnp.sum(jnp.where(first, prod, 0.0), axis=1, keepdims=True),
                  jnp.sum(jnp.where(first, 0.0, prod), axis=1, keepdims=True))
        tri_x, tri_i = _tri(tb, False), _tri(tb, True)
        causal = lax.broadcasted_iota(jnp.int32, (tb, tb), 1) < lax.broadcasted_iota(jnp.int32, (tb, tb), 0)
        causal_c = [causal[c * rc:(c + 1) * rc, :] for c in range(tb // rc)]
        rows = [slice(c * rc, (c + 1) * rc) for c in range(tb // rc)]

        def group(j_lo, width, carry, diag_last):
            k0 = pl.multiple_of(j_lo * tb, tb)
            kb = k_ref[pl.ds(k0, width * tb), :]
            vb = v_ref[pl.ds(k0, width * tb), :]
            cols = [slice(t * tb, (t + 1) * tb) for t in range(width)]
            zw = [_dot(qs[h][rows[c], :], kb, NT) for h, c in chains]
            dww = [_dot(dos[h][rows[c], :], vb, NT) for h, c in chains]
            terms = [[_sb_softplus_terms(z[:, cl]) for cl in cols] for z in zw]
            lss = [[ls for ls, _ in tc] for tc in terms]
            ns = [[n for _, n in tc] for tc in terms]
            if diag_last:
                ns = [nc[:-1] + [jnp.where(causal_c[c], nc[-1], 0.0)] for nc, (h, c) in zip(ns, chains)]
            sufs = [[_hi_lo_dot(n, tri_x) for n in nc] for nc in ns]
            wbs, gs, runs_n = [], [], []
            for (h, c), lc, nc, sc, dw, (run_n, _, _) in zip(chains, lss, ns, sufs, dww, carry):
                wb, g = [None] * width, [None] * width
                for t in reversed(range(width)):
                    w = jnp.exp(lc[t] + (sc[t] + run_n))
                    if diag_last and t == width - 1:
                        w = jnp.where(causal_c[c], w, 0.0)
                    wb[t] = w.astype(BF16)
                    g[t] = wb[t].astype(F32) * dw[:, cols[t]]
                    run_n = run_n + jnp.sum(nc[t], axis=1, keepdims=True)
                wbs.append(wb)
                gs.append(g)
                runs_n.append(run_n)
            gsufs = [[_hi_lo_dot(g, tri_i) for g in gc] for gc in gs]
            out = []
            dk_t = [jnp.zeros((tb, SB_PAIR), F32) for _ in range(width)]
            dv_t = [jnp.zeros((tb, SB_PAIR), F32) for _ in range(width)]
            for (h, c), lc, gc, gsc, wb, run_n, (_, run_g, dq_acc) in zip(chains, lss, gs, gsufs, wbs, runs_n, carry):
                qh, doh, delta = qs[h][rows[c], :], dos[h][rows[c], :], deltas[h][rows[c], :]
                dzb = [None] * width
                for t in reversed(range(width)):
                    pre = delta - (gsc[t] + run_g)
                    dz = gc[t] - jnp.exp(lc[t]) * (gc[t] + pre)
                    if diag_last and t == width - 1:
                        dz = jnp.where(causal_c[c], dz, 0.0)
                    run_g = run_g + jnp.sum(gc[t], axis=1, keepdims=True)
                    dzb[t] = dz.astype(BF16)
                    dk_t[t] = dk_t[t] + _dot(dz.T.astype(BF16), qh)
                    dv_t[t] = dv_t[t] + _dot(wb[t].astype(F32).T.astype(BF16), doh)
                out.append((run_n, run_g, dq_acc + _dot(dzb[0] if width == 1 else jnp.concatenate(dzb, axis=1), kb)))
            for t in range(width):
                dk_ref[pl.ds(pl.multiple_of((j_lo + t) * tb, tb), tb), :] += dk_t[t]
                dv_ref[pl.ds(pl.multiple_of((j_lo + t) * tb, tb), tb), :] += dv_t[t]
            return tuple(out)

        zero = (jnp.zeros((rc, 1), F32), jnp.zeros((rc, 1), F32), jnp.zeros((rc, SB_PAIR), F32))
        carry = lax.cond(i == 0, lambda c: group(0, 1, c, True), lambda c: group(i - 1, 2, c, True),
                         (zero,) * len(chains))

        def step(st):
            j, _, c = st
            c = group(j, 1, c, False)
            return j - 1, _sb_alive([r[0] for r in c]), c

        _, _, carry = lax.while_loop(lambda st: jnp.logical_and(st[0] >= 0, st[1]), step,
                                     (i - 2, _sb_alive([r[0] for r in carry]), carry))
        dqs = [jnp.concatenate([carry[n][2] for n, (h, c) in enumerate(chains) if h == hh], axis=0) for hh in range(2)]
        dq_ref[...] = (jnp.where(first, dqs[0], dqs[1]) * (1.0 / math.sqrt(SB_HEAD))).astype(BF16)

        @pl.when(i == s // tb - 1)
        def _():
            dkb_ref[...] = dk_ref[...].astype(BF16)
            dvb_ref[...] = dv_ref[...].astype(BF16)

    qtile = pl.BlockSpec((tb, SB_PAIR), lambda p, i: (i, p))
    col = pl.BlockSpec((s, SB_PAIR), lambda p, i: (0, p))
    return _pallas_with_job(
        body, job, lambda: pl.program_id(0) * (s // tb) + pl.program_id(1), npair * (s // tb),
        grid=(npair, s // tb),
        in_specs=[qtile, col, pl.BlockSpec((s, SB_PAIR), lambda p, i: (0, npair + p)), qtile, qtile],
        out_specs=[qtile, col, col],
        out_shape=[jax.ShapeDtypeStruct((s, SB_W), BF16)] * 3,
        scratch_shapes=[pltpu.VMEM((s, SB_PAIR), F32)] * 2,
        compiler_params=_cparams(2, VMEM_LIMIT_MB), name=name, args=(q, kv, kv, o, do))


def _mixnorm_math(yl, ys, gl, gs):
    def rms(x, g):
        return x * lax.rsqrt(jnp.mean(x * x, axis=-1, keepdims=True) + RMS_EPS) * g
    return rms(yl, gl), rms(ys, gs)


def _mixnorm_fwd(yl, ys, gl, gs, *, name):
    s, c = yl.shape
    tm = min(512, s)

    def body(yl_ref, ys_ref, gl_ref, gs_ref, y_ref):
        a, b = _mixnorm_math(yl_ref[...], ys_ref[...], gl_ref[...], gs_ref[...])
        y_ref[:, 0:c] = a.astype(BF16)
        y_ref[:, c:2 * c] = b.astype(BF16)

    return pl.pallas_call(
        body, grid=(s // tm,), in_specs=[_row_spec(tm, c), _row_spec(tm, c), _par_spec(c), _par_spec(c)],
        out_specs=_row_spec(tm, 2 * c), out_shape=jax.ShapeDtypeStruct((s, 2 * c), BF16),
        compiler_params=_cparams(1), name=name)(yl, ys, gl, gs)


def _mixnorm_bwd(yl, ys, gl, gs, dy, *, name, job=None):
    s, c = yl.shape
    tm = min(512, s)

    def body(yl_ref, ys_ref, gl_ref, gs_ref, dy_ref, dyl_ref, dys_ref, dgl_ref, dgs_ref):
        @pl.when(pl.program_id(0) == 0)
        def _():
            dgl_ref[...] = jnp.zeros_like(dgl_ref)
            dgs_ref[...] = jnp.zeros_like(dgs_ref)

        _, vjp = jax.vjp(_mixnorm_math, yl_ref[...], ys_ref[...], gl_ref[...], gs_ref[...])
        dyl, dys, dgl, dgs = vjp((dy_ref[:, 0:c], dy_ref[:, c:2 * c]))
        dyl_ref[...] = dyl
        dys_ref[...] = dys
        dgl_ref[...] += dgl
        dgs_ref[...] += dgs

    return _pallas_with_job(
        body, job, lambda: pl.program_id(0), s // tm, grid=(s // tm,),
        in_specs=[_row_spec(tm, c), _row_spec(tm, c), _par_spec(c), _par_spec(c), _row_spec(tm, 2 * c)],
        out_specs=[_row_spec(tm, c), _row_spec(tm, c), _par_spec(c), _par_spec(c)],
        out_shape=[jax.ShapeDtypeStruct((s, c), F32)] * 2 + [jax.ShapeDtypeStruct((1, c), F32)] * 2,
        compiler_params=_cparams(1), name=name, args=(yl, ys, gl, gs, dy))


def _cross_probs(qh, kh):
    sc = _dot(qh, kh, NT) * (1.0 / math.sqrt(MEM_HD))
    e = jnp.exp(sc - jnp.max(sc, axis=-1, keepdims=True))
    return e / jnp.sum(e, axis=-1, keepdims=True)


def _cross_fwd(q, kv, *, name):
    s, d = q.shape
    mlen = kv.shape[1]
    tm = min(512, s)

    def body(q_ref, kv_ref, o_ref):
        for h in range(MEM_HEADS):
            cols = slice(h * MEM_HD, (h + 1) * MEM_HD)
            p = _cross_probs(q_ref[:, cols].astype(BF16), kv_ref[h].astype(BF16))
            o_ref[:, cols] = _dot(p.astype(BF16), kv_ref[MEM_HEADS + h].astype(BF16)).astype(BF16)

    return pl.pallas_call(
        body, grid=(s // tm,),
        in_specs=[_row_spec(tm, d), pl.BlockSpec((2 * MEM_HEADS, mlen, MEM_HD), lambda i: (0, 0, 0))],
        out_specs=_row_spec(tm, d), out_shape=jax.ShapeDtypeStruct((s, d), BF16),
        compiler_params=_cparams(1, VMEM_LIMIT_MB), name=name)(q, kv)


def _cross_bwd(q, kv, do, *, name):
    s, d = q.shape
    mlen = kv.shape[1]
    tm = min(512, s)

    def body(q_ref, kv_ref, do_ref, dq_ref, dkv_ref):
        @pl.when(pl.program_id(0) == 0)
        def _():
            dkv_ref[...] = jnp.zeros_like(dkv_ref)

        for h in range(MEM_HEADS):
            cols = slice(h * MEM_HD, (h + 1) * MEM_HD)
            qh = q_ref[:, cols].astype(BF16)
            kh = kv_ref[h].astype(BF16)
            doh = do_ref[:, cols].astype(BF16)
            p = _cross_probs(qh, kh)
            dp = _dot(doh, kv_ref[MEM_HEADS + h].astype(BF16), NT)
            ds = p * (dp - jnp.sum(dp * p, axis=-1, keepdims=True)) * (1.0 / math.sqrt(MEM_HD))
            dq_ref[:, cols] = _dot(ds.astype(BF16), kh).astype(BF16)
            dkv_ref[h] += _dot(ds.T.astype(BF16), qh)
            dkv_ref[MEM_HEADS + h] += _dot(p.T.astype(BF16), doh)

    kv_spec = pl.BlockSpec((2 * MEM_HEADS, mlen, MEM_HD), lambda i: (0, 0, 0))
    return pl.pallas_call(
        body, grid=(s // tm,), in_specs=[_row_spec(tm, d), kv_spec, _row_spec(tm, d)],
        out_specs=[_row_spec(tm, d), kv_spec],
        out_shape=[jax.ShapeDtypeStruct((s, d), BF16), jax.ShapeDtypeStruct(kv.shape, F32)],
        compiler_params=_cparams(1, VMEM_LIMIT_MB), name=name)(q, kv, do)


def _proj_bwd_dx(parts, w_in, dz, *, name, job=None, ln_bwd=None):
    s = dz.shape[0]
    d = D_MODEL
    tm = min(256, s)
    widths = [p.shape[1] for p in parts]

    n_parts = len(parts)

    def body(*refs):
        part_refs, w_ref, dz_ref = refs[:n_parts], refs[n_parts], refs[n_parts + 1]
        acc = ALPHA * dz_ref[...]
        off = 0
        for p_ref, wd in zip(part_refs, widths):
            acc = acc + _dot(p_ref[...].astype(BF16), w_ref[:, off:off + wd], NT)
            off += wd
        if ln_bwd is not None:
            x_ref, f_ref, g_ref, b_ref = refs[n_parts + 2:n_parts + 6]
            dzp, dg, db = _ln_bwd_tile(x_ref[...], f_ref[...], ln_bwd[2], g_ref[...], b_ref[...], acc)
            _ln_bwd_store(pl.program_id(0) == 0, dzp, dg, db, *refs[n_parts + 6:])
        else:
            refs[-1][...] = acc

    in_specs = [_row_spec(tm, wd) for wd in widths] + [pl.BlockSpec(w_in.shape, lambda i: (0, 0)), _row_spec(tm, d)]
    args = (*parts, w_in, dz)
    if ln_bwd is not None:
        ln_in, out_specs = _ln_bwd_specs(tm, d)
        in_specs, out_shape = in_specs + ln_in, _ln_bwd_shapes(s, d)
        args += (ln_bwd[0], ln_bwd[1], ln_bwd[3], ln_bwd[4])
    else:
        out_specs, out_shape = [_row_spec(tm, d)], [jax.ShapeDtypeStruct((s, d), F32)]
    outs, job_out = _pallas_with_job(
        body, job, lambda: pl.program_id(0), s // tm, grid=(s // tm,), in_specs=in_specs, out_specs=out_specs,
        out_shape=out_shape, compiler_params=_cparams(1, VMEM_LIMIT_MB), name=name, args=args)
    return (outs[0] if ln_bwd is None else outs), job_out


def _proj_dw(h, parts, *, name):
    s, d = h.shape
    tk = min(1024, s)
    widths = [p.shape[1] for p in parts]

    def body(*refs):
        h_ref, part_refs, o_ref, acc_ref = refs[0], refs[1:-2], refs[-2], refs[-1]
        k = pl.program_id(0)

        @pl.when(k == 0)
        def _():
            acc_ref[...] = jnp.zeros_like(acc_ref)

        ht = h_ref[...].astype(F32).T.astype(BF16)
        off = 0
        for p_ref, wd in zip(part_refs, widths):
            acc_ref[:, off:off + wd] += _dot(ht, p_ref[...].astype(BF16))
            off += wd

        @pl.when(k == s // tk - 1)
        def _():
            o_ref[...] = acc_ref[...].astype(BF16)

    return pl.pallas_call(
        body, grid=(s // tk,), in_specs=[_row_spec(tk, d)] + [_row_spec(tk, wd) for wd in widths],
        out_specs=pl.BlockSpec((d, sum(widths)), lambda k: (0, 0)),
        out_shape=jax.ShapeDtypeStruct((d, sum(widths)), BF16),
        scratch_shapes=[pltpu.VMEM((d, sum(widths)), F32)],
        compiler_params=_cparams(1, VMEM_LIMIT_MB), name=name)(h, *parts)


def _mesh_pos():
    return lax.axis_index("x"), lax.axis_index("y"), lax.axis_index("c")


class _GatherJob:
    n_phases = 3

    def __init__(self, shards):
        n = len(shards)
        self.inputs = list(shards)
        self.out_shape = [jax.ShapeDtypeStruct((N_DEV,) + a.shape, a.dtype) for a in shards]
        self.scratch = [pltpu.SemaphoreType.DMA((n, 7)), pltpu.SemaphoreType.DMA((n, 7)), pltpu.SemaphoreType.DMA((n,))]

    def phase(self, k, ins, outs, sems):
        send_sems, recv_sems, local_sems = sems
        n = len(ins)
        x, y, c = _mesh_pos()
        me, sibling = (x, y, c), (x, y, 1 - c)
        chips = [(1 - x, y), (x, 1 - y), (1 - x, 1 - y)]

        def copy(a, slot, block, to, src=None):
            dst = outs[a].at[4 * block[0] + 2 * block[1] + block[2]]
            return pltpu.make_async_remote_copy(
                src_ref=dst if src is None else src, dst_ref=dst,
                send_sem=send_sems.at[a, slot], recv_sem=recv_sems.at[a, slot], device_id=to, device_id_type=MESH)

        def mine():
            return [pltpu.make_async_copy(ins[a], outs[a].at[4 * x + 2 * y + c], local_sems.at[a]) for a in range(n)]

        def first():
            return [cp for a in range(n) for cp in
                    [copy(a, 0, me, sibling, src=ins[a])] + [copy(a, 1 + j, me, (*chip, c), src=ins[a]) for j, chip in enumerate(chips)]]

        def passed(a, j):
            return copy(a, 4 + j, (*chips[j], c), sibling)

        if k == 0:
            for cp in mine() + first():
                cp.start()
        elif k == 1:
            for a in range(n):
                for j, chip in enumerate(chips):
                    copy(a, 1 + j, (*chip, c), me).wait_recv()
                    passed(a, j).start()
        else:
            for a in range(n):
                copy(a, 0, sibling, me).wait_recv()
                for j, chip in enumerate(chips):
                    copy(a, 4 + j, (*chip, 1 - c), me).wait_recv()
            for cp in first() + [passed(a, j) for a in range(n) for j in range(3)]:
                cp.wait_send()
            for cp in mine():
                cp.wait()


class _SiblingJob:
    n_phases = 2

    def __init__(self, grads):
        n, half = len(grads), N_DEV // 2
        self.inputs = list(grads)
        self.out_shape = [jax.ShapeDtypeStruct((half,) + g.shape[1:], g.dtype) for g in grads]
        self.scratch = [pltpu.SemaphoreType.DMA((n, half)), pltpu.SemaphoreType.DMA((n, half))]

    def phase(self, k, ins, outs, sems):
        send_sems, recv_sems = sems
        x, y, c = _mesh_pos()
        copies = [pltpu.make_async_remote_copy(
            src_ref=ins[a].at[2 * j + (1 - c)], dst_ref=outs[a].at[j], send_sem=send_sems.at[a, j],
            recv_sem=recv_sems.at[a, j], device_id=(x, y, 1 - c), device_id_type=MESH)
            for a in range(len(ins)) for j in range(N_DEV // 2)]
        for cp in copies:
            if k == 0:
                cp.start()
            else:
                cp.wait()


class _ChipJob:
    n_phases = 2

    def __init__(self, partials):
        n, half = len(partials), N_DEV // 2
        self.inputs = list(partials)
        self.out_shape = [jax.ShapeDtypeStruct(p.shape, p.dtype) for p in partials]
        self.scratch = [pltpu.SemaphoreType.DMA((n, half)), pltpu.SemaphoreType.DMA((n, half)), pltpu.SemaphoreType.DMA((n,))]

    def phase(self, k, ins, outs, sems):
        send_sems, recv_sems, local_sems = sems
        n, half = len(ins), N_DEV // 2
        x, y, c = _mesh_pos()
        jme = 2 * x + y
        local = [pltpu.make_async_copy(ins[a].at[jme], outs[a].at[jme], local_sems.at[a]) for a in range(n)]
        for cp in local:
            if k == 0:
                cp.start()
            else:
                cp.wait()
        for j in range(half):
            @pl.when(jme != j)
            def _(j=j):
                for a in range(n):
                    if k == 0:
                        pltpu.make_async_remote_copy(
                            src_ref=ins[a].at[j], dst_ref=outs[a].at[jme], send_sem=send_sems.at[a, j],
                            recv_sem=recv_sems.at[a, jme], device_id=(j // 2, j % 2, c), device_id_type=MESH).start()
                    else:
                        arrive = pltpu.make_async_remote_copy(
                            src_ref=ins[a].at[j], dst_ref=outs[a].at[j], send_sem=send_sems.at[a, j],
                            recv_sem=recv_sems.at[a, j], device_id=(j // 2, j % 2, c), device_id_type=MESH)
                        arrive.wait_recv()
                        arrive.wait_send()


def _phase_steps(n_phases, n_steps):
    return [0, n_steps - 1] if n_phases == 2 else [0, n_steps // 2, n_steps - 1]


def _pallas_with_job(body, job, step_fn, n_steps, *, grid, in_specs, out_specs, out_shape, scratch_shapes=(),
                     compiler_params, name, args):
    in_specs, out_specs, out_shape, scratch_shapes = list(in_specs), list(out_specs), list(out_shape), list(scratch_shapes)
    if job is None:
        outs = pl.pallas_call(body, grid=grid, in_specs=in_specs, out_specs=out_specs, out_shape=out_shape,
                              scratch_shapes=scratch_shapes, compiler_params=compiler_params, name=name)(*args)
        return list(outs), None
    ni, no, ns = len(in_specs), len(out_specs), len(scratch_shapes)
    ji, jo = len(job.inputs), len(job.out_shape)
    steps = _phase_steps(job.n_phases, n_steps)

    def wrapped(*refs):
        own_in, job_in = refs[:ni], refs[ni:ni + ji]
        own_out, job_out = refs[ni + ji:ni + ji + no], refs[ni + ji + no:ni + ji + no + jo]
        own_scr, job_scr = refs[ni + ji + no + jo:ni + ji + no + jo + ns], refs[ni + ji + no + jo + ns:]
        step = step_fn()
        for k, at in enumerate(steps):
            @pl.when(step == at)
            def _(k=k):
                job.phase(k, job_in, job_out, job_scr)
        body(*own_in, *own_out, *own_scr)

    any_spec = pl.BlockSpec(memory_space=pl.ANY)
    outs = pl.pallas_call(
        wrapped, grid=grid, in_specs=in_specs + [any_spec] * ji, out_specs=out_specs + [any_spec] * jo,
        out_shape=out_shape + list(job.out_shape), scratch_shapes=scratch_shapes + list(job.scratch),
        compiler_params=compiler_params, name=name)(*args, *job.inputs)
    return list(outs[:no]), list(outs[no:])


def _run_job(job, *, name):
    ji, jo = len(job.inputs), len(job.out_shape)

    def body(*refs):
        for k in range(job.n_phases):
            job.phase(k, refs[:ji], refs[ji:ji + jo], refs[ji + jo:])

    any_spec = pl.BlockSpec(memory_space=pl.ANY)
    return pl.pallas_call(body, in_specs=[any_spec] * ji, out_specs=[any_spec] * jo, out_shape=list(job.out_shape),
                          scratch_shapes=list(job.scratch), name=name)(*job.inputs)


def _pair_add(grad, recv, cj_idx, *, name):
    half, r, cdim = recv.shape
    tr = _row_tile(r, 512)

    def body(cj_ref, g_ref, r_ref, own_ref, tb_ref):
        t = g_ref[...].astype(F32) + r_ref[...].astype(F32)
        tb_ref[...] = t.astype(BF16)

        @pl.when(pl.program_id(1) == cj_ref[1])
        def _():
            own_ref[...] = t

    grid_spec = pltpu.PrefetchScalarGridSpec(
        num_scalar_prefetch=1, grid=(r // tr, half),
        in_specs=[pl.BlockSpec((None, tr, cdim), lambda i, j, cj: (2 * j + cj[0], i, 0)),
                  pl.BlockSpec((None, tr, cdim), lambda i, j, cj: (j, i, 0))],
        out_specs=[pl.BlockSpec((tr, cdim), lambda i, j, cj: (i, 0)),
                   pl.BlockSpec((None, tr, cdim), lambda i, j, cj: (j, i, 0))])
    return pl.pallas_call(
        body, grid_spec=grid_spec,
        out_shape=[jax.ShapeDtypeStruct((r, cdim), F32), jax.ShapeDtypeStruct(recv.shape, BF16)],
        compiler_params=_cparams(2), name=name)(cj_idx, grad, recv)


def _adam_math(w, g, m, v):
    m = ADAM_B1 * m + (1.0 - ADAM_B1) * g
    v = ADAM_B2 * v + (1.0 - ADAM_B2) * (g * g)
    m_hat = m / (1.0 - ADAM_B1 ** ADAM_STEP)
    v_hat = v / (1.0 - ADAM_B2 ** ADAM_STEP)
    delta = -ADAM_LR * (m_hat / (jnp.sqrt(v_hat) + ADAM_EPS) + ADAM_WD * w)
    return delta, m, v


def _reduce_adam(own, recv, j_idx, w, m, v, *, name):
    half, r, cdim = recv.shape
    tr = _row_tile(r, 256)

    def body(j_ref, own_ref, recv_ref, w_ref, m_ref, v_ref, g_ref, d_ref, nm_ref, nv_ref):
        jme = j_ref[0]
        g = own_ref[...]
        for sl in range(half):
            g = g + jnp.where(sl == jme, 0.0, recv_ref[sl].astype(F32))
        delta, nm, nv = _adam_math(w_ref[...], g, m_ref[...], v_ref[...])
        g_ref[...] = g
        d_ref[...] = delta
        nm_ref[...] = nm
        nv_ref[...] = nv

    tile = pl.BlockSpec((tr, cdim), lambda i, j_ref: (i, 0))
    grid_spec = pltpu.PrefetchScalarGridSpec(
        num_scalar_prefetch=1, grid=(r // tr,),
        in_specs=[tile, pl.BlockSpec((half, tr, cdim), lambda i, j_ref: (0, i, 0)), tile, tile, tile],
        out_specs=[tile] * 4)
    return pl.pallas_call(
        body, grid_spec=grid_spec, out_shape=[jax.ShapeDtypeStruct((r, cdim), F32)] * 4,
        compiler_params=_cparams(1), name=name)(j_idx, own, recv, w, m, v)


class _SmallGatherJob:
    n_phases = 2

    def __init__(self, pack):
        self.inputs = [pack]
        self.out_shape = [jax.ShapeDtypeStruct((N_DEV,) + pack.shape, pack.dtype)]
        self.scratch = [pltpu.SemaphoreType.DMA((N_DEV,)), pltpu.SemaphoreType.DMA((N_DEV,)), pltpu.SemaphoreType.DMA(())]

    def phase(self, k, ins, outs, sems):
        (pack,), (land,), (send_sems, recv_sems, local_sem) = ins, outs, sems
        x, y, c = _mesh_pos()
        me = 4 * x + 2 * y + c
        mine = pltpu.make_async_copy(pack, land.at[me], local_sem)
        if k == 0:
            mine.start()
        else:
            mine.wait()
        for t in range(N_DEV):
            @pl.when(me != t)
            def _(t=t):
                peer = (t // 4, (t // 2) % 2, t % 2)
                if k == 0:
                    pltpu.make_async_remote_copy(
                        src_ref=pack, dst_ref=land.at[me], send_sem=send_sems.at[t], recv_sem=recv_sems.at[me],
                        device_id=peer, device_id_type=MESH).start()
                else:
                    cp = pltpu.make_async_remote_copy(
                        src_ref=pack, dst_ref=land.at[t], send_sem=send_sems.at[t], recv_sem=recv_sems.at[t],
                        device_id=peer, device_id_type=MESH)
                    cp.wait_recv()
                    cp.wait_send()


class _Jobs:
    def __init__(self, jobs):
        self.jobs = jobs
        self.n_phases = jobs[0].n_phases
        assert all(j.n_phases == self.n_phases for j in jobs)
        self.inputs = [a for j in jobs for a in j.inputs]
        self.out_shape = [o for j in jobs for o in j.out_shape]
        self.scratch = [sc for j in jobs for sc in j.scratch]

    def phase(self, k, ins, outs, sems):
        i = o = sc = 0
        for j in self.jobs:
            ni, no, nsc = len(j.inputs), len(j.out_shape), len(j.scratch)
            j.phase(k, ins[i:i + ni], outs[o:o + no], sems[sc:sc + nsc])
            i, o, sc = i + ni, o + no, sc + nsc


def _adam_small(w, land, m, v, *, name):
    def body(w_ref, land_ref, m_ref, v_ref, g_ref, d_ref, nm_ref, nv_ref):
        total = land_ref[0]
        for t in range(1, N_DEV):
            total = total + land_ref[t]
        g_ref[...] = total
        d_ref[...], nm_ref[...], nv_ref[...] = _adam_math(w_ref[...], total, m_ref[...], v_ref[...])

    vmem = pl.BlockSpec(memory_space=pltpu.VMEM)
    return pl.pallas_call(
        body, in_specs=[vmem] * 4, out_specs=[vmem] * 4, out_shape=[jax.ShapeDtypeStruct(w.shape, F32)] * 4,
        name=name)(w, land, m, v)


def _block_diag(w):
    h, a, b = w.shape
    eye = jnp.eye(h, dtype=w.dtype)
    return (eye[:, None, :, None] * w[:, :, None, :]).reshape(h * a, h * b)


def _diag_blocks(w, h):
    a = w.shape[0] // h
    return jnp.stack([w[i * a:(i + 1) * a, i * a:(i + 1) * a] for i in range(h)])


def _local_step(x, mem, target, p, hooks=None):
    g = {}
    p = dict(p)
    w_rg, w_ig = _block_diag(p["w_rgate"]).astype(BF16), _block_diag(p["w_igate"]).astype(BF16)

    (gu1, f1, xb, h1, h1b), arrived = _ffn_fwd(x, p["ffn1_w13"], p["ffn1_w2"], name="ffn1_fwd", ln=(p["ln1_g"], p["ln1_b"]),
                                               job=hooks and hooks.gather_job(_GATHER_MIXER))
    if hooks:
        p.update(hooks.gathered(_GATHER_MIXER, arrived))
    w_in = p["w_in"]
    w_in_lru, w_in_q, w_in_kv = w_in[:, :2 * LRU_W], w_in[:, 2 * LRU_W:2 * LRU_W + SB_W], w_in[:, 2 * LRU_W + SB_W:]
    xbg = _matmul(h1b, w_in_lru, name="proj_lru")
    q_sb = _matmul(h1b, w_in_q, out_dtype=BF16, scale=1.0 / math.sqrt(SB_HEAD), name="proj_q")
    kv_sb = _matmul(h1b, w_in_kv, out_dtype=BF16, name="proj_kv")
    xc, hseq, y_lru = _lru_fwd(xbg, p["conv_w"], p["conv_b"], w_rg, p["b_rgate"], w_ig, p["b_igate"],
                               p["lru_lambda"], name="lru_fwd")
    y_sb, arrived = _sb_fwd(q_sb, kv_sb, name="sb_fwd", job=hooks and hooks.gather_job(_GATHER_FFN2))
    if hooks:
        p.update(hooks.gathered(_GATHER_FFN2, arrived))
    ymix = _mixnorm_fwd(y_lru, y_sb, p["g_lru"], p["g_sb"], name="mixnorm_fwd")
    mix, h2, h2b = _matmul(ymix, p["w_out"], ln=(h1, 1.0, p["ln2_g"], p["ln2_b"]), name="mix_out_ln2")
    qm = _matmul(h2b, p["mem_wq"], out_dtype=BF16, name="cross_q")
    kv = _matmul(mem, p["mem_wkv"], name="cross_kv")
    o_cross = _cross_fwd(qm, kv, name="cross_fwd")
    cross, h3, h3b = _matmul(o_cross, p["mem_wo"], ln=(h2, 1.0, p["ln3_g"], p["ln3_b"]), name="cross_out_ln3")
    (gu2, dz4, dz4b, g["ln4_g"], g["ln4_b"], loss), _ = _ffn_fwd(
        h3b, p["ffn2_w13"], p["ffn2_w2"], loss_head=(h3, target, p["ln4_g"], p["ln4_b"]), name="ffn2_fwd_loss")
    (dz3, dz3b, dgu2, act2, g["ln3_g"], g["ln3_b"]), _ = _ffn_bwd(
        dz4, gu2, p["ffn2_w13"], p["ffn2_w2"], ln_bwd=(h2, cross, 1.0, p["ln3_g"], p["ln3_b"]), name="ffn2_bwd_ln3")
    g["ffn2_w13"] = _matmul(h3b, dgu2, trans_a=True, tm=1024, tn=FF_CHUNK, tk=2048, out_dtype=BF16, name="ffn2_dw13")
    g["ffn2_w2"] = _matmul(act2, dz4b, trans_a=True, tm=FF_CHUNK, tn=1024, tk=2048, scale=0.5, out_dtype=BF16, name="ffn2_dw2")
    g["mem_wo"] = _matmul(o_cross, dz3b, trans_a=True, tm=1024, tn=1024, tk=1024, out_dtype=BF16, name="cross_dwo")
    do_cross = _matmul(dz3b, p["mem_wo"], trans_b=True, out_dtype=BF16, name="cross_do")
    dqm, dkv = _cross_bwd(qm, kv, do_cross, name="cross_bwd")
    g["mem_wq"] = _matmul(h2b, dqm, trans_a=True, tm=1024, tn=1024, tk=1024, out_dtype=BF16, name="cross_dwq")
    g["mem_wkv"] = _matmul(mem, dkv, trans_a=True, tm=1024, tn=MEM_HD, tk=256, out_dtype=BF16, name="cross_dwkv")
    dz2, dz2b, g["ln2_g"], g["ln2_b"] = _matmul(dqm, p["mem_wq"], trans_b=True, res=dz3, res_coeff=ALPHA,
                                                ln_bwd=(h1, mix, 1.0, p["ln2_g"], p["ln2_b"]), name="cross_dh_ln2")
    g["w_out"] = _matmul(ymix, dz2b, trans_a=True, tm=1024, tn=1024, tk=1024, out_dtype=BF16, name="mix_dwout")
    dymix = _matmul(dz2b, p["w_out"], trans_b=True, name="mix_dy")
    (dy_lru, dy_sb, g["g_lru"], g["g_sb"]), arrived = _mixnorm_bwd(
        y_lru, y_sb, p["g_lru"], p["g_sb"], dymix, name="mixnorm_bwd", job=hooks and hooks.sibling_job(_REDUCE_EARLY, g))
    (dq, dk, dv), arrived = _sb_bwd(q_sb, kv_sb, y_sb, dy_sb, name="sb_bwd",
                                    job=hooks and hooks.chip_job(_REDUCE_EARLY, arrived))
    if hooks:
        hooks.reduced(_REDUCE_EARLY, arrived)
    dxc, dgate, dwr, dwi, g["b_rgate"], g["b_igate"], g["lru_lambda"] = _lru_bwd(
        xc, xbg, hseq, dy_lru, w_rg, p["b_rgate"], w_ig, p["b_igate"], p["lru_lambda"], name="lru_bwd")
    g["w_rgate"], g["w_igate"] = _diag_blocks(dwr, 8), _diag_blocks(dwi, 8)
    dxb, g["conv_w"], g["conv_b"] = _conv_bwd(dxc, xbg, p["conv_w"], name="conv_bwd")
    parts = [dxb, dgate, dq, dk, dv]
    g["w_in"] = _proj_dw(h1b, parts, name="proj_dw")
    (dz1, dz1b, g["ln1_g"], g["ln1_b"]), arrived = _proj_bwd_dx(
        parts, w_in, dz2, ln_bwd=(x, f1, 0.5, p["ln1_g"], p["ln1_b"]), name="proj_dh_ln1",
        job=hooks and hooks.sibling_job(['w_in'], g))
    (grad_x, dgu1, act1), _ = _ffn_bwd(dz1, gu1, p["ffn1_w13"], p["ffn1_w2"], name="ffn1_bwd")
    dw13 = dict(trans_a=True, tm=1024, tn=FF_CHUNK, tk=2048, out_dtype=BF16, name="ffn1_dw13")
    dw2 = dict(trans_a=True, tm=FF_CHUNK, tn=1024, tk=2048, scale=0.5, out_dtype=BF16, name="ffn1_dw2")
    if not hooks:
        g["ffn1_w13"] = _matmul(xb, dgu1, **dw13)
        g["ffn1_w2"] = _matmul(act1, dz1b, **dw2)
        return loss, grad_x, g
    g["ffn1_w13"], arrived = _matmul(xb, dgu1, job=_Jobs([hooks.chip_job(['w_in'], arrived), hooks.small_job(g)]), **dw13)
    hooks.reduced(['w_in'], arrived[:1])
    hooks.small_land = arrived[1]
    arrived = _run_job(hooks.sibling_job(['ffn1_w13'], g), name="reduce_sibling_w13")
    g["ffn1_w2"], arrived = _matmul(act1, dz1b, job=hooks.chip_job(['ffn1_w13'], arrived), **dw2)
    hooks.reduced(['ffn1_w13'], arrived)
    arrived = _run_job(hooks.sibling_job(['ffn1_w2'], g), name="reduce_sibling_w2")
    hooks.reduced(['ffn1_w2'], _run_job(hooks.chip_job(['ffn1_w2'], arrived), name="reduce_chips_w2"))
    return loss, grad_x, g


_WEIGHTS = ['ffn1_w13', 'ffn1_w2', 'ln1_g', 'ln1_b', 'w_in', 'conv_w', 'conv_b', 'w_rgate', 'b_rgate', 'w_igate',
            'b_igate', 'lru_lambda', 'g_lru', 'g_sb', 'w_out', 'ln2_g', 'ln2_b', 'mem_wq', 'mem_wkv', 'mem_wo',
            'ln3_g', 'ln3_b', 'ffn2_w13', 'ffn2_w2', 'ln4_g', 'ln4_b']
_SHARDED = ['ffn1_w13', 'ffn1_w2', 'w_in', 'w_out', 'mem_wq', 'mem_wkv', 'mem_wo', 'ffn2_w13', 'ffn2_w2']
_SMALL = ['ln1_g', 'ln1_b', 'ln2_g', 'ln2_b', 'ln3_g', 'ln3_b', 'ln4_g', 'ln4_b', 'conv_b', 'b_rgate', 'b_igate',
          'lru_lambda', 'g_lru', 'g_sb', 'w_rgate', 'w_igate']


def _pack_small(d, conv_w_full=None):
    rows = [d[n].reshape(-1) for n in _SMALL]
    if conv_w_full is not None:
        rows.append(conv_w_full.reshape(-1))
    flat = jnp.concatenate(rows)
    pad = (-flat.shape[0]) % (8 * D_MODEL)
    return jnp.pad(flat, (0, pad)).reshape(-1, D_MODEL)


def _unpack_small(pack, like, with_conv):
    flat = pack.reshape(-1)
    out, off = {}, 0
    for n in _SMALL:
        size = math.prod(like[n].shape)
        out[n] = flat[off:off + size].reshape(like[n].shape)
        off += size
    conv = flat[off:off + CONV_W * LRU_W].reshape(CONV_W, LRU_W) if with_conv else None
    return out, conv


_GATHER_FIRST = ['ffn1_w13', 'ffn1_w2']
_GATHER_MIXER = ['w_in', 'conv_w', 'w_out', 'mem_wq', 'mem_wkv', 'mem_wo']
_GATHER_FFN2 = ['ffn2_w13', 'ffn2_w2']
_REDUCE_EARLY = ['ffn2_w13', 'ffn2_w2', 'mem_wo', 'mem_wq', 'mem_wkv', 'w_out']


def _weight_layout(n, full):
    if n in ('ffn1_w13', 'ffn2_w13', 'mem_wkv'):
        return full
    if n == 'w_in':
        return full.transpose(1, 0, 2).reshape(D_MODEL, -1)
    if n == 'conv_w':
        return full.transpose(1, 0, 2).reshape(CONV_W, LRU_W)
    return full.reshape(-1, full.shape[-1])


def _grad_blocks(n, g):
    if n == 'w_in':
        return g.reshape(D_MODEL, N_DEV, -1).transpose(1, 0, 2)
    return g.reshape((N_DEV, -1, g.shape[-1]))


class _Hooks:
    def __init__(self, w, cj_idx):
        self.w, self.cj_idx = w, cj_idx
        self.sums, self.recv = {}, {}

    def shard(self, n):
        return self.w[n][0] if n == 'conv_w' else self.w[n][0].astype(BF16)

    def gather_job(self, names):
        return _GatherJob([self.shard(n) for n in names])

    def gathered(self, names, outs):
        return {n: _weight_layout(n, o) for n, o in zip(names, outs)}

    def sibling_job(self, names, g):
        self.blocks = [_grad_blocks(n, g[n]) for n in names]
        return _SiblingJob(self.blocks)

    def chip_job(self, names, from_sibling):
        for n, b, r in zip(names, self.blocks, from_sibling):
            self.sums[n] = _pair_add(b, r, self.cj_idx, name=f"pair_add_{n}")
        return _ChipJob([self.sums[n][1] for n in names])

    def reduced(self, names, from_chips):
        self.recv.update(zip(names, from_chips))

    def small_job(self, g):
        return _SmallGatherJob(_pack_small(g, conv_w_full=g["conv_w"]))


def kernel(x, mem, ffn1_w13, ffn1_w2, ln1_g, ln1_b, w_in, conv_w, conv_b, w_rgate, b_rgate, w_igate, b_igate, lru_lambda, g_lru, g_sb, w_out, ln2_g, ln2_b, mem_wq, mem_wkv, mem_wo, ln3_g, ln3_b, ffn2_w13, ffn2_w2, ln4_g, ln4_b, loss_target, m_ffn1_w13, m_ffn1_w2, m_ln1_g, m_ln1_b, m_w_in, m_conv_w, m_conv_b, m_w_rgate, m_b_rgate, m_w_igate, m_b_igate, m_lru_lambda, m_g_lru, m_g_sb, m_w_out, m_ln2_g, m_ln2_b, m_mem_wq, m_mem_wkv, m_mem_wo, m_ln3_g, m_ln3_b, m_ffn2_w13, m_ffn2_w2, m_ln4_g, m_ln4_b, v_ffn1_w13, v_ffn1_w2, v_ln1_g, v_ln1_b, v_w_in, v_conv_w, v_conv_b, v_w_rgate, v_b_rgate, v_w_igate, v_b_igate, v_lru_lambda, v_g_lru, v_g_sb, v_w_out, v_ln2_g, v_ln2_b, v_mem_wq, v_mem_wkv, v_mem_wo, v_ln3_g, v_ln3_b, v_ffn2_w13, v_ffn2_w2, v_ln4_g, v_ln4_b):
    args = locals()
    w = {n: args[n] for n in _WEIGHTS}
    mom = {n: args["m_" + n] for n in _WEIGHTS}
    var = {n: args["v_" + n] for n in _WEIGHTS}
    ix, iy, ic = lax.axis_index("x"), lax.axis_index("y"), lax.axis_index("c")

    j_idx = jnp.reshape(2 * ix + iy, (1,)).astype(jnp.int32)
    hooks = _Hooks(w, jnp.stack([ic, 2 * ix + iy]).astype(jnp.int32))
    p = {n: (w[n][0] if w[n].ndim == 4 else w[n]) for n in _WEIGHTS if n not in _SHARDED and n != 'conv_w'}
    p.update(hooks.gathered(_GATHER_FIRST, _run_job(hooks.gather_job(_GATHER_FIRST), name="gather_ffn1")))
    loss_local, grad_x, g = _local_step(x[0], mem[0], loss_target[0], p, hooks)
    loss = lax.psum(loss_local[0, 0], ("x", "y", "c"))

    grads, delta, new_m, new_v = {}, {}, {}, {}
    for n in _SHARDED:
        grads[n], delta[n], new_m[n], new_v[n] = (
            o.reshape(w[n].shape) for o in _reduce_adam(hooks.sums[n][0], hooks.recv[n], j_idx, w[n][0], mom[n][0],
                                                        var[n][0], name=f"adam_{n}"))

    me = 4 * ix + 2 * iy + ic
    own_cols = lambda full: lax.dynamic_slice_in_dim(full, me * (LRU_W // N_DEV), LRU_W // N_DEV, axis=1)
    land = hooks.small_land
    conv_off = sum(math.prod(w[n].shape) for n in _SMALL)
    land_conv = land.reshape(N_DEV, -1)[:, conv_off:conv_off + CONV_W * LRU_W].reshape(N_DEV, CONV_W, LRU_W)
    land_own = jnp.pad(own_cols(land_conv.reshape(N_DEV * CONV_W, LRU_W)).reshape(N_DEV, -1),
                       ((0, 0), (0, CONV_W * LRU_W - CONV_W * LRU_W // N_DEV)))
    land = lax.dynamic_update_slice(land.reshape(N_DEV, -1), land_own, (0, conv_off)).reshape(land.shape)
    pk = lambda d: _pack_small({n: d[n] for n in _SMALL}, conv_w_full=jnp.pad(d["conv_w"].reshape(-1), (0, CONV_W * LRU_W - CONV_W * LRU_W // N_DEV)))
    g_s, d_s, m_s, v_s = _adam_small(pk(w), land, pk(mom), pk(var), name="adam_small")
    for src, dst in ((g_s, grads), (d_s, delta), (m_s, new_m), (v_s, new_v)):
        vals, conv = _unpack_small(src, w, with_conv=True)
        dst.update(vals)
        dst["conv_w"] = conv.reshape(-1)[:CONV_W * LRU_W // N_DEV].reshape(w["conv_w"].shape)

    return (loss, grad_x[None], *[grads[n] for n in _WEIGHTS], *[delta[n] for n in _WEIGHTS],
            *[new_m[n] for n in _WEIGHTS], *[new_v[n] for n in _WEIGHTS])
```

```python
import functools
import math

import jax
import jax.numpy as jnp
from jax import lax
from jax.experimental import pallas as pl
from jax.experimental.pallas import tpu as pltpu

F32, BF16 = jnp.float32, jnp.bfloat16

D_MODEL = 1024
LRU_W = 512
SB_W = 512
SB_PAIR = 128
SB_HEAD = 64
SB_ROWS = 128
SB_SKIP = -110.0
D_FF = 2688
FF_CHUNK = 672
N_DEV = 8
MEM_HEADS = 4
MEM_HD = 256
CONV_W = 4
ALPHA = 2.0 ** 0.25
LN_EPS = 1e-5
RMS_EPS = 1e-6
LRU_C = 8.0
ADAM_LR, ADAM_B1, ADAM_B2, ADAM_EPS, ADAM_WD, ADAM_STEP = 0.001, 0.9, 0.999, 1e-08, 0.01, 10

NN = (((1,), (0,)), ((), ()))
NT = (((1,), (1,)), ((), ()))
MESH = pl.DeviceIdType.MESH
VMEM_LIMIT_MB = 56


def _dot(a, b, dn=NN):
    return lax.dot_general(a, b, dn, preferred_element_type=F32)


def _cparams(n_axes, vmem_mb=None):
    kw = dict(dimension_semantics=("arbitrary",) * n_axes)
    if vmem_mb is not None:
        kw["vmem_limit_bytes"] = vmem_mb << 20
    return pltpu.CompilerParams(**kw)


def _row_tile(rows, want):
    if rows <= want:
        return rows
    t = want - want % 8
    while rows % t:
        t -= 8
    return t


def _matmul(a, b, *, name, out_dtype=F32, trans_a=False, trans_b=False, tm=512, tn=1024, tk=1024,
            scale=1.0, res=None, res_coeff=1.0, job=None, ln=None, ln_bwd=None):
    a_chunked, b_chunked = a.ndim == 3, b.ndim == 3
    nc = a.shape[0] if a_chunked else (b.shape[0] if b_chunked else 1)
    a2, b2 = a.shape[-2:], b.shape[-2:]
    (kdim, m) = a2 if trans_a else a2[::-1]
    n = b2[0] if trans_b else b2[1]
    tm, tn, tk = min(tm, m), min(tn, n), min(tk, kdim)
    assert m % tm == 0 and n % tn == 0 and kdim % tk == 0, (name, m, n, kdim)
    nk = kdim // tk

    def a_idx(c, i, j, k):
        idx = (k, i) if trans_a else (i, k)
        return (c,) + idx if a_chunked else idx

    def b_idx(c, i, j, k):
        idx = (j, k) if trans_b else (k, j)
        return (c,) + idx if b_chunked else idx

    a_blk = (tk, tm) if trans_a else (tm, tk)
    b_blk = (tn, tk) if trans_b else (tk, tn)
    in_specs = [pl.BlockSpec(((None,) + a_blk) if a_chunked else a_blk, a_idx),
                pl.BlockSpec(((None,) + b_blk) if b_chunked else b_blk, b_idx)]
    args = [a, b]
    if res is not None:
        in_specs.append(pl.BlockSpec((None, tm, tn), lambda c, i, j, k: (c, i, j)))
        args.append(res.reshape((nc, m, n)))
    if ln is not None:
        assert nc == 1 and tn == n and res is None
        in_specs += [pl.BlockSpec((tm, tn), lambda c, i, j, k: (i, 0)), pl.BlockSpec((1, tn), lambda c, i, j, k: (0, 0)),
                     pl.BlockSpec((1, tn), lambda c, i, j, k: (0, 0))]
        args += [ln[0], ln[2], ln[3]]
    if ln_bwd is not None:
        assert nc == 1 and tn == n and res is not None and ln is None
        in_specs += [pl.BlockSpec((tm, tn), lambda c, i, j, k: (i, 0))] * 2 + [pl.BlockSpec((1, tn), lambda c, i, j, k: (0, 0))] * 2
        args += [ln_bwd[0], ln_bwd[1], ln_bwd[3], ln_bwd[4]]

    def body(*refs):
        if ln_bwd is not None:
            a_ref, b_ref, r_ref, x_ref, f_ref, g_ref, bb_ref, o_ref, dzb_ref, dg_ref, db_ref, acc_ref = refs
        elif res is not None:
            a_ref, b_ref, r_ref, o_ref, acc_ref = refs
        elif ln is not None:
            a_ref, b_ref, x_ref, g_ref, bb_ref, o_ref, h_ref, hb_ref, acc_ref = refs
        else:
            a_ref, b_ref, o_ref, acc_ref = refs
        k = pl.program_id(3)

        @pl.when(k == 0)
        def _():
            acc_ref[...] = jnp.zeros_like(acc_ref)

        av = a_ref[...]
        if trans_a:
            av = av.astype(F32).T
        acc_ref[...] += _dot(av.astype(BF16), b_ref[...].astype(BF16), NT if trans_b else NN)

        @pl.when(k == nk - 1)
        def _():
            out = acc_ref[...]
            if scale != 1.0:
                out = out * scale
            if res is not None:
                out = out + res_coeff * r_ref[...]
            if ln_bwd is not None:
                dz, dg, db = _ln_bwd_tile(x_ref[...], f_ref[...], ln_bwd[2], g_ref[...], bb_ref[...], out)
                _ln_bwd_store(pl.program_id(1) == 0, dz, dg, db, o_ref, dzb_ref, dg_ref, db_ref)
                return
            o_ref[...] = out.astype(out_dtype)
            if ln is not None:
                h = _ln_math(ALPHA * x_ref[...] + ln[1] * out, g_ref[...], bb_ref[...])
                h_ref[...] = h
                hb_ref[...] = h.astype(BF16)

    grid = (nc, m // tm, n // tn, nk)

    def step():
        idx = pl.program_id(0)
        for ax in range(1, 4):
            idx = idx * grid[ax] + pl.program_id(ax)
        return idx

    out_specs = [pl.BlockSpec((None, tm, tn), lambda c, i, j, k: (c, i, j))]
    out_shape = [jax.ShapeDtypeStruct((nc, m, n), out_dtype)]
    if ln is not None:
        out_specs += [pl.BlockSpec((tm, tn), lambda c, i, j, k: (i, 0))] * 2
        out_shape += [jax.ShapeDtypeStruct((m, n), F32), jax.ShapeDtypeStruct((m, n), BF16)]
    if ln_bwd is not None:
        out_specs += [pl.BlockSpec((tm, tn), lambda c, i, j, k: (i, 0))] + [pl.BlockSpec((1, tn), lambda c, i, j, k: (0, 0))] * 2
        out_shape += _ln_bwd_shapes(m, n)[1:]
    outs, job_out = _pallas_with_job(
        body, job, step, math.prod(grid), grid=grid, in_specs=in_specs, out_specs=out_specs, out_shape=out_shape,
        scratch_shapes=[pltpu.VMEM((tm, tn), F32)],
        compiler_params=_cparams(4, VMEM_LIMIT_MB), name=name, args=args)
    out = outs[0] if (a_chunked or b_chunked) else outs[0][0]
    if ln is not None:
        return out, outs[1], outs[2]
    if ln_bwd is not None:
        return out, outs[1], outs[2], outs[3]
    return out if job is None else (out, job_out)


def _ln_math(z, g, b):
    mu = jnp.mean(z, axis=-1, keepdims=True)
    zc = z - mu
    var = jnp.mean(zc * zc, axis=-1, keepdims=True)
    return zc * lax.rsqrt(var + LN_EPS) * g + b


def _row_spec(tm, d):
    return pl.BlockSpec((tm, d), lambda i: (i, 0))


def _par_spec(d, rows=1):
    return pl.BlockSpec((rows, d), lambda i: (0, 0))


def _ln_bwd_tile(x, f, fscale, g, b, dy):
    _, vjp = jax.vjp(_ln_math, ALPHA * x + fscale * f, g, b)
    return vjp(dy)


def _ln_bwd_specs(tm, d):
    return ([_row_spec(tm, d), _row_spec(tm, d), _par_spec(d), _par_spec(d)],
            [_row_spec(tm, d), _row_spec(tm, d), _par_spec(d), _par_spec(d)])


def _ln_bwd_shapes(s, d):
    return [jax.ShapeDtypeStruct((s, d), F32), jax.ShapeDtypeStruct((s, d), BF16),
            jax.ShapeDtypeStruct((1, d), F32), jax.ShapeDtypeStruct((1, d), F32)]


def _ln_bwd_store(first, dz, dg, db, dz_ref, dzb_ref, dg_ref, db_ref):
    @pl.when(first)
    def _():
        dg_ref[...] = jnp.zeros_like(dg_ref)
        db_ref[...] = jnp.zeros_like(db_ref)

    dz_ref[...] = dz
    dzb_ref[...] = dz.astype(BF16)
    dg_ref[...] += dg
    db_ref[...] += db


def _load_weights_once(pairs, sem):
    @pl.when(pl.program_id(0) == 0)
    def _():
        copies = [pltpu.make_async_copy(src, dst, sem.at[n]) for n, (src, dst) in enumerate(pairs)]
        for c in copies:
            c.start()
        for c in copies:
            c.wait()


def _ffn_fwd(h, w13c, w2, *, name, job=None, ln=None, loss_head=None):
    s, d = h.shape
    tm = min(256, s)
    half = N_DEV // 2

    def body(*refs):
        if loss_head is not None:
            (h_ref, w13_hbm, w2_hbm, x_ref, t_ref, g_ref, b_ref, gu_ref, dz_ref, dzb_ref, dg_ref, db_ref, loss_ref,
             w13_v, w2_v, sem) = refs
        elif ln is not None:
            h_ref, w13_hbm, w2_hbm, g_ref, b_ref, gu_ref, f_ref, hb_ref, n_ref, nb_ref, w13_v, w2_v, sem = refs
        else:
            h_ref, w13_hbm, w2_hbm, gu_ref, f_ref, hb_ref, w13_v, w2_v, sem = refs
        _load_weights_once([(w13_hbm, w13_v), (w2_hbm, w2_v)], sem)
        hb = h_ref[...].astype(BF16)
        if loss_head is None:
            hb_ref[...] = hb
        acc = jnp.zeros((tm, d), F32)
        for k in range(half):
            g = _dot(hb, w13_v[k])
            u = _dot(hb, w13_v[k + half])
            gu_ref[k] = g.astype(BF16)
            gu_ref[k + half] = u.astype(BF16)
            a = g * jax.nn.sigmoid(g) * u
            acc = acc + _dot(a.astype(BF16), w2_v[pl.ds(k * FF_CHUNK, FF_CHUNK), :])
        if loss_head is not None:
            first = pl.program_id(0) == 0

            @pl.when(first)
            def _():
                loss_ref[...] = jnp.zeros_like(loss_ref)

            y, vjp = jax.vjp(_ln_math, ALPHA * x_ref[...] + 0.5 * acc, g_ref[...], b_ref[...])
            err = y - t_ref[...]
            sq = jnp.sum(err * err, axis=1, keepdims=True)
            loss_ref[...] += jnp.sum(sq, axis=0, keepdims=True) * (0.5 / d)
            dz, dg, db = vjp(err * (1.0 / d))
            _ln_bwd_store(first, dz, dg, db, dz_ref, dzb_ref, dg_ref, db_ref)
            return
        f_ref[...] = acc
        if ln is not None:
            hn = _ln_math(ALPHA * h_ref[...] + 0.5 * acc, g_ref[...], b_ref[...])
            n_ref[...] = hn
            nb_ref[...] = hn.astype(BF16)

    any_spec = pl.BlockSpec(memory_space=pl.ANY)
    gu_spec = pl.BlockSpec((N_DEV, tm, FF_CHUNK), lambda i: (0, i, 0))
    gu_shape = jax.ShapeDtypeStruct((N_DEV, s, FF_CHUNK), BF16)
    scratch = [pltpu.VMEM(w13c.shape, BF16), pltpu.VMEM(w2.shape, BF16), pltpu.SemaphoreType.DMA((2,))]
    if loss_head is not None:
        return _pallas_with_job(
            body, job, lambda: pl.program_id(0), s // tm, grid=(s // tm,),
            in_specs=[_row_spec(tm, d), any_spec, any_spec, _row_spec(tm, d), _row_spec(tm, d), _par_spec(d), _par_spec(d)],
            out_specs=[gu_spec] + _ln_bwd_specs(tm, d)[1] + [pl.BlockSpec((1, 1), lambda i: (0, 0))],
            out_shape=[gu_shape] + _ln_bwd_shapes(s, d) + [jax.ShapeDtypeStruct((1, 1), F32)],
            scratch_shapes=scratch, compiler_params=_cparams(1, VMEM_LIMIT_MB), name=name,
            args=(h, w13c, w2) + tuple(loss_head))
    n_ln = 0 if ln is None else 2
    return _pallas_with_job(
        body, job, lambda: pl.program_id(0), s // tm,
        grid=(s // tm,), in_specs=[_row_spec(tm, d), any_spec, any_spec] + [_par_spec(d)] * n_ln,
        out_specs=[pl.BlockSpec((N_DEV, tm, FF_CHUNK), lambda i: (0, i, 0)), _row_spec(tm, d), _row_spec(tm, d)]
                  + [_row_spec(tm, d)] * n_ln,
        out_shape=[jax.ShapeDtypeStruct((N_DEV, s, FF_CHUNK), BF16), jax.ShapeDtypeStruct((s, d), F32),
                   jax.ShapeDtypeStruct((s, d), BF16)]
                  + [jax.ShapeDtypeStruct((s, d), F32), jax.ShapeDtypeStruct((s, d), BF16)][:n_ln],
        scratch_shapes=[pltpu.VMEM(w13c.shape, BF16), pltpu.VMEM(w2.shape, BF16), pltpu.SemaphoreType.DMA((2,))],
        compiler_params=_cparams(1, VMEM_LIMIT_MB), name=name, args=(h, w13c, w2) + (() if ln is None else tuple(ln)))


def _ffn_bwd(dz, gu, w13c, w2, *, name, job=None, ln_bwd=None):
    s, d = dz.shape
    tm = min(256, s)
    half = N_DEV // 2

    def body(*refs):
        if ln_bwd is not None:
            (dz_ref, gu_ref, w13_hbm, w2_hbm, x_ref, f_ref, g_ref, b_ref, dzp_ref, dzpb_ref, dgu_ref, act_ref,
             dg_ref, db_ref, w13_v, w2_v, sem) = refs
        else:
            dz_ref, gu_ref, w13_hbm, w2_hbm, dh_ref, dgu_ref, act_ref, w13_v, w2_v, sem = refs
        _load_weights_once([(w13_hbm, w13_v), (w2_hbm, w2_v)], sem)
        dzv = dz_ref[...]
        dfb = (0.5 * dzv).astype(BF16)
        acc = ALPHA * dzv
        for k in range(half):
            g = gu_ref[k].astype(F32)
            u = gu_ref[k + half].astype(F32)
            da = _dot(dfb, w2_v[pl.ds(k * FF_CHUNK, FF_CHUNK), :], NT)
            sg = jax.nn.sigmoid(g)
            silu = g * sg
            dg = (da * u * (sg * (1.0 + g * (1.0 - sg)))).astype(BF16)
            du = (da * silu).astype(BF16)
            act_ref[k] = (silu * u).astype(BF16)
            dgu_ref[k] = dg
            dgu_ref[k + half] = du
            acc = acc + _dot(dg, w13_v[k], NT) + _dot(du, w13_v[k + half], NT)
        if ln_bwd is not None:
            dzp, dg, db = _ln_bwd_tile(x_ref[...], f_ref[...], ln_bwd[2], g_ref[...], b_ref[...], acc)
            _ln_bwd_store(pl.program_id(0) == 0, dzp, dg, db, dzp_ref, dzpb_ref, dg_ref, db_ref)
        else:
            dh_ref[...] = acc

    any_spec = pl.BlockSpec(memory_space=pl.ANY)
    in_specs = [_row_spec(tm, d), pl.BlockSpec((N_DEV, tm, FF_CHUNK), lambda i: (0, i, 0)), any_spec, any_spec]
    mid_specs = [pl.BlockSpec((N_DEV, tm, FF_CHUNK), lambda i: (0, i, 0)), pl.BlockSpec((half, tm, FF_CHUNK), lambda i: (0, i, 0))]
    mid_shapes = [jax.ShapeDtypeStruct((N_DEV, s, FF_CHUNK), BF16), jax.ShapeDtypeStruct((half, s, FF_CHUNK), BF16)]
    args = (dz, gu, w13c, w2)
    if ln_bwd is not None:
        ln_in, ln_out = _ln_bwd_specs(tm, d)
        in_specs, out_specs = in_specs + ln_in, ln_out[:2] + mid_specs + ln_out[2:]
        shapes = _ln_bwd_shapes(s, d)
        out_shape = shapes[:2] + mid_shapes + shapes[2:]
        args += (ln_bwd[0], ln_bwd[1], ln_bwd[3], ln_bwd[4])
    else:
        out_specs, out_shape = [_row_spec(tm, d)] + mid_specs, [jax.ShapeDtypeStruct((s, d), F32)] + mid_shapes
    return _pallas_with_job(
        body, job, lambda: pl.program_id(0), s // tm, grid=(s // tm,), in_specs=in_specs, out_specs=out_specs,
        out_shape=out_shape,
        scratch_shapes=[pltpu.VMEM(w13c.shape, BF16), pltpu.VMEM(w2.shape, BF16), pltpu.SemaphoreType.DMA((2,))],
        compiler_params=_cparams(1, VMEM_LIMIT_MB), name=name, args=args)


def _shift_down(x, prev8, d):
    t, c = x.shape
    row = lax.broadcasted_iota(jnp.int32, (t, c), 0)
    xr = pltpu.roll(x, d, 0)
    pr = pltpu.roll(prev8, d, 0)
    if t > 8:
        pr = jnp.concatenate([pr, jnp.zeros((t - 8, c), x.dtype)], axis=0)
    return jnp.where(row < d, pr, xr)


def _shift_up(x, next8, d):
    t, c = x.shape
    row = lax.broadcasted_iota(jnp.int32, (t, c), 0)
    xr = pltpu.roll(x, t - d, 0)
    nr = pltpu.roll(next8, 8 - d, 0)
    if t > 8:
        nr = jnp.concatenate([jnp.zeros((t - 8, c), x.dtype), nr], axis=0)
    return jnp.where(row >= t - d, nr, xr)


def _scan_fwd(a, u, carry):
    t = a.shape[0]
    row = lax.broadcasted_iota(jnp.int32, a.shape, 0) % 8
    for d in (1, 2, 4):
        a_s, u_s = pltpu.roll(a, d, 0), pltpu.roll(u, d, 0)
        m = row >= d
        u = jnp.where(m, u + a * u_s, u)
        a = jnp.where(m, a * a_s, a)
    out = []
    for g in range(t // 8):
        hg = u[8 * g:8 * g + 8, :] + a[8 * g:8 * g + 8, :] * carry
        out.append(hg)
        carry = hg[7:8, :]
    return jnp.concatenate(out, axis=0)


def _scan_bwd(b, x, carry):
    t = b.shape[0]
    row = lax.broadcasted_iota(jnp.int32, b.shape, 0) % 8
    for d in (1, 2, 4):
        b_s, x_s = pltpu.roll(b, t - d, 0), pltpu.roll(x, t - d, 0)
        m = row < 8 - d
        x = jnp.where(m, x + b * x_s, x)
        b = jnp.where(m, b * b_s, b)
    out = [None] * (t // 8)
    for g in reversed(range(t // 8)):
        lg = x[8 * g:8 * g + 8, :] + b[8 * g:8 * g + 8, :] * carry
        out[g] = lg
        carry = lg[0:1, :]
    return jnp.concatenate(out, axis=0)


def _lru_elem(xc, pr, pi, b_r, b_i, lam):
    r = jax.nn.sigmoid(pr + b_r)
    ig = jax.nn.sigmoid(pi + b_i)
    softplus_neg_lam = jnp.maximum(-lam, 0.0) + jnp.log1p(jnp.exp(-jnp.abs(lam)))
    log_a = (-LRU_C) * r * softplus_neg_lam
    a = jnp.exp(log_a)
    one_minus_a2 = -jnp.tanh(log_a) * (jnp.exp(2.0 * log_a) + 1.0)
    u = jnp.sqrt(one_minus_a2) * (ig * xc)
    return a, u


def _gelu_tanh(x):
    return 0.5 * x * (1.0 + jnp.tanh(math.sqrt(2.0 / math.pi) * (x + 0.044715 * (x * x * x))))


def _conv_fwd(xb, prev8, w, b):
    out = b + w[3:4, :] * xb
    for d in range(1, CONV_W):
        out = out + w[3 - d:4 - d, :] * _shift_down(xb, prev8, d)
    return out


def _lru_fwd(xbg, conv_w, conv_b, w_rg, b_rg, w_ig, b_ig, lam, *, name):
    s = xbg.shape[0]
    c = LRU_W
    t = min(512, s)
    t8 = t // 8

    def body(xb_ref, gate_ref, prev_ref, cw_ref, cb_ref, wr_ref, br_ref, wi_ref, bi_ref, lam_ref,
             xc_ref, h_ref, y_ref, carry_ref):
        i = pl.program_id(0)

        @pl.when(i == 0)
        def _():
            carry_ref[...] = jnp.zeros_like(carry_ref)

        prev8 = jnp.where(i > 0, prev_ref[...], 0.0)
        xc = _conv_fwd(xb_ref[...], prev8, cw_ref[...], cb_ref[...])
        xcb = xc.astype(BF16)
        a, u = _lru_elem(xc, _dot(xcb, wr_ref[...]), _dot(xcb, wi_ref[...]), br_ref[...], bi_ref[...], lam_ref[...])
        h = _scan_fwd(a, u, carry_ref[0:1, :])
        carry_ref[...] = jnp.broadcast_to(h[t - 1:t, :], carry_ref.shape)
        xc_ref[...] = xc
        h_ref[...] = h
        y_ref[...] = h * _gelu_tanh(gate_ref[...])

    tile = lambda col: pl.BlockSpec((t, c), lambda i: (i, col))
    par = lambda rows: pl.BlockSpec((rows, c), lambda i: (0, 0))
    out_spec = pl.BlockSpec((t, c), lambda i: (i, 0))
    return pl.pallas_call(
        body, grid=(s // t,),
        in_specs=[tile(0), tile(1), pl.BlockSpec((8, c), lambda i: (jnp.maximum(i * t8 - 1, 0), 0)),
                  par(CONV_W), par(1), pl.BlockSpec((c, c), lambda i: (0, 0)), par(1),
                  pl.BlockSpec((c, c), lambda i: (0, 0)), par(1), par(1)],
        out_specs=[out_spec, out_spec, out_spec],
        out_shape=[jax.ShapeDtypeStruct((s, c), F32)] * 3,
        scratch_shapes=[pltpu.VMEM((8, c), F32)],
        compiler_params=_cparams(1, VMEM_LIMIT_MB), name=name)(
            xbg, xbg, xbg, conv_w, conv_b, w_rg, b_rg, w_ig, b_ig, lam)


def _lru_bwd(xc, xbg, hseq, dy, w_rg, b_rg, w_ig, b_ig, lam, *, name):
    s, c = xc.shape
    t = min(512, s)
    t8 = t // 8
    nt = s // t

    def body(xc_ref, gate_ref, h_ref, hprev_ref, dy_ref, wr_ref, br_ref, wi_ref, bi_ref, lam_ref,
             dxc_ref, dgate_ref, dwr_ref, dwi_ref, dbr_ref, dbi_ref, dlam_ref, a8_ref, l8_ref):
        i = pl.program_id(0)

        @pl.when(i == 0)
        def _():
            for r in (dwr_ref, dwi_ref, dbr_ref, dbi_ref, dlam_ref, a8_ref, l8_ref):
                r[...] = jnp.zeros_like(r)

        xc_v = xc_ref[...]
        xcb = xc_v.astype(BF16)
        (a, _), vjp = jax.vjp(_lru_elem, xc_v, _dot(xcb, wr_ref[...]), _dot(xcb, wi_ref[...]),
                              br_ref[...], bi_ref[...], lam_ref[...])
        gl, vjp_gelu = jax.vjp(_gelu_tanh, gate_ref[...])
        dyv, hs = dy_ref[...], h_ref[...]
        (dgate,) = vjp_gelu(dyv * hs)
        b_next = _shift_up(a, a8_ref[...], 1)
        lt = _scan_bwd(b_next, dyv * gl, l8_ref[0:1, :])
        a8_ref[...] = a[0:8, :]
        l8_ref[...] = lt[0:8, :]
        hprev8 = jnp.where(i < nt - 1, hprev_ref[...], 0.0)
        da = lt * _shift_down(hs, hprev8, 1)
        dxc_e, dpr, dpi, dbr, dbi, dlam = vjp((da, lt))
        dprb, dpib = dpr.astype(BF16), dpi.astype(BF16)
        dxc_ref[...] = dxc_e + _dot(dprb, wr_ref[...], NT) + _dot(dpib, wi_ref[...], NT)
        dgate_ref[...] = dgate.astype(BF16)
        xct = xc_v.T.astype(BF16)
        dwr_ref[...] += _dot(xct, dprb)
        dwi_ref[...] += _dot(xct, dpib)
        dbr_ref[...] += dbr
        dbi_ref[...] += dbi
        dlam_ref[...] += dlam

    rev = lambda col: pl.BlockSpec((t, c), lambda i: (nt - 1 - i, col))
    par = pl.BlockSpec((1, c), lambda i: (0, 0))
    sq = pl.BlockSpec((c, c), lambda i: (0, 0))
    return pl.pallas_call(
        body, grid=(nt,),
        in_specs=[rev(0), rev(1), rev(0), pl.BlockSpec((8, c), lambda i: (jnp.maximum((nt - 1 - i) * t8 - 1, 0), 0)),
                  rev(0), sq, par, sq, par, par],
        out_specs=[rev(0), rev(0), sq, sq, par, par, par],
        out_shape=[jax.ShapeDtypeStruct((s, c), F32), jax.ShapeDtypeStruct((s, c), BF16)]
                  + [jax.ShapeDtypeStruct((c, c), F32)] * 2 + [jax.ShapeDtypeStruct((1, c), F32)] * 3,
        scratch_shapes=[pltpu.VMEM((8, c), F32), pltpu.VMEM((8, c), F32)],
        compiler_params=_cparams(1, VMEM_LIMIT_MB), name=name)(
            xc, xbg, hseq, hseq, dy, w_rg, b_rg, w_ig, b_ig, lam)


def _conv_bwd(dxc, xbg, conv_w, *, name):
    s, c = dxc.shape
    t = min(512, s)
    t8 = t // 8
    nt = s // t

    def body(dxc_ref, next_ref, xb_ref, prev_ref, w_ref, dxb_ref, dw_ref, db_ref):
        i = pl.program_id(0)

        @pl.when(i == 0)
        def _():
            dw_ref[...] = jnp.zeros_like(dw_ref)
            db_ref[...] = jnp.zeros_like(db_ref)

        g = dxc_ref[...]
        xb = xb_ref[...]
        w = w_ref[...]
        next8 = jnp.where(i < nt - 1, next_ref[...], 0.0)
        prev8 = jnp.where(i > 0, prev_ref[...], 0.0)
        dxb = w[3:4, :] * g
        dw_ref[3:4, :] += jnp.sum(g * xb, axis=0, keepdims=True)
        for d in range(1, CONV_W):
            dxb = dxb + w[3 - d:4 - d, :] * _shift_up(g, next8, d)
            dw_ref[3 - d:4 - d, :] += jnp.sum(g * _shift_down(xb, prev8, d), axis=0, keepdims=True)
        dxb_ref[...] = dxb.astype(BF16)
        db_ref[...] += jnp.sum(g, axis=0, keepdims=True)

    tile = pl.BlockSpec((t, c), lambda i: (i, 0))
    return pl.pallas_call(
        body, grid=(nt,),
        in_specs=[tile, pl.BlockSpec((8, c), lambda i: (jnp.minimum((i + 1) * t8, s // 8 - 1), 0)),
                  tile, pl.BlockSpec((8, c), lambda i: (jnp.maximum(i * t8 - 1, 0), 0)),
                  pl.BlockSpec((CONV_W, c), lambda i: (0, 0))],
        out_specs=[tile, pl.BlockSpec((CONV_W, c), lambda i: (0, 0)), pl.BlockSpec((1, c), lambda i: (0, 0))],
        out_shape=[jax.ShapeDtypeStruct((s, c), BF16), jax.ShapeDtypeStruct((CONV_W, c), F32),
                   jax.ShapeDtypeStruct((1, c), F32)],
        compiler_params=_cparams(1), name=name)(dxc, dxc, xbg, xbg, conv_w)


def _hi_lo_dot(x, tri):
    hi = x.astype(BF16)
    lo = (x - hi.astype(F32)).astype(BF16)
    return _dot(hi, tri) + _dot(lo, tri)


def _tri(tk, inclusive):
    r = lax.broadcasted_iota(jnp.int32, (tk, tk), 0)
    c = lax.broadcasted_iota(jnp.int32, (tk, tk), 1)
    return ((r >= c) if inclusive else (r > c)).astype(BF16)


def _split_heads(x):
    first = lax.broadcasted_iota(jnp.int32, x.shape, 1) < SB_HEAD
    zero = jnp.zeros_like(x)
    return first, (jnp.where(first, x, zero), jnp.where(first, zero, x))


def _sb_softplus_terms(z):
    ls = jnp.minimum(z, 0.0) - jnp.log(1.0 + jnp.exp(-jnp.abs(z)))
    return ls, ls - z


def _sb_alive(runs):
    m = runs[0]
    for r in runs[1:]:
        m = jnp.maximum(m, r)
    return jnp.max(m) > SB_SKIP


def _sb_fwd(q, kv, *, name, job=None):
    s = q.shape[0]
    tb = min(256, s)
    rc = min(SB_ROWS, tb)
    npair = SB_W // SB_PAIR
    chains = [(h, c) for h in range(2) for c in range(tb // rc)]

    def body(q_ref, k_ref, v_ref, o_ref):
        i = pl.program_id(1)
        first, qs = _split_heads(q_ref[...])
        tri = _tri(tb, False)
        causal = lax.broadcasted_iota(jnp.int32, (tb, tb), 1) < lax.broadcasted_iota(jnp.int32, (tb, tb), 0)
        causal_c = [causal[c * rc:(c + 1) * rc, :] for c in range(tb // rc)]

        def group(j_lo, width, carry, diag_last):
            k0 = pl.multiple_of(j_lo * tb, tb)
            kb = k_ref[pl.ds(k0, width * tb), :]
            vb = v_ref[pl.ds(k0, width * tb), :]
            zw = [_dot(qs[h][c * rc:(c + 1) * rc, :], kb, NT) for h, c in chains]
            cols = [slice(t * tb, (t + 1) * tb) for t in range(width)]
            terms = [[_sb_softplus_terms(z[:, cl]) for cl in cols] for z in zw]
            lss = [[ls for ls, _ in tc] for tc in terms]
            ns = [[n for _, n in tc] for tc in terms]
            if diag_last:
                ns = [nc[:-1] + [jnp.where(causal_c[c], nc[-1], 0.0)] for nc, (h, c) in zip(ns, chains)]
            sufs = [[_hi_lo_dot(n, tri) for n in nc] for nc in ns]
            out = []
            for (h, c), lc, nc, sc, (run, acc) in zip(chains, lss, ns, sufs, carry):
                ws = [None] * width
                for t in reversed(range(width)):
                    w = jnp.exp(lc[t] + (sc[t] + run))
                    if diag_last and t == width - 1:
                        w = jnp.where(causal_c[c], w, 0.0)
                    ws[t] = w.astype(BF16)
                    run = run + jnp.sum(nc[t], axis=1, keepdims=True)
                out.append((run, acc + _dot(ws[0] if width == 1 else jnp.concatenate(ws, axis=1), vb)))
            return tuple(out)

        zero = (jnp.zeros((rc, 1), F32), jnp.zeros((rc, SB_PAIR), F32))
        carry = lax.cond(i == 0, lambda c: group(0, 1, c, True), lambda c: group(i - 1, 2, c, True),
                         (zero,) * len(chains))

        def step(st):
            j, _, c = st
            c = group(j, 1, c, False)
            return j - 1, _sb_alive([run for run, _ in c]), c

        _, _, carry = lax.while_loop(lambda st: jnp.logical_and(st[0] >= 0, st[1]), step,
                                     (i - 2, _sb_alive([run for run, _ in carry]), carry))
        accs = [jnp.concatenate([carry[n][1] for n, (h, c) in enumerate(chains) if h == hh], axis=0) for hh in range(2)]
        o_ref[...] = jnp.where(first, accs[0], accs[1])

    (o,), job_out = _pallas_with_job(
        body, job, lambda: pl.program_id(0) * (s // tb) + pl.program_id(1), npair * (s // tb),
        grid=(npair, s // tb),
        in_specs=[pl.BlockSpec((tb, SB_PAIR), lambda p, i: (i, p)),
                  pl.BlockSpec((s, SB_PAIR), lambda p, i: (0, p)),
                  pl.BlockSpec((s, SB_PAIR), lambda p, i: (0, npair + p))],
        out_specs=[pl.BlockSpec((tb, SB_PAIR), lambda p, i: (i, p))],
        out_shape=[jax.ShapeDtypeStruct((s, SB_W), F32)],
        compiler_params=_cparams(2, VMEM_LIMIT_MB), name=name, args=(q, kv, kv))
    return o, job_out


def _sb_bwd(q, kv, o, do, *, name, job=None):
    s = q.shape[0]
    tb = min(256, s)
    rc = min(SB_ROWS, tb)
    npair = SB_W // SB_PAIR
    chains = [(h, c) for h in range(2) for c in range(tb // rc)]

    def body(q_ref, k_ref, v_ref, o_ref, do_ref, dq_ref, dkb_ref, dvb_ref, dk_ref, dv_ref):
        i = pl.program_id(1)

        @pl.when(i == 0)
        def _():
            dk_ref[...] = jnp.zeros_like(dk_ref)
            dv_ref[...] = jnp.zeros_like(dv_ref)

        first, qs = _split_heads(q_ref[...])
        dob = do_ref[...].astype(BF16)
        _, dos = _split_heads(dob)
        prod = dob.astype(F32) * o_ref[...]
        deltas = (jnp.sum(jnp.where(first, prod, 0.0), axis=1, keepdims=True),
                  jnp.sum(jnp.where(first, 0.0, prod), axis=1, keepdims=True))
        tri_x, tri_i = _tri(tb, False), _tri(tb, True)
        causal = lax.broadcasted_iota(jnp.int32, (tb, tb), 1) < lax.broadcasted_iota(jnp.int32, (tb, tb), 0)
        causal_c = [causal[c * rc:(c + 1) * rc, :] for c in range(tb // rc)]
        rows = [slice(c * rc, (c + 1) * rc) for c in range(tb // rc)]

        def group(j_lo, width, carry, diag_last):
            k0 = pl.multiple_of(j_lo * tb, tb)
            kb = k_ref[pl.ds(k0, width * tb), :]
            vb = v_ref[pl.ds(k0, width * tb), :]
            cols = [slice(t * tb, (t + 1) * tb) for t in range(width)]
            zw = [_dot(qs[h][rows[c], :], kb, NT) for h, c in chains]
            dww = [_dot(dos[h][rows[c], :], vb, NT) for h, c in chains]
            terms = [[_sb_softplus_terms(z[:, cl]) for cl in cols] for z in zw]
            lss = [[ls for ls, _ in tc] for tc in terms]
            ns = [[n for _, n in tc] for tc in terms]
            if diag_last:
                ns = [nc[:-1] + [jnp.where(causal_c[c], nc[-1], 0.0)] for nc, (h, c) in zip(ns, chains)]
            sufs = [[_hi_lo_dot(n, tri_x) for n in nc] for nc in ns]
            wbs, gs, runs_n = [], [], []
            for (h, c), lc, nc, sc, dw, (run_n, _, _) in zip(chains, lss, ns, sufs, dww, carry):
                wb, g = [None] * width, [None] * width
                for t in reversed(range(width)):
                    w = jnp.exp(lc[t] + (sc[t] + run_n))
                    if diag_last and t == width - 1:
                        w = jnp.where(causal_c[c], w, 0.0)
                    wb[t] = w.astype(BF16)
                    g[t] = wb[t].astype(F32) * dw[:, cols[t]]
                    run_n = run_n + jnp.sum(nc[t], axis=1, keepdims=True)
                wbs.append(wb)
                gs.append(g)
                runs_n.append(run_n)
            gsufs = [[_hi_lo_dot(g, tri_i) for g in gc] for gc in gs]
            out = []
            dk_t = [jnp.zeros((tb, SB_PAIR), F32) for _ in range(width)]
            dv_t = [jnp.zeros((tb, SB_PAIR), F32) for _ in range(width)]
            for (h, c), lc, gc, gsc, wb, run_n, (_, run_g, dq_acc) in zip(chains, lss, gs, gsufs, wbs, runs_n, carry):
                qh, doh, delta = qs[h][rows[c], :], dos[h][rows[c], :], deltas[h][rows[c], :]
                dzb = [None] * width
                for t in reversed(range(width)):
                    pre = delta - (gsc[t] + run_g)
                    dz = gc[t] - jnp.exp(lc[t]) * (gc[t] + pre)
                    if diag_last and t == width - 1:
                        dz = jnp.where(causal_c[c], dz, 0.0)
                    run_g = run_g + jnp.sum(gc[t], axis=1, keepdims=True)
                    dzb[t] = dz.astype(BF16)
                    dk_t[t] = dk_t[t] + _dot(dz.T.astype(BF16), qh)
                    dv_t[t] = dv_t[t] + _dot(wb[t].astype(F32).T.astype(BF16), doh)
                out.append((run_n, run_g, dq_acc + _dot(dzb[0] if width == 1 else jnp.concatenate(dzb, axis=1), kb)))
            for t in range(width):
                dk_ref[pl.ds(pl.multiple_of((j_lo + t) * tb, tb), tb), :] += dk_t[t]
                dv_ref[pl.ds(pl.multiple_of((j_lo + t) * tb, tb), tb), :] += dv_t[t]
            return tuple(out)

        zero = (jnp.zeros((rc, 1), F32), jnp.zeros((rc, 1), F32), jnp.zeros((rc, SB_PAIR), F32))
        carry = lax.cond(i == 0, lambda c: group(0, 1, c, True), lambda c: group(i - 1, 2, c, True),
                         (zero,) * len(chains))

        def step(st):
            j, _, c = st
            c = group(j, 1, c, False)
            return j - 1, _sb_alive([r[0] for r in c]), c

        _, _, carry = lax.while_loop(lambda st: jnp.logical_and(st[0] >= 0, st[1]), step,
                                     (i - 2, _sb_alive([r[0] for r in carry]), carry))
        dqs = [jnp.concatenate([carry[n][2] for n, (h, c) in enumerate(chains) if h == hh], axis=0) for hh in range(2)]
        dq_ref[...] = (jnp.where(first, dqs[0], dqs[1]) * (1.0 / math.sqrt(SB_HEAD))).astype(BF16)

        @pl.when(i == s // tb - 1)
        def _():
            dkb_ref[...] = dk_ref[...].astype(BF16)
            dvb_ref[...] = dv_ref[...].astype(BF16)

    qtile = pl.BlockSpec((tb, SB_PAIR), lambda p, i: (i, p))
    col = pl.BlockSpec((s, SB_PAIR), lambda p, i: (0, p))
    return _pallas_with_job(
        body, job, lambda: pl.program_id(0) * (s // tb) + pl.program_id(1), npair * (s // tb),
        grid=(npair, s // tb),
        in_specs=[qtile, col, pl.BlockSpec((s, SB_PAIR), lambda p, i: (0, npair + p)), qtile, qtile],
        out_specs=[qtile, col, col],
        out_shape=[jax.ShapeDtypeStruct((s, SB_W), BF16)] * 3,
        scratch_shapes=[pltpu.VMEM((s, SB_PAIR), F32)] * 2,
        compiler_params=_cparams(2, VMEM_LIMIT_MB), name=name, args=(q, kv, kv, o, do))


def _mixnorm_math(yl, ys, gl, gs):
    def rms(x, g):
        return x * lax.rsqrt(jnp.mean(x * x, axis=-1, keepdims=True) + RMS_EPS) * g
    return rms(yl, gl), rms(ys, gs)


def _mixnorm_fwd(yl, ys, gl, gs, *, name):
    s, c = yl.shape
    tm = min(512, s)

    def body(yl_ref, ys_ref, gl_ref, gs_ref, y_ref):
        a, b = _mixnorm_math(yl_ref[...], ys_ref[...], gl_ref[...], gs_ref[...])
        y_ref[:, 0:c] = a.astype(BF16)
        y_ref[:, c:2 * c] = b.astype(BF16)

    return pl.pallas_call(
        body, grid=(s // tm,), in_specs=[_row_spec(tm, c), _row_spec(tm, c), _par_spec(c), _par_spec(c)],
        out_specs=_row_spec(tm, 2 * c), out_shape=jax.ShapeDtypeStruct((s, 2 * c), BF16),
        compiler_params=_cparams(1), name=name)(yl, ys, gl, gs)


def _cross_probs(qh, kh):
    sc = _dot(qh, kh, NT) * (1.0 / math.sqrt(MEM_HD))
    e = jnp.exp(sc - jnp.max(sc, axis=-1, keepdims=True))
    return e / jnp.sum(e, axis=-1, keepdims=True)


def _cross_fwd(q, kv, *, name):
    s, d = q.shape
    mlen = kv.shape[1]
    tm = min(512, s)

    def body(q_ref, kv_ref, o_ref):
        for h in range(MEM_HEADS):
            cols = slice(h * MEM_HD, (h + 1) * MEM_HD)
            p = _cross_probs(q_ref[:, cols].astype(BF16), kv_ref[h].astype(BF16))
            o_ref[:, cols] = _dot(p.astype(BF16), kv_ref[MEM_HEADS + h].astype(BF16)).astype(BF16)

    return pl.pallas_call(
        body, grid=(s // tm,),
        in_specs=[_row_spec(tm, d), pl.BlockSpec((2 * MEM_HEADS, mlen, MEM_HD), lambda i: (0, 0, 0))],
        out_specs=_row_spec(tm, d), out_shape=jax.ShapeDtypeStruct((s, d), BF16),
        compiler_params=_cparams(1, VMEM_LIMIT_MB), name=name)(q, kv)


def _cross_bwd(q, kv, do, *, name):
    s, d = q.shape
    mlen = kv.shape[1]
    tm = min(512, s)

    def body(q_ref, kv_ref, do_ref, dq_ref, dkv_ref):
        @pl.when(pl.program_id(0) == 0)
        def _():
            dkv_ref[...] = jnp.zeros_like(dkv_ref)

        for h in range(MEM_HEADS):
            cols = slice(h * MEM_HD, (h + 1) * MEM_HD)
            qh = q_ref[:, cols].astype(BF16)
            kh = kv_ref[h].astype(BF16)
            doh = do_ref[:, cols].astype(BF16)
            p = _cross_probs(qh, kh)
            dp = _dot(doh, kv_ref[MEM_HEADS + h].astype(BF16), NT)
            ds = p * (dp - jnp.sum(dp * p, axis=-1, keepdims=True)) * (1.0 / math.sqrt(MEM_HD))
            dq_ref[:, cols] = _dot(ds.astype(BF16), kh).astype(BF16)
            dkv_ref[h] += _dot(ds.T.astype(BF16), qh)
            dkv_ref[MEM_HEADS + h] += _dot(p.T.astype(BF16), doh)

    kv_spec = pl.BlockSpec((2 * MEM_HEADS, mlen, MEM_HD), lambda i: (0, 0, 0))
    return pl.pallas_call(
        body, grid=(s // tm,), in_specs=[_row_spec(tm, d), kv_spec, _row_spec(tm, d)],
        out_specs=[_row_spec(tm, d), kv_spec],
        out_shape=[jax.ShapeDtypeStruct((s, d), BF16), jax.ShapeDtypeStruct(kv.shape, F32)],
        compiler_params=_cparams(1, VMEM_LIMIT_MB), name=name)(q, kv, do)


def _proj_bwd_dx(parts, w_in, dz, *, name, job=None, ln_bwd=None):
    s = dz.shape[0]
    d = D_MODEL
    tm = min(256, s)
    widths = [p.shape[1] for p in parts]

    n_parts = len(parts)

    def body(*refs):
        part_refs, w_ref, dz_ref = refs[:n_parts], refs[n_parts], refs[n_parts + 1]
        acc = ALPHA * dz_ref[...]
        off = 0
        for p_ref, wd in zip(part_refs, widths):
            acc = acc + _dot(p_ref[...].astype(BF16), w_ref[:, off:off + wd], NT)
            off += wd
        if ln_bwd is not None:
            x_ref, f_ref, g_ref, b_ref = refs[n_parts + 2:n_parts + 6]
            dzp, dg, db = _ln_bwd_tile(x_ref[...], f_ref[...], ln_bwd[2], g_ref[...], b_ref[...], acc)
            _ln_bwd_store(pl.program_id(0) == 0, dzp, dg, db, *refs[n_parts + 6:])
        else:
            refs[-1][...] = acc

    in_specs = [_row_spec(tm, wd) for wd in widths] + [pl.BlockSpec(w_in.shape, lambda i: (0, 0)), _row_spec(tm, d)]
    args = (*parts, w_in, dz)
    if ln_bwd is not None:
        ln_in, out_specs = _ln_bwd_specs(tm, d)
        in_specs, out_shape = in_specs + ln_in, _ln_bwd_shapes(s, d)
        args += (ln_bwd[0], ln_bwd[1], ln_bwd[3], ln_bwd[4])
    else:
        out_specs, out_shape = [_row_spec(tm, d)], [jax.ShapeDtypeStruct((s, d), F32)]
    outs, job_out = _pallas_with_job(
        body, job, lambda: pl.program_id(0), s // tm, grid=(s // tm,), in_specs=in_specs, out_specs=out_specs,
        out_shape=out_shape, compiler_params=_cparams(1, VMEM_LIMIT_MB), name=name, args=args)
    return (outs[0] if ln_bwd is None else outs), job_out


def _proj_fwd(hb, w_in, *, name):
    s, d = hb.shape
    tm = min(512, s)
    lru, sb = 2 * LRU_W, SB_W

    def body(h_ref, w_ref, xbg_ref, q_ref, kv_ref):
        h = h_ref[...]
        xbg_ref[...] = _dot(h, w_ref[:, 0:lru])
        q_ref[...] = (_dot(h, w_ref[:, lru:lru + sb]) * (1.0 / math.sqrt(SB_HEAD))).astype(BF16)
        kv_ref[...] = _dot(h, w_ref[:, lru + sb:lru + 3 * sb]).astype(BF16)

    return pl.pallas_call(
        body, grid=(s // tm,), in_specs=[_row_spec(tm, d), pl.BlockSpec(w_in.shape, lambda i: (0, 0))],
        out_specs=[_row_spec(tm, lru), _row_spec(tm, sb), _row_spec(tm, 2 * sb)],
        out_shape=[jax.ShapeDtypeStruct((s, lru), F32), jax.ShapeDtypeStruct((s, sb), BF16),
                   jax.ShapeDtypeStruct((s, 2 * sb), BF16)],
        compiler_params=_cparams(1, VMEM_LIMIT_MB), name=name)(hb, w_in)


def _mix_bwd(dzb, w_out, yl, ys, gl, gs, *, name, job=None):
    s, c = yl.shape
    tm = min(512, s)

    def body(dz_ref, w_ref, yl_ref, ys_ref, gl_ref, gs_ref, dyl_ref, dys_ref, dgl_ref, dgs_ref):
        @pl.when(pl.program_id(0) == 0)
        def _():
            dgl_ref[...] = jnp.zeros_like(dgl_ref)
            dgs_ref[...] = jnp.zeros_like(dgs_ref)

        dy = _dot(dz_ref[...], w_ref[...], NT)
        _, vjp = jax.vjp(_mixnorm_math, yl_ref[...], ys_ref[...], gl_ref[...], gs_ref[...])
        dyl, dys, dgl, dgs = vjp((dy[:, 0:c], dy[:, c:2 * c]))
        dyl_ref[...] = dyl
        dys_ref[...] = dys
        dgl_ref[...] += dgl
        dgs_ref[...] += dgs

    return _pallas_with_job(
        body, job, lambda: pl.program_id(0), s // tm, grid=(s // tm,),
        in_specs=[_row_spec(tm, 2 * c), pl.BlockSpec(w_out.shape, lambda i: (0, 0)), _row_spec(tm, c), _row_spec(tm, c),
                  _par_spec(c), _par_spec(c)],
        out_specs=[_row_spec(tm, c), _row_spec(tm, c), _par_spec(c), _par_spec(c)],
        out_shape=[jax.ShapeDtypeStruct((s, c), F32)] * 2 + [jax.ShapeDtypeStruct((1, c), F32)] * 2,
        compiler_params=_cparams(1, VMEM_LIMIT_MB), name=name, args=(dzb, w_out, yl, ys, gl, gs))


def _proj_dw(h, parts, *, name):
    s, d = h.shape
    tk = min(1024, s)
    widths = [p.shape[1] for p in parts]

    def body(*refs):
        h_ref, part_refs, o_ref, acc_ref = refs[0], refs[1:-2], refs[-2], refs[-1]
        k = pl.program_id(0)

        @pl.when(k == 0)
        def _():
            acc_ref[...] = jnp.zeros_like(acc_ref)

        ht = h_ref[...].astype(F32).T.astype(BF16)
        off = 0
        for p_ref, wd in zip(part_refs, widths):
            acc_ref[:, off:off + wd] += _dot(ht, p_ref[...].astype(BF16))
            off += wd

        @pl.when(k == s // tk - 1)
        def _():
            o_ref[...] = acc_ref[...].astype(BF16)

    return pl.pallas_call(
        body, grid=(s // tk,), in_specs=[_row_spec(tk, d)] + [_row_spec(tk, wd) for wd in widths],
        out_specs=pl.BlockSpec((d, sum(widths)), lambda k: (0, 0)),
        out_shape=jax.ShapeDtypeStruct((d, sum(widths)), BF16),
        scratch_shapes=[pltpu.VMEM((d, sum(widths)), F32)],
        compiler_params=_cparams(1, VMEM_LIMIT_MB), name=name)(h, *parts)


def _mesh_pos():
    return lax.axis_index("x"), lax.axis_index("y"), lax.axis_index("c")


class _GatherJob:
    n_phases = 3

    def __init__(self, shards):
        n = len(shards)
        self.inputs = list(shards)
        self.out_shape = [jax.ShapeDtypeStruct((N_DEV,) + a.shape, a.dtype) for a in shards]
        self.scratch = [pltpu.SemaphoreType.DMA((n, 7)), pltpu.SemaphoreType.DMA((n, 7)), pltpu.SemaphoreType.DMA((n,))]

    def phase(self, k, ins, outs, sems):
        send_sems, recv_sems, local_sems = sems
        n = len(ins)
        x, y, c = _mesh_pos()
        me, sibling = (x, y, c), (x, y, 1 - c)
        chips = [(1 - x, y), (x, 1 - y), (1 - x, 1 - y)]

        def copy(a, slot, block, to, src=None):
            dst = outs[a].at[4 * block[0] + 2 * block[1] + block[2]]
            return pltpu.make_async_remote_copy(
                src_ref=dst if src is None else src, dst_ref=dst,
                send_sem=send_sems.at[a, slot], recv_sem=recv_sems.at[a, slot], device_id=to, device_id_type=MESH)

        def mine():
            return [pltpu.make_async_copy(ins[a], outs[a].at[4 * x + 2 * y + c], local_sems.at[a]) for a in range(n)]

        def first():
            return [cp for a in range(n) for cp in
                    [copy(a, 0, me, sibling, src=ins[a])] + [copy(a, 1 + j, me, (*chip, c), src=ins[a]) for j, chip in enumerate(chips)]]

        def passed(a, j):
            return copy(a, 4 + j, (*chips[j], c), sibling)

        if k == 0:
            for cp in mine() + first():
                cp.start()
        elif k == 1:
            for a in range(n):
                for j, chip in enumerate(chips):
                    copy(a, 1 + j, (*chip, c), me).wait_recv()
                    passed(a, j).start()
        else:
            for a in range(n):
                copy(a, 0, sibling, me).wait_recv()
                for j, chip in enumerate(chips):
                    copy(a, 4 + j, (*chip, 1 - c), me).wait_recv()
            for cp in first() + [passed(a, j) for a in range(n) for j in range(3)]:
                cp.wait_send()
            for cp in mine():
                cp.wait()


class _SiblingJob:
    n_phases = 2

    def __init__(self, grads):
        n, half = len(grads), N_DEV // 2
        self.inputs = list(grads)
        self.out_shape = [jax.ShapeDtypeStruct((half,) + g.shape[1:], g.dtype) for g in grads]
        self.scratch = [pltpu.SemaphoreType.DMA((n, half)), pltpu.SemaphoreType.DMA((n, half))]

    def phase(self, k, ins, outs, sems):
        send_sems, recv_sems = sems
        x, y, c = _mesh_pos()
        copies = [pltpu.make_async_remote_copy(
            src_ref=ins[a].at[2 * j + (1 - c)], dst_ref=outs[a].at[j], send_sem=send_sems.at[a, j],
            recv_sem=recv_sems.at[a, j], device_id=(x, y, 1 - c), device_id_type=MESH)
            for a in range(len(ins)) for j in range(N_DEV // 2)]
        for cp in copies:
            if k == 0:
                cp.start()
            else:
                cp.wait()


class _ChipJob:
    n_phases = 2

    def __init__(self, partials):
        n, half = len(partials), N_DEV // 2
        self.inputs = list(partials)
        self.out_shape = [jax.ShapeDtypeStruct(p.shape, p.dtype) for p in partials]
        self.scratch = [pltpu.SemaphoreType.DMA((n, half)), pltpu.SemaphoreType.DMA((n, half)), pltpu.SemaphoreType.DMA((n,))]

    def phase(self, k, ins, outs, sems):
        send_sems, recv_sems, local_sems = sems
        n, half = len(ins), N_DEV // 2
        x, y, c = _mesh_pos()
        jme = 2 * x + y
        local = [pltpu.make_async_copy(ins[a].at[jme], outs[a].at[jme], local_sems.at[a]) for a in range(n)]
        for cp in local:
            if k == 0:
                cp.start()
            else:
                cp.wait()
        for j in range(half):
            @pl.when(jme != j)
            def _(j=j):
                for a in range(n):
                    if k == 0:
                        pltpu.make_async_remote_copy(
                            src_ref=ins[a].at[j], dst_ref=outs[a].at[jme], send_sem=send_sems.at[a, j],
                            recv_sem=recv_sems.at[a, jme], device_id=(j // 2, j % 2, c), device_id_type=MESH).start()
                    else:
                        arrive = pltpu.make_async_remote_copy(
                            src_ref=ins[a].at[j], dst_ref=outs[a].at[j], send_sem=send_sems.at[a, j],
                            recv_sem=recv_sems.at[a, j], device_id=(j // 2, j % 2, c), device_id_type=MESH)
                        arrive.wait_recv()
                        arrive.wait_send()


def _phase_steps(n_phases, n_steps):
    return [0, n_steps - 1] if n_phases == 2 else [0, n_steps // 2, n_steps - 1]


def _pallas_with_job(body, job, step_fn, n_steps, *, grid, in_specs, out_specs, out_shape, scratch_shapes=(),
                     compiler_params, name, args):
    in_specs, out_specs, out_shape, scratch_shapes = list(in_specs), list(out_specs), list(out_shape), list(scratch_shapes)
    if job is None:
        outs = pl.pallas_call(body, grid=grid, in_specs=in_specs, out_specs=out_specs, out_shape=out_shape,
                              scratch_shapes=scratch_shapes, compiler_params=compiler_params, name=name)(*args)
        return list(outs), None
    ni, no, ns = len(in_specs), len(out_specs), len(scratch_shapes)
    ji, jo = len(job.inputs), len(job.out_shape)
    steps = _phase_steps(job.n_phases, n_steps)

    def wrapped(*refs):
        own_in, job_in = refs[:ni], refs[ni:ni + ji]
        own_out, job_out = refs[ni + ji:ni + ji + no], refs[ni + ji + no:ni + ji + no + jo]
        own_scr, job_scr = refs[ni + ji + no + jo:ni + ji + no + jo + ns], refs[ni + ji + no + jo + ns:]
        step = step_fn()
        for k, at in enumerate(steps):
            @pl.when(step == at)
            def _(k=k):
                job.phase(k, job_in, job_out, job_scr)
        body(*own_in, *own_out, *own_scr)

    any_spec = pl.BlockSpec(memory_space=pl.ANY)
    outs = pl.pallas_call(
        wrapped, grid=grid, in_specs=in_specs + [any_spec] * ji, out_specs=out_specs + [any_spec] * jo,
        out_shape=out_shape + list(job.out_shape), scratch_shapes=scratch_shapes + list(job.scratch),
        compiler_params=compiler_params, name=name)(*args, *job.inputs)
    return list(outs[:no]), list(outs[no:])


def _run_job(job, *, name):
    ji, jo = len(job.inputs), len(job.out_shape)

    def body(*refs):
        for k in range(job.n_phases):
            job.phase(k, refs[:ji], refs[ji:ji + jo], refs[ji + jo:])

    any_spec = pl.BlockSpec(memory_space=pl.ANY)
    return pl.pallas_call(body, in_specs=[any_spec] * ji, out_specs=[any_spec] * jo, out_shape=list(job.out_shape),
                          scratch_shapes=list(job.scratch), name=name)(*job.inputs)


def _pair_add(grad, recv, cj_idx, *, name):
    half, r, cdim = recv.shape
    tr = _row_tile(r, 512)

    def body(cj_ref, g_ref, r_ref, own_ref, tb_ref):
        t = g_ref[...].astype(F32) + r_ref[...].astype(F32)
        tb_ref[...] = t.astype(BF16)

        @pl.when(pl.program_id(1) == cj_ref[1])
        def _():
            own_ref[...] = t

    grid_spec = pltpu.PrefetchScalarGridSpec(
        num_scalar_prefetch=1, grid=(r // tr, half),
        in_specs=[pl.BlockSpec((None, tr, cdim), lambda i, j, cj: (2 * j + cj[0], i, 0)),
                  pl.BlockSpec((None, tr, cdim), lambda i, j, cj: (j, i, 0))],
        out_specs=[pl.BlockSpec((tr, cdim), lambda i, j, cj: (i, 0)),
                   pl.BlockSpec((None, tr, cdim), lambda i, j, cj: (j, i, 0))])
    return pl.pallas_call(
        body, grid_spec=grid_spec,
        out_shape=[jax.ShapeDtypeStruct((r, cdim), F32), jax.ShapeDtypeStruct(recv.shape, BF16)],
        compiler_params=_cparams(2), name=name)(cj_idx, grad, recv)


def _adam_math(w, g, m, v):
    m = ADAM_B1 * m + (1.0 - ADAM_B1) * g
    v = ADAM_B2 * v + (1.0 - ADAM_B2) * (g * g)
    m_hat = m / (1.0 - ADAM_B1 ** ADAM_STEP)
    v_hat = v / (1.0 - ADAM_B2 ** ADAM_STEP)
    delta = -ADAM_LR * (m_hat / (jnp.sqrt(v_hat) + ADAM_EPS) + ADAM_WD * w)
    return delta, m, v


def _reduce_adam(own, recv, j_idx, w, m, v, *, name):
    half, r, cdim = recv.shape
    tr = _row_tile(r, 256)

    def body(j_ref, own_ref, recv_ref, w_ref, m_ref, v_ref, g_ref, d_ref, nm_ref, nv_ref):
        jme = j_ref[0]
        g = own_ref[...]
        for sl in range(half):
            g = g + jnp.where(sl == jme, 0.0, recv_ref[sl].astype(F32))
        delta, nm, nv = _adam_math(w_ref[...], g, m_ref[...], v_ref[...])
        g_ref[...] = g
        d_ref[...] = delta
        nm_ref[...] = nm
        nv_ref[...] = nv

    tile = pl.BlockSpec((tr, cdim), lambda i, j_ref: (i, 0))
    grid_spec = pltpu.PrefetchScalarGridSpec(
        num_scalar_prefetch=1, grid=(r // tr,),
        in_specs=[tile, pl.BlockSpec((half, tr, cdim), lambda i, j_ref: (0, i, 0)), tile, tile, tile],
        out_specs=[tile] * 4)
    return pl.pallas_call(
        body, grid_spec=grid_spec, out_shape=[jax.ShapeDtypeStruct((r, cdim), F32)] * 4,
        compiler_params=_cparams(1), name=name)(j_idx, own, recv, w, m, v)


class _SmallGatherJob:
    n_phases = 2

    def __init__(self, pack):
        self.inputs = [pack]
        self.out_shape = [jax.ShapeDtypeStruct((N_DEV,) + pack.shape, pack.dtype)]
        self.scratch = [pltpu.SemaphoreType.DMA((N_DEV,)), pltpu.SemaphoreType.DMA((N_DEV,)), pltpu.SemaphoreType.DMA(())]

    def phase(self, k, ins, outs, sems):
        (pack,), (land,), (send_sems, recv_sems, local_sem) = ins, outs, sems
        x, y, c = _mesh_pos()
        me = 4 * x + 2 * y + c
        mine = pltpu.make_async_copy(pack, land.at[me], local_sem)
        if k == 0:
            mine.start()
        else:
            mine.wait()
        for t in range(N_DEV):
            @pl.when(me != t)
            def _(t=t):
                peer = (t // 4, (t // 2) % 2, t % 2)
                if k == 0:
                    pltpu.make_async_remote_copy(
                        src_ref=pack, dst_ref=land.at[me], send_sem=send_sems.at[t], recv_sem=recv_sems.at[me],
                        device_id=peer, device_id_type=MESH).start()
                else:
                    cp = pltpu.make_async_remote_copy(
                        src_ref=pack, dst_ref=land.at[t], send_sem=send_sems.at[t], recv_sem=recv_sems.at[t],
                        device_id=peer, device_id_type=MESH)
                    cp.wait_recv()
                    cp.wait_send()


class _Jobs:
    def __init__(self, jobs):
        self.jobs = jobs
        self.n_phases = jobs[0].n_phases
        assert all(j.n_phases == self.n_phases for j in jobs)
        self.inputs = [a for j in jobs for a in j.inputs]
        self.out_shape = [o for j in jobs for o in j.out_shape]
        self.scratch = [sc for j in jobs for sc in j.scratch]

    def phase(self, k, ins, outs, sems):
        i = o = sc = 0
        for j in self.jobs:
            ni, no, nsc = len(j.inputs), len(j.out_shape), len(j.scratch)
            j.phase(k, ins[i:i + ni], outs[o:o + no], sems[sc:sc + nsc])
            i, o, sc = i + ni, o + no, sc + nsc


def _adam_small(w, land, m, v, *, name):
    def body(w_ref, land_ref, m_ref, v_ref, g_ref, d_ref, nm_ref, nv_ref):
        total = land_ref[0]
        for t in range(1, N_DEV):
            total = total + land_ref[t]
        g_ref[...] = total
        d_ref[...], nm_ref[...], nv_ref[...] = _adam_math(w_ref[...], total, m_ref[...], v_ref[...])

    vmem = pl.BlockSpec(memory_space=pltpu.VMEM)
    return pl.pallas_call(
        body, in_specs=[vmem] * 4, out_specs=[vmem] * 4, out_shape=[jax.ShapeDtypeStruct(w.shape, F32)] * 4,
        name=name)(w, land, m, v)


def _block_diag(w):
    h, a, b = w.shape
    eye = jnp.eye(h, dtype=w.dtype)
    return (eye[:, None, :, None] * w[:, :, None, :]).reshape(h * a, h * b)


def _diag_blocks(w, h):
    a = w.shape[0] // h
    return jnp.stack([w[i * a:(i + 1) * a, i * a:(i + 1) * a] for i in range(h)])


def _local_step(x, mem, target, p, hooks=None):
    g = {}
    p = dict(p)
    w_rg, w_ig = _block_diag(p["w_rgate"]).astype(BF16), _block_diag(p["w_igate"]).astype(BF16)

    (gu1, f1, xb, h1, h1b), arrived = _ffn_fwd(x, p["ffn1_w13"], p["ffn1_w2"], name="ffn1_fwd", ln=(p["ln1_g"], p["ln1_b"]),
                                               job=hooks and hooks.gather_job(_GATHER_MIXER))
    if hooks:
        p.update(hooks.gathered(_GATHER_MIXER, arrived))
    w_in = p["w_in"]
    xbg, q_sb, kv_sb = _proj_fwd(h1b, w_in, name="proj_fwd")
    xc, hseq, y_lru = _lru_fwd(xbg, p["conv_w"], p["conv_b"], w_rg, p["b_rgate"], w_ig, p["b_igate"],
                               p["lru_lambda"], name="lru_fwd")
    y_sb, arrived = _sb_fwd(q_sb, kv_sb, name="sb_fwd", job=hooks and hooks.gather_job(_GATHER_FFN2))
    if hooks:
        p.update(hooks.gathered(_GATHER_FFN2, arrived))
    ymix = _mixnorm_fwd(y_lru, y_sb, p["g_lru"], p["g_sb"], name="mixnorm_fwd")
    mix, h2, h2b = _matmul(ymix, p["w_out"], ln=(h1, 1.0, p["ln2_g"], p["ln2_b"]), name="mix_out_ln2")
    qm = _matmul(h2b, p["mem_wq"], out_dtype=BF16, name="cross_q")
    kv = _matmul(mem, p["mem_wkv"], name="cross_kv")
    o_cross = _cross_fwd(qm, kv, name="cross_fwd")
    cross, h3, h3b = _matmul(o_cross, p["mem_wo"], ln=(h2, 1.0, p["ln3_g"], p["ln3_b"]), name="cross_out_ln3")
    (gu2, dz4, dz4b, g["ln4_g"], g["ln4_b"], loss), _ = _ffn_fwd(
        h3b, p["ffn2_w13"], p["ffn2_w2"], loss_head=(h3, target, p["ln4_g"], p["ln4_b"]), name="ffn2_fwd_loss")
    (dz3, dz3b, dgu2, act2, g["ln3_g"], g["ln3_b"]), _ = _ffn_bwd(
        dz4, gu2, p["ffn2_w13"], p["ffn2_w2"], ln_bwd=(h2, cross, 1.0, p["ln3_g"], p["ln3_b"]), name="ffn2_bwd_ln3")
    g["ffn2_w13"] = _matmul(h3b, dgu2, trans_a=True, tm=1024, tn=FF_CHUNK, tk=2048, out_dtype=BF16, name="ffn2_dw13")
    g["ffn2_w2"] = _matmul(act2, dz4b, trans_a=True, tm=FF_CHUNK, tn=1024, tk=2048, scale=0.5, out_dtype=BF16, name="ffn2_dw2")
    g["mem_wo"] = _matmul(o_cross, dz3b, trans_a=True, tm=1024, tn=1024, tk=1024, out_dtype=BF16, name="cross_dwo")
    do_cross = _matmul(dz3b, p["mem_wo"], trans_b=True, out_dtype=BF16, name="cross_do")
    dqm, dkv = _cross_bwd(qm, kv, do_cross, name="cross_bwd")
    g["mem_wq"] = _matmul(h2b, dqm, trans_a=True, tm=1024, tn=1024, tk=1024, out_dtype=BF16, name="cross_dwq")
    g["mem_wkv"] = _matmul(mem, dkv, trans_a=True, tm=1024, tn=MEM_HD, tk=256, out_dtype=BF16, name="cross_dwkv")
    dz2, dz2b, g["ln2_g"], g["ln2_b"] = _matmul(dqm, p["mem_wq"], trans_b=True, res=dz3, res_coeff=ALPHA,
                                                ln_bwd=(h1, mix, 1.0, p["ln2_g"], p["ln2_b"]), name="cross_dh_ln2")
    g["w_out"] = _matmul(ymix, dz2b, trans_a=True, tm=1024, tn=1024, tk=1024, out_dtype=BF16, name="mix_dwout")
    (dy_lru, dy_sb, g["g_lru"], g["g_sb"]), arrived = _mix_bwd(
        dz2b, p["w_out"], y_lru, y_sb, p["g_lru"], p["g_sb"], name="mix_bwd", job=hooks and hooks.sibling_job(_REDUCE_EARLY, g))
    (dq, dk, dv), arrived = _sb_bwd(q_sb, kv_sb, y_sb, dy_sb, name="sb_bwd",
                                    job=hooks and hooks.chip_job(_REDUCE_EARLY, arrived))
    if hooks:
        hooks.reduced(_REDUCE_EARLY, arrived)
    dxc, dgate, dwr, dwi, g["b_rgate"], g["b_igate"], g["lru_lambda"] = _lru_bwd(
        xc, xbg, hseq, dy_lru, w_rg, p["b_rgate"], w_ig, p["b_igate"], p["lru_lambda"], name="lru_bwd")
    g["w_rgate"], g["w_igate"] = _diag_blocks(dwr, 8), _diag_blocks(dwi, 8)
    dxb, g["conv_w"], g["conv_b"] = _conv_bwd(dxc, xbg, p["conv_w"], name="conv_bwd")
    parts = [dxb, dgate, dq, dk, dv]
    g["w_in"] = _proj_dw(h1b, parts, name="proj_dw")
    (dz1, dz1b, g["ln1_g"], g["ln1_b"]), arrived = _proj_bwd_dx(
        parts, w_in, dz2, ln_bwd=(x, f1, 0.5, p["ln1_g"], p["ln1_b"]), name="proj_dh_ln1",
        job=hooks and hooks.sibling_job(['w_in'], g))
    (grad_x, dgu1, act1), _ = _ffn_bwd(dz1, gu1, p["ffn1_w13"], p["ffn1_w2"], name="ffn1_bwd")
    dw13 = dict(trans_a=True, tm=1024, tn=FF_CHUNK, tk=2048, out_dtype=BF16, name="ffn1_dw13")
    dw2 = dict(trans_a=True, tm=FF_CHUNK, tn=1024, tk=2048, scale=0.5, out_dtype=BF16, name="ffn1_dw2")
    if not hooks:
        g["ffn1_w13"] = _matmul(xb, dgu1, **dw13)
        g["ffn1_w2"] = _matmul(act1, dz1b, **dw2)
        return loss, grad_x, g
    g["ffn1_w13"], arrived = _matmul(xb, dgu1, job=_Jobs([hooks.chip_job(['w_in'], arrived), hooks.small_job(g, loss)]), **dw13)
    hooks.reduced(['w_in'], arrived[:1])
    hooks.small_land = arrived[1]
    arrived = _run_job(hooks.sibling_job(['ffn1_w13'], g), name="reduce_sibling_w13")
    g["ffn1_w2"], arrived = _matmul(act1, dz1b, job=hooks.chip_job(['ffn1_w13'], arrived), **dw2)
    hooks.reduced(['ffn1_w13'], arrived)
    arrived = _run_job(hooks.sibling_job(['ffn1_w2'], g), name="reduce_sibling_w2")
    hooks.reduced(['ffn1_w2'], _run_job(hooks.chip_job(['ffn1_w2'], arrived), name="reduce_chips_w2"))
    return loss, grad_x, g


_WEIGHTS = ['ffn1_w13', 'ffn1_w2', 'ln1_g', 'ln1_b', 'w_in', 'conv_w', 'conv_b', 'w_rgate', 'b_rgate', 'w_igate',
            'b_igate', 'lru_lambda', 'g_lru', 'g_sb', 'w_out', 'ln2_g', 'ln2_b', 'mem_wq', 'mem_wkv', 'mem_wo',
            'ln3_g', 'ln3_b', 'ffn2_w13', 'ffn2_w2', 'ln4_g', 'ln4_b']
_SHARDED = ['ffn1_w13', 'ffn1_w2', 'w_in', 'w_out', 'mem_wq', 'mem_wkv', 'mem_wo', 'ffn2_w13', 'ffn2_w2']
_SMALL = ['ln1_g', 'ln1_b', 'ln2_g', 'ln2_b', 'ln3_g', 'ln3_b', 'ln4_g', 'ln4_b', 'conv_b', 'b_rgate', 'b_igate',
          'lru_lambda', 'g_lru', 'g_sb', 'w_rgate', 'w_igate']


def _pack_small(d, conv_w_full=None, loss=None):
    rows = [d[n].reshape(-1) for n in _SMALL]
    if conv_w_full is not None:
        rows.append(conv_w_full.reshape(-1))
    if loss is not None:
        rows.append(loss.reshape(-1))
    flat = jnp.concatenate(rows)
    pad = (-flat.shape[0]) % (8 * D_MODEL)
    return jnp.pad(flat, (0, pad)).reshape(-1, D_MODEL)


def _unpack_small(pack, like, with_conv):
    flat = pack.reshape(-1)
    out, off = {}, 0
    for n in _SMALL:
        size = math.prod(like[n].shape)
        out[n] = flat[off:off + size].reshape(like[n].shape)
        off += size
    conv = flat[off:off + CONV_W * LRU_W].reshape(CONV_W, LRU_W) if with_conv else None
    return out, conv


_GATHER_FIRST = ['ffn1_w13', 'ffn1_w2']
_GATHER_MIXER = ['w_in', 'conv_w', 'w_out', 'mem_wq', 'mem_wkv', 'mem_wo']
_GATHER_FFN2 = ['ffn2_w13', 'ffn2_w2']
_REDUCE_EARLY = ['ffn2_w13', 'ffn2_w2', 'mem_wo', 'mem_wq', 'mem_wkv', 'w_out']


def _weight_layout(n, full):
    if n in ('ffn1_w13', 'ffn2_w13', 'mem_wkv'):
        return full
    if n == 'w_in':
        return full.transpose(1, 0, 2).reshape(D_MODEL, -1)
    if n == 'conv_w':
        return full.transpose(1, 0, 2).reshape(CONV_W, LRU_W)
    return full.reshape(-1, full.shape[-1])


def _grad_blocks(n, g):
    if n == 'w_in':
        return g.reshape(D_MODEL, N_DEV, -1).transpose(1, 0, 2)
    return g.reshape((N_DEV, -1, g.shape[-1]))


class _Hooks:
    def __init__(self, w, cj_idx):
        self.w, self.cj_idx = w, cj_idx
        self.sums, self.recv = {}, {}

    def shard(self, n):
        return self.w[n][0] if n == 'conv_w' else self.w[n][0].astype(BF16)

    def gather_job(self, names):
        return _GatherJob([self.shard(n) for n in names])

    def gathered(self, names, outs):
        return {n: _weight_layout(n, o) for n, o in zip(names, outs)}

    def sibling_job(self, names, g):
        self.blocks = [_grad_blocks(n, g[n]) for n in names]
        return _SiblingJob(self.blocks)

    def chip_job(self, names, from_sibling):
        for n, b, r in zip(names, self.blocks, from_sibling):
            self.sums[n] = _pair_add(b, r, self.cj_idx, name=f"pair_add_{n}")
        return _ChipJob([self.sums[n][1] for n in names])

    def reduced(self, names, from_chips):
        self.recv.update(zip(names, from_chips))

    def small_job(self, g, loss):
        return _SmallGatherJob(_pack_small(g, conv_w_full=g["conv_w"], loss=loss))


def kernel(x, mem, ffn1_w13, ffn1_w2, ln1_g, ln1_b, w_in, conv_w, conv_b, w_rgate, b_rgate, w_igate, b_igate, lru_lambda, g_lru, g_sb, w_out, ln2_g, ln2_b, mem_wq, mem_wkv, mem_wo, ln3_g, ln3_b, ffn2_w13, ffn2_w2, ln4_g, ln4_b, loss_target, m_ffn1_w13, m_ffn1_w2, m_ln1_g, m_ln1_b, m_w_in, m_conv_w, m_conv_b, m_w_rgate, m_b_rgate, m_w_igate, m_b_igate, m_lru_lambda, m_g_lru, m_g_sb, m_w_out, m_ln2_g, m_ln2_b, m_mem_wq, m_mem_wkv, m_mem_wo, m_ln3_g, m_ln3_b, m_ffn2_w13, m_ffn2_w2, m_ln4_g, m_ln4_b, v_ffn1_w13, v_ffn1_w2, v_ln1_g, v_ln1_b, v_w_in, v_conv_w, v_conv_b, v_w_rgate, v_b_rgate, v_w_igate, v_b_igate, v_lru_lambda, v_g_lru, v_g_sb, v_w_out, v_ln2_g, v_ln2_b, v_mem_wq, v_mem_wkv, v_mem_wo, v_ln3_g, v_ln3_b, v_ffn2_w13, v_ffn2_w2, v_ln4_g, v_ln4_b):
    args = locals()
    w = {n: args[n] for n in _WEIGHTS}
    mom = {n: args["m_" + n] for n in _WEIGHTS}
    var = {n: args["v_" + n] for n in _WEIGHTS}
    ix, iy, ic = lax.axis_index("x"), lax.axis_index("y"), lax.axis_index("c")

    j_idx = jnp.reshape(2 * ix + iy, (1,)).astype(jnp.int32)
    hooks = _Hooks(w, jnp.stack([ic, 2 * ix + iy]).astype(jnp.int32))
    p = {n: (w[n][0] if w[n].ndim == 4 else w[n]) for n in _WEIGHTS if n not in _SHARDED and n != 'conv_w'}
    p.update(hooks.gathered(_GATHER_FIRST, _run_job(hooks.gather_job(_GATHER_FIRST), name="gather_ffn1")))
    _, grad_x, g = _local_step(x[0], mem[0], loss_target[0], p, hooks)

    grads, delta, new_m, new_v = {}, {}, {}, {}
    for n in _SHARDED:
        grads[n], delta[n], new_m[n], new_v[n] = (
            o.reshape(w[n].shape) for o in _reduce_adam(hooks.sums[n][0], hooks.recv[n], j_idx, w[n][0], mom[n][0],
                                                        var[n][0], name=f"adam_{n}"))

    me = 4 * ix + 2 * iy + ic
    own_cols = lambda full: lax.dynamic_slice_in_dim(full, me * (LRU_W // N_DEV), LRU_W // N_DEV, axis=1)
    land = hooks.small_land
    conv_off = sum(math.prod(w[n].shape) for n in _SMALL)
    land_conv = land.reshape(N_DEV, -1)[:, conv_off:conv_off + CONV_W * LRU_W].reshape(N_DEV, CONV_W, LRU_W)
    land_own = jnp.pad(own_cols(land_conv.reshape(N_DEV * CONV_W, LRU_W)).reshape(N_DEV, -1),
                       ((0, 0), (0, CONV_W * LRU_W - CONV_W * LRU_W // N_DEV)))
    land = lax.dynamic_update_slice(land.reshape(N_DEV, -1), land_own, (0, conv_off)).reshape(land.shape)
    pk = lambda d: _pack_small({n: d[n] for n in _SMALL}, conv_w_full=jnp.pad(d["conv_w"].reshape(-1), (0, CONV_W * LRU_W - CONV_W * LRU_W // N_DEV)))
    g_s, d_s, m_s, v_s = _adam_small(pk(w), land, pk(mom), pk(var), name="adam_small")
    for src, dst in ((g_s, grads), (d_s, delta), (m_s, new_m), (v_s, new_v)):
        vals, conv = _unpack_small(src, w, with_conv=True)
        dst.update(vals)
        dst["conv_w"] = conv.reshape(-1)[:CONV_W * LRU_W // N_DEV].reshape(w["conv_w"].shape)
    loss = g_s.reshape(-1)[conv_off + CONV_W * LRU_W]

    return (loss, grad_x[None], *[grads[n] for n in _WEIGHTS], *[delta[n] for n in _WEIGHTS],
            *[new_m[n] for n in _WEIGHTS], *[new_v[n] for n in _WEIGHTS])
```

```python
import functools
import math

import jax
import jax.numpy as jnp
from jax import lax
from jax.experimental import pallas as pl
from jax.experimental.pallas import tpu as pltpu

F32, BF16 = jnp.float32, jnp.bfloat16

D_MODEL = 1024
LRU_W = 512
SB_W = 512
SB_PAIR = 128
SB_HEAD = 64
SB_ROWS = 128
SB_SKIP = -110.0
D_FF = 2688
FF_CHUNK = 672
N_DEV = 8
MEM_HEADS = 4
MEM_HD = 256
CONV_W = 4
ALPHA = 2.0 ** 0.25
LN_EPS = 1e-5
RMS_EPS = 1e-6
LRU_C = 8.0
ADAM_LR, ADAM_B1, ADAM_B2, ADAM_EPS, ADAM_WD, ADAM_STEP = 0.001, 0.9, 0.999, 1e-08, 0.01, 10

NN = (((1,), (0,)), ((), ()))
NT = (((1,), (1,)), ((), ()))
MESH = pl.DeviceIdType.MESH
VMEM_LIMIT_MB = 56


def _dot(a, b, dn=NN):
    return lax.dot_general(a, b, dn, preferred_element_type=F32)


def _cparams(n_axes, vmem_mb=None):
    kw = dict(dimension_semantics=("arbitrary",) * n_axes)
    if vmem_mb is not None:
        kw["vmem_limit_bytes"] = vmem_mb << 20
    return pltpu.CompilerParams(**kw)


def _row_tile(rows, want):
    if rows <= want:
        return rows
    t = want - want % 8
    while rows % t:
        t -= 8
    return t


def _matmul(a, b, *, name, out_dtype=F32, trans_a=False, trans_b=False, tm=512, tn=1024, tk=1024,
            scale=1.0, res=None, res_coeff=1.0, job=None, ln=None, ln_bwd=None):
    a_chunked, b_chunked = a.ndim == 3, b.ndim == 3
    nc = a.shape[0] if a_chunked else (b.shape[0] if b_chunked else 1)
    a2, b2 = a.shape[-2:], b.shape[-2:]
    (kdim, m) = a2 if trans_a else a2[::-1]
    n = b2[0] if trans_b else b2[1]
    tm, tn, tk = min(tm, m), min(tn, n), min(tk, kdim)
    assert m % tm == 0 and n % tn == 0 and kdim % tk == 0, (name, m, n, kdim)
    nk = kdim // tk

    def a_idx(c, i, j, k):
        idx = (k, i) if trans_a else (i, k)
        return (c,) + idx if a_chunked else idx

    def b_idx(c, i, j, k):
        idx = (j, k) if trans_b else (k, j)
        return (c,) + idx if b_chunked else idx

    a_blk = (tk, tm) if trans_a else (tm, tk)
    b_blk = (tn, tk) if trans_b else (tk, tn)
    in_specs = [pl.BlockSpec(((None,) + a_blk) if a_chunked else a_blk, a_idx),
                pl.BlockSpec(((None,) + b_blk) if b_chunked else b_blk, b_idx)]
    args = [a, b]
    if res is not None:
        in_specs.append(pl.BlockSpec((None, tm, tn), lambda c, i, j, k: (c, i, j)))
        args.append(res.reshape((nc, m, n)))
    if ln is not None:
        assert nc == 1 and tn == n and res is None
        in_specs += [pl.BlockSpec((tm, tn), lambda c, i, j, k: (i, 0)), pl.BlockSpec((1, tn), lambda c, i, j, k: (0, 0)),
                     pl.BlockSpec((1, tn), lambda c, i, j, k: (0, 0))]
        args += [ln[0], ln[2], ln[3]]
    if ln_bwd is not None:
        assert nc == 1 and tn == n and res is not None and ln is None
        in_specs += [pl.BlockSpec((tm, tn), lambda c, i, j, k: (i, 0))] * 2 + [pl.BlockSpec((1, tn), lambda c, i, j, k: (0, 0))] * 2
        args += [ln_bwd[0], ln_bwd[1], ln_bwd[3], ln_bwd[4]]

    def body(*refs):
        if ln_bwd is not None:
            a_ref, b_ref, r_ref, x_ref, f_ref, g_ref, bb_ref, o_ref, dzb_ref, dg_ref, db_ref, acc_ref = refs
        elif res is not None:
            a_ref, b_ref, r_ref, o_ref, acc_ref = refs
        elif ln is not None:
            a_ref, b_ref, x_ref, g_ref, bb_ref, o_ref, h_ref, hb_ref, acc_ref = refs
        else:
            a_ref, b_ref, o_ref, acc_ref = refs
        k = pl.program_id(3)

        @pl.when(k == 0)
        def _():
            acc_ref[...] = jnp.zeros_like(acc_ref)

        av = a_ref[...]
        if trans_a:
            av = av.astype(F32).T
        acc_ref[...] += _dot(av.astype(BF16), b_ref[...].astype(BF16), NT if trans_b else NN)

        @pl.when(k == nk - 1)
        def _():
            out = acc_ref[...]
            if scale != 1.0:
                out = out * scale
            if res is not None:
                out = out + res_coeff * r_ref[...]
            if ln_bwd is not None:
                dz, dg, db = _ln_bwd_tile(x_ref[...], f_ref[...], ln_bwd[2], g_ref[...], bb_ref[...], out)
                _ln_bwd_store(pl.program_id(1) == 0, dz, dg, db, o_ref, dzb_ref, dg_ref, db_ref)
                return
            o_ref[...] = out.astype(out_dtype)
            if ln is not None:
                h = _ln_math(ALPHA * x_ref[...] + ln[1] * out, g_ref[...], bb_ref[...])
                h_ref[...] = h
                hb_ref[...] = h.astype(BF16)

    grid = (nc, m // tm, n // tn, nk)

    def step():
        idx = pl.program_id(0)
        for ax in range(1, 4):
            idx = idx * grid[ax] + pl.program_id(ax)
        return idx

    out_specs = [pl.BlockSpec((None, tm, tn), lambda c, i, j, k: (c, i, j))]
    out_shape = [jax.ShapeDtypeStruct((nc, m, n), out_dtype)]
    if ln is not None:
        out_specs += [pl.BlockSpec((tm, tn), lambda c, i, j, k: (i, 0))] * 2
        out_shape += [jax.ShapeDtypeStruct((m, n), F32), jax.ShapeDtypeStruct((m, n), BF16)]
    if ln_bwd is not None:
        out_specs += [pl.BlockSpec((tm, tn), lambda c, i, j, k: (i, 0))] + [pl.BlockSpec((1, tn), lambda c, i, j, k: (0, 0))] * 2
        out_shape += _ln_bwd_shapes(m, n)[1:]
    outs, job_out = _pallas_with_job(
        body, job, step, math.prod(grid), grid=grid, in_specs=in_specs, out_specs=out_specs, out_shape=out_shape,
        scratch_shapes=[pltpu.VMEM((tm, tn), F32)],
        compiler_params=_cparams(4, VMEM_LIMIT_MB), name=name, args=args)
    out = outs[0] if (a_chunked or b_chunked) else outs[0][0]
    if ln is not None:
        return out, outs[1], outs[2]
    if ln_bwd is not None:
        return out, outs[1], outs[2], outs[3]
    return out if job is None else (out, job_out)


def _ln_math(z, g, b):
    mu = jnp.mean(z, axis=-1, keepdims=True)
    zc = z - mu
    var = jnp.mean(zc * zc, axis=-1, keepdims=True)
    return zc * lax.rsqrt(var + LN_EPS) * g + b


def _row_spec(tm, d):
    return pl.BlockSpec((tm, d), lambda i: (i, 0))


def _par_spec(d, rows=1):
    return pl.BlockSpec((rows, d), lambda i: (0, 0))


def _ln_bwd_tile(x, f, fscale, g, b, dy):
    _, vjp = jax.vjp(_ln_math, ALPHA * x + fscale * f, g, b)
    return vjp(dy)


def _ln_bwd_specs(tm, d):
    return ([_row_spec(tm, d), _row_spec(tm, d), _par_spec(d), _par_spec(d)],
            [_row_spec(tm, d), _row_spec(tm, d), _par_spec(d), _par_spec(d)])


def _ln_bwd_shapes(s, d):
    return [jax.ShapeDtypeStruct((s, d), F32), jax.ShapeDtypeStruct((s, d), BF16),
            jax.ShapeDtypeStruct((1, d), F32), jax.ShapeDtypeStruct((1, d), F32)]


def _ln_bwd_store(first, dz, dg, db, dz_ref, dzb_ref, dg_ref, db_ref):
    @pl.when(first)
    def _():
        dg_ref[...] = jnp.zeros_like(dg_ref)
        db_ref[...] = jnp.zeros_like(db_ref)

    dz_ref[...] = dz
    dzb_ref[...] = dz.astype(BF16)
    dg_ref[...] += dg
    db_ref[...] += db


def _load_weights_once(pairs, sem):
    @pl.when(pl.program_id(0) == 0)
    def _():
        copies = [pltpu.make_async_copy(src, dst, sem.at[n]) for n, (src, dst) in enumerate(pairs)]
        for c in copies:
            c.start()
        for c in copies:
            c.wait()


def _ffn_fwd(h, w13c, w2, *, name, job=None, ln=None, loss_head=None):
    s, d = h.shape
    tm = min(256, s)
    half = N_DEV // 2

    def body(*refs):
        if loss_head is not None:
            (h_ref, w13_hbm, w2_hbm, x_ref, t_ref, g_ref, b_ref, gu_ref, dz_ref, dzb_ref, dg_ref, db_ref, loss_ref,
             w13_v, w2_v, sem) = refs
        elif ln is not None:
            h_ref, w13_hbm, w2_hbm, g_ref, b_ref, gu_ref, f_ref, hb_ref, n_ref, nb_ref, w13_v, w2_v, sem = refs
        else:
            h_ref, w13_hbm, w2_hbm, gu_ref, f_ref, hb_ref, w13_v, w2_v, sem = refs
        _load_weights_once([(w13_hbm, w13_v), (w2_hbm, w2_v)], sem)
        hb = h_ref[...].astype(BF16)
        if loss_head is None:
            hb_ref[...] = hb
        acc = jnp.zeros((tm, d), F32)
        for k in range(half):
            g = _dot(hb, w13_v[k], NT)
            u = _dot(hb, w13_v[k + half], NT)
            gu_ref[k] = g.astype(BF16)
            gu_ref[k + half] = u.astype(BF16)
            a = g * jax.nn.sigmoid(g) * u
            acc = acc + _dot(a.astype(BF16), w2_v[pl.ds(k * FF_CHUNK, FF_CHUNK), :])
        if loss_head is not None:
            first = pl.program_id(0) == 0

            @pl.when(first)
            def _():
                loss_ref[...] = jnp.zeros_like(loss_ref)

            y, vjp = jax.vjp(_ln_math, ALPHA * x_ref[...] + 0.5 * acc, g_ref[...], b_ref[...])
            err = y - t_ref[...]
            sq = jnp.sum(err * err, axis=1, keepdims=True)
            loss_ref[...] += jnp.sum(sq, axis=0, keepdims=True) * (0.5 / d)
            dz, dg, db = vjp(err * (1.0 / d))
            _ln_bwd_store(first, dz, dg, db, dz_ref, dzb_ref, dg_ref, db_ref)
            return
        f_ref[...] = acc
        if ln is not None:
            hn = _ln_math(ALPHA * h_ref[...] + 0.5 * acc, g_ref[...], b_ref[...])
            n_ref[...] = hn
            nb_ref[...] = hn.astype(BF16)

    any_spec = pl.BlockSpec(memory_space=pl.ANY)
    gu_spec = pl.BlockSpec((N_DEV, tm, FF_CHUNK), lambda i: (0, i, 0))
    gu_shape = jax.ShapeDtypeStruct((N_DEV, s, FF_CHUNK), BF16)
    scratch = [pltpu.VMEM(w13c.shape, BF16), pltpu.VMEM(w2.shape, BF16), pltpu.SemaphoreType.DMA((2,))]
    if loss_head is not None:
        return _pallas_with_job(
            body, job, lambda: pl.program_id(0), s // tm, grid=(s // tm,),
            in_specs=[_row_spec(tm, d), any_spec, any_spec, _row_spec(tm, d), _row_spec(tm, d), _par_spec(d), _par_spec(d)],
            out_specs=[gu_spec] + _ln_bwd_specs(tm, d)[1] + [pl.BlockSpec((1, 1), lambda i: (0, 0))],
            out_shape=[gu_shape] + _ln_bwd_shapes(s, d) + [jax.ShapeDtypeStruct((1, 1), F32)],
            scratch_shapes=scratch, compiler_params=_cparams(1, VMEM_LIMIT_MB), name=name,
            args=(h, w13c, w2) + tuple(loss_head))
    n_ln = 0 if ln is None else 2
    return _pallas_with_job(
        body, job, lambda: pl.program_id(0), s // tm,
        grid=(s // tm,), in_specs=[_row_spec(tm, d), any_spec, any_spec] + [_par_spec(d)] * n_ln,
        out_specs=[pl.BlockSpec((N_DEV, tm, FF_CHUNK), lambda i: (0, i, 0)), _row_spec(tm, d), _row_spec(tm, d)]
                  + [_row_spec(tm, d)] * n_ln,
        out_shape=[jax.ShapeDtypeStruct((N_DEV, s, FF_CHUNK), BF16), jax.ShapeDtypeStruct((s, d), F32),
                   jax.ShapeDtypeStruct((s, d), BF16)]
                  + [jax.ShapeDtypeStruct((s, d), F32), jax.ShapeDtypeStruct((s, d), BF16)][:n_ln],
        scratch_shapes=[pltpu.VMEM(w13c.shape, BF16), pltpu.VMEM(w2.shape, BF16), pltpu.SemaphoreType.DMA((2,))],
        compiler_params=_cparams(1, VMEM_LIMIT_MB), name=name, args=(h, w13c, w2) + (() if ln is None else tuple(ln)))


def _ffn_bwd(dz, gu, w13c, w2, *, name, job=None, ln_bwd=None):
    s, d = dz.shape
    tm = min(256, s)
    half = N_DEV // 2

    def body(*refs):
        if ln_bwd is not None:
            (dz_ref, gu_ref, w13_hbm, w2_hbm, x_ref, f_ref, g_ref, b_ref, dzp_ref, dzpb_ref, dgu_ref, act_ref,
             dg_ref, db_ref, w13_v, w2_v, sem) = refs
        else:
            dz_ref, gu_ref, w13_hbm, w2_hbm, dh_ref, dgu_ref, act_ref, w13_v, w2_v, sem = refs
        _load_weights_once([(w13_hbm, w13_v), (w2_hbm, w2_v)], sem)
        dzv = dz_ref[...]
        dfb = (0.5 * dzv).astype(BF16)
        acc = ALPHA * dzv
        for k in range(half):
            g = gu_ref[k].astype(F32)
            u = gu_ref[k + half].astype(F32)
            da = _dot(dfb, w2_v[pl.ds(k * FF_CHUNK, FF_CHUNK), :], NT)
            sg = jax.nn.sigmoid(g)
            silu = g * sg
            dg = (da * u * (sg * (1.0 + g * (1.0 - sg)))).astype(BF16)
            du = (da * silu).astype(BF16)
            act_ref[k] = (silu * u).astype(BF16)
            dgu_ref[k] = dg
            dgu_ref[k + half] = du
            acc = acc + _dot(dg, w13_v[k]) + _dot(du, w13_v[k + half])
        if ln_bwd is not None:
            dzp, dg, db = _ln_bwd_tile(x_ref[...], f_ref[...], ln_bwd[2], g_ref[...], b_ref[...], acc)
            _ln_bwd_store(pl.program_id(0) == 0, dzp, dg, db, dzp_ref, dzpb_ref, dg_ref, db_ref)
        else:
            dh_ref[...] = acc

    any_spec = pl.BlockSpec(memory_space=pl.ANY)
    in_specs = [_row_spec(tm, d), pl.BlockSpec((N_DEV, tm, FF_CHUNK), lambda i: (0, i, 0)), any_spec, any_spec]
    mid_specs = [pl.BlockSpec((N_DEV, tm, FF_CHUNK), lambda i: (0, i, 0)), pl.BlockSpec((half, tm, FF_CHUNK), lambda i: (0, i, 0))]
    mid_shapes = [jax.ShapeDtypeStruct((N_DEV, s, FF_CHUNK), BF16), jax.ShapeDtypeStruct((half, s, FF_CHUNK), BF16)]
    args = (dz, gu, w13c, w2)
    if ln_bwd is not None:
        ln_in, ln_out = _ln_bwd_specs(tm, d)
        in_specs, out_specs = in_specs + ln_in, ln_out[:2] + mid_specs + ln_out[2:]
        shapes = _ln_bwd_shapes(s, d)
        out_shape = shapes[:2] + mid_shapes + shapes[2:]
        args += (ln_bwd[0], ln_bwd[1], ln_bwd[3], ln_bwd[4])
    else:
        out_specs, out_shape = [_row_spec(tm, d)] + mid_specs, [jax.ShapeDtypeStruct((s, d), F32)] + mid_shapes
    return _pallas_with_job(
        body, job, lambda: pl.program_id(0), s // tm, grid=(s // tm,), in_specs=in_specs, out_specs=out_specs,
        out_shape=out_shape,
        scratch_shapes=[pltpu.VMEM(w13c.shape, BF16), pltpu.VMEM(w2.shape, BF16), pltpu.SemaphoreType.DMA((2,))],
        compiler_params=_cparams(1, VMEM_LIMIT_MB), name=name, args=args)


def _shift_down(x, prev8, d):
    t, c = x.shape
    row = lax.broadcasted_iota(jnp.int32, (t, c), 0)
    xr = pltpu.roll(x, d, 0)
    pr = pltpu.roll(prev8, d, 0)
    if t > 8:
        pr = jnp.concatenate([pr, jnp.zeros((t - 8, c), x.dtype)], axis=0)
    return jnp.where(row < d, pr, xr)


def _shift_up(x, next8, d):
    t, c = x.shape
    row = lax.broadcasted_iota(jnp.int32, (t, c), 0)
    xr = pltpu.roll(x, t - d, 0)
    nr = pltpu.roll(next8, 8 - d, 0)
    if t > 8:
        nr = jnp.concatenate([jnp.zeros((t - 8, c), x.dtype), nr], axis=0)
    return jnp.where(row >= t - d, nr, xr)


def _scan_fwd(a, u, carry):
    t = a.shape[0]
    row = lax.broadcasted_iota(jnp.int32, a.shape, 0) % 8
    for d in (1, 2, 4):
        a_s, u_s = pltpu.roll(a, d, 0), pltpu.roll(u, d, 0)
        m = row >= d
        u = jnp.where(m, u + a * u_s, u)
        a = jnp.where(m, a * a_s, a)
    out = []
    for g in range(t // 8):
        hg = u[8 * g:8 * g + 8, :] + a[8 * g:8 * g + 8, :] * carry
        out.append(hg)
        carry = hg[7:8, :]
    return jnp.concatenate(out, axis=0)


def _scan_bwd(b, x, carry):
    t = b.shape[0]
    row = lax.broadcasted_iota(jnp.int32, b.shape, 0) % 8
    for d in (1, 2, 4):
        b_s, x_s = pltpu.roll(b, t - d, 0), pltpu.roll(x, t - d, 0)
        m = row < 8 - d
        x = jnp.where(m, x + b * x_s, x)
        b = jnp.where(m, b * b_s, b)
    out = [None] * (t // 8)
    for g in reversed(range(t // 8)):
        lg = x[8 * g:8 * g + 8, :] + b[8 * g:8 * g + 8, :] * carry
        out[g] = lg
        carry = lg[0:1, :]
    return jnp.concatenate(out, axis=0)


def _lru_elem(xc, pr, pi, b_r, b_i, lam):
    r = jax.nn.sigmoid(pr + b_r)
    ig = jax.nn.sigmoid(pi + b_i)
    softplus_neg_lam = jnp.maximum(-lam, 0.0) + jnp.log1p(jnp.exp(-jnp.abs(lam)))
    log_a = (-LRU_C) * r * softplus_neg_lam
    a = jnp.exp(log_a)
    one_minus_a2 = -jnp.tanh(log_a) * (jnp.exp(2.0 * log_a) + 1.0)
    u = jnp.sqrt(one_minus_a2) * (ig * xc)
    return a, u


def _gelu_tanh(x):
    return 0.5 * x * (1.0 + jnp.tanh(math.sqrt(2.0 / math.pi) * (x + 0.044715 * (x * x * x))))


def _conv_fwd(xb, prev8, w, b):
    out = b + w[3:4, :] * xb
    for d in range(1, CONV_W):
        out = out + w[3 - d:4 - d, :] * _shift_down(xb, prev8, d)
    return out


def _lru_fwd(xbg, conv_w, conv_b, w_rg, b_rg, w_ig, b_ig, lam, *, name):
    s = xbg.shape[0]
    c = LRU_W
    t = min(512, s)
    t8 = t // 8

    def body(xb_ref, gate_ref, prev_ref, cw_ref, cb_ref, wr_ref, br_ref, wi_ref, bi_ref, lam_ref,
             xc_ref, h_ref, y_ref, carry_ref):
        i = pl.program_id(0)

        @pl.when(i == 0)
        def _():
            carry_ref[...] = jnp.zeros_like(carry_ref)

        prev8 = jnp.where(i > 0, prev_ref[...], 0.0)
        xc = _conv_fwd(xb_ref[...], prev8, cw_ref[...], cb_ref[...])
        xcb = xc.astype(BF16)
        a, u = _lru_elem(xc, _dot(xcb, wr_ref[...]), _dot(xcb, wi_ref[...]), br_ref[...], bi_ref[...], lam_ref[...])
        h = _scan_fwd(a, u, carry_ref[0:1, :])
        carry_ref[...] = jnp.broadcast_to(h[t - 1:t, :], carry_ref.shape)
        xc_ref[...] = xc
        h_ref[...] = h
        y_ref[...] = h * _gelu_tanh(gate_ref[...])

    tile = lambda col: pl.BlockSpec((t, c), lambda i: (i, col))
    par = lambda rows: pl.BlockSpec((rows, c), lambda i: (0, 0))
    out_spec = pl.BlockSpec((t, c), lambda i: (i, 0))
    return pl.pallas_call(
        body, grid=(s // t,),
        in_specs=[tile(0), tile(1), pl.BlockSpec((8, c), lambda i: (jnp.maximum(i * t8 - 1, 0), 0)),
                  par(CONV_W), par(1), pl.BlockSpec((c, c), lambda i: (0, 0)), par(1),
                  pl.BlockSpec((c, c), lambda i: (0, 0)), par(1), par(1)],
        out_specs=[out_spec, out_spec, out_spec],
        out_shape=[jax.ShapeDtypeStruct((s, c), F32)] * 3,
        scratch_shapes=[pltpu.VMEM((8, c), F32)],
        compiler_params=_cparams(1, VMEM_LIMIT_MB), name=name)(
            xbg, xbg, xbg, conv_w, conv_b, w_rg, b_rg, w_ig, b_ig, lam)


def _lru_bwd(xc, xbg, hseq, dy, w_rg, b_rg, w_ig, b_ig, lam, *, name):
    s, c = xc.shape
    t = min(512, s)
    t8 = t // 8
    nt = s // t

    def body(xc_ref, gate_ref, h_ref, hprev_ref, dy_ref, wr_ref, br_ref, wi_ref, bi_ref, lam_ref,
             dxc_ref, dgate_ref, dwr_ref, dwi_ref, dbr_ref, dbi_ref, dlam_ref, a8_ref, l8_ref):
        i = pl.program_id(0)

        @pl.when(i == 0)
        def _():
            for r in (dwr_ref, dwi_ref, dbr_ref, dbi_ref, dlam_ref, a8_ref, l8_ref):
                r[...] = jnp.zeros_like(r)

        xc_v = xc_ref[...]
        xcb = xc_v.astype(BF16)
        (a, _), vjp = jax.vjp(_lru_elem, xc_v, _dot(xcb, wr_ref[...]), _dot(xcb, wi_ref[...]),
                              br_ref[...], bi_ref[...], lam_ref[...])
        gl, vjp_gelu = jax.vjp(_gelu_tanh, gate_ref[...])
        dyv, hs = dy_ref[...], h_ref[...]
        (dgate,) = vjp_gelu(dyv * hs)
        b_next = _shift_up(a, a8_ref[...], 1)
        lt = _scan_bwd(b_next, dyv * gl, l8_ref[0:1, :])
        a8_ref[...] = a[0:8, :]
        l8_ref[...] = lt[0:8, :]
        hprev8 = jnp.where(i < nt - 1, hprev_ref[...], 0.0)
        da = lt * _shift_down(hs, hprev8, 1)
        dxc_e, dpr, dpi, dbr, dbi, dlam = vjp((da, lt))
        dprb, dpib = dpr.astype(BF16), dpi.astype(BF16)
        dxc_ref[...] = dxc_e + _dot(dprb, wr_ref[...], NT) + _dot(dpib, wi_ref[...], NT)
        dgate_ref[...] = dgate.astype(BF16)
        xct = xc_v.T.astype(BF16)
        dwr_ref[...] += _dot(xct, dprb)
        dwi_ref[...] += _dot(xct, dpib)
        dbr_ref[...] += dbr
        dbi_ref[...] += dbi
        dlam_ref[...] += dlam

    rev = lambda col: pl.BlockSpec((t, c), lambda i: (nt - 1 - i, col))
    par = pl.BlockSpec((1, c), lambda i: (0, 0))
    sq = pl.BlockSpec((c, c), lambda i: (0, 0))
    return pl.pallas_call(
        body, grid=(nt,),
        in_specs=[rev(0), rev(1), rev(0), pl.BlockSpec((8, c), lambda i: (jnp.maximum((nt - 1 - i) * t8 - 1, 0), 0)),
                  rev(0), sq, par, sq, par, par],
        out_specs=[rev(0), rev(0), sq, sq, par, par, par],
        out_shape=[jax.ShapeDtypeStruct((s, c), F32), jax.ShapeDtypeStruct((s, c), BF16)]
                  + [jax.ShapeDtypeStruct((c, c), F32)] * 2 + [jax.ShapeDtypeStruct((1, c), F32)] * 3,
        scratch_shapes=[pltpu.VMEM((8, c), F32), pltpu.VMEM((8, c), F32)],
        compiler_params=_cparams(1, VMEM_LIMIT_MB), name=name)(
            xc, xbg, hseq, hseq, dy, w_rg, b_rg, w_ig, b_ig, lam)


def _conv_bwd(dxc, xbg, conv_w, *, name):
    s, c = dxc.shape
    t = min(512, s)
    t8 = t // 8
    nt = s // t

    def body(dxc_ref, next_ref, xb_ref, prev_ref, w_ref, dxb_ref, dw_ref, db_ref):
        i = pl.program_id(0)

        @pl.when(i == 0)
        def _():
            dw_ref[...] = jnp.zeros_like(dw_ref)
            db_ref[...] = jnp.zeros_like(db_ref)

        g = dxc_ref[...]
        xb = xb_ref[...]
        w = w_ref[...]
        next8 = jnp.where(i < nt - 1, next_ref[...], 0.0)
        prev8 = jnp.where(i > 0, prev_ref[...], 0.0)
        dxb = w[3:4, :] * g
        dw_ref[3:4, :] += jnp.sum(g * xb, axis=0, keepdims=True)
        for d in range(1, CONV_W):
            dxb = dxb + w[3 - d:4 - d, :] * _shift_up(g, next8, d)
            dw_ref[3 - d:4 - d, :] += jnp.sum(g * _shift_down(xb, prev8, d), axis=0, keepdims=True)
        dxb_ref[...] = dxb.astype(BF16)
        db_ref[...] += jnp.sum(g, axis=0, keepdims=True)

    tile = pl.BlockSpec((t, c), lambda i: (i, 0))
    return pl.pallas_call(
        body, grid=(nt,),
        in_specs=[tile, pl.BlockSpec((8, c), lambda i: (jnp.minimum((i + 1) * t8, s // 8 - 1), 0)),
                  tile, pl.BlockSpec((8, c), lambda i: (jnp.maximum(i * t8 - 1, 0), 0)),
                  pl.BlockSpec((CONV_W, c), lambda i: (0, 0))],
        out_specs=[tile, pl.BlockSpec((CONV_W, c), lambda i: (0, 0)), pl.BlockSpec((1, c), lambda i: (0, 0))],
        out_shape=[jax.ShapeDtypeStruct((s, c), BF16), jax.ShapeDtypeStruct((CONV_W, c), F32),
                   jax.ShapeDtypeStruct((1, c), F32)],
        compiler_params=_cparams(1), name=name)(dxc, dxc, xbg, xbg, conv_w)


def _hi_lo_dot(x, tri):
    hi = x.astype(BF16)
    lo = (x - hi.astype(F32)).astype(BF16)
    return _dot(hi, tri) + _dot(lo, tri)


def _tri(tk, inclusive):
    r = lax.broadcasted_iota(jnp.int32, (tk, tk), 0)
    c = lax.broadcasted_iota(jnp.int32, (tk, tk), 1)
    return ((r >= c) if inclusive else (r > c)).astype(BF16)


def _split_heads(x):
    first = lax.broadcasted_iota(jnp.int32, x.shape, 1) < SB_HEAD
    zero = jnp.zeros_like(x)
    return first, (jnp.where(first, x, zero), jnp.where(first, zero, x))


def _sb_softplus_terms(z):
    ls = jnp.minimum(z, 0.0) - jnp.log(1.0 + jnp.exp(-jnp.abs(z)))
    return ls, ls - z


def _sb_alive(runs):
    m = runs[0]
    for r in runs[1:]:
        m = jnp.maximum(m, r)
    return jnp.max(m) > SB_SKIP


def _sb_fwd(q, kv, *, name, job=None):
    s = q.shape[0]
    tb = min(256, s)
    rc = min(SB_ROWS, tb)
    npair = SB_W // SB_PAIR
    chains = [(h, c) for h in range(2) for c in range(tb // rc)]

    def body(q_ref, k_ref, v_ref, o_ref):
        i = pl.program_id(1)
        first, qs = _split_heads(q_ref[...])
        tri = _tri(tb, False)
        causal = lax.broadcasted_iota(jnp.int32, (tb, tb), 1) < lax.broadcasted_iota(jnp.int32, (tb, tb), 0)
        causal_c = [causal[c * rc:(c + 1) * rc, :] for c in range(tb // rc)]

        def group(j_lo, width, carry, diag_last):
            k0 = pl.multiple_of(j_lo * tb, tb)
            kb = k_ref[pl.ds(k0, width * tb), :]
            vb = v_ref[pl.ds(k0, width * tb), :]
            zw = [_dot(qs[h][c * rc:(c + 1) * rc, :], kb, NT) for h, c in chains]
            cols = [slice(t * tb, (t + 1) * tb) for t in range(width)]
            terms = [[_sb_softplus_terms(z[:, cl]) for cl in cols] for z in zw]
            lss = [[ls for ls, _ in tc] for tc in terms]
            ns = [[n for _, n in tc] for tc in terms]
            if diag_last:
                ns = [nc[:-1] + [jnp.where(causal_c[c], nc[-1], 0.0)] for nc, (h, c) in zip(ns, chains)]
            sufs = [[_hi_lo_dot(n, tri) for n in nc] for nc in ns]
            out = []
            for (h, c), lc, nc, sc, (run, acc) in zip(chains, lss, ns, sufs, carry):
                ws = [None] * width
                for t in reversed(range(width)):
                    w = jnp.exp(lc[t] + (sc[t] + run))
                    if diag_last and t == width - 1:
                        w = jnp.where(causal_c[c], w, 0.0)
                    ws[t] = w.astype(BF16)
                    run = run + jnp.sum(nc[t], axis=1, keepdims=True)
                out.append((run, acc + _dot(ws[0] if width == 1 else jnp.concatenate(ws, axis=1), vb)))
            return tuple(out)

        zero = (jnp.zeros((rc, 1), F32), jnp.zeros((rc, SB_PAIR), F32))
        carry = lax.cond(i == 0, lambda c: group(0, 1, c, True), lambda c: group(i - 1, 2, c, True),
                         (zero,) * len(chains))

        def step(st):
            j, _, c = st
            c = group(j, 1, c, False)
            return j - 1, _sb_alive([run for run, _ in c]), c

        _, _, carry = lax.while_loop(lambda st: jnp.logical_and(st[0] >= 0, st[1]), step,
                                     (i - 2, _sb_alive([run for run, _ in carry]), carry))
        accs = [jnp.concatenate([carry[n][1] for n, (h, c) in enumerate(chains) if h == hh], axis=0) for hh in range(2)]
        o_ref[...] = jnp.where(first, accs[0], accs[1])

    (o,), job_out = _pallas_with_job(
        body, job, lambda: pl.program_id(0) * (s // tb) + pl.program_id(1), npair * (s // tb),
        grid=(npair, s // tb),
        in_specs=[pl.BlockSpec((tb, SB_PAIR), lambda p, i: (i, p)),
                  pl.BlockSpec((s, SB_PAIR), lambda p, i: (0, p)),
                  pl.BlockSpec((s, SB_PAIR), lambda p, i: (0, npair + p))],
        out_specs=[pl.BlockSpec((tb, SB_PAIR), lambda p, i: (i, p))],
        out_shape=[jax.ShapeDtypeStruct((s, SB_W), F32)],
        compiler_params=_cparams(2, VMEM_LIMIT_MB), name=name, args=(q, kv, kv))
    return o, job_out


def _sb_bwd(q, kv, o, do, *, name, job=None):
    s = q.shape[0]
    tb = min(256, s)
    rc = min(SB_ROWS, tb)
    npair = SB_W // SB_PAIR
    chains = [(h, c) for h in range(2) for c in range(tb // rc)]

    def body(q_ref, k_ref, v_ref, o_ref, do_ref, dq_ref, dkb_ref, dvb_ref, dk_ref, dv_ref):
        i = pl.program_id(1)

        @pl.when(i == 0)
        def _():
            dk_ref[...] = jnp.zeros_like(dk_ref)
            dv_ref[...] = jnp.zeros_like(dv_ref)

        first, qs = _split_heads(q_ref[...])
        dob = do_ref[...].astype(BF16)
        _, dos = _split_heads(dob)
        prod = dob.astype(F32) * o_ref[...]
        deltas = (jnp.sum(jnp.where(first, prod, 0.0), axis=1, keepdims=True),
                  jnp.sum(jnp.where(first, 0.0, prod), axis=1, keepdims=True))
        tri_x, tri_i = _tri(tb, False), _tri(tb, True)
        causal = lax.broadcasted_iota(jnp.int32, (tb, tb), 1) < lax.broadcasted_iota(jnp.int32, (tb, tb), 0)
        causal_c = [causal[c * rc:(c + 1) * rc, :] for c in range(tb // rc)]
        rows = [slice(c * rc, (c + 1) * rc) for c in range(tb // rc)]

        def group(j_lo, width, carry, diag_last):
            k0 = pl.multiple_of(j_lo * tb, tb)
            kb = k_ref[pl.ds(k0, width * tb), :]
            vb = v_ref[pl.ds(k0, width * tb), :]
            cols = [slice(t * tb, (t + 1) * tb) for t in range(width)]
            zw = [_dot(qs[h][rows[c], :], kb, NT) for h, c in chains]
            dww = [_dot(dos[h][rows[c], :], vb, NT) for h, c in chains]
            terms = [[_sb_softplus_terms(z[:, cl]) for cl in cols] for z in zw]
            lss = [[ls for ls, _ in tc] for tc in terms]
            ns = [[n for _, n in tc] for tc in terms]
            if diag_last:
                ns = [nc[:-1] + [jnp.where(causal_c[c], nc[-1], 0.0)] for nc, (h, c) in zip(ns, chains)]
            sufs = [[_hi_lo_dot(n, tri_x) for n in nc] for nc in ns]
            wbs, gs, runs_n = [], [], []
            for (h, c), lc, nc, sc, dw, (run_n, _, _) in zip(chains, lss, ns, sufs, dww, carry):
                wb, g = [None] * width, [None] * width
                for t in reversed(range(width)):
                    w = jnp.exp(lc[t] + (sc[t] + run_n))
                    if diag_last and t == width - 1:
                        w = jnp.where(causal_c[c], w, 0.0)
                    wb[t] = w.astype(BF16)
                    g[t] = wb[t].astype(F32) * dw[:, cols[t]]
                    run_n = run_n + jnp.sum(nc[t], axis=1, keepdims=True)
                wbs.append(wb)
                gs.append(g)
                runs_n.append(run_n)
            gsufs = [[_hi_lo_dot(g, tri_i) for g in gc] for gc in gs]
            out = []
            dk_t = [jnp.zeros((tb, SB_PAIR), F32) for _ in range(width)]
            dv_t = [jnp.zeros((tb, SB_PAIR), F32) for _ in range(width)]
            for (h, c), lc, gc, gsc, wb, run_n, (_, run_g, dq_acc) in zip(chains, lss, gs, gsufs, wbs, runs_n, carry):
                qh, doh, delta = qs[h][rows[c], :], dos[h][rows[c], :], deltas[h][rows[c], :]
                dzb = [None] * width
                for t in reversed(range(width)):
                    pre = delta - (gsc[t] + run_g)
                    dz = gc[t] - jnp.exp(lc[t]) * (gc[t] + pre)
                    if diag_last and t == width - 1:
                        dz = jnp.where(causal_c[c], dz, 0.0)
                    run_g = run_g + jnp.sum(gc[t], axis=1, keepdims=True)
                    dzb[t] = dz.astype(BF16)
                    dk_t[t] = dk_t[t] + _dot(dz.T.astype(BF16), qh)
                    dv_t[t] = dv_t[t] + _dot(wb[t].astype(F32).T.astype(BF16), doh)
                out.append((run_n, run_g, dq_acc + _dot(dzb[0] if width == 1 else jnp.concatenate(dzb, axis=1), kb)))
            for t in range(width):
                dk_ref[pl.ds(pl.multiple_of((j_lo + t) * tb, tb), tb), :] += dk_t[t]
                dv_ref[pl.ds(pl.multiple_of((j_lo + t) * tb, tb), tb), :] += dv_t[t]
            return tuple(out)

        zero = (jnp.zeros((rc, 1), F32), jnp.zeros((rc, 1), F32), jnp.zeros((rc, SB_PAIR), F32))
        carry = lax.cond(i == 0, lambda c: group(0, 1, c, True), lambda c: group(i - 1, 2, c, True),
                         (zero,) * len(chains))

        def step(st):
            j, _, c = st
            c = group(j, 1, c, False)
            return j - 1, _sb_alive([r[0] for r in c]), c

        _, _, carry = lax.while_loop(lambda st: jnp.logical_and(st[0] >= 0, st[1]), step,
                                     (i - 2, _sb_alive([r[0] for r in carry]), carry))
        dqs = [jnp.concatenate([carry[n][2] for n, (h, c) in enumerate(chains) if h == hh], axis=0) for hh in range(2)]
        dq_ref[...] = (jnp.where(first, dqs[0], dqs[1]) * (1.0 / math.sqrt(SB_HEAD))).astype(BF16)

        @pl.when(i == s // tb - 1)
        def _():
            dkb_ref[...] = dk_ref[...].astype(BF16)
            dvb_ref[...] = dv_ref[...].astype(BF16)

    qtile = pl.BlockSpec((tb, SB_PAIR), lambda p, i: (i, p))
    col = pl.BlockSpec((s, SB_PAIR), lambda p, i: (0, p))
    return _pallas_with_job(
        body, job, lambda: pl.program_id(0) * (s // tb) + pl.program_id(1), npair * (s // tb),
        grid=(npair, s // tb),
        in_specs=[qtile, col, pl.BlockSpec((s, SB_PAIR), lambda p, i: (0, npair + p)), qtile, qtile],
        out_specs=[qtile, col, col],
        out_shape=[jax.ShapeDtypeStruct((s, SB_W), BF16)] * 3,
        scratch_shapes=[pltpu.VMEM((s, SB_PAIR), F32)] * 2,
        compiler_params=_cparams(2, VMEM_LIMIT_MB), name=name, args=(q, kv, kv, o, do))


def _mixnorm_math(yl, ys, gl, gs):
    def rms(x, g):
        return x * lax.rsqrt(jnp.mean(x * x, axis=-1, keepdims=True) + RMS_EPS) * g
    return rms(yl, gl), rms(ys, gs)


def _mixnorm_fwd(yl, ys, gl, gs, *, name):
    s, c = yl.shape
    tm = min(512, s)

    def body(yl_ref, ys_ref, gl_ref, gs_ref, y_ref):
        a, b = _mixnorm_math(yl_ref[...], ys_ref[...], gl_ref[...], gs_ref[...])
        y_ref[:, 0:c] = a.astype(BF16)
        y_ref[:, c:2 * c] = b.astype(BF16)

    return pl.pallas_call(
        body, grid=(s // tm,), in_specs=[_row_spec(tm, c), _row_spec(tm, c), _par_spec(c), _par_spec(c)],
        out_specs=_row_spec(tm, 2 * c), out_shape=jax.ShapeDtypeStruct((s, 2 * c), BF16),
        compiler_params=_cparams(1), name=name)(yl, ys, gl, gs)


def _cross_probs(qh, kh):
    sc = _dot(qh, kh, NT) * (1.0 / math.sqrt(MEM_HD))
    e = jnp.exp(sc - jnp.max(sc, axis=-1, keepdims=True))
    return e / jnp.sum(e, axis=-1, keepdims=True)


def _cross_fwd(q, kv, *, name):
    s, d = q.shape
    mlen = kv.shape[1]
    tm = min(512, s)

    def body(q_ref, kv_ref, o_ref):
        for h in range(MEM_HEADS):
            cols = slice(h * MEM_HD, (h + 1) * MEM_HD)
            p = _cross_probs(q_ref[:, cols].astype(BF16), kv_ref[h].astype(BF16))
            o_ref[:, cols] = _dot(p.astype(BF16), kv_ref[MEM_HEADS + h].astype(BF16)).astype(BF16)

    return pl.pallas_call(
        body, grid=(s // tm,),
        in_specs=[_row_spec(tm, d), pl.BlockSpec((2 * MEM_HEADS, mlen, MEM_HD), lambda i: (0, 0, 0))],
        out_specs=_row_spec(tm, d), out_shape=jax.ShapeDtypeStruct((s, d), BF16),
        compiler_params=_cparams(1, VMEM_LIMIT_MB), name=name)(q, kv)


def _cross_bwd(q, kv, do, *, name):
    s, d = q.shape
    mlen = kv.shape[1]
    tm = min(512, s)

    def body(q_ref, kv_ref, do_ref, dq_ref, dkv_ref):
        @pl.when(pl.program_id(0) == 0)
        def _():
            dkv_ref[...] = jnp.zeros_like(dkv_ref)

        for h in range(MEM_HEADS):
            cols = slice(h * MEM_HD, (h + 1) * MEM_HD)
            qh = q_ref[:, cols].astype(BF16)
            kh = kv_ref[h].astype(BF16)
            doh = do_ref[:, cols].astype(BF16)
            p = _cross_probs(qh, kh)
            dp = _dot(doh, kv_ref[MEM_HEADS + h].astype(BF16), NT)
            ds = p * (dp - jnp.sum(dp * p, axis=-1, keepdims=True)) * (1.0 / math.sqrt(MEM_HD))
            dq_ref[:, cols] = _dot(ds.astype(BF16), kh).astype(BF16)
            dkv_ref[h] += _dot(ds.T.astype(BF16), qh)
            dkv_ref[MEM_HEADS + h] += _dot(p.T.astype(BF16), doh)

    kv_spec = pl.BlockSpec((2 * MEM_HEADS, mlen, MEM_HD), lambda i: (0, 0, 0))
    return pl.pallas_call(
        body, grid=(s // tm,), in_specs=[_row_spec(tm, d), kv_spec, _row_spec(tm, d)],
        out_specs=[_row_spec(tm, d), kv_spec],
        out_shape=[jax.ShapeDtypeStruct((s, d), BF16), jax.ShapeDtypeStruct(kv.shape, F32)],
        compiler_params=_cparams(1, VMEM_LIMIT_MB), name=name)(q, kv, do)


def _proj_bwd_dx(parts, w_in, dz, *, name, job=None, ln_bwd=None):
    s = dz.shape[0]
    d = D_MODEL
    tm = min(256, s)
    widths = [p.shape[1] for p in parts]

    n_parts = len(parts)

    def body(*refs):
        part_refs, w_ref, dz_ref = refs[:n_parts], refs[n_parts], refs[n_parts + 1]
        acc = ALPHA * dz_ref[...]
        off = 0
        for p_ref, wd in zip(part_refs, widths):
            acc = acc + _dot(p_ref[...].astype(BF16), w_ref[:, off:off + wd], NT)
            off += wd
        if ln_bwd is not None:
            x_ref, f_ref, g_ref, b_ref = refs[n_parts + 2:n_parts + 6]
            dzp, dg, db = _ln_bwd_tile(x_ref[...], f_ref[...], ln_bwd[2], g_ref[...], b_ref[...], acc)
            _ln_bwd_store(pl.program_id(0) == 0, dzp, dg, db, *refs[n_parts + 6:])
        else:
            refs[-1][...] = acc

    in_specs = [_row_spec(tm, wd) for wd in widths] + [pl.BlockSpec(w_in.shape, lambda i: (0, 0)), _row_spec(tm, d)]
    args = (*parts, w_in, dz)
    if ln_bwd is not None:
        ln_in, out_specs = _ln_bwd_specs(tm, d)
        in_specs, out_shape = in_specs + ln_in, _ln_bwd_shapes(s, d)
        args += (ln_bwd[0], ln_bwd[1], ln_bwd[3], ln_bwd[4])
    else:
        out_specs, out_shape = [_row_spec(tm, d)], [jax.ShapeDtypeStruct((s, d), F32)]
    outs, job_out = _pallas_with_job(
        body, job, lambda: pl.program_id(0), s // tm, grid=(s // tm,), in_specs=in_specs, out_specs=out_specs,
        out_shape=out_shape, compiler_params=_cparams(1, VMEM_LIMIT_MB), name=name, args=args)
    return (outs[0] if ln_bwd is None else outs), job_out


def _proj_fwd(hb, w_in, *, name):
    s, d = hb.shape
    tm = min(512, s)
    lru, sb = 2 * LRU_W, SB_W

    def body(h_ref, w_ref, xbg_ref, q_ref, kv_ref):
        h = h_ref[...]
        xbg_ref[...] = _dot(h, w_ref[:, 0:lru])
        q_ref[...] = (_dot(h, w_ref[:, lru:lru + sb]) * (1.0 / math.sqrt(SB_HEAD))).astype(BF16)
        kv_ref[...] = _dot(h, w_ref[:, lru + sb:lru + 3 * sb]).astype(BF16)

    return pl.pallas_call(
        body, grid=(s // tm,), in_specs=[_row_spec(tm, d), pl.BlockSpec(w_in.shape, lambda i: (0, 0))],
        out_specs=[_row_spec(tm, lru), _row_spec(tm, sb), _row_spec(tm, 2 * sb)],
        out_shape=[jax.ShapeDtypeStruct((s, lru), F32), jax.ShapeDtypeStruct((s, sb), BF16),
                   jax.ShapeDtypeStruct((s, 2 * sb), BF16)],
        compiler_params=_cparams(1, VMEM_LIMIT_MB), name=name)(hb, w_in)


def _mix_bwd(dzb, w_out, yl, ys, gl, gs, *, name, job=None):
    s, c = yl.shape
    tm = min(512, s)

    def body(dz_ref, w_ref, yl_ref, ys_ref, gl_ref, gs_ref, dyl_ref, dys_ref, dgl_ref, dgs_ref):
        @pl.when(pl.program_id(0) == 0)
        def _():
            dgl_ref[...] = jnp.zeros_like(dgl_ref)
            dgs_ref[...] = jnp.zeros_like(dgs_ref)

        dy = _dot(dz_ref[...], w_ref[...], NT)
        _, vjp = jax.vjp(_mixnorm_math, yl_ref[...], ys_ref[...], gl_ref[...], gs_ref[...])
        dyl, dys, dgl, dgs = vjp((dy[:, 0:c], dy[:, c:2 * c]))
        dyl_ref[...] = dyl
        dys_ref[...] = dys
        dgl_ref[...] += dgl
        dgs_ref[...] += dgs

    return _pallas_with_job(
        body, job, lambda: pl.program_id(0), s // tm, grid=(s // tm,),
        in_specs=[_row_spec(tm, 2 * c), pl.BlockSpec(w_out.shape, lambda i: (0, 0)), _row_spec(tm, c), _row_spec(tm, c),
                  _par_spec(c), _par_spec(c)],
        out_specs=[_row_spec(tm, c), _row_spec(tm, c), _par_spec(c), _par_spec(c)],
        out_shape=[jax.ShapeDtypeStruct((s, c), F32)] * 2 + [jax.ShapeDtypeStruct((1, c), F32)] * 2,
        compiler_params=_cparams(1, VMEM_LIMIT_MB), name=name, args=(dzb, w_out, yl, ys, gl, gs))


def _proj_dw(h, parts, *, name):
    s, d = h.shape
    tk = min(1024, s)
    widths = [p.shape[1] for p in parts]

    def body(*refs):
        h_ref, part_refs, o_ref, acc_ref = refs[0], refs[1:-2], refs[-2], refs[-1]
        k = pl.program_id(0)

        @pl.when(k == 0)
        def _():
            acc_ref[...] = jnp.zeros_like(acc_ref)

        ht = h_ref[...].astype(F32).T.astype(BF16)
        off = 0
        for p_ref, wd in zip(part_refs, widths):
            acc_ref[:, off:off + wd] += _dot(ht, p_ref[...].astype(BF16))
            off += wd

        @pl.when(k == s // tk - 1)
        def _():
            o_ref[...] = acc_ref[...].astype(BF16)

    return pl.pallas_call(
        body, grid=(s // tk,), in_specs=[_row_spec(tk, d)] + [_row_spec(tk, wd) for wd in widths],
        out_specs=pl.BlockSpec((d, sum(widths)), lambda k: (0, 0)),
        out_shape=jax.ShapeDtypeStruct((d, sum(widths)), BF16),
        scratch_shapes=[pltpu.VMEM((d, sum(widths)), F32)],
        compiler_params=_cparams(1, VMEM_LIMIT_MB), name=name)(h, *parts)


def _mesh_pos():
    return lax.axis_index("x"), lax.axis_index("y"), lax.axis_index("c")


class _GatherJob:
    n_phases = 3

    def __init__(self, shards):
        n = len(shards)
        self.inputs = list(shards)
        self.out_shape = [jax.ShapeDtypeStruct((N_DEV,) + a.shape, a.dtype) for a in shards]
        self.scratch = [pltpu.SemaphoreType.DMA((n, 7)), pltpu.SemaphoreType.DMA((n, 7)), pltpu.SemaphoreType.DMA((n,))]

    def phase(self, k, ins, outs, sems):
        send_sems, recv_sems, local_sems = sems
        n = len(ins)
        x, y, c = _mesh_pos()
        me, sibling = (x, y, c), (x, y, 1 - c)
        chips = [(1 - x, y), (x, 1 - y), (1 - x, 1 - y)]

        def copy(a, slot, block, to, src=None):
            dst = outs[a].at[4 * block[0] + 2 * block[1] + block[2]]
            return pltpu.make_async_remote_copy(
                src_ref=dst if src is None else src, dst_ref=dst,
                send_sem=send_sems.at[a, slot], recv_sem=recv_sems.at[a, slot], device_id=to, device_id_type=MESH)

        def mine():
            return [pltpu.make_async_copy(ins[a], outs[a].at[4 * x + 2 * y + c], local_sems.at[a]) for a in range(n)]

        def first():
            return [cp for a in range(n) for cp in
                    [copy(a, 0, me, sibling, src=ins[a])] + [copy(a, 1 + j, me, (*chip, c), src=ins[a]) for j, chip in enumerate(chips)]]

        def passed(a, j):
            return copy(a, 4 + j, (*chips[j], c), sibling)

        if k == 0:
            for cp in mine() + first():
                cp.start()
        elif k == 1:
            for a in range(n):
                for j, chip in enumerate(chips):
                    copy(a, 1 + j, (*chip, c), me).wait_recv()
                    passed(a, j).start()
        else:
            for a in range(n):
                copy(a, 0, sibling, me).wait_recv()
                for j, chip in enumerate(chips):
                    copy(a, 4 + j, (*chip, 1 - c), me).wait_recv()
            for cp in first() + [passed(a, j) for a in range(n) for j in range(3)]:
                cp.wait_send()
            for cp in mine():
                cp.wait()


class _SiblingJob:
    n_phases = 2

    def __init__(self, grads):
        n, half = len(grads), N_DEV // 2
        self.inputs = list(grads)
        self.out_shape = [jax.ShapeDtypeStruct((half,) + g.shape[1:], g.dtype) for g in grads]
        self.scratch = [pltpu.SemaphoreType.DMA((n, half)), pltpu.SemaphoreType.DMA((n, half))]

    def phase(self, k, ins, outs, sems):
        send_sems, recv_sems = sems
        x, y, c = _mesh_pos()
        copies = [pltpu.make_async_remote_copy(
            src_ref=ins[a].at[2 * j + (1 - c)], dst_ref=outs[a].at[j], send_sem=send_sems.at[a, j],
            recv_sem=recv_sems.at[a, j], device_id=(x, y, 1 - c), device_id_type=MESH)
            for a in range(len(ins)) for j in range(N_DEV // 2)]
        for cp in copies:
            if k == 0:
                cp.start()
            else:
                cp.wait()


class _ChipJob:
    n_phases = 2

    def __init__(self, partials):
        n, half = len(partials), N_DEV // 2
        self.inputs = list(partials)
        self.out_shape = [jax.ShapeDtypeStruct(p.shape, p.dtype) for p in partials]
        self.scratch = [pltpu.SemaphoreType.DMA((n, half)), pltpu.SemaphoreType.DMA((n, half)), pltpu.SemaphoreType.DMA((n,))]

    def phase(self, k, ins, outs, sems):
        send_sems, recv_sems, local_sems = sems
        n, half = len(ins), N_DEV // 2
        x, y, c = _mesh_pos()
        jme = 2 * x + y
        local = [pltpu.make_async_copy(ins[a].at[jme], outs[a].at[jme], local_sems.at[a]) for a in range(n)]
        for cp in local:
            if k == 0:
                cp.start()
            else:
                cp.wait()
        for j in range(half):
            @pl.when(jme != j)
            def _(j=j):
                for a in range(n):
                    if k == 0:
                        pltpu.make_async_remote_copy(
                            src_ref=ins[a].at[j], dst_ref=outs[a].at[jme], send_sem=send_sems.at[a, j],
                            recv_sem=recv_sems.at[a, jme], device_id=(j // 2, j % 2, c), device_id_type=MESH).start()
                    else:
                        arrive = pltpu.make_async_remote_copy(
                            src_ref=ins[a].at[j], dst_ref=outs[a].at[j], send_sem=send_sems.at[a, j],
                            recv_sem=recv_sems.at[a, j], device_id=(j // 2, j % 2, c), device_id_type=MESH)
                        arrive.wait_recv()
                        arrive.wait_send()


def _phase_steps(n_phases, n_steps):
    return [0, n_steps - 1] if n_phases == 2 else [0, n_steps // 2, n_steps - 1]


def _pallas_with_job(body, job, step_fn, n_steps, *, grid, in_specs, out_specs, out_shape, scratch_shapes=(),
                     compiler_params, name, args):
    in_specs, out_specs, out_shape, scratch_shapes = list(in_specs), list(out_specs), list(out_shape), list(scratch_shapes)
    if job is None:
        outs = pl.pallas_call(body, grid=grid, in_specs=in_specs, out_specs=out_specs, out_shape=out_shape,
                              scratch_shapes=scratch_shapes, compiler_params=compiler_params, name=name)(*args)
        return list(outs), None
    ni, no, ns = len(in_specs), len(out_specs), len(scratch_shapes)
    ji, jo = len(job.inputs), len(job.out_shape)
    steps = _phase_steps(job.n_phases, n_steps)

    def wrapped(*refs):
        own_in, job_in = refs[:ni], refs[ni:ni + ji]
        own_out, job_out = refs[ni + ji:ni + ji + no], refs[ni + ji + no:ni + ji + no + jo]
        own_scr, job_scr = refs[ni + ji + no + jo:ni + ji + no + jo + ns], refs[ni + ji + no + jo + ns:]
        step = step_fn()
        for k, at in enumerate(steps):
            @pl.when(step == at)
            def _(k=k):
                job.phase(k, job_in, job_out, job_scr)
        body(*own_in, *own_out, *own_scr)

    any_spec = pl.BlockSpec(memory_space=pl.ANY)
    outs = pl.pallas_call(
        wrapped, grid=grid, in_specs=in_specs + [any_spec] * ji, out_specs=out_specs + [any_spec] * jo,
        out_shape=out_shape + list(job.out_shape), scratch_shapes=scratch_shapes + list(job.scratch),
        compiler_params=compiler_params, name=name)(*args, *job.inputs)
    return list(outs[:no]), list(outs[no:])


def _run_job(job, *, name):
    ji, jo = len(job.inputs), len(job.out_shape)

    def body(*refs):
        for k in range(job.n_phases):
            job.phase(k, refs[:ji], refs[ji:ji + jo], refs[ji + jo:])

    any_spec = pl.BlockSpec(memory_space=pl.ANY)
    return pl.pallas_call(body, in_specs=[any_spec] * ji, out_specs=[any_spec] * jo, out_shape=list(job.out_shape),
                          scratch_shapes=list(job.scratch), name=name)(*job.inputs)


def _pair_add(grad, recv, cj_idx, *, name):
    half, r, cdim = recv.shape
    tr = _row_tile(r, 512)

    def body(cj_ref, g_ref, r_ref, own_ref, tb_ref):
        t = g_ref[...].astype(F32) + r_ref[...].astype(F32)
        tb_ref[...] = t.astype(BF16)

        @pl.when(pl.program_id(1) == cj_ref[1])
        def _():
            own_ref[...] = t

    grid_spec = pltpu.PrefetchScalarGridSpec(
        num_scalar_prefetch=1, grid=(r // tr, half),
        in_specs=[pl.BlockSpec((None, tr, cdim), lambda i, j, cj: (2 * j + cj[0], i, 0)),
                  pl.BlockSpec((None, tr, cdim), lambda i, j, cj: (j, i, 0))],
        out_specs=[pl.BlockSpec((tr, cdim), lambda i, j, cj: (i, 0)),
                   pl.BlockSpec((None, tr, cdim), lambda i, j, cj: (j, i, 0))])
    return pl.pallas_call(
        body, grid_spec=grid_spec,
        out_shape=[jax.ShapeDtypeStruct((r, cdim), F32), jax.ShapeDtypeStruct(recv.shape, BF16)],
        compiler_params=_cparams(2), name=name)(cj_idx, grad, recv)


def _adam_math(w, g, m, v):
    m = ADAM_B1 * m + (1.0 - ADAM_B1) * g
    v = ADAM_B2 * v + (1.0 - ADAM_B2) * (g * g)
    m_hat = m / (1.0 - ADAM_B1 ** ADAM_STEP)
    v_hat = v / (1.0 - ADAM_B2 ** ADAM_STEP)
    delta = -ADAM_LR * (m_hat / (jnp.sqrt(v_hat) + ADAM_EPS) + ADAM_WD * w)
    return delta, m, v


def _reduce_adam(own, recv, j_idx, w, m, v, *, name):
    half, r, cdim = recv.shape
    tr = _row_tile(r, 256)

    def body(j_ref, own_ref, recv_ref, w_ref, m_ref, v_ref, g_ref, d_ref, nm_ref, nv_ref):
        jme = j_ref[0]
        g = own_ref[...]
        for sl in range(half):
            g = g + jnp.where(sl == jme, 0.0, recv_ref[sl].astype(F32))
        delta, nm, nv = _adam_math(w_ref[...], g, m_ref[...], v_ref[...])
        g_ref[...] = g
        d_ref[...] = delta
        nm_ref[...] = nm
        nv_ref[...] = nv

    tile = pl.BlockSpec((tr, cdim), lambda i, j_ref: (i, 0))
    grid_spec = pltpu.PrefetchScalarGridSpec(
        num_scalar_prefetch=1, grid=(r // tr,),
        in_specs=[tile, pl.BlockSpec((half, tr, cdim), lambda i, j_ref: (0, i, 0)), tile, tile, tile],
        out_specs=[tile] * 4)
    return pl.pallas_call(
        body, grid_spec=grid_spec, out_shape=[jax.ShapeDtypeStruct((r, cdim), F32)] * 4,
        compiler_params=_cparams(1), name=name)(j_idx, own, recv, w, m, v)


class _SmallGatherJob:
    n_phases = 2

    def __init__(self, pack):
        self.inputs = [pack]
        self.out_shape = [jax.ShapeDtypeStruct((N_DEV,) + pack.shape, pack.dtype)]
        self.scratch = [pltpu.SemaphoreType.DMA((N_DEV,)), pltpu.SemaphoreType.DMA((N_DEV,)), pltpu.SemaphoreType.DMA(())]

    def phase(self, k, ins, outs, sems):
        (pack,), (land,), (send_sems, recv_sems, local_sem) = ins, outs, sems
        x, y, c = _mesh_pos()
        me = 4 * x + 2 * y + c
        mine = pltpu.make_async_copy(pack, land.at[me], local_sem)
        if k == 0:
            mine.start()
        else:
            mine.wait()
        for t in range(N_DEV):
            @pl.when(me != t)
            def _(t=t):
                peer = (t // 4, (t // 2) % 2, t % 2)
                if k == 0:
                    pltpu.make_async_remote_copy(
                        src_ref=pack, dst_ref=land.at[me], send_sem=send_sems.at[t], recv_sem=recv_sems.at[me],
                        device_id=peer, device_id_type=MESH).start()
                else:
                    cp = pltpu.make_async_remote_copy(
                        src_ref=pack, dst_ref=land.at[t], send_sem=send_sems.at[t], recv_sem=recv_sems.at[t],
                        device_id=peer, device_id_type=MESH)
                    cp.wait_recv()
                    cp.wait_send()


class _Jobs:
    def __init__(self, jobs):
        self.jobs = jobs
        self.n_phases = jobs[0].n_phases
        assert all(j.n_phases == self.n_phases for j in jobs)
        self.inputs = [a for j in jobs for a in j.inputs]
        self.out_shape = [o for j in jobs for o in j.out_shape]
        self.scratch = [sc for j in jobs for sc in j.scratch]

    def phase(self, k, ins, outs, sems):
        i = o = sc = 0
        for j in self.jobs:
            ni, no, nsc = len(j.inputs), len(j.out_shape), len(j.scratch)
            j.phase(k, ins[i:i + ni], outs[o:o + no], sems[sc:sc + nsc])
            i, o, sc = i + ni, o + no, sc + nsc


def _adam_small(w, land, m, v, *, name):
    def body(w_ref, land_ref, m_ref, v_ref, g_ref, d_ref, nm_ref, nv_ref):
        total = land_ref[0]
        for t in range(1, N_DEV):
            total = total + land_ref[t]
        g_ref[...] = total
        d_ref[...], nm_ref[...], nv_ref[...] = _adam_math(w_ref[...], total, m_ref[...], v_ref[...])

    vmem = pl.BlockSpec(memory_space=pltpu.VMEM)
    return pl.pallas_call(
        body, in_specs=[vmem] * 4, out_specs=[vmem] * 4, out_shape=[jax.ShapeDtypeStruct(w.shape, F32)] * 4,
        name=name)(w, land, m, v)


def _block_diag(w):
    h, a, b = w.shape
    eye = jnp.eye(h, dtype=w.dtype)
    return (eye[:, None, :, None] * w[:, :, None, :]).reshape(h * a, h * b)


def _diag_blocks(w, h):
    a = w.shape[0] // h
    return jnp.stack([w[i * a:(i + 1) * a, i * a:(i + 1) * a] for i in range(h)])


def _local_step(x, mem, target, p, hooks=None):
    g = {}
    p = dict(p)
    w_rg, w_ig = _block_diag(p["w_rgate"]).astype(BF16), _block_diag(p["w_igate"]).astype(BF16)

    (gu1, f1, xb, h1, h1b), arrived = _ffn_fwd(x, p["ffn1_w13"], p["ffn1_w2"], name="ffn1_fwd", ln=(p["ln1_g"], p["ln1_b"]),
                                               job=hooks and hooks.gather_job(_GATHER_MIXER))
    if hooks:
        p.update(hooks.gathered(_GATHER_MIXER, arrived))
    w_in = p["w_in"]
    xbg, q_sb, kv_sb = _proj_fwd(h1b, w_in, name="proj_fwd")
    xc, hseq, y_lru = _lru_fwd(xbg, p["conv_w"], p["conv_b"], w_rg, p["b_rgate"], w_ig, p["b_igate"],
                               p["lru_lambda"], name="lru_fwd")
    y_sb, arrived = _sb_fwd(q_sb, kv_sb, name="sb_fwd", job=hooks and hooks.gather_job(_GATHER_FFN2))
    if hooks:
        p.update(hooks.gathered(_GATHER_FFN2, arrived))
    ymix = _mixnorm_fwd(y_lru, y_sb, p["g_lru"], p["g_sb"], name="mixnorm_fwd")
    mix, h2, h2b = _matmul(ymix, p["w_out"], ln=(h1, 1.0, p["ln2_g"], p["ln2_b"]), name="mix_out_ln2")
    qm = _matmul(h2b, p["mem_wq"], out_dtype=BF16, name="cross_q")
    kv = _matmul(mem, p["mem_wkv"], name="cross_kv")
    o_cross = _cross_fwd(qm, kv, name="cross_fwd")
    cross, h3, h3b = _matmul(o_cross, p["mem_wo"], ln=(h2, 1.0, p["ln3_g"], p["ln3_b"]), name="cross_out_ln3")
    (gu2, dz4, dz4b, g["ln4_g"], g["ln4_b"], loss), _ = _ffn_fwd(
        h3b, p["ffn2_w13"], p["ffn2_w2"], loss_head=(h3, target, p["ln4_g"], p["ln4_b"]), name="ffn2_fwd_loss")
    (dz3, dz3b, dgu2, act2, g["ln3_g"], g["ln3_b"]), _ = _ffn_bwd(
        dz4, gu2, p["ffn2_w13"], p["ffn2_w2"], ln_bwd=(h2, cross, 1.0, p["ln3_g"], p["ln3_b"]), name="ffn2_bwd_ln3")
    g["ffn2_w13"] = _matmul(dgu2, h3b, trans_a=True, tm=FF_CHUNK, tn=1024, tk=2048, out_dtype=BF16, name="ffn2_dw13")
    g["ffn2_w2"] = _matmul(act2, dz4b, trans_a=True, tm=FF_CHUNK, tn=1024, tk=2048, scale=0.5, out_dtype=BF16, name="ffn2_dw2")
    g["mem_wo"] = _matmul(o_cross, dz3b, trans_a=True, tm=1024, tn=1024, tk=1024, out_dtype=BF16, name="cross_dwo")
    do_cross = _matmul(dz3b, p["mem_wo"], trans_b=True, out_dtype=BF16, name="cross_do")
    dqm, dkv = _cross_bwd(qm, kv, do_cross, name="cross_bwd")
    g["mem_wq"] = _matmul(h2b, dqm, trans_a=True, tm=1024, tn=1024, tk=1024, out_dtype=BF16, name="cross_dwq")
    g["mem_wkv"] = _matmul(mem, dkv, trans_a=True, tm=1024, tn=MEM_HD, tk=256, out_dtype=BF16, name="cross_dwkv")
    dz2, dz2b, g["ln2_g"], g["ln2_b"] = _matmul(dqm, p["mem_wq"], trans_b=True, res=dz3, res_coeff=ALPHA,
                                                ln_bwd=(h1, mix, 1.0, p["ln2_g"], p["ln2_b"]), name="cross_dh_ln2")
    g["w_out"] = _matmul(ymix, dz2b, trans_a=True, tm=1024, tn=1024, tk=1024, out_dtype=BF16, name="mix_dwout")
    (dy_lru, dy_sb, g["g_lru"], g["g_sb"]), arrived = _mix_bwd(
        dz2b, p["w_out"], y_lru, y_sb, p["g_lru"], p["g_sb"], name="mix_bwd", job=hooks and hooks.sibling_job(_REDUCE_EARLY, g))
    (dq, dk, dv), arrived = _sb_bwd(q_sb, kv_sb, y_sb, dy_sb, name="sb_bwd",
                                    job=hooks and hooks.chip_job(_REDUCE_EARLY, arrived))
    if hooks:
        hooks.reduced(_REDUCE_EARLY, arrived)
    dxc, dgate, dwr, dwi, g["b_rgate"], g["b_igate"], g["lru_lambda"] = _lru_bwd(
        xc, xbg, hseq, dy_lru, w_rg, p["b_rgate"], w_ig, p["b_igate"], p["lru_lambda"], name="lru_bwd")
    g["w_rgate"], g["w_igate"] = _diag_blocks(dwr, 8), _diag_blocks(dwi, 8)
    dxb, g["conv_w"], g["conv_b"] = _conv_bwd(dxc, xbg, p["conv_w"], name="conv_bwd")
    parts = [dxb, dgate, dq, dk, dv]
    g["w_in"] = _proj_dw(h1b, parts, name="proj_dw")
    (dz1, dz1b, g["ln1_g"], g["ln1_b"]), arrived = _proj_bwd_dx(
        parts, w_in, dz2, ln_bwd=(x, f1, 0.5, p["ln1_g"], p["ln1_b"]), name="proj_dh_ln1",
        job=hooks and hooks.sibling_job(['w_in'], g))
    (grad_x, dgu1, act1), _ = _ffn_bwd(dz1, gu1, p["ffn1_w13"], p["ffn1_w2"], name="ffn1_bwd")
    dw13 = dict(trans_a=True, tm=FF_CHUNK, tn=1024, tk=2048, out_dtype=BF16, name="ffn1_dw13")
    dw2 = dict(trans_a=True, tm=FF_CHUNK, tn=1024, tk=2048, scale=0.5, out_dtype=BF16, name="ffn1_dw2")
    if not hooks:
        g["ffn1_w13"] = _matmul(dgu1, xb, **dw13)
        g["ffn1_w2"] = _matmul(act1, dz1b, **dw2)
        return loss, grad_x, g
    g["ffn1_w13"], arrived = _matmul(dgu1, xb, job=_Jobs([hooks.chip_job(['w_in'], arrived), hooks.small_job(g, loss)]), **dw13)
    hooks.reduced(['w_in'], arrived[:1])
    hooks.small_land = arrived[1]
    arrived = _run_job(hooks.sibling_job(['ffn1_w13'], g), name="reduce_sibling_w13")
    g["ffn1_w2"], arrived = _matmul(act1, dz1b, job=hooks.chip_job(['ffn1_w13'], arrived), **dw2)
    hooks.reduced(['ffn1_w13'], arrived)
    arrived = _run_job(hooks.sibling_job(['ffn1_w2'], g), name="reduce_sibling_w2")
    hooks.reduced(['ffn1_w2'], _run_job(hooks.chip_job(['ffn1_w2'], arrived), name="reduce_chips_w2"))
    return loss, grad_x, g


_WEIGHTS = ['ffn1_w13', 'ffn1_w2', 'ln1_g', 'ln1_b', 'w_in', 'conv_w', 'conv_b', 'w_rgate', 'b_rgate', 'w_igate',
            'b_igate', 'lru_lambda', 'g_lru', 'g_sb', 'w_out', 'ln2_g', 'ln2_b', 'mem_wq', 'mem_wkv', 'mem_wo',
            'ln3_g', 'ln3_b', 'ffn2_w13', 'ffn2_w2', 'ln4_g', 'ln4_b']
_SHARDED = ['ffn1_w13', 'ffn1_w2', 'w_in', 'w_out', 'mem_wq', 'mem_wkv', 'mem_wo', 'ffn2_w13', 'ffn2_w2']
_SMALL = ['ln1_g', 'ln1_b', 'ln2_g', 'ln2_b', 'ln3_g', 'ln3_b', 'ln4_g', 'ln4_b', 'conv_b', 'b_rgate', 'b_igate',
          'lru_lambda', 'g_lru', 'g_sb', 'w_rgate', 'w_igate']


def _pack_small(d, conv_w_full=None, loss=None):
    rows = [d[n].reshape(-1) for n in _SMALL]
    if conv_w_full is not None:
        rows.append(conv_w_full.reshape(-1))
    if loss is not None:
        rows.append(loss.reshape(-1))
    flat = jnp.concatenate(rows)
    pad = (-flat.shape[0]) % (8 * D_MODEL)
    return jnp.pad(flat, (0, pad)).reshape(-1, D_MODEL)


def _unpack_small(pack, like, with_conv):
    flat = pack.reshape(-1)
    out, off = {}, 0
    for n in _SMALL:
        size = math.prod(like[n].shape)
        out[n] = flat[off:off + size].reshape(like[n].shape)
        off += size
    conv = flat[off:off + CONV_W * LRU_W].reshape(CONV_W, LRU_W) if with_conv else None
    return out, conv


_GATHER_FIRST = ['ffn1_w13', 'ffn1_w2']
_GATHER_MIXER = ['w_in', 'conv_w', 'w_out', 'mem_wq', 'mem_wkv', 'mem_wo']
_GATHER_FFN2 = ['ffn2_w13', 'ffn2_w2']
_REDUCE_EARLY = ['ffn2_w13', 'ffn2_w2', 'mem_wo', 'mem_wq', 'mem_wkv', 'w_out']


_TRANSPOSED = ('ffn1_w13', 'ffn2_w13')


def _own_layout(n, a):
    return a.T if n in _TRANSPOSED else a


def _weight_layout(n, full):
    if n in ('ffn1_w13', 'ffn2_w13', 'mem_wkv'):
        return full
    if n == 'w_in':
        return full.transpose(1, 0, 2).reshape(D_MODEL, -1)
    if n == 'conv_w':
        return full.transpose(1, 0, 2).reshape(CONV_W, LRU_W)
    return full.reshape(-1, full.shape[-1])


def _grad_blocks(n, g):
    if n == 'w_in':
        return g.reshape(D_MODEL, N_DEV, -1).transpose(1, 0, 2)
    return g.reshape((N_DEV, -1, g.shape[-1]))


class _Hooks:
    def __init__(self, w, cj_idx):
        self.w, self.cj_idx = w, cj_idx
        self.sums, self.recv = {}, {}

    def shard(self, n):
        return self.w[n][0] if n == 'conv_w' else _own_layout(n, self.w[n][0]).astype(BF16)

    def gather_job(self, names):
        return _GatherJob([self.shard(n) for n in names])

    def gathered(self, names, outs):
        return {n: _weight_layout(n, o) for n, o in zip(names, outs)}

    def sibling_job(self, names, g):
        self.blocks = [_grad_blocks(n, g[n]) for n in names]
        return _SiblingJob(self.blocks)

    def chip_job(self, names, from_sibling):
        for n, b, r in zip(names, self.blocks, from_sibling):
            self.sums[n] = _pair_add(b, r, self.cj_idx, name=f"pair_add_{n}")
        return _ChipJob([self.sums[n][1] for n in names])

    def reduced(self, names, from_chips):
        self.recv.update(zip(names, from_chips))

    def small_job(self, g, loss):
        return _SmallGatherJob(_pack_small(g, conv_w_full=g["conv_w"], loss=loss))


def kernel(x, mem, ffn1_w13, ffn1_w2, ln1_g, ln1_b, w_in, conv_w, conv_b, w_rgate, b_rgate, w_igate, b_igate, lru_lambda, g_lru, g_sb, w_out, ln2_g, ln2_b, mem_wq, mem_wkv, mem_wo, ln3_g, ln3_b, ffn2_w13, ffn2_w2, ln4_g, ln4_b, loss_target, m_ffn1_w13, m_ffn1_w2, m_ln1_g, m_ln1_b, m_w_in, m_conv_w, m_conv_b, m_w_rgate, m_b_rgate, m_w_igate, m_b_igate, m_lru_lambda, m_g_lru, m_g_sb, m_w_out, m_ln2_g, m_ln2_b, m_mem_wq, m_mem_wkv, m_mem_wo, m_ln3_g, m_ln3_b, m_ffn2_w13, m_ffn2_w2, m_ln4_g, m_ln4_b, v_ffn1_w13, v_ffn1_w2, v_ln1_g, v_ln1_b, v_w_in, v_conv_w, v_conv_b, v_w_rgate, v_b_rgate, v_w_igate, v_b_igate, v_lru_lambda, v_g_lru, v_g_sb, v_w_out, v_ln2_g, v_ln2_b, v_mem_wq, v_mem_wkv, v_mem_wo, v_ln3_g, v_ln3_b, v_ffn2_w13, v_ffn2_w2, v_ln4_g, v_ln4_b):
    args = locals()
    w = {n: args[n] for n in _WEIGHTS}
    mom = {n: args["m_" + n] for n in _WEIGHTS}
    var = {n: args["v_" + n] for n in _WEIGHTS}
    ix, iy, ic = lax.axis_index("x"), lax.axis_index("y"), lax.axis_index("c")

    j_idx = jnp.reshape(2 * ix + iy, (1,)).astype(jnp.int32)
    hooks = _Hooks(w, jnp.stack([ic, 2 * ix + iy]).astype(jnp.int32))
    p = {n: (w[n][0] if w[n].ndim == 4 else w[n]) for n in _WEIGHTS if n not in _SHARDED and n != 'conv_w'}
    p.update(hooks.gathered(_GATHER_FIRST, _run_job(hooks.gather_job(_GATHER_FIRST), name="gather_ffn1")))
    _, grad_x, g = _local_step(x[0], mem[0], loss_target[0], p, hooks)

    grads, delta, new_m, new_v = {}, {}, {}, {}
    for n in _SHARDED:
        grads[n], delta[n], new_m[n], new_v[n] = (
            _own_layout(n, o).reshape(w[n].shape) for o in _reduce_adam(
                hooks.sums[n][0], hooks.recv[n], j_idx, _own_layout(n, w[n][0]), _own_layout(n, mom[n][0]),
                _own_layout(n, var[n][0]), name=f"adam_{n}"))

    me = 4 * ix + 2 * iy + ic
    own_cols = lambda full: lax.dynamic_slice_in_dim(full, me * (LRU_W // N_DEV), LRU_W // N_DEV, axis=1)
    land = hooks.small_land
    conv_off = sum(math.prod(w[n].shape) for n in _SMALL)
    land_conv = land.reshape(N_DEV, -1)[:, conv_off:conv_off + CONV_W * LRU_W].reshape(N_DEV, CONV_W, LRU_W)
    land_own = jnp.pad(own_cols(land_conv.reshape(N_DEV * CONV_W, LRU_W)).reshape(N_DEV, -1),
                       ((0, 0), (0, CONV_W * LRU_W - CONV_W * LRU_W // N_DEV)))
    land = lax.dynamic_update_slice(land.reshape(N_DEV, -1), land_own, (0, conv_off)).reshape(land.shape)
    pk = lambda d: _pack_small({n: d[n] for n in _SMALL}, conv_w_full=jnp.pad(d["conv_w"].reshape(-1), (0, CONV_W * LRU_W - CONV_W * LRU_W // N_DEV)))
    g_s, d_s, m_s, v_s = _adam_small(pk(w), land, pk(mom), pk(var), name="adam_small")
    for src, dst in ((g_s, grads), (d_s, delta), (m_s, new_m), (v_s, new_v)):
        vals, conv = _unpack_small(src, w, with_conv=True)
        dst.update(vals)
        dst["conv_w"] = conv.reshape(-1)[:CONV_W * LRU_W // N_DEV].reshape(w["conv_w"].shape)
    loss = g_s.reshape(-1)[conv_off + CONV_W * LRU_W]

    return (loss, grad_x[None], *[grads[n] for n in _WEIGHTS], *[delta[n] for n in _WEIGHTS],
            *[new_m[n] for n in _WEIGHTS], *[new_v[n] for n in _WEIGHTS])
```

```python
import functools
import math

import jax
import jax.numpy as jnp
from jax import lax
from jax.experimental import pallas as pl
from jax.experimental.pallas import tpu as pltpu

F32, BF16 = jnp.float32, jnp.bfloat16

D_MODEL = 1024
LRU_W = 512
SB_W = 512
SB_PAIR = 128
SB_HEAD = 64
SB_ROWS = 128
SB_SKIP = -110.0
D_FF = 2688
FF_CHUNK = 672
N_DEV = 8
MEM_HEADS = 4
MEM_HD = 256
CONV_W = 4
ALPHA = 2.0 ** 0.25
LN_EPS = 1e-5
RMS_EPS = 1e-6
LRU_C = 8.0
ADAM_LR, ADAM_B1, ADAM_B2, ADAM_EPS, ADAM_WD, ADAM_STEP = 0.001, 0.9, 0.999, 1e-08, 0.01, 10

NN = (((1,), (0,)), ((), ()))
NT = (((1,), (1,)), ((), ()))
MESH = pl.DeviceIdType.MESH
VMEM_LIMIT_MB = 56


def _dot(a, b, dn=NN):
    return lax.dot_general(a, b, dn, preferred_element_type=F32)


def _cparams(n_axes, vmem_mb=None):
    kw = dict(dimension_semantics=("arbitrary",) * n_axes)
    if vmem_mb is not None:
        kw["vmem_limit_bytes"] = vmem_mb << 20
    return pltpu.CompilerParams(**kw)


def _row_tile(rows, want):
    if rows <= want:
        return rows
    t = want - want % 8
    while rows % t:
        t -= 8
    return t


def _matmul(a, b, *, name, out_dtype=F32, trans_a=False, trans_b=False, tm=512, tn=1024, tk=1024,
            scale=1.0, res=None, res_coeff=1.0, job=None, ln=None, ln_bwd=None):
    a_chunked, b_chunked = a.ndim == 3, b.ndim == 3
    nc = a.shape[0] if a_chunked else (b.shape[0] if b_chunked else 1)
    a2, b2 = a.shape[-2:], b.shape[-2:]
    (kdim, m) = a2 if trans_a else a2[::-1]
    n = b2[0] if trans_b else b2[1]
    tm, tn, tk = min(tm, m), min(tn, n), min(tk, kdim)
    assert m % tm == 0 and n % tn == 0 and kdim % tk == 0, (name, m, n, kdim)
    nk = kdim // tk

    def a_idx(c, i, j, k):
        idx = (k, i) if trans_a else (i, k)
        return (c,) + idx if a_chunked else idx

    def b_idx(c, i, j, k):
        idx = (j, k) if trans_b else (k, j)
        return (c,) + idx if b_chunked else idx

    a_blk = (tk, tm) if trans_a else (tm, tk)
    b_blk = (tn, tk) if trans_b else (tk, tn)
    in_specs = [pl.BlockSpec(((None,) + a_blk) if a_chunked else a_blk, a_idx),
                pl.BlockSpec(((None,) + b_blk) if b_chunked else b_blk, b_idx)]
    args = [a, b]
    if res is not None:
        in_specs.append(pl.BlockSpec((None, tm, tn), lambda c, i, j, k: (c, i, j)))
        args.append(res.reshape((nc, m, n)))
    if ln is not None:
        assert nc == 1 and tn == n and res is None
        in_specs += [pl.BlockSpec((tm, tn), lambda c, i, j, k: (i, 0)), pl.BlockSpec((1, tn), lambda c, i, j, k: (0, 0)),
                     pl.BlockSpec((1, tn), lambda c, i, j, k: (0, 0))]
        args += [ln[0], ln[2], ln[3]]
    if ln_bwd is not None:
        assert nc == 1 and tn == n and res is not None and ln is None
        in_specs += [pl.BlockSpec((tm, tn), lambda c, i, j, k: (i, 0))] * 2 + [pl.BlockSpec((1, tn), lambda c, i, j, k: (0, 0))] * 2
        args += [ln_bwd[0], ln_bwd[1], ln_bwd[3], ln_bwd[4]]

    def body(*refs):
        if ln_bwd is not None:
            a_ref, b_ref, r_ref, x_ref, f_ref, g_ref, bb_ref, o_ref, dzb_ref, dg_ref, db_ref, acc_ref = refs
        elif res is not None:
            a_ref, b_ref, r_ref, o_ref, acc_ref = refs
        elif ln is not None:
            a_ref, b_ref, x_ref, g_ref, bb_ref, o_ref, h_ref, hb_ref, acc_ref = refs
        else:
            a_ref, b_ref, o_ref, acc_ref = refs
        k = pl.program_id(3)

        @pl.when(k == 0)
        def _():
            acc_ref[...] = jnp.zeros_like(acc_ref)

        av = a_ref[...]
        if trans_a:
            av = av.astype(F32).T
        acc_ref[...] += _dot(av.astype(BF16), b_ref[...].astype(BF16), NT if trans_b else NN)

        @pl.when(k == nk - 1)
        def _():
            out = acc_ref[...]
            if scale != 1.0:
                out = out * scale
            if res is not None:
                out = out + res_coeff * r_ref[...]
            if ln_bwd is not None:
                dz, dg, db = _ln_bwd_tile(x_ref[...], f_ref[...], ln_bwd[2], g_ref[...], bb_ref[...], out)
                _ln_bwd_store(pl.program_id(1) == 0, dz, dg, db, o_ref, dzb_ref, dg_ref, db_ref)
                return
            o_ref[...] = out.astype(out_dtype)
            if ln is not None:
                h = _ln_math(ALPHA * x_ref[...] + ln[1] * out, g_ref[...], bb_ref[...])
                h_ref[...] = h
                hb_ref[...] = h.astype(BF16)

    grid = (nc, m // tm, n // tn, nk)

    def step():
        idx = pl.program_id(0)
        for ax in range(1, 4):
            idx = idx * grid[ax] + pl.program_id(ax)
        return idx

    out_specs = [pl.BlockSpec((None, tm, tn), lambda c, i, j, k: (c, i, j))]
    out_shape = [jax.ShapeDtypeStruct((nc, m, n), out_dtype)]
    if ln is not None:
        out_specs += [pl.BlockSpec((tm, tn), lambda c, i, j, k: (i, 0))] * 2
        out_shape += [jax.ShapeDtypeStruct((m, n), F32), jax.ShapeDtypeStruct((m, n), BF16)]
    if ln_bwd is not None:
        out_specs += [pl.BlockSpec((tm, tn), lambda c, i, j, k: (i, 0))] + [pl.BlockSpec((1, tn), lambda c, i, j, k: (0, 0))] * 2
        out_shape += _ln_bwd_shapes(m, n)[1:]
    outs, job_out = _pallas_with_job(
        body, job, step, math.prod(grid), grid=grid, in_specs=in_specs, out_specs=out_specs, out_shape=out_shape,
        scratch_shapes=[pltpu.VMEM((tm, tn), F32)],
        compiler_params=_cparams(4, VMEM_LIMIT_MB), name=name, args=args)
    out = outs[0] if (a_chunked or b_chunked) else outs[0][0]
    if ln is not None:
        return out, outs[1], outs[2]
    if ln_bwd is not None:
        return out, outs[1], outs[2], outs[3]
    return out if job is None else (out, job_out)


def _ln_math(z, g, b):
    mu = jnp.mean(z, axis=-1, keepdims=True)
    zc = z - mu
    var = jnp.mean(zc * zc, axis=-1, keepdims=True)
    return zc * lax.rsqrt(var + LN_EPS) * g + b


def _row_spec(tm, d):
    return pl.BlockSpec((tm, d), lambda i: (i, 0))


def _par_spec(d, rows=1):
    return pl.BlockSpec((rows, d), lambda i: (0, 0))


def _ln_bwd_tile(x, f, fscale, g, b, dy):
    _, vjp = jax.vjp(_ln_math, ALPHA * x + fscale * f, g, b)
    return vjp(dy)


def _ln_bwd_specs(tm, d):
    return ([_row_spec(tm, d), _row_spec(tm, d), _par_spec(d), _par_spec(d)],
            [_row_spec(tm, d), _row_spec(tm, d), _par_spec(d), _par_spec(d)])


def _ln_bwd_shapes(s, d):
    return [jax.ShapeDtypeStruct((s, d), F32), jax.ShapeDtypeStruct((s, d), BF16),
            jax.ShapeDtypeStruct((1, d), F32), jax.ShapeDtypeStruct((1, d), F32)]


def _ln_bwd_store(first, dz, dg, db, dz_ref, dzb_ref, dg_ref, db_ref):
    @pl.when(first)
    def _():
        dg_ref[...] = jnp.zeros_like(dg_ref)
        db_ref[...] = jnp.zeros_like(db_ref)

    dz_ref[...] = dz
    dzb_ref[...] = dz.astype(BF16)
    dg_ref[...] += dg
    db_ref[...] += db


def _load_weights_once(pairs, sem):
    @pl.when(pl.program_id(0) == 0)
    def _():
        copies = [pltpu.make_async_copy(src, dst, sem.at[n]) for n, (src, dst) in enumerate(pairs)]
        for c in copies:
            c.start()
        for c in copies:
            c.wait()


def _ffn_fwd(h, w13c, w2, *, name, job=None, ln=None, loss_head=None):
    s, d = h.shape
    tm = min(256, s)
    half = N_DEV // 2

    def body(*refs):
        if loss_head is not None:
            (h_ref, w13_hbm, w2_hbm, x_ref, t_ref, g_ref, b_ref, gu_ref, dz_ref, dzb_ref, dg_ref, db_ref, loss_ref,
             w13_v, w2_v, sem) = refs
        elif ln is not None:
            h_ref, w13_hbm, w2_hbm, g_ref, b_ref, gu_ref, f_ref, hb_ref, n_ref, nb_ref, w13_v, w2_v, sem = refs
        else:
            h_ref, w13_hbm, w2_hbm, gu_ref, f_ref, hb_ref, w13_v, w2_v, sem = refs
        _load_weights_once([(w13_hbm, w13_v), (w2_hbm, w2_v)], sem)
        hb = h_ref[...].astype(BF16)
        if loss_head is None:
            hb_ref[...] = hb
        acc = jnp.zeros((tm, d), F32)
        for k in range(half):
            g = _dot(hb, w13_v[k], NT)
            u = _dot(hb, w13_v[k + half], NT)
            gu_ref[k] = g.astype(BF16)
            gu_ref[k + half] = u.astype(BF16)
            a = g * jax.nn.sigmoid(g) * u
            acc = acc + _dot(a.astype(BF16), w2_v[pl.ds(k * FF_CHUNK, FF_CHUNK), :])
        if loss_head is not None:
            first = pl.program_id(0) == 0

            @pl.when(first)
            def _():
                loss_ref[...] = jnp.zeros_like(loss_ref)

            y, vjp = jax.vjp(_ln_math, ALPHA * x_ref[...] + 0.5 * acc, g_ref[...], b_ref[...])
            err = y - t_ref[...]
            sq = jnp.sum(err * err, axis=1, keepdims=True)
            loss_ref[...] += jnp.sum(sq, axis=0, keepdims=True) * (0.5 / d)
            dz, dg, db = vjp(err * (1.0 / d))
            _ln_bwd_store(first, dz, dg, db, dz_ref, dzb_ref, dg_ref, db_ref)
            return
        f_ref[...] = acc
        if ln is not None:
            hn = _ln_math(ALPHA * h_ref[...] + 0.5 * acc, g_ref[...], b_ref[...])
            n_ref[...] = hn
            nb_ref[...] = hn.astype(BF16)

    any_spec = pl.BlockSpec(memory_space=pl.ANY)
    gu_spec = pl.BlockSpec((N_DEV, tm, FF_CHUNK), lambda i: (0, i, 0))
    gu_shape = jax.ShapeDtypeStruct((N_DEV, s, FF_CHUNK), BF16)
    scratch = [pltpu.VMEM(w13c.shape, BF16), pltpu.VMEM(w2.shape, BF16), pltpu.SemaphoreType.DMA((2,))]
    if loss_head is not None:
        return _pallas_with_job(
            body, job, lambda: pl.program_id(0), s // tm, grid=(s // tm,),
            in_specs=[_row_spec(tm, d), any_spec, any_spec, _row_spec(tm, d), _row_spec(tm, d), _par_spec(d), _par_spec(d)],
            out_specs=[gu_spec] + _ln_bwd_specs(tm, d)[1] + [pl.BlockSpec((1, 1), lambda i: (0, 0))],
            out_shape=[gu_shape] + _ln_bwd_shapes(s, d) + [jax.ShapeDtypeStruct((1, 1), F32)],
            scratch_shapes=scratch, compiler_params=_cparams(1, VMEM_LIMIT_MB), name=name,
            args=(h, w13c, w2) + tuple(loss_head))
    n_ln = 0 if ln is None else 2
    return _pallas_with_job(
        body, job, lambda: pl.program_id(0), s // tm,
        grid=(s // tm,), in_specs=[_row_spec(tm, d), any_spec, any_spec] + [_par_spec(d)] * n_ln,
        out_specs=[pl.BlockSpec((N_DEV, tm, FF_CHUNK), lambda i: (0, i, 0)), _row_spec(tm, d), _row_spec(tm, d)]
                  + [_row_spec(tm, d)] * n_ln,
        out_shape=[jax.ShapeDtypeStruct((N_DEV, s, FF_CHUNK), BF16), jax.ShapeDtypeStruct((s, d), F32),
                   jax.ShapeDtypeStruct((s, d), BF16)]
                  + [jax.ShapeDtypeStruct((s, d), F32), jax.ShapeDtypeStruct((s, d), BF16)][:n_ln],
        scratch_shapes=[pltpu.VMEM(w13c.shape, BF16), pltpu.VMEM(w2.shape, BF16), pltpu.SemaphoreType.DMA((2,))],
        compiler_params=_cparams(1, VMEM_LIMIT_MB), name=name, args=(h, w13c, w2) + (() if ln is None else tuple(ln)))


def _ffn_bwd(dz, gu, w13c, w2, *, name, job=None, ln_bwd=None):
    s, d = dz.shape
    tm = min(256, s)
    half = N_DEV // 2

    def body(*refs):
        if ln_bwd is not None:
            (dz_ref, gu_ref, w13_hbm, w2_hbm, x_ref, f_ref, g_ref, b_ref, dzp_ref, dzpb_ref, dgu_ref, act_ref,
             dg_ref, db_ref, w13_v, w2_v, sem) = refs
        else:
            dz_ref, gu_ref, w13_hbm, w2_hbm, dh_ref, dgu_ref, act_ref, w13_v, w2_v, sem = refs
        _load_weights_once([(w13_hbm, w13_v), (w2_hbm, w2_v)], sem)
        dzv = dz_ref[...]
        dfb = (0.5 * dzv).astype(BF16)
        acc = ALPHA * dzv
        for k in range(half):
            g = gu_ref[k].astype(F32)
            u = gu_ref[k + half].astype(F32)
            da = _dot(dfb, w2_v[pl.ds(k * FF_CHUNK, FF_CHUNK), :], NT)
            sg = jax.nn.sigmoid(g)
            silu = g * sg
            dg = (da * u * (sg * (1.0 + g * (1.0 - sg)))).astype(BF16)
            du = (da * silu).astype(BF16)
            act_ref[k] = (silu * u).astype(BF16)
            dgu_ref[k] = dg
            dgu_ref[k + half] = du
            acc = acc + _dot(dg, w13_v[k]) + _dot(du, w13_v[k + half])
        if ln_bwd is not None:
            dzp, dg, db = _ln_bwd_tile(x_ref[...], f_ref[...], ln_bwd[2], g_ref[...], b_ref[...], acc)
            _ln_bwd_store(pl.program_id(0) == 0, dzp, dg, db, dzp_ref, dzpb_ref, dg_ref, db_ref)
        else:
            dh_ref[...] = acc

    any_spec = pl.BlockSpec(memory_space=pl.ANY)
    in_specs = [_row_spec(tm, d), pl.BlockSpec((N_DEV, tm, FF_CHUNK), lambda i: (0, i, 0)), any_spec, any_spec]
    mid_specs = [pl.BlockSpec((N_DEV, tm, FF_CHUNK), lambda i: (0, i, 0)), pl.BlockSpec((half, tm, FF_CHUNK), lambda i: (0, i, 0))]
    mid_shapes = [jax.ShapeDtypeStruct((N_DEV, s, FF_CHUNK), BF16), jax.ShapeDtypeStruct((half, s, FF_CHUNK), BF16)]
    args = (dz, gu, w13c, w2)
    if ln_bwd is not None:
        ln_in, ln_out = _ln_bwd_specs(tm, d)
        in_specs, out_specs = in_specs + ln_in, ln_out[:2] + mid_specs + ln_out[2:]
        shapes = _ln_bwd_shapes(s, d)
        out_shape = shapes[:2] + mid_shapes + shapes[2:]
        args += (ln_bwd[0], ln_bwd[1], ln_bwd[3], ln_bwd[4])
    else:
        out_specs, out_shape = [_row_spec(tm, d)] + mid_specs, [jax.ShapeDtypeStruct((s, d), F32)] + mid_shapes
    return _pallas_with_job(
        body, job, lambda: pl.program_id(0), s // tm, grid=(s // tm,), in_specs=in_specs, out_specs=out_specs,
        out_shape=out_shape,
        scratch_shapes=[pltpu.VMEM(w13c.shape, BF16), pltpu.VMEM(w2.shape, BF16), pltpu.SemaphoreType.DMA((2,))],
        compiler_params=_cparams(1, VMEM_LIMIT_MB), name=name, args=args)


def _shift_down(x, prev8, d):
    t, c = x.shape
    row = lax.broadcasted_iota(jnp.int32, (t, c), 0)
    xr = pltpu.roll(x, d, 0)
    pr = pltpu.roll(prev8, d, 0)
    if t > 8:
        pr = jnp.concatenate([pr, jnp.zeros((t - 8, c), x.dtype)], axis=0)
    return jnp.where(row < d, pr, xr)


def _shift_up(x, next8, d):
    t, c = x.shape
    row = lax.broadcasted_iota(jnp.int32, (t, c), 0)
    xr = pltpu.roll(x, t - d, 0)
    nr = pltpu.roll(next8, 8 - d, 0)
    if t > 8:
        nr = jnp.concatenate([jnp.zeros((t - 8, c), x.dtype), nr], axis=0)
    return jnp.where(row >= t - d, nr, xr)


def _scan_fwd(a, u, carry):
    t = a.shape[0]
    row = lax.broadcasted_iota(jnp.int32, a.shape, 0) % 8
    for d in (1, 2, 4):
        a_s, u_s = pltpu.roll(a, d, 0), pltpu.roll(u, d, 0)
        m = row >= d
        u = jnp.where(m, u + a * u_s, u)
        a = jnp.where(m, a * a_s, a)
    out = []
    for g in range(t // 8):
        hg = u[8 * g:8 * g + 8, :] + a[8 * g:8 * g + 8, :] * carry
        out.append(hg)
        carry = hg[7:8, :]
    return jnp.concatenate(out, axis=0)


def _scan_bwd(b, x, carry):
    t = b.shape[0]
    row = lax.broadcasted_iota(jnp.int32, b.shape, 0) % 8
    for d in (1, 2, 4):
        b_s, x_s = pltpu.roll(b, t - d, 0), pltpu.roll(x, t - d, 0)
        m = row < 8 - d
        x = jnp.where(m, x + b * x_s, x)
        b = jnp.where(m, b * b_s, b)
    out = [None] * (t // 8)
    for g in reversed(range(t // 8)):
        lg = x[8 * g:8 * g + 8, :] + b[8 * g:8 * g + 8, :] * carry
        out[g] = lg
        carry = lg[0:1, :]
    return jnp.concatenate(out, axis=0)


def _lru_elem(xc, pr, pi, b_r, b_i, lam):
    r = jax.nn.sigmoid(pr + b_r)
    ig = jax.nn.sigmoid(pi + b_i)
    softplus_neg_lam = jnp.maximum(-lam, 0.0) + jnp.log1p(jnp.exp(-jnp.abs(lam)))
    log_a = (-LRU_C) * r * softplus_neg_lam
    a = jnp.exp(log_a)
    one_minus_a2 = -jnp.tanh(log_a) * (jnp.exp(2.0 * log_a) + 1.0)
    u = jnp.sqrt(one_minus_a2) * (ig * xc)
    return a, u


def _gelu_tanh(x):
    return 0.5 * x * (1.0 + jnp.tanh(math.sqrt(2.0 / math.pi) * (x + 0.044715 * (x * x * x))))


def _conv_fwd(xb, prev8, w, b):
    out = b + w[3:4, :] * xb
    for d in range(1, CONV_W):
        out = out + w[3 - d:4 - d, :] * _shift_down(xb, prev8, d)
    return out


def _lru_fwd(xbg, conv_w, conv_b, w_rg, b_rg, w_ig, b_ig, lam, *, name):
    s = xbg.shape[0]
    c = LRU_W
    t = min(512, s)
    t8 = t // 8

    def body(xb_ref, gate_ref, prev_ref, cw_ref, cb_ref, wr_ref, br_ref, wi_ref, bi_ref, lam_ref,
             xc_ref, h_ref, y_ref, carry_ref):
        i = pl.program_id(0)

        @pl.when(i == 0)
        def _():
            carry_ref[...] = jnp.zeros_like(carry_ref)

        prev8 = jnp.where(i > 0, prev_ref[...], 0.0)
        xc = _conv_fwd(xb_ref[...], prev8, cw_ref[...], cb_ref[...])
        xcb = xc.astype(BF16)
        a, u = _lru_elem(xc, _dot(xcb, wr_ref[...]), _dot(xcb, wi_ref[...]), br_ref[...], bi_ref[...], lam_ref[...])
        h = _scan_fwd(a, u, carry_ref[0:1, :])
        carry_ref[...] = jnp.broadcast_to(h[t - 1:t, :], carry_ref.shape)
        xc_ref[...] = xc
        h_ref[...] = h
        y_ref[...] = h * _gelu_tanh(gate_ref[...])

    tile = lambda col: pl.BlockSpec((t, c), lambda i: (i, col))
    par = lambda rows: pl.BlockSpec((rows, c), lambda i: (0, 0))
    out_spec = pl.BlockSpec((t, c), lambda i: (i, 0))
    return pl.pallas_call(
        body, grid=(s // t,),
        in_specs=[tile(0), tile(1), pl.BlockSpec((8, c), lambda i: (jnp.maximum(i * t8 - 1, 0), 0)),
                  par(CONV_W), par(1), pl.BlockSpec((c, c), lambda i: (0, 0)), par(1),
                  pl.BlockSpec((c, c), lambda i: (0, 0)), par(1), par(1)],
        out_specs=[out_spec, out_spec, out_spec],
        out_shape=[jax.ShapeDtypeStruct((s, c), F32)] * 3,
        scratch_shapes=[pltpu.VMEM((8, c), F32)],
        compiler_params=_cparams(1, VMEM_LIMIT_MB), name=name)(
            xbg, xbg, xbg, conv_w, conv_b, w_rg, b_rg, w_ig, b_ig, lam)


def _lru_bwd(xc, xbg, hseq, dy, w_rg, b_rg, w_ig, b_ig, lam, *, name):
    s, c = xc.shape
    t = min(512, s)
    t8 = t // 8
    nt = s // t

    def body(xc_ref, gate_ref, h_ref, hprev_ref, dy_ref, wr_ref, br_ref, wi_ref, bi_ref, lam_ref,
             dxc_ref, dgate_ref, dwr_ref, dwi_ref, dbr_ref, dbi_ref, dlam_ref, a8_ref, l8_ref):
        i = pl.program_id(0)

        @pl.when(i == 0)
        def _():
            for r in (dwr_ref, dwi_ref, dbr_ref, dbi_ref, dlam_ref, a8_ref, l8_ref):
                r[...] = jnp.zeros_like(r)

        xc_v = xc_ref[...]
        xcb = xc_v.astype(BF16)
        (a, _), vjp = jax.vjp(_lru_elem, xc_v, _dot(xcb, wr_ref[...]), _dot(xcb, wi_ref[...]),
                              br_ref[...], bi_ref[...], lam_ref[...])
        gl, vjp_gelu = jax.vjp(_gelu_tanh, gate_ref[...])
        dyv, hs = dy_ref[...], h_ref[...]
        (dgate,) = vjp_gelu(dyv * hs)
        b_next = _shift_up(a, a8_ref[...], 1)
        lt = _scan_bwd(b_next, dyv * gl, l8_ref[0:1, :])
        a8_ref[...] = a[0:8, :]
        l8_ref[...] = lt[0:8, :]
        hprev8 = jnp.where(i < nt - 1, hprev_ref[...], 0.0)
        da = lt * _shift_down(hs, hprev8, 1)
        dxc_e, dpr, dpi, dbr, dbi, dlam = vjp((da, lt))
        dprb, dpib = dpr.astype(BF16), dpi.astype(BF16)
        dxc_ref[...] = dxc_e + _dot(dprb, wr_ref[...], NT) + _dot(dpib, wi_ref[...], NT)
        dgate_ref[...] = dgate.astype(BF16)
        xct = xc_v.T.astype(BF16)
        dwr_ref[...] += _dot(xct, dprb)
        dwi_ref[...] += _dot(xct, dpib)
        dbr_ref[...] += dbr
        dbi_ref[...] += dbi
        dlam_ref[...] += dlam

    rev = lambda col: pl.BlockSpec((t, c), lambda i: (nt - 1 - i, col))
    par = pl.BlockSpec((1, c), lambda i: (0, 0))
    sq = pl.BlockSpec((c, c), lambda i: (0, 0))
    return pl.pallas_call(
        body, grid=(nt,),
        in_specs=[rev(0), rev(1), rev(0), pl.BlockSpec((8, c), lambda i: (jnp.maximum((nt - 1 - i) * t8 - 1, 0), 0)),
                  rev(0), sq, par, sq, par, par],
        out_specs=[rev(0), rev(0), sq, sq, par, par, par],
        out_shape=[jax.ShapeDtypeStruct((s, c), F32), jax.ShapeDtypeStruct((s, c), BF16)]
                  + [jax.ShapeDtypeStruct((c, c), F32)] * 2 + [jax.ShapeDtypeStruct((1, c), F32)] * 3,
        scratch_shapes=[pltpu.VMEM((8, c), F32), pltpu.VMEM((8, c), F32)],
        compiler_params=_cparams(1, VMEM_LIMIT_MB), name=name)(
            xc, xbg, hseq, hseq, dy, w_rg, b_rg, w_ig, b_ig, lam)


def _conv_bwd(dxc, xbg, conv_w, *, name):
    s, c = dxc.shape
    t = min(512, s)
    t8 = t // 8
    nt = s // t

    def body(dxc_ref, next_ref, xb_ref, prev_ref, w_ref, dxb_ref, dw_ref, db_ref):
        i = pl.program_id(0)

        @pl.when(i == 0)
        def _():
            dw_ref[...] = jnp.zeros_like(dw_ref)
            db_ref[...] = jnp.zeros_like(db_ref)

        g = dxc_ref[...]
        xb = xb_ref[...]
        w = w_ref[...]
        next8 = jnp.where(i < nt - 1, next_ref[...], 0.0)
        prev8 = jnp.where(i > 0, prev_ref[...], 0.0)
        dxb = w[3:4, :] * g
        dw_ref[3:4, :] += jnp.sum(g * xb, axis=0, keepdims=True)
        for d in range(1, CONV_W):
            dxb = dxb + w[3 - d:4 - d, :] * _shift_up(g, next8, d)
            dw_ref[3 - d:4 - d, :] += jnp.sum(g * _shift_down(xb, prev8, d), axis=0, keepdims=True)
        dxb_ref[...] = dxb.astype(BF16)
        db_ref[...] += jnp.sum(g, axis=0, keepdims=True)

    tile = pl.BlockSpec((t, c), lambda i: (i, 0))
    return pl.pallas_call(
        body, grid=(nt,),
        in_specs=[tile, pl.BlockSpec((8, c), lambda i: (jnp.minimum((i + 1) * t8, s // 8 - 1), 0)),
                  tile, pl.BlockSpec((8, c), lambda i: (jnp.maximum(i * t8 - 1, 0), 0)),
                  pl.BlockSpec((CONV_W, c), lambda i: (0, 0))],
        out_specs=[tile, pl.BlockSpec((CONV_W, c), lambda i: (0, 0)), pl.BlockSpec((1, c), lambda i: (0, 0))],
        out_shape=[jax.ShapeDtypeStruct((s, c), BF16), jax.ShapeDtypeStruct((CONV_W, c), F32),
                   jax.ShapeDtypeStruct((1, c), F32)],
        compiler_params=_cparams(1), name=name)(dxc, dxc, xbg, xbg, conv_w)


def _hi_lo_dot(x, tri):
    hi = x.astype(BF16)
    lo = (x - hi.astype(F32)).astype(BF16)
    return _dot(hi, tri) + _dot(lo, tri)


def _tri(tk, inclusive):
    r = lax.broadcasted_iota(jnp.int32, (tk, tk), 0)
    c = lax.broadcasted_iota(jnp.int32, (tk, tk), 1)
    return ((r >= c) if inclusive else (r > c)).astype(BF16)


def _split_heads(x):
    first = lax.broadcasted_iota(jnp.int32, x.shape, 1) < SB_HEAD
    zero = jnp.zeros_like(x)
    return first, (jnp.where(first, x, zero), jnp.where(first, zero, x))


def _sb_softplus_terms(z):
    ls = jnp.minimum(z, 0.0) - jnp.log(1.0 + jnp.exp(-jnp.abs(z)))
    return ls, ls - z


def _sb_alive(runs):
    m = runs[0]
    for r in runs[1:]:
        m = jnp.maximum(m, r)
    return jnp.max(m) > SB_SKIP


def _sb_fwd(q, kv, *, name, job=None):
    s = q.shape[0]
    tb = min(256, s)
    rc = min(SB_ROWS, tb)
    npair = SB_W // SB_PAIR
    chains = [(h, c) for h in range(2) for c in range(tb // rc)]

    def body(q_ref, k_ref, v_ref, o_ref):
        i = pl.program_id(1)
        first, qs = _split_heads(q_ref[...])
        tri = _tri(tb, False)
        causal = lax.broadcasted_iota(jnp.int32, (tb, tb), 1) < lax.broadcasted_iota(jnp.int32, (tb, tb), 0)
        causal_c = [causal[c * rc:(c + 1) * rc, :] for c in range(tb // rc)]

        def group(j_lo, width, carry, diag_last):
            k0 = pl.multiple_of(j_lo * tb, tb)
            kb = k_ref[pl.ds(k0, width * tb), :]
            vb = v_ref[pl.ds(k0, width * tb), :]
            zw = [_dot(qs[h][c * rc:(c + 1) * rc, :], kb, NT) for h, c in chains]
            cols = [slice(t * tb, (t + 1) * tb) for t in range(width)]
            terms = [[_sb_softplus_terms(z[:, cl]) for cl in cols] for z in zw]
            lss = [[ls for ls, _ in tc] for tc in terms]
            ns = [[n for _, n in tc] for tc in terms]
            if diag_last:
                ns = [nc[:-1] + [jnp.where(causal_c[c], nc[-1], 0.0)] for nc, (h, c) in zip(ns, chains)]
            sufs = [[_hi_lo_dot(n, tri) for n in nc] for nc in ns]
            out = []
            for (h, c), lc, nc, sc, (run, acc) in zip(chains, lss, ns, sufs, carry):
                ws = [None] * width
                for t in reversed(range(width)):
                    w = jnp.exp(lc[t] + (sc[t] + run))
                    if diag_last and t == width - 1:
                        w = jnp.where(causal_c[c], w, 0.0)
                    ws[t] = w.astype(BF16)
                    run = run + jnp.sum(nc[t], axis=1, keepdims=True)
                out.append((run, acc + _dot(ws[0] if width == 1 else jnp.concatenate(ws, axis=1), vb)))
            return tuple(out)

        zero = (jnp.zeros((rc, 1), F32), jnp.zeros((rc, SB_PAIR), F32))
        carry = lax.cond(i == 0, lambda c: group(0, 1, c, True), lambda c: group(i - 1, 2, c, True),
                         (zero,) * len(chains))

        def step(st):
            j, _, c = st
            c = group(j, 1, c, False)
            return j - 1, _sb_alive([run for run, _ in c]), c

        _, _, carry = lax.while_loop(lambda st: jnp.logical_and(st[0] >= 0, st[1]), step,
                                     (i - 2, _sb_alive([run for run, _ in carry]), carry))
        accs = [jnp.concatenate([carry[n][1] for n, (h, c) in enumerate(chains) if h == hh], axis=0) for hh in range(2)]
        o_ref[...] = jnp.where(first, accs[0], accs[1])

    (o,), job_out = _pallas_with_job(
        body, job, lambda: pl.program_id(0) * (s // tb) + pl.program_id(1), npair * (s // tb),
        grid=(npair, s // tb),
        in_specs=[pl.BlockSpec((tb, SB_PAIR), lambda p, i: (i, p)),
                  pl.BlockSpec((s, SB_PAIR), lambda p, i: (0, p)),
                  pl.BlockSpec((s, SB_PAIR), lambda p, i: (0, npair + p))],
        out_specs=[pl.BlockSpec((tb, SB_PAIR), lambda p, i: (i, p))],
        out_shape=[jax.ShapeDtypeStruct((s, SB_W), F32)],
        compiler_params=_cparams(2, VMEM_LIMIT_MB), name=name, args=(q, kv, kv))
    return o, job_out


def _sb_bwd(q, kv, o, do, *, name, job=None):
    s = q.shape[0]
    tb = min(256, s)
    rc = min(SB_ROWS, tb)
    npair = SB_W // SB_PAIR
    chains = [(h, c) for h in range(2) for c in range(tb // rc)]

    def body(q_ref, k_ref, v_ref, o_ref, do_ref, dq_ref, dkb_ref, dvb_ref, dk_ref, dv_ref):
        i = pl.program_id(1)

        @pl.when(i == 0)
        def _():
            dk_ref[...] = jnp.zeros_like(dk_ref)
            dv_ref[...] = jnp.zeros_like(dv_ref)

        first, qs = _split_heads(q_ref[...])
        dob = do_ref[...].astype(BF16)
        _, dos = _split_heads(dob)
        prod = dob.astype(F32) * o_ref[...]
        deltas = (jnp.sum(jnp.where(first, prod, 0.0), axis=1, keepdims=True),
                  jnp.sum(jnp.where(first, 0.0, prod), axis=1, keepdims=True))
        tri_x, tri_i = _tri(tb, False), _tri(tb, True)
        causal = lax.broadcasted_iota(jnp.int32, (tb, tb), 1) < lax.broadcasted_iota(jnp.int32, (tb, tb), 0)
        causal_c = [causal[c * rc:(c + 1) * rc, :] for c in range(tb // rc)]
        rows = [slice(c * rc, (c + 1) * rc) for c in range(tb // rc)]

        def group(j_lo, width, carry, diag_last):
            k0 = pl.multiple_of(j_lo * tb, tb)
            kb = k_ref[pl.ds(k0, width * tb), :]
            vb = v_ref[pl.ds(k0, width * tb), :]
            cols = [slice(t * tb, (t + 1) * tb) for t in range(width)]
            zw = [_dot(qs[h][rows[c], :], kb, NT) for h, c in chains]
            dww = [_dot(dos[h][rows[c], :], vb, NT) for h, c in chains]
            terms = [[_sb_softplus_terms(z[:, cl]) for cl in cols] for z in zw]
            lss = [[ls for ls, _ in tc] for tc in terms]
            ns = [[n for _, n in tc] for tc in terms]
            if diag_last:
                ns = [nc[:-1] + [jnp.where(causal_c[c], nc[-1], 0.0)] for nc, (h, c) in zip(ns, chains)]
            sufs = [[_hi_lo_dot(n, tri_x) for n in nc] for nc in ns]
            wbs, gs, runs_n = [], [], []
            for (h, c), lc, nc, sc, dw, (run_n, _, _) in zip(chains, lss, ns, sufs, dww, carry):
                wb, g = [None] * width, [None] * width
                for t in reversed(range(width)):
                    w = jnp.exp(lc[t] + (sc[t] + run_n))
                    if diag_last and t == width - 1:
                        w = jnp.where(causal_c[c], w, 0.0)
                    wb[t] = w.astype(BF16)
                    g[t] = wb[t].astype(F32) * dw[:, cols[t]]
                    run_n = run_n + jnp.sum(nc[t], axis=1, keepdims=True)
                wbs.append(wb)
                gs.append(g)
                runs_n.append(run_n)
            gsufs = [[_hi_lo_dot(g, tri_i) for g in gc] for gc in gs]
            out = []
            dk_t = [jnp.zeros((tb, SB_PAIR), F32) for _ in range(width)]
            dv_t = [jnp.zeros((tb, SB_PAIR), F32) for _ in range(width)]
            for (h, c), lc, gc, gsc, wb, run_n, (_, run_g, dq_acc) in zip(chains, lss, gs, gsufs, wbs, runs_n, carry):
                qh, doh, delta = qs[h][rows[c], :], dos[h][rows[c], :], deltas[h][rows[c], :]
                dzb = [None] * width
                for t in reversed(range(width)):
                    pre = delta - (gsc[t] + run_g)
                    dz = gc[t] - jnp.exp(lc[t]) * (gc[t] + pre)
                    if diag_last and t == width - 1:
                        dz = jnp.where(causal_c[c], dz, 0.0)
                    run_g = run_g + jnp.sum(gc[t], axis=1, keepdims=True)
                    dzb[t] = dz.astype(BF16)
                    dk_t[t] = dk_t[t] + _dot(dz.T.astype(BF16), qh)
                    dv_t[t] = dv_t[t] + _dot(wb[t].astype(F32).T.astype(BF16), doh)
                out.append((run_n, run_g, dq_acc + _dot(dzb[0] if width == 1 else jnp.concatenate(dzb, axis=1), kb)))
            for t in range(width):
                dk_ref[pl.ds(pl.multiple_of((j_lo + t) * tb, tb), tb), :] += dk_t[t]
                dv_ref[pl.ds(pl.multiple_of((j_lo + t) * tb, tb), tb), :] += dv_t[t]
            return tuple(out)

        zero = (jnp.zeros((rc, 1), F32), jnp.zeros((rc, 1), F32), jnp.zeros((rc, SB_PAIR), F32))
        carry = lax.cond(i == 0, lambda c: group(0, 1, c, True), lambda c: group(i - 1, 2, c, True),
                         (zero,) * len(chains))

        def step(st):
            j, _, c = st
            c = group(j, 1, c, False)
            return j - 1, _sb_alive([r[0] for r in c]), c

        _, _, carry = lax.while_loop(lambda st: jnp.logical_and(st[0] >= 0, st[1]), step,
                                     (i - 2, _sb_alive([r[0] for r in carry]), carry))
        dqs = [jnp.concatenate([carry[n][2] for n, (h, c) in enumerate(chains) if h == hh], axis=0) for hh in range(2)]
        dq_ref[...] = (jnp.where(first, dqs[0], dqs[1]) * (1.0 / math.sqrt(SB_HEAD))).astype(BF16)

        @pl.when(i == s // tb - 1)
        def _():
            dkb_ref[...] = dk_ref[...].astype(BF16)
            dvb_ref[...] = dv_ref[...].astype(BF16)

    qtile = pl.BlockSpec((tb, SB_PAIR), lambda p, i: (i, p))
    col = pl.BlockSpec((s, SB_PAIR), lambda p, i: (0, p))
    return _pallas_with_job(
        body, job, lambda: pl.program_id(0) * (s // tb) + pl.program_id(1), npair * (s // tb),
        grid=(npair, s // tb),
        in_specs=[qtile, col, pl.BlockSpec((s, SB_PAIR), lambda p, i: (0, npair + p)), qtile, qtile],
        out_specs=[qtile, col, col],
        out_shape=[jax.ShapeDtypeStruct((s, SB_W), BF16)] * 3,
        scratch_shapes=[pltpu.VMEM((s, SB_PAIR), F32)] * 2,
        compiler_params=_cparams(2, VMEM_LIMIT_MB), name=name, args=(q, kv, kv, o, do))


def _mixnorm_math(yl, ys, gl, gs):
    def rms(x, g):
        return x * lax.rsqrt(jnp.mean(x * x, axis=-1, keepdims=True) + RMS_EPS) * g
    return rms(yl, gl), rms(ys, gs)


def _mixnorm_fwd(yl, ys, gl, gs, *, name):
    s, c = yl.shape
    tm = min(512, s)

    def body(yl_ref, ys_ref, gl_ref, gs_ref, y_ref):
        a, b = _mixnorm_math(yl_ref[...], ys_ref[...], gl_ref[...], gs_ref[...])
        y_ref[:, 0:c] = a.astype(BF16)
        y_ref[:, c:2 * c] = b.astype(BF16)

    return pl.pallas_call(
        body, grid=(s // tm,), in_specs=[_row_spec(tm, c), _row_spec(tm, c), _par_spec(c), _par_spec(c)],
        out_specs=_row_spec(tm, 2 * c), out_shape=jax.ShapeDtypeStruct((s, 2 * c), BF16),
        compiler_params=_cparams(1), name=name)(yl, ys, gl, gs)


def _cross_probs(qh, kh):
    sc = _dot(qh, kh, NT) * (1.0 / math.sqrt(MEM_HD))
    e = jnp.exp(sc - jnp.max(sc, axis=-1, keepdims=True))
    return e / jnp.sum(e, axis=-1, keepdims=True)


def _cross_fwd(q, kv, *, name):
    s, d = q.shape
    mlen = kv.shape[1]
    tm = min(512, s)

    def body(q_ref, kv_ref, o_ref):
        for h in range(MEM_HEADS):
            cols = slice(h * MEM_HD, (h + 1) * MEM_HD)
            p = _cross_probs(q_ref[:, cols].astype(BF16), kv_ref[h].astype(BF16))
            o_ref[:, cols] = _dot(p.astype(BF16), kv_ref[MEM_HEADS + h].astype(BF16)).astype(BF16)

    return pl.pallas_call(
        body, grid=(s // tm,),
        in_specs=[_row_spec(tm, d), pl.BlockSpec((2 * MEM_HEADS, mlen, MEM_HD), lambda i: (0, 0, 0))],
        out_specs=_row_spec(tm, d), out_shape=jax.ShapeDtypeStruct((s, d), BF16),
        compiler_params=_cparams(1, VMEM_LIMIT_MB), name=name)(q, kv)


def _cross_bwd(q, kv, do, *, name):
    s, d = q.shape
    mlen = kv.shape[1]
    tm = min(512, s)

    def body(q_ref, kv_ref, do_ref, dq_ref, dkv_ref):
        @pl.when(pl.program_id(0) == 0)
        def _():
            dkv_ref[...] = jnp.zeros_like(dkv_ref)

        for h in range(MEM_HEADS):
            cols = slice(h * MEM_HD, (h + 1) * MEM_HD)
            qh = q_ref[:, cols].astype(BF16)
            kh = kv_ref[h].astype(BF16)
            doh = do_ref[:, cols].astype(BF16)
            p = _cross_probs(qh, kh)
            dp = _dot(doh, kv_ref[MEM_HEADS + h].astype(BF16), NT)
            ds = p * (dp - jnp.sum(dp * p, axis=-1, keepdims=True)) * (1.0 / math.sqrt(MEM_HD))
            dq_ref[:, cols] = _dot(ds.astype(BF16), kh).astype(BF16)
            dkv_ref[h] += _dot(ds.T.astype(BF16), qh)
            dkv_ref[MEM_HEADS + h] += _dot(p.T.astype(BF16), doh)

    kv_spec = pl.BlockSpec((2 * MEM_HEADS, mlen, MEM_HD), lambda i: (0, 0, 0))
    return pl.pallas_call(
        body, grid=(s // tm,), in_specs=[_row_spec(tm, d), kv_spec, _row_spec(tm, d)],
        out_specs=[_row_spec(tm, d), kv_spec],
        out_shape=[jax.ShapeDtypeStruct((s, d), BF16), jax.ShapeDtypeStruct(kv.shape, F32)],
        compiler_params=_cparams(1, VMEM_LIMIT_MB), name=name)(q, kv, do)


def _proj_bwd_dx(parts, w_in, dz, *, name, job=None, ln_bwd=None):
    s = dz.shape[0]
    d = D_MODEL
    tm = min(256, s)
    widths = [p.shape[1] for p in parts]

    n_parts = len(parts)

    def body(*refs):
        part_refs, w_ref, dz_ref = refs[:n_parts], refs[n_parts], refs[n_parts + 1]
        acc = ALPHA * dz_ref[...]
        off = 0
        for p_ref, wd in zip(part_refs, widths):
            acc = acc + _dot(p_ref[...].astype(BF16), w_ref[off:off + wd, :])
            off += wd
        if ln_bwd is not None:
            x_ref, f_ref, g_ref, b_ref = refs[n_parts + 2:n_parts + 6]
            dzp, dg, db = _ln_bwd_tile(x_ref[...], f_ref[...], ln_bwd[2], g_ref[...], b_ref[...], acc)
            _ln_bwd_store(pl.program_id(0) == 0, dzp, dg, db, *refs[n_parts + 6:])
        else:
            refs[-1][...] = acc

    in_specs = [_row_spec(tm, wd) for wd in widths] + [pl.BlockSpec(w_in.shape, lambda i: (0, 0)), _row_spec(tm, d)]
    args = (*parts, w_in, dz)
    if ln_bwd is not None:
        ln_in, out_specs = _ln_bwd_specs(tm, d)
        in_specs, out_shape = in_specs + ln_in, _ln_bwd_shapes(s, d)
        args += (ln_bwd[0], ln_bwd[1], ln_bwd[3], ln_bwd[4])
    else:
        out_specs, out_shape = [_row_spec(tm, d)], [jax.ShapeDtypeStruct((s, d), F32)]
    outs, job_out = _pallas_with_job(
        body, job, lambda: pl.program_id(0), s // tm, grid=(s // tm,), in_specs=in_specs, out_specs=out_specs,
        out_shape=out_shape, compiler_params=_cparams(1, VMEM_LIMIT_MB), name=name, args=args)
    return (outs[0] if ln_bwd is None else outs), job_out


def _proj_fwd(hb, w_in_t, *, name):
    s, d = hb.shape
    tm = min(512, s)
    lru, sb = 2 * LRU_W, SB_W

    def body(h_ref, w_ref, xbg_ref, q_ref, kv_ref):
        h = h_ref[...]
        xbg_ref[...] = _dot(h, w_ref[0:lru, :], NT)
        q_ref[...] = (_dot(h, w_ref[lru:lru + sb, :], NT) * (1.0 / math.sqrt(SB_HEAD))).astype(BF16)
        kv_ref[...] = _dot(h, w_ref[lru + sb:lru + 3 * sb, :], NT).astype(BF16)

    return pl.pallas_call(
        body, grid=(s // tm,), in_specs=[_row_spec(tm, d), pl.BlockSpec(w_in_t.shape, lambda i: (0, 0))],
        out_specs=[_row_spec(tm, lru), _row_spec(tm, sb), _row_spec(tm, 2 * sb)],
        out_shape=[jax.ShapeDtypeStruct((s, lru), F32), jax.ShapeDtypeStruct((s, sb), BF16),
                   jax.ShapeDtypeStruct((s, 2 * sb), BF16)],
        compiler_params=_cparams(1, VMEM_LIMIT_MB), name=name)(hb, w_in_t)


def _mix_bwd(dzb, w_out, yl, ys, gl, gs, *, name, job=None):
    s, c = yl.shape
    tm = min(512, s)

    def body(dz_ref, w_ref, yl_ref, ys_ref, gl_ref, gs_ref, dyl_ref, dys_ref, dgl_ref, dgs_ref):
        @pl.when(pl.program_id(0) == 0)
        def _():
            dgl_ref[...] = jnp.zeros_like(dgl_ref)
            dgs_ref[...] = jnp.zeros_like(dgs_ref)

        dy = _dot(dz_ref[...], w_ref[...], NT)
        _, vjp = jax.vjp(_mixnorm_math, yl_ref[...], ys_ref[...], gl_ref[...], gs_ref[...])
        dyl, dys, dgl, dgs = vjp((dy[:, 0:c], dy[:, c:2 * c]))
        dyl_ref[...] = dyl
        dys_ref[...] = dys
        dgl_ref[...] += dgl
        dgs_ref[...] += dgs

    return _pallas_with_job(
        body, job, lambda: pl.program_id(0), s // tm, grid=(s // tm,),
        in_specs=[_row_spec(tm, 2 * c), pl.BlockSpec(w_out.shape, lambda i: (0, 0)), _row_spec(tm, c), _row_spec(tm, c),
                  _par_spec(c), _par_spec(c)],
        out_specs=[_row_spec(tm, c), _row_spec(tm, c), _par_spec(c), _par_spec(c)],
        out_shape=[jax.ShapeDtypeStruct((s, c), F32)] * 2 + [jax.ShapeDtypeStruct((1, c), F32)] * 2,
        compiler_params=_cparams(1, VMEM_LIMIT_MB), name=name, args=(dzb, w_out, yl, ys, gl, gs))


def _proj_dw(h, parts, *, name):
    s, d = h.shape
    tk = min(1024, s)
    widths = [p.shape[1] for p in parts]

    def body(*refs):
        h_ref, part_refs, o_ref, acc_ref = refs[0], refs[1:-2], refs[-2], refs[-1]
        k = pl.program_id(0)

        @pl.when(k == 0)
        def _():
            acc_ref[...] = jnp.zeros_like(acc_ref)

        hv = h_ref[...].astype(BF16)
        off = 0
        for p_ref, wd in zip(part_refs, widths):
            acc_ref[off:off + wd, :] += _dot(p_ref[...].astype(F32).T.astype(BF16), hv)
            off += wd

        @pl.when(k == s // tk - 1)
        def _():
            o_ref[...] = acc_ref[...].astype(BF16)

    return pl.pallas_call(
        body, grid=(s // tk,), in_specs=[_row_spec(tk, d)] + [_row_spec(tk, wd) for wd in widths],
        out_specs=pl.BlockSpec((sum(widths), d), lambda k: (0, 0)),
        out_shape=jax.ShapeDtypeStruct((sum(widths), d), BF16),
        scratch_shapes=[pltpu.VMEM((sum(widths), d), F32)],
        compiler_params=_cparams(1, VMEM_LIMIT_MB), name=name)(h, *parts)


def _mesh_pos():
    return lax.axis_index("x"), lax.axis_index("y"), lax.axis_index("c")


class _GatherJob:
    n_phases = 3

    def __init__(self, shards):
        n = len(shards)
        self.inputs = list(shards)
        self.out_shape = [jax.ShapeDtypeStruct((N_DEV,) + a.shape, a.dtype) for a in shards]
        self.scratch = [pltpu.SemaphoreType.DMA((n, 7)), pltpu.SemaphoreType.DMA((n, 7)), pltpu.SemaphoreType.DMA((n,))]

    def phase(self, k, ins, outs, sems):
        send_sems, recv_sems, local_sems = sems
        n = len(ins)
        x, y, c = _mesh_pos()
        me, sibling = (x, y, c), (x, y, 1 - c)
        chips = [(1 - x, y), (x, 1 - y), (1 - x, 1 - y)]

        def copy(a, slot, block, to, src=None):
            dst = outs[a].at[4 * block[0] + 2 * block[1] + block[2]]
            return pltpu.make_async_remote_copy(
                src_ref=dst if src is None else src, dst_ref=dst,
                send_sem=send_sems.at[a, slot], recv_sem=recv_sems.at[a, slot], device_id=to, device_id_type=MESH)

        def mine():
            return [pltpu.make_async_copy(ins[a], outs[a].at[4 * x + 2 * y + c], local_sems.at[a]) for a in range(n)]

        def first():
            return [cp for a in range(n) for cp in
                    [copy(a, 0, me, sibling, src=ins[a])] + [copy(a, 1 + j, me, (*chip, c), src=ins[a]) for j, chip in enumerate(chips)]]

        def passed(a, j):
            return copy(a, 4 + j, (*chips[j], c), sibling)

        if k == 0:
            for cp in mine() + first():
                cp.start()
        elif k == 1:
            for a in range(n):
                for j, chip in enumerate(chips):
                    copy(a, 1 + j, (*chip, c), me).wait_recv()
                    passed(a, j).start()
        else:
            for a in range(n):
                copy(a, 0, sibling, me).wait_recv()
                for j, chip in enumerate(chips):
                    copy(a, 4 + j, (*chip, 1 - c), me).wait_recv()
            for cp in first() + [passed(a, j) for a in range(n) for j in range(3)]:
                cp.wait_send()
            for cp in mine():
                cp.wait()


class _SiblingJob:
    n_phases = 2

    def __init__(self, grads):
        n, half = len(grads), N_DEV // 2
        self.inputs = list(grads)
        self.out_shape = [jax.ShapeDtypeStruct((half,) + g.shape[1:], g.dtype) for g in grads]
        self.scratch = [pltpu.SemaphoreType.DMA((n, half)), pltpu.SemaphoreType.DMA((n, half))]

    def phase(self, k, ins, outs, sems):
        send_sems, recv_sems = sems
        x, y, c = _mesh_pos()
        copies = [pltpu.make_async_remote_copy(
            src_ref=ins[a].at[2 * j + (1 - c)], dst_ref=outs[a].at[j], send_sem=send_sems.at[a, j],
            recv_sem=recv_sems.at[a, j], device_id=(x, y, 1 - c), device_id_type=MESH)
            for a in range(len(ins)) for j in range(N_DEV // 2)]
        for cp in copies:
            if k == 0:
                cp.start()
            else:
                cp.wait()


class _ChipJob:
    n_phases = 2

    def __init__(self, partials):
        n, half = len(partials), N_DEV // 2
        self.inputs = list(partials)
        self.out_shape = [jax.ShapeDtypeStruct(p.shape, p.dtype) for p in partials]
        self.scratch = [pltpu.SemaphoreType.DMA((n, half)), pltpu.SemaphoreType.DMA((n, half)), pltpu.SemaphoreType.DMA((n,))]

    def phase(self, k, ins, outs, sems):
        send_sems, recv_sems, local_sems = sems
        n, half = len(ins), N_DEV // 2
        x, y, c = _mesh_pos()
        jme = 2 * x + y
        local = [pltpu.make_async_copy(ins[a].at[jme], outs[a].at[jme], local_sems.at[a]) for a in range(n)]
        for cp in local:
            if k == 0:
                cp.start()
            else:
                cp.wait()
        for j in range(half):
            @pl.when(jme != j)
            def _(j=j):
                for a in range(n):
                    if k == 0:
                        pltpu.make_async_remote_copy(
                            src_ref=ins[a].at[j], dst_ref=outs[a].at[jme], send_sem=send_sems.at[a, j],
                            recv_sem=recv_sems.at[a, jme], device_id=(j // 2, j % 2, c), device_id_type=MESH).start()
                    else:
                        arrive = pltpu.make_async_remote_copy(
                            src_ref=ins[a].at[j], dst_ref=outs[a].at[j], send_sem=send_sems.at[a, j],
                            recv_sem=recv_sems.at[a, j], device_id=(j // 2, j % 2, c), device_id_type=MESH)
                        arrive.wait_recv()
                        arrive.wait_send()


def _phase_steps(n_phases, n_steps):
    return [0, n_steps - 1] if n_phases == 2 else [0, n_steps // 2, n_steps - 1]


def _pallas_with_job(body, job, step_fn, n_steps, *, grid, in_specs, out_specs, out_shape, scratch_shapes=(),
                     compiler_params, name, args):
    in_specs, out_specs, out_shape, scratch_shapes = list(in_specs), list(out_specs), list(out_shape), list(scratch_shapes)
    if job is None:
        outs = pl.pallas_call(body, grid=grid, in_specs=in_specs, out_specs=out_specs, out_shape=out_shape,
                              scratch_shapes=scratch_shapes, compiler_params=compiler_params, name=name)(*args)
        return list(outs), None
    ni, no, ns = len(in_specs), len(out_specs), len(scratch_shapes)
    ji, jo = len(job.inputs), len(job.out_shape)
    steps = _phase_steps(job.n_phases, n_steps)

    def wrapped(*refs):
        own_in, job_in = refs[:ni], refs[ni:ni + ji]
        own_out, job_out = refs[ni + ji:ni + ji + no], refs[ni + ji + no:ni + ji + no + jo]
        own_scr, job_scr = refs[ni + ji + no + jo:ni + ji + no + jo + ns], refs[ni + ji + no + jo + ns:]
        step = step_fn()
        for k, at in enumerate(steps):
            @pl.when(step == at)
            def _(k=k):
                job.phase(k, job_in, job_out, job_scr)
        body(*own_in, *own_out, *own_scr)

    any_spec = pl.BlockSpec(memory_space=pl.ANY)
    outs = pl.pallas_call(
        wrapped, grid=grid, in_specs=in_specs + [any_spec] * ji, out_specs=out_specs + [any_spec] * jo,
        out_shape=out_shape + list(job.out_shape), scratch_shapes=scratch_shapes + list(job.scratch),
        compiler_params=compiler_params, name=name)(*args, *job.inputs)
    return list(outs[:no]), list(outs[no:])


def _run_job(job, *, name):
    ji, jo = len(job.inputs), len(job.out_shape)

    def body(*refs):
        for k in range(job.n_phases):
            job.phase(k, refs[:ji], refs[ji:ji + jo], refs[ji + jo:])

    any_spec = pl.BlockSpec(memory_space=pl.ANY)
    return pl.pallas_call(body, in_specs=[any_spec] * ji, out_specs=[any_spec] * jo, out_shape=list(job.out_shape),
                          scratch_shapes=list(job.scratch), name=name)(*job.inputs)


def _pair_add(grad, recv, cj_idx, *, name):
    half, r, cdim = recv.shape
    tr = _row_tile(r, 512)

    def body(cj_ref, g_ref, r_ref, own_ref, tb_ref):
        t = g_ref[...].astype(F32) + r_ref[...].astype(F32)
        tb_ref[...] = t.astype(BF16)

        @pl.when(pl.program_id(1) == cj_ref[1])
        def _():
            own_ref[...] = t

    grid_spec = pltpu.PrefetchScalarGridSpec(
        num_scalar_prefetch=1, grid=(r // tr, half),
        in_specs=[pl.BlockSpec((None, tr, cdim), lambda i, j, cj: (2 * j + cj[0], i, 0)),
                  pl.BlockSpec((None, tr, cdim), lambda i, j, cj: (j, i, 0))],
        out_specs=[pl.BlockSpec((tr, cdim), lambda i, j, cj: (i, 0)),
                   pl.BlockSpec((None, tr, cdim), lambda i, j, cj: (j, i, 0))])
    return pl.pallas_call(
        body, grid_spec=grid_spec,
        out_shape=[jax.ShapeDtypeStruct((r, cdim), F32), jax.ShapeDtypeStruct(recv.shape, BF16)],
        compiler_params=_cparams(2), name=name)(cj_idx, grad, recv)


def _adam_math(w, g, m, v):
    m = ADAM_B1 * m + (1.0 - ADAM_B1) * g
    v = ADAM_B2 * v + (1.0 - ADAM_B2) * (g * g)
    m_hat = m / (1.0 - ADAM_B1 ** ADAM_STEP)
    v_hat = v / (1.0 - ADAM_B2 ** ADAM_STEP)
    delta = -ADAM_LR * (m_hat / (jnp.sqrt(v_hat) + ADAM_EPS) + ADAM_WD * w)
    return delta, m, v


def _reduce_adam(own, recv, j_idx, w, m, v, *, name):
    half, r, cdim = recv.shape
    tr = _row_tile(r, 256)

    def body(j_ref, own_ref, recv_ref, w_ref, m_ref, v_ref, g_ref, d_ref, nm_ref, nv_ref):
        jme = j_ref[0]
        g = own_ref[...]
        for sl in range(half):
            g = g + jnp.where(sl == jme, 0.0, recv_ref[sl].astype(F32))
        delta, nm, nv = _adam_math(w_ref[...], g, m_ref[...], v_ref[...])
        g_ref[...] = g
        d_ref[...] = delta
        nm_ref[...] = nm
        nv_ref[...] = nv

    tile = pl.BlockSpec((tr, cdim), lambda i, j_ref: (i, 0))
    grid_spec = pltpu.PrefetchScalarGridSpec(
        num_scalar_prefetch=1, grid=(r // tr,),
        in_specs=[tile, pl.BlockSpec((half, tr, cdim), lambda i, j_ref: (0, i, 0)), tile, tile, tile],
        out_specs=[tile] * 4)
    return pl.pallas_call(
        body, grid_spec=grid_spec, out_shape=[jax.ShapeDtypeStruct((r, cdim), F32)] * 4,
        compiler_params=_cparams(1), name=name)(j_idx, own, recv, w, m, v)


class _SmallGatherJob:
    n_phases = 2

    def __init__(self, pack):
        self.inputs = [pack]
        self.out_shape = [jax.ShapeDtypeStruct((N_DEV,) + pack.shape, pack.dtype)]
        self.scratch = [pltpu.SemaphoreType.DMA((N_DEV,)), pltpu.SemaphoreType.DMA((N_DEV,)), pltpu.SemaphoreType.DMA(())]

    def phase(self, k, ins, outs, sems):
        (pack,), (land,), (send_sems, recv_sems, local_sem) = ins, outs, sems
        x, y, c = _mesh_pos()
        me = 4 * x + 2 * y + c
        mine = pltpu.make_async_copy(pack, land.at[me], local_sem)
        if k == 0:
            mine.start()
        else:
            mine.wait()
        for t in range(N_DEV):
            @pl.when(me != t)
            def _(t=t):
                peer = (t // 4, (t // 2) % 2, t % 2)
                if k == 0:
                    pltpu.make_async_remote_copy(
                        src_ref=pack, dst_ref=land.at[me], send_sem=send_sems.at[t], recv_sem=recv_sems.at[me],
                        device_id=peer, device_id_type=MESH).start()
                else:
                    cp = pltpu.make_async_remote_copy(
                        src_ref=pack, dst_ref=land.at[t], send_sem=send_sems.at[t], recv_sem=recv_sems.at[t],
                        device_id=peer, device_id_type=MESH)
                    cp.wait_recv()
                    cp.wait_send()


class _Jobs:
    def __init__(self, jobs):
        self.jobs = jobs
        self.n_phases = jobs[0].n_phases
        assert all(j.n_phases == self.n_phases for j in jobs)
        self.inputs = [a for j in jobs for a in j.inputs]
        self.out_shape = [o for j in jobs for o in j.out_shape]
        self.scratch = [sc for j in jobs for sc in j.scratch]

    def phase(self, k, ins, outs, sems):
        i = o = sc = 0
        for j in self.jobs:
            ni, no, nsc = len(j.inputs), len(j.out_shape), len(j.scratch)
            j.phase(k, ins[i:i + ni], outs[o:o + no], sems[sc:sc + nsc])
            i, o, sc = i + ni, o + no, sc + nsc


def _adam_small(w, land, m, v, *, name):
    def body(w_ref, land_ref, m_ref, v_ref, g_ref, d_ref, nm_ref, nv_ref):
        total = land_ref[0]
        for t in range(1, N_DEV):
            total = total + land_ref[t]
        g_ref[...] = total
        d_ref[...], nm_ref[...], nv_ref[...] = _adam_math(w_ref[...], total, m_ref[...], v_ref[...])

    vmem = pl.BlockSpec(memory_space=pltpu.VMEM)
    return pl.pallas_call(
        body, in_specs=[vmem] * 4, out_specs=[vmem] * 4, out_shape=[jax.ShapeDtypeStruct(w.shape, F32)] * 4,
        name=name)(w, land, m, v)


def _block_diag(w):
    h, a, b = w.shape
    eye = jnp.eye(h, dtype=w.dtype)
    return (eye[:, None, :, None] * w[:, :, None, :]).reshape(h * a, h * b)


def _diag_blocks(w, h):
    a = w.shape[0] // h
    return jnp.stack([w[i * a:(i + 1) * a, i * a:(i + 1) * a] for i in range(h)])


def _local_step(x, mem, target, p, hooks=None):
    g = {}
    p = dict(p)
    w_rg, w_ig = _block_diag(p["w_rgate"]).astype(BF16), _block_diag(p["w_igate"]).astype(BF16)

    (gu1, f1, xb, h1, h1b), arrived = _ffn_fwd(x, p["ffn1_w13"], p["ffn1_w2"], name="ffn1_fwd", ln=(p["ln1_g"], p["ln1_b"]),
                                               job=hooks and hooks.gather_job(_GATHER_MIXER))
    if hooks:
        p.update(hooks.gathered(_GATHER_MIXER, arrived))
    w_in = p["w_in"]
    xbg, q_sb, kv_sb = _proj_fwd(h1b, w_in, name="proj_fwd")
    xc, hseq, y_lru = _lru_fwd(xbg, p["conv_w"], p["conv_b"], w_rg, p["b_rgate"], w_ig, p["b_igate"],
                               p["lru_lambda"], name="lru_fwd")
    y_sb, arrived = _sb_fwd(q_sb, kv_sb, name="sb_fwd", job=hooks and hooks.gather_job(_GATHER_FFN2))
    if hooks:
        p.update(hooks.gathered(_GATHER_FFN2, arrived))
    ymix = _mixnorm_fwd(y_lru, y_sb, p["g_lru"], p["g_sb"], name="mixnorm_fwd")
    mix, h2, h2b = _matmul(ymix, p["w_out"], ln=(h1, 1.0, p["ln2_g"], p["ln2_b"]), name="mix_out_ln2")
    qm = _matmul(h2b, p["mem_wq"], out_dtype=BF16, name="cross_q")
    kv = _matmul(mem, p["mem_wkv"], name="cross_kv")
    o_cross = _cross_fwd(qm, kv, name="cross_fwd")
    cross, h3, h3b = _matmul(o_cross, p["mem_wo"], ln=(h2, 1.0, p["ln3_g"], p["ln3_b"]), name="cross_out_ln3")
    (gu2, dz4, dz4b, g["ln4_g"], g["ln4_b"], loss), _ = _ffn_fwd(
        h3b, p["ffn2_w13"], p["ffn2_w2"], loss_head=(h3, target, p["ln4_g"], p["ln4_b"]), name="ffn2_fwd_loss")
    (dz3, dz3b, dgu2, act2, g["ln3_g"], g["ln3_b"]), _ = _ffn_bwd(
        dz4, gu2, p["ffn2_w13"], p["ffn2_w2"], ln_bwd=(h2, cross, 1.0, p["ln3_g"], p["ln3_b"]), name="ffn2_bwd_ln3")
    g["ffn2_w13"] = _matmul(dgu2, h3b, trans_a=True, tm=FF_CHUNK, tn=1024, tk=2048, out_dtype=BF16, name="ffn2_dw13")
    g["ffn2_w2"] = _matmul(act2, dz4b, trans_a=True, tm=FF_CHUNK, tn=1024, tk=2048, scale=0.5, out_dtype=BF16, name="ffn2_dw2")
    g["mem_wo"] = _matmul(o_cross, dz3b, trans_a=True, tm=1024, tn=1024, tk=1024, out_dtype=BF16, name="cross_dwo")
    do_cross = _matmul(dz3b, p["mem_wo"], trans_b=True, out_dtype=BF16, name="cross_do")
    dqm, dkv = _cross_bwd(qm, kv, do_cross, name="cross_bwd")
    g["mem_wq"] = _matmul(h2b, dqm, trans_a=True, tm=1024, tn=1024, tk=1024, out_dtype=BF16, name="cross_dwq")
    g["mem_wkv"] = _matmul(mem, dkv, trans_a=True, tm=1024, tn=MEM_HD, tk=256, out_dtype=BF16, name="cross_dwkv")
    dz2, dz2b, g["ln2_g"], g["ln2_b"] = _matmul(dqm, p["mem_wq"], trans_b=True, res=dz3, res_coeff=ALPHA,
                                                ln_bwd=(h1, mix, 1.0, p["ln2_g"], p["ln2_b"]), name="cross_dh_ln2")
    g["w_out"] = _matmul(ymix, dz2b, trans_a=True, tm=1024, tn=1024, tk=1024, out_dtype=BF16, name="mix_dwout")
    (dy_lru, dy_sb, g["g_lru"], g["g_sb"]), arrived = _mix_bwd(
        dz2b, p["w_out"], y_lru, y_sb, p["g_lru"], p["g_sb"], name="mix_bwd", job=hooks and hooks.sibling_job(_REDUCE_EARLY, g))
    (dq, dk, dv), arrived = _sb_bwd(q_sb, kv_sb, y_sb, dy_sb, name="sb_bwd",
                                    job=hooks and hooks.chip_job(_REDUCE_EARLY, arrived))
    if hooks:
        hooks.reduced(_REDUCE_EARLY, arrived)
    dxc, dgate, dwr, dwi, g["b_rgate"], g["b_igate"], g["lru_lambda"] = _lru_bwd(
        xc, xbg, hseq, dy_lru, w_rg, p["b_rgate"], w_ig, p["b_igate"], p["lru_lambda"], name="lru_bwd")
    g["w_rgate"], g["w_igate"] = _diag_blocks(dwr, 8), _diag_blocks(dwi, 8)
    dxb, g["conv_w"], g["conv_b"] = _conv_bwd(dxc, xbg, p["conv_w"], name="conv_bwd")
    parts = [dxb, dgate, dq, dk, dv]
    g["w_in"] = _proj_dw(h1b, parts, name="proj_dw")
    (dz1, dz1b, g["ln1_g"], g["ln1_b"]), arrived = _proj_bwd_dx(
        parts, w_in, dz2, ln_bwd=(x, f1, 0.5, p["ln1_g"], p["ln1_b"]), name="proj_dh_ln1",
        job=hooks and hooks.sibling_job(['w_in'], g))
    (grad_x, dgu1, act1), _ = _ffn_bwd(dz1, gu1, p["ffn1_w13"], p["ffn1_w2"], name="ffn1_bwd")
    dw13 = dict(trans_a=True, tm=FF_CHUNK, tn=1024, tk=2048, out_dtype=BF16, name="ffn1_dw13")
    dw2 = dict(trans_a=True, tm=FF_CHUNK, tn=1024, tk=2048, scale=0.5, out_dtype=BF16, name="ffn1_dw2")
    if not hooks:
        g["ffn1_w13"] = _matmul(dgu1, xb, **dw13)
        g["ffn1_w2"] = _matmul(act1, dz1b, **dw2)
        return loss, grad_x, g
    g["ffn1_w13"], arrived = _matmul(dgu1, xb, job=_Jobs([hooks.chip_job(['w_in'], arrived), hooks.small_job(g, loss)]), **dw13)
    hooks.reduced(['w_in'], arrived[:1])
    hooks.small_land = arrived[1]
    arrived = _run_job(hooks.sibling_job(['ffn1_w13'], g), name="reduce_sibling_w13")
    g["ffn1_w2"], arrived = _matmul(act1, dz1b, job=hooks.chip_job(['ffn1_w13'], arrived), **dw2)
    hooks.reduced(['ffn1_w13'], arrived)
    arrived = _run_job(hooks.sibling_job(['ffn1_w2'], g), name="reduce_sibling_w2")
    hooks.reduced(['ffn1_w2'], _run_job(hooks.chip_job(['ffn1_w2'], arrived), name="reduce_chips_w2"))
    return loss, grad_x, g


_WEIGHTS = ['ffn1_w13', 'ffn1_w2', 'ln1_g', 'ln1_b', 'w_in', 'conv_w', 'conv_b', 'w_rgate', 'b_rgate', 'w_igate',
            'b_igate', 'lru_lambda', 'g_lru', 'g_sb', 'w_out', 'ln2_g', 'ln2_b', 'mem_wq', 'mem_wkv', 'mem_wo',
            'ln3_g', 'ln3_b', 'ffn2_w13', 'ffn2_w2', 'ln4_g', 'ln4_b']
_SHARDED = ['ffn1_w13', 'ffn1_w2', 'w_in', 'w_out', 'mem_wq', 'mem_wkv', 'mem_wo', 'ffn2_w13', 'ffn2_w2']
_SMALL = ['ln1_g', 'ln1_b', 'ln2_g', 'ln2_b', 'ln3_g', 'ln3_b', 'ln4_g', 'ln4_b', 'conv_b', 'b_rgate', 'b_igate',
          'lru_lambda', 'g_lru', 'g_sb', 'w_rgate', 'w_igate']


def _pack_small(d, conv_w_full=None, loss=None):
    rows = [d[n].reshape(-1) for n in _SMALL]
    if conv_w_full is not None:
        rows.append(conv_w_full.reshape(-1))
    if loss is not None:
        rows.append(loss.reshape(-1))
    flat = jnp.concatenate(rows)
    pad = (-flat.shape[0]) % (8 * D_MODEL)
    return jnp.pad(flat, (0, pad)).reshape(-1, D_MODEL)


def _unpack_small(pack, like, with_conv):
    flat = pack.reshape(-1)
    out, off = {}, 0
    for n in _SMALL:
        size = math.prod(like[n].shape)
        out[n] = flat[off:off + size].reshape(like[n].shape)
        off += size
    conv = flat[off:off + CONV_W * LRU_W].reshape(CONV_W, LRU_W) if with_conv else None
    return out, conv


_GATHER_FIRST = ['ffn1_w13', 'ffn1_w2']
_GATHER_MIXER = ['w_in', 'conv_w', 'w_out', 'mem_wq', 'mem_wkv', 'mem_wo']
_GATHER_FFN2 = ['ffn2_w13', 'ffn2_w2']
_REDUCE_EARLY = ['ffn2_w13', 'ffn2_w2', 'mem_wo', 'mem_wq', 'mem_wkv', 'w_out']


_TRANSPOSED = ('ffn1_w13', 'ffn2_w13', 'w_in')


def _own_layout(n, a):
    return a.T if n in _TRANSPOSED else a


def _weight_layout(n, full):
    if n in ('ffn1_w13', 'ffn2_w13', 'mem_wkv'):
        return full
    if n == 'conv_w':
        return full.transpose(1, 0, 2).reshape(CONV_W, LRU_W)
    return full.reshape(-1, full.shape[-1])


def _grad_blocks(n, g):
    return g.reshape((N_DEV, -1, g.shape[-1]))


class _Hooks:
    def __init__(self, w, cj_idx):
        self.w, self.cj_idx = w, cj_idx
        self.sums, self.recv = {}, {}

    def shard(self, n):
        return self.w[n][0] if n == 'conv_w' else _own_layout(n, self.w[n][0]).astype(BF16)

    def gather_job(self, names):
        return _GatherJob([self.shard(n) for n in names])

    def gathered(self, names, outs):
        return {n: _weight_layout(n, o) for n, o in zip(names, outs)}

    def sibling_job(self, names, g):
        self.blocks = [_grad_blocks(n, g[n]) for n in names]
        return _SiblingJob(self.blocks)

    def chip_job(self, names, from_sibling):
        for n, b, r in zip(names, self.blocks, from_sibling):
            self.sums[n] = _pair_add(b, r, self.cj_idx, name=f"pair_add_{n}")
        return _ChipJob([self.sums[n][1] for n in names])

    def reduced(self, names, from_chips):
        self.recv.update(zip(names, from_chips))

    def small_job(self, g, loss):
        return _SmallGatherJob(_pack_small(g, conv_w_full=g["conv_w"], loss=loss))


def kernel(x, mem, ffn1_w13, ffn1_w2, ln1_g, ln1_b, w_in, conv_w, conv_b, w_rgate, b_rgate, w_igate, b_igate, lru_lambda, g_lru, g_sb, w_out, ln2_g, ln2_b, mem_wq, mem_wkv, mem_wo, ln3_g, ln3_b, ffn2_w13, ffn2_w2, ln4_g, ln4_b, loss_target, m_ffn1_w13, m_ffn1_w2, m_ln1_g, m_ln1_b, m_w_in, m_conv_w, m_conv_b, m_w_rgate, m_b_rgate, m_w_igate, m_b_igate, m_lru_lambda, m_g_lru, m_g_sb, m_w_out, m_ln2_g, m_ln2_b, m_mem_wq, m_mem_wkv, m_mem_wo, m_ln3_g, m_ln3_b, m_ffn2_w13, m_ffn2_w2, m_ln4_g, m_ln4_b, v_ffn1_w13, v_ffn1_w2, v_ln1_g, v_ln1_b, v_w_in, v_conv_w, v_conv_b, v_w_rgate, v_b_rgate, v_w_igate, v_b_igate, v_lru_lambda, v_g_lru, v_g_sb, v_w_out, v_ln2_g, v_ln2_b, v_mem_wq, v_mem_wkv, v_mem_wo, v_ln3_g, v_ln3_b, v_ffn2_w13, v_ffn2_w2, v_ln4_g, v_ln4_b):
    args = locals()
    w = {n: args[n] for n in _WEIGHTS}
    mom = {n: args["m_" + n] for n in _WEIGHTS}
    var = {n: args["v_" + n] for n in _WEIGHTS}
    ix, iy, ic = lax.axis_index("x"), lax.axis_index("y"), lax.axis_index("c")

    j_idx = jnp.reshape(2 * ix + iy, (1,)).astype(jnp.int32)
    hooks = _Hooks(w, jnp.stack([ic, 2 * ix + iy]).astype(jnp.int32))
    p = {n: (w[n][0] if w[n].ndim == 4 else w[n]) for n in _WEIGHTS if n not in _SHARDED and n != 'conv_w'}
    p.update(hooks.gathered(_GATHER_FIRST, _run_job(hooks.gather_job(_GATHER_FIRST), name="gather_ffn1")))
    _, grad_x, g = _local_step(x[0], mem[0], loss_target[0], p, hooks)

    grads, delta, new_m, new_v = {}, {}, {}, {}
    for n in _SHARDED:
        grads[n], delta[n], new_m[n], new_v[n] = (
            _own_layout(n, o).reshape(w[n].shape) for o in _reduce_adam(
                hooks.sums[n][0], hooks.recv[n], j_idx, _own_layout(n, w[n][0]), _own_layout(n, mom[n][0]),
                _own_layout(n, var[n][0]), name=f"adam_{n}"))

    me = 4 * ix + 2 * iy + ic
    own_cols = lambda full: lax.dynamic_slice_in_dim(full, me * (LRU_W // N_DEV), LRU_W // N_DEV, axis=1)
    land = hooks.small_land
    conv_off = sum(math.prod(w[n].shape) for n in _SMALL)
    land_conv = land.reshape(N_DEV, -1)[:, conv_off:conv_off + CONV_W * LRU_W].reshape(N_DEV, CONV_W, LRU_W)
    land_own = jnp.pad(own_cols(land_conv.reshape(N_DEV * CONV_W, LRU_W)).reshape(N_DEV, -1),
                       ((0, 0), (0, CONV_W * LRU_W - CONV_W * LRU_W // N_DEV)))
    land = lax.dynamic_update_slice(land.reshape(N_DEV, -1), land_own, (0, conv_off)).reshape(land.shape)
    pk = lambda d: _pack_small({n: d[n] for n in _SMALL}, conv_w_full=jnp.pad(d["conv_w"].reshape(-1), (0, CONV_W * LRU_W - CONV_W * LRU_W // N_DEV)))
    g_s, d_s, m_s, v_s = _adam_small(pk(w), land, pk(mom), pk(var), name="adam_small")
    for src, dst in ((g_s, grads), (d_s, delta), (m_s, new_m), (v_s, new_v)):
        vals, conv = _unpack_small(src, w, with_conv=True)
        dst.update(vals)
        dst["conv_w"] = conv.reshape(-1)[:CONV_W * LRU_W // N_DEV].reshape(w["conv_w"].shape)
    loss = g_s.reshape(-1)[conv_off + CONV_W * LRU_W]

    return (loss, grad_x[None], *[grads[n] for n in _WEIGHTS], *[delta[n] for n in _WEIGHTS],
            *[new_m[n] for n in _WEIGHTS], *[new_v[n] for n in _WEIGHTS])
```

```python
import functools
import math

import jax
import jax.numpy as jnp
from jax import lax
from jax.experimental import pallas as pl
from jax.experimental.pallas import tpu as pltpu

F32, BF16 = jnp.float32, jnp.bfloat16

D_MODEL = 1024
LRU_W = 512
SB_W = 512
SB_PAIR = 128
SB_HEAD = 64
SB_ROWS = 128
SB_SKIP = -110.0
D_FF = 2688
FF_CHUNK = 672
N_DEV = 8
MEM_HEADS = 4
MEM_HD = 256
CONV_W = 4
ALPHA = 2.0 ** 0.25
LN_EPS = 1e-5
RMS_EPS = 1e-6
LRU_C = 8.0
ADAM_LR, ADAM_B1, ADAM_B2, ADAM_EPS, ADAM_WD, ADAM_STEP = 0.001, 0.9, 0.999, 1e-08, 0.01, 10

NN = (((1,), (0,)), ((), ()))
NT = (((1,), (1,)), ((), ()))
MESH = pl.DeviceIdType.MESH
VMEM_LIMIT_MB = 56


def _dot(a, b, dn=NN):
    return lax.dot_general(a, b, dn, preferred_element_type=F32)


def _cparams(n_axes, vmem_mb=None):
    kw = dict(dimension_semantics=("arbitrary",) * n_axes)
    if vmem_mb is not None:
        kw["vmem_limit_bytes"] = vmem_mb << 20
    return pltpu.CompilerParams(**kw)


def _row_tile(rows, want):
    if rows <= want:
        return rows
    t = want - want % 8
    while rows % t:
        t -= 8
    return t


def _matmul(a, b, *, name, out_dtype=F32, trans_a=False, trans_b=False, tm=512, tn=1024, tk=1024,
            scale=1.0, res=None, res_coeff=1.0, job=None, ln=None, ln_bwd=None):
    a_chunked, b_chunked = a.ndim == 3, b.ndim == 3
    nc = a.shape[0] if a_chunked else (b.shape[0] if b_chunked else 1)
    a2, b2 = a.shape[-2:], b.shape[-2:]
    (kdim, m) = a2 if trans_a else a2[::-1]
    n = b2[0] if trans_b else b2[1]
    tm, tn, tk = min(tm, m), min(tn, n), min(tk, kdim)
    assert m % tm == 0 and n % tn == 0 and kdim % tk == 0, (name, m, n, kdim)
    nk = kdim // tk

    def a_idx(c, i, j, k):
        idx = (k, i) if trans_a else (i, k)
        return (c,) + idx if a_chunked else idx

    def b_idx(c, i, j, k):
        idx = (j, k) if trans_b else (k, j)
        return (c,) + idx if b_chunked else idx

    a_blk = (tk, tm) if trans_a else (tm, tk)
    b_blk = (tn, tk) if trans_b else (tk, tn)
    in_specs = [pl.BlockSpec(((None,) + a_blk) if a_chunked else a_blk, a_idx),
                pl.BlockSpec(((None,) + b_blk) if b_chunked else b_blk, b_idx)]
    args = [a, b]
    if res is not None:
        in_specs.append(pl.BlockSpec((None, tm, tn), lambda c, i, j, k: (c, i, j)))
        args.append(res.reshape((nc, m, n)))
    if ln is not None:
        assert nc == 1 and tn == n and res is None
        in_specs += [pl.BlockSpec((tm, tn), lambda c, i, j, k: (i, 0)), pl.BlockSpec((1, tn), lambda c, i, j, k: (0, 0)),
                     pl.BlockSpec((1, tn), lambda c, i, j, k: (0, 0))]
        args += [ln[0], ln[2], ln[3]]
    if ln_bwd is not None:
        assert nc == 1 and tn == n and res is not None and ln is None
        in_specs += [pl.BlockSpec((tm, tn), lambda c, i, j, k: (i, 0))] * 2 + [pl.BlockSpec((1, tn), lambda c, i, j, k: (0, 0))] * 2
        args += [ln_bwd[0], ln_bwd[1], ln_bwd[3], ln_bwd[4]]

    def body(*refs):
        if ln_bwd is not None:
            a_ref, b_ref, r_ref, x_ref, f_ref, g_ref, bb_ref, o_ref, dzb_ref, dg_ref, db_ref, acc_ref = refs
        elif res is not None:
            a_ref, b_ref, r_ref, o_ref, acc_ref = refs
        elif ln is not None:
            a_ref, b_ref, x_ref, g_ref, bb_ref, o_ref, h_ref, hb_ref, acc_ref = refs
        else:
            a_ref, b_ref, o_ref, acc_ref = refs
        k = pl.program_id(3)

        @pl.when(k == 0)
        def _():
            acc_ref[...] = jnp.zeros_like(acc_ref)

        av = a_ref[...]
        if trans_a:
            av = av.astype(F32).T
        acc_ref[...] += _dot(av.astype(BF16), b_ref[...].astype(BF16), NT if trans_b else NN)

        @pl.when(k == nk - 1)
        def _():
            out = acc_ref[...]
            if scale != 1.0:
                out = out * scale
            if res is not None:
                out = out + res_coeff * r_ref[...]
            if ln_bwd is not None:
                dz, dg, db = _ln_bwd_tile(x_ref[...], f_ref[...], ln_bwd[2], g_ref[...], bb_ref[...], out)
                _ln_bwd_store(pl.program_id(1) == 0, dz, dg, db, o_ref, dzb_ref, dg_ref, db_ref)
                return
            o_ref[...] = out.astype(out_dtype)
            if ln is not None:
                h = _ln_math(ALPHA * x_ref[...] + ln[1] * out, g_ref[...], bb_ref[...])
                h_ref[...] = h
                hb_ref[...] = h.astype(BF16)

    grid = (nc, m // tm, n // tn, nk)

    def step():
        idx = pl.program_id(0)
        for ax in range(1, 4):
            idx = idx * grid[ax] + pl.program_id(ax)
        return idx

    out_specs = [pl.BlockSpec((None, tm, tn), lambda c, i, j, k: (c, i, j))]
    out_shape = [jax.ShapeDtypeStruct((nc, m, n), out_dtype)]
    if ln is not None:
        out_specs += [pl.BlockSpec((tm, tn), lambda c, i, j, k: (i, 0))] * 2
        out_shape += [jax.ShapeDtypeStruct((m, n), F32), jax.ShapeDtypeStruct((m, n), BF16)]
    if ln_bwd is not None:
        out_specs += [pl.BlockSpec((tm, tn), lambda c, i, j, k: (i, 0))] + [pl.BlockSpec((1, tn), lambda c, i, j, k: (0, 0))] * 2
        out_shape += _ln_bwd_shapes(m, n)[1:]
    outs, job_out = _pallas_with_job(
        body, job, step, math.prod(grid), grid=grid, in_specs=in_specs, out_specs=out_specs, out_shape=out_shape,
        scratch_shapes=[pltpu.VMEM((tm, tn), F32)],
        compiler_params=_cparams(4, VMEM_LIMIT_MB), name=name, args=args)
    out = outs[0] if (a_chunked or b_chunked) else outs[0][0]
    if ln is not None:
        return out, outs[1], outs[2]
    if ln_bwd is not None:
        return out, outs[1], outs[2], outs[3]
    return out if job is None else (out, job_out)


def _ln_math(z, g, b):
    mu = jnp.mean(z, axis=-1, keepdims=True)
    zc = z - mu
    var = jnp.mean(zc * zc, axis=-1, keepdims=True)
    return zc * lax.rsqrt(var + LN_EPS) * g + b


def _row_spec(tm, d):
    return pl.BlockSpec((tm, d), lambda i: (i, 0))


def _par_spec(d, rows=1):
    return pl.BlockSpec((rows, d), lambda i: (0, 0))


def _ln_bwd_tile(x, f, fscale, g, b, dy):
    _, vjp = jax.vjp(_ln_math, ALPHA * x + fscale * f, g, b)
    return vjp(dy)


def _ln_bwd_specs(tm, d):
    return ([_row_spec(tm, d), _row_spec(tm, d), _par_spec(d), _par_spec(d)],
            [_row_spec(tm, d), _row_spec(tm, d), _par_spec(d), _par_spec(d)])


def _ln_bwd_shapes(s, d):
    return [jax.ShapeDtypeStruct((s, d), F32), jax.ShapeDtypeStruct((s, d), BF16),
            jax.ShapeDtypeStruct((1, d), F32), jax.ShapeDtypeStruct((1, d), F32)]


def _ln_bwd_store(first, dz, dg, db, dz_ref, dzb_ref, dg_ref, db_ref):
    @pl.when(first)
    def _():
        dg_ref[...] = jnp.zeros_like(dg_ref)
        db_ref[...] = jnp.zeros_like(db_ref)

    dz_ref[...] = dz
    dzb_ref[...] = dz.astype(BF16)
    dg_ref[...] += dg
    db_ref[...] += db


def _load_weights_once(pairs, sem):
    @pl.when(pl.program_id(0) == 0)
    def _():
        copies = [pltpu.make_async_copy(src, dst, sem.at[n]) for n, (src, dst) in enumerate(pairs)]
        for c in copies:
            c.start()
        for c in copies:
            c.wait()


def _ffn_fwd(h, w13c, w2, *, name, job=None, ln=None, loss_head=None):
    s, d = h.shape
    tm = min(256, s)
    half = N_DEV // 2

    def body(*refs):
        if loss_head is not None:
            (h_ref, w13_hbm, w2_hbm, x_ref, t_ref, g_ref, b_ref, gu_ref, dz_ref, dzb_ref, dg_ref, db_ref, loss_ref,
             w13_v, w2_v, sem) = refs
        elif ln is not None:
            h_ref, w13_hbm, w2_hbm, g_ref, b_ref, gu_ref, f_ref, hb_ref, n_ref, nb_ref, w13_v, w2_v, sem = refs
        else:
            h_ref, w13_hbm, w2_hbm, gu_ref, f_ref, hb_ref, w13_v, w2_v, sem = refs
        _load_weights_once([(w13_hbm, w13_v), (w2_hbm, w2_v)], sem)
        hb = h_ref[...].astype(BF16)
        if loss_head is None:
            hb_ref[...] = hb
        acc = jnp.zeros((tm, d), F32)
        for k in range(half):
            g = _dot(hb, w13_v[k], NT)
            u = _dot(hb, w13_v[k + half], NT)
            gu_ref[k] = g.astype(BF16)
            gu_ref[k + half] = u.astype(BF16)
            a = g * jax.nn.sigmoid(g) * u
            acc = acc + _dot(a.astype(BF16), w2_v[pl.ds(k * FF_CHUNK, FF_CHUNK), :])
        if loss_head is not None:
            first = pl.program_id(0) == 0

            @pl.when(first)
            def _():
                loss_ref[...] = jnp.zeros_like(loss_ref)

            y, vjp = jax.vjp(_ln_math, ALPHA * x_ref[...] + 0.5 * acc, g_ref[...], b_ref[...])
            err = y - t_ref[...]
            sq = jnp.sum(err * err, axis=1, keepdims=True)
            loss_ref[...] += jnp.sum(sq, axis=0, keepdims=True) * (0.5 / d)
            dz, dg, db = vjp(err * (1.0 / d))
            _ln_bwd_store(first, dz, dg, db, dz_ref, dzb_ref, dg_ref, db_ref)
            return
        f_ref[...] = acc
        if ln is not None:
            hn = _ln_math(ALPHA * h_ref[...] + 0.5 * acc, g_ref[...], b_ref[...])
            n_ref[...] = hn
            nb_ref[...] = hn.astype(BF16)

    any_spec = pl.BlockSpec(memory_space=pl.ANY)
    gu_spec = pl.BlockSpec((N_DEV, tm, FF_CHUNK), lambda i: (0, i, 0))
    gu_shape = jax.ShapeDtypeStruct((N_DEV, s, FF_CHUNK), BF16)
    scratch = [pltpu.VMEM(w13c.shape, BF16), pltpu.VMEM(w2.shape, BF16), pltpu.SemaphoreType.DMA((2,))]
    if loss_head is not None:
        return _pallas_with_job(
            body, job, lambda: pl.program_id(0), s // tm, grid=(s // tm,),
            in_specs=[_row_spec(tm, d), any_spec, any_spec, _row_spec(tm, d), _row_spec(tm, d), _par_spec(d), _par_spec(d)],
            out_specs=[gu_spec] + _ln_bwd_specs(tm, d)[1] + [pl.BlockSpec((1, 1), lambda i: (0, 0))],
            out_shape=[gu_shape] + _ln_bwd_shapes(s, d) + [jax.ShapeDtypeStruct((1, 1), F32)],
            scratch_shapes=scratch, compiler_params=_cparams(1, VMEM_LIMIT_MB), name=name,
            args=(h, w13c, w2) + tuple(loss_head))
    n_ln = 0 if ln is None else 2
    return _pallas_with_job(
        body, job, lambda: pl.program_id(0), s // tm,
        grid=(s // tm,), in_specs=[_row_spec(tm, d), any_spec, any_spec] + [_par_spec(d)] * n_ln,
        out_specs=[pl.BlockSpec((N_DEV, tm, FF_CHUNK), lambda i: (0, i, 0)), _row_spec(tm, d), _row_spec(tm, d)]
                  + [_row_spec(tm, d)] * n_ln,
        out_shape=[jax.ShapeDtypeStruct((N_DEV, s, FF_CHUNK), BF16), jax.ShapeDtypeStruct((s, d), F32),
                   jax.ShapeDtypeStruct((s, d), BF16)]
                  + [jax.ShapeDtypeStruct((s, d), F32), jax.ShapeDtypeStruct((s, d), BF16)][:n_ln],
        scratch_shapes=[pltpu.VMEM(w13c.shape, BF16), pltpu.VMEM(w2.shape, BF16), pltpu.SemaphoreType.DMA((2,))],
        compiler_params=_cparams(1, VMEM_LIMIT_MB), name=name, args=(h, w13c, w2) + (() if ln is None else tuple(ln)))


def _ffn_bwd(dz, gu, w13c, w2, *, name, job=None, ln_bwd=None):
    s, d = dz.shape
    tm = min(256, s)
    half = N_DEV // 2

    def body(*refs):
        if ln_bwd is not None:
            (dz_ref, gu_ref, w13_hbm, w2_hbm, x_ref, f_ref, g_ref, b_ref, dzp_ref, dzpb_ref, dgu_ref, act_ref,
             dg_ref, db_ref, w13_v, w2_v, sem) = refs
        else:
            dz_ref, gu_ref, w13_hbm, w2_hbm, dh_ref, dgu_ref, act_ref, w13_v, w2_v, sem = refs
        _load_weights_once([(w13_hbm, w13_v), (w2_hbm, w2_v)], sem)
        dzv = dz_ref[...]
        dfb = (0.5 * dzv).astype(BF16)
        acc = ALPHA * dzv
        for k in range(half):
            g = gu_ref[k].astype(F32)
            u = gu_ref[k + half].astype(F32)
            da = _dot(dfb, w2_v[pl.ds(k * FF_CHUNK, FF_CHUNK), :], NT)
            sg = jax.nn.sigmoid(g)
            silu = g * sg
            dg = (da * u * (sg * (1.0 + g * (1.0 - sg)))).astype(BF16)
            du = (da * silu).astype(BF16)
            act_ref[k] = (silu * u).astype(BF16)
            dgu_ref[k] = dg
            dgu_ref[k + half] = du
            acc = acc + _dot(dg, w13_v[k]) + _dot(du, w13_v[k + half])
        if ln_bwd is not None:
            dzp, dg, db = _ln_bwd_tile(x_ref[...], f_ref[...], ln_bwd[2], g_ref[...], b_ref[...], acc)
            _ln_bwd_store(pl.program_id(0) == 0, dzp, dg, db, dzp_ref, dzpb_ref, dg_ref, db_ref)
        else:
            dh_ref[...] = acc

    any_spec = pl.BlockSpec(memory_space=pl.ANY)
    in_specs = [_row_spec(tm, d), pl.BlockSpec((N_DEV, tm, FF_CHUNK), lambda i: (0, i, 0)), any_spec, any_spec]
    mid_specs = [pl.BlockSpec((N_DEV, tm, FF_CHUNK), lambda i: (0, i, 0)), pl.BlockSpec((half, tm, FF_CHUNK), lambda i: (0, i, 0))]
    mid_shapes = [jax.ShapeDtypeStruct((N_DEV, s, FF_CHUNK), BF16), jax.ShapeDtypeStruct((half, s, FF_CHUNK), BF16)]
    args = (dz, gu, w13c, w2)
    if ln_bwd is not None:
        ln_in, ln_out = _ln_bwd_specs(tm, d)
        in_specs, out_specs = in_specs + ln_in, ln_out[:2] + mid_specs + ln_out[2:]
        shapes = _ln_bwd_shapes(s, d)
        out_shape = shapes[:2] + mid_shapes + shapes[2:]
        args += (ln_bwd[0], ln_bwd[1], ln_bwd[3], ln_bwd[4])
    else:
        out_specs, out_shape = [_row_spec(tm, d)] + mid_specs, [jax.ShapeDtypeStruct((s, d), F32)] + mid_shapes
    return _pallas_with_job(
        body, job, lambda: pl.program_id(0), s // tm, grid=(s // tm,), in_specs=in_specs, out_specs=out_specs,
        out_shape=out_shape,
        scratch_shapes=[pltpu.VMEM(w13c.shape, BF16), pltpu.VMEM(w2.shape, BF16), pltpu.SemaphoreType.DMA((2,))],
        compiler_params=_cparams(1, VMEM_LIMIT_MB), name=name, args=args)


def _shift_down(x, prev8, d):
    t, c = x.shape
    row = lax.broadcasted_iota(jnp.int32, (t, c), 0)
    xr = pltpu.roll(x, d, 0)
    pr = pltpu.roll(prev8, d, 0)
    if t > 8:
        pr = jnp.concatenate([pr, jnp.zeros((t - 8, c), x.dtype)], axis=0)
    return jnp.where(row < d, pr, xr)


def _shift_up(x, next8, d):
    t, c = x.shape
    row = lax.broadcasted_iota(jnp.int32, (t, c), 0)
    xr = pltpu.roll(x, t - d, 0)
    nr = pltpu.roll(next8, 8 - d, 0)
    if t > 8:
        nr = jnp.concatenate([jnp.zeros((t - 8, c), x.dtype), nr], axis=0)
    return jnp.where(row >= t - d, nr, xr)


def _scan_fwd(a, u, carry):
    t = a.shape[0]
    row = lax.broadcasted_iota(jnp.int32, a.shape, 0) % 8
    for d in (1, 2, 4):
        a_s, u_s = pltpu.roll(a, d, 0), pltpu.roll(u, d, 0)
        m = row >= d
        u = jnp.where(m, u + a * u_s, u)
        a = jnp.where(m, a * a_s, a)
    out = []
    for g in range(t // 8):
        hg = u[8 * g:8 * g + 8, :] + a[8 * g:8 * g + 8, :] * carry
        out.append(hg)
        carry = hg[7:8, :]
    return jnp.concatenate(out, axis=0)


def _scan_bwd(b, x, carry):
    t = b.shape[0]
    row = lax.broadcasted_iota(jnp.int32, b.shape, 0) % 8
    for d in (1, 2, 4):
        b_s, x_s = pltpu.roll(b, t - d, 0), pltpu.roll(x, t - d, 0)
        m = row < 8 - d
        x = jnp.where(m, x + b * x_s, x)
        b = jnp.where(m, b * b_s, b)
    out = [None] * (t // 8)
    for g in reversed(range(t // 8)):
        lg = x[8 * g:8 * g + 8, :] + b[8 * g:8 * g + 8, :] * carry
        out[g] = lg
        carry = lg[0:1, :]
    return jnp.concatenate(out, axis=0)


def _lru_elem(xc, pr, pi, b_r, b_i, lam):
    r = jax.nn.sigmoid(pr + b_r)
    ig = jax.nn.sigmoid(pi + b_i)
    softplus_neg_lam = jnp.maximum(-lam, 0.0) + jnp.log1p(jnp.exp(-jnp.abs(lam)))
    log_a = (-LRU_C) * r * softplus_neg_lam
    a = jnp.exp(log_a)
    one_minus_a2 = -jnp.tanh(log_a) * (jnp.exp(2.0 * log_a) + 1.0)
    u = jnp.sqrt(one_minus_a2) * (ig * xc)
    return a, u


def _gelu_tanh(x):
    return 0.5 * x * (1.0 + jnp.tanh(math.sqrt(2.0 / math.pi) * (x + 0.044715 * (x * x * x))))


def _conv_fwd(xb, prev8, w, b):
    out = b + w[3:4, :] * xb
    for d in range(1, CONV_W):
        out = out + w[3 - d:4 - d, :] * _shift_down(xb, prev8, d)
    return out


def _lru_fwd(xbg, conv_w, conv_b, w_rg, b_rg, w_ig, b_ig, lam, *, name):
    s = xbg.shape[0]
    c = LRU_W
    t = min(512, s)
    t8 = t // 8

    def body(xb_ref, gate_ref, prev_ref, cw_ref, cb_ref, wr_ref, br_ref, wi_ref, bi_ref, lam_ref,
             xc_ref, h_ref, y_ref, carry_ref):
        i = pl.program_id(0)

        @pl.when(i == 0)
        def _():
            carry_ref[...] = jnp.zeros_like(carry_ref)

        prev8 = jnp.where(i > 0, prev_ref[...], 0.0)
        xc = _conv_fwd(xb_ref[...], prev8, cw_ref[...], cb_ref[...])
        xcb = xc.astype(BF16)
        a, u = _lru_elem(xc, _dot(xcb, wr_ref[...]), _dot(xcb, wi_ref[...]), br_ref[...], bi_ref[...], lam_ref[...])
        h = _scan_fwd(a, u, carry_ref[0:1, :])
        carry_ref[...] = jnp.broadcast_to(h[t - 1:t, :], carry_ref.shape)
        xc_ref[...] = xc
        h_ref[...] = h
        y_ref[...] = h * _gelu_tanh(gate_ref[...])

    tile = lambda col: pl.BlockSpec((t, c), lambda i: (i, col))
    par = lambda rows: pl.BlockSpec((rows, c), lambda i: (0, 0))
    out_spec = pl.BlockSpec((t, c), lambda i: (i, 0))
    return pl.pallas_call(
        body, grid=(s // t,),
        in_specs=[tile(0), tile(1), pl.BlockSpec((8, c), lambda i: (jnp.maximum(i * t8 - 1, 0), 0)),
                  par(CONV_W), par(1), pl.BlockSpec((c, c), lambda i: (0, 0)), par(1),
                  pl.BlockSpec((c, c), lambda i: (0, 0)), par(1), par(1)],
        out_specs=[out_spec, out_spec, out_spec],
        out_shape=[jax.ShapeDtypeStruct((s, c), F32)] * 3,
        scratch_shapes=[pltpu.VMEM((8, c), F32)],
        compiler_params=_cparams(1, VMEM_LIMIT_MB), name=name)(
            xbg, xbg, xbg, conv_w, conv_b, w_rg, b_rg, w_ig, b_ig, lam)


def _lru_bwd(xc, xbg, hseq, dy, w_rg, b_rg, w_ig, b_ig, lam, *, name):
    s, c = xc.shape
    t = min(512, s)
    t8 = t // 8
    nt = s // t

    def body(xc_ref, gate_ref, h_ref, hprev_ref, dy_ref, wr_ref, br_ref, wi_ref, bi_ref, lam_ref,
             dxc_ref, dgate_ref, dwr_ref, dwi_ref, dbr_ref, dbi_ref, dlam_ref, a8_ref, l8_ref):
        i = pl.program_id(0)

        @pl.when(i == 0)
        def _():
            for r in (dwr_ref, dwi_ref, dbr_ref, dbi_ref, dlam_ref, a8_ref, l8_ref):
                r[...] = jnp.zeros_like(r)

        xc_v = xc_ref[...]
        xcb = xc_v.astype(BF16)
        (a, _), vjp = jax.vjp(_lru_elem, xc_v, _dot(xcb, wr_ref[...]), _dot(xcb, wi_ref[...]),
                              br_ref[...], bi_ref[...], lam_ref[...])
        gl, vjp_gelu = jax.vjp(_gelu_tanh, gate_ref[...])
        dyv, hs = dy_ref[...], h_ref[...]
        (dgate,) = vjp_gelu(dyv * hs)
        b_next = _shift_up(a, a8_ref[...], 1)
        lt = _scan_bwd(b_next, dyv * gl, l8_ref[0:1, :])
        a8_ref[...] = a[0:8, :]
        l8_ref[...] = lt[0:8, :]
        hprev8 = jnp.where(i < nt - 1, hprev_ref[...], 0.0)
        da = lt * _shift_down(hs, hprev8, 1)
        dxc_e, dpr, dpi, dbr, dbi, dlam = vjp((da, lt))
        dprb, dpib = dpr.astype(BF16), dpi.astype(BF16)
        dxc_ref[...] = dxc_e + _dot(dprb, wr_ref[...], NT) + _dot(dpib, wi_ref[...], NT)
        dgate_ref[...] = dgate.astype(BF16)
        xct = xc_v.T.astype(BF16)
        dwr_ref[...] += _dot(xct, dprb)
        dwi_ref[...] += _dot(xct, dpib)
        dbr_ref[...] += dbr
        dbi_ref[...] += dbi
        dlam_ref[...] += dlam

    rev = lambda col: pl.BlockSpec((t, c), lambda i: (nt - 1 - i, col))
    par = pl.BlockSpec((1, c), lambda i: (0, 0))
    sq = pl.BlockSpec((c, c), lambda i: (0, 0))
    return pl.pallas_call(
        body, grid=(nt,),
        in_specs=[rev(0), rev(1), rev(0), pl.BlockSpec((8, c), lambda i: (jnp.maximum((nt - 1 - i) * t8 - 1, 0), 0)),
                  rev(0), sq, par, sq, par, par],
        out_specs=[rev(0), rev(0), sq, sq, par, par, par],
        out_shape=[jax.ShapeDtypeStruct((s, c), F32), jax.ShapeDtypeStruct((s, c), BF16)]
                  + [jax.ShapeDtypeStruct((c, c), F32)] * 2 + [jax.ShapeDtypeStruct((1, c), F32)] * 3,
        scratch_shapes=[pltpu.VMEM((8, c), F32), pltpu.VMEM((8, c), F32)],
        compiler_params=_cparams(1, VMEM_LIMIT_MB), name=name)(
            xc, xbg, hseq, hseq, dy, w_rg, b_rg, w_ig, b_ig, lam)


def _conv_bwd(dxc, xbg, conv_w, *, name):
    s, c = dxc.shape
    t = min(512, s)
    t8 = t // 8
    nt = s // t

    def body(dxc_ref, next_ref, xb_ref, prev_ref, w_ref, dxb_ref, dw_ref, db_ref):
        i = pl.program_id(0)

        @pl.when(i == 0)
        def _():
            dw_ref[...] = jnp.zeros_like(dw_ref)
            db_ref[...] = jnp.zeros_like(db_ref)

        g = dxc_ref[...]
        xb = xb_ref[...]
        w = w_ref[...]
        next8 = jnp.where(i < nt - 1, next_ref[...], 0.0)
        prev8 = jnp.where(i > 0, prev_ref[...], 0.0)
        dxb = w[3:4, :] * g
        dw_ref[3:4, :] += jnp.sum(g * xb, axis=0, keepdims=True)
        for d in range(1, CONV_W):
            dxb = dxb + w[3 - d:4 - d, :] * _shift_up(g, next8, d)
            dw_ref[3 - d:4 - d, :] += jnp.sum(g * _shift_down(xb, prev8, d), axis=0, keepdims=True)
        dxb_ref[...] = dxb.astype(BF16)
        db_ref[...] += jnp.sum(g, axis=0, keepdims=True)

    tile = pl.BlockSpec((t, c), lambda i: (i, 0))
    return pl.pallas_call(
        body, grid=(nt,),
        in_specs=[tile, pl.BlockSpec((8, c), lambda i: (jnp.minimum((i + 1) * t8, s // 8 - 1), 0)),
                  tile, pl.BlockSpec((8, c), lambda i: (jnp.maximum(i * t8 - 1, 0), 0)),
                  pl.BlockSpec((CONV_W, c), lambda i: (0, 0))],
        out_specs=[tile, pl.BlockSpec((CONV_W, c), lambda i: (0, 0)), pl.BlockSpec((1, c), lambda i: (0, 0))],
        out_shape=[jax.ShapeDtypeStruct((s, c), BF16), jax.ShapeDtypeStruct((CONV_W, c), F32),
                   jax.ShapeDtypeStruct((1, c), F32)],
        compiler_params=_cparams(1), name=name)(dxc, dxc, xbg, xbg, conv_w)


def _hi_lo_dot(x, tri):
    hi = x.astype(BF16)
    lo = (x - hi.astype(F32)).astype(BF16)
    return _dot(hi, tri) + _dot(lo, tri)


def _tri(tk, inclusive):
    r = lax.broadcasted_iota(jnp.int32, (tk, tk), 0)
    c = lax.broadcasted_iota(jnp.int32, (tk, tk), 1)
    return ((r >= c) if inclusive else (r > c)).astype(BF16)


def _split_heads(x):
    first = lax.broadcasted_iota(jnp.int32, x.shape, 1) < SB_HEAD
    zero = jnp.zeros_like(x)
    return first, (jnp.where(first, x, zero), jnp.where(first, zero, x))


def _sb_softplus_terms(z):
    ls = jnp.minimum(z, 0.0) - jnp.log(1.0 + jnp.exp(-jnp.abs(z)))
    return ls, ls - z


def _sb_alive(runs):
    m = runs[0]
    for r in runs[1:]:
        m = jnp.maximum(m, r)
    return jnp.max(m) > SB_SKIP


def _sb_fwd(q, kv, *, name, job=None):
    s = q.shape[0]
    tb = min(256, s)
    rc = min(SB_ROWS, tb)
    npair = SB_W // SB_PAIR
    chains = [(h, c) for h in range(2) for c in range(tb // rc)]

    def body(q_ref, k_ref, v_ref, o_ref):
        i = pl.program_id(1)
        first, qs = _split_heads(q_ref[...])
        tri = _tri(tb, False)
        causal = lax.broadcasted_iota(jnp.int32, (tb, tb), 1) < lax.broadcasted_iota(jnp.int32, (tb, tb), 0)
        causal_c = [causal[c * rc:(c + 1) * rc, :] for c in range(tb // rc)]

        def group(j_lo, width, carry, diag_last):
            k0 = pl.multiple_of(j_lo * tb, tb)
            kb = k_ref[pl.ds(k0, width * tb), :]
            vb = v_ref[pl.ds(k0, width * tb), :]
            zw = [_dot(qs[h][c * rc:(c + 1) * rc, :], kb, NT) for h, c in chains]
            cols = [slice(t * tb, (t + 1) * tb) for t in range(width)]
            terms = [[_sb_softplus_terms(z[:, cl]) for cl in cols] for z in zw]
            lss = [[ls for ls, _ in tc] for tc in terms]
            ns = [[n for _, n in tc] for tc in terms]
            if diag_last:
                ns = [nc[:-1] + [jnp.where(causal_c[c], nc[-1], 0.0)] for nc, (h, c) in zip(ns, chains)]
            sufs = [[_hi_lo_dot(n, tri) for n in nc] for nc in ns]
            out = []
            for (h, c), lc, nc, sc, (run, acc) in zip(chains, lss, ns, sufs, carry):
                ws = [None] * width
                for t in reversed(range(width)):
                    w = jnp.exp(lc[t] + (sc[t] + run))
                    if diag_last and t == width - 1:
                        w = jnp.where(causal_c[c], w, 0.0)
                    ws[t] = w.astype(BF16)
                    run = run + jnp.sum(nc[t], axis=1, keepdims=True)
                out.append((run, acc + _dot(ws[0] if width == 1 else jnp.concatenate(ws, axis=1), vb)))
            return tuple(out)

        zero = (jnp.zeros((rc, 1), F32), jnp.zeros((rc, SB_PAIR), F32))
        carry = lax.cond(i == 0, lambda c: group(0, 1, c, True), lambda c: group(i - 1, 2, c, True),
                         (zero,) * len(chains))

        def step(st):
            j, _, c = st
            c = group(j, 1, c, False)
            return j - 1, _sb_alive([run for run, _ in c]), c

        _, _, carry = lax.while_loop(lambda st: jnp.logical_and(st[0] >= 0, st[1]), step,
                                     (i - 2, _sb_alive([run for run, _ in carry]), carry))
        accs = [jnp.concatenate([carry[n][1] for n, (h, c) in enumerate(chains) if h == hh], axis=0) for hh in range(2)]
        o_ref[...] = jnp.where(first, accs[0], accs[1])

    (o,), job_out = _pallas_with_job(
        body, job, lambda: pl.program_id(0) * (s // tb) + pl.program_id(1), npair * (s // tb),
        grid=(npair, s // tb),
        in_specs=[pl.BlockSpec((tb, SB_PAIR), lambda p, i: (i, p)),
                  pl.BlockSpec((s, SB_PAIR), lambda p, i: (0, p)),
                  pl.BlockSpec((s, SB_PAIR), lambda p, i: (0, npair + p))],
        out_specs=[pl.BlockSpec((tb, SB_PAIR), lambda p, i: (i, p))],
        out_shape=[jax.ShapeDtypeStruct((s, SB_W), F32)],
        compiler_params=_cparams(2, VMEM_LIMIT_MB), name=name, args=(q, kv, kv))
    return o, job_out


def _sb_bwd(q, kv, o, do, *, name, job=None):
    s = q.shape[0]
    tb = min(256, s)
    rc = min(SB_ROWS, tb)
    npair = SB_W // SB_PAIR
    chains = [(h, c) for h in range(2) for c in range(tb // rc)]

    def body(q_ref, k_ref, v_ref, o_ref, do_ref, dq_ref, dkb_ref, dvb_ref, dk_ref, dv_ref):
        i = pl.program_id(1)

        @pl.when(i == 0)
        def _():
            dk_ref[...] = jnp.zeros_like(dk_ref)
            dv_ref[...] = jnp.zeros_like(dv_ref)

        first, qs = _split_heads(q_ref[...])
        dob = do_ref[...].astype(BF16)
        _, dos = _split_heads(dob)
        prod = dob.astype(F32) * o_ref[...]
        deltas = (jnp.sum(jnp.where(first, prod, 0.0), axis=1, keepdims=True),
                  jnp.sum(jnp.where(first, 0.0, prod), axis=1, keepdims=True))
        tri_x, tri_i = _tri(tb, False), _tri(tb, True)
        causal = lax.broadcasted_iota(jnp.int32, (tb, tb), 1) < lax.broadcasted_iota(jnp.int32, (tb, tb), 0)
        causal_c = [causal[c * rc:(c + 1) * rc, :] for c in range(tb // rc)]
        rows = [slice(c * rc, (c + 1) * rc) for c in range(tb // rc)]

        def group(j_lo, width, carry, diag_last):
            k0 = pl.multiple_of(j_lo * tb, tb)
            kb = k_ref[pl.ds(k0, width * tb), :]
            vb = v_ref[pl.ds(k0, width * tb), :]
            cols = [slice(t * tb, (t + 1) * tb) for t in range(width)]
            zw = [_dot(qs[h][rows[c], :], kb, NT) for h, c in chains]
            dww = [_dot(dos[h][rows[c], :], vb, NT) for h, c in chains]
            terms = [[_sb_softplus_terms(z[:, cl]) for cl in cols] for z in zw]
            lss = [[ls for ls, _ in tc] for tc in terms]
            ns = [[n for _, n in tc] for tc in terms]
            if diag_last:
                ns = [nc[:-1] + [jnp.where(causal_c[c], nc[-1], 0.0)] for nc, (h, c) in zip(ns, chains)]
            sufs = [[_hi_lo_dot(n, tri_x) for n in nc] for nc in ns]
            wbs, gs, runs_n = [], [], []
            for (h, c), lc, nc, sc, dw, (run_n, _, _) in zip(chains, lss, ns, sufs, dww, carry):
                wb, g = [None] * width, [None] * width
                for t in reversed(range(width)):
                    w = jnp.exp(lc[t] + (sc[t] + run_n))
                    if diag_last and t == width - 1:
                        w = jnp.where(causal_c[c], w, 0.0)
                    wb[t] = w.astype(BF16)
                    g[t] = wb[t].astype(F32) * dw[:, cols[t]]
                    run_n = run_n + jnp.sum(nc[t], axis=1, keepdims=True)
                wbs.append(wb)
                gs.append(g)
                runs_n.append(run_n)
            gsufs = [[_hi_lo_dot(g, tri_i) for g in gc] for gc in gs]
            out = []
            dk_t = [jnp.zeros((tb, SB_PAIR), F32) for _ in range(width)]
            dv_t = [jnp.zeros((tb, SB_PAIR), F32) for _ in range(width)]
            for (h, c), lc, gc, gsc, wb, run_n, (_, run_g, dq_acc) in zip(chains, lss, gs, gsufs, wbs, runs_n, carry):
                qh, doh, delta = qs[h][rows[c], :], dos[h][rows[c], :], deltas[h][rows[c], :]
                dzb = [None] * width
                for t in reversed(range(width)):
                    pre = delta - (gsc[t] + run_g)
                    dz = gc[t] - jnp.exp(lc[t]) * (gc[t] + pre)
                    if diag_last and t == width - 1:
                        dz = jnp.where(causal_c[c], dz, 0.0)
                    run_g = run_g + jnp.sum(gc[t], axis=1, keepdims=True)
                    dzb[t] = dz.astype(BF16)
                    dk_t[t] = dk_t[t] + _dot(dz.T.astype(BF16), qh)
                    dv_t[t] = dv_t[t] + _dot(wb[t].astype(F32).T.astype(BF16), doh)
                out.append((run_n, run_g, dq_acc + _dot(dzb[0] if width == 1 else jnp.concatenate(dzb, axis=1), kb)))
            for t in range(width):
                dk_ref[pl.ds(pl.multiple_of((j_lo + t) * tb, tb), tb), :] += dk_t[t]
                dv_ref[pl.ds(pl.multiple_of((j_lo + t) * tb, tb), tb), :] += dv_t[t]
            return tuple(out)

        zero = (jnp.zeros((rc, 1), F32), jnp.zeros((rc, 1), F32), jnp.zeros((rc, SB_PAIR), F32))
        carry = lax.cond(i == 0, lambda c: group(0, 1, c, True), lambda c: group(i - 1, 2, c, True),
                         (zero,) * len(chains))

        def step(st):
            j, _, c = st
            c = group(j, 1, c, False)
            return j - 1, _sb_alive([r[0] for r in c]), c

        _, _, carry = lax.while_loop(lambda st: jnp.logical_and(st[0] >= 0, st[1]), step,
                                     (i - 2, _sb_alive([r[0] for r in carry]), carry))
        dqs = [jnp.concatenate([carry[n][2] for n, (h, c) in enumerate(chains) if h == hh], axis=0) for hh in range(2)]
        dq_ref[...] = (jnp.where(first, dqs[0], dqs[1]) * (1.0 / math.sqrt(SB_HEAD))).astype(BF16)

        @pl.when(i == s // tb - 1)
        def _():
            dkb_ref[...] = dk_ref[...].astype(BF16)
            dvb_ref[...] = dv_ref[...].astype(BF16)

    qtile = pl.BlockSpec((tb, SB_PAIR), lambda p, i: (i, p))
    col = pl.BlockSpec((s, SB_PAIR), lambda p, i: (0, p))
    return _pallas_with_job(
        body, job, lambda: pl.program_id(0) * (s // tb) + pl.program_id(1), npair * (s // tb),
        grid=(npair, s // tb),
        in_specs=[qtile, col, pl.BlockSpec((s, SB_PAIR), lambda p, i: (0, npair + p)), qtile, qtile],
        out_specs=[qtile, col, col],
        out_shape=[jax.ShapeDtypeStruct((s, SB_W), BF16)] * 3,
        scratch_shapes=[pltpu.VMEM((s, SB_PAIR), F32)] * 2,
        compiler_params=_cparams(2, VMEM_LIMIT_MB), name=name, args=(q, kv, kv, o, do))


def _mixnorm_math(yl, ys, gl, gs):
    def rms(x, g):
        return x * lax.rsqrt(jnp.mean(x * x, axis=-1, keepdims=True) + RMS_EPS) * g
    return rms(yl, gl), rms(ys, gs)


def _mixnorm_fwd(yl, ys, gl, gs, *, name):
    s, c = yl.shape
    tm = min(512, s)

    def body(yl_ref, ys_ref, gl_ref, gs_ref, y_ref):
        a, b = _mixnorm_math(yl_ref[...], ys_ref[...], gl_ref[...], gs_ref[...])
        y_ref[:, 0:c] = a.astype(BF16)
        y_ref[:, c:2 * c] = b.astype(BF16)

    return pl.pallas_call(
        body, grid=(s // tm,), in_specs=[_row_spec(tm, c), _row_spec(tm, c), _par_spec(c), _par_spec(c)],
        out_specs=_row_spec(tm, 2 * c), out_shape=jax.ShapeDtypeStruct((s, 2 * c), BF16),
        compiler_params=_cparams(1), name=name)(yl, ys, gl, gs)


def _cross_probs(qh, kh):
    sc = _dot(qh, kh, NT) * (1.0 / math.sqrt(MEM_HD))
    e = jnp.exp(sc - jnp.max(sc, axis=-1, keepdims=True))
    return e / jnp.sum(e, axis=-1, keepdims=True)


def _cross_fwd(q, kv, *, name):
    s, d = q.shape
    mlen = kv.shape[1]
    tm = min(512, s)

    def body(q_ref, kv_ref, o_ref):
        for h in range(MEM_HEADS):
            cols = slice(h * MEM_HD, (h + 1) * MEM_HD)
            p = _cross_probs(q_ref[:, cols].astype(BF16), kv_ref[h].astype(BF16))
            o_ref[:, cols] = _dot(p.astype(BF16), kv_ref[MEM_HEADS + h].astype(BF16)).astype(BF16)

    return pl.pallas_call(
        body, grid=(s // tm,),
        in_specs=[_row_spec(tm, d), pl.BlockSpec((2 * MEM_HEADS, mlen, MEM_HD), lambda i: (0, 0, 0))],
        out_specs=_row_spec(tm, d), out_shape=jax.ShapeDtypeStruct((s, d), BF16),
        compiler_params=_cparams(1, VMEM_LIMIT_MB), name=name)(q, kv)


def _cross_bwd(q, kv, do, *, name):
    s, d = q.shape
    mlen = kv.shape[1]
    tm = min(512, s)

    def body(q_ref, kv_ref, do_ref, dq_ref, dkv_ref):
        @pl.when(pl.program_id(0) == 0)
        def _():
            dkv_ref[...] = jnp.zeros_like(dkv_ref)

        for h in range(MEM_HEADS):
            cols = slice(h * MEM_HD, (h + 1) * MEM_HD)
            qh = q_ref[:, cols].astype(BF16)
            kh = kv_ref[h].astype(BF16)
            doh = do_ref[:, cols].astype(BF16)
            p = _cross_probs(qh, kh)
            dp = _dot(doh, kv_ref[MEM_HEADS + h].astype(BF16), NT)
            ds = p * (dp - jnp.sum(dp * p, axis=-1, keepdims=True)) * (1.0 / math.sqrt(MEM_HD))
            dq_ref[:, cols] = _dot(ds.astype(BF16), kh).astype(BF16)
            dkv_ref[h] += _dot(ds.T.astype(BF16), qh)
            dkv_ref[MEM_HEADS + h] += _dot(p.T.astype(BF16), doh)

    kv_spec = pl.BlockSpec((2 * MEM_HEADS, mlen, MEM_HD), lambda i: (0, 0, 0))
    return pl.pallas_call(
        body, grid=(s // tm,), in_specs=[_row_spec(tm, d), kv_spec, _row_spec(tm, d)],
        out_specs=[_row_spec(tm, d), kv_spec],
        out_shape=[jax.ShapeDtypeStruct((s, d), BF16), jax.ShapeDtypeStruct(kv.shape, F32)],
        compiler_params=_cparams(1, VMEM_LIMIT_MB), name=name)(q, kv, do)


def _proj_bwd_dx(parts, w_in, dz, *, name, job=None, ln_bwd=None):
    s = dz.shape[0]
    d = D_MODEL
    tm = min(256, s)
    widths = [p.shape[1] for p in parts]

    n_parts = len(parts)

    def body(*refs):
        part_refs, w_ref, dz_ref = refs[:n_parts], refs[n_parts], refs[n_parts + 1]
        acc = ALPHA * dz_ref[...]
        off = 0
        for p_ref, wd in zip(part_refs, widths):
            acc = acc + _dot(p_ref[...].astype(BF16), w_ref[off:off + wd, :])
            off += wd
        if ln_bwd is not None:
            x_ref, f_ref, g_ref, b_ref = refs[n_parts + 2:n_parts + 6]
            dzp, dg, db = _ln_bwd_tile(x_ref[...], f_ref[...], ln_bwd[2], g_ref[...], b_ref[...], acc)
            _ln_bwd_store(pl.program_id(0) == 0, dzp, dg, db, *refs[n_parts + 6:])
        else:
            refs[-1][...] = acc

    in_specs = [_row_spec(tm, wd) for wd in widths] + [pl.BlockSpec(w_in.shape, lambda i: (0, 0)), _row_spec(tm, d)]
    args = (*parts, w_in, dz)
    if ln_bwd is not None:
        ln_in, out_specs = _ln_bwd_specs(tm, d)
        in_specs, out_shape = in_specs + ln_in, _ln_bwd_shapes(s, d)
        args += (ln_bwd[0], ln_bwd[1], ln_bwd[3], ln_bwd[4])
    else:
        out_specs, out_shape = [_row_spec(tm, d)], [jax.ShapeDtypeStruct((s, d), F32)]
    outs, job_out = _pallas_with_job(
        body, job, lambda: pl.program_id(0), s // tm, grid=(s // tm,), in_specs=in_specs, out_specs=out_specs,
        out_shape=out_shape, compiler_params=_cparams(1, VMEM_LIMIT_MB), name=name, args=args)
    return (outs[0] if ln_bwd is None else outs), job_out


def _proj_fwd(hb, w_in_t, *, name):
    s, d = hb.shape
    tm = min(512, s)
    lru, sb = 2 * LRU_W, SB_W

    def body(h_ref, w_ref, xbg_ref, q_ref, kv_ref):
        h = h_ref[...]
        xbg_ref[...] = _dot(h, w_ref[0:lru, :], NT)
        q_ref[...] = (_dot(h, w_ref[lru:lru + sb, :], NT) * (1.0 / math.sqrt(SB_HEAD))).astype(BF16)
        kv_ref[...] = _dot(h, w_ref[lru + sb:lru + 3 * sb, :], NT).astype(BF16)

    return pl.pallas_call(
        body, grid=(s // tm,), in_specs=[_row_spec(tm, d), pl.BlockSpec(w_in_t.shape, lambda i: (0, 0))],
        out_specs=[_row_spec(tm, lru), _row_spec(tm, sb), _row_spec(tm, 2 * sb)],
        out_shape=[jax.ShapeDtypeStruct((s, lru), F32), jax.ShapeDtypeStruct((s, sb), BF16),
                   jax.ShapeDtypeStruct((s, 2 * sb), BF16)],
        compiler_params=_cparams(1, VMEM_LIMIT_MB), name=name)(hb, w_in_t)


def _mix_bwd(dzb, w_out, yl, ys, gl, gs, *, name, job=None):
    s, c = yl.shape
    tm = min(512, s)

    def body(dz_ref, w_ref, yl_ref, ys_ref, gl_ref, gs_ref, dyl_ref, dys_ref, dgl_ref, dgs_ref):
        @pl.when(pl.program_id(0) == 0)
        def _():
            dgl_ref[...] = jnp.zeros_like(dgl_ref)
            dgs_ref[...] = jnp.zeros_like(dgs_ref)

        dy = _dot(dz_ref[...], w_ref[...], NT)
        _, vjp = jax.vjp(_mixnorm_math, yl_ref[...], ys_ref[...], gl_ref[...], gs_ref[...])
        dyl, dys, dgl, dgs = vjp((dy[:, 0:c], dy[:, c:2 * c]))
        dyl_ref[...] = dyl
        dys_ref[...] = dys.astype(BF16)
        dgl_ref[...] += dgl
        dgs_ref[...] += dgs

    return _pallas_with_job(
        body, job, lambda: pl.program_id(0), s // tm, grid=(s // tm,),
        in_specs=[_row_spec(tm, 2 * c), pl.BlockSpec(w_out.shape, lambda i: (0, 0)), _row_spec(tm, c), _row_spec(tm, c),
                  _par_spec(c), _par_spec(c)],
        out_specs=[_row_spec(tm, c), _row_spec(tm, c), _par_spec(c), _par_spec(c)],
        out_shape=[jax.ShapeDtypeStruct((s, c), F32), jax.ShapeDtypeStruct((s, c), BF16)] + [jax.ShapeDtypeStruct((1, c), F32)] * 2,
        compiler_params=_cparams(1, VMEM_LIMIT_MB), name=name, args=(dzb, w_out, yl, ys, gl, gs))


def _proj_dw(h, parts, *, name):
    s, d = h.shape
    tk = min(2048, s)
    widths = [p.shape[1] for p in parts]

    def body(*refs):
        h_ref, part_refs, o_ref, acc_ref = refs[0], refs[1:-2], refs[-2], refs[-1]
        k = pl.program_id(0)

        @pl.when(k == 0)
        def _():
            acc_ref[...] = jnp.zeros_like(acc_ref)

        hv = h_ref[...].astype(BF16)
        off = 0
        for p_ref, wd in zip(part_refs, widths):
            acc_ref[off:off + wd, :] += _dot(p_ref[...].astype(F32).T.astype(BF16), hv)
            off += wd

        @pl.when(k == s // tk - 1)
        def _():
            o_ref[...] = acc_ref[...].astype(BF16)

    return pl.pallas_call(
        body, grid=(s // tk,), in_specs=[_row_spec(tk, d)] + [_row_spec(tk, wd) for wd in widths],
        out_specs=pl.BlockSpec((sum(widths), d), lambda k: (0, 0)),
        out_shape=jax.ShapeDtypeStruct((sum(widths), d), BF16),
        scratch_shapes=[pltpu.VMEM((sum(widths), d), F32)],
        compiler_params=_cparams(1, VMEM_LIMIT_MB), name=name)(h, *parts)


def _mesh_pos():
    return lax.axis_index("x"), lax.axis_index("y"), lax.axis_index("c")


class _GatherJob:
    n_phases = 3

    def __init__(self, shards):
        n = len(shards)
        self.inputs = list(shards)
        self.out_shape = [jax.ShapeDtypeStruct((N_DEV,) + a.shape, a.dtype) for a in shards]
        self.scratch = [pltpu.SemaphoreType.DMA((n, 7)), pltpu.SemaphoreType.DMA((n, 7)), pltpu.SemaphoreType.DMA((n,))]

    def phase(self, k, ins, outs, sems):
        send_sems, recv_sems, local_sems = sems
        n = len(ins)
        x, y, c = _mesh_pos()
        me, sibling = (x, y, c), (x, y, 1 - c)
        chips = [(1 - x, y), (x, 1 - y), (1 - x, 1 - y)]

        def copy(a, slot, block, to, src=None):
            dst = outs[a].at[4 * block[0] + 2 * block[1] + block[2]]
            return pltpu.make_async_remote_copy(
                src_ref=dst if src is None else src, dst_ref=dst,
                send_sem=send_sems.at[a, slot], recv_sem=recv_sems.at[a, slot], device_id=to, device_id_type=MESH)

        def mine():
            return [pltpu.make_async_copy(ins[a], outs[a].at[4 * x + 2 * y + c], local_sems.at[a]) for a in range(n)]

        def first():
            return [cp for a in range(n) for cp in
                    [copy(a, 0, me, sibling, src=ins[a])] + [copy(a, 1 + j, me, (*chip, c), src=ins[a]) for j, chip in enumerate(chips)]]

        def passed(a, j):
            return copy(a, 4 + j, (*chips[j], c), sibling)

        if k == 0:
            for cp in mine() + first():
                cp.start()
        elif k == 1:
            for a in range(n):
                for j, chip in enumerate(chips):
                    copy(a, 1 + j, (*chip, c), me).wait_recv()
                    passed(a, j).start()
        else:
            for a in range(n):
                copy(a, 0, sibling, me).wait_recv()
                for j, chip in enumerate(chips):
                    copy(a, 4 + j, (*chip, 1 - c), me).wait_recv()
            for cp in first() + [passed(a, j) for a in range(n) for j in range(3)]:
                cp.wait_send()
            for cp in mine():
                cp.wait()


class _SiblingJob:
    n_phases = 2

    def __init__(self, grads):
        n, half = len(grads), N_DEV // 2
        self.inputs = list(grads)
        self.out_shape = [jax.ShapeDtypeStruct((half,) + g.shape[1:], g.dtype) for g in grads]
        self.scratch = [pltpu.SemaphoreType.DMA((n, half)), pltpu.SemaphoreType.DMA((n, half))]

    def phase(self, k, ins, outs, sems):
        send_sems, recv_sems = sems
        x, y, c = _mesh_pos()
        copies = [pltpu.make_async_remote_copy(
            src_ref=ins[a].at[2 * j + (1 - c)], dst_ref=outs[a].at[j], send_sem=send_sems.at[a, j],
            recv_sem=recv_sems.at[a, j], device_id=(x, y, 1 - c), device_id_type=MESH)
            for a in range(len(ins)) for j in range(N_DEV // 2)]
        for cp in copies:
            if k == 0:
                cp.start()
            else:
                cp.wait()


class _ChipJob:
    n_phases = 2

    def __init__(self, partials):
        n, half = len(partials), N_DEV // 2
        self.inputs = list(partials)
        self.out_shape = [jax.ShapeDtypeStruct(p.shape, p.dtype) for p in partials]
        self.scratch = [pltpu.SemaphoreType.DMA((n, half)), pltpu.SemaphoreType.DMA((n, half)), pltpu.SemaphoreType.DMA((n,))]

    def phase(self, k, ins, outs, sems):
        send_sems, recv_sems, local_sems = sems
        n, half = len(ins), N_DEV // 2
        x, y, c = _mesh_pos()
        jme = 2 * x + y
        local = [pltpu.make_async_copy(ins[a].at[jme], outs[a].at[jme], local_sems.at[a]) for a in range(n)]
        for cp in local:
            if k == 0:
                cp.start()
            else:
                cp.wait()
        for j in range(half):
            @pl.when(jme != j)
            def _(j=j):
                for a in range(n):
                    if k == 0:
                        pltpu.make_async_remote_copy(
                            src_ref=ins[a].at[j], dst_ref=outs[a].at[jme], send_sem=send_sems.at[a, j],
                            recv_sem=recv_sems.at[a, jme], device_id=(j // 2, j % 2, c), device_id_type=MESH).start()
                    else:
                        arrive = pltpu.make_async_remote_copy(
                            src_ref=ins[a].at[j], dst_ref=outs[a].at[j], send_sem=send_sems.at[a, j],
                            recv_sem=recv_sems.at[a, j], device_id=(j // 2, j % 2, c), device_id_type=MESH)
                        arrive.wait_recv()
                        arrive.wait_send()


def _phase_steps(n_phases, n_steps):
    return [0, n_steps - 1] if n_phases == 2 else [0, n_steps // 2, n_steps - 1]


def _pallas_with_job(body, job, step_fn, n_steps, *, grid, in_specs, out_specs, out_shape, scratch_shapes=(),
                     compiler_params, name, args):
    in_specs, out_specs, out_shape, scratch_shapes = list(in_specs), list(out_specs), list(out_shape), list(scratch_shapes)
    if job is None:
        outs = pl.pallas_call(body, grid=grid, in_specs=in_specs, out_specs=out_specs, out_shape=out_shape,
                              scratch_shapes=scratch_shapes, compiler_params=compiler_params, name=name)(*args)
        return list(outs), None
    ni, no, ns = len(in_specs), len(out_specs), len(scratch_shapes)
    ji, jo = len(job.inputs), len(job.out_shape)
    steps = _phase_steps(job.n_phases, n_steps)

    def wrapped(*refs):
        own_in, job_in = refs[:ni], refs[ni:ni + ji]
        own_out, job_out = refs[ni + ji:ni + ji + no], refs[ni + ji + no:ni + ji + no + jo]
        own_scr, job_scr = refs[ni + ji + no + jo:ni + ji + no + jo + ns], refs[ni + ji + no + jo + ns:]
        step = step_fn()
        for k, at in enumerate(steps):
            @pl.when(step == at)
            def _(k=k):
                job.phase(k, job_in, job_out, job_scr)
        body(*own_in, *own_out, *own_scr)

    any_spec = pl.BlockSpec(memory_space=pl.ANY)
    outs = pl.pallas_call(
        wrapped, grid=grid, in_specs=in_specs + [any_spec] * ji, out_specs=out_specs + [any_spec] * jo,
        out_shape=out_shape + list(job.out_shape), scratch_shapes=scratch_shapes + list(job.scratch),
        compiler_params=compiler_params, name=name)(*args, *job.inputs)
    return list(outs[:no]), list(outs[no:])


def _run_job(job, *, name):
    ji, jo = len(job.inputs), len(job.out_shape)

    def body(*refs):
        for k in range(job.n_phases):
            job.phase(k, refs[:ji], refs[ji:ji + jo], refs[ji + jo:])

    any_spec = pl.BlockSpec(memory_space=pl.ANY)
    return pl.pallas_call(body, in_specs=[any_spec] * ji, out_specs=[any_spec] * jo, out_shape=list(job.out_shape),
                          scratch_shapes=list(job.scratch), name=name)(*job.inputs)


def _pair_add(grad, recv, cj_idx, *, name):
    half, r, cdim = recv.shape
    tr = _row_tile(r, 512)

    def body(cj_ref, g_ref, r_ref, own_ref, tb_ref):
        t = g_ref[...].astype(F32) + r_ref[...].astype(F32)
        tb_ref[...] = t.astype(BF16)

        @pl.when(pl.program_id(1) == cj_ref[1])
        def _():
            own_ref[...] = t

    grid_spec = pltpu.PrefetchScalarGridSpec(
        num_scalar_prefetch=1, grid=(r // tr, half),
        in_specs=[pl.BlockSpec((None, tr, cdim), lambda i, j, cj: (2 * j + cj[0], i, 0)),
                  pl.BlockSpec((None, tr, cdim), lambda i, j, cj: (j, i, 0))],
        out_specs=[pl.BlockSpec((tr, cdim), lambda i, j, cj: (i, 0)),
                   pl.BlockSpec((None, tr, cdim), lambda i, j, cj: (j, i, 0))])
    return pl.pallas_call(
        body, grid_spec=grid_spec,
        out_shape=[jax.ShapeDtypeStruct((r, cdim), F32), jax.ShapeDtypeStruct(recv.shape, BF16)],
        compiler_params=_cparams(2), name=name)(cj_idx, grad, recv)


def _adam_math(w, g, m, v):
    m = ADAM_B1 * m + (1.0 - ADAM_B1) * g
    v = ADAM_B2 * v + (1.0 - ADAM_B2) * (g * g)
    m_hat = m / (1.0 - ADAM_B1 ** ADAM_STEP)
    v_hat = v / (1.0 - ADAM_B2 ** ADAM_STEP)
    delta = -ADAM_LR * (m_hat / (jnp.sqrt(v_hat) + ADAM_EPS) + ADAM_WD * w)
    return delta, m, v


def _reduce_adam(own, recv, j_idx, w, m, v, *, name):
    half, r, cdim = recv.shape
    tr = _row_tile(r, 256)

    def body(j_ref, own_ref, recv_ref, w_ref, m_ref, v_ref, g_ref, d_ref, nm_ref, nv_ref):
        jme = j_ref[0]
        g = own_ref[...]
        for sl in range(half):
            g = g + jnp.where(sl == jme, 0.0, recv_ref[sl].astype(F32))
        delta, nm, nv = _adam_math(w_ref[...], g, m_ref[...], v_ref[...])
        g_ref[...] = g
        d_ref[...] = delta
        nm_ref[...] = nm
        nv_ref[...] = nv

    tile = pl.BlockSpec((tr, cdim), lambda i, j_ref: (i, 0))
    grid_spec = pltpu.PrefetchScalarGridSpec(
        num_scalar_prefetch=1, grid=(r // tr,),
        in_specs=[tile, pl.BlockSpec((half, tr, cdim), lambda i, j_ref: (0, i, 0)), tile, tile, tile],
        out_specs=[tile] * 4)
    return pl.pallas_call(
        body, grid_spec=grid_spec, out_shape=[jax.ShapeDtypeStruct((r, cdim), F32)] * 4,
        compiler_params=_cparams(1), name=name)(j_idx, own, recv, w, m, v)


class _SmallGatherJob:
    n_phases = 2

    def __init__(self, pack):
        self.inputs = [pack]
        self.out_shape = [jax.ShapeDtypeStruct((N_DEV,) + pack.shape, pack.dtype)]
        self.scratch = [pltpu.SemaphoreType.DMA((N_DEV,)), pltpu.SemaphoreType.DMA((N_DEV,)), pltpu.SemaphoreType.DMA(())]

    def phase(self, k, ins, outs, sems):
        (pack,), (land,), (send_sems, recv_sems, local_sem) = ins, outs, sems
        x, y, c = _mesh_pos()
        me = 4 * x + 2 * y + c
        mine = pltpu.make_async_copy(pack, land.at[me], local_sem)
        if k == 0:
            mine.start()
        else:
            mine.wait()
        for t in range(N_DEV):
            @pl.when(me != t)
            def _(t=t):
                peer = (t // 4, (t // 2) % 2, t % 2)
                if k == 0:
                    pltpu.make_async_remote_copy(
                        src_ref=pack, dst_ref=land.at[me], send_sem=send_sems.at[t], recv_sem=recv_sems.at[me],
                        device_id=peer, device_id_type=MESH).start()
                else:
                    cp = pltpu.make_async_remote_copy(
                        src_ref=pack, dst_ref=land.at[t], send_sem=send_sems.at[t], recv_sem=recv_sems.at[t],
                        device_id=peer, device_id_type=MESH)
                    cp.wait_recv()
                    cp.wait_send()


class _Jobs:
    def __init__(self, jobs):
        self.jobs = jobs
        self.n_phases = jobs[0].n_phases
        assert all(j.n_phases == self.n_phases for j in jobs)
        self.inputs = [a for j in jobs for a in j.inputs]
        self.out_shape = [o for j in jobs for o in j.out_shape]
        self.scratch = [sc for j in jobs for sc in j.scratch]

    def phase(self, k, ins, outs, sems):
        i = o = sc = 0
        for j in self.jobs:
            ni, no, nsc = len(j.inputs), len(j.out_shape), len(j.scratch)
            j.phase(k, ins[i:i + ni], outs[o:o + no], sems[sc:sc + nsc])
            i, o, sc = i + ni, o + no, sc + nsc


def _adam_small(w, land, m, v, *, name):
    def body(w_ref, land_ref, m_ref, v_ref, g_ref, d_ref, nm_ref, nv_ref):
        total = land_ref[0]
        for t in range(1, N_DEV):
            total = total + land_ref[t]
        g_ref[...] = total
        d_ref[...], nm_ref[...], nv_ref[...] = _adam_math(w_ref[...], total, m_ref[...], v_ref[...])

    vmem = pl.BlockSpec(memory_space=pltpu.VMEM)
    return pl.pallas_call(
        body, in_specs=[vmem] * 4, out_specs=[vmem] * 4, out_shape=[jax.ShapeDtypeStruct(w.shape, F32)] * 4,
        name=name)(w, land, m, v)


def _block_diag(w):
    h, a, b = w.shape
    eye = jnp.eye(h, dtype=w.dtype)
    return (eye[:, None, :, None] * w[:, :, None, :]).reshape(h * a, h * b)


def _diag_blocks(w, h):
    a = w.shape[0] // h
    return jnp.stack([w[i * a:(i + 1) * a, i * a:(i + 1) * a] for i in range(h)])


def _local_step(x, mem, target, p, hooks=None):
    g = {}
    p = dict(p)
    w_rg, w_ig = _block_diag(p["w_rgate"]).astype(BF16), _block_diag(p["w_igate"]).astype(BF16)

    (gu1, f1, xb, h1, h1b), arrived = _ffn_fwd(x, p["ffn1_w13"], p["ffn1_w2"], name="ffn1_fwd", ln=(p["ln1_g"], p["ln1_b"]),
                                               job=hooks and hooks.gather_job(_GATHER_MIXER))
    if hooks:
        p.update(hooks.gathered(_GATHER_MIXER, arrived))
    w_in = p["w_in"]
    xbg, q_sb, kv_sb = _proj_fwd(h1b, w_in, name="proj_fwd")
    xc, hseq, y_lru = _lru_fwd(xbg, p["conv_w"], p["conv_b"], w_rg, p["b_rgate"], w_ig, p["b_igate"],
                               p["lru_lambda"], name="lru_fwd")
    y_sb, arrived = _sb_fwd(q_sb, kv_sb, name="sb_fwd", job=hooks and hooks.gather_job(_GATHER_FFN2))
    if hooks:
        p.update(hooks.gathered(_GATHER_FFN2, arrived))
    ymix = _mixnorm_fwd(y_lru, y_sb, p["g_lru"], p["g_sb"], name="mixnorm_fwd")
    mix, h2, h2b = _matmul(ymix, p["w_out"], ln=(h1, 1.0, p["ln2_g"], p["ln2_b"]), name="mix_out_ln2")
    qm = _matmul(h2b, p["mem_wq"], out_dtype=BF16, name="cross_q")
    kv = _matmul(mem, p["mem_wkv"], name="cross_kv")
    o_cross = _cross_fwd(qm, kv, name="cross_fwd")
    cross, h3, h3b = _matmul(o_cross, p["mem_wo"], ln=(h2, 1.0, p["ln3_g"], p["ln3_b"]), name="cross_out_ln3")
    (gu2, dz4, dz4b, g["ln4_g"], g["ln4_b"], loss), _ = _ffn_fwd(
        h3b, p["ffn2_w13"], p["ffn2_w2"], loss_head=(h3, target, p["ln4_g"], p["ln4_b"]), name="ffn2_fwd_loss")
    (dz3, dz3b, dgu2, act2, g["ln3_g"], g["ln3_b"]), _ = _ffn_bwd(
        dz4, gu2, p["ffn2_w13"], p["ffn2_w2"], ln_bwd=(h2, cross, 1.0, p["ln3_g"], p["ln3_b"]), name="ffn2_bwd_ln3")
    g["ffn2_w13"] = _matmul(dgu2, h3b, trans_a=True, tm=FF_CHUNK, tn=1024, tk=2048, out_dtype=BF16, name="ffn2_dw13")
    g["ffn2_w2"] = _matmul(act2, dz4b, trans_a=True, tm=FF_CHUNK, tn=1024, tk=2048, scale=0.5, out_dtype=BF16, name="ffn2_dw2")
    g["mem_wo"] = _matmul(o_cross, dz3b, trans_a=True, tm=1024, tn=1024, tk=1024, out_dtype=BF16, name="cross_dwo")
    do_cross = _matmul(dz3b, p["mem_wo"], trans_b=True, out_dtype=BF16, name="cross_do")
    dqm, dkv = _cross_bwd(qm, kv, do_cross, name="cross_bwd")
    g["mem_wq"] = _matmul(h2b, dqm, trans_a=True, tm=1024, tn=1024, tk=1024, out_dtype=BF16, name="cross_dwq")
    g["mem_wkv"] = _matmul(mem, dkv, trans_a=True, tm=1024, tn=MEM_HD, tk=256, out_dtype=BF16, name="cross_dwkv")
    dz2, dz2b, g["ln2_g"], g["ln2_b"] = _matmul(dqm, p["mem_wq"], trans_b=True, res=dz3, res_coeff=ALPHA,
                                                ln_bwd=(h1, mix, 1.0, p["ln2_g"], p["ln2_b"]), name="cross_dh_ln2")
    g["w_out"] = _matmul(ymix, dz2b, trans_a=True, tm=1024, tn=1024, tk=1024, out_dtype=BF16, name="mix_dwout")
    (dy_lru, dy_sb, g["g_lru"], g["g_sb"]), arrived = _mix_bwd(
        dz2b, p["w_out"], y_lru, y_sb, p["g_lru"], p["g_sb"], name="mix_bwd", job=hooks and hooks.sibling_job(_REDUCE_EARLY, g))
    (dq, dk, dv), arrived = _sb_bwd(q_sb, kv_sb, y_sb, dy_sb, name="sb_bwd",
                                    job=hooks and hooks.chip_job(_REDUCE_EARLY, arrived))
    if hooks:
        hooks.reduced(_REDUCE_EARLY, arrived)
    dxc, dgate, dwr, dwi, g["b_rgate"], g["b_igate"], g["lru_lambda"] = _lru_bwd(
        xc, xbg, hseq, dy_lru, w_rg, p["b_rgate"], w_ig, p["b_igate"], p["lru_lambda"], name="lru_bwd")
    g["w_rgate"], g["w_igate"] = _diag_blocks(dwr, 8), _diag_blocks(dwi, 8)
    dxb, g["conv_w"], g["conv_b"] = _conv_bwd(dxc, xbg, p["conv_w"], name="conv_bwd")
    parts = [dxb, dgate, dq, dk, dv]
    g["w_in"] = _proj_dw(h1b, parts, name="proj_dw")
    (dz1, dz1b, g["ln1_g"], g["ln1_b"]), arrived = _proj_bwd_dx(
        parts, w_in, dz2, ln_bwd=(x, f1, 0.5, p["ln1_g"], p["ln1_b"]), name="proj_dh_ln1",
        job=hooks and hooks.sibling_job(['w_in'], g))
    (grad_x, dgu1, act1), _ = _ffn_bwd(dz1, gu1, p["ffn1_w13"], p["ffn1_w2"], name="ffn1_bwd")
    dw13 = dict(trans_a=True, tm=FF_CHUNK, tn=1024, tk=2048, out_dtype=BF16, name="ffn1_dw13")
    dw2 = dict(trans_a=True, tm=FF_CHUNK, tn=1024, tk=2048, scale=0.5, out_dtype=BF16, name="ffn1_dw2")
    if not hooks:
        g["ffn1_w13"] = _matmul(dgu1, xb, **dw13)
        g["ffn1_w2"] = _matmul(act1, dz1b, **dw2)
        return loss, grad_x, g
    g["ffn1_w13"], arrived = _matmul(dgu1, xb, job=_Jobs([hooks.chip_job(['w_in'], arrived), hooks.small_job(g, loss)]), **dw13)
    hooks.reduced(['w_in'], arrived[:1])
    hooks.small_land = arrived[1]
    arrived = _run_job(hooks.sibling_job(['ffn1_w13'], g), name="reduce_sibling_w13")
    g["ffn1_w2"], arrived = _matmul(act1, dz1b, job=hooks.chip_job(['ffn1_w13'], arrived), **dw2)
    hooks.reduced(['ffn1_w13'], arrived)
    arrived = _run_job(hooks.sibling_job(['ffn1_w2'], g), name="reduce_sibling_w2")
    hooks.reduced(['ffn1_w2'], _run_job(hooks.chip_job(['ffn1_w2'], arrived), name="reduce_chips_w2"))
    return loss, grad_x, g


_WEIGHTS = ['ffn1_w13', 'ffn1_w2', 'ln1_g', 'ln1_b', 'w_in', 'conv_w', 'conv_b', 'w_rgate', 'b_rgate', 'w_igate',
            'b_igate', 'lru_lambda', 'g_lru', 'g_sb', 'w_out', 'ln2_g', 'ln2_b', 'mem_wq', 'mem_wkv', 'mem_wo',
            'ln3_g', 'ln3_b', 'ffn2_w13', 'ffn2_w2', 'ln4_g', 'ln4_b']
_SHARDED = ['ffn1_w13', 'ffn1_w2', 'w_in', 'w_out', 'mem_wq', 'mem_wkv', 'mem_wo', 'ffn2_w13', 'ffn2_w2']
_SMALL = ['ln1_g', 'ln1_b', 'ln2_g', 'ln2_b', 'ln3_g', 'ln3_b', 'ln4_g', 'ln4_b', 'conv_b', 'b_rgate', 'b_igate',
          'lru_lambda', 'g_lru', 'g_sb', 'w_rgate', 'w_igate']


def _pack_small(d, conv_w_full=None, loss=None):
    rows = [d[n].reshape(-1) for n in _SMALL]
    if conv_w_full is not None:
        rows.append(conv_w_full.reshape(-1))
    if loss is not None:
        rows.append(loss.reshape(-1))
    flat = jnp.concatenate(rows)
    pad = (-flat.shape[0]) % (8 * D_MODEL)
    return jnp.pad(flat, (0, pad)).reshape(-1, D_MODEL)


def _unpack_small(pack, like, with_conv):
    flat = pack.reshape(-1)
    out, off = {}, 0
    for n in _SMALL:
        size = math.prod(like[n].shape)
        out[n] = flat[off:off + size].reshape(like[n].shape)
        off += size
    conv = flat[off:off + CONV_W * LRU_W].reshape(CONV_W, LRU_W) if with_conv else None
    return out, conv


_GATHER_FIRST = ['ffn1_w13', 'ffn1_w2']
_GATHER_MIXER = ['w_in', 'conv_w', 'w_out', 'mem_wq', 'mem_wkv', 'mem_wo']
_GATHER_FFN2 = ['ffn2_w13', 'ffn2_w2']
_REDUCE_EARLY = ['ffn2_w13', 'ffn2_w2', 'mem_wo', 'mem_wq', 'mem_wkv', 'w_out']


_TRANSPOSED = ('ffn1_w13', 'ffn2_w13', 'w_in')


def _own_layout(n, a):
    return a.T if n in _TRANSPOSED else a


def _weight_layout(n, full):
    if n in ('ffn1_w13', 'ffn2_w13', 'mem_wkv'):
        return full
    if n == 'conv_w':
        return full.transpose(1, 0, 2).reshape(CONV_W, LRU_W)
    return full.reshape(-1, full.shape[-1])


def _grad_blocks(n, g):
    return g.reshape((N_DEV, -1, g.shape[-1]))


class _Hooks:
    def __init__(self, w, cj_idx):
        self.w, self.cj_idx = w, cj_idx
        self.sums, self.recv = {}, {}

    def shard(self, n):
        return self.w[n][0] if n == 'conv_w' else _own_layout(n, self.w[n][0]).astype(BF16)

    def gather_job(self, names):
        return _GatherJob([self.shard(n) for n in names])

    def gathered(self, names, outs):
        return {n: _weight_layout(n, o) for n, o in zip(names, outs)}

    def sibling_job(self, names, g):
        self.blocks = [_grad_blocks(n, g[n]) for n in names]
        return _SiblingJob(self.blocks)

    def chip_job(self, names, from_sibling):
        for n, b, r in zip(names, self.blocks, from_sibling):
            self.sums[n] = _pair_add(b, r, self.cj_idx, name=f"pair_add_{n}")
        return _ChipJob([self.sums[n][1] for n in names])

    def reduced(self, names, from_chips):
        self.recv.update(zip(names, from_chips))

    def small_job(self, g, loss):
        return _SmallGatherJob(_pack_small(g, conv_w_full=g["conv_w"], loss=loss))


def kernel(x, mem, ffn1_w13, ffn1_w2, ln1_g, ln1_b, w_in, conv_w, conv_b, w_rgate, b_rgate, w_igate, b_igate, lru_lambda, g_lru, g_sb, w_out, ln2_g, ln2_b, mem_wq, mem_wkv, mem_wo, ln3_g, ln3_b, ffn2_w13, ffn2_w2, ln4_g, ln4_b, loss_target, m_ffn1_w13, m_ffn1_w2, m_ln1_g, m_ln1_b, m_w_in, m_conv_w, m_conv_b, m_w_rgate, m_b_rgate, m_w_igate, m_b_igate, m_lru_lambda, m_g_lru, m_g_sb, m_w_out, m_ln2_g, m_ln2_b, m_mem_wq, m_mem_wkv, m_mem_wo, m_ln3_g, m_ln3_b, m_ffn2_w13, m_ffn2_w2, m_ln4_g, m_ln4_b, v_ffn1_w13, v_ffn1_w2, v_ln1_g, v_ln1_b, v_w_in, v_conv_w, v_conv_b, v_w_rgate, v_b_rgate, v_w_igate, v_b_igate, v_lru_lambda, v_g_lru, v_g_sb, v_w_out, v_ln2_g, v_ln2_b, v_mem_wq, v_mem_wkv, v_mem_wo, v_ln3_g, v_ln3_b, v_ffn2_w13, v_ffn2_w2, v_ln4_g, v_ln4_b):
    args = locals()
    w = {n: args[n] for n in _WEIGHTS}
    mom = {n: args["m_" + n] for n in _WEIGHTS}
    var = {n: args["v_" + n] for n in _WEIGHTS}
    ix, iy, ic = lax.axis_index("x"), lax.axis_index("y"), lax.axis_index("c")

    j_idx = jnp.reshape(2 * ix + iy, (1,)).astype(jnp.int32)
    hooks = _Hooks(w, jnp.stack([ic, 2 * ix + iy]).astype(jnp.int32))
    p = {n: (w[n][0] if w[n].ndim == 4 else w[n]) for n in _WEIGHTS if n not in _SHARDED and n != 'conv_w'}
    p.update(hooks.gathered(_GATHER_FIRST, _run_job(hooks.gather_job(_GATHER_FIRST), name="gather_ffn1")))
    _, grad_x, g = _local_step(x[0], mem[0], loss_target[0], p, hooks)

    grads, delta, new_m, new_v = {}, {}, {}, {}
    for n in _SHARDED:
        grads[n], delta[n], new_m[n], new_v[n] = (
            _own_layout(n, o).reshape(w[n].shape) for o in _reduce_adam(
                hooks.sums[n][0], hooks.recv[n], j_idx, _own_layout(n, w[n][0]), _own_layout(n, mom[n][0]),
                _own_layout(n, var[n][0]), name=f"adam_{n}"))

    me = 4 * ix + 2 * iy + ic
    own_cols = lambda full: lax.dynamic_slice_in_dim(full, me * (LRU_W // N_DEV), LRU_W // N_DEV, axis=1)
    land = hooks.small_land
    conv_off = sum(math.prod(w[n].shape) for n in _SMALL)
    land_conv = land.reshape(N_DEV, -1)[:, conv_off:conv_off + CONV_W * LRU_W].reshape(N_DEV, CONV_W, LRU_W)
    land_own = jnp.pad(own_cols(land_conv.reshape(N_DEV * CONV_W, LRU_W)).reshape(N_DEV, -1),
                       ((0, 0), (0, CONV_W * LRU_W - CONV_W * LRU_W // N_DEV)))
    land = lax.dynamic_update_slice(land.reshape(N_DEV, -1), land_own, (0, conv_off)).reshape(land.shape)
    pk = lambda d: _pack_small({n: d[n] for n in _SMALL}, conv_w_full=jnp.pad(d["conv_w"].reshape(-1), (0, CONV_W * LRU_W - CONV_W * LRU_W // N_DEV)))
    g_s, d_s, m_s, v_s = _adam_small(pk(w), land, pk(mom), pk(var), name="adam_small")
    for src, dst in ((g_s, grads), (d_s, delta), (m_s, new_m), (v_s, new_v)):
        vals, conv = _unpack_small(src, w, with_conv=True)
        dst.update(vals)
        dst["conv_w"] = conv.reshape(-1)[:CONV_W * LRU_W // N_DEV].reshape(w["conv_w"].shape)
    loss = g_s.reshape(-1)[conv_off + CONV_W * LRU_W]

    return (loss, grad_x[None], *[grads[n] for n in _WEIGHTS], *[delta[n] for n in _WEIGHTS],
            *[new_m[n] for n in _WEIGHTS], *[new_v[n] for n in _WEIGHTS])
```

```python
import functools
import math

import jax
import jax.numpy as jnp
from jax import lax
from jax.experimental import pallas as pl
from jax.experimental.pallas import tpu as pltpu

F32, BF16 = jnp.float32, jnp.bfloat16

D_MODEL = 1024
LRU_W = 512
SB_W = 512
SB_PAIR = 128
SB_HEAD = 64
SB_ROWS = 128
SB_SKIP = -110.0
D_FF = 2688
FF_CHUNK = 672
N_DEV = 8
MEM_HEADS = 4
MEM_HD = 256
CONV_W = 4
ALPHA = 2.0 ** 0.25
LN_EPS = 1e-5
RMS_EPS = 1e-6
LRU_C = 8.0
ADAM_LR, ADAM_B1, ADAM_B2, ADAM_EPS, ADAM_WD, ADAM_STEP = 0.001, 0.9, 0.999, 1e-08, 0.01, 10

NN = (((1,), (0,)), ((), ()))
NT = (((1,), (1,)), ((), ()))
MESH = pl.DeviceIdType.MESH
VMEM_LIMIT_MB = 56


def _dot(a, b, dn=NN):
    return lax.dot_general(a, b, dn, preferred_element_type=F32)


def _cparams(n_axes, vmem_mb=None):
    kw = dict(dimension_semantics=("arbitrary",) * n_axes)
    if vmem_mb is not None:
        kw["vmem_limit_bytes"] = vmem_mb << 20
    return pltpu.CompilerParams(**kw)


def _row_tile(rows, want):
    if rows <= want:
        return rows
    t = want - want % 8
    while rows % t:
        t -= 8
    return t


def _matmul(a, b, *, name, out_dtype=F32, trans_a=False, trans_b=False, tm=512, tn=1024, tk=1024,
            scale=1.0, res=None, res_coeff=1.0, job=None, ln=None, ln_bwd=None):
    a_chunked, b_chunked = a.ndim == 3, b.ndim == 3
    nc = a.shape[0] if a_chunked else (b.shape[0] if b_chunked else 1)
    a2, b2 = a.shape[-2:], b.shape[-2:]
    (kdim, m) = a2 if trans_a else a2[::-1]
    n = b2[0] if trans_b else b2[1]
    tm, tn, tk = min(tm, m), min(tn, n), min(tk, kdim)
    assert m % tm == 0 and n % tn == 0 and kdim % tk == 0, (name, m, n, kdim)
    nk = kdim // tk

    def a_idx(c, i, j, k):
        idx = (k, i) if trans_a else (i, k)
        return (c,) + idx if a_chunked else idx

    def b_idx(c, i, j, k):
        idx = (j, k) if trans_b else (k, j)
        return (c,) + idx if b_chunked else idx

    a_blk = (tk, tm) if trans_a else (tm, tk)
    b_blk = (tn, tk) if trans_b else (tk, tn)
    in_specs = [pl.BlockSpec(((None,) + a_blk) if a_chunked else a_blk, a_idx),
                pl.BlockSpec(((None,) + b_blk) if b_chunked else b_blk, b_idx)]
    args = [a, b]
    if res is not None:
        in_specs.append(pl.BlockSpec((None, tm, tn), lambda c, i, j, k: (c, i, j)))
        args.append(res.reshape((nc, m, n)))
    if ln is not None:
        assert nc == 1 and tn == n and res is None
        in_specs += [pl.BlockSpec((tm, tn), lambda c, i, j, k: (i, 0)), pl.BlockSpec((1, tn), lambda c, i, j, k: (0, 0)),
                     pl.BlockSpec((1, tn), lambda c, i, j, k: (0, 0))]
        args += [ln[0], ln[2], ln[3]]
    if ln_bwd is not None:
        assert nc == 1 and tn == n and res is not None and ln is None
        in_specs += [pl.BlockSpec((tm, tn), lambda c, i, j, k: (i, 0))] * 2 + [pl.BlockSpec((1, tn), lambda c, i, j, k: (0, 0))] * 2
        args += [ln_bwd[0], ln_bwd[1], ln_bwd[3], ln_bwd[4]]

    def body(*refs):
        if ln_bwd is not None:
            a_ref, b_ref, r_ref, x_ref, f_ref, g_ref, bb_ref, o_ref, dzb_ref, dg_ref, db_ref, acc_ref = refs
        elif res is not None:
            a_ref, b_ref, r_ref, o_ref, acc_ref = refs
        elif ln is not None:
            a_ref, b_ref, x_ref, g_ref, bb_ref, o_ref, h_ref, hb_ref, acc_ref = refs
        else:
            a_ref, b_ref, o_ref, acc_ref = refs
        k = pl.program_id(3)

        @pl.when(k == 0)
        def _():
            acc_ref[...] = jnp.zeros_like(acc_ref)

        av = a_ref[...]
        if trans_a:
            av = av.astype(F32).T
        acc_ref[...] += _dot(av.astype(BF16), b_ref[...].astype(BF16), NT if trans_b else NN)

        @pl.when(k == nk - 1)
        def _():
            out = acc_ref[...]
            if scale != 1.0:
                out = out * scale
            if res is not None:
                out = out + res_coeff * r_ref[...]
            if ln_bwd is not None:
                dz, dg, db = _ln_bwd_tile(x_ref[...], f_ref[...], ln_bwd[2], g_ref[...], bb_ref[...], out)
                _ln_bwd_store(pl.program_id(1) == 0, dz, dg, db, o_ref, dzb_ref, dg_ref, db_ref)
                return
            o_ref[...] = out.astype(out_dtype)
            if ln is not None:
                h = _ln_math(ALPHA * x_ref[...] + ln[1] * out, g_ref[...], bb_ref[...])
                h_ref[...] = h
                hb_ref[...] = h.astype(BF16)

    grid = (nc, m // tm, n // tn, nk)

    def step():
        idx = pl.program_id(0)
        for ax in range(1, 4):
            idx = idx * grid[ax] + pl.program_id(ax)
        return idx

    out_specs = [pl.BlockSpec((None, tm, tn), lambda c, i, j, k: (c, i, j))]
    out_shape = [jax.ShapeDtypeStruct((nc, m, n), out_dtype)]
    if ln is not None:
        out_specs += [pl.BlockSpec((tm, tn), lambda c, i, j, k: (i, 0))] * 2
        out_shape += [jax.ShapeDtypeStruct((m, n), F32), jax.ShapeDtypeStruct((m, n), BF16)]
    if ln_bwd is not None:
        out_specs += [pl.BlockSpec((tm, tn), lambda c, i, j, k: (i, 0))] + [pl.BlockSpec((1, tn), lambda c, i, j, k: (0, 0))] * 2
        out_shape += _ln_bwd_shapes(m, n)[1:]
    outs, job_out = _pallas_with_job(
        body, job, step, math.prod(grid), grid=grid, in_specs=in_specs, out_specs=out_specs, out_shape=out_shape,
        scratch_shapes=[pltpu.VMEM((tm, tn), F32)],
        compiler_params=_cparams(4, VMEM_LIMIT_MB), name=name, args=args)
    out = outs[0] if (a_chunked or b_chunked) else outs[0][0]
    if ln is not None:
        return out, outs[1], outs[2]
    if ln_bwd is not None:
        return out, outs[1], outs[2], outs[3]
    return out if job is None else (out, job_out)


def _ln_math(z, g, b):
    mu = jnp.mean(z, axis=-1, keepdims=True)
    zc = z - mu
    var = jnp.mean(zc * zc, axis=-1, keepdims=True)
    return zc * lax.rsqrt(var + LN_EPS) * g + b


def _row_spec(tm, d):
    return pl.BlockSpec((tm, d), lambda i: (i, 0))


def _par_spec(d, rows=1):
    return pl.BlockSpec((rows, d), lambda i: (0, 0))


def _ln_bwd_tile(x, f, fscale, g, b, dy):
    _, vjp = jax.vjp(_ln_math, ALPHA * x + fscale * f, g, b)
    return vjp(dy)


def _ln_bwd_specs(tm, d):
    return ([_row_spec(tm, d), _row_spec(tm, d), _par_spec(d), _par_spec(d)],
            [_row_spec(tm, d), _row_spec(tm, d), _par_spec(d), _par_spec(d)])


def _ln_bwd_shapes(s, d):
    return [jax.ShapeDtypeStruct((s, d), F32), jax.ShapeDtypeStruct((s, d), BF16),
            jax.ShapeDtypeStruct((1, d), F32), jax.ShapeDtypeStruct((1, d), F32)]


def _ln_bwd_store(first, dz, dg, db, dz_ref, dzb_ref, dg_ref, db_ref):
    @pl.when(first)
    def _():
        dg_ref[...] = jnp.zeros_like(dg_ref)
        db_ref[...] = jnp.zeros_like(db_ref)

    dz_ref[...] = dz
    dzb_ref[...] = dz.astype(BF16)
    dg_ref[...] += dg
    db_ref[...] += db


def _load_weights_once(pairs, sem):
    @pl.when(pl.program_id(0) == 0)
    def _():
        copies = [pltpu.make_async_copy(src, dst, sem.at[n]) for n, (src, dst) in enumerate(pairs)]
        for c in copies:
            c.start()
        for c in copies:
            c.wait()


def _ffn_fwd(h, w13c, w2, *, name, job=None, ln=None, loss_head=None):
    s, d = h.shape
    tm = min(512, s)
    sub = min(256, tm)
    half = N_DEV // 2

    def body(*refs):
        if loss_head is not None:
            (h_ref, w13_hbm, w2_hbm, x_ref, t_ref, g_ref, b_ref, gu_ref, dz_ref, dzb_ref, dg_ref, db_ref, loss_ref,
             w13_v, w2_v, sem) = refs
        elif ln is not None:
            h_ref, w13_hbm, w2_hbm, g_ref, b_ref, gu_ref, f_ref, hb_ref, n_ref, nb_ref, w13_v, w2_v, sem = refs
        else:
            h_ref, w13_hbm, w2_hbm, gu_ref, f_ref, hb_ref, w13_v, w2_v, sem = refs
        _load_weights_once([(w13_hbm, w13_v), (w2_hbm, w2_v)], sem)
        rows_of = [slice(r, r + sub) for r in range(0, tm, sub)]
        accs = []
        for rows in rows_of:
            hb = h_ref[rows, :].astype(BF16)
            if loss_head is None:
                hb_ref[rows, :] = hb
            acc = jnp.zeros((sub, d), F32)
            for k in range(half):
                g = _dot(hb, w13_v[k], NT)
                u = _dot(hb, w13_v[k + half], NT)
                gu_ref[k, rows, :] = g.astype(BF16)
                gu_ref[k + half, rows, :] = u.astype(BF16)
                a = g * jax.nn.sigmoid(g) * u
                acc = acc + _dot(a.astype(BF16), w2_v[pl.ds(k * FF_CHUNK, FF_CHUNK), :])
            accs.append(acc)
        if loss_head is not None:
            first = pl.program_id(0) == 0

            @pl.when(first)
            def _():
                loss_ref[...] = jnp.zeros_like(loss_ref)

                dg_ref[...] = jnp.zeros_like(dg_ref)
                db_ref[...] = jnp.zeros_like(db_ref)

            for rows, acc in zip(rows_of, accs):
                y, vjp = jax.vjp(_ln_math, ALPHA * x_ref[rows, :] + 0.5 * acc, g_ref[...], b_ref[...])
                err = y - t_ref[rows, :]
                sq = jnp.sum(err * err, axis=1, keepdims=True)
                loss_ref[...] += jnp.sum(sq, axis=0, keepdims=True) * (0.5 / d)
                dz, dg, db = vjp(err * (1.0 / d))
                dz_ref[rows, :] = dz
                dzb_ref[rows, :] = dz.astype(BF16)
                dg_ref[...] += dg
                db_ref[...] += db
            return
        for rows, acc in zip(rows_of, accs):
            f_ref[rows, :] = acc
            if ln is not None:
                hn = _ln_math(ALPHA * h_ref[rows, :] + 0.5 * acc, g_ref[...], b_ref[...])
                n_ref[rows, :] = hn
                nb_ref[rows, :] = hn.astype(BF16)

    any_spec = pl.BlockSpec(memory_space=pl.ANY)
    gu_spec = pl.BlockSpec((N_DEV, tm, FF_CHUNK), lambda i: (0, i, 0))
    gu_shape = jax.ShapeDtypeStruct((N_DEV, s, FF_CHUNK), BF16)
    scratch = [pltpu.VMEM(w13c.shape, BF16), pltpu.VMEM(w2.shape, BF16), pltpu.SemaphoreType.DMA((2,))]
    if loss_head is not None:
        return _pallas_with_job(
            body, job, lambda: pl.program_id(0), s // tm, grid=(s // tm,),
            in_specs=[_row_spec(tm, d), any_spec, any_spec, _row_spec(tm, d), _row_spec(tm, d), _par_spec(d), _par_spec(d)],
            out_specs=[gu_spec] + _ln_bwd_specs(tm, d)[1] + [pl.BlockSpec((1, 1), lambda i: (0, 0))],
            out_shape=[gu_shape] + _ln_bwd_shapes(s, d) + [jax.ShapeDtypeStruct((1, 1), F32)],
            scratch_shapes=scratch, compiler_params=_cparams(1, VMEM_LIMIT_MB), name=name,
            args=(h, w13c, w2) + tuple(loss_head))
    n_ln = 0 if ln is None else 2
    return _pallas_with_job(
        body, job, lambda: pl.program_id(0), s // tm,
        grid=(s // tm,), in_specs=[_row_spec(tm, d), any_spec, any_spec] + [_par_spec(d)] * n_ln,
        out_specs=[pl.BlockSpec((N_DEV, tm, FF_CHUNK), lambda i: (0, i, 0)), _row_spec(tm, d), _row_spec(tm, d)]
                  + [_row_spec(tm, d)] * n_ln,
        out_shape=[jax.ShapeDtypeStruct((N_DEV, s, FF_CHUNK), BF16), jax.ShapeDtypeStruct((s, d), F32),
                   jax.ShapeDtypeStruct((s, d), BF16)]
                  + [jax.ShapeDtypeStruct((s, d), F32), jax.ShapeDtypeStruct((s, d), BF16)][:n_ln],
        scratch_shapes=[pltpu.VMEM(w13c.shape, BF16), pltpu.VMEM(w2.shape, BF16), pltpu.SemaphoreType.DMA((2,))],
        compiler_params=_cparams(1, VMEM_LIMIT_MB), name=name, args=(h, w13c, w2) + (() if ln is None else tuple(ln)))


def _ffn_bwd(dz, gu, w13c, w2, *, name, job=None, ln_bwd=None):
    s, d = dz.shape
    tm = min(256, s)
    half = N_DEV // 2

    def body(*refs):
        if ln_bwd is not None:
            (dz_ref, gu_ref, w13_hbm, w2_hbm, x_ref, f_ref, g_ref, b_ref, dzp_ref, dzpb_ref, dgu_ref, act_ref,
             dg_ref, db_ref, w13_v, w2_v, sem) = refs
        else:
            dz_ref, gu_ref, w13_hbm, w2_hbm, dh_ref, dgu_ref, act_ref, w13_v, w2_v, sem = refs
        _load_weights_once([(w13_hbm, w13_v), (w2_hbm, w2_v)], sem)
        dzv = dz_ref[...]
        dfb = (0.5 * dzv).astype(BF16)
        acc = ALPHA * dzv
        for k in range(half):
            g = gu_ref[k].astype(F32)
            u = gu_ref[k + half].astype(F32)
            da = _dot(dfb, w2_v[pl.ds(k * FF_CHUNK, FF_CHUNK), :], NT)
            sg = jax.nn.sigmoid(g)
            silu = g * sg
            dg = (da * u * (sg * (1.0 + g * (1.0 - sg)))).astype(BF16)
            du = (da * silu).astype(BF16)
            act_ref[k] = (silu * u).astype(BF16)
            dgu_ref[k] = dg
            dgu_ref[k + half] = du
            acc = acc + _dot(dg, w13_v[k]) + _dot(du, w13_v[k + half])
        if ln_bwd is not None:
            dzp, dg, db = _ln_bwd_tile(x_ref[...], f_ref[...], ln_bwd[2], g_ref[...], b_ref[...], acc)
            _ln_bwd_store(pl.program_id(0) == 0, dzp, dg, db, dzp_ref, dzpb_ref, dg_ref, db_ref)
        else:
            dh_ref[...] = acc

    any_spec = pl.BlockSpec(memory_space=pl.ANY)
    in_specs = [_row_spec(tm, d), pl.BlockSpec((N_DEV, tm, FF_CHUNK), lambda i: (0, i, 0)), any_spec, any_spec]
    mid_specs = [pl.BlockSpec((N_DEV, tm, FF_CHUNK), lambda i: (0, i, 0)), pl.BlockSpec((half, tm, FF_CHUNK), lambda i: (0, i, 0))]
    mid_shapes = [jax.ShapeDtypeStruct((N_DEV, s, FF_CHUNK), BF16), jax.ShapeDtypeStruct((half, s, FF_CHUNK), BF16)]
    args = (dz, gu, w13c, w2)
    if ln_bwd is not None:
        ln_in, ln_out = _ln_bwd_specs(tm, d)
        in_specs, out_specs = in_specs + ln_in, ln_out[:2] + mid_specs + ln_out[2:]
        shapes = _ln_bwd_shapes(s, d)
        out_shape = shapes[:2] + mid_shapes + shapes[2:]
        args += (ln_bwd[0], ln_bwd[1], ln_bwd[3], ln_bwd[4])
    else:
        out_specs, out_shape = [_row_spec(tm, d)] + mid_specs, [jax.ShapeDtypeStruct((s, d), F32)] + mid_shapes
    return _pallas_with_job(
        body, job, lambda: pl.program_id(0), s // tm, grid=(s // tm,), in_specs=in_specs, out_specs=out_specs,
        out_shape=out_shape,
        scratch_shapes=[pltpu.VMEM(w13c.shape, BF16), pltpu.VMEM(w2.shape, BF16), pltpu.SemaphoreType.DMA((2,))],
        compiler_params=_cparams(1, VMEM_LIMIT_MB), name=name, args=args)


def _shift_down(x, prev8, d):
    t, c = x.shape
    row = lax.broadcasted_iota(jnp.int32, (t, c), 0)
    xr = pltpu.roll(x, d, 0)
    pr = pltpu.roll(prev8, d, 0)
    if t > 8:
        pr = jnp.concatenate([pr, jnp.zeros((t - 8, c), x.dtype)], axis=0)
    return jnp.where(row < d, pr, xr)


def _shift_up(x, next8, d):
    t, c = x.shape
    row = lax.broadcasted_iota(jnp.int32, (t, c), 0)
    xr = pltpu.roll(x, t - d, 0)
    nr = pltpu.roll(next8, 8 - d, 0)
    if t > 8:
        nr = jnp.concatenate([jnp.zeros((t - 8, c), x.dtype), nr], axis=0)
    return jnp.where(row >= t - d, nr, xr)


def _scan_fwd(a, u, carry):
    t = a.shape[0]
    row = lax.broadcasted_iota(jnp.int32, a.shape, 0) % 8
    for d in (1, 2, 4):
        a_s, u_s = pltpu.roll(a, d, 0), pltpu.roll(u, d, 0)
        m = row >= d
        u = jnp.where(m, u + a * u_s, u)
        a = jnp.where(m, a * a_s, a)
    out = []
    for g in range(t // 8):
        hg = u[8 * g:8 * g + 8, :] + a[8 * g:8 * g + 8, :] * carry
        out.append(hg)
        carry = hg[7:8, :]
    return jnp.concatenate(out, axis=0)


def _scan_bwd(b, x, carry):
    t = b.shape[0]
    row = lax.broadcasted_iota(jnp.int32, b.shape, 0) % 8
    for d in (1, 2, 4):
        b_s, x_s = pltpu.roll(b, t - d, 0), pltpu.roll(x, t - d, 0)
        m = row < 8 - d
        x = jnp.where(m, x + b * x_s, x)
        b = jnp.where(m, b * b_s, b)
    out = [None] * (t // 8)
    for g in reversed(range(t // 8)):
        lg = x[8 * g:8 * g + 8, :] + b[8 * g:8 * g + 8, :] * carry
        out[g] = lg
        carry = lg[0:1, :]
    return jnp.concatenate(out, axis=0)


def _lru_elem(xc, pr, pi, b_r, b_i, lam):
    r = jax.nn.sigmoid(pr + b_r)
    ig = jax.nn.sigmoid(pi + b_i)
    softplus_neg_lam = jnp.maximum(-lam, 0.0) + jnp.log1p(jnp.exp(-jnp.abs(lam)))
    log_a = (-LRU_C) * r * softplus_neg_lam
    a = jnp.exp(log_a)
    one_minus_a2 = -jnp.tanh(log_a) * (jnp.exp(2.0 * log_a) + 1.0)
    u = jnp.sqrt(one_minus_a2) * (ig * xc)
    return a, u


def _gelu_tanh(x):
    return 0.5 * x * (1.0 + jnp.tanh(math.sqrt(2.0 / math.pi) * (x + 0.044715 * (x * x * x))))


def _conv_fwd(xb, prev8, w, b):
    out = b + w[3:4, :] * xb
    for d in range(1, CONV_W):
        out = out + w[3 - d:4 - d, :] * _shift_down(xb, prev8, d)
    return out


def _lru_fwd(xbg, conv_w, conv_b, w_rg, b_rg, w_ig, b_ig, lam, *, name):
    s = xbg.shape[0]
    c = LRU_W
    t = min(512, s)
    t8 = t // 8

    def body(xb_ref, gate_ref, prev_ref, cw_ref, cb_ref, wr_ref, br_ref, wi_ref, bi_ref, lam_ref,
             xc_ref, h_ref, y_ref, carry_ref):
        i = pl.program_id(0)

        @pl.when(i == 0)
        def _():
            carry_ref[...] = jnp.zeros_like(carry_ref)

        prev8 = jnp.where(i > 0, prev_ref[...], 0.0)
        xc = _conv_fwd(xb_ref[...], prev8, cw_ref[...], cb_ref[...])
        xcb = xc.astype(BF16)
        a, u = _lru_elem(xc, _dot(xcb, wr_ref[...]), _dot(xcb, wi_ref[...]), br_ref[...], bi_ref[...], lam_ref[...])
        h = _scan_fwd(a, u, carry_ref[0:1, :])
        carry_ref[...] = jnp.broadcast_to(h[t - 1:t, :], carry_ref.shape)
        xc_ref[...] = xc
        h_ref[...] = h
        y_ref[...] = h * _gelu_tanh(gate_ref[...])

    tile = lambda col: pl.BlockSpec((t, c), lambda i: (i, col))
    par = lambda rows: pl.BlockSpec((rows, c), lambda i: (0, 0))
    out_spec = pl.BlockSpec((t, c), lambda i: (i, 0))
    return pl.pallas_call(
        body, grid=(s // t,),
        in_specs=[tile(0), tile(1), pl.BlockSpec((8, c), lambda i: (jnp.maximum(i * t8 - 1, 0), 0)),
                  par(CONV_W), par(1), pl.BlockSpec((c, c), lambda i: (0, 0)), par(1),
                  pl.BlockSpec((c, c), lambda i: (0, 0)), par(1), par(1)],
        out_specs=[out_spec, out_spec, out_spec],
        out_shape=[jax.ShapeDtypeStruct((s, c), F32)] * 3,
        scratch_shapes=[pltpu.VMEM((8, c), F32)],
        compiler_params=_cparams(1, VMEM_LIMIT_MB), name=name)(
            xbg, xbg, xbg, conv_w, conv_b, w_rg, b_rg, w_ig, b_ig, lam)


def _lru_bwd(xc, xbg, hseq, dy, w_rg, b_rg, w_ig, b_ig, lam, *, name):
    s, c = xc.shape
    t = min(512, s)
    t8 = t // 8
    nt = s // t

    def body(xc_ref, gate_ref, h_ref, hprev_ref, dy_ref, wr_ref, br_ref, wi_ref, bi_ref, lam_ref,
             dxc_ref, dgate_ref, dwr_ref, dwi_ref, dbr_ref, dbi_ref, dlam_ref, a8_ref, l8_ref):
        i = pl.program_id(0)

        @pl.when(i == 0)
        def _():
            for r in (dwr_ref, dwi_ref, dbr_ref, dbi_ref, dlam_ref, a8_ref, l8_ref):
                r[...] = jnp.zeros_like(r)

        xc_v = xc_ref[...]
        xcb = xc_v.astype(BF16)
        (a, _), vjp = jax.vjp(_lru_elem, xc_v, _dot(xcb, wr_ref[...]), _dot(xcb, wi_ref[...]),
                              br_ref[...], bi_ref[...], lam_ref[...])
        gl, vjp_gelu = jax.vjp(_gelu_tanh, gate_ref[...])
        dyv, hs = dy_ref[...], h_ref[...]
        (dgate,) = vjp_gelu(dyv * hs)
        b_next = _shift_up(a, a8_ref[...], 1)
        lt = _scan_bwd(b_next, dyv * gl, l8_ref[0:1, :])
        a8_ref[...] = a[0:8, :]
        l8_ref[...] = lt[0:8, :]
        hprev8 = jnp.where(i < nt - 1, hprev_ref[...], 0.0)
        da = lt * _shift_down(hs, hprev8, 1)
        dxc_e, dpr, dpi, dbr, dbi, dlam = vjp((da, lt))
        dprb, dpib = dpr.astype(BF16), dpi.astype(BF16)
        dxc_ref[...] = dxc_e + _dot(dprb, wr_ref[...], NT) + _dot(dpib, wi_ref[...], NT)
        dgate_ref[...] = dgate.astype(BF16)
        xct = xc_v.T.astype(BF16)
        dwr_ref[...] += _dot(xct, dprb)
        dwi_ref[...] += _dot(xct, dpib)
        dbr_ref[...] += dbr
        dbi_ref[...] += dbi
        dlam_ref[...] += dlam

    rev = lambda col: pl.BlockSpec((t, c), lambda i: (nt - 1 - i, col))
    par = pl.BlockSpec((1, c), lambda i: (0, 0))
    sq = pl.BlockSpec((c, c), lambda i: (0, 0))
    return pl.pallas_call(
        body, grid=(nt,),
        in_specs=[rev(0), rev(1), rev(0), pl.BlockSpec((8, c), lambda i: (jnp.maximum((nt - 1 - i) * t8 - 1, 0), 0)),
                  rev(0), sq, par, sq, par, par],
        out_specs=[rev(0), rev(0), sq, sq, par, par, par],
        out_shape=[jax.ShapeDtypeStruct((s, c), F32), jax.ShapeDtypeStruct((s, c), BF16)]
                  + [jax.ShapeDtypeStruct((c, c), F32)] * 2 + [jax.ShapeDtypeStruct((1, c), F32)] * 3,
        scratch_shapes=[pltpu.VMEM((8, c), F32), pltpu.VMEM((8, c), F32)],
        compiler_params=_cparams(1, VMEM_LIMIT_MB), name=name)(
            xc, xbg, hseq, hseq, dy, w_rg, b_rg, w_ig, b_ig, lam)


def _conv_bwd(dxc, xbg, conv_w, *, name):
    s, c = dxc.shape
    t = min(512, s)
    t8 = t // 8
    nt = s // t

    def body(dxc_ref, next_ref, xb_ref, prev_ref, w_ref, dxb_ref, dw_ref, db_ref):
        i = pl.program_id(0)

        @pl.when(i == 0)
        def _():
            dw_ref[...] = jnp.zeros_like(dw_ref)
            db_ref[...] = jnp.zeros_like(db_ref)

        g = dxc_ref[...]
        xb = xb_ref[...]
        w = w_ref[...]
        next8 = jnp.where(i < nt - 1, next_ref[...], 0.0)
        prev8 = jnp.where(i > 0, prev_ref[...], 0.0)
        dxb = w[3:4, :] * g
        dw_ref[3:4, :] += jnp.sum(g * xb, axis=0, keepdims=True)
        for d in range(1, CONV_W):
            dxb = dxb + w[3 - d:4 - d, :] * _shift_up(g, next8, d)
            dw_ref[3 - d:4 - d, :] += jnp.sum(g * _shift_down(xb, prev8, d), axis=0, keepdims=True)
        dxb_ref[...] = dxb.astype(BF16)
        db_ref[...] += jnp.sum(g, axis=0, keepdims=True)

    tile = pl.BlockSpec((t, c), lambda i: (i, 0))
    return pl.pallas_call(
        body, grid=(nt,),
        in_specs=[tile, pl.BlockSpec((8, c), lambda i: (jnp.minimum((i + 1) * t8, s // 8 - 1), 0)),
                  tile, pl.BlockSpec((8, c), lambda i: (jnp.maximum(i * t8 - 1, 0), 0)),
                  pl.BlockSpec((CONV_W, c), lambda i: (0, 0))],
        out_specs=[tile, pl.BlockSpec((CONV_W, c), lambda i: (0, 0)), pl.BlockSpec((1, c), lambda i: (0, 0))],
        out_shape=[jax.ShapeDtypeStruct((s, c), BF16), jax.ShapeDtypeStruct((CONV_W, c), F32),
                   jax.ShapeDtypeStruct((1, c), F32)],
        compiler_params=_cparams(1), name=name)(dxc, dxc, xbg, xbg, conv_w)


def _hi_lo_dot(x, tri):
    hi = x.astype(BF16)
    lo = (x - hi.astype(F32)).astype(BF16)
    return _dot(hi, tri) + _dot(lo, tri)


def _tri(tk, inclusive):
    r = lax.broadcasted_iota(jnp.int32, (tk, tk), 0)
    c = lax.broadcasted_iota(jnp.int32, (tk, tk), 1)
    return ((r >= c) if inclusive else (r > c)).astype(BF16)


def _split_heads(x):
    first = lax.broadcasted_iota(jnp.int32, x.shape, 1) < SB_HEAD
    zero = jnp.zeros_like(x)
    return first, (jnp.where(first, x, zero), jnp.where(first, zero, x))


def _sb_softplus_terms(z):
    ls = jnp.minimum(z, 0.0) - jnp.log(1.0 + jnp.exp(-jnp.abs(z)))
    return ls, ls - z


def _sb_alive(runs):
    m = runs[0]
    for r in runs[1:]:
        m = jnp.maximum(m, r)
    return jnp.max(m) > SB_SKIP


def _sb_fwd(q, kv, *, name, job=None):
    s = q.shape[0]
    tb = min(256, s)
    rc = min(SB_ROWS, tb)
    npair = SB_W // SB_PAIR
    chains = [(h, c) for h in range(2) for c in range(tb // rc)]

    def body(q_ref, k_ref, v_ref, o_ref):
        i = pl.program_id(1)
        first, qs = _split_heads(q_ref[...])
        tri = _tri(tb, False)
        causal = lax.broadcasted_iota(jnp.int32, (tb, tb), 1) < lax.broadcasted_iota(jnp.int32, (tb, tb), 0)
        causal_c = [causal[c * rc:(c + 1) * rc, :] for c in range(tb // rc)]

        def group(j_lo, width, carry, diag_last):
            k0 = pl.multiple_of(j_lo * tb, tb)
            kb = k_ref[pl.ds(k0, width * tb), :]
            vb = v_ref[pl.ds(k0, width * tb), :]
            zw = [_dot(qs[h][c * rc:(c + 1) * rc, :], kb, NT) for h, c in chains]
            cols = [slice(t * tb, (t + 1) * tb) for t in range(width)]
            terms = [[_sb_softplus_terms(z[:, cl]) for cl in cols] for z in zw]
            lss = [[ls for ls, _ in tc] for tc in terms]
            ns = [[n for _, n in tc] for tc in terms]
            if diag_last:
                ns = [nc[:-1] + [jnp.where(causal_c[c], nc[-1], 0.0)] for nc, (h, c) in zip(ns, chains)]
            sufs = [[_hi_lo_dot(n, tri) for n in nc] for nc in ns]
            out = []
            for (h, c), lc, nc, sc, (run, acc) in zip(chains, lss, ns, sufs, carry):
                ws = [None] * width
                for t in reversed(range(width)):
                    w = jnp.exp(lc[t] + (sc[t] + run))
                    if diag_last and t == width - 1:
                        w = jnp.where(causal_c[c], w, 0.0)
                    ws[t] = w.astype(BF16)
                    run = run + jnp.sum(nc[t], axis=1, keepdims=True)
                out.append((run, acc + _dot(ws[0] if width == 1 else jnp.concatenate(ws, axis=1), vb)))
            return tuple(out)

        zero = (jnp.zeros((rc, 1), F32), jnp.zeros((rc, SB_PAIR), F32))
        carry = lax.cond(i == 0, lambda c: group(0, 1, c, True), lambda c: group(i - 1, 2, c, True),
                         (zero,) * len(chains))

        def step(st):
            j, _, c = st
            c = group(j, 1, c, False)
            return j - 1, _sb_alive([run for run, _ in c]), c

        _, _, carry = lax.while_loop(lambda st: jnp.logical_and(st[0] >= 0, st[1]), step,
                                     (i - 2, _sb_alive([run for run, _ in carry]), carry))
        accs = [jnp.concatenate([carry[n][1] for n, (h, c) in enumerate(chains) if h == hh], axis=0) for hh in range(2)]
        o_ref[...] = jnp.where(first, accs[0], accs[1])

    (o,), job_out = _pallas_with_job(
        body, job, lambda: pl.program_id(0) * (s // tb) + pl.program_id(1), npair * (s // tb),
        grid=(npair, s // tb),
        in_specs=[pl.BlockSpec((tb, SB_PAIR), lambda p, i: (i, p)),
                  pl.BlockSpec((s, SB_PAIR), lambda p, i: (0, p)),
                  pl.BlockSpec((s, SB_PAIR), lambda p, i: (0, npair + p))],
        out_specs=[pl.BlockSpec((tb, SB_PAIR), lambda p, i: (i, p))],
        out_shape=[jax.ShapeDtypeStruct((s, SB_W), F32)],
        compiler_params=_cparams(2, VMEM_LIMIT_MB), name=name, args=(q, kv, kv))
    return o, job_out


def _sb_bwd(q, kv, o, do, *, name, job=None):
    s = q.shape[0]
    tb = min(256, s)
    rc = min(SB_ROWS, tb)
    npair = SB_W // SB_PAIR
    chains = [(h, c) for h in range(2) for c in range(tb // rc)]

    def body(q_ref, k_ref, v_ref, o_ref, do_ref, dq_ref, dkb_ref, dvb_ref, dk_ref, dv_ref):
        i = pl.program_id(1)

        @pl.when(i == 0)
        def _():
            dk_ref[...] = jnp.zeros_like(dk_ref)
            dv_ref[...] = jnp.zeros_like(dv_ref)

        first, qs = _split_heads(q_ref[...])
        dob = do_ref[...].astype(BF16)
        _, dos = _split_heads(dob)
        prod = dob.astype(F32) * o_ref[...]
        deltas = (jnp.sum(jnp.where(first, prod, 0.0), axis=1, keepdims=True),
                  jnp.sum(jnp.where(first, 0.0, prod), axis=1, keepdims=True))
        tri_x, tri_i = _tri(tb, False), _tri(tb, True)
        causal = lax.broadcasted_iota(jnp.int32, (tb, tb), 1) < lax.broadcasted_iota(jnp.int32, (tb, tb), 0)
        causal_c = [causal[c * rc:(c + 1) * rc, :] for c in range(tb // rc)]
        rows = [slice(c * rc, (c + 1) * rc) for c in range(tb // rc)]

        def group(j_lo, width, carry, diag_last):
            k0 = pl.multiple_of(j_lo * tb, tb)
            kb = k_ref[pl.ds(k0, width * tb), :]
            vb = v_ref[pl.ds(k0, width * tb), :]
            cols = [slice(t * tb, (t + 1) * tb) for t in range(width)]
            zw = [_dot(qs[h][rows[c], :], kb, NT) for h, c in chains]
            dww = [_dot(dos[h][rows[c], :], vb, NT) for h, c in chains]
            terms = [[_sb_softplus_terms(z[:, cl]) for cl in cols] for z in zw]
            lss = [[ls for ls, _ in tc] for tc in terms]
            ns = [[n for _, n in tc] for tc in terms]
            if diag_last:
                ns = [nc[:-1] + [jnp.where(causal_c[c], nc[-1], 0.0)] for nc, (h, c) in zip(ns, chains)]
            sufs = [[_hi_lo_dot(n, tri_x) for n in nc] for nc in ns]
            wbs, gs, runs_n = [], [], []
            for (h, c), lc, nc, sc, dw, (run_n, _, _) in zip(chains, lss, ns, sufs, dww, carry):
                wb, g = [None] * width, [None] * width
                for t in reversed(range(width)):
                    w = jnp.exp(lc[t] + (sc[t] + run_n))
                    if diag_last and t == width - 1:
                        w = jnp.where(causal_c[c], w, 0.0)
                    wb[t] = w.astype(BF16)
                    g[t] = wb[t].astype(F32) * dw[:, cols[t]]
                    run_n = run_n + jnp.sum(nc[t], axis=1, keepdims=True)
                wbs.append(wb)
                gs.append(g)
                runs_n.append(run_n)
            gsufs = [[_hi_lo_dot(g, tri_i) for g in gc] for gc in gs]
            out = []
            dk_t = [jnp.zeros((tb, SB_PAIR), F32) for _ in range(width)]
            dv_t = [jnp.zeros((tb, SB_PAIR), F32) for _ in range(width)]
            for (h, c), lc, gc, gsc, wb, run_n, (_, run_g, dq_acc) in zip(chains, lss, gs, gsufs, wbs, runs_n, carry):
                qh, doh, delta = qs[h][rows[c], :], dos[h][rows[c], :], deltas[h][rows[c], :]
                dzb = [None] * width
                for t in reversed(range(width)):
                    pre = delta - (gsc[t] + run_g)
                    dz = gc[t] - jnp.exp(lc[t]) * (gc[t] + pre)
                    if diag_last and t == width - 1:
                        dz = jnp.where(causal_c[c], dz, 0.0)
                    run_g = run_g + jnp.sum(gc[t], axis=1, keepdims=True)
                    dzb[t] = dz.astype(BF16)
                    dk_t[t] = dk_t[t] + _dot(dz.T.astype(BF16), qh)
                    dv_t[t] = dv_t[t] + _dot(wb[t].astype(F32).T.astype(BF16), doh)
                out.append((run_n, run_g, dq_acc + _dot(dzb[0] if width == 1 else jnp.concatenate(dzb, axis=1), kb)))
            for t in range(width):
                dk_ref[pl.ds(pl.multiple_of((j_lo + t) * tb, tb), tb), :] += dk_t[t]
                dv_ref[pl.ds(pl.multiple_of((j_lo + t) * tb, tb), tb), :] += dv_t[t]
            return tuple(out)

        zero = (jnp.zeros((rc, 1), F32), jnp.zeros((rc, 1), F32), jnp.zeros((rc, SB_PAIR), F32))
        carry = lax.cond(i == 0, lambda c: group(0, 1, c, True), lambda c: group(i - 1, 2, c, True),
                         (zero,) * len(chains))

        def step(st):
            j, _, c = st
            c = group(j, 1, c, False)
            return j - 1, _sb_alive([r[0] for r in c]), c

        _, _, carry = lax.while_loop(lambda st: jnp.logical_and(st[0] >= 0, st[1]), step,
                                     (i - 2, _sb_alive([r[0] for r in carry]), carry))
        dqs = [jnp.concatenate([carry[n][2] for n, (h, c) in enumerate(chains) if h == hh], axis=0) for hh in range(2)]
        dq_ref[...] = (jnp.where(first, dqs[0], dqs[1]) * (1.0 / math.sqrt(SB_HEAD))).astype(BF16)

        @pl.when(i == s // tb - 1)
        def _():
            dkb_ref[...] = dk_ref[...].astype(BF16)
            dvb_ref[...] = dv_ref[...].astype(BF16)

    qtile = pl.BlockSpec((tb, SB_PAIR), lambda p, i: (i, p))
    col = pl.BlockSpec((s, SB_PAIR), lambda p, i: (0, p))
    return _pallas_with_job(
        body, job, lambda: pl.program_id(0) * (s // tb) + pl.program_id(1), npair * (s // tb),
        grid=(npair, s // tb),
        in_specs=[qtile, col, pl.BlockSpec((s, SB_PAIR), lambda p, i: (0, npair + p)), qtile, qtile],
        out_specs=[qtile, col, col],
        out_shape=[jax.ShapeDtypeStruct((s, SB_W), BF16)] * 3,
        scratch_shapes=[pltpu.VMEM((s, SB_PAIR), F32)] * 2,
        compiler_params=_cparams(2, VMEM_LIMIT_MB), name=name, args=(q, kv, kv, o, do))


def _mixnorm_math(yl, ys, gl, gs):
    def rms(x, g):
        return x * lax.rsqrt(jnp.mean(x * x, axis=-1, keepdims=True) + RMS_EPS) * g
    return rms(yl, gl), rms(ys, gs)


def _mixnorm_fwd(yl, ys, gl, gs, *, name):
    s, c = yl.shape
    tm = min(512, s)

    def body(yl_ref, ys_ref, gl_ref, gs_ref, y_ref):
        a, b = _mixnorm_math(yl_ref[...], ys_ref[...], gl_ref[...], gs_ref[...])
        y_ref[:, 0:c] = a.astype(BF16)
        y_ref[:, c:2 * c] = b.astype(BF16)

    return pl.pallas_call(
        body, grid=(s // tm,), in_specs=[_row_spec(tm, c), _row_spec(tm, c), _par_spec(c), _par_spec(c)],
        out_specs=_row_spec(tm, 2 * c), out_shape=jax.ShapeDtypeStruct((s, 2 * c), BF16),
        compiler_params=_cparams(1), name=name)(yl, ys, gl, gs)


def _cross_probs(qh, kh):
    sc = _dot(qh, kh, NT) * (1.0 / math.sqrt(MEM_HD))
    e = jnp.exp(sc - jnp.max(sc, axis=-1, keepdims=True))
    return e / jnp.sum(e, axis=-1, keepdims=True)


def _cross_fwd(q, kv, *, name):
    s, d = q.shape
    mlen = kv.shape[1]
    tm = min(512, s)

    def body(q_ref, kv_ref, o_ref):
        for h in range(MEM_HEADS):
            cols = slice(h * MEM_HD, (h + 1) * MEM_HD)
            p = _cross_probs(q_ref[:, cols].astype(BF16), kv_ref[h].astype(BF16))
            o_ref[:, cols] = _dot(p.astype(BF16), kv_ref[MEM_HEADS + h].astype(BF16)).astype(BF16)

    return pl.pallas_call(
        body, grid=(s // tm,),
        in_specs=[_row_spec(tm, d), pl.BlockSpec((2 * MEM_HEADS, mlen, MEM_HD), lambda i: (0, 0, 0))],
        out_specs=_row_spec(tm, d), out_shape=jax.ShapeDtypeStruct((s, d), BF16),
        compiler_params=_cparams(1, VMEM_LIMIT_MB), name=name)(q, kv)


def _cross_bwd(q, kv, do, *, name):
    s, d = q.shape
    mlen = kv.shape[1]
    tm = min(512, s)

    def body(q_ref, kv_ref, do_ref, dq_ref, dkv_ref):
        @pl.when(pl.program_id(0) == 0)
        def _():
            dkv_ref[...] = jnp.zeros_like(dkv_ref)

        for h in range(MEM_HEADS):
            cols = slice(h * MEM_HD, (h + 1) * MEM_HD)
            qh = q_ref[:, cols].astype(BF16)
            kh = kv_ref[h].astype(BF16)
            doh = do_ref[:, cols].astype(BF16)
            p = _cross_probs(qh, kh)
            dp = _dot(doh, kv_ref[MEM_HEADS + h].astype(BF16), NT)
            ds = p * (dp - jnp.sum(dp * p, axis=-1, keepdims=True)) * (1.0 / math.sqrt(MEM_HD))
            dq_ref[:, cols] = _dot(ds.astype(BF16), kh).astype(BF16)
            dkv_ref[h] += _dot(ds.T.astype(BF16), qh)
            dkv_ref[MEM_HEADS + h] += _dot(p.T.astype(BF16), doh)

    kv_spec = pl.BlockSpec((2 * MEM_HEADS, mlen, MEM_HD), lambda i: (0, 0, 0))
    return pl.pallas_call(
        body, grid=(s // tm,), in_specs=[_row_spec(tm, d), kv_spec, _row_spec(tm, d)],
        out_specs=[_row_spec(tm, d), kv_spec],
        out_shape=[jax.ShapeDtypeStruct((s, d), BF16), jax.ShapeDtypeStruct(kv.shape, F32)],
        compiler_params=_cparams(1, VMEM_LIMIT_MB), name=name)(q, kv, do)


def _proj_bwd_dx(parts, w_in, dz, *, name, job=None, ln_bwd=None):
    s = dz.shape[0]
    d = D_MODEL
    tm = min(256, s)
    widths = [p.shape[1] for p in parts]

    n_parts = len(parts)

    def body(*refs):
        part_refs, w_ref, dz_ref = refs[:n_parts], refs[n_parts], refs[n_parts + 1]
        acc = ALPHA * dz_ref[...]
        off = 0
        for p_ref, wd in zip(part_refs, widths):
            acc = acc + _dot(p_ref[...].astype(BF16), w_ref[off:off + wd, :])
            off += wd
        if ln_bwd is not None:
            x_ref, f_ref, g_ref, b_ref = refs[n_parts + 2:n_parts + 6]
            dzp, dg, db = _ln_bwd_tile(x_ref[...], f_ref[...], ln_bwd[2], g_ref[...], b_ref[...], acc)
            _ln_bwd_store(pl.program_id(0) == 0, dzp, dg, db, *refs[n_parts + 6:])
        else:
            refs[-1][...] = acc

    in_specs = [_row_spec(tm, wd) for wd in widths] + [pl.BlockSpec(w_in.shape, lambda i: (0, 0)), _row_spec(tm, d)]
    args = (*parts, w_in, dz)
    if ln_bwd is not None:
        ln_in, out_specs = _ln_bwd_specs(tm, d)
        in_specs, out_shape = in_specs + ln_in, _ln_bwd_shapes(s, d)
        args += (ln_bwd[0], ln_bwd[1], ln_bwd[3], ln_bwd[4])
    else:
        out_specs, out_shape = [_row_spec(tm, d)], [jax.ShapeDtypeStruct((s, d), F32)]
    outs, job_out = _pallas_with_job(
        body, job, lambda: pl.program_id(0), s // tm, grid=(s // tm,), in_specs=in_specs, out_specs=out_specs,
        out_shape=out_shape, compiler_params=_cparams(1, VMEM_LIMIT_MB), name=name, args=args)
    return (outs[0] if ln_bwd is None else outs), job_out


def _proj_fwd(hb, w_in_t, *, name):
    s, d = hb.shape
    tm = min(512, s)
    lru, sb = 2 * LRU_W, SB_W

    def body(h_ref, w_ref, xbg_ref, q_ref, kv_ref):
        h = h_ref[...]
        xbg_ref[...] = _dot(h, w_ref[0:lru, :], NT)
        q_ref[...] = (_dot(h, w_ref[lru:lru + sb, :], NT) * (1.0 / math.sqrt(SB_HEAD))).astype(BF16)
        kv_ref[...] = _dot(h, w_ref[lru + sb:lru + 3 * sb, :], NT).astype(BF16)

    return pl.pallas_call(
        body, grid=(s // tm,), in_specs=[_row_spec(tm, d), pl.BlockSpec(w_in_t.shape, lambda i: (0, 0))],
        out_specs=[_row_spec(tm, lru), _row_spec(tm, sb), _row_spec(tm, 2 * sb)],
        out_shape=[jax.ShapeDtypeStruct((s, lru), F32), jax.ShapeDtypeStruct((s, sb), BF16),
                   jax.ShapeDtypeStruct((s, 2 * sb), BF16)],
        compiler_params=_cparams(1, VMEM_LIMIT_MB), name=name)(hb, w_in_t)


def _mix_bwd(dzb, w_out, yl, ys, gl, gs, *, name, job=None):
    s, c = yl.shape
    tm = min(512, s)

    def body(dz_ref, w_ref, yl_ref, ys_ref, gl_ref, gs_ref, dyl_ref, dys_ref, dgl_ref, dgs_ref):
        @pl.when(pl.program_id(0) == 0)
        def _():
            dgl_ref[...] = jnp.zeros_like(dgl_ref)
            dgs_ref[...] = jnp.zeros_like(dgs_ref)

        dy = _dot(dz_ref[...], w_ref[...], NT)
        _, vjp = jax.vjp(_mixnorm_math, yl_ref[...], ys_ref[...], gl_ref[...], gs_ref[...])
        dyl, dys, dgl, dgs = vjp((dy[:, 0:c], dy[:, c:2 * c]))
        dyl_ref[...] = dyl
        dys_ref[...] = dys.astype(BF16)
        dgl_ref[...] += dgl
        dgs_ref[...] += dgs

    return _pallas_with_job(
        body, job, lambda: pl.program_id(0), s // tm, grid=(s // tm,),
        in_specs=[_row_spec(tm, 2 * c), pl.BlockSpec(w_out.shape, lambda i: (0, 0)), _row_spec(tm, c), _row_spec(tm, c),
                  _par_spec(c), _par_spec(c)],
        out_specs=[_row_spec(tm, c), _row_spec(tm, c), _par_spec(c), _par_spec(c)],
        out_shape=[jax.ShapeDtypeStruct((s, c), F32), jax.ShapeDtypeStruct((s, c), BF16)] + [jax.ShapeDtypeStruct((1, c), F32)] * 2,
        compiler_params=_cparams(1, VMEM_LIMIT_MB), name=name, args=(dzb, w_out, yl, ys, gl, gs))


def _proj_dw(h, parts, *, name):
    s, d = h.shape
    tk = min(2048, s)
    widths = [p.shape[1] for p in parts]

    def body(*refs):
        h_ref, part_refs, o_ref, acc_ref = refs[0], refs[1:-2], refs[-2], refs[-1]
        k = pl.program_id(0)

        @pl.when(k == 0)
        def _():
            acc_ref[...] = jnp.zeros_like(acc_ref)

        hv = h_ref[...].astype(BF16)
        off = 0
        for p_ref, wd in zip(part_refs, widths):
            acc_ref[off:off + wd, :] += _dot(p_ref[...].astype(F32).T.astype(BF16), hv)
            off += wd

        @pl.when(k == s // tk - 1)
        def _():
            o_ref[...] = acc_ref[...].astype(BF16)

    return pl.pallas_call(
        body, grid=(s // tk,), in_specs=[_row_spec(tk, d)] + [_row_spec(tk, wd) for wd in widths],
        out_specs=pl.BlockSpec((sum(widths), d), lambda k: (0, 0)),
        out_shape=jax.ShapeDtypeStruct((sum(widths), d), BF16),
        scratch_shapes=[pltpu.VMEM((sum(widths), d), F32)],
        compiler_params=_cparams(1, VMEM_LIMIT_MB), name=name)(h, *parts)


def _mesh_pos():
    return lax.axis_index("x"), lax.axis_index("y"), lax.axis_index("c")


class _GatherJob:
    n_phases = 3

    def __init__(self, shards):
        n = len(shards)
        self.inputs = list(shards)
        self.out_shape = [jax.ShapeDtypeStruct((N_DEV,) + a.shape, a.dtype) for a in shards]
        self.scratch = [pltpu.SemaphoreType.DMA((n, 7)), pltpu.SemaphoreType.DMA((n, 7)), pltpu.SemaphoreType.DMA((n,))]

    def phase(self, k, ins, outs, sems):
        send_sems, recv_sems, local_sems = sems
        n = len(ins)
        x, y, c = _mesh_pos()
        me, sibling = (x, y, c), (x, y, 1 - c)
        chips = [(1 - x, y), (x, 1 - y), (1 - x, 1 - y)]

        def copy(a, slot, block, to, src=None):
            dst = outs[a].at[4 * block[0] + 2 * block[1] + block[2]]
            return pltpu.make_async_remote_copy(
                src_ref=dst if src is None else src, dst_ref=dst,
                send_sem=send_sems.at[a, slot], recv_sem=recv_sems.at[a, slot], device_id=to, device_id_type=MESH)

        def mine():
            return [pltpu.make_async_copy(ins[a], outs[a].at[4 * x + 2 * y + c], local_sems.at[a]) for a in range(n)]

        def first():
            return [cp for a in range(n) for cp in
                    [copy(a, 0, me, sibling, src=ins[a])] + [copy(a, 1 + j, me, (*chip, c), src=ins[a]) for j, chip in enumerate(chips)]]

        def passed(a, j):
            return copy(a, 4 + j, (*chips[j], c), sibling)

        if k == 0:
            for cp in mine() + first():
                cp.start()
        elif k == 1:
            for a in range(n):
                for j, chip in enumerate(chips):
                    copy(a, 1 + j, (*chip, c), me).wait_recv()
                    passed(a, j).start()
        else:
            for a in range(n):
                copy(a, 0, sibling, me).wait_recv()
                for j, chip in enumerate(chips):
                    copy(a, 4 + j, (*chip, 1 - c), me).wait_recv()
            for cp in first() + [passed(a, j) for a in range(n) for j in range(3)]:
                cp.wait_send()
            for cp in mine():
                cp.wait()


class _SiblingJob:
    n_phases = 2

    def __init__(self, grads):
        n, half = len(grads), N_DEV // 2
        self.inputs = list(grads)
        self.out_shape = [jax.ShapeDtypeStruct((half,) + g.shape[1:], g.dtype) for g in grads]
        self.scratch = [pltpu.SemaphoreType.DMA((n, half)), pltpu.SemaphoreType.DMA((n, half))]

    def phase(self, k, ins, outs, sems):
        send_sems, recv_sems = sems
        x, y, c = _mesh_pos()
        copies = [pltpu.make_async_remote_copy(
            src_ref=ins[a].at[2 * j + (1 - c)], dst_ref=outs[a].at[j], send_sem=send_sems.at[a, j],
            recv_sem=recv_sems.at[a, j], device_id=(x, y, 1 - c), device_id_type=MESH)
            for a in range(len(ins)) for j in range(N_DEV // 2)]
        for cp in copies:
            if k == 0:
                cp.start()
            else:
                cp.wait()


class _ChipJob:
    n_phases = 2

    def __init__(self, partials):
        n, half = len(partials), N_DEV // 2
        self.inputs = list(partials)
        self.out_shape = [jax.ShapeDtypeStruct(p.shape, p.dtype) for p in partials]
        self.scratch = [pltpu.SemaphoreType.DMA((n, half)), pltpu.SemaphoreType.DMA((n, half)), pltpu.SemaphoreType.DMA((n,))]

    def phase(self, k, ins, outs, sems):
        send_sems, recv_sems, local_sems = sems
        n, half = len(ins), N_DEV // 2
        x, y, c = _mesh_pos()
        jme = 2 * x + y
        local = [pltpu.make_async_copy(ins[a].at[jme], outs[a].at[jme], local_sems.at[a]) for a in range(n)]
        for cp in local:
            if k == 0:
                cp.start()
            else:
                cp.wait()
        for j in range(half):
            @pl.when(jme != j)
            def _(j=j):
                for a in range(n):
                    if k == 0:
                        pltpu.make_async_remote_copy(
                            src_ref=ins[a].at[j], dst_ref=outs[a].at[jme], send_sem=send_sems.at[a, j],
                            recv_sem=recv_sems.at[a, jme], device_id=(j // 2, j % 2, c), device_id_type=MESH).start()
                    else:
                        arrive = pltpu.make_async_remote_copy(
                            src_ref=ins[a].at[j], dst_ref=outs[a].at[j], send_sem=send_sems.at[a, j],
                            recv_sem=recv_sems.at[a, j], device_id=(j // 2, j % 2, c), device_id_type=MESH)
                        arrive.wait_recv()
                        arrive.wait_send()


def _phase_steps(n_phases, n_steps):
    return [0, n_steps - 1] if n_phases == 2 else [0, n_steps // 2, n_steps - 1]


def _pallas_with_job(body, job, step_fn, n_steps, *, grid, in_specs, out_specs, out_shape, scratch_shapes=(),
                     compiler_params, name, args):
    in_specs, out_specs, out_shape, scratch_shapes = list(in_specs), list(out_specs), list(out_shape), list(scratch_shapes)
    if job is None:
        outs = pl.pallas_call(body, grid=grid, in_specs=in_specs, out_specs=out_specs, out_shape=out_shape,
                              scratch_shapes=scratch_shapes, compiler_params=compiler_params, name=name)(*args)
        return list(outs), None
    ni, no, ns = len(in_specs), len(out_specs), len(scratch_shapes)
    ji, jo = len(job.inputs), len(job.out_shape)
    steps = _phase_steps(job.n_phases, n_steps)

    def wrapped(*refs):
        own_in, job_in = refs[:ni], refs[ni:ni + ji]
        own_out, job_out = refs[ni + ji:ni + ji + no], refs[ni + ji + no:ni + ji + no + jo]
        own_scr, job_scr = refs[ni + ji + no + jo:ni + ji + no + jo + ns], refs[ni + ji + no + jo + ns:]
        step = step_fn()
        for k, at in enumerate(steps):
            @pl.when(step == at)
            def _(k=k):
                job.phase(k, job_in, job_out, job_scr)
        body(*own_in, *own_out, *own_scr)

    any_spec = pl.BlockSpec(memory_space=pl.ANY)
    outs = pl.pallas_call(
        wrapped, grid=grid, in_specs=in_specs + [any_spec] * ji, out_specs=out_specs + [any_spec] * jo,
        out_shape=out_shape + list(job.out_shape), scratch_shapes=scratch_shapes + list(job.scratch),
        compiler_params=compiler_params, name=name)(*args, *job.inputs)
    return list(outs[:no]), list(outs[no:])


def _run_job(job, *, name):
    ji, jo = len(job.inputs), len(job.out_shape)

    def body(*refs):
        for k in range(job.n_phases):
            job.phase(k, refs[:ji], refs[ji:ji + jo], refs[ji + jo:])

    any_spec = pl.BlockSpec(memory_space=pl.ANY)
    return pl.pallas_call(body, in_specs=[any_spec] * ji, out_specs=[any_spec] * jo, out_shape=list(job.out_shape),
                          scratch_shapes=list(job.scratch), name=name)(*job.inputs)


def _pair_add(grad, recv, cj_idx, *, name):
    half, r, cdim = recv.shape
    tr = _row_tile(r, 512)

    def body(cj_ref, g_ref, r_ref, own_ref, tb_ref):
        t = g_ref[...].astype(F32) + r_ref[...].astype(F32)
        tb_ref[...] = t.astype(BF16)

        @pl.when(pl.program_id(1) == cj_ref[1])
        def _():
            own_ref[...] = t

    grid_spec = pltpu.PrefetchScalarGridSpec(
        num_scalar_prefetch=1, grid=(r // tr, half),
        in_specs=[pl.BlockSpec((None, tr, cdim), lambda i, j, cj: (2 * j + cj[0], i, 0)),
                  pl.BlockSpec((None, tr, cdim), lambda i, j, cj: (j, i, 0))],
        out_specs=[pl.BlockSpec((tr, cdim), lambda i, j, cj: (i, 0)),
                   pl.BlockSpec((None, tr, cdim), lambda i, j, cj: (j, i, 0))])
    return pl.pallas_call(
        body, grid_spec=grid_spec,
        out_shape=[jax.ShapeDtypeStruct((r, cdim), F32), jax.ShapeDtypeStruct(recv.shape, BF16)],
        compiler_params=_cparams(2), name=name)(cj_idx, grad, recv)


def _adam_math(w, g, m, v):
    m = ADAM_B1 * m + (1.0 - ADAM_B1) * g
    v = ADAM_B2 * v + (1.0 - ADAM_B2) * (g * g)
    m_hat = m / (1.0 - ADAM_B1 ** ADAM_STEP)
    v_hat = v / (1.0 - ADAM_B2 ** ADAM_STEP)
    delta = -ADAM_LR * (m_hat / (jnp.sqrt(v_hat) + ADAM_EPS) + ADAM_WD * w)
    return delta, m, v


def _reduce_adam(own, recv, j_idx, w, m, v, *, name):
    half, r, cdim = recv.shape
    tr = _row_tile(r, 256)

    def body(j_ref, own_ref, recv_ref, w_ref, m_ref, v_ref, g_ref, d_ref, nm_ref, nv_ref):
        jme = j_ref[0]
        g = own_ref[...]
        for sl in range(half):
            g = g + jnp.where(sl == jme, 0.0, recv_ref[sl].astype(F32))
        delta, nm, nv = _adam_math(w_ref[...], g, m_ref[...], v_ref[...])
        g_ref[...] = g
        d_ref[...] = delta
        nm_ref[...] = nm
        nv_ref[...] = nv

    tile = pl.BlockSpec((tr, cdim), lambda i, j_ref: (i, 0))
    grid_spec = pltpu.PrefetchScalarGridSpec(
        num_scalar_prefetch=1, grid=(r // tr,),
        in_specs=[tile, pl.BlockSpec((half, tr, cdim), lambda i, j_ref: (0, i, 0)), tile, tile, tile],
        out_specs=[tile] * 4)
    return pl.pallas_call(
        body, grid_spec=grid_spec, out_shape=[jax.ShapeDtypeStruct((r, cdim), F32)] * 4,
        compiler_params=_cparams(1), name=name)(j_idx, own, recv, w, m, v)


class _SmallGatherJob:
    n_phases = 2

    def __init__(self, pack):
        self.inputs = [pack]
        self.out_shape = [jax.ShapeDtypeStruct((N_DEV,) + pack.shape, pack.dtype)]
        self.scratch = [pltpu.SemaphoreType.DMA((N_DEV,)), pltpu.SemaphoreType.DMA((N_DEV,)), pltpu.SemaphoreType.DMA(())]

    def phase(self, k, ins, outs, sems):
        (pack,), (land,), (send_sems, recv_sems, local_sem) = ins, outs, sems
        x, y, c = _mesh_pos()
        me = 4 * x + 2 * y + c
        mine = pltpu.make_async_copy(pack, land.at[me], local_sem)
        if k == 0:
            mine.start()
        else:
            mine.wait()
        for t in range(N_DEV):
            @pl.when(me != t)
            def _(t=t):
                peer = (t // 4, (t // 2) % 2, t % 2)
                if k == 0:
                    pltpu.make_async_remote_copy(
                        src_ref=pack, dst_ref=land.at[me], send_sem=send_sems.at[t], recv_sem=recv_sems.at[me],
                        device_id=peer, device_id_type=MESH).start()
                else:
                    cp = pltpu.make_async_remote_copy(
                        src_ref=pack, dst_ref=land.at[t], send_sem=send_sems.at[t], recv_sem=recv_sems.at[t],
                        device_id=peer, device_id_type=MESH)
                    cp.wait_recv()
                    cp.wait_send()


class _Jobs:
    def __init__(self, jobs):
        self.jobs = jobs
        self.n_phases = jobs[0].n_phases
        assert all(j.n_phases == self.n_phases for j in jobs)
        self.inputs = [a for j in jobs for a in j.inputs]
        self.out_shape = [o for j in jobs for o in j.out_shape]
        self.scratch = [sc for j in jobs for sc in j.scratch]

    def phase(self, k, ins, outs, sems):
        i = o = sc = 0
        for j in self.jobs:
            ni, no, nsc = len(j.inputs), len(j.out_shape), len(j.scratch)
            j.phase(k, ins[i:i + ni], outs[o:o + no], sems[sc:sc + nsc])
            i, o, sc = i + ni, o + no, sc + nsc


def _adam_small(w, land, m, v, *, name):
    def body(w_ref, land_ref, m_ref, v_ref, g_ref, d_ref, nm_ref, nv_ref):
        total = land_ref[0]
        for t in range(1, N_DEV):
            total = total + land_ref[t]
        g_ref[...] = total
        d_ref[...], nm_ref[...], nv_ref[...] = _adam_math(w_ref[...], total, m_ref[...], v_ref[...])

    vmem = pl.BlockSpec(memory_space=pltpu.VMEM)
    return pl.pallas_call(
        body, in_specs=[vmem] * 4, out_specs=[vmem] * 4, out_shape=[jax.ShapeDtypeStruct(w.shape, F32)] * 4,
        name=name)(w, land, m, v)


def _block_diag(w):
    h, a, b = w.shape
    eye = jnp.eye(h, dtype=w.dtype)
    return (eye[:, None, :, None] * w[:, :, None, :]).reshape(h * a, h * b)


def _diag_blocks(w, h):
    a = w.shape[0] // h
    return jnp.stack([w[i * a:(i + 1) * a, i * a:(i + 1) * a] for i in range(h)])


def _local_step(x, mem, target, p, hooks=None):
    g = {}
    p = dict(p)
    w_rg, w_ig = _block_diag(p["w_rgate"]).astype(BF16), _block_diag(p["w_igate"]).astype(BF16)

    (gu1, f1, xb, h1, h1b), arrived = _ffn_fwd(x, p["ffn1_w13"], p["ffn1_w2"], name="ffn1_fwd", ln=(p["ln1_g"], p["ln1_b"]),
                                               job=hooks and hooks.gather_job(_GATHER_MIXER))
    if hooks:
        p.update(hooks.gathered(_GATHER_MIXER, arrived))
    w_in = p["w_in"]
    xbg, q_sb, kv_sb = _proj_fwd(h1b, w_in, name="proj_fwd")
    xc, hseq, y_lru = _lru_fwd(xbg, p["conv_w"], p["conv_b"], w_rg, p["b_rgate"], w_ig, p["b_igate"],
                               p["lru_lambda"], name="lru_fwd")
    y_sb, arrived = _sb_fwd(q_sb, kv_sb, name="sb_fwd", job=hooks and hooks.gather_job(_GATHER_FFN2))
    if hooks:
        p.update(hooks.gathered(_GATHER_FFN2, arrived))
    ymix = _mixnorm_fwd(y_lru, y_sb, p["g_lru"], p["g_sb"], name="mixnorm_fwd")
    mix, h2, h2b = _matmul(ymix, p["w_out"], ln=(h1, 1.0, p["ln2_g"], p["ln2_b"]), name="mix_out_ln2")
    qm = _matmul(h2b, p["mem_wq"], out_dtype=BF16, name="cross_q")
    kv = _matmul(mem, p["mem_wkv"], name="cross_kv")
    o_cross = _cross_fwd(qm, kv, name="cross_fwd")
    cross, h3, h3b = _matmul(o_cross, p["mem_wo"], ln=(h2, 1.0, p["ln3_g"], p["ln3_b"]), name="cross_out_ln3")
    (gu2, dz4, dz4b, g["ln4_g"], g["ln4_b"], loss), _ = _ffn_fwd(
        h3b, p["ffn2_w13"], p["ffn2_w2"], loss_head=(h3, target, p["ln4_g"], p["ln4_b"]), name="ffn2_fwd_loss")
    (dz3, dz3b, dgu2, act2, g["ln3_g"], g["ln3_b"]), _ = _ffn_bwd(
        dz4, gu2, p["ffn2_w13"], p["ffn2_w2"], ln_bwd=(h2, cross, 1.0, p["ln3_g"], p["ln3_b"]), name="ffn2_bwd_ln3")
    g["ffn2_w13"] = _matmul(dgu2, h3b, trans_a=True, tm=FF_CHUNK, tn=1024, tk=2048, out_dtype=BF16, name="ffn2_dw13")
    g["ffn2_w2"] = _matmul(act2, dz4b, trans_a=True, tm=FF_CHUNK, tn=1024, tk=2048, scale=0.5, out_dtype=BF16, name="ffn2_dw2")
    g["mem_wo"] = _matmul(o_cross, dz3b, trans_a=True, tm=1024, tn=1024, tk=1024, out_dtype=BF16, name="cross_dwo")
    do_cross = _matmul(dz3b, p["mem_wo"], trans_b=True, out_dtype=BF16, name="cross_do")
    dqm, dkv = _cross_bwd(qm, kv, do_cross, name="cross_bwd")
    g["mem_wq"] = _matmul(h2b, dqm, trans_a=True, tm=1024, tn=1024, tk=1024, out_dtype=BF16, name="cross_dwq")
    g["mem_wkv"] = _matmul(mem, dkv, trans_a=True, tm=1024, tn=MEM_HD, tk=256, out_dtype=BF16, name="cross_dwkv")
    dz2, dz2b, g["ln2_g"], g["ln2_b"] = _matmul(dqm, p["mem_wq"], trans_b=True, res=dz3, res_coeff=ALPHA,
                                                ln_bwd=(h1, mix, 1.0, p["ln2_g"], p["ln2_b"]), name="cross_dh_ln2")
    g["w_out"] = _matmul(ymix, dz2b, trans_a=True, tm=1024, tn=1024, tk=1024, out_dtype=BF16, name="mix_dwout")
    (dy_lru, dy_sb, g["g_lru"], g["g_sb"]), arrived = _mix_bwd(
        dz2b, p["w_out"], y_lru, y_sb, p["g_lru"], p["g_sb"], name="mix_bwd", job=hooks and hooks.sibling_job(_REDUCE_EARLY, g))
    (dq, dk, dv), arrived = _sb_bwd(q_sb, kv_sb, y_sb, dy_sb, name="sb_bwd",
                                    job=hooks and hooks.chip_job(_REDUCE_EARLY, arrived))
    if hooks:
        hooks.reduced(_REDUCE_EARLY, arrived)
    dxc, dgate, dwr, dwi, g["b_rgate"], g["b_igate"], g["lru_lambda"] = _lru_bwd(
        xc, xbg, hseq, dy_lru, w_rg, p["b_rgate"], w_ig, p["b_igate"], p["lru_lambda"], name="lru_bwd")
    g["w_rgate"], g["w_igate"] = _diag_blocks(dwr, 8), _diag_blocks(dwi, 8)
    dxb, g["conv_w"], g["conv_b"] = _conv_bwd(dxc, xbg, p["conv_w"], name="conv_bwd")
    parts = [dxb, dgate, dq, dk, dv]
    g["w_in"] = _proj_dw(h1b, parts, name="proj_dw")
    (dz1, dz1b, g["ln1_g"], g["ln1_b"]), arrived = _proj_bwd_dx(
        parts, w_in, dz2, ln_bwd=(x, f1, 0.5, p["ln1_g"], p["ln1_b"]), name="proj_dh_ln1",
        job=hooks and hooks.sibling_job(['w_in'], g))
    (grad_x, dgu1, act1), _ = _ffn_bwd(dz1, gu1, p["ffn1_w13"], p["ffn1_w2"], name="ffn1_bwd")
    dw13 = dict(trans_a=True, tm=FF_CHUNK, tn=1024, tk=2048, out_dtype=BF16, name="ffn1_dw13")
    dw2 = dict(trans_a=True, tm=FF_CHUNK, tn=1024, tk=2048, scale=0.5, out_dtype=BF16, name="ffn1_dw2")
    if not hooks:
        g["ffn1_w13"] = _matmul(dgu1, xb, **dw13)
        g["ffn1_w2"] = _matmul(act1, dz1b, **dw2)
        return loss, grad_x, g
    g["ffn1_w13"], arrived = _matmul(dgu1, xb, job=_Jobs([hooks.chip_job(['w_in'], arrived), hooks.small_job(g, loss)]), **dw13)
    hooks.reduced(['w_in'], arrived[:1])
    hooks.small_land = arrived[1]
    arrived = _run_job(hooks.sibling_job(['ffn1_w13'], g), name="reduce_sibling_w13")
    g["ffn1_w2"], arrived = _matmul(act1, dz1b, job=hooks.chip_job(['ffn1_w13'], arrived), **dw2)
    hooks.reduced(['ffn1_w13'], arrived)
    arrived = _run_job(hooks.sibling_job(['ffn1_w2'], g), name="reduce_sibling_w2")
    hooks.reduced(['ffn1_w2'], _run_job(hooks.chip_job(['ffn1_w2'], arrived), name="reduce_chips_w2"))
    return loss, grad_x, g


_WEIGHTS = ['ffn1_w13', 'ffn1_w2', 'ln1_g', 'ln1_b', 'w_in', 'conv_w', 'conv_b', 'w_rgate', 'b_rgate', 'w_igate',
            'b_igate', 'lru_lambda', 'g_lru', 'g_sb', 'w_out', 'ln2_g', 'ln2_b', 'mem_wq', 'mem_wkv', 'mem_wo',
            'ln3_g', 'ln3_b', 'ffn2_w13', 'ffn2_w2', 'ln4_g', 'ln4_b']
_SHARDED = ['ffn1_w13', 'ffn1_w2', 'w_in', 'w_out', 'mem_wq', 'mem_wkv', 'mem_wo', 'ffn2_w13', 'ffn2_w2']
_SMALL = ['ln1_g', 'ln1_b', 'ln2_g', 'ln2_b', 'ln3_g', 'ln3_b', 'ln4_g', 'ln4_b', 'conv_b', 'b_rgate', 'b_igate',
          'lru_lambda', 'g_lru', 'g_sb', 'w_rgate', 'w_igate']


def _pack_small(d, conv_w_full=None, loss=None):
    rows = [d[n].reshape(-1) for n in _SMALL]
    if conv_w_full is not None:
        rows.append(conv_w_full.reshape(-1))
    if loss is not None:
        rows.append(loss.reshape(-1))
    flat = jnp.concatenate(rows)
    pad = (-flat.shape[0]) % (8 * D_MODEL)
    return jnp.pad(flat, (0, pad)).reshape(-1, D_MODEL)


def _unpack_small(pack, like, with_conv):
    flat = pack.reshape(-1)
    out, off = {}, 0
    for n in _SMALL:
        size = math.prod(like[n].shape)
        out[n] = flat[off:off + size].reshape(like[n].shape)
        off += size
    conv = flat[off:off + CONV_W * LRU_W].reshape(CONV_W, LRU_W) if with_conv else None
    return out, conv


_GATHER_FIRST = ['ffn1_w13', 'ffn1_w2']
_GATHER_MIXER = ['w_in', 'conv_w', 'w_out', 'mem_wq', 'mem_wkv', 'mem_wo']
_GATHER_FFN2 = ['ffn2_w13', 'ffn2_w2']
_REDUCE_EARLY = ['ffn2_w13', 'ffn2_w2', 'mem_wo', 'mem_wq', 'mem_wkv', 'w_out']


_TRANSPOSED = ('ffn1_w13', 'ffn2_w13', 'w_in')


def _own_layout(n, a):
    return a.T if n in _TRANSPOSED else a


def _weight_layout(n, full):
    if n in ('ffn1_w13', 'ffn2_w13', 'mem_wkv'):
        return full
    if n == 'conv_w':
        return full.transpose(1, 0, 2).reshape(CONV_W, LRU_W)
    return full.reshape(-1, full.shape[-1])


def _grad_blocks(n, g):
    return g.reshape((N_DEV, -1, g.shape[-1]))


class _Hooks:
    def __init__(self, w, cj_idx):
        self.w, self.cj_idx = w, cj_idx
        self.sums, self.recv = {}, {}

    def shard(self, n):
        return self.w[n][0] if n == 'conv_w' else _own_layout(n, self.w[n][0]).astype(BF16)

    def gather_job(self, names):
        return _GatherJob([self.shard(n) for n in names])

    def gathered(self, names, outs):
        return {n: _weight_layout(n, o) for n, o in zip(names, outs)}

    def sibling_job(self, names, g):
        self.blocks = [_grad_blocks(n, g[n]) for n in names]
        return _SiblingJob(self.blocks)

    def chip_job(self, names, from_sibling):
        for n, b, r in zip(names, self.blocks, from_sibling):
            self.sums[n] = _pair_add(b, r, self.cj_idx, name=f"pair_add_{n}")
        return _ChipJob([self.sums[n][1] for n in names])

    def reduced(self, names, from_chips):
        self.recv.update(zip(names, from_chips))

    def small_job(self, g, loss):
        return _SmallGatherJob(_pack_small(g, conv_w_full=g["conv_w"], loss=loss))


def kernel(x, mem, ffn1_w13, ffn1_w2, ln1_g, ln1_b, w_in, conv_w, conv_b, w_rgate, b_rgate, w_igate, b_igate, lru_lambda, g_lru, g_sb, w_out, ln2_g, ln2_b, mem_wq, mem_wkv, mem_wo, ln3_g, ln3_b, ffn2_w13, ffn2_w2, ln4_g, ln4_b, loss_target, m_ffn1_w13, m_ffn1_w2, m_ln1_g, m_ln1_b, m_w_in, m_conv_w, m_conv_b, m_w_rgate, m_b_rgate, m_w_igate, m_b_igate, m_lru_lambda, m_g_lru, m_g_sb, m_w_out, m_ln2_g, m_ln2_b, m_mem_wq, m_mem_wkv, m_mem_wo, m_ln3_g, m_ln3_b, m_ffn2_w13, m_ffn2_w2, m_ln4_g, m_ln4_b, v_ffn1_w13, v_ffn1_w2, v_ln1_g, v_ln1_b, v_w_in, v_conv_w, v_conv_b, v_w_rgate, v_b_rgate, v_w_igate, v_b_igate, v_lru_lambda, v_g_lru, v_g_sb, v_w_out, v_ln2_g, v_ln2_b, v_mem_wq, v_mem_wkv, v_mem_wo, v_ln3_g, v_ln3_b, v_ffn2_w13, v_ffn2_w2, v_ln4_g, v_ln4_b):
    args = locals()
    w = {n: args[n] for n in _WEIGHTS}
    mom = {n: args["m_" + n] for n in _WEIGHTS}
    var = {n: args["v_" + n] for n in _WEIGHTS}
    ix, iy, ic = lax.axis_index("x"), lax.axis_index("y"), lax.axis_index("c")

    j_idx = jnp.reshape(2 * ix + iy, (1,)).astype(jnp.int32)
    hooks = _Hooks(w, jnp.stack([ic, 2 * ix + iy]).astype(jnp.int32))
    p = {n: (w[n][0] if w[n].ndim == 4 else w[n]) for n in _WEIGHTS if n not in _SHARDED and n != 'conv_w'}
    p.update(hooks.gathered(_GATHER_FIRST, _run_job(hooks.gather_job(_GATHER_FIRST), name="gather_ffn1")))
    _, grad_x, g = _local_step(x[0], mem[0], loss_target[0], p, hooks)

    grads, delta, new_m, new_v = {}, {}, {}, {}
    for n in _SHARDED:
        grads[n], delta[n], new_m[n], new_v[n] = (
            _own_layout(n, o).reshape(w[n].shape) for o in _reduce_adam(
                hooks.sums[n][0], hooks.recv[n], j_idx, _own_layout(n, w[n][0]), _own_layout(n, mom[n][0]),
                _own_layout(n, var[n][0]), name=f"adam_{n}"))

    me = 4 * ix + 2 * iy + ic
    own_cols = lambda full: lax.dynamic_slice_in_dim(full, me * (LRU_W // N_DEV), LRU_W // N_DEV, axis=1)
    land = hooks.small_land
    conv_off = sum(math.prod(w[n].shape) for n in _SMALL)
    land_conv = land.reshape(N_DEV, -1)[:, conv_off:conv_off + CONV_W * LRU_W].reshape(N_DEV, CONV_W, LRU_W)
    land_own = jnp.pad(own_cols(land_conv.reshape(N_DEV * CONV_W, LRU_W)).reshape(N_DEV, -1),
                       ((0, 0), (0, CONV_W * LRU_W - CONV_W * LRU_W // N_DEV)))
    land = lax.dynamic_update_slice(land.reshape(N_DEV, -1), land_own, (0, conv_off)).reshape(land.shape)
    pk = lambda d: _pack_small({n: d[n] for n in _SMALL}, conv_w_full=jnp.pad(d["conv_w"].reshape(-1), (0, CONV_W * LRU_W - CONV_W * LRU_W // N_DEV)))
    g_s, d_s, m_s, v_s = _adam_small(pk(w), land, pk(mom), pk(var), name="adam_small")
    for src, dst in ((g_s, grads), (d_s, delta), (m_s, new_m), (v_s, new_v)):
        vals, conv = _unpack_small(src, w, with_conv=True)
        dst.update(vals)
        dst["conv_w"] = conv.reshape(-1)[:CONV_W * LRU_W // N_DEV].reshape(w["conv_w"].shape)
    loss = g_s.reshape(-1)[conv_off + CONV_W * LRU_W]

    return (loss, grad_x[None], *[grads[n] for n in _WEIGHTS], *[delta[n] for n in _WEIGHTS],
            *[new_m[n] for n in _WEIGHTS], *[new_v[n] for n in _WEIGHTS])
```
